```python
import jax, jax.numpy as jnp
from jax import lax
import numpy as np

D_MODEL = 1024
BATCH = 8
SEQ = 8192
DEPTH = 2

N_META = 16
D_MIX = D_MODEL
D_POOL = D_MIX // 4
POOL_WINDOWS = (2, 4, 8, 16)
POOL_GROUPS = len(POOL_WINDOWS)
POOL_GW = D_POOL // POOL_GROUPS
D_CONV = D_MIX // 4
CONV_K = 31
D_RNN = D_MIX - D_POOL - D_CONV
RG_HEADS = 8
RG_HD = D_RNN // RG_HEADS
RG_CONV_K = 4
RG_C = 8.0
D_IN = D_POOL + 2 * D_CONV + 2 * D_RNN
D_FF = 4 * D_MODEL
EPS = 1e-6

kernel_name = "hymba_pool_conformer_rglru_hybrid"


def _rmsnorm(x, g):
    xf = x.astype(jnp.float32)
    y = xf * lax.rsqrt(jnp.mean(xf * xf, axis=-1, keepdims=True) + EPS)
    return y.astype(x.dtype) * g


def _layernorm(x, g, b):
    xf = x.astype(jnp.float32)
    mu = jnp.mean(xf, axis=-1, keepdims=True)
    var = jnp.mean(jnp.square(xf - mu), axis=-1, keepdims=True)
    return ((xf - mu) * lax.rsqrt(var + EPS)).astype(x.dtype) * g + b


def _causal_depthwise_conv(x, w, b):
    K, C = w.shape
    y = lax.conv_general_dilated(
        x, w[:, None, :], window_strides=(1,), padding=[(K - 1, 0)],
        dimension_numbers=('NWC', 'WIO', 'NWC'), feature_group_count=C)
    return y + b


def _pool_mixer(u, w_grp, scale):
    B_, T, _ = u.shape
    uf = u.astype(jnp.float32)
    cs = jnp.cumsum(uf, axis=1)
    pos = jnp.arange(1, T + 1, dtype=jnp.int32)
    outs = []
    for g, w in enumerate(POOL_WINDOWS):
        sl = slice(g * POOL_GW, (g + 1) * POOL_GW)
        c = cs[..., sl]
        prev = jnp.pad(c, ((0, 0), (w, 0), (0, 0)))[:, :T]
        cnt = jnp.minimum(pos, w).astype(jnp.float32)[None, :, None]
        outs.append((c - prev) / cnt - uf[..., sl])
    pooled = jnp.stack(outs, axis=2).astype(u.dtype)
    mixed = jnp.einsum('btgc,gcd->btgd', pooled, w_grp).reshape(B_, T, D_POOL)
    return mixed * scale


def _conformer_conv(v, gt, w_dw, b_dw, ln_g, ln_b, w_pw):
    u = v * jax.nn.sigmoid(gt)
    u = _causal_depthwise_conv(u, w_dw, b_dw)
    u = jax.nn.silu(_layernorm(u, ln_g, ln_b))
    return u @ w_pw


def _linear_scan(a, b):
    def comb(l, r):
        return (l[0] * r[0], r[0] * l[1] + r[1])
    _, h = lax.associative_scan(comb, (a, b), axis=1)
    return h


def _rglru_branch(gate_in, x_in, conv_w, conv_b, w_a, b_a, w_x, b_x, lam):
    B_, T, _ = x_in.shape
    f32 = jnp.float32
    xc = _causal_depthwise_conv(x_in, conv_w, conv_b)
    xh = xc.reshape(B_, T, RG_HEADS, RG_HD)
    r = jax.nn.sigmoid((jnp.einsum('bthc,hcd->bthd', xh, w_a).reshape(B_, T, D_RNN) + b_a).astype(f32))
    i = jax.nn.sigmoid((jnp.einsum('bthc,hcd->bthd', xh, w_x).reshape(B_, T, D_RNN) + b_x).astype(f32))
    log_a = -RG_C * r * jax.nn.softplus(-lam.astype(f32))
    a = jnp.exp(log_a)
    mult = jnp.sqrt(-jnp.expm1(2.0 * log_a))
    h = _linear_scan(a, mult * (i * xc.astype(f32)))
    return (jax.nn.gelu(gate_in.astype(f32)) * h).astype(x_in.dtype)


def _fwd_setup_inputs(seed: int = 0) -> dict:
    key = jax.random.key(seed)
    ks = jax.random.split(key, 24)
    f32 = jnp.float32
    L = DEPTH

    def nrm(k, shape, scale):
        return jax.random.normal(k, shape, f32) * scale

    u = jax.random.uniform(ks[15], (L, D_RNN), f32, 0.9, 0.999)
    a0 = u ** (1.0 / RG_C)
    rg_lambda = jnp.log(a0) - jnp.log1p(-a0)
    return {
        "x": nrm(ks[0], (BATCH, SEQ, D_MODEL), 1.0),
        "meta_tokens": nrm(ks[1], (N_META, D_MODEL), 1.0),
        "mix_norm_g": 1.0 + nrm(ks[2], (L, D_MODEL), 0.05),
        "w_in": nrm(ks[3], (L, D_MODEL, D_IN), D_MODEL ** -0.5),
        "pool_w": nrm(ks[4], (L, POOL_GROUPS, POOL_GW, POOL_GW), POOL_GW ** -0.5),
        "pool_scale": 1.0 + nrm(ks[5], (L, D_POOL), 0.1),
        "convb_dw_w": nrm(ks[6], (L, CONV_K, D_CONV), CONV_K ** -0.5),
        "convb_dw_b": nrm(ks[7], (L, D_CONV), 0.02),
        "convb_ln_g": 1.0 + nrm(ks[8], (L, D_CONV), 0.05),
        "convb_ln_b": nrm(ks[9], (L, D_CONV), 0.02),
        "convb_pw_w": nrm(ks[10], (L, D_CONV, D_CONV), D_CONV ** -0.5),
        "rg_conv_w": nrm(ks[11], (L, RG_CONV_K, D_RNN), RG_CONV_K ** -0.5),
        "rg_conv_b": nrm(ks[12], (L, D_RNN), 0.02),
        "rg_w_a": nrm(ks[13], (L, RG_HEADS, RG_HD, RG_HD), RG_HD ** -0.5),
        "rg_b_a": nrm(ks[14], (L, D_RNN), 0.02),
        "rg_w_x": nrm(ks[16], (L, RG_HEADS, RG_HD, RG_HD), RG_HD ** -0.5),
        "rg_b_x": nrm(ks[17], (L, D_RNN), 0.02),
        "rg_lambda": rg_lambda,
        "w_out": nrm(ks[18], (L, D_MIX, D_MODEL), D_MIX ** -0.5),
        "mlp_norm_g": 1.0 + nrm(ks[19], (L, D_MODEL), 0.05),
        "w_up": nrm(ks[20], (L, D_MODEL, D_FF), D_MODEL ** -0.5),
        "w_down": nrm(ks[21], (L, D_FF, D_MODEL), D_FF ** -0.5),
        "final_norm_g": 1.0 + nrm(ks[22], (D_MODEL,), 0.05),
    }


def _fwd_reference(x, meta_tokens, mix_norm_g, w_in, pool_w, pool_scale, convb_dw_w, convb_dw_b,
              convb_ln_g, convb_ln_b, convb_pw_w, rg_conv_w, rg_conv_b, rg_w_a, rg_b_a,
              rg_w_x, rg_b_x, rg_lambda, w_out, mlp_norm_g, w_up, w_down, final_norm_g):
    B_ = x.shape[0]
    meta = jnp.broadcast_to(meta_tokens[None].astype(x.dtype), (B_, N_META, D_MODEL))
    h = jnp.concatenate([meta, x], axis=1)
    splits = [D_POOL, D_POOL + D_CONV, D_POOL + 2 * D_CONV, D_POOL + 2 * D_CONV + D_RNN]
    for l in range(DEPTH):
        u = _rmsnorm(h, mix_norm_g[l])
        p = u @ w_in[l]
        p_pool, p_bval, p_bgate, p_cgate, p_cx = jnp.split(p, splits, axis=-1)
        y_a = _pool_mixer(p_pool, pool_w[l], pool_scale[l])
        y_b = _conformer_conv(p_bval, p_bgate, convb_dw_w[l], convb_dw_b[l],
                              convb_ln_g[l], convb_ln_b[l], convb_pw_w[l])
        y_c = _rglru_branch(p_cgate, p_cx, rg_conv_w[l], rg_conv_b[l], rg_w_a[l], rg_b_a[l],
                            rg_w_x[l], rg_b_x[l], rg_lambda[l])
        y = jnp.concatenate([y_a, y_b, y_c], axis=-1)
        h = h + y @ w_out[l]
        u = _rmsnorm(h, mlp_norm_g[l])
        h = h + jnp.square(jax.nn.relu(u @ w_up[l])) @ w_down[l]
    h = _rmsnorm(h, final_norm_g)
    return h[:, N_META:]


import jax as _jax
import jax.numpy as _jnp

TWIN_FORMAT = 'train_step'
FWD_PARAMS = ['x', 'meta_tokens', 'mix_norm_g', 'w_in', 'pool_w', 'pool_scale', 'convb_dw_w', 'convb_dw_b', 'convb_ln_g', 'convb_ln_b', 'convb_pw_w', 'rg_conv_w', 'rg_conv_b', 'rg_w_a', 'rg_b_a', 'rg_w_x', 'rg_b_x', 'rg_lambda', 'w_out', 'mlp_norm_g', 'w_up', 'w_down', 'final_norm_g']
TWIN_WEIGHTS = ['meta_tokens', 'mix_norm_g', 'w_in', 'pool_w', 'pool_scale', 'convb_dw_w', 'convb_dw_b', 'convb_ln_g', 'convb_ln_b', 'convb_pw_w', 'rg_conv_w', 'rg_conv_b', 'rg_w_a', 'rg_b_a', 'rg_w_x', 'rg_b_x', 'rg_lambda', 'w_out', 'mlp_norm_g', 'w_up', 'w_down', 'final_norm_g']
TWIN_DIFF_INPUT = 'x'
TWIN_INPUTS = ['x', 'meta_tokens', 'mix_norm_g', 'w_in', 'pool_w', 'pool_scale', 'convb_dw_w', 'convb_dw_b', 'convb_ln_g', 'convb_ln_b', 'convb_pw_w', 'rg_conv_w', 'rg_conv_b', 'rg_w_a', 'rg_b_a', 'rg_w_x', 'rg_b_x', 'rg_lambda', 'w_out', 'mlp_norm_g', 'w_up', 'w_down', 'final_norm_g', 'loss_target', 'm_meta_tokens', 'm_mix_norm_g', 'm_w_in', 'm_pool_w', 'm_pool_scale', 'm_convb_dw_w', 'm_convb_dw_b', 'm_convb_ln_g', 'm_convb_ln_b', 'm_convb_pw_w', 'm_rg_conv_w', 'm_rg_conv_b', 'm_rg_w_a', 'm_rg_b_a', 'm_rg_w_x', 'm_rg_b_x', 'm_rg_lambda', 'm_w_out', 'm_mlp_norm_g', 'm_w_up', 'm_w_down', 'm_final_norm_g', 'v_meta_tokens', 'v_mix_norm_g', 'v_w_in', 'v_pool_w', 'v_pool_scale', 'v_convb_dw_w', 'v_convb_dw_b', 'v_convb_ln_g', 'v_convb_ln_b', 'v_convb_pw_w', 'v_rg_conv_w', 'v_rg_conv_b', 'v_rg_w_a', 'v_rg_b_a', 'v_rg_w_x', 'v_rg_b_x', 'v_rg_lambda', 'v_w_out', 'v_mlp_norm_g', 'v_w_up', 'v_w_down', 'v_final_norm_g']
TWIN_OUTPUTS = ['loss', 'grad_x', 'grad_meta_tokens', 'grad_mix_norm_g', 'grad_w_in', 'grad_pool_w', 'grad_pool_scale', 'grad_convb_dw_w', 'grad_convb_dw_b', 'grad_convb_ln_g', 'grad_convb_ln_b', 'grad_convb_pw_w', 'grad_rg_conv_w', 'grad_rg_conv_b', 'grad_rg_w_a', 'grad_rg_b_a', 'grad_rg_w_x', 'grad_rg_b_x', 'grad_rg_lambda', 'grad_w_out', 'grad_mlp_norm_g', 'grad_w_up', 'grad_w_down', 'grad_final_norm_g', 'delta_meta_tokens', 'delta_mix_norm_g', 'delta_w_in', 'delta_pool_w', 'delta_pool_scale', 'delta_convb_dw_w', 'delta_convb_dw_b', 'delta_convb_ln_g', 'delta_convb_ln_b', 'delta_convb_pw_w', 'delta_rg_conv_w', 'delta_rg_conv_b', 'delta_rg_w_a', 'delta_rg_b_a', 'delta_rg_w_x', 'delta_rg_b_x', 'delta_rg_lambda', 'delta_w_out', 'delta_mlp_norm_g', 'delta_w_up', 'delta_w_down', 'delta_final_norm_g', 'new_m_meta_tokens', 'new_m_mix_norm_g', 'new_m_w_in', 'new_m_pool_w', 'new_m_pool_scale', 'new_m_convb_dw_w', 'new_m_convb_dw_b', 'new_m_convb_ln_g', 'new_m_convb_ln_b', 'new_m_convb_pw_w', 'new_m_rg_conv_w', 'new_m_rg_conv_b', 'new_m_rg_w_a', 'new_m_rg_b_a', 'new_m_rg_w_x', 'new_m_rg_b_x', 'new_m_rg_lambda', 'new_m_w_out', 'new_m_mlp_norm_g', 'new_m_w_up', 'new_m_w_down', 'new_m_final_norm_g', 'new_v_meta_tokens', 'new_v_mix_norm_g', 'new_v_w_in', 'new_v_pool_w', 'new_v_pool_scale', 'new_v_convb_dw_w', 'new_v_convb_dw_b', 'new_v_convb_ln_g', 'new_v_convb_ln_b', 'new_v_convb_pw_w', 'new_v_rg_conv_w', 'new_v_rg_conv_b', 'new_v_rg_w_a', 'new_v_rg_b_a', 'new_v_rg_w_x', 'new_v_rg_b_x', 'new_v_rg_lambda', 'new_v_w_out', 'new_v_mlp_norm_g', 'new_v_w_up', 'new_v_w_down', 'new_v_final_norm_g']
TWIN_LEAF_KINDS = {'loss': 'loss', 'grad_x': 'grad_x', 'grad_meta_tokens': 'grad_w', 'grad_mix_norm_g': 'grad_w', 'grad_w_in': 'grad_w', 'grad_pool_w': 'grad_w', 'grad_pool_scale': 'grad_w', 'grad_convb_dw_w': 'grad_w', 'grad_convb_dw_b': 'grad_w', 'grad_convb_ln_g': 'grad_w', 'grad_convb_ln_b': 'grad_w', 'grad_convb_pw_w': 'grad_w', 'grad_rg_conv_w': 'grad_w', 'grad_rg_conv_b': 'grad_w', 'grad_rg_w_a': 'grad_w', 'grad_rg_b_a': 'grad_w', 'grad_rg_w_x': 'grad_w', 'grad_rg_b_x': 'grad_w', 'grad_rg_lambda': 'grad_w', 'grad_w_out': 'grad_w', 'grad_mlp_norm_g': 'grad_w', 'grad_w_up': 'grad_w', 'grad_w_down': 'grad_w', 'grad_final_norm_g': 'grad_w', 'delta_meta_tokens': 'delta_w', 'delta_mix_norm_g': 'delta_w', 'delta_w_in': 'delta_w', 'delta_pool_w': 'delta_w', 'delta_pool_scale': 'delta_w', 'delta_convb_dw_w': 'delta_w', 'delta_convb_dw_b': 'delta_w', 'delta_convb_ln_g': 'delta_w', 'delta_convb_ln_b': 'delta_w', 'delta_convb_pw_w': 'delta_w', 'delta_rg_conv_w': 'delta_w', 'delta_rg_conv_b': 'delta_w', 'delta_rg_w_a': 'delta_w', 'delta_rg_b_a': 'delta_w', 'delta_rg_w_x': 'delta_w', 'delta_rg_b_x': 'delta_w', 'delta_rg_lambda': 'delta_w', 'delta_w_out': 'delta_w', 'delta_mlp_norm_g': 'delta_w', 'delta_w_up': 'delta_w', 'delta_w_down': 'delta_w', 'delta_final_norm_g': 'delta_w', 'new_m_meta_tokens': 'new_m', 'new_m_mix_norm_g': 'new_m', 'new_m_w_in': 'new_m', 'new_m_pool_w': 'new_m', 'new_m_pool_scale': 'new_m', 'new_m_convb_dw_w': 'new_m', 'new_m_convb_dw_b': 'new_m', 'new_m_convb_ln_g': 'new_m', 'new_m_convb_ln_b': 'new_m', 'new_m_convb_pw_w': 'new_m', 'new_m_rg_conv_w': 'new_m', 'new_m_rg_conv_b': 'new_m', 'new_m_rg_w_a': 'new_m', 'new_m_rg_b_a': 'new_m', 'new_m_rg_w_x': 'new_m', 'new_m_rg_b_x': 'new_m', 'new_m_rg_lambda': 'new_m', 'new_m_w_out': 'new_m', 'new_m_mlp_norm_g': 'new_m', 'new_m_w_up': 'new_m', 'new_m_w_down': 'new_m', 'new_m_final_norm_g': 'new_m', 'new_v_meta_tokens': 'new_v', 'new_v_mix_norm_g': 'new_v', 'new_v_w_in': 'new_v', 'new_v_pool_w': 'new_v', 'new_v_pool_scale': 'new_v', 'new_v_convb_dw_w': 'new_v', 'new_v_convb_dw_b': 'new_v', 'new_v_convb_ln_g': 'new_v', 'new_v_convb_ln_b': 'new_v', 'new_v_convb_pw_w': 'new_v', 'new_v_rg_conv_w': 'new_v', 'new_v_rg_conv_b': 'new_v', 'new_v_rg_w_a': 'new_v', 'new_v_rg_b_a': 'new_v', 'new_v_rg_w_x': 'new_v', 'new_v_rg_b_x': 'new_v', 'new_v_rg_lambda': 'new_v', 'new_v_w_out': 'new_v', 'new_v_mlp_norm_g': 'new_v', 'new_v_w_up': 'new_v', 'new_v_w_down': 'new_v', 'new_v_final_norm_g': 'new_v'}


def _forward(args):
    return _fwd_reference(*[args[k] for k in FWD_PARAMS])


def _output_shape():
    def fwd():
        inp = _fwd_setup_inputs(0)
        return _fwd_reference(*[inp[k] for k in FWD_PARAMS])
    out = _jax.eval_shape(fwd)
    return out.shape, out.dtype

N_MICROBATCH = 1
ADAM_LR = 0.001
ADAM_B1 = 0.9
ADAM_B2 = 0.999
ADAM_EPS = 1e-08
ADAM_WD = 0.01
ADAM_STEP = 10
PER_EXAMPLE_BATCH_AXIS = {'x': 0, 'loss_target': 0}
SHARED_INPUTS = []
_WEIGHT_DTYPES = {'meta_tokens': _jnp.float32, 'mix_norm_g': _jnp.float32, 'w_in': _jnp.float32, 'pool_w': _jnp.float32, 'pool_scale': _jnp.float32, 'convb_dw_w': _jnp.float32, 'convb_dw_b': _jnp.float32, 'convb_ln_g': _jnp.float32, 'convb_ln_b': _jnp.float32, 'convb_pw_w': _jnp.float32, 'rg_conv_w': _jnp.float32, 'rg_conv_b': _jnp.float32, 'rg_w_a': _jnp.float32, 'rg_b_a': _jnp.float32, 'rg_w_x': _jnp.float32, 'rg_b_x': _jnp.float32, 'rg_lambda': _jnp.float32, 'w_out': _jnp.float32, 'mlp_norm_g': _jnp.float32, 'w_up': _jnp.float32, 'w_down': _jnp.float32, 'final_norm_g': _jnp.float32}
MOMENT_SCALE = {'meta_tokens': 7.455984e-03, 'mix_norm_g': 3.343802e-01, 'w_in': 2.416897e-01, 'pool_w': 1.809161e-01, 'pool_scale': 1.877363e-01, 'convb_dw_w': 1.489797e-01, 'convb_dw_b': 1.035846e+00, 'convb_ln_g': 3.712421e-01, 'convb_ln_b': 5.672111e-01, 'convb_pw_w': 2.295517e-01, 'rg_conv_w': 3.919452e-01, 'rg_conv_b': 2.363794e+00, 'rg_w_a': 8.501143e-02, 'rg_b_a': 1.059000e-01, 'rg_w_x': 1.718455e-01, 'rg_b_x': 1.745495e-01, 'rg_lambda': 2.561793e-01, 'w_out': 2.621295e-01, 'mlp_norm_g': 2.318382e-01, 'w_up': 1.172264e-01, 'w_down': 5.107277e-01, 'final_norm_g': 6.519035e+01}


def _to_microbatches(a, axis):
    t = _jnp.moveaxis(a, axis, 0)
    t = t.reshape((N_MICROBATCH, t.shape[0] // N_MICROBATCH) + t.shape[1:])
    return _jnp.moveaxis(t, 1, axis + 1)


def setup_inputs(seed: int = 0) -> dict:
    inp = _fwd_setup_inputs(seed)
    key = _jax.random.fold_in(_jax.random.key(seed), 7919)
    shape, _ = _output_shape()
    out = dict(inp)
    out["loss_target"] = _jax.random.normal(_jax.random.fold_in(key, 0), shape, _jnp.float32)
    for i, name in enumerate(TWIN_WEIGHTS):
        w = inp[name].astype(_jnp.float32)
        if MOMENT_SCALE is None:
            s = _jnp.sqrt(_jnp.mean(_jnp.square(w)) + 1e-30)
        else:
            s = MOMENT_SCALE[name]
        km, kv = _jax.random.split(_jax.random.fold_in(key, i + 1))
        out[name] = w
        out["m_" + name] = s * _jax.random.normal(km, w.shape, _jnp.float32)
        out["v_" + name] = (s * s) * _jax.random.uniform(kv, w.shape, _jnp.float32, 0.5, 1.5)
    if N_MICROBATCH > 1:
        for name, axis in PER_EXAMPLE_BATCH_AXIS.items():
            out[name] = _to_microbatches(out[name], axis)
    return {'x': out['x'], 'meta_tokens': out['meta_tokens'], 'mix_norm_g': out['mix_norm_g'], 'w_in': out['w_in'], 'pool_w': out['pool_w'], 'pool_scale': out['pool_scale'], 'convb_dw_w': out['convb_dw_w'], 'convb_dw_b': out['convb_dw_b'], 'convb_ln_g': out['convb_ln_g'], 'convb_ln_b': out['convb_ln_b'], 'convb_pw_w': out['convb_pw_w'], 'rg_conv_w': out['rg_conv_w'], 'rg_conv_b': out['rg_conv_b'], 'rg_w_a': out['rg_w_a'], 'rg_b_a': out['rg_b_a'], 'rg_w_x': out['rg_w_x'], 'rg_b_x': out['rg_b_x'], 'rg_lambda': out['rg_lambda'], 'w_out': out['w_out'], 'mlp_norm_g': out['mlp_norm_g'], 'w_up': out['w_up'], 'w_down': out['w_down'], 'final_norm_g': out['final_norm_g'], 'loss_target': out['loss_target'], 'm_meta_tokens': out['m_meta_tokens'], 'm_mix_norm_g': out['m_mix_norm_g'], 'm_w_in': out['m_w_in'], 'm_pool_w': out['m_pool_w'], 'm_pool_scale': out['m_pool_scale'], 'm_convb_dw_w': out['m_convb_dw_w'], 'm_convb_dw_b': out['m_convb_dw_b'], 'm_convb_ln_g': out['m_convb_ln_g'], 'm_convb_ln_b': out['m_convb_ln_b'], 'm_convb_pw_w': out['m_convb_pw_w'], 'm_rg_conv_w': out['m_rg_conv_w'], 'm_rg_conv_b': out['m_rg_conv_b'], 'm_rg_w_a': out['m_rg_w_a'], 'm_rg_b_a': out['m_rg_b_a'], 'm_rg_w_x': out['m_rg_w_x'], 'm_rg_b_x': out['m_rg_b_x'], 'm_rg_lambda': out['m_rg_lambda'], 'm_w_out': out['m_w_out'], 'm_mlp_norm_g': out['m_mlp_norm_g'], 'm_w_up': out['m_w_up'], 'm_w_down': out['m_w_down'], 'm_final_norm_g': out['m_final_norm_g'], 'v_meta_tokens': out['v_meta_tokens'], 'v_mix_norm_g': out['v_mix_norm_g'], 'v_w_in': out['v_w_in'], 'v_pool_w': out['v_pool_w'], 'v_pool_scale': out['v_pool_scale'], 'v_convb_dw_w': out['v_convb_dw_w'], 'v_convb_dw_b': out['v_convb_dw_b'], 'v_convb_ln_g': out['v_convb_ln_g'], 'v_convb_ln_b': out['v_convb_ln_b'], 'v_convb_pw_w': out['v_convb_pw_w'], 'v_rg_conv_w': out['v_rg_conv_w'], 'v_rg_conv_b': out['v_rg_conv_b'], 'v_rg_w_a': out['v_rg_w_a'], 'v_rg_b_a': out['v_rg_b_a'], 'v_rg_w_x': out['v_rg_w_x'], 'v_rg_b_x': out['v_rg_b_x'], 'v_rg_lambda': out['v_rg_lambda'], 'v_w_out': out['v_w_out'], 'v_mlp_norm_g': out['v_mlp_norm_g'], 'v_w_up': out['v_w_up'], 'v_w_down': out['v_w_down'], 'v_final_norm_g': out['v_final_norm_g']}


def _loss(weights, diff, rest, loss_target):
    with _jax.named_scope("forward"):
        args = {**rest, TWIN_DIFF_INPUT: diff, **{k: w.astype(_WEIGHT_DTYPES[k]) for k, w in weights.items()}}
        y = _forward(args)
    with _jax.named_scope("loss_head"):
        err = _jnp.square(y.astype(_jnp.float32) - loss_target)
        return 0.5 * _jnp.sum(_jnp.mean(err, axis=-1)) if err.ndim else 0.5 * err


def _adamw(w, g, m, v):
    m = ADAM_B1 * m + (1.0 - ADAM_B1) * g
    v = ADAM_B2 * v + (1.0 - ADAM_B2) * _jnp.square(g)
    m_hat = m / (1.0 - ADAM_B1 ** ADAM_STEP)
    v_hat = v / (1.0 - ADAM_B2 ** ADAM_STEP)
    delta = -ADAM_LR * (m_hat / (_jnp.sqrt(v_hat) + ADAM_EPS) + ADAM_WD * w)
    return delta, m, v


def reference(x, meta_tokens, mix_norm_g, w_in, pool_w, pool_scale, convb_dw_w, convb_dw_b, convb_ln_g, convb_ln_b, convb_pw_w, rg_conv_w, rg_conv_b, rg_w_a, rg_b_a, rg_w_x, rg_b_x, rg_lambda, w_out, mlp_norm_g, w_up, w_down, final_norm_g, loss_target, m_meta_tokens, m_mix_norm_g, m_w_in, m_pool_w, m_pool_scale, m_convb_dw_w, m_convb_dw_b, m_convb_ln_g, m_convb_ln_b, m_convb_pw_w, m_rg_conv_w, m_rg_conv_b, m_rg_w_a, m_rg_b_a, m_rg_w_x, m_rg_b_x, m_rg_lambda, m_w_out, m_mlp_norm_g, m_w_up, m_w_down, m_final_norm_g, v_meta_tokens, v_mix_norm_g, v_w_in, v_pool_w, v_pool_scale, v_convb_dw_w, v_convb_dw_b, v_convb_ln_g, v_convb_ln_b, v_convb_pw_w, v_rg_conv_w, v_rg_conv_b, v_rg_w_a, v_rg_b_a, v_rg_w_x, v_rg_b_x, v_rg_lambda, v_w_out, v_mlp_norm_g, v_w_up, v_w_down, v_final_norm_g):
    given = dict(x=x, meta_tokens=meta_tokens, mix_norm_g=mix_norm_g, w_in=w_in, pool_w=pool_w, pool_scale=pool_scale, convb_dw_w=convb_dw_w, convb_dw_b=convb_dw_b, convb_ln_g=convb_ln_g, convb_ln_b=convb_ln_b, convb_pw_w=convb_pw_w, rg_conv_w=rg_conv_w, rg_conv_b=rg_conv_b, rg_w_a=rg_w_a, rg_b_a=rg_b_a, rg_w_x=rg_w_x, rg_b_x=rg_b_x, rg_lambda=rg_lambda, w_out=w_out, mlp_norm_g=mlp_norm_g, w_up=w_up, w_down=w_down, final_norm_g=final_norm_g, loss_target=loss_target, m_meta_tokens=m_meta_tokens, m_mix_norm_g=m_mix_norm_g, m_w_in=m_w_in, m_pool_w=m_pool_w, m_pool_scale=m_pool_scale, m_convb_dw_w=m_convb_dw_w, m_convb_dw_b=m_convb_dw_b, m_convb_ln_g=m_convb_ln_g, m_convb_ln_b=m_convb_ln_b, m_convb_pw_w=m_convb_pw_w, m_rg_conv_w=m_rg_conv_w, m_rg_conv_b=m_rg_conv_b, m_rg_w_a=m_rg_w_a, m_rg_b_a=m_rg_b_a, m_rg_w_x=m_rg_w_x, m_rg_b_x=m_rg_b_x, m_rg_lambda=m_rg_lambda, m_w_out=m_w_out, m_mlp_norm_g=m_mlp_norm_g, m_w_up=m_w_up, m_w_down=m_w_down, m_final_norm_g=m_final_norm_g, v_meta_tokens=v_meta_tokens, v_mix_norm_g=v_mix_norm_g, v_w_in=v_w_in, v_pool_w=v_pool_w, v_pool_scale=v_pool_scale, v_convb_dw_w=v_convb_dw_w, v_convb_dw_b=v_convb_dw_b, v_convb_ln_g=v_convb_ln_g, v_convb_ln_b=v_convb_ln_b, v_convb_pw_w=v_convb_pw_w, v_rg_conv_w=v_rg_conv_w, v_rg_conv_b=v_rg_conv_b, v_rg_w_a=v_rg_w_a, v_rg_b_a=v_rg_b_a, v_rg_w_x=v_rg_w_x, v_rg_b_x=v_rg_b_x, v_rg_lambda=v_rg_lambda, v_w_out=v_w_out, v_mlp_norm_g=v_mlp_norm_g, v_w_up=v_w_up, v_w_down=v_w_down, v_final_norm_g=v_final_norm_g)
    weights = {n: given[n] for n in TWIN_WEIGHTS}
    shared = {n: given[n] for n in SHARED_INPUTS}
    per_example = {n: given[n] for n in ['x']}
    grad_fn = _jax.value_and_grad(_loss, argnums=(0, 1))

    def one_microbatch(ex, loss_target):
        ex = dict(ex)
        diff = ex.pop(TWIN_DIFF_INPUT)
        return grad_fn(weights, diff, {**shared, **ex}, loss_target)

    if N_MICROBATCH == 1:
        loss, (grad_w, grad_x) = one_microbatch(per_example, given["loss_target"])
    else:
        def body(carry, xs):
            loss_sum, grad_sum = carry
            l_k, (gw_k, gx_k) = one_microbatch(xs[0], xs[1])
            with _jax.named_scope("update"):
                return (loss_sum + l_k, _jax.tree.map(_jnp.add, grad_sum, gw_k)), gx_k

        init = (_jnp.zeros((), _jnp.float32), _jax.tree.map(_jnp.zeros_like, weights))
        (loss, grad_w), grad_x = _jax.lax.scan(body, init, (per_example, given["loss_target"]))
    with _jax.named_scope("update"):
        delta_w, new_m, new_v = {}, {}, {}
        for n in TWIN_WEIGHTS:
            delta_w[n], new_m[n], new_v[n] = _adamw(weights[n], grad_w[n], given["m_" + n], given["v_" + n])
    return (loss, grad_x, *[grad_w[n] for n in TWIN_WEIGHTS], *[delta_w[n] for n in TWIN_WEIGHTS],
            *[new_m[n] for n in TWIN_WEIGHTS], *[new_v[n] for n in TWIN_WEIGHTS])
```

```python
import functools

import jax
import jax.numpy as jnp
from jax import lax
from jax.experimental import pallas as pl
from jax.experimental.pallas import tpu as pltpu

F32, BF16 = jnp.float32, jnp.bfloat16
MESH = pl.DeviceIdType.MESH
ANY = pl.BlockSpec(memory_space=pl.ANY)

D_MODEL = 1024
N_META = 16
D_POOL = 256
D_CONV = 256
D_RNN = 512
D_IN = D_POOL + 2 * D_CONV + 2 * D_RNN
D_FF = 4096
FF_CHUNK = 1024
POOL_GW = 64
CONV_K = 31
RG_CONV_K = 4
RG_HD = 64
RG_C = 8.0
EPS = 1e-6
ADAM_LR, ADAM_B1, ADAM_B2, ADAM_EPS, ADAM_WD, ADAM_STEP = 0.001, 0.9, 0.999, 1e-08, 0.01, 10

HALO = 32
ROW_ALIGN = 256
TM_MIX = 256
TM_MAT = 768
N_CHIPS = 4
VMEM_LIMIT = 56 * 1024 * 1024

BIG = ("w_in", "w_out", "w_up", "w_down")
SMALL_SHARDED = ("meta_tokens", "convb_dw_w", "convb_pw_w", "rg_conv_w")
SMALL_REPL = ("mix_norm_g", "pool_w", "pool_scale", "convb_dw_b", "convb_ln_g", "convb_ln_b", "rg_conv_b",
              "rg_w_a", "rg_b_a", "rg_w_x", "rg_b_x", "rg_lambda", "mlp_norm_g", "final_norm_g")
WEIGHTS = ("meta_tokens", "mix_norm_g", "w_in", "pool_w", "pool_scale", "convb_dw_w", "convb_dw_b", "convb_ln_g",
           "convb_ln_b", "convb_pw_w", "rg_conv_w", "rg_conv_b", "rg_w_a", "rg_b_a", "rg_w_x", "rg_b_x",
           "rg_lambda", "w_out", "mlp_norm_g", "w_up", "w_down", "final_norm_g")


def _params(*sem):
    return pltpu.CompilerParams(dimension_semantics=sem, vmem_limit_bytes=VMEM_LIMIT)


def _row_tile(t, cap):
    best = None
    for tm in range(128, cap + 1, 128):
        if t % tm == 0:
            best = tm
    assert best is not None, (t, cap)
    return best


def _dot(a, b):
    return jnp.dot(a, b, preferred_element_type=F32)


def _dot_nt(a, b):
    return lax.dot_general(a, b, (((1,), (1,)), ((), ())), preferred_element_type=F32)


def _dot_tn(a, b):
    return lax.dot_general(a, b, (((0,), (0,)), ((), ())), preferred_element_type=F32)


def _rms(x):
    r = lax.rsqrt(jnp.mean(x * x, axis=-1, keepdims=True) + EPS)
    return r, x * r


def _rms_bwd(du, n, r, g):
    dn = du * g
    return r * (dn - n * jnp.mean(dn * n, axis=-1, keepdims=True))


def _sig(x):
    return jax.nn.sigmoid(x)


def _colsum(x):
    return jnp.sum(x, axis=0, keepdims=True)


def _neg_expm1(x):
    series = -x * (1.0 + x * (0.5 + x * (1.0 / 6 + x * (1.0 / 24 + x * (1.0 / 120)))))
    return jnp.where(x > -0.05, series, 1.0 - jnp.exp(x))


_GELU_K0 = 0.7978845608028654
_GELU_K1 = 0.044715


def _gelu_and_grad(x):
    th = jnp.tanh(_GELU_K0 * (x + _GELU_K1 * x * x * x))
    val = 0.5 * x * (1.0 + th)
    grad = 0.5 * (1.0 + th) + 0.5 * x * (1.0 - th * th) * _GELU_K0 * (1.0 + 3.0 * _GELU_K1 * x * x)
    return val, grad


def _full(a):
    nd = a.ndim
    return pl.BlockSpec(a.shape, lambda *_: (0,) * nd)


def _lane_sel(lane, a2, a4, a8, a16):
    return jnp.where(lane < POOL_GW, a2, jnp.where(lane < 2 * POOL_GW, a4, jnp.where(lane < 3 * POOL_GW, a8, a16)))


def _window_sums(ref, base, tm, step):
    sh = lambda j: ref[pl.ds(base + step * j, tm), :]
    s2 = sh(0) + sh(1)
    s4 = s2 + sh(2) + sh(3)
    s8 = s4 + sh(4) + sh(5) + sh(6) + sh(7)
    s16 = s8 + sh(8) + sh(9) + sh(10) + sh(11) + sh(12) + sh(13) + sh(14) + sh(15)
    return s2, s4, s8, s16


def _pool_counts(tm, t0):
    lane = lax.broadcasted_iota(jnp.int32, (tm, D_POOL), 1)
    row = lax.broadcasted_iota(jnp.int32, (tm, D_POOL), 0) + t0
    cnt = jnp.minimum(row + 1, _lane_sel(lane, 2, 4, 8, 16)).astype(F32)
    return lane, cnt


def _pool_fwd(ext_q, tm, t0):
    lane, cnt = _pool_counts(tm, t0)
    q = ext_q[pl.ds(HALO, tm), :]
    pooled = _lane_sel(lane, *_window_sums(ext_q, HALO, tm, -1)) / cnt - q
    return pooled, lane, cnt


def _conv_ln_fwd(ext_u, dww_ref, dwb, lng, lnb, tm):
    c = dwb + dww_ref[0:1, :] * ext_u[pl.ds(HALO - (CONV_K - 1), tm), :]
    for k in range(1, CONV_K):
        c = c + dww_ref[k:k + 1, :] * ext_u[pl.ds(HALO - (CONV_K - 1) + k, tm), :]
    mu = jnp.mean(c, axis=-1, keepdims=True)
    cc = c - mu
    rstd = lax.rsqrt(jnp.mean(cc * cc, axis=-1, keepdims=True) + EPS)
    z = cc * rstd
    l = z * lng + lnb
    sl = _sig(l)
    return z, rstd, l, sl, l * sl


def _rg_fwd(ext_x, cw_ref, cb, wa, ba, wx, bx, lam, tm):
    xc = cb + cw_ref[0:1, :] * ext_x[pl.ds(HALO - (RG_CONV_K - 1), tm), :]
    for k in range(1, RG_CONV_K):
        xc = xc + cw_ref[k:k + 1, :] * ext_x[pl.ds(HALO - (RG_CONV_K - 1) + k, tm), :]
    xcb = xc.astype(BF16)
    r = _sig(_dot(xcb, wa) + ba)
    ig = _sig(_dot(xcb, wx) + bx)
    sp = jnp.maximum(-lam, 0.0) + jnp.log(1.0 + jnp.exp(-jnp.abs(lam)))
    log_a = (-RG_C * r) * sp
    a = jnp.exp(log_a)
    m = jnp.sqrt(_neg_expm1(2.0 * log_a))
    return xc, xcb, r, ig, sp, a, m


def _scan_rows(a_ref, b_ref, out_ref, carry, tm, reverse):
    rows = lax.broadcasted_iota(jnp.int32, (8, D_RNN), 0)
    ngrp = tm // 8

    def grp(gi, hb):
        st = pl.multiple_of((ngrp - 1 - gi if reverse else gi) * 8, 8)
        a8 = a_ref[pl.ds(st, 8), :]
        b8 = b_ref[pl.ds(st, 8), :]
        out = jnp.zeros((8, D_RNN), F32)
        for j in (range(7, -1, -1) if reverse else range(8)):
            aj = jnp.broadcast_to(a8[j:j + 1, :], (8, D_RNN))
            bj = jnp.broadcast_to(b8[j:j + 1, :], (8, D_RNN))
            if reverse:
                cur = bj + hb
                hb = aj * cur
            else:
                cur = aj * hb + bj
                hb = cur
            out = jnp.where(rows == j, cur, out)
        out_ref[pl.ds(st, 8), :] = out
        return hb

    carry[...] = lax.fori_loop(0, ngrp, grp, carry[...])


_MIX_W = ("wp", "psc", "dww", "dwb", "lng", "lnb", "wpw", "cw", "cb", "wa", "ba", "wx", "bx", "lam")


def _mixer_fwd(p, mw):
    t = p.shape[0]
    tm = _row_tile(t, TM_MIX)

    def body(p_ref, wp, psc, dww, dwb, lng, lnb, wpw, cw, cb, wa, ba, wx, bx, lam, y_ref, hs_ref,
             ext_q, ext_u, ext_x, a_s, b_s, hcar):
        i = pl.program_id(0)

        @pl.when(i == 0)
        def _():
            ext_q[0:HALO, :] = jnp.zeros((HALO, D_POOL), F32)
            ext_u[0:HALO, :] = jnp.zeros((HALO, D_CONV), F32)
            ext_x[0:HALO, :] = jnp.zeros((HALO, D_RNN), F32)
            hcar[...] = jnp.zeros((8, D_RNN), F32)

        ext_q[pl.ds(HALO, tm), :] = p_ref[:, 0:256]
        pooled, _, _ = _pool_fwd(ext_q, tm, i * tm)
        y_ref[:, 0:256] = (_dot(pooled.astype(BF16), wp[...]) * psc[...]).astype(BF16)

        ext_u[pl.ds(HALO, tm), :] = p_ref[:, 256:512] * _sig(p_ref[:, 512:768])
        act = _conv_ln_fwd(ext_u, dww, dwb[...], lng[...], lnb[...], tm)[4]
        y_ref[:, 256:512] = _dot(act.astype(BF16), wpw[...]).astype(BF16)

        ext_x[pl.ds(HALO, tm), :] = p_ref[:, 1280:1792]
        xc, _, _, ig, _, a, m = _rg_fwd(ext_x, cw, cb[...], wa[...], ba[...], wx[...], bx[...], lam[...], tm)
        a_s[...] = a
        b_s[...] = m * (ig * xc)
        _scan_rows(a_s, b_s, hs_ref, hcar, tm, reverse=False)
        y_ref[:, 512:1024] = (_gelu_and_grad(p_ref[:, 768:1280])[0] * hs_ref[...]).astype(BF16)

        ext_q[0:HALO, :] = ext_q[pl.ds(tm, HALO), :]
        ext_u[0:HALO, :] = ext_u[pl.ds(tm, HALO), :]
        ext_x[0:HALO, :] = ext_x[pl.ds(tm, HALO), :]

    ws = [mw[k] for k in _MIX_W]
    return pl.pallas_call(
        body, name="mixer_fwd", grid=(t // tm,),
        in_specs=[pl.BlockSpec((tm, D_IN), lambda i: (i, 0))] + [_full(w) for w in ws],
        out_specs=[pl.BlockSpec((tm, D_MODEL), lambda i: (i, 0)), pl.BlockSpec((tm, D_RNN), lambda i: (i, 0))],
        out_shape=[jax.ShapeDtypeStruct((t, D_MODEL), BF16), jax.ShapeDtypeStruct((t, D_RNN), F32)],
        scratch_shapes=[pltpu.VMEM((HALO + tm, D_POOL), F32), pltpu.VMEM((HALO + tm, D_CONV), F32),
                        pltpu.VMEM((HALO + tm, D_RNN), F32), pltpu.VMEM((tm, D_RNN), F32),
                        pltpu.VMEM((tm, D_RNN), F32), pltpu.VMEM((8, D_RNN), F32)],
        compiler_params=_params("arbitrary"),
    )(p, *ws)


_MIX_G = (("wp", (D_POOL, D_POOL)), ("psc", (1, D_POOL)), ("dww", (32, D_CONV)), ("dwb", (1, D_CONV)),
          ("lng", (1, D_CONV)), ("lnb", (1, D_CONV)), ("wpw", (D_CONV, D_CONV)), ("cw", (8, D_RNN)),
          ("cb", (1, D_RNN)), ("wa", (D_RNN, D_RNN)), ("ba", (1, D_RNN)), ("wx", (D_RNN, D_RNN)),
          ("bx", (1, D_RNN)), ("lam", (1, D_RNN)))


def _mixer_bwd(p, dy, hs, mw):
    t = p.shape[0]
    tm = _row_tile(t, TM_MIX)
    nt = t // tm
    hb = tm // HALO

    def body(p_ref, ph_ref, dy_ref, hs_ref, hsh_ref, wp, psc, dww, dwb, lng, lnb, wpw, cw, cb, wa, ba, wx, bx, lam,
             dp_ref, g_wp, g_psc, g_dww, g_dwb, g_lng, g_lnb, g_wpw, g_cw, g_cb, g_wa, g_ba, g_wx, g_bx, g_lam,
             ext_q, ext_u, ext_x, ext_h, ee, dc_s, dx_s, a_s, b_s, g_s, gcar):
        step = pl.program_id(0)
        i = nt - 1 - step
        grads = (g_wp, g_psc, g_dww, g_dwb, g_lng, g_lnb, g_wpw, g_cw, g_cb, g_wa, g_ba, g_wx, g_bx, g_lam)

        @pl.when(step == 0)
        def _():
            for gr in grads:
                gr[...] = jnp.zeros(gr.shape, F32)
            ee[pl.ds(tm, HALO), :] = jnp.zeros((HALO, D_POOL), F32)
            dc_s[pl.ds(tm, HALO), :] = jnp.zeros((HALO, D_CONV), F32)
            dx_s[pl.ds(tm, HALO), :] = jnp.zeros((HALO, D_RNN), F32)
            gcar[...] = jnp.zeros((8, D_RNN), F32)

        hm = jnp.where(i == 0, 0.0, 1.0)

        ext_q[0:HALO, :] = ph_ref[:, 0:256] * hm
        ext_q[pl.ds(HALO, tm), :] = p_ref[:, 0:256]
        pooled, lane, cnt = _pool_fwd(ext_q, tm, i * tm)
        pooled_b = pooled.astype(BF16)
        dya = dy_ref[:, 0:256]
        g_psc[...] += _colsum(dya * _dot(pooled_b, wp[...]))
        dmixed_b = (dya * psc[...]).astype(BF16)
        dpooled = _dot_nt(dmixed_b, wp[...])
        g_wp[...] += _dot_tn(pooled_b, dmixed_b)
        ee[0:tm, :] = dpooled / cnt
        dp_ref[:, 0:256] = _lane_sel(lane, *_window_sums(ee, 0, tm, 1)) - dpooled
        ee[pl.ds(tm, HALO), :] = ee[0:HALO, :]

        v = p_ref[:, 256:512]
        s = _sig(p_ref[:, 512:768])
        ext_u[0:HALO, :] = ph_ref[:, 256:512] * _sig(ph_ref[:, 512:768]) * hm
        ext_u[pl.ds(HALO, tm), :] = v * s
        z, rstd, l, sl, act = _conv_ln_fwd(ext_u, dww, dwb[...], lng[...], lnb[...], tm)
        dyb_b = dy_ref[:, 256:512].astype(BF16)
        dact = _dot_nt(dyb_b, wpw[...])
        g_wpw[...] += _dot_tn(act.astype(BF16), dyb_b)
        dl = dact * (sl * (1.0 + l * (1.0 - sl)))
        g_lng[...] += _colsum(dl * z)
        g_lnb[...] += _colsum(dl)
        dz = dl * lng[...]
        dc = rstd * (dz - jnp.mean(dz, axis=-1, keepdims=True) - z * jnp.mean(dz * z, axis=-1, keepdims=True))
        g_dwb[...] += _colsum(dc)
        dc_s[0:tm, :] = dc
        for k in range(CONV_K):
            g_dww[k:k + 1, :] += _colsum(dc * ext_u[pl.ds(HALO - (CONV_K - 1) + k, tm), :])
        du0 = dww[CONV_K - 1:CONV_K, :] * dc
        for j in range(1, CONV_K):
            du0 = du0 + dww[CONV_K - 1 - j:CONV_K - j, :] * dc_s[pl.ds(j, tm), :]
        dp_ref[:, 256:512] = du0 * s
        dp_ref[:, 512:768] = du0 * v * (s * (1.0 - s))
        dc_s[pl.ds(tm, HALO), :] = dc_s[0:HALO, :]

        ext_x[0:HALO, :] = ph_ref[:, 1280:1792] * hm
        ext_x[pl.ds(HALO, tm), :] = p_ref[:, 1280:1792]
        xc, xcb, r, ig, sp, a, m = _rg_fwd(ext_x, cw, cb[...], wa[...], ba[...], wx[...], bx[...], lam[...], tm)
        ext_h[0:HALO, :] = hsh_ref[...] * hm
        ext_h[pl.ds(HALO, tm), :] = hs_ref[...]
        dyc = dy_ref[:, 512:1024]
        gl, dgl = _gelu_and_grad(p_ref[:, 768:1280])
        dp_ref[:, 768:1280] = dyc * hs_ref[...] * dgl
        a_s[...] = a
        b_s[...] = dyc * gl
        _scan_rows(a_s, b_s, g_s, gcar, tm, reverse=True)
        g = g_s[...]
        da = g * ext_h[pl.ds(HALO - 1, tm), :]
        dm = g * (ig * xc)
        dig = g * (m * xc)
        dlog_a = da * a - dm * (a * a) / m
        g_lam[...] += _colsum(dlog_a * (-RG_C * r)) * (-_sig(-lam[...]))
        dra = (dlog_a * (-RG_C * sp)) * (r * (1.0 - r))
        dia = dig * (ig * (1.0 - ig))
        g_ba[...] += _colsum(dra)
        g_bx[...] += _colsum(dia)
        dra_b = dra.astype(BF16)
        dia_b = dia.astype(BF16)
        dxc = g * (m * ig) + _dot_nt(dra_b, wa[...]) + _dot_nt(dia_b, wx[...])
        g_wa[...] += _dot_tn(xcb, dra_b)
        g_wx[...] += _dot_tn(xcb, dia_b)
        g_cb[...] += _colsum(dxc)
        dx_s[0:tm, :] = dxc
        for k in range(RG_CONV_K):
            g_cw[k:k + 1, :] += _colsum(dxc * ext_x[pl.ds(HALO - (RG_CONV_K - 1) + k, tm), :])
        dxin = cw[RG_CONV_K - 1:RG_CONV_K, :] * dxc
        for j in range(1, RG_CONV_K):
            dxin = dxin + cw[RG_CONV_K - 1 - j:RG_CONV_K - j, :] * dx_s[pl.ds(j, tm), :]
        dp_ref[:, 1280:1792] = dxin
        dx_s[pl.ds(tm, HALO), :] = dx_s[0:HALO, :]

    ws = [mw[k] for k in _MIX_W]
    tile = lambda w: pl.BlockSpec((tm, w), lambda s: (nt - 1 - s, 0))
    halo = lambda w: pl.BlockSpec((HALO, w), lambda s: (jnp.maximum((nt - 1 - s) * hb - 1, 0), 0))
    outs = pl.pallas_call(
        body, name="mixer_bwd", grid=(nt,),
        in_specs=[tile(D_IN), halo(D_IN), tile(D_MODEL), tile(D_RNN), halo(D_RNN)] + [_full(w) for w in ws],
        out_specs=[tile(D_IN)] + [pl.BlockSpec(shp, lambda s: (0, 0)) for _, shp in _MIX_G],
        out_shape=[jax.ShapeDtypeStruct((t, D_IN), F32)] + [jax.ShapeDtypeStruct(shp, F32) for _, shp in _MIX_G],
        scratch_shapes=[pltpu.VMEM((HALO + tm, D_POOL), F32), pltpu.VMEM((HALO + tm, D_CONV), F32),
                        pltpu.VMEM((HALO + tm, D_RNN), F32), pltpu.VMEM((HALO + tm, D_RNN), F32),
                        pltpu.VMEM((tm + HALO, D_POOL), F32), pltpu.VMEM((tm + HALO, D_CONV), F32),
                        pltpu.VMEM((tm + HALO, D_RNN), F32), pltpu.VMEM((tm, D_RNN), F32),
                        pltpu.VMEM((tm, D_RNN), F32), pltpu.VMEM((tm, D_RNN), F32), pltpu.VMEM((8, D_RNN), F32)],
        compiler_params=_params("arbitrary"),
    )(p, p, dy, hs, hs, *ws)
    return outs[0], {k: o for (k, _), o in zip(_MIX_G, outs[1:])}


def _in_proj(h, g, w):
    t = h.shape[0]
    tm = _row_tile(t, TM_MAT)

    def body(h_ref, g_ref, w_ref, p_ref, u_ref):
        u = (_rms(h_ref[...])[1] * g_ref[...]).astype(BF16)
        u_ref[...] = u
        p_ref[...] = _dot(u, w_ref[...])

    return pl.pallas_call(
        body, name="in_proj", grid=(t // tm,),
        in_specs=[pl.BlockSpec((tm, D_MODEL), lambda i: (i, 0)), _full(g), _full(w)],
        out_specs=[pl.BlockSpec((tm, D_IN), lambda i: (i, 0)), pl.BlockSpec((tm, D_MODEL), lambda i: (i, 0))],
        out_shape=[jax.ShapeDtypeStruct((t, D_IN), F32), jax.ShapeDtypeStruct((t, D_MODEL), BF16)],
        compiler_params=_params("parallel"),
    )(h, g, w)


def _mid_fwd(y, h0, w_out, g, w_up):
    t = h0.shape[0]
    tm = _row_tile(t, TM_MAT)
    nj = D_FF // FF_CHUNK

    def body(y_ref, h0_ref, wo_ref, g_ref, wu_ref, h1_ref, u2_ref, f_ref):
        @pl.when(pl.program_id(1) == 0)
        def _():
            h1 = h0_ref[...] + _dot(y_ref[...], wo_ref[...])
            h1_ref[...] = h1
            u2_ref[...] = (_rms(h1)[1] * g_ref[...]).astype(BF16)

        f_ref[...] = _dot(u2_ref[...], wu_ref[...]).astype(BF16)

    row = lambda w: pl.BlockSpec((tm, w), lambda i, j: (i, 0))
    return pl.pallas_call(
        body, name="mid_fwd", grid=(t // tm, nj),
        in_specs=[row(D_MODEL), row(D_MODEL), _full(w_out), _full(g),
                  pl.BlockSpec((None, D_MODEL, FF_CHUNK), lambda i, j: (j, 0, 0))],
        out_specs=[row(D_MODEL), row(D_MODEL), pl.BlockSpec((tm, FF_CHUNK), lambda i, j: (i, j))],
        out_shape=[jax.ShapeDtypeStruct((t, D_MODEL), F32), jax.ShapeDtypeStruct((t, D_MODEL), BF16),
                   jax.ShapeDtypeStruct((t, D_FF), BF16)],
        compiler_params=_params("parallel", "arbitrary"),
    )(y, h0, w_out, g, w_up)


def _down_fwd(f, h1, w_down):
    t = h1.shape[0]
    tm = _row_tile(t, TM_MAT)

    def body(f_ref, h1_ref, wd_ref, h2_ref):
        acc = h1_ref[...]
        for c in range(D_FF // FF_CHUNK):
            cols = slice(c * FF_CHUNK, (c + 1) * FF_CHUNK)
            a = jnp.square(jnp.maximum(f_ref[:, cols].astype(F32), 0.0)).astype(BF16)
            acc = acc + _dot(a, wd_ref[cols, :])
        h2_ref[...] = acc

    return pl.pallas_call(
        body, name="down_fwd", grid=(t // tm,),
        in_specs=[pl.BlockSpec((tm, D_FF), lambda i: (i, 0)), pl.BlockSpec((tm, D_MODEL), lambda i: (i, 0)), _full(w_down)],
        out_specs=pl.BlockSpec((tm, D_MODEL), lambda i: (i, 0)),
        out_shape=jax.ShapeDtypeStruct((t, D_MODEL), F32),
        compiler_params=_params("parallel"),
    )(f, h1, w_down)


def _loss_head(h, g, tgt, t_real):
    t = h.shape[0]
    tm = _row_tile(t, TM_MAT)

    def body(h_ref, g_ref, tgt_ref, loss_ref, dh_ref, dg_ref):
        i = pl.program_id(0)

        @pl.when(i == 0)
        def _():
            loss_ref[...] = jnp.zeros(loss_ref.shape, F32)
            dg_ref[...] = jnp.zeros(dg_ref.shape, F32)

        r, n = _rms(h_ref[...])
        row = lax.broadcasted_iota(jnp.int32, (tm, 1), 0) + i * tm
        valid = jnp.logical_and(row >= N_META, row < t_real)
        diff = jnp.where(valid, n * g_ref[...] - tgt_ref[...], 0.0)
        loss_ref[...] += 0.5 * jnp.sum(jnp.mean(diff * diff, axis=-1, keepdims=True))
        dy = diff * (1.0 / D_MODEL)
        dg_ref[...] += _colsum(dy * n)
        dh_ref[...] = _rms_bwd(dy, n, r, g_ref[...])

    return pl.pallas_call(
        body, name="loss_head", grid=(t // tm,),
        in_specs=[pl.BlockSpec((tm, D_MODEL), lambda i: (i, 0)), _full(g), pl.BlockSpec((tm, D_MODEL), lambda i: (i, 0))],
        out_specs=[pl.BlockSpec((8, 128), lambda i: (0, 0)), pl.BlockSpec((tm, D_MODEL), lambda i: (i, 0)),
                   pl.BlockSpec((1, D_MODEL), lambda i: (0, 0))],
        out_shape=[jax.ShapeDtypeStruct((8, 128), F32), jax.ShapeDtypeStruct((t, D_MODEL), F32),
                   jax.ShapeDtypeStruct((1, D_MODEL), F32)],
        compiler_params=_params("arbitrary"),
    )(h, g, tgt)


def _mlp_bwd(dh2, f, h1, g, w_up, w_down):
    t = dh2.shape[0]
    tm = _row_tile(t, TM_MAT)
    nj = D_FF // FF_CHUNK

    def body(dh2_ref, f_ref, wd_ref, wu_ref, h1_ref, g_ref, df_ref, dh1_ref, dg_ref, acc, dhb):
        i, j = pl.program_id(0), pl.program_id(1)

        @pl.when(j == 0)
        def _():
            dhb[...] = dh2_ref[...].astype(BF16)
            acc[...] = jnp.zeros(acc.shape, F32)

        @pl.when(jnp.logical_and(i == 0, j == 0))
        def _():
            dg_ref[...] = jnp.zeros(dg_ref.shape, F32)

        dact = _dot_nt(dhb[...], wd_ref[...])
        df = (dact * (2.0 * jnp.maximum(f_ref[...].astype(F32), 0.0))).astype(BF16)
        df_ref[...] = df
        acc[...] += _dot_nt(df, wu_ref[...])

        @pl.when(j == nj - 1)
        def _():
            r, n = _rms(h1_ref[...])
            du2 = acc[...]
            dg_ref[...] += _colsum(du2 * n)
            dh1_ref[...] = dh2_ref[...] + _rms_bwd(du2, n, r, g_ref[...])

    row = lambda w: pl.BlockSpec((tm, w), lambda i, j: (i, 0))
    return pl.pallas_call(
        body, name="mlp_bwd", grid=(t // tm, nj),
        in_specs=[row(D_MODEL), pl.BlockSpec((tm, FF_CHUNK), lambda i, j: (i, j)),
                  pl.BlockSpec((None, FF_CHUNK, D_MODEL), lambda i, j: (j, 0, 0)),
                  pl.BlockSpec((None, D_MODEL, FF_CHUNK), lambda i, j: (j, 0, 0)), row(D_MODEL), _full(g)],
        out_specs=[pl.BlockSpec((tm, FF_CHUNK), lambda i, j: (i, j)), row(D_MODEL),
                   pl.BlockSpec((1, D_MODEL), lambda i, j: (0, 0))],
        out_shape=[jax.ShapeDtypeStruct((t, D_FF), BF16), jax.ShapeDtypeStruct((t, D_MODEL), F32),
                   jax.ShapeDtypeStruct((1, D_MODEL), F32)],
        scratch_shapes=[pltpu.VMEM((tm, D_MODEL), F32), pltpu.VMEM((tm, D_MODEL), BF16)],
        compiler_params=_params("arbitrary", "arbitrary"),
    )(dh2, f, w_down, w_up, h1, g)


def _out_bwd(dh1, w_out):
    t = dh1.shape[0]
    tm = _row_tile(t, TM_MAT)

    def body(dh_ref, w_ref, dy_ref):
        dy_ref[...] = _dot_nt(dh_ref[...].astype(BF16), w_ref[...])

    return pl.pallas_call(
        body, name="out_bwd", grid=(t // tm,),
        in_specs=[pl.BlockSpec((tm, D_MODEL), lambda i: (i, 0)), _full(w_out)],
        out_specs=pl.BlockSpec((tm, D_MODEL), lambda i: (i, 0)),
        out_shape=jax.ShapeDtypeStruct((t, D_MODEL), F32),
        compiler_params=_params("parallel"),
    )(dh1, w_out)


def _in_bwd(dp, dh1, h0, g, w_in):
    t = dp.shape[0]
    tm = _row_tile(t, TM_MAT)

    def body(dp_ref, dh1_ref, h0_ref, g_ref, w_ref, dh0_ref, dg_ref):
        @pl.when(pl.program_id(0) == 0)
        def _():
            dg_ref[...] = jnp.zeros(dg_ref.shape, F32)

        du = _dot_nt(dp_ref[...].astype(BF16), w_ref[...])
        r, n = _rms(h0_ref[...])
        dg_ref[...] += _colsum(du * n)
        dh0_ref[...] = dh1_ref[...] + _rms_bwd(du, n, r, g_ref[...])

    row = lambda w: pl.BlockSpec((tm, w), lambda i: (i, 0))
    return pl.pallas_call(
        body, name="in_bwd", grid=(t // tm,),
        in_specs=[row(D_IN), row(D_MODEL), row(D_MODEL), _full(g), _full(w_in)],
        out_specs=[row(D_MODEL), pl.BlockSpec((1, D_MODEL), lambda i: (0, 0))],
        out_shape=[jax.ShapeDtypeStruct((t, D_MODEL), F32), jax.ShapeDtypeStruct((1, D_MODEL), F32)],
        compiler_params=_params("arbitrary"),
    )(dp, dh1, h0, g, w_in)


def _tn_matmul(a, b, kc, nc, relu2, name):
    t, k = a.shape
    n = b.shape[1]
    tt = _row_tile(t, TM_MAT)
    gk, gn = k // kc, n // nc

    def body(a_ref, b_ref, o_ref):
        @pl.when(pl.program_id(2) == 0)
        def _():
            o_ref[...] = jnp.zeros(o_ref.shape, F32)

        av = a_ref[...]
        if relu2:
            av = jnp.square(jnp.maximum(av.astype(F32), 0.0))
        o_ref[...] += _dot_tn(av.astype(BF16), b_ref[...].astype(BF16))

    return pl.pallas_call(
        body, name=name, grid=(gk, gn, t // tt),
        in_specs=[pl.BlockSpec((tt, kc), lambda ik, jn, it: (it, ik)), pl.BlockSpec((tt, nc), lambda ik, jn, it: (it, jn))],
        out_specs=pl.BlockSpec((None, kc, nc), lambda ik, jn, it: (ik * gn + jn, 0, 0)),
        out_shape=jax.ShapeDtypeStruct((gk * gn, kc, nc), F32),
        compiler_params=_params("parallel", "parallel", "arbitrary"),
    )(a, b)


def _block_diag(blocks):
    nb, hd, _ = blocks.shape
    eye = jnp.eye(nb, dtype=blocks.dtype)
    return (blocks[:, :, None, :] * eye[:, None, :, None]).reshape(nb * hd, nb * hd)


def _diag_blocks(m, nb):
    hd = m.shape[0] // nb
    return jnp.stack([m[b * hd:(b + 1) * hd, b * hd:(b + 1) * hd] for b in range(nb)])


def _mixer_weights(w, l):
    row = lambda a: a.reshape(1, -1)
    return dict(
        wp=_block_diag(w["pool_w"][l]).astype(BF16), psc=row(w["pool_scale"][l]),
        dww=jnp.pad(w["convb_dw_w"][l], ((0, 32 - CONV_K), (0, 0))), dwb=row(w["convb_dw_b"][l]),
        lng=row(w["convb_ln_g"][l]), lnb=row(w["convb_ln_b"][l]), wpw=w["convb_pw_w"][l].astype(BF16),
        cw=jnp.pad(w["rg_conv_w"][l], ((0, 8 - RG_CONV_K), (0, 0))), cb=row(w["rg_conv_b"][l]),
        wa=_block_diag(w["rg_w_a"][l]).astype(BF16), ba=row(w["rg_b_a"][l]),
        wx=_block_diag(w["rg_w_x"][l]).astype(BF16), bx=row(w["rg_b_x"][l]), lam=row(w["rg_lambda"][l]))


def _local_step(h, tgt, t_real, w, big):
    depth = len(big)
    saved = []
    for l in range(depth):
        mw = _mixer_weights(w, l)
        g1 = w["mix_norm_g"][l].reshape(1, -1)
        g2 = w["mlp_norm_g"][l].reshape(1, -1)
        p, u = _in_proj(h, g1, big[l]["w_in"])
        y, hs = _mixer_fwd(p, mw)
        h1, u2, f = _mid_fwd(y, h, big[l]["w_out"], g2, big[l]["w_up"])
        h2 = _down_fwd(f, h1, big[l]["w_down"].reshape(D_FF, D_MODEL))
        saved.append(dict(mw=mw, g1=g1, g2=g2, h0=h, p=p, u=u, y=y, hs=hs, h1=h1, u2=u2, f=f))
        h = h2
    gf = w["final_norm_g"].reshape(1, -1)
    loss, dh, dgf = _loss_head(h, gf, tgt, t_real)

    gbig = [None] * depth
    gs = {k: [None] * depth for k in ("mix_norm_g", "mlp_norm_g", "pool_w", "pool_scale", "convb_dw_w", "convb_dw_b",
                                      "convb_ln_g", "convb_ln_b", "convb_pw_w", "rg_conv_w", "rg_conv_b", "rg_w_a",
                                      "rg_b_a", "rg_w_x", "rg_b_x", "rg_lambda")}
    for l in reversed(range(depth)):
        s = saved[l]
        df, dh1, dg2 = _mlp_bwd(dh, s["f"], s["h1"], s["g2"], big[l]["w_up"], big[l]["w_down"])
        g_down = _tn_matmul(s["f"], dh, FF_CHUNK, D_MODEL, True, "dw_down")
        g_up = _tn_matmul(s["u2"], df, D_MODEL, FF_CHUNK, False, "dw_up")
        dy = _out_bwd(dh1, big[l]["w_out"])
        g_out = _tn_matmul(s["y"], dh1, D_MODEL, D_MODEL, False, "dw_out")
        dp, mg = _mixer_bwd(s["p"], dy, s["hs"], s["mw"])
        dh, dg1 = _in_bwd(dp, dh1, s["h0"], s["g1"], big[l]["w_in"])
        g_in = _tn_matmul(s["u"], dp, D_MODEL, D_IN, False, "dw_in")
        gbig[l] = dict(
            w_in=g_in[0].reshape(D_MODEL, N_CHIPS, D_IN // N_CHIPS).transpose(1, 0, 2),
            w_out=g_out.reshape(N_CHIPS, D_MODEL // N_CHIPS, D_MODEL), w_up=g_up, w_down=g_down)
        gs["mix_norm_g"][l] = dg1[0]
        gs["mlp_norm_g"][l] = dg2[0]
        gs["pool_w"][l] = _diag_blocks(mg["wp"], D_POOL // POOL_GW)
        gs["pool_scale"][l] = mg["psc"][0]
        gs["convb_dw_w"][l] = mg["dww"][:CONV_K]
        gs["convb_dw_b"][l] = mg["dwb"][0]
        gs["convb_ln_g"][l] = mg["lng"][0]
        gs["convb_ln_b"][l] = mg["lnb"][0]
        gs["convb_pw_w"][l] = mg["wpw"]
        gs["rg_conv_w"][l] = mg["cw"][:RG_CONV_K]
        gs["rg_conv_b"][l] = mg["cb"][0]
        gs["rg_w_a"][l] = _diag_blocks(mg["wa"], D_RNN // RG_HD)
        gs["rg_b_a"][l] = mg["ba"][0]
        gs["rg_w_x"][l] = _diag_blocks(mg["wx"], D_RNN // RG_HD)
        gs["rg_b_x"][l] = mg["bx"][0]
        gs["rg_lambda"][l] = mg["lam"][0]
    gsmall = {k: jnp.stack(v) for k, v in gs.items()}
    gsmall["final_norm_g"] = dgf[0]
    return loss[0, 0], dh, gbig, gsmall


def _place():
    return lax.axis_index("x"), lax.axis_index("y"), lax.axis_index("c")


def _other_chips(x, y):
    return [(1 - x, y), (x, 1 - y), (1 - x, 1 - y)]


def _gather_weights(shards, small):
    srcs = [shards[k] for k in BIG] + [small]
    items = [(a, l) for a in range(len(BIG)) for l in range(2)] + [(len(BIG), None)]
    out_shape = [jax.ShapeDtypeStruct((N_CHIPS,) + (srcs[a].shape[1:] if l is not None else srcs[a].shape), srcs[a].dtype)
                 for a, l in items]
    ns, ni = len(srcs), len(items)

    def body(*refs):
        src_refs, dst_refs = refs[:ns], refs[ns:ns + ni]
        send_sems, recv_sems, loc_sems = refs[ns + ni:]
        x, y, c = _place()
        me = 2 * x + y
        chips = _other_chips(x, y)
        local, remote = [], []
        for n, (a, l) in enumerate(items):
            src = src_refs[a] if l is None else src_refs[a].at[l]
            cp = pltpu.make_async_copy(src, dst_refs[n].at[me], loc_sems.at[n])
            cp.start()
            local.append(cp)
            for j, (px, py) in enumerate(chips):
                out = pltpu.make_async_remote_copy(src, dst_refs[n].at[me], send_sems.at[3 * n + j], recv_sems.at[3 * n + j],
                                                   device_id=(px, py, c), device_id_type=MESH)
                out.start()
                remote.append(pltpu.make_async_remote_copy(src, dst_refs[n].at[2 * px + py], send_sems.at[3 * n + j],
                                                           recv_sems.at[3 * n + j], device_id=(px, py, c), device_id_type=MESH))
        for cp in remote:
            cp.wait()
        for cp in local:
            cp.wait()

    return pl.pallas_call(
        body, name="gather_weights", in_specs=[ANY] * ns, out_specs=[ANY] * ni, out_shape=out_shape,
        scratch_shapes=[pltpu.SemaphoreType.DMA((3 * ni,)), pltpu.SemaphoreType.DMA((3 * ni,)), pltpu.SemaphoreType.DMA((ni,))],
    )(*srcs)


def _send_halves_to_sibling(gs):
    n = len(gs)
    out_shape = [jax.ShapeDtypeStruct((g.shape[0], g.shape[1] // 2, g.shape[2]), g.dtype) for g in gs]

    def body(*refs):
        g_refs, out_refs = refs[:n], refs[n:2 * n]
        send_sems, recv_sems = refs[2 * n:]
        x, y, c = _place()
        copies = []
        for i in range(n):
            r2 = g_refs[i].shape[1] // 2
            src = g_refs[i].at[:, pl.ds(pl.multiple_of((1 - c) * r2, 8), r2)]
            cp = pltpu.make_async_remote_copy(src, out_refs[i], send_sems.at[i], recv_sems.at[i],
                                              device_id=(x, y, 1 - c), device_id_type=MESH)
            cp.start()
            copies.append(cp)
        for cp in copies:
            cp.wait()

    return pl.pallas_call(
        body, name="rs_sibling", in_specs=[ANY] * n, out_specs=[ANY] * n, out_shape=out_shape,
        scratch_shapes=[pltpu.SemaphoreType.DMA((n,)), pltpu.SemaphoreType.DMA((n,))],
    )(*gs)


def _add_halves(g, recv, c1):
    nk, r, cd = g.shape
    r2 = r // 2

    def body(c_ref, g_ref, r_ref, pa_ref, pab_ref):
        s = g_ref[...] + r_ref[...]
        pa_ref[...] = s
        pab_ref[...] = s.astype(BF16)

    blk = pl.BlockSpec((None, r2, cd), lambda k, c_ref: (k, 0, 0))
    return pl.pallas_call(
        body, name="rs_add_halves",
        grid_spec=pltpu.PrefetchScalarGridSpec(
            num_scalar_prefetch=1, grid=(nk,),
            in_specs=[pl.BlockSpec((None, r2, cd), lambda k, c_ref: (k, c_ref[0], 0)), blk], out_specs=[blk, blk]),
        out_shape=[jax.ShapeDtypeStruct((nk, r2, cd), F32), jax.ShapeDtypeStruct((nk, r2, cd), BF16)],
        compiler_params=_params("parallel"),
    )(c1, g, recv)


def _send_partials_to_chips(pabs):
    n = len(pabs)
    out_shape = [jax.ShapeDtypeStruct((3,) + p.shape[1:], p.dtype) for p in pabs]

    def body(*refs):
        p_refs, out_refs = refs[:n], refs[n:2 * n]
        send_sems, recv_sems = refs[2 * n:]
        x, y, c = _place()
        copies = []
        for i in range(n):
            for j, (px, py) in enumerate(_other_chips(x, y)):
                cp = pltpu.make_async_remote_copy(p_refs[i].at[2 * px + py], out_refs[i].at[j], send_sems.at[3 * i + j],
                                                  recv_sems.at[3 * i + j], device_id=(px, py, c), device_id_type=MESH)
                cp.start()
                copies.append(cp)
        for cp in copies:
            cp.wait()

    return pl.pallas_call(
        body, name="rs_chips", in_specs=[ANY] * n, out_specs=[ANY] * n, out_shape=out_shape,
        scratch_shapes=[pltpu.SemaphoreType.DMA((3 * n,)), pltpu.SemaphoreType.DMA((3 * n,))],
    )(*pabs)


def _sum_partials(pa, recv, me1):
    nk, r2, cd = pa.shape

    def body(me_ref, pa_ref, r_ref, s_ref):
        s_ref[...] = ((pa_ref[...] + r_ref[0].astype(F32)) + r_ref[1].astype(F32)) + r_ref[2].astype(F32)

    return pl.pallas_call(
        body, name="rs_sum_partials",
        grid_spec=pltpu.PrefetchScalarGridSpec(
            num_scalar_prefetch=1, grid=(1,),
            in_specs=[pl.BlockSpec((None, r2, cd), lambda i, me_ref: (me_ref[0], 0, 0)),
                      pl.BlockSpec((3, r2, cd), lambda i, me_ref: (0, 0, 0))],
            out_specs=pl.BlockSpec((r2, cd), lambda i, me_ref: (0, 0))),
        out_shape=jax.ShapeDtypeStruct((r2, cd), F32),
        compiler_params=_params("arbitrary"),
    )(me1, pa, recv)


def _share_sums_with_sibling(sums):
    na = len(sums)
    flat = [s for per_layer in sums for s in per_layer]
    n = len(flat)
    out_shape = [jax.ShapeDtypeStruct((2, 2 * per_layer[0].shape[0], per_layer[0].shape[1]), F32) for per_layer in sums]

    def body(*refs):
        s_refs, out_refs = refs[:n], refs[n:n + na]
        send_sems, recv_sems, loc_sems = refs[n + na:]
        x, y, c = _place()
        copies = []
        for a in range(na):
            for l in range(2):
                i = 2 * a + l
                r2 = s_refs[i].shape[0]
                mine = out_refs[a].at[l, pl.ds(pl.multiple_of(c * r2, 8), r2)]
                cp = pltpu.make_async_copy(s_refs[i], mine, loc_sems.at[i])
                cp.start()
                copies.append(cp)
                cp = pltpu.make_async_remote_copy(s_refs[i], mine, send_sems.at[i], recv_sems.at[i],
                                                  device_id=(x, y, 1 - c), device_id_type=MESH)
                cp.start()
                copies.append(cp)
        for cp in copies:
            cp.wait()

    return pl.pallas_call(
        body, name="rs_share", in_specs=[ANY] * n, out_specs=[ANY] * na, out_shape=out_shape,
        scratch_shapes=[pltpu.SemaphoreType.DMA((n,)), pltpu.SemaphoreType.DMA((n,)), pltpu.SemaphoreType.DMA((n,))],
    )(*flat)


def _allreduce_small(v):
    def body(v_ref, out_ref, rbuf, send_sems, recv_sems):
        x, y, c = _place()
        out_ref[...] = v_ref[...]
        for s, peer in enumerate([(x, y, 1 - c), (1 - x, y, c), (x, 1 - y, c)]):
            cp = pltpu.make_async_remote_copy(out_ref, rbuf.at[s], send_sems.at[s], recv_sems.at[s],
                                              device_id=peer, device_id_type=MESH)
            cp.start()
            cp.wait()
            out_ref[...] = out_ref[...] + rbuf[s]

    vm = pl.BlockSpec(memory_space=pltpu.VMEM)
    return pl.pallas_call(
        body, name="allreduce_small", in_specs=[vm], out_specs=vm, out_shape=jax.ShapeDtypeStruct(v.shape, v.dtype),
        scratch_shapes=[pltpu.VMEM((3,) + v.shape, v.dtype), pltpu.SemaphoreType.DMA((3,)), pltpu.SemaphoreType.DMA((3,))],
        compiler_params=pltpu.CompilerParams(vmem_limit_bytes=VMEM_LIMIT),
    )(v)


def _adamw_math(w, g, m, v):
    m = ADAM_B1 * m + (1.0 - ADAM_B1) * g
    v = ADAM_B2 * v + (1.0 - ADAM_B2) * jnp.square(g)
    m_hat = m / (1.0 - ADAM_B1 ** ADAM_STEP)
    v_hat = v / (1.0 - ADAM_B2 ** ADAM_STEP)
    return -ADAM_LR * (m_hat / (jnp.sqrt(v_hat) + ADAM_EPS) + ADAM_WD * w), m, v


def _adamw_big(w, g, m, v):
    shape = w.shape
    cd = shape[-1]
    rows = w.size // cd
    tr = 512
    flat = lambda a: a.reshape(rows, cd)

    def body(w_ref, g_ref, m_ref, v_ref, d_ref, mo_ref, vo_ref):
        d_ref[...], mo_ref[...], vo_ref[...] = _adamw_math(w_ref[...], g_ref[...], m_ref[...], v_ref[...])

    blk = pl.BlockSpec((tr, cd), lambda i: (i, 0))
    outs = pl.pallas_call(
        body, name="adamw_big", grid=(rows // tr,), in_specs=[blk] * 4, out_specs=[blk] * 3,
        out_shape=[jax.ShapeDtypeStruct((rows, cd), F32)] * 3, compiler_params=_params("parallel"),
    )(flat(w), flat(g), flat(m), flat(v))
    return [o.reshape(shape) for o in outs]


def _adamw_small(ws, ms, vs, gpack, offsets, strides, me1):
    n = len(ws)

    def body(me_ref, *refs):
        w_refs, m_refs, v_refs = refs[:n], refs[n:2 * n], refs[2 * n:3 * n]
        g_ref = refs[3 * n]
        outs = refs[3 * n + 1:]
        for i in range(n):
            rows = w_refs[i].shape[0]
            if strides[i]:
                g = g_ref[pl.ds(pl.multiple_of(offsets[i] + me_ref[0] * strides[i], 8), rows), :]
            else:
                g = g_ref[offsets[i]:offsets[i] + rows, :]
            d, m, v = _adamw_math(w_refs[i][...], g, m_refs[i][...], v_refs[i][...])
            outs[4 * i][...] = g
            outs[4 * i + 1][...] = d
            outs[4 * i + 2][...] = m
            outs[4 * i + 3][...] = v

    full = lambda a: pl.BlockSpec(a.shape, lambda i, me_ref: (0, 0))
    ins = list(ws) + list(ms) + list(vs) + [gpack]
    out_arrs = [w for w in ws for _ in range(4)]
    outs = pl.pallas_call(
        body, name="adamw_small",
        grid_spec=pltpu.PrefetchScalarGridSpec(num_scalar_prefetch=1, grid=(1,), in_specs=[full(a) for a in ins],
                                               out_specs=[full(a) for a in out_arrs]),
        out_shape=[jax.ShapeDtypeStruct(a.shape, F32) for a in out_arrs],
        compiler_params=_params("arbitrary"),
    )(me1, *ins)
    return [outs[4 * i:4 * i + 4] for i in range(n)]


LANES = 128
SUBLANES = 8


def _rows_of(size):
    return -(-size // (LANES * SUBLANES)) * SUBLANES


def _as_rows(a, rows=None):
    flat = a.reshape(-1)
    rows = _rows_of(flat.size) if rows is None else rows
    return jnp.pad(flat, (0, rows * LANES - flat.size)).reshape(rows, LANES)


def _to_shard_major(name, full):
    if name == "meta_tokens":
        return full.reshape(N_META, N_CHIPS, -1).transpose(1, 0, 2)
    if name == "convb_pw_w":
        return full.reshape(2, N_CHIPS, -1, D_CONV).transpose(1, 0, 2, 3)
    return full.reshape(full.shape[0], full.shape[1], N_CHIPS, -1).transpose(2, 0, 1, 3)


def _from_shard_major(name, sm):
    if name == "meta_tokens":
        return sm.transpose(1, 0, 2).reshape(N_META, -1)
    if name == "convb_pw_w":
        return sm.transpose(1, 0, 2, 3).reshape(2, -1, D_CONV)
    return sm.transpose(1, 2, 0, 3).reshape(sm.shape[1], sm.shape[2], -1)


def kernel(x, meta_tokens, mix_norm_g, w_in, pool_w, pool_scale, convb_dw_w, convb_dw_b, convb_ln_g, convb_ln_b, convb_pw_w, rg_conv_w, rg_conv_b, rg_w_a, rg_b_a, rg_w_x, rg_b_x, rg_lambda, w_out, mlp_norm_g, w_up, w_down, final_norm_g, loss_target, m_meta_tokens, m_mix_norm_g, m_w_in, m_pool_w, m_pool_scale, m_convb_dw_w, m_convb_dw_b, m_convb_ln_g, m_convb_ln_b, m_convb_pw_w, m_rg_conv_w, m_rg_conv_b, m_rg_w_a, m_rg_b_a, m_rg_w_x, m_rg_b_x, m_rg_lambda, m_w_out, m_mlp_norm_g, m_w_up, m_w_down, m_final_norm_g, v_meta_tokens, v_mix_norm_g, v_w_in, v_pool_w, v_pool_scale, v_convb_dw_w, v_convb_dw_b, v_convb_ln_g, v_convb_ln_b, v_convb_pw_w, v_rg_conv_w, v_rg_conv_b, v_rg_w_a, v_rg_b_a, v_rg_w_x, v_rg_b_x, v_rg_lambda, v_w_out, v_mlp_norm_g, v_w_up, v_w_down, v_final_norm_g):
    given = dict(locals())
    w = {k: given[k] for k in WEIGHTS}
    mom = {k: given["m_" + k] for k in WEIGHTS}
    var = {k: given["v_" + k] for k in WEIGHTS}
    xi, yi, ci = _place()
    me1 = (2 * xi + yi).astype(jnp.int32).reshape(1)
    c1 = ci.astype(jnp.int32).reshape(1)

    small_rows = [_rows_of(w[k].size) for k in SMALL_SHARDED]
    small_pack = jnp.concatenate([_as_rows(w[k]) for k in SMALL_SHARDED])
    gathered = _gather_weights({k: w[k].astype(BF16) for k in BIG}, small_pack)
    big = []
    for l in range(2):
        g_in, g_out, g_up, g_down = (gathered[2 * a + l] for a in range(len(BIG)))
        big.append(dict(w_in=g_in.transpose(1, 0, 2).reshape(D_MODEL, D_IN), w_out=g_out.reshape(D_MODEL, D_MODEL),
                        w_up=g_up, w_down=g_down))
    wfull = dict(w)
    off = 0
    for k, rows in zip(SMALL_SHARDED, small_rows):
        sm = gathered[-1][:, off:off + rows].reshape(N_CHIPS, -1)[:, :w[k].size].reshape((N_CHIPS,) + w[k].shape)
        wfull[k] = _from_shard_major(k, sm)
        off += rows

    seq = x.shape[1]
    t_real = N_META + seq
    t_pad = -(-t_real // ROW_ALIGN) * ROW_ALIGN
    tail = jnp.zeros((t_pad - t_real, D_MODEL), F32)
    h = jnp.concatenate([wfull["meta_tokens"], x[0], tail])
    tgt = jnp.concatenate([jnp.zeros((N_META, D_MODEL), F32), loss_target[0], tail])
    loss, dh, gbig, gsmall = _local_step(h, tgt, t_real, wfull, big)
    grad_x = dh[N_META:t_real][None]
    gsmall["meta_tokens"] = dh[:N_META]

    parts = [gbig[l][k] for k in BIG for l in range(2)]
    recv_a = _send_halves_to_sibling(parts)
    added = [_add_halves(g, r, c1) for g, r in zip(parts, recv_a)]
    recv_b = _send_partials_to_chips([pab for _, pab in added])
    sums = [_sum_partials(pa, rb, me1) for (pa, _), rb in zip(added, recv_b)]
    grads_big = _share_sums_with_sibling([sums[2 * a:2 * a + 2] for a in range(len(BIG))])

    pieces, offsets, strides = [], {}, {}
    row = 0
    for k in SMALL_REPL:
        rows = _rows_of(w[k].size)
        pieces.append(_as_rows(gsmall[k], rows))
        offsets[k], strides[k] = row, 0
        row += rows
    for k in SMALL_SHARDED:
        rows = _rows_of(w[k].size)
        sm = _to_shard_major(k, gsmall[k]).reshape(N_CHIPS, -1)
        pieces.append(jnp.pad(sm, ((0, 0), (0, rows * LANES - sm.shape[1]))).reshape(N_CHIPS * rows, LANES))
        offsets[k], strides[k] = row, rows
        row += N_CHIPS * rows
    gpack = _allreduce_small(jnp.concatenate(pieces))

    out = {}
    for a, k in enumerate(BIG):
        d, m2, v2 = _adamw_big(w[k], grads_big[a], mom[k], var[k])
        out[k] = (grads_big[a], d, m2, v2)
    names = SMALL_REPL + SMALL_SHARDED
    as_rows = lambda a: a.reshape(-1, LANES) if a.size % (LANES * SUBLANES) == 0 or a.size < LANES * SUBLANES else _as_rows(a)
    res = _adamw_small([as_rows(w[k]) for k in names], [as_rows(mom[k]) for k in names], [as_rows(var[k]) for k in names],
                       gpack, [offsets[k] for k in names], [strides[k] for k in names], me1)
    for k, r4 in zip(names, res):
        out[k] = tuple(o.reshape(-1)[:w[k].size].reshape(w[k].shape) for o in r4)

    loss = lax.psum(loss, ("x", "y", "c"))
    return (loss, grad_x, *[out[k][0] for k in WEIGHTS], *[out[k][1] for k in WEIGHTS],
            *[out[k][2] for k in WEIGHTS], *[out[k][3] for k in WEIGHTS])
```

```python
import functools

import jax
import jax.numpy as jnp
from jax import lax
from jax.experimental import pallas as pl
from jax.experimental.pallas import tpu as pltpu

F32, BF16 = jnp.float32, jnp.bfloat16
MESH = pl.DeviceIdType.MESH
ANY = pl.BlockSpec(memory_space=pl.ANY)

D_MODEL = 1024
N_META = 16
D_POOL = 256
D_CONV = 256
D_RNN = 512
D_IN = D_POOL + 2 * D_CONV + 2 * D_RNN
D_FF = 4096
FF_CHUNK = 1024
POOL_GW = 64
CONV_K = 31
RG_CONV_K = 4
RG_HD = 64
RG_C = 8.0
EPS = 1e-6
ADAM_LR, ADAM_B1, ADAM_B2, ADAM_EPS, ADAM_WD, ADAM_STEP = 0.001, 0.9, 0.999, 1e-08, 0.01, 10

HALO = 32
ROW_ALIGN = 256
TM_MIX = 256
TM_MAT = 768
N_CHIPS = 4
VMEM_LIMIT = 56 * 1024 * 1024

BIG = ("w_in", "w_out", "w_up", "w_down")
SMALL_SHARDED = ("meta_tokens", "convb_dw_w", "convb_pw_w", "rg_conv_w")
SMALL_REPL = ("mix_norm_g", "pool_w", "pool_scale", "convb_dw_b", "convb_ln_g", "convb_ln_b", "rg_conv_b",
              "rg_w_a", "rg_b_a", "rg_w_x", "rg_b_x", "rg_lambda", "mlp_norm_g", "final_norm_g")
WEIGHTS = ("meta_tokens", "mix_norm_g", "w_in", "pool_w", "pool_scale", "convb_dw_w", "convb_dw_b", "convb_ln_g",
           "convb_ln_b", "convb_pw_w", "rg_conv_w", "rg_conv_b", "rg_w_a", "rg_b_a", "rg_w_x", "rg_b_x",
           "rg_lambda", "w_out", "mlp_norm_g", "w_up", "w_down", "final_norm_g")


def _params(*sem):
    return pltpu.CompilerParams(dimension_semantics=sem, vmem_limit_bytes=VMEM_LIMIT)


def _row_tile(t, cap):
    best = None
    for tm in range(128, cap + 1, 128):
        if t % tm == 0:
            best = tm
    assert best is not None, (t, cap)
    return best


def _dot(a, b):
    return jnp.dot(a, b, preferred_element_type=F32)


def _dot_nt(a, b):
    return lax.dot_general(a, b, (((1,), (1,)), ((), ())), preferred_element_type=F32)


def _dot_tn(a, b):
    return lax.dot_general(a, b, (((0,), (0,)), ((), ())), preferred_element_type=F32)


def _rms(x):
    r = lax.rsqrt(jnp.mean(x * x, axis=-1, keepdims=True) + EPS)
    return r, x * r


def _rms_bwd(du, n, r, g):
    dn = du * g
    return r * (dn - n * jnp.mean(dn * n, axis=-1, keepdims=True))


def _sig(x):
    return jax.nn.sigmoid(x)


def _colsum(x):
    return jnp.sum(x, axis=0, keepdims=True)


def _neg_expm1(x):
    series = -x * (1.0 + x * (0.5 + x * (1.0 / 6 + x * (1.0 / 24 + x * (1.0 / 120)))))
    return jnp.where(x > -0.05, series, 1.0 - jnp.exp(x))


_GELU_K0 = 0.7978845608028654
_GELU_K1 = 0.044715


def _gelu_and_grad(x):
    th = jnp.tanh(_GELU_K0 * (x + _GELU_K1 * x * x * x))
    val = 0.5 * x * (1.0 + th)
    grad = 0.5 * (1.0 + th) + 0.5 * x * (1.0 - th * th) * _GELU_K0 * (1.0 + 3.0 * _GELU_K1 * x * x)
    return val, grad


def _full(a):
    nd = a.ndim
    return pl.BlockSpec(a.shape, lambda *_: (0,) * nd)


def _lane_sel(lane, a2, a4, a8, a16):
    return jnp.where(lane < POOL_GW, a2, jnp.where(lane < 2 * POOL_GW, a4, jnp.where(lane < 3 * POOL_GW, a8, a16)))


def _window_sums(ref, base, tm, step):
    sh = lambda j: ref[pl.ds(base + step * j, tm), :]
    s2 = sh(0) + sh(1)
    s4 = s2 + sh(2) + sh(3)
    s8 = s4 + sh(4) + sh(5) + sh(6) + sh(7)
    s16 = s8 + sh(8) + sh(9) + sh(10) + sh(11) + sh(12) + sh(13) + sh(14) + sh(15)
    return s2, s4, s8, s16


def _pool_counts(tm, t0):
    lane = lax.broadcasted_iota(jnp.int32, (tm, D_POOL), 1)
    row = lax.broadcasted_iota(jnp.int32, (tm, D_POOL), 0) + t0
    cnt = jnp.minimum(row + 1, _lane_sel(lane, 2, 4, 8, 16)).astype(F32)
    return lane, cnt


def _pool_fwd(ext_q, tm, t0):
    lane, cnt = _pool_counts(tm, t0)
    q = ext_q[pl.ds(HALO, tm), :]
    pooled = _lane_sel(lane, *_window_sums(ext_q, HALO, tm, -1)) / cnt - q
    return pooled, lane, cnt


def _conv_ln_fwd(ext_u, dww_ref, dwb, lng, lnb, tm):
    c = dwb + dww_ref[0:1, :] * ext_u[pl.ds(HALO - (CONV_K - 1), tm), :]
    for k in range(1, CONV_K):
        c = c + dww_ref[k:k + 1, :] * ext_u[pl.ds(HALO - (CONV_K - 1) + k, tm), :]
    mu = jnp.mean(c, axis=-1, keepdims=True)
    cc = c - mu
    rstd = lax.rsqrt(jnp.mean(cc * cc, axis=-1, keepdims=True) + EPS)
    z = cc * rstd
    l = z * lng + lnb
    sl = _sig(l)
    return z, rstd, l, sl, l * sl


def _rg_fwd(ext_x, cw_ref, cb, wa, ba, wx, bx, lam, tm):
    xc = cb + cw_ref[0:1, :] * ext_x[pl.ds(HALO - (RG_CONV_K - 1), tm), :]
    for k in range(1, RG_CONV_K):
        xc = xc + cw_ref[k:k + 1, :] * ext_x[pl.ds(HALO - (RG_CONV_K - 1) + k, tm), :]
    xcb = xc.astype(BF16)
    r = _sig(_dot(xcb, wa) + ba)
    ig = _sig(_dot(xcb, wx) + bx)
    sp = jnp.maximum(-lam, 0.0) + jnp.log(1.0 + jnp.exp(-jnp.abs(lam)))
    log_a = (-RG_C * r) * sp
    a = jnp.exp(log_a)
    m = jnp.sqrt(_neg_expm1(2.0 * log_a))
    return xc, xcb, r, ig, sp, a, m


def _scan_rows(a_ref, b_ref, out_ref, carry, tm, reverse):
    rows = lax.broadcasted_iota(jnp.int32, (8, D_RNN), 0)
    ngrp = tm // 8

    def grp(gi, hb):
        st = pl.multiple_of((ngrp - 1 - gi if reverse else gi) * 8, 8)
        a8 = a_ref[pl.ds(st, 8), :]
        b8 = b_ref[pl.ds(st, 8), :]
        out = jnp.zeros((8, D_RNN), F32)
        for j in (range(7, -1, -1) if reverse else range(8)):
            aj = jnp.broadcast_to(a8[j:j + 1, :], (8, D_RNN))
            bj = jnp.broadcast_to(b8[j:j + 1, :], (8, D_RNN))
            if reverse:
                cur = bj + hb
                hb = aj * cur
            else:
                cur = aj * hb + bj
                hb = cur
            out = jnp.where(rows == j, cur, out)
        out_ref[pl.ds(st, 8), :] = out
        return hb

    carry[...] = lax.fori_loop(0, ngrp, grp, carry[...])


_MIX_W = ("wp", "psc", "dww", "dwb", "lng", "lnb", "wpw", "cw", "cb", "wa", "ba", "wx", "bx", "lam")


def _mixer_fwd(p, mw):
    t = p.shape[0]
    tm = _row_tile(t, TM_MIX)

    def body(p_ref, wp, psc, dww, dwb, lng, lnb, wpw, cw, cb, wa, ba, wx, bx, lam, y_ref, hs_ref,
             ext_q, ext_u, ext_x, a_s, b_s, hcar):
        i = pl.program_id(0)

        @pl.when(i == 0)
        def _():
            ext_q[0:HALO, :] = jnp.zeros((HALO, D_POOL), F32)
            ext_u[0:HALO, :] = jnp.zeros((HALO, D_CONV), F32)
            ext_x[0:HALO, :] = jnp.zeros((HALO, D_RNN), F32)
            hcar[...] = jnp.zeros((8, D_RNN), F32)

        ext_q[pl.ds(HALO, tm), :] = p_ref[:, 0:256]
        pooled, _, _ = _pool_fwd(ext_q, tm, i * tm)
        y_ref[:, 0:256] = (_dot(pooled.astype(BF16), wp[...]) * psc[...]).astype(BF16)

        ext_u[pl.ds(HALO, tm), :] = p_ref[:, 256:512] * _sig(p_ref[:, 512:768])
        act = _conv_ln_fwd(ext_u, dww, dwb[...], lng[...], lnb[...], tm)[4]
        y_ref[:, 256:512] = _dot(act.astype(BF16), wpw[...]).astype(BF16)

        ext_x[pl.ds(HALO, tm), :] = p_ref[:, 1280:1792]
        xc, _, _, ig, _, a, m = _rg_fwd(ext_x, cw, cb[...], wa[...], ba[...], wx[...], bx[...], lam[...], tm)
        a_s[...] = a
        b_s[...] = m * (ig * xc)
        _scan_rows(a_s, b_s, hs_ref, hcar, tm, reverse=False)
        y_ref[:, 512:1024] = (_gelu_and_grad(p_ref[:, 768:1280])[0] * hs_ref[...]).astype(BF16)

        ext_q[0:HALO, :] = ext_q[pl.ds(tm, HALO), :]
        ext_u[0:HALO, :] = ext_u[pl.ds(tm, HALO), :]
        ext_x[0:HALO, :] = ext_x[pl.ds(tm, HALO), :]

    ws = [mw[k] for k in _MIX_W]
    return pl.pallas_call(
        body, name="mixer_fwd", grid=(t // tm,),
        in_specs=[pl.BlockSpec((tm, D_IN), lambda i: (i, 0))] + [_full(w) for w in ws],
        out_specs=[pl.BlockSpec((tm, D_MODEL), lambda i: (i, 0)), pl.BlockSpec((tm, D_RNN), lambda i: (i, 0))],
        out_shape=[jax.ShapeDtypeStruct((t, D_MODEL), BF16), jax.ShapeDtypeStruct((t, D_RNN), F32)],
        scratch_shapes=[pltpu.VMEM((HALO + tm, D_POOL), F32), pltpu.VMEM((HALO + tm, D_CONV), F32),
                        pltpu.VMEM((HALO + tm, D_RNN), F32), pltpu.VMEM((tm, D_RNN), F32),
                        pltpu.VMEM((tm, D_RNN), F32), pltpu.VMEM((8, D_RNN), F32)],
        compiler_params=_params("arbitrary"),
    )(p, *ws)


_MIX_G = (("wp", (D_POOL, D_POOL)), ("psc", (1, D_POOL)), ("dww", (32, D_CONV)), ("dwb", (1, D_CONV)),
          ("lng", (1, D_CONV)), ("lnb", (1, D_CONV)), ("wpw", (D_CONV, D_CONV)), ("cw", (8, D_RNN)),
          ("cb", (1, D_RNN)), ("wa", (D_RNN, D_RNN)), ("ba", (1, D_RNN)), ("wx", (D_RNN, D_RNN)),
          ("bx", (1, D_RNN)), ("lam", (1, D_RNN)))


def _mixer_bwd(p, dy, hs, mw):
    t = p.shape[0]
    tm = _row_tile(t, TM_MIX)
    nt = t // tm
    hb = tm // HALO

    def body(p_ref, ph_ref, dy_ref, hs_ref, hsh_ref, wp, psc, dww, dwb, lng, lnb, wpw, cw, cb, wa, ba, wx, bx, lam,
             dp_ref, g_wp, g_psc, g_dww, g_dwb, g_lng, g_lnb, g_wpw, g_cw, g_cb, g_wa, g_ba, g_wx, g_bx, g_lam,
             ext_q, ext_u, ext_x, ext_h, ee, dc_s, dx_s, a_s, b_s, g_s, gcar):
        step = pl.program_id(0)
        i = nt - 1 - step
        grads = (g_wp, g_psc, g_dww, g_dwb, g_lng, g_lnb, g_wpw, g_cw, g_cb, g_wa, g_ba, g_wx, g_bx, g_lam)

        @pl.when(step == 0)
        def _():
            for gr in grads:
                gr[...] = jnp.zeros(gr.shape, F32)
            ee[pl.ds(tm, HALO), :] = jnp.zeros((HALO, D_POOL), F32)
            dc_s[pl.ds(tm, HALO), :] = jnp.zeros((HALO, D_CONV), F32)
            dx_s[pl.ds(tm, HALO), :] = jnp.zeros((HALO, D_RNN), F32)
            gcar[...] = jnp.zeros((8, D_RNN), F32)

        hm = jnp.where(i == 0, 0.0, 1.0)

        ext_q[0:HALO, :] = ph_ref[:, 0:256] * hm
        ext_q[pl.ds(HALO, tm), :] = p_ref[:, 0:256]
        pooled, lane, cnt = _pool_fwd(ext_q, tm, i * tm)
        pooled_b = pooled.astype(BF16)
        dya = dy_ref[:, 0:256]
        g_psc[...] += _colsum(dya * _dot(pooled_b, wp[...]))
        dmixed_b = (dya * psc[...]).astype(BF16)
        dpooled = _dot_nt(dmixed_b, wp[...])
        g_wp[...] += _dot_tn(pooled_b, dmixed_b)
        ee[0:tm, :] = dpooled / cnt
        dp_ref[:, 0:256] = _lane_sel(lane, *_window_sums(ee, 0, tm, 1)) - dpooled
        ee[pl.ds(tm, HALO), :] = ee[0:HALO, :]

        v = p_ref[:, 256:512]
        s = _sig(p_ref[:, 512:768])
        ext_u[0:HALO, :] = ph_ref[:, 256:512] * _sig(ph_ref[:, 512:768]) * hm
        ext_u[pl.ds(HALO, tm), :] = v * s
        z, rstd, l, sl, act = _conv_ln_fwd(ext_u, dww, dwb[...], lng[...], lnb[...], tm)
        dyb_b = dy_ref[:, 256:512].astype(BF16)
        dact = _dot_nt(dyb_b, wpw[...])
        g_wpw[...] += _dot_tn(act.astype(BF16), dyb_b)
        dl = dact * (sl * (1.0 + l * (1.0 - sl)))
        g_lng[...] += _colsum(dl * z)
        g_lnb[...] += _colsum(dl)
        dz = dl * lng[...]
        dc = rstd * (dz - jnp.mean(dz, axis=-1, keepdims=True) - z * jnp.mean(dz * z, axis=-1, keepdims=True))
        g_dwb[...] += _colsum(dc)
        dc_s[0:tm, :] = dc
        for k in range(CONV_K):
            g_dww[k:k + 1, :] += _colsum(dc * ext_u[pl.ds(HALO - (CONV_K - 1) + k, tm), :])
        du0 = dww[CONV_K - 1:CONV_K, :] * dc
        for j in range(1, CONV_K):
            du0 = du0 + dww[CONV_K - 1 - j:CONV_K - j, :] * dc_s[pl.ds(j, tm), :]
        dp_ref[:, 256:512] = du0 * s
        dp_ref[:, 512:768] = du0 * v * (s * (1.0 - s))
        dc_s[pl.ds(tm, HALO), :] = dc_s[0:HALO, :]

        ext_x[0:HALO, :] = ph_ref[:, 1280:1792] * hm
        ext_x[pl.ds(HALO, tm), :] = p_ref[:, 1280:1792]
        xc, xcb, r, ig, sp, a, m = _rg_fwd(ext_x, cw, cb[...], wa[...], ba[...], wx[...], bx[...], lam[...], tm)
        ext_h[0:HALO, :] = hsh_ref[...] * hm
        ext_h[pl.ds(HALO, tm), :] = hs_ref[...]
        dyc = dy_ref[:, 512:1024]
        gl, dgl = _gelu_and_grad(p_ref[:, 768:1280])
        dp_ref[:, 768:1280] = dyc * hs_ref[...] * dgl
        a_s[...] = a
        b_s[...] = dyc * gl
        _scan_rows(a_s, b_s, g_s, gcar, tm, reverse=True)
        g = g_s[...]
        da = g * ext_h[pl.ds(HALO - 1, tm), :]
        dm = g * (ig * xc)
        dig = g * (m * xc)
        dlog_a = da * a - dm * (a * a) / m
        g_lam[...] += _colsum(dlog_a * (-RG_C * r)) * (-_sig(-lam[...]))
        dra = (dlog_a * (-RG_C * sp)) * (r * (1.0 - r))
        dia = dig * (ig * (1.0 - ig))
        g_ba[...] += _colsum(dra)
        g_bx[...] += _colsum(dia)
        dra_b = dra.astype(BF16)
        dia_b = dia.astype(BF16)
        dxc = g * (m * ig) + _dot_nt(dra_b, wa[...]) + _dot_nt(dia_b, wx[...])
        g_wa[...] += _dot_tn(xcb, dra_b)
        g_wx[...] += _dot_tn(xcb, dia_b)
        g_cb[...] += _colsum(dxc)
        dx_s[0:tm, :] = dxc
        for k in range(RG_CONV_K):
            g_cw[k:k + 1, :] += _colsum(dxc * ext_x[pl.ds(HALO - (RG_CONV_K - 1) + k, tm), :])
        dxin = cw[RG_CONV_K - 1:RG_CONV_K, :] * dxc
        for j in range(1, RG_CONV_K):
            dxin = dxin + cw[RG_CONV_K - 1 - j:RG_CONV_K - j, :] * dx_s[pl.ds(j, tm), :]
        dp_ref[:, 1280:1792] = dxin
        dx_s[pl.ds(tm, HALO), :] = dx_s[0:HALO, :]

    ws = [mw[k] for k in _MIX_W]
    tile = lambda w: pl.BlockSpec((tm, w), lambda s: (nt - 1 - s, 0))
    halo = lambda w: pl.BlockSpec((HALO, w), lambda s: (jnp.maximum((nt - 1 - s) * hb - 1, 0), 0))
    outs = pl.pallas_call(
        body, name="mixer_bwd", grid=(nt,),
        in_specs=[tile(D_IN), halo(D_IN), tile(D_MODEL), tile(D_RNN), halo(D_RNN)] + [_full(w) for w in ws],
        out_specs=[tile(D_IN)] + [pl.BlockSpec(shp, lambda s: (0, 0)) for _, shp in _MIX_G],
        out_shape=[jax.ShapeDtypeStruct((t, D_IN), F32)] + [jax.ShapeDtypeStruct(shp, F32) for _, shp in _MIX_G],
        scratch_shapes=[pltpu.VMEM((HALO + tm, D_POOL), F32), pltpu.VMEM((HALO + tm, D_CONV), F32),
                        pltpu.VMEM((HALO + tm, D_RNN), F32), pltpu.VMEM((HALO + tm, D_RNN), F32),
                        pltpu.VMEM((tm + HALO, D_POOL), F32), pltpu.VMEM((tm + HALO, D_CONV), F32),
                        pltpu.VMEM((tm + HALO, D_RNN), F32), pltpu.VMEM((tm, D_RNN), F32),
                        pltpu.VMEM((tm, D_RNN), F32), pltpu.VMEM((tm, D_RNN), F32), pltpu.VMEM((8, D_RNN), F32)],
        compiler_params=_params("arbitrary"),
    )(p, p, dy, hs, hs, *ws)
    return outs[0], {k: o for (k, _), o in zip(_MIX_G, outs[1:])}


def _in_proj(h, g, w):
    t = h.shape[0]
    tm = _row_tile(t, TM_MAT)

    def body(h_ref, g_ref, w_ref, p_ref, u_ref):
        u = (_rms(h_ref[...])[1] * g_ref[...]).astype(BF16)
        u_ref[...] = u
        p_ref[...] = _dot(u, w_ref[...])

    return pl.pallas_call(
        body, name="in_proj", grid=(t // tm,),
        in_specs=[pl.BlockSpec((tm, D_MODEL), lambda i: (i, 0)), _full(g), _full(w)],
        out_specs=[pl.BlockSpec((tm, D_IN), lambda i: (i, 0)), pl.BlockSpec((tm, D_MODEL), lambda i: (i, 0))],
        out_shape=[jax.ShapeDtypeStruct((t, D_IN), F32), jax.ShapeDtypeStruct((t, D_MODEL), BF16)],
        compiler_params=_params("parallel"),
    )(h, g, w)


def _mid_fwd(y, h0, w_out, g, w_up):
    t = h0.shape[0]
    tm = _row_tile(t, TM_MAT)
    nj = D_FF // FF_CHUNK

    def body(y_ref, h0_ref, wo_ref, g_ref, wu_ref, h1_ref, u2_ref, f_ref):
        @pl.when(pl.program_id(1) == 0)
        def _():
            h1 = h0_ref[...] + _dot(y_ref[...], wo_ref[...])
            h1_ref[...] = h1
            u2_ref[...] = (_rms(h1)[1] * g_ref[...]).astype(BF16)

        f_ref[...] = _dot(u2_ref[...], wu_ref[...]).astype(BF16)

    row = lambda w: pl.BlockSpec((tm, w), lambda i, j: (i, 0))
    return pl.pallas_call(
        body, name="mid_fwd", grid=(t // tm, nj),
        in_specs=[row(D_MODEL), row(D_MODEL), _full(w_out), _full(g),
                  pl.BlockSpec((None, D_MODEL, FF_CHUNK), lambda i, j: (j, 0, 0))],
        out_specs=[row(D_MODEL), row(D_MODEL), pl.BlockSpec((tm, FF_CHUNK), lambda i, j: (i, j))],
        out_shape=[jax.ShapeDtypeStruct((t, D_MODEL), F32), jax.ShapeDtypeStruct((t, D_MODEL), BF16),
                   jax.ShapeDtypeStruct((t, D_FF), BF16)],
        compiler_params=_params("parallel", "arbitrary"),
    )(y, h0, w_out, g, w_up)


def _down_fwd(f, h1, w_down):
    t = h1.shape[0]
    tm = _row_tile(t, TM_MAT)

    def body(f_ref, h1_ref, wd_ref, h2_ref):
        acc = h1_ref[...]
        for c in range(D_FF // FF_CHUNK):
            cols = slice(c * FF_CHUNK, (c + 1) * FF_CHUNK)
            a = jnp.square(jnp.maximum(f_ref[:, cols].astype(F32), 0.0)).astype(BF16)
            acc = acc + _dot(a, wd_ref[cols, :])
        h2_ref[...] = acc

    return pl.pallas_call(
        body, name="down_fwd", grid=(t // tm,),
        in_specs=[pl.BlockSpec((tm, D_FF), lambda i: (i, 0)), pl.BlockSpec((tm, D_MODEL), lambda i: (i, 0)), _full(w_down)],
        out_specs=pl.BlockSpec((tm, D_MODEL), lambda i: (i, 0)),
        out_shape=jax.ShapeDtypeStruct((t, D_MODEL), F32),
        compiler_params=_params("parallel"),
    )(f, h1, w_down)


def _loss_head(h, g, tgt, t_real):
    t = h.shape[0]
    tm = _row_tile(t, TM_MAT)

    def body(h_ref, g_ref, tgt_ref, loss_ref, dh_ref, dg_ref):
        i = pl.program_id(0)

        @pl.when(i == 0)
        def _():
            loss_ref[...] = jnp.zeros(loss_ref.shape, F32)
            dg_ref[...] = jnp.zeros(dg_ref.shape, F32)

        r, n = _rms(h_ref[...])
        row = lax.broadcasted_iota(jnp.int32, (tm, 1), 0) + i * tm
        valid = jnp.logical_and(row >= N_META, row < t_real)
        diff = jnp.where(valid, n * g_ref[...] - tgt_ref[...], 0.0)
        loss_ref[...] += 0.5 * jnp.sum(jnp.mean(diff * diff, axis=-1, keepdims=True))
        dy = diff * (1.0 / D_MODEL)
        dg_ref[...] += _colsum(dy * n)
        dh_ref[...] = _rms_bwd(dy, n, r, g_ref[...])

    return pl.pallas_call(
        body, name="loss_head", grid=(t // tm,),
        in_specs=[pl.BlockSpec((tm, D_MODEL), lambda i: (i, 0)), _full(g), pl.BlockSpec((tm, D_MODEL), lambda i: (i, 0))],
        out_specs=[pl.BlockSpec((8, 128), lambda i: (0, 0)), pl.BlockSpec((tm, D_MODEL), lambda i: (i, 0)),
                   pl.BlockSpec((1, D_MODEL), lambda i: (0, 0))],
        out_shape=[jax.ShapeDtypeStruct((8, 128), F32), jax.ShapeDtypeStruct((t, D_MODEL), F32),
                   jax.ShapeDtypeStruct((1, D_MODEL), F32)],
        compiler_params=_params("arbitrary"),
    )(h, g, tgt)


def _mlp_bwd(dh2, f, h1, g, w_up, w_down):
    t = dh2.shape[0]
    tm = _row_tile(t, TM_MAT)
    nj = D_FF // FF_CHUNK

    def body(dh2_ref, f_ref, wd_ref, wu_ref, h1_ref, g_ref, df_ref, dh1_ref, dg_ref, acc, dhb):
        i, j = pl.program_id(0), pl.program_id(1)

        @pl.when(j == 0)
        def _():
            dhb[...] = dh2_ref[...].astype(BF16)
            acc[...] = jnp.zeros(acc.shape, F32)

        @pl.when(jnp.logical_and(i == 0, j == 0))
        def _():
            dg_ref[...] = jnp.zeros(dg_ref.shape, F32)

        dact = _dot_nt(dhb[...], wd_ref[...])
        df = (dact * (2.0 * jnp.maximum(f_ref[...].astype(F32), 0.0))).astype(BF16)
        df_ref[...] = df
        acc[...] += _dot_nt(df, wu_ref[...])

        @pl.when(j == nj - 1)
        def _():
            r, n = _rms(h1_ref[...])
            du2 = acc[...]
            dg_ref[...] += _colsum(du2 * n)
            dh1_ref[...] = dh2_ref[...] + _rms_bwd(du2, n, r, g_ref[...])

    row = lambda w: pl.BlockSpec((tm, w), lambda i, j: (i, 0))
    return pl.pallas_call(
        body, name="mlp_bwd", grid=(t // tm, nj),
        in_specs=[row(D_MODEL), pl.BlockSpec((tm, FF_CHUNK), lambda i, j: (i, j)),
                  pl.BlockSpec((None, FF_CHUNK, D_MODEL), lambda i, j: (j, 0, 0)),
                  pl.BlockSpec((None, D_MODEL, FF_CHUNK), lambda i, j: (j, 0, 0)), row(D_MODEL), _full(g)],
        out_specs=[pl.BlockSpec((tm, FF_CHUNK), lambda i, j: (i, j)), row(D_MODEL),
                   pl.BlockSpec((1, D_MODEL), lambda i, j: (0, 0))],
        out_shape=[jax.ShapeDtypeStruct((t, D_FF), BF16), jax.ShapeDtypeStruct((t, D_MODEL), F32),
                   jax.ShapeDtypeStruct((1, D_MODEL), F32)],
        scratch_shapes=[pltpu.VMEM((tm, D_MODEL), F32), pltpu.VMEM((tm, D_MODEL), BF16)],
        compiler_params=_params("arbitrary", "arbitrary"),
    )(dh2, f, w_down, w_up, h1, g)


def _out_bwd(dh1, w_out):
    t = dh1.shape[0]
    tm = _row_tile(t, TM_MAT)

    def body(dh_ref, w_ref, dy_ref):
        dy_ref[...] = _dot_nt(dh_ref[...].astype(BF16), w_ref[...])

    return pl.pallas_call(
        body, name="out_bwd", grid=(t // tm,),
        in_specs=[pl.BlockSpec((tm, D_MODEL), lambda i: (i, 0)), _full(w_out)],
        out_specs=pl.BlockSpec((tm, D_MODEL), lambda i: (i, 0)),
        out_shape=jax.ShapeDtypeStruct((t, D_MODEL), F32),
        compiler_params=_params("parallel"),
    )(dh1, w_out)


def _in_bwd(dp, dh1, h0, g, w_in):
    t = dp.shape[0]
    tm = _row_tile(t, TM_MAT)

    def body(dp_ref, dh1_ref, h0_ref, g_ref, w_ref, dh0_ref, dg_ref):
        @pl.when(pl.program_id(0) == 0)
        def _():
            dg_ref[...] = jnp.zeros(dg_ref.shape, F32)

        du = _dot_nt(dp_ref[...].astype(BF16), w_ref[...])
        r, n = _rms(h0_ref[...])
        dg_ref[...] += _colsum(du * n)
        dh0_ref[...] = dh1_ref[...] + _rms_bwd(du, n, r, g_ref[...])

    row = lambda w: pl.BlockSpec((tm, w), lambda i: (i, 0))
    return pl.pallas_call(
        body, name="in_bwd", grid=(t // tm,),
        in_specs=[row(D_IN), row(D_MODEL), row(D_MODEL), _full(g), _full(w_in)],
        out_specs=[row(D_MODEL), pl.BlockSpec((1, D_MODEL), lambda i: (0, 0))],
        out_shape=[jax.ShapeDtypeStruct((t, D_MODEL), F32), jax.ShapeDtypeStruct((1, D_MODEL), F32)],
        compiler_params=_params("arbitrary"),
    )(dp, dh1, h0, g, w_in)


def _tn_matmul(a, b, kc, nc, relu2, name):
    t, k = a.shape
    n = b.shape[1]
    tt = _row_tile(t, TM_MAT)
    gk, gn = k // kc, n // nc

    def body(a_ref, b_ref, o_ref):
        @pl.when(pl.program_id(2) == 0)
        def _():
            o_ref[...] = jnp.zeros(o_ref.shape, F32)

        av = a_ref[...]
        if relu2:
            av = jnp.square(jnp.maximum(av.astype(F32), 0.0))
        o_ref[...] += _dot_tn(av.astype(BF16), b_ref[...].astype(BF16))

    return pl.pallas_call(
        body, name=name, grid=(gk, gn, t // tt),
        in_specs=[pl.BlockSpec((tt, kc), lambda ik, jn, it: (it, ik)), pl.BlockSpec((tt, nc), lambda ik, jn, it: (it, jn))],
        out_specs=pl.BlockSpec((None, kc, nc), lambda ik, jn, it: (ik * gn + jn, 0, 0)),
        out_shape=jax.ShapeDtypeStruct((gk * gn, kc, nc), F32),
        compiler_params=_params("parallel", "parallel", "arbitrary"),
    )(a, b)


def _block_diag(blocks):
    nb, hd, _ = blocks.shape
    eye = jnp.eye(nb, dtype=blocks.dtype)
    return (blocks[:, :, None, :] * eye[:, None, :, None]).reshape(nb * hd, nb * hd)


def _diag_blocks(m, nb):
    hd = m.shape[0] // nb
    return jnp.stack([m[b * hd:(b + 1) * hd, b * hd:(b + 1) * hd] for b in range(nb)])


def _mixer_weights(w, l):
    row = lambda a: a.reshape(1, -1)
    return dict(
        wp=_block_diag(w["pool_w"][l]).astype(BF16), psc=row(w["pool_scale"][l]),
        dww=jnp.pad(w["convb_dw_w"][l], ((0, 32 - CONV_K), (0, 0))), dwb=row(w["convb_dw_b"][l]),
        lng=row(w["convb_ln_g"][l]), lnb=row(w["convb_ln_b"][l]), wpw=w["convb_pw_w"][l].astype(BF16),
        cw=jnp.pad(w["rg_conv_w"][l], ((0, 8 - RG_CONV_K), (0, 0))), cb=row(w["rg_conv_b"][l]),
        wa=_block_diag(w["rg_w_a"][l]).astype(BF16), ba=row(w["rg_b_a"][l]),
        wx=_block_diag(w["rg_w_x"][l]).astype(BF16), bx=row(w["rg_b_x"][l]), lam=row(w["rg_lambda"][l]))


def _local_step(h, tgt, t_real, w, fetch):
    depth = 2
    saved = []
    big = []
    for l in range(depth):
        mw = _mixer_weights(w, l)
        g1 = w["mix_norm_g"][l].reshape(1, -1)
        g2 = w["mlp_norm_g"][l].reshape(1, -1)
        wl = dict(w_in=fetch(l, "w_in", h))
        p, u = _in_proj(h, g1, wl["w_in"])
        y, hs = _mixer_fwd(p, mw)
        wl["w_out"], wl["w_up"] = fetch(l, "w_out", y), fetch(l, "w_up", y)
        h1, u2, f = _mid_fwd(y, h, wl["w_out"], g2, wl["w_up"])
        wl["w_down"] = fetch(l, "w_down", f)
        h2 = _down_fwd(f, h1, wl["w_down"].reshape(D_FF, D_MODEL))
        saved.append(dict(mw=mw, g1=g1, g2=g2, h0=h, p=p, u=u, y=y, hs=hs, h1=h1, u2=u2, f=f))
        big.append(wl)
        h = h2
    gf = w["final_norm_g"].reshape(1, -1)
    loss, dh, dgf = _loss_head(h, gf, tgt, t_real)

    gbig = [None] * depth
    gs = {k: [None] * depth for k in ("mix_norm_g", "mlp_norm_g", "pool_w", "pool_scale", "convb_dw_w", "convb_dw_b",
                                      "convb_ln_g", "convb_ln_b", "convb_pw_w", "rg_conv_w", "rg_conv_b", "rg_w_a",
                                      "rg_b_a", "rg_w_x", "rg_b_x", "rg_lambda")}
    for l in reversed(range(depth)):
        s = saved[l]
        df, dh1, dg2 = _mlp_bwd(dh, s["f"], s["h1"], s["g2"], big[l]["w_up"], big[l]["w_down"])
        g_down = _tn_matmul(s["f"], dh, FF_CHUNK, D_MODEL, True, "dw_down")
        g_up = _tn_matmul(s["u2"], df, D_MODEL, FF_CHUNK, False, "dw_up")
        dy = _out_bwd(dh1, big[l]["w_out"])
        g_out = _tn_matmul(s["y"], dh1, D_MODEL, D_MODEL, False, "dw_out")
        dp, mg = _mixer_bwd(s["p"], dy, s["hs"], s["mw"])
        dh, dg1 = _in_bwd(dp, dh1, s["h0"], s["g1"], big[l]["w_in"])
        g_in = _tn_matmul(s["u"], dp, D_MODEL, D_IN, False, "dw_in")
        gbig[l] = dict(
            w_in=g_in[0].reshape(D_MODEL, N_CHIPS, D_IN // N_CHIPS).transpose(1, 0, 2),
            w_out=g_out.reshape(N_CHIPS, D_MODEL // N_CHIPS, D_MODEL), w_up=g_up, w_down=g_down)
        gs["mix_norm_g"][l] = dg1[0]
        gs["mlp_norm_g"][l] = dg2[0]
        gs["pool_w"][l] = _diag_blocks(mg["wp"], D_POOL // POOL_GW)
        gs["pool_scale"][l] = mg["psc"][0]
        gs["convb_dw_w"][l] = mg["dww"][:CONV_K]
        gs["convb_dw_b"][l] = mg["dwb"][0]
        gs["convb_ln_g"][l] = mg["lng"][0]
        gs["convb_ln_b"][l] = mg["lnb"][0]
        gs["convb_pw_w"][l] = mg["wpw"]
        gs["rg_conv_w"][l] = mg["cw"][:RG_CONV_K]
        gs["rg_conv_b"][l] = mg["cb"][0]
        gs["rg_w_a"][l] = _diag_blocks(mg["wa"], D_RNN // RG_HD)
        gs["rg_b_a"][l] = mg["ba"][0]
        gs["rg_w_x"][l] = _diag_blocks(mg["wx"], D_RNN // RG_HD)
        gs["rg_b_x"][l] = mg["bx"][0]
        gs["rg_lambda"][l] = mg["lam"][0]
    gsmall = {k: jnp.stack(v) for k, v in gs.items()}
    gsmall["final_norm_g"] = dgf[0]
    return loss[0, 0], dh, gbig, gsmall


def _place():
    return lax.axis_index("x"), lax.axis_index("y"), lax.axis_index("c")


def _other_chips(x, y):
    return [(1 - x, y), (x, 1 - y), (1 - x, 1 - y)]


def _gather_now(srcs):
    ns = len(srcs)
    out_shape = [jax.ShapeDtypeStruct((N_CHIPS,) + s.shape, s.dtype) for s in srcs]

    def body(*refs):
        src_refs, dst_refs = refs[:ns], refs[ns:2 * ns]
        send_sems, recv_sems, loc_sems = refs[2 * ns:]
        x, y, c = _place()
        me = 2 * x + y
        local, remote = [], []
        for n in range(ns):
            cp = pltpu.make_async_copy(src_refs[n], dst_refs[n].at[me], loc_sems.at[n])
            cp.start()
            local.append(cp)
            for j, (px, py) in enumerate(_other_chips(x, y)):
                out = pltpu.make_async_remote_copy(src_refs[n], dst_refs[n].at[me], send_sems.at[3 * n + j],
                                                   recv_sems.at[3 * n + j], device_id=(px, py, c), device_id_type=MESH)
                out.start()
                remote.append(pltpu.make_async_remote_copy(src_refs[n], dst_refs[n].at[2 * px + py], send_sems.at[3 * n + j],
                                                           recv_sems.at[3 * n + j], device_id=(px, py, c), device_id_type=MESH))
        for cp in remote:
            cp.wait()
        for cp in local:
            cp.wait()

    return pl.pallas_call(
        body, name="gather_now", in_specs=[ANY] * ns, out_specs=[ANY] * ns, out_shape=out_shape,
        scratch_shapes=[pltpu.SemaphoreType.DMA((3 * ns,)), pltpu.SemaphoreType.DMA((3 * ns,)), pltpu.SemaphoreType.DMA((ns,))],
    )(*srcs)


HBM_SPEC = pl.BlockSpec(memory_space=pltpu.HBM)
SEM_SPEC = pl.BlockSpec(memory_space=pltpu.SEMAPHORE)
DATAFLOW = pltpu.SideEffectType.DATAFLOW_SIDE_EFFECTING


def _gather_copies(src_refs, land_refs, send_sem, recv_sem, first):
    x, y, c = _place()
    me = 2 * x + y
    out = []
    for n in range(len(src_refs)):
        for j, (px, py) in enumerate(_other_chips(x, y)):
            out.append(pltpu.make_async_remote_copy(src_refs[n], land_refs[n].at[me], send_sem.at[first + 3 * n + j],
                                                    recv_sem.at[first + 3 * n + j], device_id=(px, py, c), device_id_type=MESH))
    return out


def _gather_start(groups, me):
    srcs = [pltpu.with_memory_space_constraint(s, pltpu.HBM) for g in groups for s in g]
    lands = [pltpu.with_memory_space_constraint(
        lax.dynamic_update_slice(jnp.zeros((N_CHIPS,) + s.shape, s.dtype), s[None], (me,) + (0,) * s.ndim), pltpu.HBM)
        for g in groups for s in g]
    n, ng = len(srcs), len(groups)
    first = [sum(len(g) for g in groups[:i]) for i in range(ng)]

    def body(*refs):
        src_refs, land_refs = refs[:n], refs[n:2 * n]
        sems = refs[2 * n:2 * n + 2 * ng]
        token = refs[-1]
        for gi, g in enumerate(groups):
            lo, hi = first[gi], first[gi] + len(g)
            for cp in _gather_copies(src_refs[lo:hi], land_refs[lo:hi], sems[2 * gi], sems[2 * gi + 1], 0):
                cp.start()
        token[...] = jnp.zeros(token.shape, token.dtype)

    sem_shapes = [pltpu.SemaphoreType.DMA((3 * len(g),)) for g in groups for _ in range(2)]
    outs = pl.pallas_call(
        body, name="gather_start",
        out_shape=sem_shapes + [pltpu.HBM(a.shape, a.dtype) for a in srcs + lands] + [jax.ShapeDtypeStruct((8, 128), F32)],
        in_specs=[HBM_SPEC] * (2 * n),
        out_specs=[SEM_SPEC] * (2 * ng) + [HBM_SPEC] * (2 * n) + [pl.BlockSpec(memory_space=pltpu.VMEM)],
        input_output_aliases={i: 2 * ng + i for i in range(2 * n)},
        compiler_params=pltpu.CompilerParams(has_side_effects=DATAFLOW),
    )(*srcs, *lands)
    sems, thru, token = outs[:2 * ng], outs[2 * ng:2 * ng + 2 * n], outs[-1]
    state = []
    for gi, g in enumerate(groups):
        lo, hi = first[gi], first[gi] + len(g)
        state.append((sems[2 * gi], sems[2 * gi + 1], thru[lo:hi], thru[n + lo:n + hi]))
    return state, token


def _gather_wait(state, after, name):
    send_sem, recv_sem, srcs, lands = state
    n = len(srcs)

    def body(*refs):
        src_refs, land_refs = refs[:n], refs[n:2 * n]
        send, recv = refs[2 * n], refs[2 * n + 1]
        for cp in _gather_copies(src_refs, land_refs, send, recv, 0):
            cp.wait_send()
            cp.wait_recv()

    outs = pl.pallas_call(
        body, name=name,
        out_shape=[pltpu.HBM(a.shape, a.dtype) for a in list(srcs) + list(lands)],
        in_specs=[HBM_SPEC] * (2 * n) + [SEM_SPEC, SEM_SPEC, ANY],
        out_specs=[HBM_SPEC] * (2 * n),
        input_output_aliases={i: i for i in range(2 * n)},
        compiler_params=pltpu.CompilerParams(has_side_effects=DATAFLOW),
    )(*srcs, *lands, send_sem, recv_sem, after)
    return outs[n:]


def _send_halves_to_sibling(gs):
    n = len(gs)
    out_shape = [jax.ShapeDtypeStruct((g.shape[0], g.shape[1] // 2, g.shape[2]), g.dtype) for g in gs]

    def body(*refs):
        g_refs, out_refs = refs[:n], refs[n:2 * n]
        send_sems, recv_sems = refs[2 * n:]
        x, y, c = _place()
        copies = []
        for i in range(n):
            r2 = g_refs[i].shape[1] // 2
            src = g_refs[i].at[:, pl.ds(pl.multiple_of((1 - c) * r2, 8), r2)]
            cp = pltpu.make_async_remote_copy(src, out_refs[i], send_sems.at[i], recv_sems.at[i],
                                              device_id=(x, y, 1 - c), device_id_type=MESH)
            cp.start()
            copies.append(cp)
        for cp in copies:
            cp.wait()

    return pl.pallas_call(
        body, name="rs_sibling", in_specs=[ANY] * n, out_specs=[ANY] * n, out_shape=out_shape,
        scratch_shapes=[pltpu.SemaphoreType.DMA((n,)), pltpu.SemaphoreType.DMA((n,))],
    )(*gs)


def _add_halves(g, recv, c1):
    nk, r, cd = g.shape
    r2 = r // 2

    def body(c_ref, g_ref, r_ref, pa_ref, pab_ref):
        s = g_ref[...] + r_ref[...]
        pa_ref[...] = s
        pab_ref[...] = s.astype(BF16)

    blk = pl.BlockSpec((None, r2, cd), lambda k, c_ref: (k, 0, 0))
    return pl.pallas_call(
        body, name="rs_add_halves",
        grid_spec=pltpu.PrefetchScalarGridSpec(
            num_scalar_prefetch=1, grid=(nk,),
            in_specs=[pl.BlockSpec((None, r2, cd), lambda k, c_ref: (k, c_ref[0], 0)), blk], out_specs=[blk, blk]),
        out_shape=[jax.ShapeDtypeStruct((nk, r2, cd), F32), jax.ShapeDtypeStruct((nk, r2, cd), BF16)],
        compiler_params=_params("parallel"),
    )(c1, g, recv)


def _send_partials_to_chips(pabs):
    n = len(pabs)
    out_shape = [jax.ShapeDtypeStruct((3,) + p.shape[1:], p.dtype) for p in pabs]

    def body(*refs):
        p_refs, out_refs = refs[:n], refs[n:2 * n]
        send_sems, recv_sems = refs[2 * n:]
        x, y, c = _place()
        copies = []
        for i in range(n):
            for j, (px, py) in enumerate(_other_chips(x, y)):
                cp = pltpu.make_async_remote_copy(p_refs[i].at[2 * px + py], out_refs[i].at[j], send_sems.at[3 * i + j],
                                                  recv_sems.at[3 * i + j], device_id=(px, py, c), device_id_type=MESH)
                cp.start()
                copies.append(cp)
        for cp in copies:
            cp.wait()

    return pl.pallas_call(
        body, name="rs_chips", in_specs=[ANY] * n, out_specs=[ANY] * n, out_shape=out_shape,
        scratch_shapes=[pltpu.SemaphoreType.DMA((3 * n,)), pltpu.SemaphoreType.DMA((3 * n,))],
    )(*pabs)


def _sum_partials(pa, recv, me1):
    nk, r2, cd = pa.shape

    def body(me_ref, pa_ref, r_ref, s_ref):
        s_ref[...] = ((pa_ref[...] + r_ref[0].astype(F32)) + r_ref[1].astype(F32)) + r_ref[2].astype(F32)

    return pl.pallas_call(
        body, name="rs_sum_partials",
        grid_spec=pltpu.PrefetchScalarGridSpec(
            num_scalar_prefetch=1, grid=(1,),
            in_specs=[pl.BlockSpec((None, r2, cd), lambda i, me_ref: (me_ref[0], 0, 0)),
                      pl.BlockSpec((3, r2, cd), lambda i, me_ref: (0, 0, 0))],
            out_specs=pl.BlockSpec((r2, cd), lambda i, me_ref: (0, 0))),
        out_shape=jax.ShapeDtypeStruct((r2, cd), F32),
        compiler_params=_params("arbitrary"),
    )(me1, pa, recv)


def _share_sums_with_sibling(sums):
    na = len(sums)
    flat = [s for per_layer in sums for s in per_layer]
    n = len(flat)
    out_shape = [jax.ShapeDtypeStruct((2, 2 * per_layer[0].shape[0], per_layer[0].shape[1]), F32) for per_layer in sums]

    def body(*refs):
        s_refs, out_refs = refs[:n], refs[n:n + na]
        send_sems, recv_sems, loc_sems = refs[n + na:]
        x, y, c = _place()
        copies = []
        for a in range(na):
            for l in range(2):
                i = 2 * a + l
                r2 = s_refs[i].shape[0]
                mine = out_refs[a].at[l, pl.ds(pl.multiple_of(c * r2, 8), r2)]
                cp = pltpu.make_async_copy(s_refs[i], mine, loc_sems.at[i])
                cp.start()
                copies.append(cp)
                cp = pltpu.make_async_remote_copy(s_refs[i], mine, send_sems.at[i], recv_sems.at[i],
                                                  device_id=(x, y, 1 - c), device_id_type=MESH)
                cp.start()
                copies.append(cp)
        for cp in copies:
            cp.wait()

    return pl.pallas_call(
        body, name="rs_share", in_specs=[ANY] * n, out_specs=[ANY] * na, out_shape=out_shape,
        scratch_shapes=[pltpu.SemaphoreType.DMA((n,)), pltpu.SemaphoreType.DMA((n,)), pltpu.SemaphoreType.DMA((n,))],
    )(*flat)


def _allreduce_small(v):
    def body(v_ref, out_ref, rbuf, send_sems, recv_sems):
        x, y, c = _place()
        out_ref[...] = v_ref[...]
        for s, peer in enumerate([(x, y, 1 - c), (1 - x, y, c), (x, 1 - y, c)]):
            cp = pltpu.make_async_remote_copy(out_ref, rbuf.at[s], send_sems.at[s], recv_sems.at[s],
                                              device_id=peer, device_id_type=MESH)
            cp.start()
            cp.wait()
            out_ref[...] = out_ref[...] + rbuf[s]

    vm = pl.BlockSpec(memory_space=pltpu.VMEM)
    return pl.pallas_call(
        body, name="allreduce_small", in_specs=[vm], out_specs=vm, out_shape=jax.ShapeDtypeStruct(v.shape, v.dtype),
        scratch_shapes=[pltpu.VMEM((3,) + v.shape, v.dtype), pltpu.SemaphoreType.DMA((3,)), pltpu.SemaphoreType.DMA((3,))],
        compiler_params=pltpu.CompilerParams(vmem_limit_bytes=VMEM_LIMIT),
    )(v)


def _adamw_math(w, g, m, v):
    m = ADAM_B1 * m + (1.0 - ADAM_B1) * g
    v = ADAM_B2 * v + (1.0 - ADAM_B2) * jnp.square(g)
    m_hat = m / (1.0 - ADAM_B1 ** ADAM_STEP)
    v_hat = v / (1.0 - ADAM_B2 ** ADAM_STEP)
    return -ADAM_LR * (m_hat / (jnp.sqrt(v_hat) + ADAM_EPS) + ADAM_WD * w), m, v


def _adamw_big(w, g, m, v):
    shape = w.shape
    cd = shape[-1]
    rows = w.size // cd
    tr = 512
    flat = lambda a: a.reshape(rows, cd)

    def body(w_ref, g_ref, m_ref, v_ref, d_ref, mo_ref, vo_ref):
        d_ref[...], mo_ref[...], vo_ref[...] = _adamw_math(w_ref[...], g_ref[...], m_ref[...], v_ref[...])

    blk = pl.BlockSpec((tr, cd), lambda i: (i, 0))
    outs = pl.pallas_call(
        body, name="adamw_big", grid=(rows // tr,), in_specs=[blk] * 4, out_specs=[blk] * 3,
        out_shape=[jax.ShapeDtypeStruct((rows, cd), F32)] * 3, compiler_params=_params("parallel"),
    )(flat(w), flat(g), flat(m), flat(v))
    return [o.reshape(shape) for o in outs]


def _adamw_small(ws, ms, vs, gpack, offsets, strides, me1):
    n = len(ws)

    def body(me_ref, *refs):
        w_refs, m_refs, v_refs = refs[:n], refs[n:2 * n], refs[2 * n:3 * n]
        g_ref = refs[3 * n]
        outs = refs[3 * n + 1:]
        for i in range(n):
            rows = w_refs[i].shape[0]
            if strides[i]:
                g = g_ref[pl.ds(pl.multiple_of(offsets[i] + me_ref[0] * strides[i], 8), rows), :]
            else:
                g = g_ref[offsets[i]:offsets[i] + rows, :]
            d, m, v = _adamw_math(w_refs[i][...], g, m_refs[i][...], v_refs[i][...])
            outs[4 * i][...] = g
            outs[4 * i + 1][...] = d
            outs[4 * i + 2][...] = m
            outs[4 * i + 3][...] = v

    full = lambda a: pl.BlockSpec(a.shape, lambda i, me_ref: (0, 0))
    ins = list(ws) + list(ms) + list(vs) + [gpack]
    out_arrs = [w for w in ws for _ in range(4)]
    outs = pl.pallas_call(
        body, name="adamw_small",
        grid_spec=pltpu.PrefetchScalarGridSpec(num_scalar_prefetch=1, grid=(1,), in_specs=[full(a) for a in ins],
                                               out_specs=[full(a) for a in out_arrs]),
        out_shape=[jax.ShapeDtypeStruct(a.shape, F32) for a in out_arrs],
        compiler_params=_params("arbitrary"),
    )(me1, *ins)
    return [outs[4 * i:4 * i + 4] for i in range(n)]


LANES = 128
SUBLANES = 8


def _rows_of(size):
    return -(-size // (LANES * SUBLANES)) * SUBLANES


def _as_rows(a, rows=None):
    flat = a.reshape(-1)
    rows = _rows_of(flat.size) if rows is None else rows
    return jnp.pad(flat, (0, rows * LANES - flat.size)).reshape(rows, LANES)


def _to_shard_major(name, full):
    if name == "meta_tokens":
        return full.reshape(N_META, N_CHIPS, -1).transpose(1, 0, 2)
    if name == "convb_pw_w":
        return full.reshape(2, N_CHIPS, -1, D_CONV).transpose(1, 0, 2, 3)
    return full.reshape(full.shape[0], full.shape[1], N_CHIPS, -1).transpose(2, 0, 1, 3)


def _from_shard_major(name, sm):
    if name == "meta_tokens":
        return sm.transpose(1, 0, 2).reshape(N_META, -1)
    if name == "convb_pw_w":
        return sm.transpose(1, 0, 2, 3).reshape(2, -1, D_CONV)
    return sm.transpose(1, 2, 0, 3).reshape(sm.shape[1], sm.shape[2], -1)


def kernel(x, meta_tokens, mix_norm_g, w_in, pool_w, pool_scale, convb_dw_w, convb_dw_b, convb_ln_g, convb_ln_b, convb_pw_w, rg_conv_w, rg_conv_b, rg_w_a, rg_b_a, rg_w_x, rg_b_x, rg_lambda, w_out, mlp_norm_g, w_up, w_down, final_norm_g, loss_target, m_meta_tokens, m_mix_norm_g, m_w_in, m_pool_w, m_pool_scale, m_convb_dw_w, m_convb_dw_b, m_convb_ln_g, m_convb_ln_b, m_convb_pw_w, m_rg_conv_w, m_rg_conv_b, m_rg_w_a, m_rg_b_a, m_rg_w_x, m_rg_b_x, m_rg_lambda, m_w_out, m_mlp_norm_g, m_w_up, m_w_down, m_final_norm_g, v_meta_tokens, v_mix_norm_g, v_w_in, v_pool_w, v_pool_scale, v_convb_dw_w, v_convb_dw_b, v_convb_ln_g, v_convb_ln_b, v_convb_pw_w, v_rg_conv_w, v_rg_conv_b, v_rg_w_a, v_rg_b_a, v_rg_w_x, v_rg_b_x, v_rg_lambda, v_w_out, v_mlp_norm_g, v_w_up, v_w_down, v_final_norm_g):
    given = dict(locals())
    w = {k: given[k] for k in WEIGHTS}
    mom = {k: given["m_" + k] for k in WEIGHTS}
    var = {k: given["v_" + k] for k in WEIGHTS}
    xi, yi, ci = _place()
    me1 = (2 * xi + yi).astype(jnp.int32).reshape(1)
    c1 = ci.astype(jnp.int32).reshape(1)

    small_rows = [_rows_of(w[k].size) for k in SMALL_SHARDED]
    small_pack = jnp.concatenate([_as_rows(w[k]) for k in SMALL_SHARDED])
    shard = lambda l, k: w[k][l].astype(BF16)
    w_in0, small_all = _gather_now([shard(0, "w_in"), small_pack])
    order = [[(0, "w_out"), (0, "w_up")], [(0, "w_down")], [(1, "w_in"), (1, "w_out"), (1, "w_up"), (1, "w_down")]]
    state, token = _gather_start([[shard(l, k) for l, k in g] for g in order], me1[0])
    landed = {}

    def fetch(l, k, after):
        if (l, k) == (0, "w_in"):
            raw = w_in0
        else:
            gi = [i for i, g in enumerate(order) if (l, k) in g][0]
            if gi not in landed:
                landed[gi] = _gather_wait(state[gi], after, "gather_wait_%d" % gi)
            raw = landed[gi][order[gi].index((l, k))]
        if k == "w_in":
            return raw.transpose(1, 0, 2).reshape(D_MODEL, D_IN)
        return raw.reshape(D_MODEL, D_MODEL) if k == "w_out" else raw

    wfull = dict(w)
    off = 0
    for k, rows in zip(SMALL_SHARDED, small_rows):
        sm = small_all[:, off:off + rows].reshape(N_CHIPS, -1)[:, :w[k].size].reshape((N_CHIPS,) + w[k].shape)
        wfull[k] = _from_shard_major(k, sm)
        off += rows
    wfull["mix_norm_g"] = w["mix_norm_g"] + token[0, 0]

    seq = x.shape[1]
    t_real = N_META + seq
    t_pad = -(-t_real // ROW_ALIGN) * ROW_ALIGN
    tail = jnp.zeros((t_pad - t_real, D_MODEL), F32)
    h = jnp.concatenate([wfull["meta_tokens"], x[0], tail])
    tgt = jnp.concatenate([jnp.zeros((N_META, D_MODEL), F32), loss_target[0], tail])
    loss, dh, gbig, gsmall = _local_step(h, tgt, t_real, wfull, fetch)
    grad_x = dh[N_META:t_real][None]
    gsmall["meta_tokens"] = dh[:N_META]

    parts = [gbig[l][k] for k in BIG for l in range(2)]
    recv_a = _send_halves_to_sibling(parts)
    added = [_add_halves(g, r, c1) for g, r in zip(parts, recv_a)]
    recv_b = _send_partials_to_chips([pab for _, pab in added])
    sums = [_sum_partials(pa, rb, me1) for (pa, _), rb in zip(added, recv_b)]
    grads_big = _share_sums_with_sibling([sums[2 * a:2 * a + 2] for a in range(len(BIG))])

    pieces, offsets, strides = [], {}, {}
    row = 0
    for k in SMALL_REPL:
        rows = _rows_of(w[k].size)
        pieces.append(_as_rows(gsmall[k], rows))
        offsets[k], strides[k] = row, 0
        row += rows
    for k in SMALL_SHARDED:
        rows = _rows_of(w[k].size)
        sm = _to_shard_major(k, gsmall[k]).reshape(N_CHIPS, -1)
        pieces.append(jnp.pad(sm, ((0, 0), (0, rows * LANES - sm.shape[1]))).reshape(N_CHIPS * rows, LANES))
        offsets[k], strides[k] = row, rows
        row += N_CHIPS * rows
    gpack = _allreduce_small(jnp.concatenate(pieces))

    out = {}
    for a, k in enumerate(BIG):
        d, m2, v2 = _adamw_big(w[k], grads_big[a], mom[k], var[k])
        out[k] = (grads_big[a], d, m2, v2)
    names = SMALL_REPL + SMALL_SHARDED
    as_rows = lambda a: a.reshape(-1, LANES) if a.size % (LANES * SUBLANES) == 0 or a.size < LANES * SUBLANES else _as_rows(a)
    res = _adamw_small([as_rows(w[k]) for k in names], [as_rows(mom[k]) for k in names], [as_rows(var[k]) for k in names],
                       gpack, [offsets[k] for k in names], [strides[k] for k in names], me1)
    for k, r4 in zip(names, res):
        out[k] = tuple(o.reshape(-1)[:w[k].size].reshape(w[k].shape) for o in r4)

    loss = lax.psum(loss, ("x", "y", "c"))
    return (loss, grad_x, *[out[k][0] for k in WEIGHTS], *[out[k][1] for k in WEIGHTS],
            *[out[k][2] for k in WEIGHTS], *[out[k][3] for k in WEIGHTS])
```

```python
import functools

import jax
import jax.numpy as jnp
from jax import lax
from jax.experimental import pallas as pl
from jax.experimental.pallas import tpu as pltpu

F32, BF16 = jnp.float32, jnp.bfloat16
MESH = pl.DeviceIdType.MESH
ANY = pl.BlockSpec(memory_space=pl.ANY)

D_MODEL = 1024
N_META = 16
D_POOL = 256
D_CONV = 256
D_RNN = 512
D_IN = D_POOL + 2 * D_CONV + 2 * D_RNN
D_FF = 4096
FF_CHUNK = 1024
POOL_GW = 64
CONV_K = 31
RG_CONV_K = 4
RG_HD = 64
RG_C = 8.0
EPS = 1e-6
ADAM_LR, ADAM_B1, ADAM_B2, ADAM_EPS, ADAM_WD, ADAM_STEP = 0.001, 0.9, 0.999, 1e-08, 0.01, 10

HALO = 32
ROW_ALIGN = 256
TM_MIX = 256
TM_MAT = 768
N_CHIPS = 4
VMEM_LIMIT = 56 * 1024 * 1024

BIG = ("w_in", "w_out", "w_up", "w_down")
SMALL_SHARDED = ("meta_tokens", "convb_dw_w", "convb_pw_w", "rg_conv_w")
SMALL_REPL = ("mix_norm_g", "pool_w", "pool_scale", "convb_dw_b", "convb_ln_g", "convb_ln_b", "rg_conv_b",
              "rg_w_a", "rg_b_a", "rg_w_x", "rg_b_x", "rg_lambda", "mlp_norm_g", "final_norm_g")
WEIGHTS = ("meta_tokens", "mix_norm_g", "w_in", "pool_w", "pool_scale", "convb_dw_w", "convb_dw_b", "convb_ln_g",
           "convb_ln_b", "convb_pw_w", "rg_conv_w", "rg_conv_b", "rg_w_a", "rg_b_a", "rg_w_x", "rg_b_x",
           "rg_lambda", "w_out", "mlp_norm_g", "w_up", "w_down", "final_norm_g")


def _params(*sem):
    return pltpu.CompilerParams(dimension_semantics=sem, vmem_limit_bytes=VMEM_LIMIT)


def _row_tile(t, cap):
    best = None
    for tm in range(128, cap + 1, 128):
        if t % tm == 0:
            best = tm
    assert best is not None, (t, cap)
    return best


def _dot(a, b):
    return jnp.dot(a, b, preferred_element_type=F32)


def _dot_nt(a, b):
    return lax.dot_general(a, b, (((1,), (1,)), ((), ())), preferred_element_type=F32)


def _dot_tn(a, b):
    return lax.dot_general(a, b, (((0,), (0,)), ((), ())), preferred_element_type=F32)


def _rms(x):
    r = lax.rsqrt(jnp.mean(x * x, axis=-1, keepdims=True) + EPS)
    return r, x * r


def _rms_bwd(du, n, r, g):
    dn = du * g
    return r * (dn - n * jnp.mean(dn * n, axis=-1, keepdims=True))


def _sig(x):
    return jax.nn.sigmoid(x)


def _colsum(x):
    return jnp.sum(x, axis=0, keepdims=True)


def _neg_expm1(x):
    series = -x * (1.0 + x * (0.5 + x * (1.0 / 6 + x * (1.0 / 24 + x * (1.0 / 120)))))
    return jnp.where(x > -0.05, series, 1.0 - jnp.exp(x))


_GELU_K0 = 0.7978845608028654
_GELU_K1 = 0.044715


def _gelu_and_grad(x):
    th = jnp.tanh(_GELU_K0 * (x + _GELU_K1 * x * x * x))
    val = 0.5 * x * (1.0 + th)
    grad = 0.5 * (1.0 + th) + 0.5 * x * (1.0 - th * th) * _GELU_K0 * (1.0 + 3.0 * _GELU_K1 * x * x)
    return val, grad


def _full(a):
    nd = a.ndim
    return pl.BlockSpec(a.shape, lambda *_: (0,) * nd)


def _after(body, n_in, deps):
    def wrapped(*refs):
        return body(*refs[:n_in], *refs[n_in + len(deps):])
    return wrapped


def _lane_sel(lane, a2, a4, a8, a16):
    return jnp.where(lane < POOL_GW, a2, jnp.where(lane < 2 * POOL_GW, a4, jnp.where(lane < 3 * POOL_GW, a8, a16)))


def _window_sums(ref, base, tm, step):
    sh = lambda j: ref[pl.ds(base + step * j, tm), :]
    s2 = sh(0) + sh(1)
    s4 = s2 + sh(2) + sh(3)
    s8 = s4 + sh(4) + sh(5) + sh(6) + sh(7)
    s16 = s8 + sh(8) + sh(9) + sh(10) + sh(11) + sh(12) + sh(13) + sh(14) + sh(15)
    return s2, s4, s8, s16


def _pool_counts(tm, t0):
    lane = lax.broadcasted_iota(jnp.int32, (tm, D_POOL), 1)
    row = lax.broadcasted_iota(jnp.int32, (tm, D_POOL), 0) + t0
    cnt = jnp.minimum(row + 1, _lane_sel(lane, 2, 4, 8, 16)).astype(F32)
    return lane, cnt


def _pool_fwd(ext_q, tm, t0):
    lane, cnt = _pool_counts(tm, t0)
    q = ext_q[pl.ds(HALO, tm), :]
    pooled = _lane_sel(lane, *_window_sums(ext_q, HALO, tm, -1)) / cnt - q
    return pooled, lane, cnt


def _conv_ln_fwd(ext_u, dww_ref, dwb, lng, lnb, tm):
    c = dwb + dww_ref[0:1, :] * ext_u[pl.ds(HALO - (CONV_K - 1), tm), :]
    for k in range(1, CONV_K):
        c = c + dww_ref[k:k + 1, :] * ext_u[pl.ds(HALO - (CONV_K - 1) + k, tm), :]
    mu = jnp.mean(c, axis=-1, keepdims=True)
    cc = c - mu
    rstd = lax.rsqrt(jnp.mean(cc * cc, axis=-1, keepdims=True) + EPS)
    z = cc * rstd
    l = z * lng + lnb
    sl = _sig(l)
    return z, rstd, l, sl, l * sl


def _rg_fwd(ext_x, cw_ref, cb, wa, ba, wx, bx, lam, tm):
    xc = cb + cw_ref[0:1, :] * ext_x[pl.ds(HALO - (RG_CONV_K - 1), tm), :]
    for k in range(1, RG_CONV_K):
        xc = xc + cw_ref[k:k + 1, :] * ext_x[pl.ds(HALO - (RG_CONV_K - 1) + k, tm), :]
    xcb = xc.astype(BF16)
    r = _sig(_dot(xcb, wa) + ba)
    ig = _sig(_dot(xcb, wx) + bx)
    sp = jnp.maximum(-lam, 0.0) + jnp.log(1.0 + jnp.exp(-jnp.abs(lam)))
    log_a = (-RG_C * r) * sp
    a = jnp.exp(log_a)
    m = jnp.sqrt(_neg_expm1(2.0 * log_a))
    return xc, xcb, r, ig, sp, a, m


def _scan_rows(a_ref, b_ref, out_ref, carry, tm, reverse):
    rows = lax.broadcasted_iota(jnp.int32, (8, D_RNN), 0)
    ngrp = tm // 8

    def grp(gi, hb):
        st = pl.multiple_of((ngrp - 1 - gi if reverse else gi) * 8, 8)
        a8 = a_ref[pl.ds(st, 8), :]
        b8 = b_ref[pl.ds(st, 8), :]
        out = jnp.zeros((8, D_RNN), F32)
        for j in (range(7, -1, -1) if reverse else range(8)):
            aj = jnp.broadcast_to(a8[j:j + 1, :], (8, D_RNN))
            bj = jnp.broadcast_to(b8[j:j + 1, :], (8, D_RNN))
            if reverse:
                cur = bj + hb
                hb = aj * cur
            else:
                cur = aj * hb + bj
                hb = cur
            out = jnp.where(rows == j, cur, out)
        out_ref[pl.ds(st, 8), :] = out
        return hb

    carry[...] = lax.fori_loop(0, ngrp, grp, carry[...])


_MIX_W = ("wp", "psc", "dww", "dwb", "lng", "lnb", "wpw", "cw", "cb", "wa", "ba", "wx", "bx", "lam")


def _mixer_fwd(p, mw):
    t = p.shape[0]
    tm = _row_tile(t, TM_MIX)

    def body(p_ref, wp, psc, dww, dwb, lng, lnb, wpw, cw, cb, wa, ba, wx, bx, lam, y_ref, hs_ref,
             ext_q, ext_u, ext_x, a_s, b_s, hcar):
        i = pl.program_id(0)

        @pl.when(i == 0)
        def _():
            ext_q[0:HALO, :] = jnp.zeros((HALO, D_POOL), F32)
            ext_u[0:HALO, :] = jnp.zeros((HALO, D_CONV), F32)
            ext_x[0:HALO, :] = jnp.zeros((HALO, D_RNN), F32)
            hcar[...] = jnp.zeros((8, D_RNN), F32)

        ext_q[pl.ds(HALO, tm), :] = p_ref[:, 0:256]
        pooled, _, _ = _pool_fwd(ext_q, tm, i * tm)
        y_ref[:, 0:256] = (_dot(pooled.astype(BF16), wp[...]) * psc[...]).astype(BF16)

        ext_u[pl.ds(HALO, tm), :] = p_ref[:, 256:512] * _sig(p_ref[:, 512:768])
        act = _conv_ln_fwd(ext_u, dww, dwb[...], lng[...], lnb[...], tm)[4]
        y_ref[:, 256:512] = _dot(act.astype(BF16), wpw[...]).astype(BF16)

        ext_x[pl.ds(HALO, tm), :] = p_ref[:, 1280:1792]
        xc, _, _, ig, _, a, m = _rg_fwd(ext_x, cw, cb[...], wa[...], ba[...], wx[...], bx[...], lam[...], tm)
        a_s[...] = a
        b_s[...] = m * (ig * xc)
        _scan_rows(a_s, b_s, hs_ref, hcar, tm, reverse=False)
        y_ref[:, 512:1024] = (_gelu_and_grad(p_ref[:, 768:1280])[0] * hs_ref[...]).astype(BF16)

        ext_q[0:HALO, :] = ext_q[pl.ds(tm, HALO), :]
        ext_u[0:HALO, :] = ext_u[pl.ds(tm, HALO), :]
        ext_x[0:HALO, :] = ext_x[pl.ds(tm, HALO), :]

    ws = [mw[k] for k in _MIX_W]
    return pl.pallas_call(
        body, name="mixer_fwd", grid=(t // tm,),
        in_specs=[pl.BlockSpec((tm, D_IN), lambda i: (i, 0))] + [_full(w) for w in ws],
        out_specs=[pl.BlockSpec((tm, D_MODEL), lambda i: (i, 0)), pl.BlockSpec((tm, D_RNN), lambda i: (i, 0))],
        out_shape=[jax.ShapeDtypeStruct((t, D_MODEL), BF16), jax.ShapeDtypeStruct((t, D_RNN), F32)],
        scratch_shapes=[pltpu.VMEM((HALO + tm, D_POOL), F32), pltpu.VMEM((HALO + tm, D_CONV), F32),
                        pltpu.VMEM((HALO + tm, D_RNN), F32), pltpu.VMEM((tm, D_RNN), F32),
                        pltpu.VMEM((tm, D_RNN), F32), pltpu.VMEM((8, D_RNN), F32)],
        compiler_params=_params("arbitrary"),
    )(p, *ws)


_MIX_G = (("wp", (D_POOL, D_POOL)), ("psc", (1, D_POOL)), ("dww", (32, D_CONV)), ("dwb", (1, D_CONV)),
          ("lng", (1, D_CONV)), ("lnb", (1, D_CONV)), ("wpw", (D_CONV, D_CONV)), ("cw", (8, D_RNN)),
          ("cb", (1, D_RNN)), ("wa", (D_RNN, D_RNN)), ("ba", (1, D_RNN)), ("wx", (D_RNN, D_RNN)),
          ("bx", (1, D_RNN)), ("lam", (1, D_RNN)))


def _mixer_bwd(p, dy, hs, mw, deps=()):
    t = p.shape[0]
    tm = _row_tile(t, TM_MIX)
    nt = t // tm
    hb = tm // HALO

    def body(p_ref, ph_ref, dy_ref, hs_ref, hsh_ref, wp, psc, dww, dwb, lng, lnb, wpw, cw, cb, wa, ba, wx, bx, lam,
             dp_ref, g_wp, g_psc, g_dww, g_dwb, g_lng, g_lnb, g_wpw, g_cw, g_cb, g_wa, g_ba, g_wx, g_bx, g_lam,
             ext_q, ext_u, ext_x, ext_h, ee, dc_s, dx_s, a_s, b_s, g_s, gcar):
        step = pl.program_id(0)
        i = nt - 1 - step
        grads = (g_wp, g_psc, g_dww, g_dwb, g_lng, g_lnb, g_wpw, g_cw, g_cb, g_wa, g_ba, g_wx, g_bx, g_lam)

        @pl.when(step == 0)
        def _():
            for gr in grads:
                gr[...] = jnp.zeros(gr.shape, F32)
            ee[pl.ds(tm, HALO), :] = jnp.zeros((HALO, D_POOL), F32)
            dc_s[pl.ds(tm, HALO), :] = jnp.zeros((HALO, D_CONV), F32)
            dx_s[pl.ds(tm, HALO), :] = jnp.zeros((HALO, D_RNN), F32)
            gcar[...] = jnp.zeros((8, D_RNN), F32)

        hm = jnp.where(i == 0, 0.0, 1.0)

        ext_q[0:HALO, :] = ph_ref[:, 0:256] * hm
        ext_q[pl.ds(HALO, tm), :] = p_ref[:, 0:256]
        pooled, lane, cnt = _pool_fwd(ext_q, tm, i * tm)
        pooled_b = pooled.astype(BF16)
        dya = dy_ref[:, 0:256]
        g_psc[...] += _colsum(dya * _dot(pooled_b, wp[...]))
        dmixed_b = (dya * psc[...]).astype(BF16)
        dpooled = _dot_nt(dmixed_b, wp[...])
        g_wp[...] += _dot_tn(pooled_b, dmixed_b)
        ee[0:tm, :] = dpooled / cnt
        dp_ref[:, 0:256] = _lane_sel(lane, *_window_sums(ee, 0, tm, 1)) - dpooled
        ee[pl.ds(tm, HALO), :] = ee[0:HALO, :]

        v = p_ref[:, 256:512]
        s = _sig(p_ref[:, 512:768])
        ext_u[0:HALO, :] = ph_ref[:, 256:512] * _sig(ph_ref[:, 512:768]) * hm
        ext_u[pl.ds(HALO, tm), :] = v * s
        z, rstd, l, sl, act = _conv_ln_fwd(ext_u, dww, dwb[...], lng[...], lnb[...], tm)
        dyb_b = dy_ref[:, 256:512].astype(BF16)
        dact = _dot_nt(dyb_b, wpw[...])
        g_wpw[...] += _dot_tn(act.astype(BF16), dyb_b)
        dl = dact * (sl * (1.0 + l * (1.0 - sl)))
        g_lng[...] += _colsum(dl * z)
        g_lnb[...] += _colsum(dl)
        dz = dl * lng[...]
        dc = rstd * (dz - jnp.mean(dz, axis=-1, keepdims=True) - z * jnp.mean(dz * z, axis=-1, keepdims=True))
        g_dwb[...] += _colsum(dc)
        dc_s[0:tm, :] = dc
        for k in range(CONV_K):
            g_dww[k:k + 1, :] += _colsum(dc * ext_u[pl.ds(HALO - (CONV_K - 1) + k, tm), :])
        du0 = dww[CONV_K - 1:CONV_K, :] * dc
        for j in range(1, CONV_K):
            du0 = du0 + dww[CONV_K - 1 - j:CONV_K - j, :] * dc_s[pl.ds(j, tm), :]
        dp_ref[:, 256:512] = du0 * s
        dp_ref[:, 512:768] = du0 * v * (s * (1.0 - s))
        dc_s[pl.ds(tm, HALO), :] = dc_s[0:HALO, :]

        ext_x[0:HALO, :] = ph_ref[:, 1280:1792] * hm
        ext_x[pl.ds(HALO, tm), :] = p_ref[:, 1280:1792]
        xc, xcb, r, ig, sp, a, m = _rg_fwd(ext_x, cw, cb[...], wa[...], ba[...], wx[...], bx[...], lam[...], tm)
        ext_h[0:HALO, :] = hsh_ref[...] * hm
        ext_h[pl.ds(HALO, tm), :] = hs_ref[...]
        dyc = dy_ref[:, 512:1024]
        gl, dgl = _gelu_and_grad(p_ref[:, 768:1280])
        dp_ref[:, 768:1280] = dyc * hs_ref[...] * dgl
        a_s[...] = a
        b_s[...] = dyc * gl
        _scan_rows(a_s, b_s, g_s, gcar, tm, reverse=True)
        g = g_s[...]
        da = g * ext_h[pl.ds(HALO - 1, tm), :]
        dm = g * (ig * xc)
        dig = g * (m * xc)
        dlog_a = da * a - dm * (a * a) / m
        g_lam[...] += _colsum(dlog_a * (-RG_C * r)) * (-_sig(-lam[...]))
        dra = (dlog_a * (-RG_C * sp)) * (r * (1.0 - r))
        dia = dig * (ig * (1.0 - ig))
        g_ba[...] += _colsum(dra)
        g_bx[...] += _colsum(dia)
        dra_b = dra.astype(BF16)
        dia_b = dia.astype(BF16)
        dxc = g * (m * ig) + _dot_nt(dra_b, wa[...]) + _dot_nt(dia_b, wx[...])
        g_wa[...] += _dot_tn(xcb, dra_b)
        g_wx[...] += _dot_tn(xcb, dia_b)
        g_cb[...] += _colsum(dxc)
        dx_s[0:tm, :] = dxc
        for k in range(RG_CONV_K):
            g_cw[k:k + 1, :] += _colsum(dxc * ext_x[pl.ds(HALO - (RG_CONV_K - 1) + k, tm), :])
        dxin = cw[RG_CONV_K - 1:RG_CONV_K, :] * dxc
        for j in range(1, RG_CONV_K):
            dxin = dxin + cw[RG_CONV_K - 1 - j:RG_CONV_K - j, :] * dx_s[pl.ds(j, tm), :]
        dp_ref[:, 1280:1792] = dxin
        dx_s[pl.ds(tm, HALO), :] = dx_s[0:HALO, :]

    ws = [mw[k] for k in _MIX_W]
    tile = lambda w: pl.BlockSpec((tm, w), lambda s: (nt - 1 - s, 0))
    halo = lambda w: pl.BlockSpec((HALO, w), lambda s: (jnp.maximum((nt - 1 - s) * hb - 1, 0), 0))
    outs = pl.pallas_call(
        _after(body, 5 + len(ws), deps), name="mixer_bwd", grid=(nt,),
        in_specs=[tile(D_IN), halo(D_IN), tile(D_MODEL), tile(D_RNN), halo(D_RNN)] + [_full(w) for w in ws] + [ANY] * len(deps),
        out_specs=[tile(D_IN)] + [pl.BlockSpec(shp, lambda s: (0, 0)) for _, shp in _MIX_G],
        out_shape=[jax.ShapeDtypeStruct((t, D_IN), F32)] + [jax.ShapeDtypeStruct(shp, F32) for _, shp in _MIX_G],
        scratch_shapes=[pltpu.VMEM((HALO + tm, D_POOL), F32), pltpu.VMEM((HALO + tm, D_CONV), F32),
                        pltpu.VMEM((HALO + tm, D_RNN), F32), pltpu.VMEM((HALO + tm, D_RNN), F32),
                        pltpu.VMEM((tm + HALO, D_POOL), F32), pltpu.VMEM((tm + HALO, D_CONV), F32),
                        pltpu.VMEM((tm + HALO, D_RNN), F32), pltpu.VMEM((tm, D_RNN), F32),
                        pltpu.VMEM((tm, D_RNN), F32), pltpu.VMEM((tm, D_RNN), F32), pltpu.VMEM((8, D_RNN), F32)],
        compiler_params=_params("arbitrary"),
    )(p, p, dy, hs, hs, *ws, *deps)
    return outs[0], {k: o for (k, _), o in zip(_MIX_G, outs[1:])}


def _in_proj(h, g, w):
    t = h.shape[0]
    tm = _row_tile(t, TM_MAT)

    def body(h_ref, g_ref, w_ref, p_ref, u_ref):
        u = (_rms(h_ref[...])[1] * g_ref[...]).astype(BF16)
        u_ref[...] = u
        p_ref[...] = _dot(u, w_ref[...])

    return pl.pallas_call(
        body, name="in_proj", grid=(t // tm,),
        in_specs=[pl.BlockSpec((tm, D_MODEL), lambda i: (i, 0)), _full(g), _full(w)],
        out_specs=[pl.BlockSpec((tm, D_IN), lambda i: (i, 0)), pl.BlockSpec((tm, D_MODEL), lambda i: (i, 0))],
        out_shape=[jax.ShapeDtypeStruct((t, D_IN), F32), jax.ShapeDtypeStruct((t, D_MODEL), BF16)],
        compiler_params=_params("parallel"),
    )(h, g, w)


def _mid_fwd(y, h0, w_out, g, w_up):
    t = h0.shape[0]
    tm = _row_tile(t, TM_MAT)
    nj = D_FF // FF_CHUNK

    def body(y_ref, h0_ref, wo_ref, g_ref, wu_ref, h1_ref, u2_ref, f_ref):
        @pl.when(pl.program_id(1) == 0)
        def _():
            h1 = h0_ref[...] + _dot(y_ref[...], wo_ref[...])
            h1_ref[...] = h1
            u2_ref[...] = (_rms(h1)[1] * g_ref[...]).astype(BF16)

        f_ref[...] = _dot(u2_ref[...], wu_ref[...]).astype(BF16)

    row = lambda w: pl.BlockSpec((tm, w), lambda i, j: (i, 0))
    return pl.pallas_call(
        body, name="mid_fwd", grid=(t // tm, nj),
        in_specs=[row(D_MODEL), row(D_MODEL), _full(w_out), _full(g),
                  pl.BlockSpec((None, D_MODEL, FF_CHUNK), lambda i, j: (j, 0, 0))],
        out_specs=[row(D_MODEL), row(D_MODEL), pl.BlockSpec((tm, FF_CHUNK), lambda i, j: (i, j))],
        out_shape=[jax.ShapeDtypeStruct((t, D_MODEL), F32), jax.ShapeDtypeStruct((t, D_MODEL), BF16),
                   jax.ShapeDtypeStruct((t, D_FF), BF16)],
        compiler_params=_params("parallel", "arbitrary"),
    )(y, h0, w_out, g, w_up)


def _down_fwd(f, h1, w_down):
    t = h1.shape[0]
    tm = _row_tile(t, TM_MAT)

    def body(f_ref, h1_ref, wd_ref, h2_ref):
        acc = h1_ref[...]
        for c in range(D_FF // FF_CHUNK):
            cols = slice(c * FF_CHUNK, (c + 1) * FF_CHUNK)
            a = jnp.square(jnp.maximum(f_ref[:, cols].astype(F32), 0.0)).astype(BF16)
            acc = acc + _dot(a, wd_ref[cols, :])
        h2_ref[...] = acc

    return pl.pallas_call(
        body, name="down_fwd", grid=(t // tm,),
        in_specs=[pl.BlockSpec((tm, D_FF), lambda i: (i, 0)), pl.BlockSpec((tm, D_MODEL), lambda i: (i, 0)), _full(w_down)],
        out_specs=pl.BlockSpec((tm, D_MODEL), lambda i: (i, 0)),
        out_shape=jax.ShapeDtypeStruct((t, D_MODEL), F32),
        compiler_params=_params("parallel"),
    )(f, h1, w_down)


def _loss_head(h, g, tgt, t_real):
    t = h.shape[0]
    tm = _row_tile(t, TM_MAT)

    def body(h_ref, g_ref, tgt_ref, loss_ref, dh_ref, dg_ref):
        i = pl.program_id(0)

        @pl.when(i == 0)
        def _():
            loss_ref[...] = jnp.zeros(loss_ref.shape, F32)
            dg_ref[...] = jnp.zeros(dg_ref.shape, F32)

        r, n = _rms(h_ref[...])
        row = lax.broadcasted_iota(jnp.int32, (tm, 1), 0) + i * tm
        valid = jnp.logical_and(row >= N_META, row < t_real)
        diff = jnp.where(valid, n * g_ref[...] - tgt_ref[...], 0.0)
        loss_ref[...] += 0.5 * jnp.sum(jnp.mean(diff * diff, axis=-1, keepdims=True))
        dy = diff * (1.0 / D_MODEL)
        dg_ref[...] += _colsum(dy * n)
        dh_ref[...] = _rms_bwd(dy, n, r, g_ref[...])

    return pl.pallas_call(
        body, name="loss_head", grid=(t // tm,),
        in_specs=[pl.BlockSpec((tm, D_MODEL), lambda i: (i, 0)), _full(g), pl.BlockSpec((tm, D_MODEL), lambda i: (i, 0))],
        out_specs=[pl.BlockSpec((8, 128), lambda i: (0, 0)), pl.BlockSpec((tm, D_MODEL), lambda i: (i, 0)),
                   pl.BlockSpec((1, D_MODEL), lambda i: (0, 0))],
        out_shape=[jax.ShapeDtypeStruct((8, 128), F32), jax.ShapeDtypeStruct((t, D_MODEL), F32),
                   jax.ShapeDtypeStruct((1, D_MODEL), F32)],
        compiler_params=_params("arbitrary"),
    )(h, g, tgt)


def _mlp_bwd(dh2, f, h1, g, w_up, w_down, deps=()):
    t = dh2.shape[0]
    tm = _row_tile(t, TM_MAT)
    nj = D_FF // FF_CHUNK

    def body(dh2_ref, f_ref, wd_ref, wu_ref, h1_ref, g_ref, df_ref, dh1_ref, dg_ref, acc, dhb):
        i, j = pl.program_id(0), pl.program_id(1)

        @pl.when(j == 0)
        def _():
            dhb[...] = dh2_ref[...].astype(BF16)
            acc[...] = jnp.zeros(acc.shape, F32)

        @pl.when(jnp.logical_and(i == 0, j == 0))
        def _():
            dg_ref[...] = jnp.zeros(dg_ref.shape, F32)

        dact = _dot_nt(dhb[...], wd_ref[...])
        df = (dact * (2.0 * jnp.maximum(f_ref[...].astype(F32), 0.0))).astype(BF16)
        df_ref[...] = df
        acc[...] += _dot_nt(df, wu_ref[...])

        @pl.when(j == nj - 1)
        def _():
            r, n = _rms(h1_ref[...])
            du2 = acc[...]
            dg_ref[...] += _colsum(du2 * n)
            dh1_ref[...] = dh2_ref[...] + _rms_bwd(du2, n, r, g_ref[...])

    row = lambda w: pl.BlockSpec((tm, w), lambda i, j: (i, 0))
    return pl.pallas_call(
        _after(body, 6, deps), name="mlp_bwd", grid=(t // tm, nj),
        in_specs=[row(D_MODEL), pl.BlockSpec((tm, FF_CHUNK), lambda i, j: (i, j)),
                  pl.BlockSpec((None, FF_CHUNK, D_MODEL), lambda i, j: (j, 0, 0)),
                  pl.BlockSpec((None, D_MODEL, FF_CHUNK), lambda i, j: (j, 0, 0)), row(D_MODEL), _full(g)] + [ANY] * len(deps),
        out_specs=[pl.BlockSpec((tm, FF_CHUNK), lambda i, j: (i, j)), row(D_MODEL),
                   pl.BlockSpec((1, D_MODEL), lambda i, j: (0, 0))],
        out_shape=[jax.ShapeDtypeStruct((t, D_FF), BF16), jax.ShapeDtypeStruct((t, D_MODEL), F32),
                   jax.ShapeDtypeStruct((1, D_MODEL), F32)],
        scratch_shapes=[pltpu.VMEM((tm, D_MODEL), F32), pltpu.VMEM((tm, D_MODEL), BF16)],
        compiler_params=_params("arbitrary", "arbitrary"),
    )(dh2, f, w_down, w_up, h1, g, *deps)


def _out_bwd(dh1, w_out, deps=()):
    t = dh1.shape[0]
    tm = _row_tile(t, TM_MAT)

    def body(dh_ref, w_ref, dy_ref):
        dy_ref[...] = _dot_nt(dh_ref[...].astype(BF16), w_ref[...])

    return pl.pallas_call(
        _after(body, 2, deps), name="out_bwd", grid=(t // tm,),
        in_specs=[pl.BlockSpec((tm, D_MODEL), lambda i: (i, 0)), _full(w_out)] + [ANY] * len(deps),
        out_specs=pl.BlockSpec((tm, D_MODEL), lambda i: (i, 0)),
        out_shape=jax.ShapeDtypeStruct((t, D_MODEL), F32),
        compiler_params=_params("parallel"),
    )(dh1, w_out, *deps)


def _in_bwd(dp, dh1, h0, g, w_in, deps=()):
    t = dp.shape[0]
    tm = _row_tile(t, TM_MAT)

    def body(dp_ref, dh1_ref, h0_ref, g_ref, w_ref, dh0_ref, dg_ref):
        @pl.when(pl.program_id(0) == 0)
        def _():
            dg_ref[...] = jnp.zeros(dg_ref.shape, F32)

        du = _dot_nt(dp_ref[...].astype(BF16), w_ref[...])
        r, n = _rms(h0_ref[...])
        dg_ref[...] += _colsum(du * n)
        dh0_ref[...] = dh1_ref[...] + _rms_bwd(du, n, r, g_ref[...])

    row = lambda w: pl.BlockSpec((tm, w), lambda i: (i, 0))
    return pl.pallas_call(
        _after(body, 5, deps), name="in_bwd", grid=(t // tm,),
        in_specs=[row(D_IN), row(D_MODEL), row(D_MODEL), _full(g), _full(w_in)] + [ANY] * len(deps),
        out_specs=[row(D_MODEL), pl.BlockSpec((1, D_MODEL), lambda i: (0, 0))],
        out_shape=[jax.ShapeDtypeStruct((t, D_MODEL), F32), jax.ShapeDtypeStruct((1, D_MODEL), F32)],
        compiler_params=_params("arbitrary"),
    )(dp, dh1, h0, g, w_in, *deps)


def _tn_matmul(a, b, kc, nc, relu2, name, deps=()):
    t, k = a.shape
    n = b.shape[1]
    tt = _row_tile(t, TM_MAT)
    gk, gn = k // kc, n // nc

    def body(a_ref, b_ref, o_ref):
        @pl.when(pl.program_id(2) == 0)
        def _():
            o_ref[...] = jnp.zeros(o_ref.shape, F32)

        av = a_ref[...]
        if relu2:
            av = jnp.square(jnp.maximum(av.astype(F32), 0.0))
        o_ref[...] += _dot_tn(av.astype(BF16), b_ref[...].astype(BF16))

    return pl.pallas_call(
        _after(body, 2, deps), name=name, grid=(gk, gn, t // tt),
        in_specs=[pl.BlockSpec((tt, kc), lambda ik, jn, it: (it, ik)), pl.BlockSpec((tt, nc), lambda ik, jn, it: (it, jn))]
        + [ANY] * len(deps),
        out_specs=pl.BlockSpec((None, kc, nc), lambda ik, jn, it: (ik * gn + jn, 0, 0)),
        out_shape=jax.ShapeDtypeStruct((gk * gn, kc, nc), F32),
        compiler_params=_params("parallel", "parallel", "arbitrary"),
    )(a, b, *deps)


def _block_diag(blocks):
    nb, hd, _ = blocks.shape
    eye = jnp.eye(nb, dtype=blocks.dtype)
    return (blocks[:, :, None, :] * eye[:, None, :, None]).reshape(nb * hd, nb * hd)


def _diag_blocks(m, nb):
    hd = m.shape[0] // nb
    return jnp.stack([m[b * hd:(b + 1) * hd, b * hd:(b + 1) * hd] for b in range(nb)])


def _mixer_weights(w, l):
    row = lambda a: a.reshape(1, -1)
    return dict(
        wp=_block_diag(w["pool_w"][l]).astype(BF16), psc=row(w["pool_scale"][l]),
        dww=jnp.pad(w["convb_dw_w"][l], ((0, 32 - CONV_K), (0, 0))), dwb=row(w["convb_dw_b"][l]),
        lng=row(w["convb_ln_g"][l]), lnb=row(w["convb_ln_b"][l]), wpw=w["convb_pw_w"][l].astype(BF16),
        cw=jnp.pad(w["rg_conv_w"][l], ((0, 8 - RG_CONV_K), (0, 0))), cb=row(w["rg_conv_b"][l]),
        wa=_block_diag(w["rg_w_a"][l]).astype(BF16), ba=row(w["rg_b_a"][l]),
        wx=_block_diag(w["rg_w_x"][l]).astype(BF16), bx=row(w["rg_b_x"][l]), lam=row(w["rg_lambda"][l]))


def _local_step(h, tgt, t_real, w, fetch, hooks):
    depth = 2
    saved = []
    big = []
    for l in range(depth):
        mw = _mixer_weights(w, l)
        g1 = w["mix_norm_g"][l].reshape(1, -1)
        g2 = w["mlp_norm_g"][l].reshape(1, -1)
        wl = dict(w_in=fetch(l, "w_in", h))
        p, u = _in_proj(h, g1, wl["w_in"])
        y, hs = _mixer_fwd(p, mw)
        wl["w_out"], wl["w_up"] = fetch(l, "w_out", y), fetch(l, "w_up", y)
        h1, u2, f = _mid_fwd(y, h, wl["w_out"], g2, wl["w_up"])
        wl["w_down"] = fetch(l, "w_down", f)
        h2 = _down_fwd(f, h1, wl["w_down"].reshape(D_FF, D_MODEL))
        saved.append(dict(mw=mw, g1=g1, g2=g2, h0=h, p=p, u=u, y=y, hs=hs, h1=h1, u2=u2, f=f))
        big.append(wl)
        h = h2
    gf = w["final_norm_g"].reshape(1, -1)
    loss, dh, dgf = _loss_head(h, gf, tgt, t_real)

    gs = {k: [None] * depth for k in ("mix_norm_g", "mlp_norm_g", "pool_w", "pool_scale", "convb_dw_w", "convb_dw_b",
                                      "convb_ln_g", "convb_ln_b", "convb_pw_w", "rg_conv_w", "rg_conv_b", "rg_w_a",
                                      "rg_b_a", "rg_w_x", "rg_b_x", "rg_lambda")}
    deps = ()
    for l in reversed(range(depth)):
        s, wl = saved[l], big[l]
        df, dh1, dg2 = _mlp_bwd(dh, s["f"], s["h1"], s["g2"], wl["w_up"], wl["w_down"], deps)
        deps = hooks.point(l, "mlp_bwd", dh1)
        g_down = _tn_matmul(s["f"], dh, FF_CHUNK, D_MODEL, True, "dw_down", deps)
        hooks.grad(l, "w_down", g_down)
        deps = hooks.point(l, "dw_down", g_down)
        g_up = _tn_matmul(s["u2"], df, D_MODEL, FF_CHUNK, False, "dw_up", deps)
        hooks.grad(l, "w_up", g_up)
        deps = hooks.point(l, "dw_up", g_up)
        dy = _out_bwd(dh1, wl["w_out"], deps)
        deps = hooks.point(l, "out_bwd", dy)
        g_out = _tn_matmul(s["y"], dh1, D_MODEL, D_MODEL, False, "dw_out", deps)
        hooks.grad(l, "w_out", g_out.reshape(N_CHIPS, D_MODEL // N_CHIPS, D_MODEL))
        deps = hooks.point(l, "dw_out", g_out)
        dp, mg = _mixer_bwd(s["p"], dy, s["hs"], s["mw"], deps)
        deps = hooks.point(l, "mixer_bwd", dp)
        g_in = _tn_matmul(s["u"], dp, D_MODEL, D_IN, False, "dw_in", deps)
        g_in = g_in[0].reshape(D_MODEL, N_CHIPS, D_IN // N_CHIPS).transpose(1, 0, 2)
        hooks.grad(l, "w_in", g_in)
        deps = hooks.point(l, "dw_in", g_in)
        dh, dg1 = _in_bwd(dp, dh1, s["h0"], s["g1"], wl["w_in"], deps)
        deps = hooks.point(l, "in_bwd", dh)
        gs["mix_norm_g"][l] = dg1[0]
        gs["mlp_norm_g"][l] = dg2[0]
        gs["pool_w"][l] = _diag_blocks(mg["wp"], D_POOL // POOL_GW)
        gs["pool_scale"][l] = mg["psc"][0]
        gs["convb_dw_w"][l] = mg["dww"][:CONV_K]
        gs["convb_dw_b"][l] = mg["dwb"][0]
        gs["convb_ln_g"][l] = mg["lng"][0]
        gs["convb_ln_b"][l] = mg["lnb"][0]
        gs["convb_pw_w"][l] = mg["wpw"]
        gs["rg_conv_w"][l] = mg["cw"][:RG_CONV_K]
        gs["rg_conv_b"][l] = mg["cb"][0]
        gs["rg_w_a"][l] = _diag_blocks(mg["wa"], D_RNN // RG_HD)
        gs["rg_b_a"][l] = mg["ba"][0]
        gs["rg_w_x"][l] = _diag_blocks(mg["wx"], D_RNN // RG_HD)
        gs["rg_b_x"][l] = mg["bx"][0]
        gs["rg_lambda"][l] = mg["lam"][0]
    gsmall = {k: jnp.stack(v) for k, v in gs.items()}
    gsmall["final_norm_g"] = dgf[0]
    return loss[0, 0], dh, gsmall


def _place():
    return lax.axis_index("x"), lax.axis_index("y"), lax.axis_index("c")


def _other_chips(x, y):
    return [(1 - x, y), (x, 1 - y), (1 - x, 1 - y)]


def _gather_now(srcs):
    ns = len(srcs)
    out_shape = [jax.ShapeDtypeStruct((N_CHIPS,) + s.shape, s.dtype) for s in srcs]

    def body(*refs):
        src_refs, dst_refs = refs[:ns], refs[ns:2 * ns]
        send_sems, recv_sems, loc_sems = refs[2 * ns:]
        x, y, c = _place()
        me = 2 * x + y
        local, remote = [], []
        for n in range(ns):
            cp = pltpu.make_async_copy(src_refs[n], dst_refs[n].at[me], loc_sems.at[n])
            cp.start()
            local.append(cp)
            for j, (px, py) in enumerate(_other_chips(x, y)):
                out = pltpu.make_async_remote_copy(src_refs[n], dst_refs[n].at[me], send_sems.at[3 * n + j],
                                                   recv_sems.at[3 * n + j], device_id=(px, py, c), device_id_type=MESH)
                out.start()
                remote.append(pltpu.make_async_remote_copy(src_refs[n], dst_refs[n].at[2 * px + py], send_sems.at[3 * n + j],
                                                           recv_sems.at[3 * n + j], device_id=(px, py, c), device_id_type=MESH))
        for cp in remote:
            cp.wait()
        for cp in local:
            cp.wait()

    return pl.pallas_call(
        body, name="gather_now", in_specs=[ANY] * ns, out_specs=[ANY] * ns, out_shape=out_shape,
        scratch_shapes=[pltpu.SemaphoreType.DMA((3 * ns,)), pltpu.SemaphoreType.DMA((3 * ns,)), pltpu.SemaphoreType.DMA((ns,))],
    )(*srcs)


HBM_SPEC = pl.BlockSpec(memory_space=pltpu.HBM)
SEM_SPEC = pl.BlockSpec(memory_space=pltpu.SEMAPHORE)
DATAFLOW = pltpu.SideEffectType.DATAFLOW_SIDE_EFFECTING


def _gather_copies(src_refs, land_refs, send_sem, recv_sem, first):
    x, y, c = _place()
    me = 2 * x + y
    out = []
    for n in range(len(src_refs)):
        for j, (px, py) in enumerate(_other_chips(x, y)):
            out.append(pltpu.make_async_remote_copy(src_refs[n], land_refs[n].at[me], send_sem.at[first + 3 * n + j],
                                                    recv_sem.at[first + 3 * n + j], device_id=(px, py, c), device_id_type=MESH))
    return out


def _gather_start(groups, me):
    srcs = [pltpu.with_memory_space_constraint(s, pltpu.HBM) for g in groups for s in g]
    lands = [pltpu.with_memory_space_constraint(
        lax.dynamic_update_slice(jnp.zeros((N_CHIPS,) + s.shape, s.dtype), s[None], (me,) + (0,) * s.ndim), pltpu.HBM)
        for g in groups for s in g]
    n, ng = len(srcs), len(groups)
    first = [sum(len(g) for g in groups[:i]) for i in range(ng)]

    def body(*refs):
        src_refs, land_refs = refs[:n], refs[n:2 * n]
        sems = refs[2 * n:2 * n + 2 * ng]
        token = refs[-1]
        for gi, g in enumerate(groups):
            lo, hi = first[gi], first[gi] + len(g)
            for cp in _gather_copies(src_refs[lo:hi], land_refs[lo:hi], sems[2 * gi], sems[2 * gi + 1], 0):
                cp.start()
        token[...] = jnp.zeros(token.shape, token.dtype)

    sem_shapes = [pltpu.SemaphoreType.DMA((3 * len(g),)) for g in groups for _ in range(2)]
    outs = pl.pallas_call(
        body, name="gather_start",
        out_shape=sem_shapes + [pltpu.HBM(a.shape, a.dtype) for a in srcs + lands] + [jax.ShapeDtypeStruct((8, 128), F32)],
        in_specs=[HBM_SPEC] * (2 * n),
        out_specs=[SEM_SPEC] * (2 * ng) + [HBM_SPEC] * (2 * n) + [pl.BlockSpec(memory_space=pltpu.VMEM)],
        input_output_aliases={i: 2 * ng + i for i in range(2 * n)},
        compiler_params=pltpu.CompilerParams(has_side_effects=DATAFLOW),
    )(*srcs, *lands)
    sems, thru, token = outs[:2 * ng], outs[2 * ng:2 * ng + 2 * n], outs[-1]
    state = []
    for gi, g in enumerate(groups):
        lo, hi = first[gi], first[gi] + len(g)
        state.append((sems[2 * gi], sems[2 * gi + 1], thru[lo:hi], thru[n + lo:n + hi]))
    return state, token


def _gather_wait(state, after, name):
    send_sem, recv_sem, srcs, lands = state
    n = len(srcs)

    def body(*refs):
        src_refs, land_refs = refs[:n], refs[n:2 * n]
        send, recv = refs[2 * n], refs[2 * n + 1]
        for cp in _gather_copies(src_refs, land_refs, send, recv, 0):
            cp.wait_send()
            cp.wait_recv()

    outs = pl.pallas_call(
        body, name=name,
        out_shape=[pltpu.HBM(a.shape, a.dtype) for a in list(srcs) + list(lands)],
        in_specs=[HBM_SPEC] * (2 * n) + [SEM_SPEC, SEM_SPEC, ANY],
        out_specs=[HBM_SPEC] * (2 * n),
        input_output_aliases={i: i for i in range(2 * n)},
        compiler_params=pltpu.CompilerParams(has_side_effects=DATAFLOW),
    )(*srcs, *lands, send_sem, recv_sem, after)
    return outs[n:]


def _add_halves(g, recv, c1):
    nk, r, cd = g.shape
    r2 = r // 2

    def body(c_ref, g_ref, r_ref, pa_ref, pab_ref):
        s = g_ref[...] + r_ref[...]
        pa_ref[...] = s
        pab_ref[...] = s.astype(BF16)

    blk = pl.BlockSpec((None, r2, cd), lambda k, c_ref: (k, 0, 0))
    return pl.pallas_call(
        body, name="rs_add_halves",
        grid_spec=pltpu.PrefetchScalarGridSpec(
            num_scalar_prefetch=1, grid=(nk,),
            in_specs=[pl.BlockSpec((None, r2, cd), lambda k, c_ref: (k, c_ref[0], 0)), blk], out_specs=[blk, blk]),
        out_shape=[jax.ShapeDtypeStruct((nk, r2, cd), F32), jax.ShapeDtypeStruct((nk, r2, cd), BF16)],
        compiler_params=_params("parallel"),
    )(c1, g, recv)


def _sum_partials(pa, recv, me1):
    nk, r2, cd = pa.shape

    def body(me_ref, pa_ref, r_ref, s_ref):
        s_ref[...] = ((pa_ref[...] + r_ref[0].astype(F32)) + r_ref[1].astype(F32)) + r_ref[2].astype(F32)

    return pl.pallas_call(
        body, name="rs_sum_partials",
        grid_spec=pltpu.PrefetchScalarGridSpec(
            num_scalar_prefetch=1, grid=(1,),
            in_specs=[pl.BlockSpec((None, r2, cd), lambda i, me_ref: (me_ref[0], 0, 0)),
                      pl.BlockSpec((3, r2, cd), lambda i, me_ref: (0, 0, 0))],
            out_specs=pl.BlockSpec((r2, cd), lambda i, me_ref: (0, 0))),
        out_shape=jax.ShapeDtypeStruct((r2, cd), F32),
        compiler_params=_params("arbitrary"),
    )(me1, pa, recv)


def _split_start(name, srcs, lands, ncopies, make_copies):
    srcs = [pltpu.with_memory_space_constraint(s, pltpu.HBM) for s in srcs]
    lands = [pltpu.with_memory_space_constraint(a, pltpu.HBM) for a in lands]
    n, m = len(srcs), len(lands)

    def body(*refs):
        src_refs, land_refs = refs[:n], refs[n:n + m]
        send, recv, token = refs[n + m], refs[n + m + 1], refs[-1]
        for cp in make_copies(src_refs, land_refs, send, recv):
            cp.start()
        token[...] = jnp.zeros(token.shape, token.dtype)

    outs = pl.pallas_call(
        body, name=name,
        out_shape=[pltpu.SemaphoreType.DMA((ncopies,)), pltpu.SemaphoreType.DMA((ncopies,))]
        + [pltpu.HBM(a.shape, a.dtype) for a in srcs + lands] + [jax.ShapeDtypeStruct((8, 128), F32)],
        in_specs=[HBM_SPEC] * (n + m),
        out_specs=[SEM_SPEC, SEM_SPEC] + [HBM_SPEC] * (n + m) + [pl.BlockSpec(memory_space=pltpu.VMEM)],
        input_output_aliases={i: 2 + i for i in range(n + m)},
        compiler_params=pltpu.CompilerParams(has_side_effects=DATAFLOW),
    )(*srcs, *lands)
    return (outs[0], outs[1], outs[2:2 + n], outs[2 + n:2 + n + m], make_copies), outs[-1]


def _split_wait(name, state, after):
    send_sem, recv_sem, srcs, lands, make_copies = state
    n, m = len(srcs), len(lands)

    def body(*refs):
        src_refs, land_refs = refs[:n], refs[n:n + m]
        for cp in make_copies(src_refs, land_refs, refs[n + m], refs[n + m + 1]):
            cp.wait_send()
            cp.wait_recv()

    outs = pl.pallas_call(
        body, name=name,
        out_shape=[pltpu.HBM(a.shape, a.dtype) for a in list(srcs) + list(lands)],
        in_specs=[HBM_SPEC] * (n + m) + [SEM_SPEC, SEM_SPEC, ANY],
        out_specs=[HBM_SPEC] * (n + m),
        input_output_aliases={i: i for i in range(n + m)},
        compiler_params=pltpu.CompilerParams(has_side_effects=DATAFLOW),
    )(*srcs, *lands, send_sem, recv_sem, after)
    return outs[n:]


def _copies_to_sibling(src_of):
    def make(src_refs, land_refs, send, recv):
        x, y, c = _place()
        return [pltpu.make_async_remote_copy(src_of(src_refs[i], c), land_refs[i], send.at[i], recv.at[i],
                                             device_id=(x, y, 1 - c), device_id_type=MESH) for i in range(len(src_refs))]
    return make


def _copies_to_chips(src_refs, land_refs, send, recv):
    x, y, c = _place()
    return [pltpu.make_async_remote_copy(src_refs[i].at[2 * px + py], land_refs[i].at[j], send.at[3 * i + j], recv.at[3 * i + j],
                                         device_id=(px, py, c), device_id_type=MESH)
            for i in range(len(src_refs)) for j, (px, py) in enumerate(_other_chips(x, y))]


def _other_half_rows(ref, c):
    r2 = ref.shape[1] // 2
    return ref.at[:, pl.ds(pl.multiple_of((1 - c) * r2, 8), r2)]


class _ReduceScatter:
    def __init__(self, tag, grads, c1, me1):
        self.tag, self.grads, self.c1, self.me1 = tag, grads, c1, me1

    def start(self):
        lands = [lax.empty((g.shape[0], g.shape[1] // 2, g.shape[2]), F32) for g in self.grads]
        self.state, token = _split_start("rs_%s_a_start" % self.tag, self.grads, lands, len(self.grads),
                                         _copies_to_sibling(_other_half_rows))
        return token

    def to_chips(self, after):
        recv = _split_wait("rs_%s_a_wait" % self.tag, self.state, after)
        added = [_add_halves(g, r, self.c1) for g, r in zip(self.grads, recv)]
        self.own = [pa for pa, _ in added]
        pabs = [pab for _, pab in added]
        lands = [lax.empty((3,) + p.shape[1:], BF16) for p in pabs]
        self.state, token = _split_start("rs_%s_b_start" % self.tag, pabs, lands, 3 * len(pabs), _copies_to_chips)
        return token

    def to_sibling(self, after):
        recv = _split_wait("rs_%s_b_wait" % self.tag, self.state, after)
        self.sums = [_sum_partials(pa, rb, self.me1) for pa, rb in zip(self.own, recv)]
        lands = [lax.empty(s.shape, F32) for s in self.sums]
        self.state, token = _split_start("rs_%s_c_start" % self.tag, self.sums, lands, len(self.sums),
                                         _copies_to_sibling(lambda ref, c: ref))
        return token

    def finish(self, after):
        return list(zip(self.sums, _split_wait("rs_%s_c_wait" % self.tag, self.state, after)))


def _allreduce_small(v):
    def body(v_ref, out_ref, rbuf, send_sems, recv_sems):
        x, y, c = _place()
        out_ref[...] = v_ref[...]
        for s, peer in enumerate([(x, y, 1 - c), (1 - x, y, c), (x, 1 - y, c)]):
            cp = pltpu.make_async_remote_copy(out_ref, rbuf.at[s], send_sems.at[s], recv_sems.at[s],
                                              device_id=peer, device_id_type=MESH)
            cp.start()
            cp.wait()
            out_ref[...] = out_ref[...] + rbuf[s]

    vm = pl.BlockSpec(memory_space=pltpu.VMEM)
    return pl.pallas_call(
        body, name="allreduce_small", in_specs=[vm], out_specs=vm, out_shape=jax.ShapeDtypeStruct(v.shape, v.dtype),
        scratch_shapes=[pltpu.VMEM((3,) + v.shape, v.dtype), pltpu.SemaphoreType.DMA((3,)), pltpu.SemaphoreType.DMA((3,))],
        compiler_params=pltpu.CompilerParams(vmem_limit_bytes=VMEM_LIMIT),
    )(v)


def _adamw_math(w, g, m, v):
    m = ADAM_B1 * m + (1.0 - ADAM_B1) * g
    v = ADAM_B2 * v + (1.0 - ADAM_B2) * jnp.square(g)
    m_hat = m / (1.0 - ADAM_B1 ** ADAM_STEP)
    v_hat = v / (1.0 - ADAM_B2 ** ADAM_STEP)
    return -ADAM_LR * (m_hat / (jnp.sqrt(v_hat) + ADAM_EPS) + ADAM_WD * w), m, v


def _adamw_big_layer(layer, w, m, v, own, sib, c1, prev):
    _, r, cd = w.shape
    r2 = r // 2

    def body(c_ref, w_ref, m_ref, v_ref, own_ref, sib_ref, *rest):
        g_ref, d_ref, mo_ref, vo_ref, token = rest[-5:]
        g = jnp.where(pl.program_id(0) == c_ref[0], own_ref[...], sib_ref[...])
        g_ref[...] = g
        d_ref[...], mo_ref[...], vo_ref[...] = _adamw_math(w_ref[...], g, m_ref[...], v_ref[...])
        token[...] = jnp.zeros(token.shape, F32)

    blk = pl.BlockSpec((None, r2, cd), lambda hh, c_ref: (layer, hh, 0))
    half = pl.BlockSpec((r2, cd), lambda hh, c_ref: (0, 0))
    prev = () if prev is None else tuple(prev)
    outs = pl.pallas_call(
        body, name="adamw_big",
        grid_spec=pltpu.PrefetchScalarGridSpec(
            num_scalar_prefetch=1, grid=(2,), in_specs=[blk, blk, blk, half, half] + [ANY] * len(prev),
            out_specs=[blk] * 4 + [pl.BlockSpec((8, 128), lambda hh, c_ref: (0, 0))]),
        out_shape=[jax.ShapeDtypeStruct(w.shape, F32)] * 4 + [jax.ShapeDtypeStruct((8, 128), F32)],
        input_output_aliases={6 + i: i for i in range(len(prev))},
        compiler_params=_params("arbitrary"),
    )(c1, w, m, v, own, sib, *prev)
    return outs[:4], outs[4]


def _adamw_small(ws, ms, vs, gpack, offsets, strides, me1):
    n = len(ws)

    def body(me_ref, *refs):
        w_refs, m_refs, v_refs = refs[:n], refs[n:2 * n], refs[2 * n:3 * n]
        g_ref = refs[3 * n]
        outs = refs[3 * n + 1:]
        for i in range(n):
            rows = w_refs[i].shape[0]
            if strides[i]:
                g = g_ref[pl.ds(pl.multiple_of(offsets[i] + me_ref[0] * strides[i], 8), rows), :]
            else:
                g = g_ref[offsets[i]:offsets[i] + rows, :]
            d, m, v = _adamw_math(w_refs[i][...], g, m_refs[i][...], v_refs[i][...])
            outs[4 * i][...] = g
            outs[4 * i + 1][...] = d
            outs[4 * i + 2][...] = m
            outs[4 * i + 3][...] = v

    full = lambda a: pl.BlockSpec(a.shape, lambda i, me_ref: (0, 0))
    ins = list(ws) + list(ms) + list(vs) + [gpack]
    out_arrs = [w for w in ws for _ in range(4)]
    outs = pl.pallas_call(
        body, name="adamw_small",
        grid_spec=pltpu.PrefetchScalarGridSpec(num_scalar_prefetch=1, grid=(1,), in_specs=[full(a) for a in ins],
                                               out_specs=[full(a) for a in out_arrs]),
        out_shape=[jax.ShapeDtypeStruct(a.shape, F32) for a in out_arrs],
        compiler_params=_params("arbitrary"),
    )(me1, *ins)
    return [outs[4 * i:4 * i + 4] for i in range(n)]


LANES = 128
SUBLANES = 8


def _rows_of(size):
    return -(-size // (LANES * SUBLANES)) * SUBLANES


def _as_rows(a, rows=None):
    flat = a.reshape(-1)
    rows = _rows_of(flat.size) if rows is None else rows
    return jnp.pad(flat, (0, rows * LANES - flat.size)).reshape(rows, LANES)


def _to_shard_major(name, full):
    if name == "meta_tokens":
        return full.reshape(N_META, N_CHIPS, -1).transpose(1, 0, 2)
    if name == "convb_pw_w":
        return full.reshape(2, N_CHIPS, -1, D_CONV).transpose(1, 0, 2, 3)
    return full.reshape(full.shape[0], full.shape[1], N_CHIPS, -1).transpose(2, 0, 1, 3)


class _GradientSchedule:
    GROUPS = {"l1": [(1, "w_down"), (1, "w_up"), (1, "w_out"), (1, "w_in")], "a0": [(0, "w_down"), (0, "w_up")],
              "b0": [(0, "w_out")], "c0": [(0, "w_in")]}
    PLAN = {
        (1, "dw_in"): [("l1", "start")],
        (1, "in_bwd"): [("l1", "to_chips")],
        (0, "mlp_bwd"): [("l1", "to_sibling")],
        (0, "dw_down"): [("l1", "finish")],
        (0, "dw_up"): [("a0", "start")],
        (0, "out_bwd"): [("a0", "to_chips")],
        (0, "dw_out"): [("b0", "start")],
        (0, "mixer_bwd"): [("a0", "to_sibling"), ("b0", "to_chips")],
        (0, "dw_in"): [("c0", "start"), ("a0", "finish"), ("b0", "to_sibling")],
        (0, "in_bwd"): [("c0", "to_chips"), ("b0", "finish")],
    }

    def __init__(self, w, mom, var, c1, me1):
        self.w, self.mom, self.var, self.c1, self.me1 = w, mom, var, c1, me1
        self.grads, self.chains, self.out = {}, {}, {}

    def grad(self, layer, name, g):
        self.grads[layer, name] = g

    def point(self, layer, kernel_name, after):
        return self.run(self.PLAN.get((layer, kernel_name), ()), after)

    def run(self, actions, after):
        deps = []
        for tag, stage in actions:
            if stage == "start":
                self.chains[tag] = _ReduceScatter(tag, [self.grads[lk] for lk in self.GROUPS[tag]], self.c1, self.me1)
                deps.append(self.chains[tag].start())
            elif stage == "finish":
                for (layer, k), (own, sib) in zip(self.GROUPS[tag], self.chains[tag].finish(after)):
                    self.out[k], token = _adamw_big_layer(layer, self.w[k], self.mom[k], self.var[k], own, sib, self.c1,
                                                          self.out.get(k))
                    deps.append(token)
            else:
                deps.append(getattr(self.chains[tag], stage)(after))
        return tuple(deps)


def _from_shard_major(name, sm):
    if name == "meta_tokens":
        return sm.transpose(1, 0, 2).reshape(N_META, -1)
    if name == "convb_pw_w":
        return sm.transpose(1, 0, 2, 3).reshape(2, -1, D_CONV)
    return sm.transpose(1, 2, 0, 3).reshape(sm.shape[1], sm.shape[2], -1)


def kernel(x, meta_tokens, mix_norm_g, w_in, pool_w, pool_scale, convb_dw_w, convb_dw_b, convb_ln_g, convb_ln_b, convb_pw_w, rg_conv_w, rg_conv_b, rg_w_a, rg_b_a, rg_w_x, rg_b_x, rg_lambda, w_out, mlp_norm_g, w_up, w_down, final_norm_g, loss_target, m_meta_tokens, m_mix_norm_g, m_w_in, m_pool_w, m_pool_scale, m_convb_dw_w, m_convb_dw_b, m_convb_ln_g, m_convb_ln_b, m_convb_pw_w, m_rg_conv_w, m_rg_conv_b, m_rg_w_a, m_rg_b_a, m_rg_w_x, m_rg_b_x, m_rg_lambda, m_w_out, m_mlp_norm_g, m_w_up, m_w_down, m_final_norm_g, v_meta_tokens, v_mix_norm_g, v_w_in, v_pool_w, v_pool_scale, v_convb_dw_w, v_convb_dw_b, v_convb_ln_g, v_convb_ln_b, v_convb_pw_w, v_rg_conv_w, v_rg_conv_b, v_rg_w_a, v_rg_b_a, v_rg_w_x, v_rg_b_x, v_rg_lambda, v_w_out, v_mlp_norm_g, v_w_up, v_w_down, v_final_norm_g):
    given = dict(locals())
    w = {k: given[k] for k in WEIGHTS}
    mom = {k: given["m_" + k] for k in WEIGHTS}
    var = {k: given["v_" + k] for k in WEIGHTS}
    xi, yi, ci = _place()
    me1 = (2 * xi + yi).astype(jnp.int32).reshape(1)
    c1 = ci.astype(jnp.int32).reshape(1)

    small_rows = [_rows_of(w[k].size) for k in SMALL_SHARDED]
    small_pack = jnp.concatenate([_as_rows(w[k]) for k in SMALL_SHARDED])
    shard = lambda l, k: w[k][l].astype(BF16)
    w_in0, small_all = _gather_now([shard(0, "w_in"), small_pack])
    order = [[(0, "w_out"), (0, "w_up")], [(0, "w_down")], [(1, "w_in"), (1, "w_out"), (1, "w_up"), (1, "w_down")]]
    state, token = _gather_start([[shard(l, k) for l, k in g] for g in order], me1[0])
    landed = {}

    def fetch(l, k, after):
        if (l, k) == (0, "w_in"):
            raw = w_in0
        else:
            gi = [i for i, g in enumerate(order) if (l, k) in g][0]
            if gi not in landed:
                landed[gi] = _gather_wait(state[gi], after, "gather_wait_%d" % gi)
            raw = landed[gi][order[gi].index((l, k))]
        if k == "w_in":
            return raw.transpose(1, 0, 2).reshape(D_MODEL, D_IN)
        return raw.reshape(D_MODEL, D_MODEL) if k == "w_out" else raw

    wfull = dict(w)
    off = 0
    for k, rows in zip(SMALL_SHARDED, small_rows):
        sm = small_all[:, off:off + rows].reshape(N_CHIPS, -1)[:, :w[k].size].reshape((N_CHIPS,) + w[k].shape)
        wfull[k] = _from_shard_major(k, sm)
        off += rows
    wfull["mix_norm_g"] = w["mix_norm_g"] + token[0, 0]

    seq = x.shape[1]
    t_real = N_META + seq
    t_pad = -(-t_real // ROW_ALIGN) * ROW_ALIGN
    tail = jnp.zeros((t_pad - t_real, D_MODEL), F32)
    h = jnp.concatenate([wfull["meta_tokens"], x[0], tail])
    tgt = jnp.concatenate([jnp.zeros((N_META, D_MODEL), F32), loss_target[0], tail])
    sched = _GradientSchedule(w, mom, var, c1, me1)
    loss, dh, gsmall = _local_step(h, tgt, t_real, wfull, fetch, sched)
    grad_x = dh[N_META:t_real][None]
    gsmall["meta_tokens"] = dh[:N_META]

    pieces, offsets, strides = [], {}, {}
    row = 0
    for k in SMALL_REPL:
        rows = _rows_of(w[k].size)
        pieces.append(_as_rows(gsmall[k], rows))
        offsets[k], strides[k] = row, 0
        row += rows
    for k in SMALL_SHARDED:
        rows = _rows_of(w[k].size)
        sm = _to_shard_major(k, gsmall[k]).reshape(N_CHIPS, -1)
        pieces.append(jnp.pad(sm, ((0, 0), (0, rows * LANES - sm.shape[1]))).reshape(N_CHIPS * rows, LANES))
        offsets[k], strides[k] = row, rows
        row += N_CHIPS * rows
    gpack = _allreduce_small(jnp.concatenate(pieces))
    sched.run([("c0", "to_sibling")], gpack)

    out = {}
    names = SMALL_REPL + SMALL_SHARDED
    as_rows = lambda a: a.reshape(-1, LANES) if a.size % (LANES * SUBLANES) == 0 or a.size < LANES * SUBLANES else _as_rows(a)
    res = _adamw_small([as_rows(w[k]) for k in names], [as_rows(mom[k]) for k in names], [as_rows(var[k]) for k in names],
                       gpack, [offsets[k] for k in names], [strides[k] for k in names], me1)
    for k, r4 in zip(names, res):
        out[k] = tuple(o.reshape(-1)[:w[k].size].reshape(w[k].shape) for o in r4)
    sched.run([("c0", "finish")], res[0][0])
    out.update(sched.out)

    loss = lax.psum(loss, ("x", "y", "c"))
    return (loss, grad_x, *[out[k][0] for k in WEIGHTS], *[out[k][1] for k in WEIGHTS],
            *[out[k][2] for k in WEIGHTS], *[out[k][3] for k in WEIGHTS])
```

```python
import functools

import jax
import jax.numpy as jnp
from jax import lax
from jax.experimental import pallas as pl
from jax.experimental.pallas import tpu as pltpu

F32, BF16 = jnp.float32, jnp.bfloat16
MESH = pl.DeviceIdType.MESH
ANY = pl.BlockSpec(memory_space=pl.ANY)

D_MODEL = 1024
N_META = 16
D_POOL = 256
D_CONV = 256
D_RNN = 512
D_IN = D_POOL + 2 * D_CONV + 2 * D_RNN
D_FF = 4096
FF_CHUNK = 1024
POOL_GW = 64
CONV_K = 31
RG_CONV_K = 4
RG_HD = 64
RG_C = 8.0
EPS = 1e-6
ADAM_LR, ADAM_B1, ADAM_B2, ADAM_EPS, ADAM_WD, ADAM_STEP = 0.001, 0.9, 0.999, 1e-08, 0.01, 10

HALO = 32
ROW_ALIGN = 256
TM_MIX = 256
TM_MAT = 768
N_CHIPS = 4
VMEM_LIMIT = 56 * 1024 * 1024

BIG = ("w_in", "w_out", "w_up", "w_down")
SMALL_SHARDED = ("meta_tokens", "convb_dw_w", "convb_pw_w", "rg_conv_w")
SMALL_REPL = ("mix_norm_g", "pool_w", "pool_scale", "convb_dw_b", "convb_ln_g", "convb_ln_b", "rg_conv_b",
              "rg_w_a", "rg_b_a", "rg_w_x", "rg_b_x", "rg_lambda", "mlp_norm_g", "final_norm_g")
WEIGHTS = ("meta_tokens", "mix_norm_g", "w_in", "pool_w", "pool_scale", "convb_dw_w", "convb_dw_b", "convb_ln_g",
           "convb_ln_b", "convb_pw_w", "rg_conv_w", "rg_conv_b", "rg_w_a", "rg_b_a", "rg_w_x", "rg_b_x",
           "rg_lambda", "w_out", "mlp_norm_g", "w_up", "w_down", "final_norm_g")


def _params(*sem):
    return pltpu.CompilerParams(dimension_semantics=sem, vmem_limit_bytes=VMEM_LIMIT)


def _row_tile(t, cap):
    best = None
    for tm in range(128, cap + 1, 128):
        if t % tm == 0:
            best = tm
    assert best is not None, (t, cap)
    return best


def _dot(a, b):
    return jnp.dot(a, b, preferred_element_type=F32)


def _dot_nt(a, b):
    return lax.dot_general(a, b, (((1,), (1,)), ((), ())), preferred_element_type=F32)


def _dot_tn(a, b):
    return lax.dot_general(a, b, (((0,), (0,)), ((), ())), preferred_element_type=F32)


def _rms(x):
    r = lax.rsqrt(jnp.mean(x * x, axis=-1, keepdims=True) + EPS)
    return r, x * r


def _rms_bwd(du, n, r, g):
    dn = du * g
    return r * (dn - n * jnp.mean(dn * n, axis=-1, keepdims=True))


def _sig(x):
    return jax.nn.sigmoid(x)


def _colsum(x):
    return jnp.sum(x, axis=0, keepdims=True)


def _one_minus_sq(a, log_a):
    x = 2.0 * log_a
    series = -x * (1.0 + x * (0.5 + x * (1.0 / 6 + x * (1.0 / 24 + x * (1.0 / 120)))))
    return jnp.where(x > -0.05, series, 1.0 - a * a)


_GELU_K0 = 0.7978845608028654
_GELU_K1 = 0.044715


def _gelu_and_grad(x):
    th = jnp.tanh(_GELU_K0 * (x + _GELU_K1 * x * x * x))
    val = 0.5 * x * (1.0 + th)
    grad = 0.5 * (1.0 + th) + 0.5 * x * (1.0 - th * th) * _GELU_K0 * (1.0 + 3.0 * _GELU_K1 * x * x)
    return val, grad


def _full(a):
    nd = a.ndim
    return pl.BlockSpec(a.shape, lambda *_: (0,) * nd)


def _after(body, n_in, deps):
    def wrapped(*refs):
        return body(*refs[:n_in], *refs[n_in + len(deps):])
    return wrapped


def _lane_sel(lane, a2, a4, a8, a16):
    return jnp.where(lane < POOL_GW, a2, jnp.where(lane < 2 * POOL_GW, a4, jnp.where(lane < 3 * POOL_GW, a8, a16)))


def _window_sums_back(src, tmp_a, tmp_b, tm):
    n = HALO + tm
    rows = lambda ref, lo, back: ref[pl.ds(lo - back, n - lo), :]
    tmp_a[pl.ds(8, n - 8), :] = rows(src, 8, 0) + rows(src, 8, 1)
    tmp_b[pl.ds(16, n - 16), :] = rows(tmp_a, 16, 0) + rows(tmp_a, 16, 2)
    s2 = rows(tmp_a, HALO, 0)
    tmp_a[pl.ds(24, n - 24), :] = rows(tmp_b, 24, 0) + rows(tmp_b, 24, 4)
    s8 = rows(tmp_a, HALO, 0)
    return s2, rows(tmp_b, HALO, 0), s8, s8 + rows(tmp_a, HALO, 8)


def _window_sums_ahead(src, tmp_a, tmp_b, tm):
    rows = lambda ref, n, ahead: ref[pl.ds(ahead, n), :]
    tmp_a[pl.ds(0, tm + 24), :] = rows(src, tm + 24, 0) + rows(src, tm + 24, 1)
    tmp_b[pl.ds(0, tm + 16), :] = rows(tmp_a, tm + 16, 0) + rows(tmp_a, tm + 16, 2)
    s2 = rows(tmp_a, tm, 0)
    tmp_a[pl.ds(0, tm + 8), :] = rows(tmp_b, tm + 8, 0) + rows(tmp_b, tm + 8, 4)
    s8 = rows(tmp_a, tm, 0)
    return s2, rows(tmp_b, tm, 0), s8, s8 + rows(tmp_a, tm, 8)


def _pool_counts(tm, t0):
    lane = lax.broadcasted_iota(jnp.int32, (tm, D_POOL), 1)
    row = lax.broadcasted_iota(jnp.int32, (tm, D_POOL), 0) + t0
    cnt = jnp.minimum(row + 1, _lane_sel(lane, 2, 4, 8, 16)).astype(F32)
    return lane, cnt


def _pool_fwd(ext_q, tmp_a, tmp_b, tm, t0):
    lane, cnt = _pool_counts(tm, t0)
    q = ext_q[pl.ds(HALO, tm), :]
    pooled = _lane_sel(lane, *_window_sums_back(ext_q, tmp_a, tmp_b, tm)) / cnt - q
    return pooled, lane, cnt


def _taps(src, w_of, offs, tm, zbuf):
    acc = None
    for r in range(8):
        ks = [k for k in range(len(offs)) if offs[k] % 8 == r]
        if not ks:
            continue
        rows = tm + (8 if r else 0)
        z = w_of(ks[0]) * src[pl.ds(offs[ks[0]] - r, rows), :]
        for k in ks[1:]:
            z = z + w_of(k) * src[pl.ds(offs[k] - r, rows), :]
        if r:
            zbuf[...] = z
            z = zbuf[pl.ds(r, tm), :]
        acc = z if acc is None else acc + z
    return acc


def _tap_grads(d_pad, src, offs, tm, g_ref):
    ch = src.shape[-1]
    for r in range(8):
        ks = [k for k in range(len(offs)) if offs[k] % 8 == r]
        if not ks:
            continue
        rows = tm + (8 if r else 0)
        d = d_pad[pl.ds(8 - r, rows), :]
        for k in ks:
            prod = d * src[pl.ds(offs[k] - r, rows), :]
            g_ref[k] += jnp.sum(prod.reshape(rows // 8, 8, ch), axis=0)


_CONV_OFFS = [HALO - (CONV_K - 1) + k for k in range(CONV_K)]


def _conv_fwd(ext_u, dww_ref, dwb, tm, zbuf):
    return dwb + _taps(ext_u, lambda k: dww_ref[k:k + 1, :], _CONV_OFFS, tm, zbuf)


def _ln_silu(c, lng, lnb):
    mu = jnp.mean(c, axis=-1, keepdims=True)
    cc = c - mu
    rstd = lax.rsqrt(jnp.mean(cc * cc, axis=-1, keepdims=True) + EPS)
    z = cc * rstd
    l = z * lng + lnb
    sl = _sig(l)
    return z, rstd, l, sl, l * sl


def _rg_fwd(ext_x, cw_ref, cb, wa, ba, wx, bx, lam, tm):
    xc = cb + cw_ref[0:1, :] * ext_x[pl.ds(HALO - (RG_CONV_K - 1), tm), :]
    for k in range(1, RG_CONV_K):
        xc = xc + cw_ref[k:k + 1, :] * ext_x[pl.ds(HALO - (RG_CONV_K - 1) + k, tm), :]
    xcb = xc.astype(BF16)
    r = _sig(_dot(xcb, wa) + ba)
    ig = _sig(_dot(xcb, wx) + bx)
    sp = jnp.maximum(-lam, 0.0) + jnp.log(1.0 + jnp.exp(-jnp.abs(lam)))
    log_a = (-RG_C * r) * sp
    a = jnp.exp(log_a)
    m = jnp.sqrt(_one_minus_sq(a, log_a))
    return xc, xcb, r, ig, sp, a, m


def _scan_rows(a_ref, b_ref, out_ref, carry, tm, reverse):
    rows = lax.broadcasted_iota(jnp.int32, (8, D_RNN), 0)
    ngrp = tm // 8

    def grp(gi, hb):
        st = pl.multiple_of((ngrp - 1 - gi if reverse else gi) * 8, 8)
        a8 = a_ref[pl.ds(st, 8), :]
        b8 = b_ref[pl.ds(st, 8), :]
        out = jnp.zeros((8, D_RNN), F32)
        for j in (range(7, -1, -1) if reverse else range(8)):
            aj = jnp.broadcast_to(a8[j:j + 1, :], (8, D_RNN))
            bj = jnp.broadcast_to(b8[j:j + 1, :], (8, D_RNN))
            if reverse:
                cur = bj + hb
                hb = aj * cur
            else:
                cur = aj * hb + bj
                hb = cur
            out = jnp.where(rows == j, cur, out)
        out_ref[pl.ds(st, 8), :] = out
        return hb

    carry[...] = lax.fori_loop(0, ngrp, grp, carry[...])


_MIX_W = ("wp", "psc", "dww", "dwb", "lng", "lnb", "wpw", "cw", "cb", "wa", "ba", "wx", "bx", "lam")


def _mixer_fwd(p, mw):
    t = p.shape[0]
    tm = _row_tile(t, TM_MIX)

    def body(p_ref, wp, psc, dww, dwb, lng, lnb, wpw, cw, cb, wa, ba, wx, bx, lam, y_ref, hs_ref, conv_ref,
             ext_q, ext_u, ext_x, tmp_a, tmp_b, zbuf, a_s, b_s, hcar):
        i = pl.program_id(0)

        @pl.when(i == 0)
        def _():
            ext_q[0:HALO, :] = jnp.zeros((HALO, D_POOL), F32)
            ext_u[0:HALO, :] = jnp.zeros((HALO, D_CONV), F32)
            ext_x[0:HALO, :] = jnp.zeros((HALO, D_RNN), F32)
            hcar[...] = jnp.zeros((8, D_RNN), F32)

        ext_q[pl.ds(HALO, tm), :] = p_ref[:, 0:256]
        pooled, _, _ = _pool_fwd(ext_q, tmp_a, tmp_b, tm, i * tm)
        y_ref[:, 0:256] = (_dot(pooled.astype(BF16), wp[...]) * psc[...]).astype(BF16)

        ext_u[pl.ds(HALO, tm), :] = p_ref[:, 256:512] * _sig(p_ref[:, 512:768])
        conv = _conv_fwd(ext_u, dww, dwb[...], tm, zbuf)
        conv_ref[...] = conv
        act = _ln_silu(conv, lng[...], lnb[...])[4]
        y_ref[:, 256:512] = _dot(act.astype(BF16), wpw[...]).astype(BF16)

        ext_x[pl.ds(HALO, tm), :] = p_ref[:, 1280:1792]
        xc, _, _, ig, _, a, m = _rg_fwd(ext_x, cw, cb[...], wa[...], ba[...], wx[...], bx[...], lam[...], tm)
        a_s[...] = a
        b_s[...] = m * (ig * xc)
        _scan_rows(a_s, b_s, hs_ref, hcar, tm, reverse=False)
        y_ref[:, 512:1024] = (_gelu_and_grad(p_ref[:, 768:1280])[0] * hs_ref[...]).astype(BF16)

        ext_q[0:HALO, :] = ext_q[pl.ds(tm, HALO), :]
        ext_u[0:HALO, :] = ext_u[pl.ds(tm, HALO), :]
        ext_x[0:HALO, :] = ext_x[pl.ds(tm, HALO), :]

    ws = [mw[k] for k in _MIX_W]
    return pl.pallas_call(
        body, name="mixer_fwd", grid=(t // tm,),
        in_specs=[pl.BlockSpec((tm, D_IN), lambda i: (i, 0))] + [_full(w) for w in ws],
        out_specs=[pl.BlockSpec((tm, D_MODEL), lambda i: (i, 0)), pl.BlockSpec((tm, D_RNN), lambda i: (i, 0)),
                   pl.BlockSpec((tm, D_CONV), lambda i: (i, 0))],
        out_shape=[jax.ShapeDtypeStruct((t, D_MODEL), BF16), jax.ShapeDtypeStruct((t, D_RNN), F32),
                   jax.ShapeDtypeStruct((t, D_CONV), F32)],
        scratch_shapes=[pltpu.VMEM((HALO + tm, D_POOL), F32), pltpu.VMEM((HALO + tm, D_CONV), F32),
                        pltpu.VMEM((HALO + tm, D_RNN), F32), pltpu.VMEM((HALO + tm, D_POOL), F32),
                        pltpu.VMEM((HALO + tm, D_POOL), F32), pltpu.VMEM((tm + 8, D_CONV), F32),
                        pltpu.VMEM((tm, D_RNN), F32), pltpu.VMEM((tm, D_RNN), F32), pltpu.VMEM((8, D_RNN), F32)],
        compiler_params=_params("arbitrary"),
    )(p, *ws)


_MIX_G = (("wp", (D_POOL, D_POOL)), ("psc", (1, D_POOL)), ("dww", (32, 8, D_CONV)), ("dwb", (1, D_CONV)),
          ("lng", (1, D_CONV)), ("lnb", (1, D_CONV)), ("wpw", (D_CONV, D_CONV)), ("cw", (8, D_RNN)),
          ("cb", (1, D_RNN)), ("wa", (D_RNN, D_RNN)), ("ba", (1, D_RNN)), ("wx", (D_RNN, D_RNN)),
          ("bx", (1, D_RNN)), ("lam", (1, D_RNN)))


def _mixer_bwd(p, dy, hs, conv, mw, deps=()):
    t = p.shape[0]
    tm = _row_tile(t, TM_MIX)
    nt = t // tm
    hb = tm // HALO

    def body(p_ref, ph_ref, dy_ref, hs_ref, hsh_ref, conv_ref, wp, psc, dww, dwb, lng, lnb, wpw, cw, cb, wa, ba, wx, bx, lam,
             dp_ref, g_wp, g_psc, g_dww, g_dwb, g_lng, g_lnb, g_wpw, g_cw, g_cb, g_wa, g_ba, g_wx, g_bx, g_lam,
             ext_q, ext_u, ext_x, ext_h, ee, dc_s, dx_s, tmp_a, tmp_b, zbuf, d_pad, a_s, b_s, g_s, gcar):
        step = pl.program_id(0)
        i = nt - 1 - step
        grads = (g_wp, g_psc, g_dww, g_dwb, g_lng, g_lnb, g_wpw, g_cw, g_cb, g_wa, g_ba, g_wx, g_bx, g_lam)

        @pl.when(step == 0)
        def _():
            for gr in grads:
                gr[...] = jnp.zeros(gr.shape, F32)
            ee[pl.ds(tm, HALO), :] = jnp.zeros((HALO, D_POOL), F32)
            dc_s[pl.ds(tm, HALO), :] = jnp.zeros((HALO, D_CONV), F32)
            dx_s[pl.ds(tm, HALO), :] = jnp.zeros((HALO, D_RNN), F32)
            d_pad[0:8, :] = jnp.zeros((8, D_CONV), F32)
            d_pad[pl.ds(tm + 8, 8), :] = jnp.zeros((8, D_CONV), F32)
            gcar[...] = jnp.zeros((8, D_RNN), F32)

        hm = jnp.where(i == 0, 0.0, 1.0)

        ext_q[0:HALO, :] = ph_ref[:, 0:256] * hm
        ext_q[pl.ds(HALO, tm), :] = p_ref[:, 0:256]
        pooled, lane, cnt = _pool_fwd(ext_q, tmp_a, tmp_b, tm, i * tm)
        pooled_b = pooled.astype(BF16)
        dya = dy_ref[:, 0:256]
        g_psc[...] += _colsum(dya * _dot(pooled_b, wp[...]))
        dmixed_b = (dya * psc[...]).astype(BF16)
        dpooled = _dot_nt(dmixed_b, wp[...])
        g_wp[...] += _dot_tn(pooled_b, dmixed_b)
        ee[0:tm, :] = dpooled / cnt
        dp_ref[:, 0:256] = _lane_sel(lane, *_window_sums_ahead(ee, tmp_a, tmp_b, tm)) - dpooled
        ee[pl.ds(tm, HALO), :] = ee[0:HALO, :]

        v = p_ref[:, 256:512]
        s = _sig(p_ref[:, 512:768])
        ext_u[0:HALO, :] = ph_ref[:, 256:512] * _sig(ph_ref[:, 512:768]) * hm
        ext_u[pl.ds(HALO, tm), :] = v * s
        z, rstd, l, sl, act = _ln_silu(conv_ref[...], lng[...], lnb[...])
        dyb_b = dy_ref[:, 256:512].astype(BF16)
        dact = _dot_nt(dyb_b, wpw[...])
        g_wpw[...] += _dot_tn(act.astype(BF16), dyb_b)
        dl = dact * (sl * (1.0 + l * (1.0 - sl)))
        g_lng[...] += _colsum(dl * z)
        g_lnb[...] += _colsum(dl)
        dz = dl * lng[...]
        dc = rstd * (dz - jnp.mean(dz, axis=-1, keepdims=True) - z * jnp.mean(dz * z, axis=-1, keepdims=True))
        g_dwb[...] += _colsum(dc)
        dc_s[0:tm, :] = dc
        d_pad[pl.ds(8, tm), :] = dc
        _tap_grads(d_pad, ext_u, _CONV_OFFS, tm, g_dww)
        du0 = _taps(dc_s, lambda j: dww[CONV_K - 1 - j:CONV_K - j, :], list(range(CONV_K)), tm, zbuf)
        dp_ref[:, 256:512] = du0 * s
        dp_ref[:, 512:768] = du0 * v * (s * (1.0 - s))
        dc_s[pl.ds(tm, HALO), :] = dc_s[0:HALO, :]

        ext_x[0:HALO, :] = ph_ref[:, 1280:1792] * hm
        ext_x[pl.ds(HALO, tm), :] = p_ref[:, 1280:1792]
        xc, xcb, r, ig, sp, a, m = _rg_fwd(ext_x, cw, cb[...], wa[...], ba[...], wx[...], bx[...], lam[...], tm)
        ext_h[0:HALO, :] = hsh_ref[...] * hm
        ext_h[pl.ds(HALO, tm), :] = hs_ref[...]
        dyc = dy_ref[:, 512:1024]
        gl, dgl = _gelu_and_grad(p_ref[:, 768:1280])
        dp_ref[:, 768:1280] = dyc * hs_ref[...] * dgl
        a_s[...] = a
        b_s[...] = dyc * gl
        _scan_rows(a_s, b_s, g_s, gcar, tm, reverse=True)
        g = g_s[...]
        da = g * ext_h[pl.ds(HALO - 1, tm), :]
        dm = g * (ig * xc)
        dig = g * (m * xc)
        dlog_a = da * a - dm * (a * a) / m
        g_lam[...] += _colsum(dlog_a * (-RG_C * r)) * (-_sig(-lam[...]))
        dra = (dlog_a * (-RG_C * sp)) * (r * (1.0 - r))
        dia = dig * (ig * (1.0 - ig))
        g_ba[...] += _colsum(dra)
        g_bx[...] += _colsum(dia)
        dra_b = dra.astype(BF16)
        dia_b = dia.astype(BF16)
        dxc = g * (m * ig) + _dot_nt(dra_b, wa[...]) + _dot_nt(dia_b, wx[...])
        g_wa[...] += _dot_tn(xcb, dra_b)
        g_wx[...] += _dot_tn(xcb, dia_b)
        g_cb[...] += _colsum(dxc)
        dx_s[0:tm, :] = dxc
        for k in range(RG_CONV_K):
            g_cw[k:k + 1, :] += _colsum(dxc * ext_x[pl.ds(HALO - (RG_CONV_K - 1) + k, tm), :])
        dxin = cw[RG_CONV_K - 1:RG_CONV_K, :] * dxc
        for j in range(1, RG_CONV_K):
            dxin = dxin + cw[RG_CONV_K - 1 - j:RG_CONV_K - j, :] * dx_s[pl.ds(j, tm), :]
        dp_ref[:, 1280:1792] = dxin
        dx_s[pl.ds(tm, HALO), :] = dx_s[0:HALO, :]

    ws = [mw[k] for k in _MIX_W]
    tile = lambda w: pl.BlockSpec((tm, w), lambda s: (nt - 1 - s, 0))
    halo = lambda w: pl.BlockSpec((HALO, w), lambda s: (jnp.maximum((nt - 1 - s) * hb - 1, 0), 0))
    outs = pl.pallas_call(
        _after(body, 6 + len(ws), deps), name="mixer_bwd", grid=(nt,),
        in_specs=[tile(D_IN), halo(D_IN), tile(D_MODEL), tile(D_RNN), halo(D_RNN), tile(D_CONV)] + [_full(w) for w in ws]
        + [ANY] * len(deps),
        out_specs=[tile(D_IN)] + [pl.BlockSpec(shp, lambda s, nd=len(shp): (0,) * nd) for _, shp in _MIX_G],
        out_shape=[jax.ShapeDtypeStruct((t, D_IN), F32)] + [jax.ShapeDtypeStruct(shp, F32) for _, shp in _MIX_G],
        scratch_shapes=[pltpu.VMEM((HALO + tm, D_POOL), F32), pltpu.VMEM((HALO + tm, D_CONV), F32),
                        pltpu.VMEM((HALO + tm, D_RNN), F32), pltpu.VMEM((HALO + tm, D_RNN), F32),
                        pltpu.VMEM((tm + HALO, D_POOL), F32), pltpu.VMEM((tm + HALO, D_CONV), F32),
                        pltpu.VMEM((tm + HALO, D_RNN), F32), pltpu.VMEM((HALO + tm, D_POOL), F32),
                        pltpu.VMEM((HALO + tm, D_POOL), F32), pltpu.VMEM((tm + 8, D_CONV), F32),
                        pltpu.VMEM((tm + 16, D_CONV), F32), pltpu.VMEM((tm, D_RNN), F32),
                        pltpu.VMEM((tm, D_RNN), F32), pltpu.VMEM((tm, D_RNN), F32), pltpu.VMEM((8, D_RNN), F32)],
        compiler_params=_params("arbitrary"),
    )(p, p, dy, hs, hs, conv, *ws, *deps)
    return outs[0], {k: o for (k, _), o in zip(_MIX_G, outs[1:])}


def _in_proj(h, g, w):
    t = h.shape[0]
    tm = _row_tile(t, TM_MAT)

    def body(h_ref, g_ref, w_ref, p_ref, u_ref):
        u = (_rms(h_ref[...])[1] * g_ref[...]).astype(BF16)
        u_ref[...] = u
        p_ref[...] = _dot(u, w_ref[...])

    return pl.pallas_call(
        body, name="in_proj", grid=(t // tm,),
        in_specs=[pl.BlockSpec((tm, D_MODEL), lambda i: (i, 0)), _full(g), _full(w)],
        out_specs=[pl.BlockSpec((tm, D_IN), lambda i: (i, 0)), pl.BlockSpec((tm, D_MODEL), lambda i: (i, 0))],
        out_shape=[jax.ShapeDtypeStruct((t, D_IN), F32), jax.ShapeDtypeStruct((t, D_MODEL), BF16)],
        compiler_params=_params("parallel"),
    )(h, g, w)


def _mid_fwd(y, h0, w_out, g, w_up):
    t = h0.shape[0]
    tm = _row_tile(t, TM_MAT)
    nj = D_FF // FF_CHUNK

    def body(y_ref, h0_ref, wo_ref, g_ref, wu_ref, h1_ref, u2_ref, f_ref):
        @pl.when(pl.program_id(1) == 0)
        def _():
            h1 = h0_ref[...] + _dot(y_ref[...], wo_ref[...])
            h1_ref[...] = h1
            u2_ref[...] = (_rms(h1)[1] * g_ref[...]).astype(BF16)

        f_ref[...] = _dot(u2_ref[...], wu_ref[...]).astype(BF16)

    row = lambda w: pl.BlockSpec((tm, w), lambda i, j: (i, 0))
    return pl.pallas_call(
        body, name="mid_fwd", grid=(t // tm, nj),
        in_specs=[row(D_MODEL), row(D_MODEL), _full(w_out), _full(g),
                  pl.BlockSpec((None, D_MODEL, FF_CHUNK), lambda i, j: (j, 0, 0))],
        out_specs=[row(D_MODEL), row(D_MODEL), pl.BlockSpec((tm, FF_CHUNK), lambda i, j: (i, j))],
        out_shape=[jax.ShapeDtypeStruct((t, D_MODEL), F32), jax.ShapeDtypeStruct((t, D_MODEL), BF16),
                   jax.ShapeDtypeStruct((t, D_FF), BF16)],
        compiler_params=_params("parallel", "arbitrary"),
    )(y, h0, w_out, g, w_up)


def _down_fwd(f, h1, w_down):
    t = h1.shape[0]
    tm = _row_tile(t, TM_MAT)

    def body(f_ref, h1_ref, wd_ref, h2_ref):
        acc = h1_ref[...]
        for c in range(D_FF // FF_CHUNK):
            cols = slice(c * FF_CHUNK, (c + 1) * FF_CHUNK)
            a = jnp.square(jnp.maximum(f_ref[:, cols].astype(F32), 0.0)).astype(BF16)
            acc = acc + _dot(a, wd_ref[cols, :])
        h2_ref[...] = acc

    return pl.pallas_call(
        body, name="down_fwd", grid=(t // tm,),
        in_specs=[pl.BlockSpec((tm, D_FF), lambda i: (i, 0)), pl.BlockSpec((tm, D_MODEL), lambda i: (i, 0)), _full(w_down)],
        out_specs=pl.BlockSpec((tm, D_MODEL), lambda i: (i, 0)),
        out_shape=jax.ShapeDtypeStruct((t, D_MODEL), F32),
        compiler_params=_params("parallel"),
    )(f, h1, w_down)


def _loss_head(h, g, tgt, t_real):
    t = h.shape[0]
    tm = _row_tile(t, TM_MAT)

    def body(h_ref, g_ref, tgt_ref, loss_ref, dh_ref, dg_ref):
        i = pl.program_id(0)

        @pl.when(i == 0)
        def _():
            loss_ref[...] = jnp.zeros(loss_ref.shape, F32)
            dg_ref[...] = jnp.zeros(dg_ref.shape, F32)

        r, n = _rms(h_ref[...])
        row = lax.broadcasted_iota(jnp.int32, (tm, 1), 0) + i * tm
        valid = jnp.logical_and(row >= N_META, row < t_real)
        diff = jnp.where(valid, n * g_ref[...] - tgt_ref[...], 0.0)
        loss_ref[...] += 0.5 * jnp.sum(jnp.mean(diff * diff, axis=-1, keepdims=True))
        dy = diff * (1.0 / D_MODEL)
        dg_ref[...] += _colsum(dy * n)
        dh_ref[...] = _rms_bwd(dy, n, r, g_ref[...])

    return pl.pallas_call(
        body, name="loss_head", grid=(t // tm,),
        in_specs=[pl.BlockSpec((tm, D_MODEL), lambda i: (i, 0)), _full(g), pl.BlockSpec((tm, D_MODEL), lambda i: (i, 0))],
        out_specs=[pl.BlockSpec((8, 128), lambda i: (0, 0)), pl.BlockSpec((tm, D_MODEL), lambda i: (i, 0)),
                   pl.BlockSpec((1, D_MODEL), lambda i: (0, 0))],
        out_shape=[jax.ShapeDtypeStruct((8, 128), F32), jax.ShapeDtypeStruct((t, D_MODEL), F32),
                   jax.ShapeDtypeStruct((1, D_MODEL), F32)],
        compiler_params=_params("arbitrary"),
    )(h, g, tgt)


def _mlp_bwd(dh2, f, h1, g, w_up, w_down, deps=()):
    t = dh2.shape[0]
    tm = _row_tile(t, TM_MAT)
    nj = D_FF // FF_CHUNK

    def body(dh2_ref, f_ref, wd_ref, wu_ref, h1_ref, g_ref, df_ref, dh1_ref, dg_ref, acc, dhb):
        i, j = pl.program_id(0), pl.program_id(1)

        @pl.when(j == 0)
        def _():
            dhb[...] = dh2_ref[...].astype(BF16)
            acc[...] = jnp.zeros(acc.shape, F32)

        @pl.when(jnp.logical_and(i == 0, j == 0))
        def _():
            dg_ref[...] = jnp.zeros(dg_ref.shape, F32)

        dact = _dot_nt(dhb[...], wd_ref[...])
        df = (dact * (2.0 * jnp.maximum(f_ref[...].astype(F32), 0.0))).astype(BF16)
        df_ref[...] = df
        acc[...] += _dot_nt(df, wu_ref[...])

        @pl.when(j == nj - 1)
        def _():
            r, n = _rms(h1_ref[...])
            du2 = acc[...]
            dg_ref[...] += _colsum(du2 * n)
            dh1_ref[...] = dh2_ref[...] + _rms_bwd(du2, n, r, g_ref[...])

    row = lambda w: pl.BlockSpec((tm, w), lambda i, j: (i, 0))
    return pl.pallas_call(
        _after(body, 6, deps), name="mlp_bwd", grid=(t // tm, nj),
        in_specs=[row(D_MODEL), pl.BlockSpec((tm, FF_CHUNK), lambda i, j: (i, j)),
                  pl.BlockSpec((None, FF_CHUNK, D_MODEL), lambda i, j: (j, 0, 0)),
                  pl.BlockSpec((None, D_MODEL, FF_CHUNK), lambda i, j: (j, 0, 0)), row(D_MODEL), _full(g)] + [ANY] * len(deps),
        out_specs=[pl.BlockSpec((tm, FF_CHUNK), lambda i, j: (i, j)), row(D_MODEL),
                   pl.BlockSpec((1, D_MODEL), lambda i, j: (0, 0))],
        out_shape=[jax.ShapeDtypeStruct((t, D_FF), BF16), jax.ShapeDtypeStruct((t, D_MODEL), F32),
                   jax.ShapeDtypeStruct((1, D_MODEL), F32)],
        scratch_shapes=[pltpu.VMEM((tm, D_MODEL), F32), pltpu.VMEM((tm, D_MODEL), BF16)],
        compiler_params=_params("arbitrary", "arbitrary"),
    )(dh2, f, w_down, w_up, h1, g, *deps)


def _out_bwd(dh1, w_out, deps=()):
    t = dh1.shape[0]
    tm = _row_tile(t, TM_MAT)

    def body(dh_ref, w_ref, dy_ref):
        dy_ref[...] = _dot_nt(dh_ref[...].astype(BF16), w_ref[...])

    return pl.pallas_call(
        _after(body, 2, deps), name="out_bwd", grid=(t // tm,),
        in_specs=[pl.BlockSpec((tm, D_MODEL), lambda i: (i, 0)), _full(w_out)] + [ANY] * len(deps),
        out_specs=pl.BlockSpec((tm, D_MODEL), lambda i: (i, 0)),
        out_shape=jax.ShapeDtypeStruct((t, D_MODEL), F32),
        compiler_params=_params("parallel"),
    )(dh1, w_out, *deps)


def _in_bwd(dp, dh1, h0, g, w_in, deps=()):
    t = dp.shape[0]
    tm = _row_tile(t, TM_MAT)

    def body(dp_ref, dh1_ref, h0_ref, g_ref, w_ref, dh0_ref, dg_ref):
        @pl.when(pl.program_id(0) == 0)
        def _():
            dg_ref[...] = jnp.zeros(dg_ref.shape, F32)

        du = _dot_nt(dp_ref[...].astype(BF16), w_ref[...])
        r, n = _rms(h0_ref[...])
        dg_ref[...] += _colsum(du * n)
        dh0_ref[...] = dh1_ref[...] + _rms_bwd(du, n, r, g_ref[...])

    row = lambda w: pl.BlockSpec((tm, w), lambda i: (i, 0))
    return pl.pallas_call(
        _after(body, 5, deps), name="in_bwd", grid=(t // tm,),
        in_specs=[row(D_IN), row(D_MODEL), row(D_MODEL), _full(g), _full(w_in)] + [ANY] * len(deps),
        out_specs=[row(D_MODEL), pl.BlockSpec((1, D_MODEL), lambda i: (0, 0))],
        out_shape=[jax.ShapeDtypeStruct((t, D_MODEL), F32), jax.ShapeDtypeStruct((1, D_MODEL), F32)],
        compiler_params=_params("arbitrary"),
    )(dp, dh1, h0, g, w_in, *deps)


def _tn_matmul(a, b, kc, nc, relu2, name, deps=()):
    t, k = a.shape
    n = b.shape[1]
    tt = _row_tile(t, TM_MAT)
    gk, gn = k // kc, n // nc

    def body(a_ref, b_ref, o_ref):
        @pl.when(pl.program_id(2) == 0)
        def _():
            o_ref[...] = jnp.zeros(o_ref.shape, F32)

        av = a_ref[...]
        if relu2:
            av = jnp.square(jnp.maximum(av.astype(F32), 0.0))
        o_ref[...] += _dot_tn(av.astype(BF16), b_ref[...].astype(BF16))

    return pl.pallas_call(
        _after(body, 2, deps), name=name, grid=(gk, gn, t // tt),
        in_specs=[pl.BlockSpec((tt, kc), lambda ik, jn, it: (it, ik)), pl.BlockSpec((tt, nc), lambda ik, jn, it: (it, jn))]
        + [ANY] * len(deps),
        out_specs=pl.BlockSpec((None, kc, nc), lambda ik, jn, it: (ik * gn + jn, 0, 0)),
        out_shape=jax.ShapeDtypeStruct((gk * gn, kc, nc), F32),
        compiler_params=_params("parallel", "parallel", "arbitrary"),
    )(a, b, *deps)


def _block_diag(blocks):
    nb, hd, _ = blocks.shape
    eye = jnp.eye(nb, dtype=blocks.dtype)
    return (blocks[:, :, None, :] * eye[:, None, :, None]).reshape(nb * hd, nb * hd)


def _diag_blocks(m, nb):
    hd = m.shape[0] // nb
    return jnp.stack([m[b * hd:(b + 1) * hd, b * hd:(b + 1) * hd] for b in range(nb)])


def _mixer_weights(w, l):
    row = lambda a: a.reshape(1, -1)
    return dict(
        wp=_block_diag(w["pool_w"][l]).astype(BF16), psc=row(w["pool_scale"][l]),
        dww=jnp.pad(w["convb_dw_w"][l], ((0, 32 - CONV_K), (0, 0))), dwb=row(w["convb_dw_b"][l]),
        lng=row(w["convb_ln_g"][l]), lnb=row(w["convb_ln_b"][l]), wpw=w["convb_pw_w"][l].astype(BF16),
        cw=jnp.pad(w["rg_conv_w"][l], ((0, 8 - RG_CONV_K), (0, 0))), cb=row(w["rg_conv_b"][l]),
        wa=_block_diag(w["rg_w_a"][l]).astype(BF16), ba=row(w["rg_b_a"][l]),
        wx=_block_diag(w["rg_w_x"][l]).astype(BF16), bx=row(w["rg_b_x"][l]), lam=row(w["rg_lambda"][l]))


def _local_step(h, tgt, t_real, w, fetch, hooks):
    depth = 2
    saved = []
    big = []
    for l in range(depth):
        mw = _mixer_weights(w, l)
        g1 = w["mix_norm_g"][l].reshape(1, -1)
        g2 = w["mlp_norm_g"][l].reshape(1, -1)
        wl = dict(w_in=fetch(l, "w_in", h))
        p, u = _in_proj(h, g1, wl["w_in"])
        y, hs, conv = _mixer_fwd(p, mw)
        wl["w_out"], wl["w_up"] = fetch(l, "w_out", y), fetch(l, "w_up", y)
        h1, u2, f = _mid_fwd(y, h, wl["w_out"], g2, wl["w_up"])
        wl["w_down"] = fetch(l, "w_down", f)
        h2 = _down_fwd(f, h1, wl["w_down"].reshape(D_FF, D_MODEL))
        saved.append(dict(mw=mw, g1=g1, g2=g2, h0=h, p=p, u=u, y=y, hs=hs, conv=conv, h1=h1, u2=u2, f=f))
        big.append(wl)
        h = h2
    gf = w["final_norm_g"].reshape(1, -1)
    loss, dh, dgf = _loss_head(h, gf, tgt, t_real)

    gs = {k: [None] * depth for k in ("mix_norm_g", "mlp_norm_g", "pool_w", "pool_scale", "convb_dw_w", "convb_dw_b",
                                      "convb_ln_g", "convb_ln_b", "convb_pw_w", "rg_conv_w", "rg_conv_b", "rg_w_a",
                                      "rg_b_a", "rg_w_x", "rg_b_x", "rg_lambda")}
    deps = ()
    for l in reversed(range(depth)):
        s, wl = saved[l], big[l]
        df, dh1, dg2 = _mlp_bwd(dh, s["f"], s["h1"], s["g2"], wl["w_up"], wl["w_down"], deps)
        deps = hooks.point(l, "mlp_bwd", dh1)
        g_down = _tn_matmul(s["f"], dh, FF_CHUNK, D_MODEL, True, "dw_down", deps)
        hooks.grad(l, "w_down", g_down)
        deps = hooks.point(l, "dw_down", g_down)
        g_up = _tn_matmul(s["u2"], df, D_MODEL, FF_CHUNK, False, "dw_up", deps)
        hooks.grad(l, "w_up", g_up)
        deps = hooks.point(l, "dw_up", g_up)
        dy = _out_bwd(dh1, wl["w_out"], deps)
        deps = hooks.point(l, "out_bwd", dy)
        g_out = _tn_matmul(s["y"], dh1, D_MODEL, D_MODEL, False, "dw_out", deps)
        hooks.grad(l, "w_out", g_out.reshape(N_CHIPS, D_MODEL // N_CHIPS, D_MODEL))
        deps = hooks.point(l, "dw_out", g_out)
        dp, mg = _mixer_bwd(s["p"], dy, s["hs"], s["conv"], s["mw"], deps)
        deps = hooks.point(l, "mixer_bwd", dp)
        g_in = _tn_matmul(s["u"], dp, D_MODEL, D_IN, False, "dw_in", deps)
        g_in = g_in[0].reshape(D_MODEL, N_CHIPS, D_IN // N_CHIPS).transpose(1, 0, 2)
        hooks.grad(l, "w_in", g_in)
        deps = hooks.point(l, "dw_in", g_in)
        dh, dg1 = _in_bwd(dp, dh1, s["h0"], s["g1"], wl["w_in"], deps)
        deps = hooks.point(l, "in_bwd", dh)
        gs["mix_norm_g"][l] = dg1[0]
        gs["mlp_norm_g"][l] = dg2[0]
        gs["pool_w"][l] = _diag_blocks(mg["wp"], D_POOL // POOL_GW)
        gs["pool_scale"][l] = mg["psc"][0]
        gs["convb_dw_w"][l] = jnp.sum(mg["dww"][:CONV_K], axis=1)
        gs["convb_dw_b"][l] = mg["dwb"][0]
        gs["convb_ln_g"][l] = mg["lng"][0]
        gs["convb_ln_b"][l] = mg["lnb"][0]
        gs["convb_pw_w"][l] = mg["wpw"]
        gs["rg_conv_w"][l] = mg["cw"][:RG_CONV_K]
        gs["rg_conv_b"][l] = mg["cb"][0]
        gs["rg_w_a"][l] = _diag_blocks(mg["wa"], D_RNN // RG_HD)
        gs["rg_b_a"][l] = mg["ba"][0]
        gs["rg_w_x"][l] = _diag_blocks(mg["wx"], D_RNN // RG_HD)
        gs["rg_b_x"][l] = mg["bx"][0]
        gs["rg_lambda"][l] = mg["lam"][0]
    gsmall = {k: jnp.stack(v) for k, v in gs.items()}
    gsmall["final_norm_g"] = dgf[0]
    return loss[0, 0], dh, gsmall


def _place():
    return lax.axis_index("x"), lax.axis_index("y"), lax.axis_index("c")


def _other_chips(x, y):
    return [(1 - x, y), (x, 1 - y), (1 - x, 1 - y)]


def _gather_now(srcs):
    ns = len(srcs)
    out_shape = [jax.ShapeDtypeStruct((N_CHIPS,) + s.shape, s.dtype) for s in srcs]

    def body(*refs):
        src_refs, dst_refs = refs[:ns], refs[ns:2 * ns]
        send_sems, recv_sems, loc_sems = refs[2 * ns:]
        x, y, c = _place()
        me = 2 * x + y
        local, remote = [], []
        for n in range(ns):
            cp = pltpu.make_async_copy(src_refs[n], dst_refs[n].at[me], loc_sems.at[n])
            cp.start()
            local.append(cp)
            for j, (px, py) in enumerate(_other_chips(x, y)):
                out = pltpu.make_async_remote_copy(src_refs[n], dst_refs[n].at[me], send_sems.at[3 * n + j],
                                                   recv_sems.at[3 * n + j], device_id=(px, py, c), device_id_type=MESH)
                out.start()
                remote.append(pltpu.make_async_remote_copy(src_refs[n], dst_refs[n].at[2 * px + py], send_sems.at[3 * n + j],
                                                           recv_sems.at[3 * n + j], device_id=(px, py, c), device_id_type=MESH))
        for cp in remote:
            cp.wait()
        for cp in local:
            cp.wait()

    return pl.pallas_call(
        body, name="gather_now", in_specs=[ANY] * ns, out_specs=[ANY] * ns, out_shape=out_shape,
        scratch_shapes=[pltpu.SemaphoreType.DMA((3 * ns,)), pltpu.SemaphoreType.DMA((3 * ns,)), pltpu.SemaphoreType.DMA((ns,))],
    )(*srcs)


HBM_SPEC = pl.BlockSpec(memory_space=pltpu.HBM)
SEM_SPEC = pl.BlockSpec(memory_space=pltpu.SEMAPHORE)
DATAFLOW = pltpu.SideEffectType.DATAFLOW_SIDE_EFFECTING


def _gather_copies(src_refs, land_refs, send_sem, recv_sem, first):
    x, y, c = _place()
    me = 2 * x + y
    out = []
    for n in range(len(src_refs)):
        for j, (px, py) in enumerate(_other_chips(x, y)):
            out.append(pltpu.make_async_remote_copy(src_refs[n], land_refs[n].at[me], send_sem.at[first + 3 * n + j],
                                                    recv_sem.at[first + 3 * n + j], device_id=(px, py, c), device_id_type=MESH))
    return out


def _gather_start(groups, me):
    srcs = [pltpu.with_memory_space_constraint(s, pltpu.HBM) for g in groups for s in g]
    lands = [pltpu.with_memory_space_constraint(
        lax.dynamic_update_slice(jnp.zeros((N_CHIPS,) + s.shape, s.dtype), s[None], (me,) + (0,) * s.ndim), pltpu.HBM)
        for g in groups for s in g]
    n, ng = len(srcs), len(groups)
    first = [sum(len(g) for g in groups[:i]) for i in range(ng)]

    def body(*refs):
        src_refs, land_refs = refs[:n], refs[n:2 * n]
        sems = refs[2 * n:2 * n + 2 * ng]
        token = refs[-1]
        for gi, g in enumerate(groups):
            lo, hi = first[gi], first[gi] + len(g)
            for cp in _gather_copies(src_refs[lo:hi], land_refs[lo:hi], sems[2 * gi], sems[2 * gi + 1], 0):
                cp.start()
        token[...] = jnp.zeros(token.shape, token.dtype)

    sem_shapes = [pltpu.SemaphoreType.DMA((3 * len(g),)) for g in groups for _ in range(2)]
    outs = pl.pallas_call(
        body, name="gather_start",
        out_shape=sem_shapes + [pltpu.HBM(a.shape, a.dtype) for a in srcs + lands] + [jax.ShapeDtypeStruct((8, 128), F32)],
        in_specs=[HBM_SPEC] * (2 * n),
        out_specs=[SEM_SPEC] * (2 * ng) + [HBM_SPEC] * (2 * n) + [pl.BlockSpec(memory_space=pltpu.VMEM)],
        input_output_aliases={i: 2 * ng + i for i in range(2 * n)},
        compiler_params=pltpu.CompilerParams(has_side_effects=DATAFLOW),
    )(*srcs, *lands)
    sems, thru, token = outs[:2 * ng], outs[2 * ng:2 * ng + 2 * n], outs[-1]
    state = []
    for gi, g in enumerate(groups):
        lo, hi = first[gi], first[gi] + len(g)
        state.append((sems[2 * gi], sems[2 * gi + 1], thru[lo:hi], thru[n + lo:n + hi]))
    return state, token


def _gather_wait(state, after, name):
    send_sem, recv_sem, srcs, lands = state
    n = len(srcs)

    def body(*refs):
        src_refs, land_refs = refs[:n], refs[n:2 * n]
        send, recv = refs[2 * n], refs[2 * n + 1]
        for cp in _gather_copies(src_refs, land_refs, send, recv, 0):
            cp.wait_send()
            cp.wait_recv()

    outs = pl.pallas_call(
        body, name=name,
        out_shape=[pltpu.HBM(a.shape, a.dtype) for a in list(srcs) + list(lands)],
        in_specs=[HBM_SPEC] * (2 * n) + [SEM_SPEC, SEM_SPEC, ANY],
        out_specs=[HBM_SPEC] * (2 * n),
        input_output_aliases={i: i for i in range(2 * n)},
        compiler_params=pltpu.CompilerParams(has_side_effects=DATAFLOW),
    )(*srcs, *lands, send_sem, recv_sem, after)
    return outs[n:]


def _add_halves(g, recv, c1):
    nk, r, cd = g.shape
    r2 = r // 2

    def body(c_ref, g_ref, r_ref, pa_ref, pab_ref):
        s = g_ref[...] + r_ref[...]
        pa_ref[...] = s
        pab_ref[...] = s.astype(BF16)

    blk = pl.BlockSpec((None, r2, cd), lambda k, c_ref: (k, 0, 0))
    return pl.pallas_call(
        body, name="rs_add_halves",
        grid_spec=pltpu.PrefetchScalarGridSpec(
            num_scalar_prefetch=1, grid=(nk,),
            in_specs=[pl.BlockSpec((None, r2, cd), lambda k, c_ref: (k, c_ref[0], 0)), blk], out_specs=[blk, blk]),
        out_shape=[jax.ShapeDtypeStruct((nk, r2, cd), F32), jax.ShapeDtypeStruct((nk, r2, cd), BF16)],
        compiler_params=_params("parallel"),
    )(c1, g, recv)


def _sum_partials(pa, recv, me1):
    nk, r2, cd = pa.shape

    def body(me_ref, pa_ref, r_ref, s_ref):
        s_ref[...] = ((pa_ref[...] + r_ref[0].astype(F32)) + r_ref[1].astype(F32)) + r_ref[2].astype(F32)

    return pl.pallas_call(
        body, name="rs_sum_partials",
        grid_spec=pltpu.PrefetchScalarGridSpec(
            num_scalar_prefetch=1, grid=(1,),
            in_specs=[pl.BlockSpec((None, r2, cd), lambda i, me_ref: (me_ref[0], 0, 0)),
                      pl.BlockSpec((3, r2, cd), lambda i, me_ref: (0, 0, 0))],
            out_specs=pl.BlockSpec((r2, cd), lambda i, me_ref: (0, 0))),
        out_shape=jax.ShapeDtypeStruct((r2, cd), F32),
        compiler_params=_params("arbitrary"),
    )(me1, pa, recv)


def _split_start(name, srcs, lands, ncopies, make_copies):
    srcs = [pltpu.with_memory_space_constraint(s, pltpu.HBM) for s in srcs]
    lands = [pltpu.with_memory_space_constraint(a, pltpu.HBM) for a in lands]
    n, m = len(srcs), len(lands)

    def body(*refs):
        src_refs, land_refs = refs[:n], refs[n:n + m]
        send, recv, token = refs[n + m], refs[n + m + 1], refs[-1]
        for cp in make_copies(src_refs, land_refs, send, recv):
            cp.start()
        token[...] = jnp.zeros(token.shape, token.dtype)

    outs = pl.pallas_call(
        body, name=name,
        out_shape=[pltpu.SemaphoreType.DMA((ncopies,)), pltpu.SemaphoreType.DMA((ncopies,))]
        + [pltpu.HBM(a.shape, a.dtype) for a in srcs + lands] + [jax.ShapeDtypeStruct((8, 128), F32)],
        in_specs=[HBM_SPEC] * (n + m),
        out_specs=[SEM_SPEC, SEM_SPEC] + [HBM_SPEC] * (n + m) + [pl.BlockSpec(memory_space=pltpu.VMEM)],
        input_output_aliases={i: 2 + i for i in range(n + m)},
        compiler_params=pltpu.CompilerParams(has_side_effects=DATAFLOW),
    )(*srcs, *lands)
    return (outs[0], outs[1], outs[2:2 + n], outs[2 + n:2 + n + m], make_copies), outs[-1]


def _split_wait(name, state, after):
    send_sem, recv_sem, srcs, lands, make_copies = state
    n, m = len(srcs), len(lands)

    def body(*refs):
        src_refs, land_refs = refs[:n], refs[n:n + m]
        for cp in make_copies(src_refs, land_refs, refs[n + m], refs[n + m + 1]):
            cp.wait_send()
            cp.wait_recv()

    outs = pl.pallas_call(
        body, name=name,
        out_shape=[pltpu.HBM(a.shape, a.dtype) for a in list(srcs) + list(lands)],
        in_specs=[HBM_SPEC] * (n + m) + [SEM_SPEC, SEM_SPEC, ANY],
        out_specs=[HBM_SPEC] * (n + m),
        input_output_aliases={i: i for i in range(n + m)},
        compiler_params=pltpu.CompilerParams(has_side_effects=DATAFLOW),
    )(*srcs, *lands, send_sem, recv_sem, after)
    return outs[:n], outs[n:]


def _copies_to_sibling(src_of):
    def make(src_refs, land_refs, send, recv):
        x, y, c = _place()
        return [pltpu.make_async_remote_copy(src_of(src_refs[i], c), land_refs[i], send.at[i], recv.at[i],
                                             device_id=(x, y, 1 - c), device_id_type=MESH) for i in range(len(src_refs))]
    return make


def _copies_to_chips(src_refs, land_refs, send, recv):
    x, y, c = _place()
    return [pltpu.make_async_remote_copy(src_refs[i].at[2 * px + py], land_refs[i].at[j], send.at[3 * i + j], recv.at[3 * i + j],
                                         device_id=(px, py, c), device_id_type=MESH)
            for i in range(len(src_refs)) for j, (px, py) in enumerate(_other_chips(x, y))]


def _other_half_rows(ref, c):
    r2 = ref.shape[1] // 2
    return ref.at[:, pl.ds(pl.multiple_of((1 - c) * r2, 8), r2)]


class _ReduceScatter:
    def __init__(self, tag, grads, c1, me1):
        self.tag, self.grads, self.c1, self.me1 = tag, grads, c1, me1

    def start(self):
        lands = [lax.empty((g.shape[0], g.shape[1] // 2, g.shape[2]), F32) for g in self.grads]
        self.state, token = _split_start("rs_%s_a_start" % self.tag, self.grads, lands, len(self.grads),
                                         _copies_to_sibling(_other_half_rows))
        return token

    def to_chips(self, after):
        grads, recv = _split_wait("rs_%s_a_wait" % self.tag, self.state, after)
        added = [_add_halves(g, r, self.c1) for g, r in zip(grads, recv)]
        self.own = [pa for pa, _ in added]
        pabs = [pab for _, pab in added]
        lands = [lax.empty((3,) + p.shape[1:], BF16) for p in pabs]
        self.state, token = _split_start("rs_%s_b_start" % self.tag, pabs, lands, 3 * len(pabs), _copies_to_chips)
        return token

    def to_sibling(self, after):
        _, recv = _split_wait("rs_%s_b_wait" % self.tag, self.state, after)
        sums = [_sum_partials(pa, rb, self.me1) for pa, rb in zip(self.own, recv)]
        lands = [lax.empty(s.shape, F32) for s in sums]
        self.state, token = _split_start("rs_%s_c_start" % self.tag, sums, lands, len(sums),
                                         _copies_to_sibling(lambda ref, c: ref))
        return token

    def finish(self, after):
        return list(zip(*_split_wait("rs_%s_c_wait" % self.tag, self.state, after)))


def _allreduce_small(v):
    def body(v_ref, out_ref, rbuf, send_sems, recv_sems):
        x, y, c = _place()
        out_ref[...] = v_ref[...]
        for s, peer in enumerate([(x, y, 1 - c), (1 - x, y, c), (x, 1 - y, c)]):
            cp = pltpu.make_async_remote_copy(out_ref, rbuf.at[s], send_sems.at[s], recv_sems.at[s],
                                              device_id=peer, device_id_type=MESH)
            cp.start()
            cp.wait()
            out_ref[...] = out_ref[...] + rbuf[s]

    vm = pl.BlockSpec(memory_space=pltpu.VMEM)
    return pl.pallas_call(
        body, name="allreduce_small", in_specs=[vm], out_specs=vm, out_shape=jax.ShapeDtypeStruct(v.shape, v.dtype),
        scratch_shapes=[pltpu.VMEM((3,) + v.shape, v.dtype), pltpu.SemaphoreType.DMA((3,)), pltpu.SemaphoreType.DMA((3,))],
        compiler_params=pltpu.CompilerParams(vmem_limit_bytes=VMEM_LIMIT),
    )(v)


def _adamw_math(w, g, m, v):
    m = ADAM_B1 * m + (1.0 - ADAM_B1) * g
    v = ADAM_B2 * v + (1.0 - ADAM_B2) * jnp.square(g)
    m_hat = m / (1.0 - ADAM_B1 ** ADAM_STEP)
    v_hat = v / (1.0 - ADAM_B2 ** ADAM_STEP)
    return -ADAM_LR * (m_hat / (jnp.sqrt(v_hat) + ADAM_EPS) + ADAM_WD * w), m, v


def _adamw_big_layer(layer, w, m, v, own, sib, c1, prev):
    _, r, cd = w.shape
    r2 = r // 2

    def body(c_ref, w_ref, m_ref, v_ref, own_ref, sib_ref, *rest):
        g_ref, d_ref, mo_ref, vo_ref, token = rest[-5:]
        g = jnp.where(pl.program_id(0) == c_ref[0], own_ref[...], sib_ref[...])
        g_ref[...] = g
        d_ref[...], mo_ref[...], vo_ref[...] = _adamw_math(w_ref[...], g, m_ref[...], v_ref[...])
        token[...] = jnp.zeros(token.shape, F32)

    blk = pl.BlockSpec((None, r2, cd), lambda hh, c_ref: (layer, hh, 0))
    half = pl.BlockSpec((r2, cd), lambda hh, c_ref: (0, 0))
    prev = () if prev is None else tuple(prev)
    outs = pl.pallas_call(
        body, name="adamw_big",
        grid_spec=pltpu.PrefetchScalarGridSpec(
            num_scalar_prefetch=1, grid=(2,), in_specs=[blk, blk, blk, half, half] + [ANY] * len(prev),
            out_specs=[blk] * 4 + [pl.BlockSpec((8, 128), lambda hh, c_ref: (0, 0))]),
        out_shape=[jax.ShapeDtypeStruct(w.shape, F32)] * 4 + [jax.ShapeDtypeStruct((8, 128), F32)],
        input_output_aliases={6 + i: i for i in range(len(prev))},
        compiler_params=_params("arbitrary"),
    )(c1, w, m, v, own, sib, *prev)
    return outs[:4], outs[4]


def _adamw_small(ws, ms, vs, gpack, offsets, strides, me1):
    n = len(ws)

    def body(me_ref, *refs):
        w_refs, m_refs, v_refs = refs[:n], refs[n:2 * n], refs[2 * n:3 * n]
        g_ref = refs[3 * n]
        outs = refs[3 * n + 1:]
        for i in range(n):
            rows = w_refs[i].shape[0]
            if strides[i]:
                g = g_ref[pl.ds(pl.multiple_of(offsets[i] + me_ref[0] * strides[i], 8), rows), :]
            else:
                g = g_ref[offsets[i]:offsets[i] + rows, :]
            d, m, v = _adamw_math(w_refs[i][...], g, m_refs[i][...], v_refs[i][...])
            outs[4 * i][...] = g
            outs[4 * i + 1][...] = d
            outs[4 * i + 2][...] = m
            outs[4 * i + 3][...] = v

    full = lambda a: pl.BlockSpec(a.shape, lambda i, me_ref: (0, 0))
    ins = list(ws) + list(ms) + list(vs) + [gpack]
    out_arrs = [w for w in ws for _ in range(4)]
    outs = pl.pallas_call(
        body, name="adamw_small",
        grid_spec=pltpu.PrefetchScalarGridSpec(num_scalar_prefetch=1, grid=(1,), in_specs=[full(a) for a in ins],
                                               out_specs=[full(a) for a in out_arrs]),
        out_shape=[jax.ShapeDtypeStruct(a.shape, F32) for a in out_arrs],
        compiler_params=_params("arbitrary"),
    )(me1, *ins)
    return [outs[4 * i:4 * i + 4] for i in range(n)]


LANES = 128
SUBLANES = 8


def _rows_of(size):
    return -(-size // (LANES * SUBLANES)) * SUBLANES


def _as_rows(a, rows=None):
    flat = a.reshape(-1)
    rows = _rows_of(flat.size) if rows is None else rows
    return jnp.pad(flat, (0, rows * LANES - flat.size)).reshape(rows, LANES)


def _to_shard_major(name, full):
    if name == "meta_tokens":
        return full.reshape(N_META, N_CHIPS, -1).transpose(1, 0, 2)
    if name == "convb_pw_w":
        return full.reshape(2, N_CHIPS, -1, D_CONV).transpose(1, 0, 2, 3)
    return full.reshape(full.shape[0], full.shape[1], N_CHIPS, -1).transpose(2, 0, 1, 3)


class _GradientSchedule:
    GROUPS = {"l1": [(1, "w_down"), (1, "w_up"), (1, "w_out"), (1, "w_in")], "a0": [(0, "w_down"), (0, "w_up")],
              "b0": [(0, "w_out")], "c0": [(0, "w_in")]}
    PLAN = {
        (1, "dw_in"): [("l1", "start")],
        (1, "in_bwd"): [("l1", "to_chips")],
        (0, "mlp_bwd"): [("l1", "to_sibling")],
        (0, "dw_down"): [("l1", "finish")],
        (0, "dw_up"): [("a0", "start")],
        (0, "out_bwd"): [("a0", "to_chips")],
        (0, "dw_out"): [("b0", "start")],
        (0, "mixer_bwd"): [("a0", "to_sibling"), ("b0", "to_chips")],
        (0, "dw_in"): [("c0", "start"), ("a0", "finish"), ("b0", "to_sibling")],
        (0, "in_bwd"): [("c0", "to_chips"), ("b0", "finish")],
    }

    def __init__(self, w, mom, var, c1, me1):
        self.w, self.mom, self.var, self.c1, self.me1 = w, mom, var, c1, me1
        self.grads, self.chains, self.out = {}, {}, {}

    def grad(self, layer, name, g):
        self.grads[layer, name] = g

    def point(self, layer, kernel_name, after):
        return self.run(self.PLAN.get((layer, kernel_name), ()), after)

    def run(self, actions, after):
        deps = []
        for tag, stage in actions:
            if stage == "start":
                self.chains[tag] = _ReduceScatter(tag, [self.grads[lk] for lk in self.GROUPS[tag]], self.c1, self.me1)
                deps.append(self.chains[tag].start())
            elif stage == "finish":
                for (layer, k), (own, sib) in zip(self.GROUPS[tag], self.chains[tag].finish(after)):
                    self.out[k], token = _adamw_big_layer(layer, self.w[k], self.mom[k], self.var[k], own, sib, self.c1,
                                                          self.out.get(k))
                    deps.append(token)
            else:
                deps.append(getattr(self.chains[tag], stage)(after))
        return tuple(deps)


def _from_shard_major(name, sm):
    if name == "meta_tokens":
        return sm.transpose(1, 0, 2).reshape(N_META, -1)
    if name == "convb_pw_w":
        return sm.transpose(1, 0, 2, 3).reshape(2, -1, D_CONV)
    return sm.transpose(1, 2, 0, 3).reshape(sm.shape[1], sm.shape[2], -1)


def kernel(x, meta_tokens, mix_norm_g, w_in, pool_w, pool_scale, convb_dw_w, convb_dw_b, convb_ln_g, convb_ln_b, convb_pw_w, rg_conv_w, rg_conv_b, rg_w_a, rg_b_a, rg_w_x, rg_b_x, rg_lambda, w_out, mlp_norm_g, w_up, w_down, final_norm_g, loss_target, m_meta_tokens, m_mix_norm_g, m_w_in, m_pool_w, m_pool_scale, m_convb_dw_w, m_convb_dw_b, m_convb_ln_g, m_convb_ln_b, m_convb_pw_w, m_rg_conv_w, m_rg_conv_b, m_rg_w_a, m_rg_b_a, m_rg_w_x, m_rg_b_x, m_rg_lambda, m_w_out, m_mlp_norm_g, m_w_up, m_w_down, m_final_norm_g, v_meta_tokens, v_mix_norm_g, v_w_in, v_pool_w, v_pool_scale, v_convb_dw_w, v_convb_dw_b, v_convb_ln_g, v_convb_ln_b, v_convb_pw_w, v_rg_conv_w, v_rg_conv_b, v_rg_w_a, v_rg_b_a, v_rg_w_x, v_rg_b_x, v_rg_lambda, v_w_out, v_mlp_norm_g, v_w_up, v_w_down, v_final_norm_g):
    given = dict(locals())
    w = {k: given[k] for k in WEIGHTS}
    mom = {k: given["m_" + k] for k in WEIGHTS}
    var = {k: given["v_" + k] for k in WEIGHTS}
    xi, yi, ci = _place()
    me1 = (2 * xi + yi).astype(jnp.int32).reshape(1)
    c1 = ci.astype(jnp.int32).reshape(1)

    small_rows = [_rows_of(w[k].size) for k in SMALL_SHARDED]
    small_pack = jnp.concatenate([_as_rows(w[k]) for k in SMALL_SHARDED])
    shard = lambda l, k: w[k][l].astype(BF16)
    w_in0, small_all = _gather_now([shard(0, "w_in"), small_pack])
    order = [[(0, "w_out"), (0, "w_up")], [(0, "w_down")], [(1, "w_in"), (1, "w_out"), (1, "w_up"), (1, "w_down")]]
    state, token = _gather_start([[shard(l, k) for l, k in g] for g in order], me1[0])
    landed = {}

    def fetch(l, k, after):
        if (l, k) == (0, "w_in"):
            raw = w_in0
        else:
            gi = [i for i, g in enumerate(order) if (l, k) in g][0]
            if gi not in landed:
                landed[gi] = _gather_wait(state[gi], after, "gather_wait_%d" % gi)
            raw = landed[gi][order[gi].index((l, k))]
        if k == "w_in":
            return raw.transpose(1, 0, 2).reshape(D_MODEL, D_IN)
        return raw.reshape(D_MODEL, D_MODEL) if k == "w_out" else raw

    wfull = dict(w)
    off = 0
    for k, rows in zip(SMALL_SHARDED, small_rows):
        sm = small_all[:, off:off + rows].reshape(N_CHIPS, -1)[:, :w[k].size].reshape((N_CHIPS,) + w[k].shape)
        wfull[k] = _from_shard_major(k, sm)
        off += rows
    wfull["mix_norm_g"] = w["mix_norm_g"] + token[0, 0]

    seq = x.shape[1]
    t_real = N_META + seq
    t_pad = -(-t_real // ROW_ALIGN) * ROW_ALIGN
    tail = jnp.zeros((t_pad - t_real, D_MODEL), F32)
    h = jnp.concatenate([wfull["meta_tokens"], x[0], tail])
    tgt = jnp.concatenate([jnp.zeros((N_META, D_MODEL), F32), loss_target[0], tail])
    sched = _GradientSchedule(w, mom, var, c1, me1)
    loss, dh, gsmall = _local_step(h, tgt, t_real, wfull, fetch, sched)
    grad_x = dh[N_META:t_real][None]
    gsmall["meta_tokens"] = dh[:N_META]

    pieces, offsets, strides = [], {}, {}
    row = 0
    for k in SMALL_REPL:
        rows = _rows_of(w[k].size)
        pieces.append(_as_rows(gsmall[k], rows))
        offsets[k], strides[k] = row, 0
        row += rows
    for k in SMALL_SHARDED:
        rows = _rows_of(w[k].size)
        sm = _to_shard_major(k, gsmall[k]).reshape(N_CHIPS, -1)
        pieces.append(jnp.pad(sm, ((0, 0), (0, rows * LANES - sm.shape[1]))).reshape(N_CHIPS * rows, LANES))
        offsets[k], strides[k] = row, rows
        row += N_CHIPS * rows
    gpack = _allreduce_small(jnp.concatenate(pieces))
    sched.run([("c0", "to_sibling")], gpack)

    out = {}
    names = SMALL_REPL + SMALL_SHARDED
    as_rows = lambda a: a.reshape(-1, LANES) if a.size % (LANES * SUBLANES) == 0 or a.size < LANES * SUBLANES else _as_rows(a)
    res = _adamw_small([as_rows(w[k]) for k in names], [as_rows(mom[k]) for k in names], [as_rows(var[k]) for k in names],
                       gpack, [offsets[k] for k in names], [strides[k] for k in names], me1)
    for k, r4 in zip(names, res):
        out[k] = tuple(o.reshape(-1)[:w[k].size].reshape(w[k].shape) for o in r4)
    sched.run([("c0", "finish")], res[0][0])
    out.update(sched.out)

    loss = lax.psum(loss, ("x", "y", "c"))
    return (loss, grad_x, *[out[k][0] for k in WEIGHTS], *[out[k][1] for k in WEIGHTS],
            *[out[k][2] for k in WEIGHTS], *[out[k][3] for k in WEIGHTS])
```

```python
import functools

import jax
import jax.numpy as jnp
from jax import lax
from jax.experimental import pallas as pl
from jax.experimental.pallas import tpu as pltpu

F32, BF16 = jnp.float32, jnp.bfloat16
MESH = pl.DeviceIdType.MESH
ANY = pl.BlockSpec(memory_space=pl.ANY)

D_MODEL = 1024
N_META = 16
D_POOL = 256
D_CONV = 256
D_RNN = 512
D_IN = D_POOL + 2 * D_CONV + 2 * D_RNN
D_FF = 4096
FF_CHUNK = 1024
POOL_GW = 64
CONV_K = 31
RG_CONV_K = 4
RG_HD = 64
RG_C = 8.0
EPS = 1e-6
ADAM_LR, ADAM_B1, ADAM_B2, ADAM_EPS, ADAM_WD, ADAM_STEP = 0.001, 0.9, 0.999, 1e-08, 0.01, 10

HALO = 32
ROW_ALIGN = 256
TM_MIX = 256
TM_MAT = 768
TM_MLP_BWD = 384
N_CHIPS = 4
VMEM_LIMIT = 56 * 1024 * 1024

BIG = ("w_in", "w_out", "w_up", "w_down")
SMALL_SHARDED = ("meta_tokens", "convb_dw_w", "convb_pw_w", "rg_conv_w")
SMALL_REPL = ("mix_norm_g", "pool_w", "pool_scale", "convb_dw_b", "convb_ln_g", "convb_ln_b", "rg_conv_b",
              "rg_w_a", "rg_b_a", "rg_w_x", "rg_b_x", "rg_lambda", "mlp_norm_g", "final_norm_g")
WEIGHTS = ("meta_tokens", "mix_norm_g", "w_in", "pool_w", "pool_scale", "convb_dw_w", "convb_dw_b", "convb_ln_g",
           "convb_ln_b", "convb_pw_w", "rg_conv_w", "rg_conv_b", "rg_w_a", "rg_b_a", "rg_w_x", "rg_b_x",
           "rg_lambda", "w_out", "mlp_norm_g", "w_up", "w_down", "final_norm_g")


def _params(*sem):
    return pltpu.CompilerParams(dimension_semantics=sem, vmem_limit_bytes=VMEM_LIMIT)


def _row_tile(t, cap):
    best = None
    for tm in range(128, cap + 1, 128):
        if t % tm == 0:
            best = tm
    assert best is not None, (t, cap)
    return best


def _dot(a, b):
    return jnp.dot(a, b, preferred_element_type=F32)


def _dot_nt(a, b):
    return lax.dot_general(a, b, (((1,), (1,)), ((), ())), preferred_element_type=F32)


def _dot_tn(a, b):
    return lax.dot_general(a, b, (((0,), (0,)), ((), ())), preferred_element_type=F32)


def _rms(x):
    r = lax.rsqrt(jnp.mean(x * x, axis=-1, keepdims=True) + EPS)
    return r, x * r


def _rms_bwd(du, n, r, g):
    dn = du * g
    return r * (dn - n * jnp.mean(dn * n, axis=-1, keepdims=True))


def _sig(x):
    return jax.nn.sigmoid(x)


def _colsum(x):
    return jnp.sum(x, axis=0, keepdims=True)


def _one_minus_sq(a, log_a):
    x = 2.0 * log_a
    series = -x * (1.0 + x * (0.5 + x * (1.0 / 6 + x * (1.0 / 24 + x * (1.0 / 120)))))
    return jnp.where(x > -0.05, series, 1.0 - a * a)


_GELU_K0 = 0.7978845608028654
_GELU_K1 = 0.044715


def _gelu_and_grad(x):
    th = jnp.tanh(_GELU_K0 * (x + _GELU_K1 * x * x * x))
    val = 0.5 * x * (1.0 + th)
    grad = 0.5 * (1.0 + th) + 0.5 * x * (1.0 - th * th) * _GELU_K0 * (1.0 + 3.0 * _GELU_K1 * x * x)
    return val, grad


def _full(a):
    nd = a.ndim
    return pl.BlockSpec(a.shape, lambda *_: (0,) * nd)


def _resident(a):
    nd = a.ndim
    return pl.BlockSpec(a.shape, lambda *_: (0,) * nd, pipeline_mode=pl.Buffered(1))


def _after(body, n_in, deps):
    def wrapped(*refs):
        return body(*refs[:n_in], *refs[n_in + len(deps):])
    return wrapped


def _lane_sel(lane, a2, a4, a8, a16):
    return jnp.where(lane < POOL_GW, a2, jnp.where(lane < 2 * POOL_GW, a4, jnp.where(lane < 3 * POOL_GW, a8, a16)))


def _window_sums_back(src, tmp_a, tmp_b, tm):
    n = HALO + tm
    rows = lambda ref, lo, back: ref[pl.ds(lo - back, n - lo), :]
    tmp_a[pl.ds(8, n - 8), :] = rows(src, 8, 0) + rows(src, 8, 1)
    tmp_b[pl.ds(16, n - 16), :] = rows(tmp_a, 16, 0) + rows(tmp_a, 16, 2)
    s2 = rows(tmp_a, HALO, 0)
    tmp_a[pl.ds(24, n - 24), :] = rows(tmp_b, 24, 0) + rows(tmp_b, 24, 4)
    s8 = rows(tmp_a, HALO, 0)
    return s2, rows(tmp_b, HALO, 0), s8, s8 + rows(tmp_a, HALO, 8)


def _window_sums_ahead(src, tmp_a, tmp_b, tm):
    rows = lambda ref, n, ahead: ref[pl.ds(ahead, n), :]
    tmp_a[pl.ds(0, tm + 24), :] = rows(src, tm + 24, 0) + rows(src, tm + 24, 1)
    tmp_b[pl.ds(0, tm + 16), :] = rows(tmp_a, tm + 16, 0) + rows(tmp_a, tm + 16, 2)
    s2 = rows(tmp_a, tm, 0)
    tmp_a[pl.ds(0, tm + 8), :] = rows(tmp_b, tm + 8, 0) + rows(tmp_b, tm + 8, 4)
    s8 = rows(tmp_a, tm, 0)
    return s2, rows(tmp_b, tm, 0), s8, s8 + rows(tmp_a, tm, 8)


def _pool_counts(tm, t0):
    lane = lax.broadcasted_iota(jnp.int32, (tm, D_POOL), 1)
    row = lax.broadcasted_iota(jnp.int32, (tm, D_POOL), 0) + t0
    cnt = jnp.minimum(row + 1, _lane_sel(lane, 2, 4, 8, 16)).astype(F32)
    return lane, cnt


def _pool_fwd(ext_q, tmp_a, tmp_b, tm, t0):
    lane, cnt = _pool_counts(tm, t0)
    q = ext_q[pl.ds(HALO, tm), :]
    pooled = _lane_sel(lane, *_window_sums_back(ext_q, tmp_a, tmp_b, tm)) / cnt - q
    return pooled, lane, cnt


def _taps(src, w_of, offs, tm, zbuf):
    acc = None
    for r in range(8):
        ks = [k for k in range(len(offs)) if offs[k] % 8 == r]
        if not ks:
            continue
        rows = tm + (8 if r else 0)
        z = w_of(ks[0]) * src[pl.ds(offs[ks[0]] - r, rows), :]
        for k in ks[1:]:
            z = z + w_of(k) * src[pl.ds(offs[k] - r, rows), :]
        if r:
            zbuf[...] = z
            z = zbuf[pl.ds(r, tm), :]
        acc = z if acc is None else acc + z
    return acc


def _tap_grads(d_pad, src, offs, tm, g_ref, zbuf):
    ch = src.shape[-1]
    for r in range(8):
        ks = [k for k in range(len(offs)) if offs[k] % 8 == r]
        if not ks:
            continue
        rows = tm + (8 if r else 0)
        if r:
            zbuf[...] = d_pad[pl.ds(8 - r, rows), :]
        for k in ks:
            d = zbuf[...] if r else d_pad[pl.ds(8, rows), :]
            prod = d * src[pl.ds(offs[k] - r, rows), :]
            g_ref[k] += jnp.sum(prod.reshape(rows // 8, 8, ch), axis=0)


_CONV_OFFS = [HALO - (CONV_K - 1) + k for k in range(CONV_K)]


def _conv_fwd(ext_u, dww_ref, dwb, tm, zbuf):
    return dwb + _taps(ext_u, lambda k: dww_ref[k:k + 1, :], _CONV_OFFS, tm, zbuf)


def _ln_silu(c, lng, lnb):
    mu = jnp.mean(c, axis=-1, keepdims=True)
    cc = c - mu
    rstd = lax.rsqrt(jnp.mean(cc * cc, axis=-1, keepdims=True) + EPS)
    z = cc * rstd
    l = z * lng + lnb
    sl = _sig(l)
    return z, rstd, l, sl, l * sl


def _rg_fwd(ext_x, cw_ref, cb, wa, ba, wx, bx, lam, tm):
    xc = cb + cw_ref[0:1, :] * ext_x[pl.ds(HALO - (RG_CONV_K - 1), tm), :]
    for k in range(1, RG_CONV_K):
        xc = xc + cw_ref[k:k + 1, :] * ext_x[pl.ds(HALO - (RG_CONV_K - 1) + k, tm), :]
    xcb = xc.astype(BF16)
    r = _sig(_dot(xcb, wa) + ba)
    ig = _sig(_dot(xcb, wx) + bx)
    sp = jnp.maximum(-lam, 0.0) + jnp.log(1.0 + jnp.exp(-jnp.abs(lam)))
    log_a = (-RG_C * r) * sp
    a = jnp.exp(log_a)
    m = jnp.sqrt(_one_minus_sq(a, log_a))
    return xc, xcb, r, ig, sp, a, m


def _scan_rows(a_ref, b_ref, out_ref, carry, tm, reverse):
    rows = lax.broadcasted_iota(jnp.int32, (8, D_RNN), 0)
    ngrp = tm // 8

    def grp(gi, hb):
        st = pl.multiple_of((ngrp - 1 - gi if reverse else gi) * 8, 8)
        a8 = a_ref[pl.ds(st, 8), :]
        b8 = b_ref[pl.ds(st, 8), :]
        out = jnp.zeros((8, D_RNN), F32)
        for j in (range(7, -1, -1) if reverse else range(8)):
            aj = jnp.broadcast_to(a8[j:j + 1, :], (8, D_RNN))
            bj = jnp.broadcast_to(b8[j:j + 1, :], (8, D_RNN))
            if reverse:
                cur = bj + hb
                hb = aj * cur
            else:
                cur = aj * hb + bj
                hb = cur
            out = jnp.where(rows == j, cur, out)
        out_ref[pl.ds(st, 8), :] = out
        return hb

    carry[...] = lax.fori_loop(0, ngrp, grp, carry[...])


_MIX_W = ("wp", "psc", "dww", "dwb", "lng", "lnb", "wpw", "cw", "cb", "wa", "ba", "wx", "bx", "lam")


def _mixer_fwd(p, mw):
    t = p.shape[0]
    tm = _row_tile(t, TM_MIX)

    def body(p_ref, wp, psc, dww, dwb, lng, lnb, wpw, cw, cb, wa, ba, wx, bx, lam, y_ref, hs_ref, conv_ref,
             ext_q, ext_u, ext_x, tmp_a, tmp_b, zbuf, a_s, b_s, hcar):
        i = pl.program_id(0)

        @pl.when(i == 0)
        def _():
            ext_q[0:HALO, :] = jnp.zeros((HALO, D_POOL), F32)
            ext_u[0:HALO, :] = jnp.zeros((HALO, D_CONV), F32)
            ext_x[0:HALO, :] = jnp.zeros((HALO, D_RNN), F32)
            hcar[...] = jnp.zeros((8, D_RNN), F32)

        ext_q[pl.ds(HALO, tm), :] = p_ref[:, 0:256]
        pooled, _, _ = _pool_fwd(ext_q, tmp_a, tmp_b, tm, i * tm)
        y_ref[:, 0:256] = (_dot(pooled.astype(BF16), wp[...]) * psc[...]).astype(BF16)

        ext_u[pl.ds(HALO, tm), :] = p_ref[:, 256:512] * _sig(p_ref[:, 512:768])
        conv = _conv_fwd(ext_u, dww, dwb[...], tm, zbuf)
        conv_ref[...] = conv
        act = _ln_silu(conv, lng[...], lnb[...])[4]
        y_ref[:, 256:512] = _dot(act.astype(BF16), wpw[...]).astype(BF16)

        ext_x[pl.ds(HALO, tm), :] = p_ref[:, 1280:1792]
        xc, _, _, ig, _, a, m = _rg_fwd(ext_x, cw, cb[...], wa[...], ba[...], wx[...], bx[...], lam[...], tm)
        a_s[...] = a
        b_s[...] = m * (ig * xc)
        _scan_rows(a_s, b_s, hs_ref, hcar, tm, reverse=False)
        y_ref[:, 512:1024] = (_gelu_and_grad(p_ref[:, 768:1280])[0] * hs_ref[...]).astype(BF16)

        ext_q[0:HALO, :] = ext_q[pl.ds(tm, HALO), :]
        ext_u[0:HALO, :] = ext_u[pl.ds(tm, HALO), :]
        ext_x[0:HALO, :] = ext_x[pl.ds(tm, HALO), :]

    ws = [mw[k] for k in _MIX_W]
    return pl.pallas_call(
        body, name="mixer_fwd", grid=(t // tm,),
        in_specs=[pl.BlockSpec((tm, D_IN), lambda i: (i, 0))] + [_full(w) for w in ws],
        out_specs=[pl.BlockSpec((tm, D_MODEL), lambda i: (i, 0)), pl.BlockSpec((tm, D_RNN), lambda i: (i, 0)),
                   pl.BlockSpec((tm, D_CONV), lambda i: (i, 0))],
        out_shape=[jax.ShapeDtypeStruct((t, D_MODEL), BF16), jax.ShapeDtypeStruct((t, D_RNN), F32),
                   jax.ShapeDtypeStruct((t, D_CONV), F32)],
        scratch_shapes=[pltpu.VMEM((HALO + tm, D_POOL), F32), pltpu.VMEM((HALO + tm, D_CONV), F32),
                        pltpu.VMEM((HALO + tm, D_RNN), F32), pltpu.VMEM((HALO + tm, D_POOL), F32),
                        pltpu.VMEM((HALO + tm, D_POOL), F32), pltpu.VMEM((tm + 8, D_CONV), F32),
                        pltpu.VMEM((tm, D_RNN), F32), pltpu.VMEM((tm, D_RNN), F32), pltpu.VMEM((8, D_RNN), F32)],
        compiler_params=_params("arbitrary"),
    )(p, *ws)


_MIX_G = (("wp", (D_POOL, D_POOL)), ("psc", (1, D_POOL)), ("dww", (32, 8, D_CONV)), ("dwb", (1, D_CONV)),
          ("lng", (1, D_CONV)), ("lnb", (1, D_CONV)), ("wpw", (D_CONV, D_CONV)), ("cw", (8, D_RNN)),
          ("cb", (1, D_RNN)), ("wa", (D_RNN, D_RNN)), ("ba", (1, D_RNN)), ("wx", (D_RNN, D_RNN)),
          ("bx", (1, D_RNN)), ("lam", (1, D_RNN)))


def _mixer_bwd(p, dy, hs, conv, mw, deps=()):
    t = p.shape[0]
    tm = _row_tile(t, TM_MIX)
    nt = t // tm
    hb = tm // HALO

    def body(p_ref, ph_ref, dy_ref, hs_ref, hsh_ref, conv_ref, wp, psc, dww, dwb, lng, lnb, wpw, cw, cb, wa, ba, wx, bx, lam,
             dp_ref, g_wp, g_psc, g_dww, g_dwb, g_lng, g_lnb, g_wpw, g_cw, g_cb, g_wa, g_ba, g_wx, g_bx, g_lam,
             ext_q, ext_u, ext_x, ext_h, ee, dc_s, dx_s, tmp_a, tmp_b, zbuf, d_pad, a_s, b_s, g_s, gcar):
        step = pl.program_id(0)
        i = nt - 1 - step
        grads = (g_wp, g_psc, g_dww, g_dwb, g_lng, g_lnb, g_wpw, g_cw, g_cb, g_wa, g_ba, g_wx, g_bx, g_lam)

        @pl.when(step == 0)
        def _():
            for gr in grads:
                gr[...] = jnp.zeros(gr.shape, F32)
            ee[pl.ds(tm, HALO), :] = jnp.zeros((HALO, D_POOL), F32)
            dc_s[pl.ds(tm, HALO), :] = jnp.zeros((HALO, D_CONV), F32)
            dx_s[pl.ds(tm, HALO), :] = jnp.zeros((HALO, D_RNN), F32)
            d_pad[0:8, :] = jnp.zeros((8, D_CONV), F32)
            d_pad[pl.ds(tm + 8, 8), :] = jnp.zeros((8, D_CONV), F32)
            gcar[...] = jnp.zeros((8, D_RNN), F32)

        hm = jnp.where(i == 0, 0.0, 1.0)

        ext_q[0:HALO, :] = ph_ref[:, 0:256] * hm
        ext_q[pl.ds(HALO, tm), :] = p_ref[:, 0:256]
        pooled, lane, cnt = _pool_fwd(ext_q, tmp_a, tmp_b, tm, i * tm)
        pooled_b = pooled.astype(BF16)
        dya = dy_ref[:, 0:256]
        g_psc[...] += _colsum(dya * _dot(pooled_b, wp[...]))
        dmixed_b = (dya * psc[...]).astype(BF16)
        dpooled = _dot_nt(dmixed_b, wp[...])
        g_wp[...] += _dot_tn(pooled_b, dmixed_b)
        ee[0:tm, :] = dpooled / cnt
        dp_ref[:, 0:256] = _lane_sel(lane, *_window_sums_ahead(ee, tmp_a, tmp_b, tm)) - dpooled
        ee[pl.ds(tm, HALO), :] = ee[0:HALO, :]

        v = p_ref[:, 256:512]
        s = _sig(p_ref[:, 512:768])
        ext_u[0:HALO, :] = ph_ref[:, 256:512] * _sig(ph_ref[:, 512:768]) * hm
        ext_u[pl.ds(HALO, tm), :] = v * s
        z, rstd, l, sl, act = _ln_silu(conv_ref[...], lng[...], lnb[...])
        dyb_b = dy_ref[:, 256:512].astype(BF16)
        dact = _dot_nt(dyb_b, wpw[...])
        g_wpw[...] += _dot_tn(act.astype(BF16), dyb_b)
        dl = dact * (sl * (1.0 + l * (1.0 - sl)))
        g_lng[...] += _colsum(dl * z)
        g_lnb[...] += _colsum(dl)
        dz = dl * lng[...]
        dc = rstd * (dz - jnp.mean(dz, axis=-1, keepdims=True) - z * jnp.mean(dz * z, axis=-1, keepdims=True))
        g_dwb[...] += _colsum(dc)
        dc_s[0:tm, :] = dc
        d_pad[pl.ds(8, tm), :] = dc
        _tap_grads(d_pad, ext_u, _CONV_OFFS, tm, g_dww, zbuf)
        du0 = _taps(dc_s, lambda j: dww[CONV_K - 1 - j:CONV_K - j, :], list(range(CONV_K)), tm, zbuf)
        dp_ref[:, 256:512] = du0 * s
        dp_ref[:, 512:768] = du0 * v * (s * (1.0 - s))
        dc_s[pl.ds(tm, HALO), :] = dc_s[0:HALO, :]

        ext_x[0:HALO, :] = ph_ref[:, 1280:1792] * hm
        ext_x[pl.ds(HALO, tm), :] = p_ref[:, 1280:1792]
        xc, xcb, r, ig, sp, a, m = _rg_fwd(ext_x, cw, cb[...], wa[...], ba[...], wx[...], bx[...], lam[...], tm)
        ext_h[0:HALO, :] = hsh_ref[...] * hm
        ext_h[pl.ds(HALO, tm), :] = hs_ref[...]
        dyc = dy_ref[:, 512:1024]
        gl, dgl = _gelu_and_grad(p_ref[:, 768:1280])
        dp_ref[:, 768:1280] = dyc * hs_ref[...] * dgl
        a_s[...] = a
        b_s[...] = dyc * gl
        _scan_rows(a_s, b_s, g_s, gcar, tm, reverse=True)
        g = g_s[...]
        da = g * ext_h[pl.ds(HALO - 1, tm), :]
        dm = g * (ig * xc)
        dig = g * (m * xc)
        dlog_a = da * a - dm * (a * a) / m
        g_lam[...] += _colsum(dlog_a * (-RG_C * r)) * (-_sig(-lam[...]))
        dra = (dlog_a * (-RG_C * sp)) * (r * (1.0 - r))
        dia = dig * (ig * (1.0 - ig))
        g_ba[...] += _colsum(dra)
        g_bx[...] += _colsum(dia)
        dra_b = dra.astype(BF16)
        dia_b = dia.astype(BF16)
        dxc = g * (m * ig) + _dot_nt(dra_b, wa[...]) + _dot_nt(dia_b, wx[...])
        g_wa[...] += _dot_tn(xcb, dra_b)
        g_wx[...] += _dot_tn(xcb, dia_b)
        g_cb[...] += _colsum(dxc)
        dx_s[0:tm, :] = dxc
        for k in range(RG_CONV_K):
            g_cw[k:k + 1, :] += _colsum(dxc * ext_x[pl.ds(HALO - (RG_CONV_K - 1) + k, tm), :])
        dxin = cw[RG_CONV_K - 1:RG_CONV_K, :] * dxc
        for j in range(1, RG_CONV_K):
            dxin = dxin + cw[RG_CONV_K - 1 - j:RG_CONV_K - j, :] * dx_s[pl.ds(j, tm), :]
        dp_ref[:, 1280:1792] = dxin
        dx_s[pl.ds(tm, HALO), :] = dx_s[0:HALO, :]

    ws = [mw[k] for k in _MIX_W]
    tile = lambda w: pl.BlockSpec((tm, w), lambda s: (nt - 1 - s, 0))
    halo = lambda w: pl.BlockSpec((HALO, w), lambda s: (jnp.maximum((nt - 1 - s) * hb - 1, 0), 0))
    outs = pl.pallas_call(
        _after(body, 6 + len(ws), deps), name="mixer_bwd", grid=(nt,),
        in_specs=[tile(D_IN), halo(D_IN), tile(D_MODEL), tile(D_RNN), halo(D_RNN), tile(D_CONV)] + [_full(w) for w in ws]
        + [ANY] * len(deps),
        out_specs=[tile(D_IN)] + [pl.BlockSpec(shp, lambda s, nd=len(shp): (0,) * nd) for _, shp in _MIX_G],
        out_shape=[jax.ShapeDtypeStruct((t, D_IN), F32)] + [jax.ShapeDtypeStruct(shp, F32) for _, shp in _MIX_G],
        scratch_shapes=[pltpu.VMEM((HALO + tm, D_POOL), F32), pltpu.VMEM((HALO + tm, D_CONV), F32),
                        pltpu.VMEM((HALO + tm, D_RNN), F32), pltpu.VMEM((HALO + tm, D_RNN), F32),
                        pltpu.VMEM((tm + HALO, D_POOL), F32), pltpu.VMEM((tm + HALO, D_CONV), F32),
                        pltpu.VMEM((tm + HALO, D_RNN), F32), pltpu.VMEM((HALO + tm, D_POOL), F32),
                        pltpu.VMEM((HALO + tm, D_POOL), F32), pltpu.VMEM((tm + 8, D_CONV), F32),
                        pltpu.VMEM((tm + 16, D_CONV), F32), pltpu.VMEM((tm, D_RNN), F32),
                        pltpu.VMEM((tm, D_RNN), F32), pltpu.VMEM((tm, D_RNN), F32), pltpu.VMEM((8, D_RNN), F32)],
        compiler_params=_params("arbitrary"),
    )(p, p, dy, hs, hs, conv, *ws, *deps)
    return outs[0], {k: o for (k, _), o in zip(_MIX_G, outs[1:])}


def _in_proj(h, g, w):
    t = h.shape[0]
    tm = _row_tile(t, TM_MAT)

    def body(h_ref, g_ref, w_ref, p_ref, u_ref):
        u = (_rms(h_ref[...])[1] * g_ref[...]).astype(BF16)
        u_ref[...] = u
        p_ref[...] = _dot(u, w_ref[...])

    return pl.pallas_call(
        body, name="in_proj", grid=(t // tm,),
        in_specs=[pl.BlockSpec((tm, D_MODEL), lambda i: (i, 0)), _full(g), _resident(w)],
        out_specs=[pl.BlockSpec((tm, D_IN), lambda i: (i, 0)), pl.BlockSpec((tm, D_MODEL), lambda i: (i, 0))],
        out_shape=[jax.ShapeDtypeStruct((t, D_IN), F32), jax.ShapeDtypeStruct((t, D_MODEL), BF16)],
        compiler_params=_params("parallel"),
    )(h, g, w)


def _mid_fwd(y, h0, w_out, g, w_up):
    t = h0.shape[0]
    tm = _row_tile(t, TM_MAT)

    def body(y_ref, h0_ref, wo_ref, g_ref, wu_ref, h1_ref, u2_ref, f_ref):
        h1 = h0_ref[...] + _dot(y_ref[...], wo_ref[...])
        h1_ref[...] = h1
        u2 = (_rms(h1)[1] * g_ref[...]).astype(BF16)
        u2_ref[...] = u2
        for c in range(D_FF // FF_CHUNK):
            f_ref[:, c * FF_CHUNK:(c + 1) * FF_CHUNK] = _dot(u2, wu_ref[c]).astype(BF16)

    row = lambda w: pl.BlockSpec((tm, w), lambda i: (i, 0))
    return pl.pallas_call(
        body, name="mid_fwd", grid=(t // tm,),
        in_specs=[row(D_MODEL), row(D_MODEL), _resident(w_out), _full(g), _resident(w_up)],
        out_specs=[row(D_MODEL), row(D_MODEL), row(D_FF)],
        out_shape=[jax.ShapeDtypeStruct((t, D_MODEL), F32), jax.ShapeDtypeStruct((t, D_MODEL), BF16),
                   jax.ShapeDtypeStruct((t, D_FF), BF16)],
        compiler_params=_params("parallel"),
    )(y, h0, w_out, g, w_up)


def _down_fwd(f, h1, w_down):
    t = h1.shape[0]
    tm = _row_tile(t, TM_MAT)

    def body(f_ref, h1_ref, wd_ref, h2_ref):
        acc = h1_ref[...]
        for c in range(D_FF // FF_CHUNK):
            cols = slice(c * FF_CHUNK, (c + 1) * FF_CHUNK)
            a = jnp.square(jnp.maximum(f_ref[:, cols].astype(F32), 0.0)).astype(BF16)
            acc = acc + _dot(a, wd_ref[cols, :])
        h2_ref[...] = acc

    return pl.pallas_call(
        body, name="down_fwd", grid=(t // tm,),
        in_specs=[pl.BlockSpec((tm, D_FF), lambda i: (i, 0)), pl.BlockSpec((tm, D_MODEL), lambda i: (i, 0)), _resident(w_down)],
        out_specs=pl.BlockSpec((tm, D_MODEL), lambda i: (i, 0)),
        out_shape=jax.ShapeDtypeStruct((t, D_MODEL), F32),
        compiler_params=_params("parallel"),
    )(f, h1, w_down)


def _loss_head(h, g, tgt, t_real):
    t = h.shape[0]
    tm = _row_tile(t, TM_MAT)

    def body(h_ref, g_ref, tgt_ref, loss_ref, dh_ref, dg_ref):
        i = pl.program_id(0)

        @pl.when(i == 0)
        def _():
            loss_ref[...] = jnp.zeros(loss_ref.shape, F32)
            dg_ref[...] = jnp.zeros(dg_ref.shape, F32)

        r, n = _rms(h_ref[...])
        row = lax.broadcasted_iota(jnp.int32, (tm, 1), 0) + i * tm
        valid = jnp.logical_and(row >= N_META, row < t_real)
        diff = jnp.where(valid, n * g_ref[...] - tgt_ref[...], 0.0)
        loss_ref[...] += 0.5 * jnp.sum(jnp.mean(diff * diff, axis=-1, keepdims=True))
        dy = diff * (1.0 / D_MODEL)
        dg_ref[...] += _colsum(dy * n)
        dh_ref[...] = _rms_bwd(dy, n, r, g_ref[...])

    return pl.pallas_call(
        body, name="loss_head", grid=(t // tm,),
        in_specs=[pl.BlockSpec((tm, D_MODEL), lambda i: (i, 0)), _full(g), pl.BlockSpec((tm, D_MODEL), lambda i: (i, 0))],
        out_specs=[pl.BlockSpec((8, 128), lambda i: (0, 0)), pl.BlockSpec((tm, D_MODEL), lambda i: (i, 0)),
                   pl.BlockSpec((1, D_MODEL), lambda i: (0, 0))],
        out_shape=[jax.ShapeDtypeStruct((8, 128), F32), jax.ShapeDtypeStruct((t, D_MODEL), F32),
                   jax.ShapeDtypeStruct((1, D_MODEL), F32)],
        compiler_params=_params("arbitrary"),
    )(h, g, tgt)


def _mlp_bwd(dh2, f, h1, g, w_up, w_down, deps=()):
    t = dh2.shape[0]
    tm = _row_tile(t, TM_MLP_BWD)

    def body(dh2_ref, f_ref, wd_ref, wu_ref, h1_ref, g_ref, df_ref, dh1_ref, dg_ref):
        @pl.when(pl.program_id(0) == 0)
        def _():
            dg_ref[...] = jnp.zeros(dg_ref.shape, F32)

        dh2 = dh2_ref[...]
        dhb = dh2.astype(BF16)
        du2 = None
        for c in range(D_FF // FF_CHUNK):
            cols = slice(c * FF_CHUNK, (c + 1) * FF_CHUNK)
            dact = _dot_nt(dhb, wd_ref[c])
            df = (dact * (2.0 * jnp.maximum(f_ref[:, cols].astype(F32), 0.0))).astype(BF16)
            df_ref[:, cols] = df
            part = _dot_nt(df, wu_ref[c])
            du2 = part if du2 is None else du2 + part
        r, n = _rms(h1_ref[...])
        dg_ref[...] += _colsum(du2 * n)
        dh1_ref[...] = dh2 + _rms_bwd(du2, n, r, g_ref[...])

    row = lambda w: pl.BlockSpec((tm, w), lambda i: (i, 0))
    return pl.pallas_call(
        _after(body, 6, deps), name="mlp_bwd", grid=(t // tm,),
        in_specs=[row(D_MODEL), row(D_FF), _resident(w_down), _resident(w_up), row(D_MODEL), _full(g)] + [ANY] * len(deps),
        out_specs=[row(D_FF), row(D_MODEL), pl.BlockSpec((1, D_MODEL), lambda i: (0, 0))],
        out_shape=[jax.ShapeDtypeStruct((t, D_FF), BF16), jax.ShapeDtypeStruct((t, D_MODEL), F32),
                   jax.ShapeDtypeStruct((1, D_MODEL), F32)],
        compiler_params=_params("arbitrary"),
    )(dh2, f, w_down, w_up, h1, g, *deps)


def _out_bwd(dh1, w_out, deps=()):
    t = dh1.shape[0]
    tm = _row_tile(t, TM_MAT)

    def body(dh_ref, w_ref, dy_ref):
        dy_ref[...] = _dot_nt(dh_ref[...].astype(BF16), w_ref[...])

    return pl.pallas_call(
        _after(body, 2, deps), name="out_bwd", grid=(t // tm,),
        in_specs=[pl.BlockSpec((tm, D_MODEL), lambda i: (i, 0)), _resident(w_out)] + [ANY] * len(deps),
        out_specs=pl.BlockSpec((tm, D_MODEL), lambda i: (i, 0)),
        out_shape=jax.ShapeDtypeStruct((t, D_MODEL), F32),
        compiler_params=_params("parallel"),
    )(dh1, w_out, *deps)


def _in_bwd(dp, dh1, h0, g, w_in, deps=()):
    t = dp.shape[0]
    tm = _row_tile(t, TM_MAT)

    def body(dp_ref, dh1_ref, h0_ref, g_ref, w_ref, dh0_ref, dg_ref):
        @pl.when(pl.program_id(0) == 0)
        def _():
            dg_ref[...] = jnp.zeros(dg_ref.shape, F32)

        du = _dot_nt(dp_ref[...].astype(BF16), w_ref[...])
        r, n = _rms(h0_ref[...])
        dg_ref[...] += _colsum(du * n)
        dh0_ref[...] = dh1_ref[...] + _rms_bwd(du, n, r, g_ref[...])

    row = lambda w: pl.BlockSpec((tm, w), lambda i: (i, 0))
    return pl.pallas_call(
        _after(body, 5, deps), name="in_bwd", grid=(t // tm,),
        in_specs=[row(D_IN), row(D_MODEL), row(D_MODEL), _full(g), _resident(w_in)] + [ANY] * len(deps),
        out_specs=[row(D_MODEL), pl.BlockSpec((1, D_MODEL), lambda i: (0, 0))],
        out_shape=[jax.ShapeDtypeStruct((t, D_MODEL), F32), jax.ShapeDtypeStruct((1, D_MODEL), F32)],
        compiler_params=_params("arbitrary"),
    )(dp, dh1, h0, g, w_in, *deps)


def _tn_matmul(a, b, kc, nc, relu2, name, deps=()):
    t, k = a.shape
    n = b.shape[1]
    tt = _row_tile(t, TM_MAT)
    gk, gn = k // kc, n // nc

    def body(a_ref, b_ref, o_ref):
        @pl.when(pl.program_id(2) == 0)
        def _():
            o_ref[...] = jnp.zeros(o_ref.shape, F32)

        av = a_ref[...]
        if relu2:
            av = jnp.square(jnp.maximum(av.astype(F32), 0.0))
        o_ref[...] += _dot_tn(av.astype(BF16), b_ref[...].astype(BF16))

    return pl.pallas_call(
        _after(body, 2, deps), name=name, grid=(gk, gn, t // tt),
        in_specs=[pl.BlockSpec((tt, kc), lambda ik, jn, it: (it, ik)), pl.BlockSpec((tt, nc), lambda ik, jn, it: (it, jn))]
        + [ANY] * len(deps),
        out_specs=pl.BlockSpec((None, kc, nc), lambda ik, jn, it: (ik * gn + jn, 0, 0)),
        out_shape=jax.ShapeDtypeStruct((gk * gn, kc, nc), F32),
        compiler_params=_params("parallel", "parallel", "arbitrary"),
    )(a, b, *deps)


def _block_diag(blocks):
    nb, hd, _ = blocks.shape
    eye = jnp.eye(nb, dtype=blocks.dtype)
    return (blocks[:, :, None, :] * eye[:, None, :, None]).reshape(nb * hd, nb * hd)


def _diag_blocks(m, nb):
    hd = m.shape[0] // nb
    eye = jnp.eye(nb, dtype=m.dtype)
    return jnp.sum(m.reshape(nb, hd, nb, hd) * eye[:, None, :, None], axis=2)


def _mixer_weights(w, l):
    row = lambda a: a.reshape(1, -1)
    return dict(
        wp=_block_diag(w["pool_w"][l]).astype(BF16), psc=row(w["pool_scale"][l]),
        dww=jnp.pad(w["convb_dw_w"][l], ((0, 32 - CONV_K), (0, 0))), dwb=row(w["convb_dw_b"][l]),
        lng=row(w["convb_ln_g"][l]), lnb=row(w["convb_ln_b"][l]), wpw=w["convb_pw_w"][l].astype(BF16),
        cw=jnp.pad(w["rg_conv_w"][l], ((0, 8 - RG_CONV_K), (0, 0))), cb=row(w["rg_conv_b"][l]),
        wa=_block_diag(w["rg_w_a"][l]).astype(BF16), ba=row(w["rg_b_a"][l]),
        wx=_block_diag(w["rg_w_x"][l]).astype(BF16), bx=row(w["rg_b_x"][l]), lam=row(w["rg_lambda"][l]))


def _local_step(h, tgt, t_real, w, fetch, hooks):
    depth = 2
    saved = []
    big = []
    for l in range(depth):
        mw = _mixer_weights(w, l)
        g1 = w["mix_norm_g"][l].reshape(1, -1)
        g2 = w["mlp_norm_g"][l].reshape(1, -1)
        wl = dict(w_in=fetch(l, "w_in", h))
        p, u = _in_proj(h, g1, wl["w_in"])
        y, hs, conv = _mixer_fwd(p, mw)
        wl["w_out"], wl["w_up"] = fetch(l, "w_out", y), fetch(l, "w_up", y)
        h1, u2, f = _mid_fwd(y, h, wl["w_out"], g2, wl["w_up"])
        wl["w_down"] = fetch(l, "w_down", f)
        h2 = _down_fwd(f, h1, wl["w_down"].reshape(D_FF, D_MODEL))
        saved.append(dict(mw=mw, g1=g1, g2=g2, h0=h, p=p, u=u, y=y, hs=hs, conv=conv, h1=h1, u2=u2, f=f))
        big.append(wl)
        h = h2
    gf = w["final_norm_g"].reshape(1, -1)
    loss, dh, dgf = _loss_head(h, gf, tgt, t_real)

    gs = {k: [None] * depth for k in ("mix_norm_g", "mlp_norm_g", "pool_w", "pool_scale", "convb_dw_w", "convb_dw_b",
                                      "convb_ln_g", "convb_ln_b", "convb_pw_w", "rg_conv_w", "rg_conv_b", "rg_w_a",
                                      "rg_b_a", "rg_w_x", "rg_b_x", "rg_lambda")}
    deps = ()
    for l in reversed(range(depth)):
        s, wl = saved[l], big[l]
        df, dh1, dg2 = _mlp_bwd(dh, s["f"], s["h1"], s["g2"], wl["w_up"], wl["w_down"], deps)
        deps = hooks.point(l, "mlp_bwd", dh1)
        g_down = _tn_matmul(s["f"], dh, FF_CHUNK, D_MODEL, True, "dw_down", deps)
        hooks.grad(l, "w_down", g_down)
        deps = hooks.point(l, "dw_down", g_down)
        g_up = _tn_matmul(s["u2"], df, D_MODEL, FF_CHUNK, False, "dw_up", deps)
        hooks.grad(l, "w_up", g_up)
        deps = hooks.point(l, "dw_up", g_up)
        dy = _out_bwd(dh1, wl["w_out"], deps)
        deps = hooks.point(l, "out_bwd", dy)
        g_out = _tn_matmul(s["y"], dh1, D_MODEL, D_MODEL, False, "dw_out", deps)
        hooks.grad(l, "w_out", g_out.reshape(N_CHIPS, D_MODEL // N_CHIPS, D_MODEL))
        deps = hooks.point(l, "dw_out", g_out)
        dp, mg = _mixer_bwd(s["p"], dy, s["hs"], s["conv"], s["mw"], deps)
        deps = hooks.point(l, "mixer_bwd", dp)
        g_in = _tn_matmul(s["u"], dp, D_MODEL, D_IN, False, "dw_in", deps)
        g_in = g_in[0].reshape(D_MODEL, N_CHIPS, D_IN // N_CHIPS).transpose(1, 0, 2)
        hooks.grad(l, "w_in", g_in)
        deps = hooks.point(l, "dw_in", g_in)
        dh, dg1 = _in_bwd(dp, dh1, s["h0"], s["g1"], wl["w_in"], deps)
        deps = hooks.point(l, "in_bwd", dh)
        gs["mix_norm_g"][l] = dg1[0]
        gs["mlp_norm_g"][l] = dg2[0]
        gs["pool_w"][l] = _diag_blocks(mg["wp"], D_POOL // POOL_GW)
        gs["pool_scale"][l] = mg["psc"][0]
        gs["convb_dw_w"][l] = jnp.sum(mg["dww"][:CONV_K], axis=1)
        gs["convb_dw_b"][l] = mg["dwb"][0]
        gs["convb_ln_g"][l] = mg["lng"][0]
        gs["convb_ln_b"][l] = mg["lnb"][0]
        gs["convb_pw_w"][l] = mg["wpw"]
        gs["rg_conv_w"][l] = mg["cw"][:RG_CONV_K]
        gs["rg_conv_b"][l] = mg["cb"][0]
        gs["rg_w_a"][l] = _diag_blocks(mg["wa"], D_RNN // RG_HD)
        gs["rg_b_a"][l] = mg["ba"][0]
        gs["rg_w_x"][l] = _diag_blocks(mg["wx"], D_RNN // RG_HD)
        gs["rg_b_x"][l] = mg["bx"][0]
        gs["rg_lambda"][l] = mg["lam"][0]
    gsmall = {k: jnp.stack(v) for k, v in gs.items()}
    gsmall["final_norm_g"] = dgf[0]
    return loss[0, 0], dh, gsmall


def _place():
    return lax.axis_index("x"), lax.axis_index("y"), lax.axis_index("c")


def _other_chips(x, y):
    return [(1 - x, y), (x, 1 - y), (1 - x, 1 - y)]


def _gather_now(srcs):
    ns = len(srcs)
    out_shape = [jax.ShapeDtypeStruct((N_CHIPS,) + s.shape, s.dtype) for s in srcs]

    def body(*refs):
        src_refs, dst_refs = refs[:ns], refs[ns:2 * ns]
        send_sems, recv_sems, loc_sems = refs[2 * ns:]
        x, y, c = _place()
        me = 2 * x + y
        local, remote = [], []
        for n in range(ns):
            cp = pltpu.make_async_copy(src_refs[n], dst_refs[n].at[me], loc_sems.at[n])
            cp.start()
            local.append(cp)
            for j, (px, py) in enumerate(_other_chips(x, y)):
                out = pltpu.make_async_remote_copy(src_refs[n], dst_refs[n].at[me], send_sems.at[3 * n + j],
                                                   recv_sems.at[3 * n + j], device_id=(px, py, c), device_id_type=MESH)
                out.start()
                remote.append(pltpu.make_async_remote_copy(src_refs[n], dst_refs[n].at[2 * px + py], send_sems.at[3 * n + j],
                                                           recv_sems.at[3 * n + j], device_id=(px, py, c), device_id_type=MESH))
        for cp in remote:
            cp.wait()
        for cp in local:
            cp.wait()

    return pl.pallas_call(
        body, name="gather_now", in_specs=[ANY] * ns, out_specs=[ANY] * ns, out_shape=out_shape,
        scratch_shapes=[pltpu.SemaphoreType.DMA((3 * ns,)), pltpu.SemaphoreType.DMA((3 * ns,)), pltpu.SemaphoreType.DMA((ns,))],
    )(*srcs)


HBM_SPEC = pl.BlockSpec(memory_space=pltpu.HBM)
SEM_SPEC = pl.BlockSpec(memory_space=pltpu.SEMAPHORE)
DATAFLOW = pltpu.SideEffectType.DATAFLOW_SIDE_EFFECTING


def _gather_copies(src_refs, land_refs, send_sem, recv_sem, first):
    x, y, c = _place()
    me = 2 * x + y
    out = []
    for n in range(len(src_refs)):
        for j, (px, py) in enumerate(_other_chips(x, y)):
            out.append(pltpu.make_async_remote_copy(src_refs[n], land_refs[n].at[me], send_sem.at[first + 3 * n + j],
                                                    recv_sem.at[first + 3 * n + j], device_id=(px, py, c), device_id_type=MESH))
    return out


def _gather_start(groups, me):
    srcs = [pltpu.with_memory_space_constraint(s, pltpu.HBM) for g in groups for s in g]
    lands = [pltpu.with_memory_space_constraint(
        lax.dynamic_update_slice(jnp.zeros((N_CHIPS,) + s.shape, s.dtype), s[None], (me,) + (0,) * s.ndim), pltpu.HBM)
        for g in groups for s in g]
    n, ng = len(srcs), len(groups)
    first = [sum(len(g) for g in groups[:i]) for i in range(ng)]

    def body(*refs):
        src_refs, land_refs = refs[:n], refs[n:2 * n]
        sems = refs[2 * n:2 * n + 2 * ng]
        token = refs[-1]
        for gi, g in enumerate(groups):
            lo, hi = first[gi], first[gi] + len(g)
            for cp in _gather_copies(src_refs[lo:hi], land_refs[lo:hi], sems[2 * gi], sems[2 * gi + 1], 0):
                cp.start()
        token[...] = jnp.zeros(token.shape, token.dtype)

    sem_shapes = [pltpu.SemaphoreType.DMA((3 * len(g),)) for g in groups for _ in range(2)]
    outs = pl.pallas_call(
        body, name="gather_start",
        out_shape=sem_shapes + [pltpu.HBM(a.shape, a.dtype) for a in srcs + lands] + [jax.ShapeDtypeStruct((8, 128), F32)],
        in_specs=[HBM_SPEC] * (2 * n),
        out_specs=[SEM_SPEC] * (2 * ng) + [HBM_SPEC] * (2 * n) + [pl.BlockSpec(memory_space=pltpu.VMEM)],
        input_output_aliases={i: 2 * ng + i for i in range(2 * n)},
        compiler_params=pltpu.CompilerParams(has_side_effects=DATAFLOW),
    )(*srcs, *lands)
    sems, thru, token = outs[:2 * ng], outs[2 * ng:2 * ng + 2 * n], outs[-1]
    state = []
    for gi, g in enumerate(groups):
        lo, hi = first[gi], first[gi] + len(g)
        state.append((sems[2 * gi], sems[2 * gi + 1], thru[lo:hi], thru[n + lo:n + hi]))
    return state, token


def _gather_wait(state, after, name):
    send_sem, recv_sem, srcs, lands = state
    n = len(srcs)

    def body(*refs):
        src_refs, land_refs = refs[:n], refs[n:2 * n]
        send, recv = refs[2 * n], refs[2 * n + 1]
        for cp in _gather_copies(src_refs, land_refs, send, recv, 0):
            cp.wait_send()
            cp.wait_recv()

    outs = pl.pallas_call(
        body, name=name,
        out_shape=[pltpu.HBM(a.shape, a.dtype) for a in list(srcs) + list(lands)],
        in_specs=[HBM_SPEC] * (2 * n) + [SEM_SPEC, SEM_SPEC, ANY],
        out_specs=[HBM_SPEC] * (2 * n),
        input_output_aliases={i: i for i in range(2 * n)},
        compiler_params=pltpu.CompilerParams(has_side_effects=DATAFLOW),
    )(*srcs, *lands, send_sem, recv_sem, after)
    return outs[n:]


def _add_halves(g, recv, c1):
    nk, r, cd = g.shape
    r2 = r // 2

    def body(c_ref, g_ref, r_ref, pa_ref, pab_ref):
        s = g_ref[...] + r_ref[...]
        pa_ref[...] = s
        pab_ref[...] = s.astype(BF16)

    blk = pl.BlockSpec((None, r2, cd), lambda k, c_ref: (k, 0, 0))
    return pl.pallas_call(
        body, name="rs_add_halves",
        grid_spec=pltpu.PrefetchScalarGridSpec(
            num_scalar_prefetch=1, grid=(nk,),
            in_specs=[pl.BlockSpec((None, r2, cd), lambda k, c_ref: (k, c_ref[0], 0)), blk], out_specs=[blk, blk]),
        out_shape=[jax.ShapeDtypeStruct((nk, r2, cd), F32), jax.ShapeDtypeStruct((nk, r2, cd), BF16)],
        compiler_params=_params("parallel"),
    )(c1, g, recv)


def _sum_partials(pa, recv, me1):
    nk, r2, cd = pa.shape

    def body(me_ref, pa_ref, r_ref, s_ref):
        s_ref[...] = ((pa_ref[...] + r_ref[0].astype(F32)) + r_ref[1].astype(F32)) + r_ref[2].astype(F32)

    return pl.pallas_call(
        body, name="rs_sum_partials",
        grid_spec=pltpu.PrefetchScalarGridSpec(
            num_scalar_prefetch=1, grid=(1,),
            in_specs=[pl.BlockSpec((None, r2, cd), lambda i, me_ref: (me_ref[0], 0, 0)),
                      pl.BlockSpec((3, r2, cd), lambda i, me_ref: (0, 0, 0))],
            out_specs=pl.BlockSpec((r2, cd), lambda i, me_ref: (0, 0))),
        out_shape=jax.ShapeDtypeStruct((r2, cd), F32),
        compiler_params=_params("arbitrary"),
    )(me1, pa, recv)


def _split_start(name, srcs, lands, ncopies, make_copies):
    srcs = [pltpu.with_memory_space_constraint(s, pltpu.HBM) for s in srcs]
    lands = [pltpu.with_memory_space_constraint(a, pltpu.HBM) for a in lands]
    n, m = len(srcs), len(lands)

    def body(*refs):
        src_refs, land_refs = refs[:n], refs[n:n + m]
        send, recv, token = refs[n + m], refs[n + m + 1], refs[-1]
        for cp in make_copies(src_refs, land_refs, send, recv):
            cp.start()
        token[...] = jnp.zeros(token.shape, token.dtype)

    outs = pl.pallas_call(
        body, name=name,
        out_shape=[pltpu.SemaphoreType.DMA((ncopies,)), pltpu.SemaphoreType.DMA((ncopies,))]
        + [pltpu.HBM(a.shape, a.dtype) for a in srcs + lands] + [jax.ShapeDtypeStruct((8, 128), F32)],
        in_specs=[HBM_SPEC] * (n + m),
        out_specs=[SEM_SPEC, SEM_SPEC] + [HBM_SPEC] * (n + m) + [pl.BlockSpec(memory_space=pltpu.VMEM)],
        input_output_aliases={i: 2 + i for i in range(n + m)},
        compiler_params=pltpu.CompilerParams(has_side_effects=DATAFLOW),
    )(*srcs, *lands)
    return (outs[0], outs[1], outs[2:2 + n], outs[2 + n:2 + n + m], make_copies), outs[-1]


def _split_wait(name, state, after):
    send_sem, recv_sem, srcs, lands, make_copies = state
    n, m = len(srcs), len(lands)

    def body(*refs):
        src_refs, land_refs = refs[:n], refs[n:n + m]
        for cp in make_copies(src_refs, land_refs, refs[n + m], refs[n + m + 1]):
            cp.wait_send()
            cp.wait_recv()

    outs = pl.pallas_call(
        body, name=name,
        out_shape=[pltpu.HBM(a.shape, a.dtype) for a in list(srcs) + list(lands)],
        in_specs=[HBM_SPEC] * (n + m) + [SEM_SPEC, SEM_SPEC, ANY],
        out_specs=[HBM_SPEC] * (n + m),
        input_output_aliases={i: i for i in range(n + m)},
        compiler_params=pltpu.CompilerParams(has_side_effects=DATAFLOW),
    )(*srcs, *lands, send_sem, recv_sem, after)
    return outs[:n], outs[n:]


def _copies_to_sibling(src_of):
    def make(src_refs, land_refs, send, recv):
        x, y, c = _place()
        return [pltpu.make_async_remote_copy(src_of(src_refs[i], c), land_refs[i], send.at[i], recv.at[i],
                                             device_id=(x, y, 1 - c), device_id_type=MESH) for i in range(len(src_refs))]
    return make


def _copies_to_chips(src_refs, land_refs, send, recv):
    x, y, c = _place()
    return [pltpu.make_async_remote_copy(src_refs[i].at[2 * px + py], land_refs[i].at[j], send.at[3 * i + j], recv.at[3 * i + j],
                                         device_id=(px, py, c), device_id_type=MESH)
            for i in range(len(src_refs)) for j, (px, py) in enumerate(_other_chips(x, y))]


def _other_half_rows(ref, c):
    r2 = ref.shape[1] // 2
    return ref.at[:, pl.ds(pl.multiple_of((1 - c) * r2, 8), r2)]


class _ReduceScatter:
    def __init__(self, tag, grads, c1, me1):
        self.tag, self.grads, self.c1, self.me1 = tag, grads, c1, me1

    def start(self):
        lands = [lax.empty((g.shape[0], g.shape[1] // 2, g.shape[2]), F32) for g in self.grads]
        self.state, token = _split_start("rs_%s_a_start" % self.tag, self.grads, lands, len(self.grads),
                                         _copies_to_sibling(_other_half_rows))
        return token

    def to_chips(self, after):
        grads, recv = _split_wait("rs_%s_a_wait" % self.tag, self.state, after)
        added = [_add_halves(g, r, self.c1) for g, r in zip(grads, recv)]
        self.own = [pa for pa, _ in added]
        pabs = [pab for _, pab in added]
        lands = [lax.empty((3,) + p.shape[1:], BF16) for p in pabs]
        self.state, token = _split_start("rs_%s_b_start" % self.tag, pabs, lands, 3 * len(pabs), _copies_to_chips)
        return token

    def to_sibling(self, after):
        _, recv = _split_wait("rs_%s_b_wait" % self.tag, self.state, after)
        sums = [_sum_partials(pa, rb, self.me1) for pa, rb in zip(self.own, recv)]
        lands = [lax.empty(s.shape, F32) for s in sums]
        self.state, token = _split_start("rs_%s_c_start" % self.tag, sums, lands, len(sums),
                                         _copies_to_sibling(lambda ref, c: ref))
        return token

    def finish(self, after):
        return list(zip(*_split_wait("rs_%s_c_wait" % self.tag, self.state, after)))


def _allreduce_small(v):
    def body(v_ref, out_ref, rbuf, send_sems, recv_sems):
        x, y, c = _place()
        out_ref[...] = v_ref[...]
        for s, peer in enumerate([(x, y, 1 - c), (1 - x, y, c), (x, 1 - y, c)]):
            cp = pltpu.make_async_remote_copy(out_ref, rbuf.at[s], send_sems.at[s], recv_sems.at[s],
                                              device_id=peer, device_id_type=MESH)
            cp.start()
            cp.wait()
            out_ref[...] = out_ref[...] + rbuf[s]

    vm = pl.BlockSpec(memory_space=pltpu.VMEM)
    return pl.pallas_call(
        body, name="allreduce_small", in_specs=[vm], out_specs=vm, out_shape=jax.ShapeDtypeStruct(v.shape, v.dtype),
        scratch_shapes=[pltpu.VMEM((3,) + v.shape, v.dtype), pltpu.SemaphoreType.DMA((3,)), pltpu.SemaphoreType.DMA((3,))],
        compiler_params=pltpu.CompilerParams(vmem_limit_bytes=VMEM_LIMIT),
    )(v)


def _adamw_math(w, g, m, v):
    m = ADAM_B1 * m + (1.0 - ADAM_B1) * g
    v = ADAM_B2 * v + (1.0 - ADAM_B2) * jnp.square(g)
    m_hat = m / (1.0 - ADAM_B1 ** ADAM_STEP)
    v_hat = v / (1.0 - ADAM_B2 ** ADAM_STEP)
    return -ADAM_LR * (m_hat / (jnp.sqrt(v_hat) + ADAM_EPS) + ADAM_WD * w), m, v


def _adamw_big_layer(layer, w, m, v, own, sib, c1, prev):
    _, r, cd = w.shape
    r2 = r // 2

    def body(c_ref, w_ref, m_ref, v_ref, own_ref, sib_ref, *rest):
        g_ref, d_ref, mo_ref, vo_ref, token = rest[-5:]
        g = jnp.where(pl.program_id(0) == c_ref[0], own_ref[...], sib_ref[...])
        g_ref[...] = g
        d_ref[...], mo_ref[...], vo_ref[...] = _adamw_math(w_ref[...], g, m_ref[...], v_ref[...])
        token[...] = jnp.zeros(token.shape, F32)

    blk = pl.BlockSpec((None, r2, cd), lambda hh, c_ref: (layer, hh, 0))
    half = pl.BlockSpec((r2, cd), lambda hh, c_ref: (0, 0))
    prev = () if prev is None else tuple(prev)
    outs = pl.pallas_call(
        body, name="adamw_big",
        grid_spec=pltpu.PrefetchScalarGridSpec(
            num_scalar_prefetch=1, grid=(2,), in_specs=[blk, blk, blk, half, half] + [ANY] * len(prev),
            out_specs=[blk] * 4 + [pl.BlockSpec((8, 128), lambda hh, c_ref: (0, 0))]),
        out_shape=[jax.ShapeDtypeStruct(w.shape, F32)] * 4 + [jax.ShapeDtypeStruct((8, 128), F32)],
        input_output_aliases={6 + i: i for i in range(len(prev))},
        compiler_params=_params("arbitrary"),
    )(c1, w, m, v, own, sib, *prev)
    return outs[:4], outs[4]


def _adamw_small(ws, ms, vs, gpack, offsets, strides, me1):
    n = len(ws)

    def body(me_ref, *refs):
        w_refs, m_refs, v_refs = refs[:n], refs[n:2 * n], refs[2 * n:3 * n]
        g_ref = refs[3 * n]
        outs = refs[3 * n + 1:]
        for i in range(n):
            rows = w_refs[i].shape[0]
            if strides[i]:
                g = g_ref[pl.ds(pl.multiple_of(offsets[i] + me_ref[0] * strides[i], 8), rows), :]
            else:
                g = g_ref[offsets[i]:offsets[i] + rows, :]
            d, m, v = _adamw_math(w_refs[i][...], g, m_refs[i][...], v_refs[i][...])
            outs[4 * i][...] = g
            outs[4 * i + 1][...] = d
            outs[4 * i + 2][...] = m
            outs[4 * i + 3][...] = v

    full = lambda a: pl.BlockSpec(a.shape, lambda i, me_ref: (0, 0))
    ins = list(ws) + list(ms) + list(vs) + [gpack]
    out_arrs = [w for w in ws for _ in range(4)]
    outs = pl.pallas_call(
        body, name="adamw_small",
        grid_spec=pltpu.PrefetchScalarGridSpec(num_scalar_prefetch=1, grid=(1,), in_specs=[full(a) for a in ins],
                                               out_specs=[full(a) for a in out_arrs]),
        out_shape=[jax.ShapeDtypeStruct(a.shape, F32) for a in out_arrs],
        compiler_params=_params("arbitrary"),
    )(me1, *ins)
    return [outs[4 * i:4 * i + 4] for i in range(n)]


LANES = 128
SUBLANES = 8


def _rows_of(size):
    return -(-size // (LANES * SUBLANES)) * SUBLANES


def _as_rows(a, rows=None):
    flat = a.reshape(-1)
    rows = _rows_of(flat.size) if rows is None else rows
    return jnp.pad(flat, (0, rows * LANES - flat.size)).reshape(rows, LANES)


def _to_shard_major(name, full):
    if name == "meta_tokens":
        return full.reshape(N_META, N_CHIPS, -1).transpose(1, 0, 2)
    if name == "convb_pw_w":
        return full.reshape(2, N_CHIPS, -1, D_CONV).transpose(1, 0, 2, 3)
    return full.reshape(full.shape[0], full.shape[1], N_CHIPS, -1).transpose(2, 0, 1, 3)


class _GradientSchedule:
    GROUPS = {"l1": [(1, "w_down"), (1, "w_up"), (1, "w_out"), (1, "w_in")], "a0": [(0, "w_down"), (0, "w_up")],
              "b0": [(0, "w_out")], "c0": [(0, "w_in")]}
    PLAN = {
        (1, "dw_in"): [("l1", "start")],
        (1, "in_bwd"): [("l1", "to_chips")],
        (0, "mlp_bwd"): [("l1", "to_sibling")],
        (0, "dw_down"): [("l1", "finish")],
        (0, "dw_up"): [("a0", "start")],
        (0, "out_bwd"): [("a0", "to_chips")],
        (0, "dw_out"): [("b0", "start")],
        (0, "mixer_bwd"): [("a0", "to_sibling"), ("b0", "to_chips")],
        (0, "dw_in"): [("c0", "start"), ("a0", "finish"), ("b0", "to_sibling")],
        (0, "in_bwd"): [("c0", "to_chips"), ("b0", "finish")],
    }

    def __init__(self, w, mom, var, c1, me1):
        self.w, self.mom, self.var, self.c1, self.me1 = w, mom, var, c1, me1
        self.grads, self.chains, self.out = {}, {}, {}

    def grad(self, layer, name, g):
        self.grads[layer, name] = g

    def point(self, layer, kernel_name, after):
        return self.run(self.PLAN.get((layer, kernel_name), ()), after)

    def run(self, actions, after):
        deps = []
        for tag, stage in actions:
            if stage == "start":
                self.chains[tag] = _ReduceScatter(tag, [self.grads[lk] for lk in self.GROUPS[tag]], self.c1, self.me1)
                deps.append(self.chains[tag].start())
            elif stage == "finish":
                for (layer, k), (own, sib) in zip(self.GROUPS[tag], self.chains[tag].finish(after)):
                    self.out[k], token = _adamw_big_layer(layer, self.w[k], self.mom[k], self.var[k], own, sib, self.c1,
                                                          self.out.get(k))
                    deps.append(token)
            else:
                deps.append(getattr(self.chains[tag], stage)(after))
        return tuple(deps)


def _from_shard_major(name, sm):
    if name == "meta_tokens":
        return sm.transpose(1, 0, 2).reshape(N_META, -1)
    if name == "convb_pw_w":
        return sm.transpose(1, 0, 2, 3).reshape(2, -1, D_CONV)
    return sm.transpose(1, 2, 0, 3).reshape(sm.shape[1], sm.shape[2], -1)


def kernel(x, meta_tokens, mix_norm_g, w_in, pool_w, pool_scale, convb_dw_w, convb_dw_b, convb_ln_g, convb_ln_b, convb_pw_w, rg_conv_w, rg_conv_b, rg_w_a, rg_b_a, rg_w_x, rg_b_x, rg_lambda, w_out, mlp_norm_g, w_up, w_down, final_norm_g, loss_target, m_meta_tokens, m_mix_norm_g, m_w_in, m_pool_w, m_pool_scale, m_convb_dw_w, m_convb_dw_b, m_convb_ln_g, m_convb_ln_b, m_convb_pw_w, m_rg_conv_w, m_rg_conv_b, m_rg_w_a, m_rg_b_a, m_rg_w_x, m_rg_b_x, m_rg_lambda, m_w_out, m_mlp_norm_g, m_w_up, m_w_down, m_final_norm_g, v_meta_tokens, v_mix_norm_g, v_w_in, v_pool_w, v_pool_scale, v_convb_dw_w, v_convb_dw_b, v_convb_ln_g, v_convb_ln_b, v_convb_pw_w, v_rg_conv_w, v_rg_conv_b, v_rg_w_a, v_rg_b_a, v_rg_w_x, v_rg_b_x, v_rg_lambda, v_w_out, v_mlp_norm_g, v_w_up, v_w_down, v_final_norm_g):
    given = dict(locals())
    w = {k: given[k] for k in WEIGHTS}
    mom = {k: given["m_" + k] for k in WEIGHTS}
    var = {k: given["v_" + k] for k in WEIGHTS}
    xi, yi, ci = _place()
    me1 = (2 * xi + yi).astype(jnp.int32).reshape(1)
    c1 = ci.astype(jnp.int32).reshape(1)

    small_rows = [_rows_of(w[k].size) for k in SMALL_SHARDED]
    small_pack = jnp.concatenate([_as_rows(w[k]) for k in SMALL_SHARDED])
    shard = lambda l, k: w[k][l].astype(BF16)
    w_in0, small_all = _gather_now([shard(0, "w_in"), small_pack])
    order = [[(0, "w_out"), (0, "w_up")], [(0, "w_down")], [(1, "w_in"), (1, "w_out"), (1, "w_up"), (1, "w_down")]]
    state, token = _gather_start([[shard(l, k) for l, k in g] for g in order], me1[0])
    landed = {}

    def fetch(l, k, after):
        if (l, k) == (0, "w_in"):
            raw = w_in0
        else:
            gi = [i for i, g in enumerate(order) if (l, k) in g][0]
            if gi not in landed:
                landed[gi] = _gather_wait(state[gi], after, "gather_wait_%d" % gi)
            raw = landed[gi][order[gi].index((l, k))]
        if k == "w_in":
            return raw.transpose(1, 0, 2).reshape(D_MODEL, D_IN)
        return raw.reshape(D_MODEL, D_MODEL) if k == "w_out" else raw

    wfull = dict(w)
    off = 0
    for k, rows in zip(SMALL_SHARDED, small_rows):
        sm = small_all[:, off:off + rows].reshape(N_CHIPS, -1)[:, :w[k].size].reshape((N_CHIPS,) + w[k].shape)
        wfull[k] = _from_shard_major(k, sm)
        off += rows
    wfull["mix_norm_g"] = w["mix_norm_g"] + token[0, 0]

    seq = x.shape[1]
    t_real = N_META + seq
    t_pad = -(-t_real // ROW_ALIGN) * ROW_ALIGN
    tail = jnp.zeros((t_pad - t_real, D_MODEL), F32)
    h = jnp.concatenate([wfull["meta_tokens"], x[0], tail])
    tgt = jnp.concatenate([jnp.zeros((N_META, D_MODEL), F32), loss_target[0], tail])
    sched = _GradientSchedule(w, mom, var, c1, me1)
    loss, dh, gsmall = _local_step(h, tgt, t_real, wfull, fetch, sched)
    grad_x = dh[N_META:t_real][None]
    gsmall["meta_tokens"] = dh[:N_META]

    pieces, offsets, strides = [], {}, {}
    row = 0
    for k in SMALL_REPL:
        rows = _rows_of(w[k].size)
        pieces.append(_as_rows(gsmall[k], rows))
        offsets[k], strides[k] = row, 0
        row += rows
    for k in SMALL_SHARDED:
        rows = _rows_of(w[k].size)
        sm = _to_shard_major(k, gsmall[k]).reshape(N_CHIPS, -1)
        pieces.append(jnp.pad(sm, ((0, 0), (0, rows * LANES - sm.shape[1]))).reshape(N_CHIPS * rows, LANES))
        offsets[k], strides[k] = row, rows
        row += N_CHIPS * rows
    gpack = _allreduce_small(jnp.concatenate(pieces))
    sched.run([("c0", "to_sibling")], gpack)

    out = {}
    names = SMALL_REPL + SMALL_SHARDED
    as_rows = lambda a: a.reshape(-1, LANES) if a.size % (LANES * SUBLANES) == 0 or a.size < LANES * SUBLANES else _as_rows(a)
    res = _adamw_small([as_rows(w[k]) for k in names], [as_rows(mom[k]) for k in names], [as_rows(var[k]) for k in names],
                       gpack, [offsets[k] for k in names], [strides[k] for k in names], me1)
    for k, r4 in zip(names, res):
        out[k] = tuple(o.reshape(-1)[:w[k].size].reshape(w[k].shape) for o in r4)
    sched.run([("c0", "finish")], res[0][0])
    out.update(sched.out)

    loss = lax.psum(loss, ("x", "y", "c"))
    return (loss, grad_x, *[out[k][0] for k in WEIGHTS], *[out[k][1] for k in WEIGHTS],
            *[out[k][2] for k in WEIGHTS], *[out[k][3] for k in WEIGHTS])
```

```python
import functools

import jax
import jax.numpy as jnp
from jax import lax
from jax.experimental import pallas as pl
from jax.experimental.pallas import tpu as pltpu

F32, BF16 = jnp.float32, jnp.bfloat16
MESH = pl.DeviceIdType.MESH
ANY = pl.BlockSpec(memory_space=pl.ANY)

D_MODEL = 1024
N_META = 16
D_POOL = 256
D_CONV = 256
D_RNN = 512
D_IN = D_POOL + 2 * D_CONV + 2 * D_RNN
D_FF = 4096
FF_CHUNK = 1024
POOL_GW = 64
CONV_K = 31
RG_CONV_K = 4
RG_HD = 64
RG_C = 8.0
EPS = 1e-6
ADAM_LR, ADAM_B1, ADAM_B2, ADAM_EPS, ADAM_WD, ADAM_STEP = 0.001, 0.9, 0.999, 1e-08, 0.01, 10

HALO = 32
ROW_ALIGN = 256
TM_MIX = 256
TM_MAT = 768
TM_MLP_BWD = 384
N_CHIPS = 4
VMEM_LIMIT = 56 * 1024 * 1024

BIG = ("w_in", "w_out", "w_up", "w_down")
SMALL_SHARDED = ("meta_tokens", "convb_dw_w", "convb_pw_w", "rg_conv_w")
SMALL_REPL = ("mix_norm_g", "pool_w", "pool_scale", "convb_dw_b", "convb_ln_g", "convb_ln_b", "rg_conv_b",
              "rg_w_a", "rg_b_a", "rg_w_x", "rg_b_x", "rg_lambda", "mlp_norm_g", "final_norm_g")
WEIGHTS = ("meta_tokens", "mix_norm_g", "w_in", "pool_w", "pool_scale", "convb_dw_w", "convb_dw_b", "convb_ln_g",
           "convb_ln_b", "convb_pw_w", "rg_conv_w", "rg_conv_b", "rg_w_a", "rg_b_a", "rg_w_x", "rg_b_x",
           "rg_lambda", "w_out", "mlp_norm_g", "w_up", "w_down", "final_norm_g")


def _params(*sem):
    return pltpu.CompilerParams(dimension_semantics=sem, vmem_limit_bytes=VMEM_LIMIT)


def _row_tile(t, cap):
    best = None
    for tm in range(128, cap + 1, 128):
        if t % tm == 0:
            best = tm
    assert best is not None, (t, cap)
    return best


def _dot(a, b):
    return jnp.dot(a, b, preferred_element_type=F32)


def _dot_nt(a, b):
    return lax.dot_general(a, b, (((1,), (1,)), ((), ())), preferred_element_type=F32)


def _dot_tn(a, b):
    return lax.dot_general(a, b, (((0,), (0,)), ((), ())), preferred_element_type=F32)


def _rms(x):
    r = lax.rsqrt(jnp.mean(x * x, axis=-1, keepdims=True) + EPS)
    return r, x * r


def _rms_bwd(du, n, r, g):
    dn = du * g
    return r * (dn - n * jnp.mean(dn * n, axis=-1, keepdims=True))


def _sig(x):
    return jax.nn.sigmoid(x)


def _colsum(x):
    return jnp.sum(x, axis=0, keepdims=True)


def _one_minus_sq(a, log_a):
    x = 2.0 * log_a
    series = -x * (1.0 + x * (0.5 + x * (1.0 / 6 + x * (1.0 / 24 + x * (1.0 / 120)))))
    return jnp.where(x > -0.05, series, 1.0 - a * a)


_GELU_K0 = 0.7978845608028654
_GELU_K1 = 0.044715


def _gelu_and_grad(x):
    th = jnp.tanh(_GELU_K0 * (x + _GELU_K1 * x * x * x))
    val = 0.5 * x * (1.0 + th)
    grad = 0.5 * (1.0 + th) + 0.5 * x * (1.0 - th * th) * _GELU_K0 * (1.0 + 3.0 * _GELU_K1 * x * x)
    return val, grad


def _full(a):
    nd = a.ndim
    return pl.BlockSpec(a.shape, lambda *_: (0,) * nd)


def _resident(a):
    nd = a.ndim
    return pl.BlockSpec(a.shape, lambda *_: (0,) * nd, pipeline_mode=pl.Buffered(1))


def _after(body, n_in, deps):
    def wrapped(*refs):
        return body(*refs[:n_in], *refs[n_in + len(deps):])
    return wrapped


def _lane_sel(lane, a2, a4, a8, a16):
    return jnp.where(lane < POOL_GW, a2, jnp.where(lane < 2 * POOL_GW, a4, jnp.where(lane < 3 * POOL_GW, a8, a16)))


def _window_sums_back(src, tmp_a, tmp_b, tm):
    n = HALO + tm
    rows = lambda ref, lo, back: ref[pl.ds(lo - back, n - lo), :]
    tmp_a[pl.ds(8, n - 8), :] = rows(src, 8, 0) + rows(src, 8, 1)
    tmp_b[pl.ds(16, n - 16), :] = rows(tmp_a, 16, 0) + rows(tmp_a, 16, 2)
    s2 = rows(tmp_a, HALO, 0)
    tmp_a[pl.ds(24, n - 24), :] = rows(tmp_b, 24, 0) + rows(tmp_b, 24, 4)
    s8 = rows(tmp_a, HALO, 0)
    return s2, rows(tmp_b, HALO, 0), s8, s8 + rows(tmp_a, HALO, 8)


def _window_sums_ahead(src, tmp_a, tmp_b, tm):
    rows = lambda ref, n, ahead: ref[pl.ds(ahead, n), :]
    tmp_a[pl.ds(0, tm + 24), :] = rows(src, tm + 24, 0) + rows(src, tm + 24, 1)
    tmp_b[pl.ds(0, tm + 16), :] = rows(tmp_a, tm + 16, 0) + rows(tmp_a, tm + 16, 2)
    s2 = rows(tmp_a, tm, 0)
    tmp_a[pl.ds(0, tm + 8), :] = rows(tmp_b, tm + 8, 0) + rows(tmp_b, tm + 8, 4)
    s8 = rows(tmp_a, tm, 0)
    return s2, rows(tmp_b, tm, 0), s8, s8 + rows(tmp_a, tm, 8)


def _pool_counts(tm, t0):
    lane = lax.broadcasted_iota(jnp.int32, (tm, D_POOL), 1)
    row = lax.broadcasted_iota(jnp.int32, (tm, D_POOL), 0) + t0
    cnt = jnp.minimum(row + 1, _lane_sel(lane, 2, 4, 8, 16)).astype(F32)
    return lane, cnt


def _pool_fwd(ext_q, tmp_a, tmp_b, tm, t0):
    lane, cnt = _pool_counts(tm, t0)
    q = ext_q[pl.ds(HALO, tm), :]
    pooled = _lane_sel(lane, *_window_sums_back(ext_q, tmp_a, tmp_b, tm)) / cnt - q
    return pooled, lane, cnt


def _taps(src, w_of, offs, tm, zbuf):
    acc = None
    for r in range(8):
        ks = [k for k in range(len(offs)) if offs[k] % 8 == r]
        if not ks:
            continue
        rows = tm + (8 if r else 0)
        z = w_of(ks[0]) * src[pl.ds(offs[ks[0]] - r, rows), :]
        for k in ks[1:]:
            z = z + w_of(k) * src[pl.ds(offs[k] - r, rows), :]
        if r:
            zbuf[...] = z
            z = zbuf[pl.ds(r, tm), :]
        acc = z if acc is None else acc + z
    return acc


def _tap_grads(d_pad, src, offs, tm, g_ref, zbuf):
    ch = src.shape[-1]
    for r in range(8):
        ks = [k for k in range(len(offs)) if offs[k] % 8 == r]
        if not ks:
            continue
        rows = tm + (8 if r else 0)
        if r:
            zbuf[...] = d_pad[pl.ds(8 - r, rows), :]
        for k in ks:
            d = zbuf[...] if r else d_pad[pl.ds(8, rows), :]
            prod = d * src[pl.ds(offs[k] - r, rows), :]
            g_ref[k] += jnp.sum(prod.reshape(rows // 8, 8, ch), axis=0)


_CONV_OFFS = [HALO - (CONV_K - 1) + k for k in range(CONV_K)]


def _conv_fwd(ext_u, dww_ref, dwb, tm, zbuf):
    return dwb + _taps(ext_u, lambda k: dww_ref[k:k + 1, :], _CONV_OFFS, tm, zbuf)


def _ln_silu(c, lng, lnb):
    mu = jnp.mean(c, axis=-1, keepdims=True)
    cc = c - mu
    rstd = lax.rsqrt(jnp.mean(cc * cc, axis=-1, keepdims=True) + EPS)
    z = cc * rstd
    l = z * lng + lnb
    sl = _sig(l)
    return z, rstd, l, sl, l * sl


def _rg_fwd(ext_x, cw_ref, cb, wa, ba, wx, bx, lam, tm):
    xc = cb + cw_ref[0:1, :] * ext_x[pl.ds(HALO - (RG_CONV_K - 1), tm), :]
    for k in range(1, RG_CONV_K):
        xc = xc + cw_ref[k:k + 1, :] * ext_x[pl.ds(HALO - (RG_CONV_K - 1) + k, tm), :]
    xcb = xc.astype(BF16)
    r = _sig(_dot(xcb, wa) + ba)
    ig = _sig(_dot(xcb, wx) + bx)
    sp = jnp.maximum(-lam, 0.0) + jnp.log(1.0 + jnp.exp(-jnp.abs(lam)))
    log_a = (-RG_C * r) * sp
    a = jnp.exp(log_a)
    m = jnp.sqrt(_one_minus_sq(a, log_a))
    return xc, xcb, r, ig, sp, a, m


def _scan_rows(a_ref, b_ref, out_ref, carry, tm, reverse):
    rows = lax.broadcasted_iota(jnp.int32, (8, D_RNN), 0)
    ngrp = tm // 8

    def grp(gi, hb):
        st = pl.multiple_of((ngrp - 1 - gi if reverse else gi) * 8, 8)
        a8 = a_ref[pl.ds(st, 8), :]
        b8 = b_ref[pl.ds(st, 8), :]
        out = jnp.zeros((8, D_RNN), F32)
        for j in (range(7, -1, -1) if reverse else range(8)):
            aj = jnp.broadcast_to(a8[j:j + 1, :], (8, D_RNN))
            bj = jnp.broadcast_to(b8[j:j + 1, :], (8, D_RNN))
            if reverse:
                cur = bj + hb
                hb = aj * cur
            else:
                cur = aj * hb + bj
                hb = cur
            out = jnp.where(rows == j, cur, out)
        out_ref[pl.ds(st, 8), :] = out
        return hb

    carry[...] = lax.fori_loop(0, ngrp, grp, carry[...])


_MIX_W = ("wp", "psc", "dww", "dwb", "lng", "lnb", "wpw", "cw", "cb", "wa", "ba", "wx", "bx", "lam")


def _mixer_fwd(h, g, w_in, mw):
    t = h.shape[0]
    tm = _row_tile(t, TM_MIX)

    def body(h_ref, g_ref, win_ref, wp, psc, dww, dwb, lng, lnb, wpw, cw, cb, wa, ba, wx, bx, lam,
             y_ref, p_ref, u_ref, hs_ref, conv_ref, ext_q, ext_u, ext_x, tmp_a, tmp_b, zbuf, a_s, b_s, hcar):
        i = pl.program_id(0)

        @pl.when(i == 0)
        def _():
            ext_q[0:HALO, :] = jnp.zeros((HALO, D_POOL), F32)
            ext_u[0:HALO, :] = jnp.zeros((HALO, D_CONV), F32)
            ext_x[0:HALO, :] = jnp.zeros((HALO, D_RNN), F32)
            hcar[...] = jnp.zeros((8, D_RNN), F32)

        u = (_rms(h_ref[...])[1] * g_ref[...]).astype(BF16)
        u_ref[...] = u
        p_ref[...] = _dot(u, win_ref[...])

        ext_q[pl.ds(HALO, tm), :] = p_ref[:, 0:256]
        pooled, _, _ = _pool_fwd(ext_q, tmp_a, tmp_b, tm, i * tm)
        y_ref[:, 0:256] = (_dot(pooled.astype(BF16), wp[...]) * psc[...]).astype(BF16)

        ext_u[pl.ds(HALO, tm), :] = p_ref[:, 256:512] * _sig(p_ref[:, 512:768])
        conv = _conv_fwd(ext_u, dww, dwb[...], tm, zbuf)
        conv_ref[...] = conv
        act = _ln_silu(conv, lng[...], lnb[...])[4]
        y_ref[:, 256:512] = _dot(act.astype(BF16), wpw[...]).astype(BF16)

        ext_x[pl.ds(HALO, tm), :] = p_ref[:, 1280:1792]
        xc, _, _, ig, _, a, m = _rg_fwd(ext_x, cw, cb[...], wa[...], ba[...], wx[...], bx[...], lam[...], tm)
        a_s[...] = a
        b_s[...] = m * (ig * xc)
        _scan_rows(a_s, b_s, hs_ref, hcar, tm, reverse=False)
        y_ref[:, 512:1024] = (_gelu_and_grad(p_ref[:, 768:1280])[0] * hs_ref[...]).astype(BF16)

        ext_q[0:HALO, :] = ext_q[pl.ds(tm, HALO), :]
        ext_u[0:HALO, :] = ext_u[pl.ds(tm, HALO), :]
        ext_x[0:HALO, :] = ext_x[pl.ds(tm, HALO), :]

    ws = [mw[k] for k in _MIX_W]
    row = lambda w: pl.BlockSpec((tm, w), lambda i: (i, 0))
    return pl.pallas_call(
        body, name="mixer_fwd", grid=(t // tm,),
        in_specs=[row(D_MODEL), _full(g), _resident(w_in)] + [_full(w) for w in ws],
        out_specs=[row(D_MODEL), row(D_IN), row(D_MODEL), row(D_RNN), row(D_CONV)],
        out_shape=[jax.ShapeDtypeStruct((t, D_MODEL), BF16), jax.ShapeDtypeStruct((t, D_IN), F32),
                   jax.ShapeDtypeStruct((t, D_MODEL), BF16), jax.ShapeDtypeStruct((t, D_RNN), F32),
                   jax.ShapeDtypeStruct((t, D_CONV), F32)],
        scratch_shapes=[pltpu.VMEM((HALO + tm, D_POOL), F32), pltpu.VMEM((HALO + tm, D_CONV), F32),
                        pltpu.VMEM((HALO + tm, D_RNN), F32), pltpu.VMEM((HALO + tm, D_POOL), F32),
                        pltpu.VMEM((HALO + tm, D_POOL), F32), pltpu.VMEM((tm + 8, D_CONV), F32),
                        pltpu.VMEM((tm, D_RNN), F32), pltpu.VMEM((tm, D_RNN), F32), pltpu.VMEM((8, D_RNN), F32)],
        compiler_params=_params("arbitrary"),
    )(h, g, w_in, *ws)


_MIX_G = (("wp", (D_POOL, D_POOL)), ("psc", (1, D_POOL)), ("dww", (32, 8, D_CONV)), ("dwb", (1, D_CONV)),
          ("lng", (1, D_CONV)), ("lnb", (1, D_CONV)), ("wpw", (D_CONV, D_CONV)), ("cw", (8, D_RNN)),
          ("cb", (1, D_RNN)), ("wa", (D_RNN, D_RNN)), ("ba", (1, D_RNN)), ("wx", (D_RNN, D_RNN)),
          ("bx", (1, D_RNN)), ("lam", (1, D_RNN)), ("g1", (1, D_MODEL)))


def _mixer_bwd(p, dh1, hs, conv, h0, g1, w_out, w_in, mw, deps=()):
    t = p.shape[0]
    tm = _row_tile(t, TM_MIX)
    nt = t // tm
    hb = tm // HALO

    def body(p_ref, ph_ref, dh1_ref, hs_ref, hsh_ref, conv_ref, h0_ref, g1_ref, wout_ref, win_ref,
             wp, psc, dww, dwb, lng, lnb, wpw, cw, cb, wa, ba, wx, bx, lam,
             dp_ref, dh0_ref, g_wp, g_psc, g_dww, g_dwb, g_lng, g_lnb, g_wpw, g_cw, g_cb, g_wa, g_ba, g_wx, g_bx, g_lam, g_g1,
             ext_q, ext_u, ext_x, ext_h, ee, dc_s, dx_s, tmp_a, tmp_b, zbuf, d_pad, a_s, b_s, g_s, gcar, dy_ref, dp_s):
        step = pl.program_id(0)
        i = nt - 1 - step
        grads = (g_wp, g_psc, g_dww, g_dwb, g_lng, g_lnb, g_wpw, g_cw, g_cb, g_wa, g_ba, g_wx, g_bx, g_lam, g_g1)
        dy_ref[...] = _dot_nt(dh1_ref[...].astype(BF16), wout_ref[...])

        @pl.when(step == 0)
        def _():
            for gr in grads:
                gr[...] = jnp.zeros(gr.shape, F32)
            ee[pl.ds(tm, HALO), :] = jnp.zeros((HALO, D_POOL), F32)
            dc_s[pl.ds(tm, HALO), :] = jnp.zeros((HALO, D_CONV), F32)
            dx_s[pl.ds(tm, HALO), :] = jnp.zeros((HALO, D_RNN), F32)
            d_pad[0:8, :] = jnp.zeros((8, D_CONV), F32)
            d_pad[pl.ds(tm + 8, 8), :] = jnp.zeros((8, D_CONV), F32)
            gcar[...] = jnp.zeros((8, D_RNN), F32)

        hm = jnp.where(i == 0, 0.0, 1.0)

        ext_q[0:HALO, :] = ph_ref[:, 0:256] * hm
        ext_q[pl.ds(HALO, tm), :] = p_ref[:, 0:256]
        pooled, lane, cnt = _pool_fwd(ext_q, tmp_a, tmp_b, tm, i * tm)
        pooled_b = pooled.astype(BF16)
        dya = dy_ref[:, 0:256]
        g_psc[...] += _colsum(dya * _dot(pooled_b, wp[...]))
        dmixed_b = (dya * psc[...]).astype(BF16)
        dpooled = _dot_nt(dmixed_b, wp[...])
        g_wp[...] += _dot_tn(pooled_b, dmixed_b)
        ee[0:tm, :] = dpooled / cnt
        dp_s[:, 0:256] = _lane_sel(lane, *_window_sums_ahead(ee, tmp_a, tmp_b, tm)) - dpooled
        ee[pl.ds(tm, HALO), :] = ee[0:HALO, :]

        v = p_ref[:, 256:512]
        s = _sig(p_ref[:, 512:768])
        ext_u[0:HALO, :] = ph_ref[:, 256:512] * _sig(ph_ref[:, 512:768]) * hm
        ext_u[pl.ds(HALO, tm), :] = v * s
        z, rstd, l, sl, act = _ln_silu(conv_ref[...], lng[...], lnb[...])
        dyb_b = dy_ref[:, 256:512].astype(BF16)
        dact = _dot_nt(dyb_b, wpw[...])
        g_wpw[...] += _dot_tn(act.astype(BF16), dyb_b)
        dl = dact * (sl * (1.0 + l * (1.0 - sl)))
        g_lng[...] += _colsum(dl * z)
        g_lnb[...] += _colsum(dl)
        dz = dl * lng[...]
        dc = rstd * (dz - jnp.mean(dz, axis=-1, keepdims=True) - z * jnp.mean(dz * z, axis=-1, keepdims=True))
        g_dwb[...] += _colsum(dc)
        dc_s[0:tm, :] = dc
        d_pad[pl.ds(8, tm), :] = dc
        _tap_grads(d_pad, ext_u, _CONV_OFFS, tm, g_dww, zbuf)
        du0 = _taps(dc_s, lambda j: dww[CONV_K - 1 - j:CONV_K - j, :], list(range(CONV_K)), tm, zbuf)
        dp_s[:, 256:512] = du0 * s
        dp_s[:, 512:768] = du0 * v * (s * (1.0 - s))
        dc_s[pl.ds(tm, HALO), :] = dc_s[0:HALO, :]

        ext_x[0:HALO, :] = ph_ref[:, 1280:1792] * hm
        ext_x[pl.ds(HALO, tm), :] = p_ref[:, 1280:1792]
        xc, xcb, r, ig, sp, a, m = _rg_fwd(ext_x, cw, cb[...], wa[...], ba[...], wx[...], bx[...], lam[...], tm)
        ext_h[0:HALO, :] = hsh_ref[...] * hm
        ext_h[pl.ds(HALO, tm), :] = hs_ref[...]
        dyc = dy_ref[:, 512:1024]
        gl, dgl = _gelu_and_grad(p_ref[:, 768:1280])
        dp_s[:, 768:1280] = dyc * hs_ref[...] * dgl
        a_s[...] = a
        b_s[...] = dyc * gl
        _scan_rows(a_s, b_s, g_s, gcar, tm, reverse=True)
        g = g_s[...]
        da = g * ext_h[pl.ds(HALO - 1, tm), :]
        dm = g * (ig * xc)
        dig = g * (m * xc)
        dlog_a = da * a - dm * (a * a) / m
        g_lam[...] += _colsum(dlog_a * (-RG_C * r)) * (-_sig(-lam[...]))
        dra = (dlog_a * (-RG_C * sp)) * (r * (1.0 - r))
        dia = dig * (ig * (1.0 - ig))
        g_ba[...] += _colsum(dra)
        g_bx[...] += _colsum(dia)
        dra_b = dra.astype(BF16)
        dia_b = dia.astype(BF16)
        dxc = g * (m * ig) + _dot_nt(dra_b, wa[...]) + _dot_nt(dia_b, wx[...])
        g_wa[...] += _dot_tn(xcb, dra_b)
        g_wx[...] += _dot_tn(xcb, dia_b)
        g_cb[...] += _colsum(dxc)
        dx_s[0:tm, :] = dxc
        for k in range(RG_CONV_K):
            g_cw[k:k + 1, :] += _colsum(dxc * ext_x[pl.ds(HALO - (RG_CONV_K - 1) + k, tm), :])
        dxin = cw[RG_CONV_K - 1:RG_CONV_K, :] * dxc
        for j in range(1, RG_CONV_K):
            dxin = dxin + cw[RG_CONV_K - 1 - j:RG_CONV_K - j, :] * dx_s[pl.ds(j, tm), :]
        dp_s[:, 1280:1792] = dxin
        dx_s[pl.ds(tm, HALO), :] = dx_s[0:HALO, :]

        dpb = dp_s[...].astype(BF16)
        dp_ref[...] = dpb
        du = _dot_nt(dpb, win_ref[...])
        r, n = _rms(h0_ref[...])
        g_g1[...] += _colsum(du * n)
        dh0_ref[...] = dh1_ref[...] + _rms_bwd(du, n, r, g1_ref[...])

    ws = [mw[k] for k in _MIX_W]
    tile = lambda w: pl.BlockSpec((tm, w), lambda s: (nt - 1 - s, 0))
    halo = lambda w: pl.BlockSpec((HALO, w), lambda s: (jnp.maximum((nt - 1 - s) * hb - 1, 0), 0))
    outs = pl.pallas_call(
        _after(body, 10 + len(ws), deps), name="mixer_bwd", grid=(nt,),
        in_specs=[tile(D_IN), halo(D_IN), tile(D_MODEL), tile(D_RNN), halo(D_RNN), tile(D_CONV), tile(D_MODEL), _full(g1),
                  _resident(w_out), _resident(w_in)] + [_full(w) for w in ws] + [ANY] * len(deps),
        out_specs=[tile(D_IN), tile(D_MODEL)] + [pl.BlockSpec(shp, lambda s, nd=len(shp): (0,) * nd) for _, shp in _MIX_G],
        out_shape=[jax.ShapeDtypeStruct((t, D_IN), BF16), jax.ShapeDtypeStruct((t, D_MODEL), F32)]
        + [jax.ShapeDtypeStruct(shp, F32) for _, shp in _MIX_G],
        scratch_shapes=[pltpu.VMEM((HALO + tm, D_POOL), F32), pltpu.VMEM((HALO + tm, D_CONV), F32),
                        pltpu.VMEM((HALO + tm, D_RNN), F32), pltpu.VMEM((HALO + tm, D_RNN), F32),
                        pltpu.VMEM((tm + HALO, D_POOL), F32), pltpu.VMEM((tm + HALO, D_CONV), F32),
                        pltpu.VMEM((tm + HALO, D_RNN), F32), pltpu.VMEM((HALO + tm, D_POOL), F32),
                        pltpu.VMEM((HALO + tm, D_POOL), F32), pltpu.VMEM((tm + 8, D_CONV), F32),
                        pltpu.VMEM((tm + 16, D_CONV), F32), pltpu.VMEM((tm, D_RNN), F32),
                        pltpu.VMEM((tm, D_RNN), F32), pltpu.VMEM((tm, D_RNN), F32), pltpu.VMEM((8, D_RNN), F32),
                        pltpu.VMEM((tm, D_MODEL), F32), pltpu.VMEM((tm, D_IN), F32)],
        compiler_params=_params("arbitrary"),
    )(p, p, dh1, hs, hs, conv, h0, g1, w_out, w_in, *ws, *deps)
    return outs[0], outs[1], {k: o for (k, _), o in zip(_MIX_G, outs[2:])}


def _mid_fwd(y, h0, w_out, g, w_up):
    t = h0.shape[0]
    tm = _row_tile(t, TM_MAT)

    def body(y_ref, h0_ref, wo_ref, g_ref, wu_ref, h1_ref, u2_ref, f_ref):
        h1 = h0_ref[...] + _dot(y_ref[...], wo_ref[...])
        h1_ref[...] = h1
        u2 = (_rms(h1)[1] * g_ref[...]).astype(BF16)
        u2_ref[...] = u2
        for c in range(D_FF // FF_CHUNK):
            f_ref[:, c * FF_CHUNK:(c + 1) * FF_CHUNK] = _dot(u2, wu_ref[c]).astype(BF16)

    row = lambda w: pl.BlockSpec((tm, w), lambda i: (i, 0))
    return pl.pallas_call(
        body, name="mid_fwd", grid=(t // tm,),
        in_specs=[row(D_MODEL), row(D_MODEL), _resident(w_out), _full(g), _resident(w_up)],
        out_specs=[row(D_MODEL), row(D_MODEL), row(D_FF)],
        out_shape=[jax.ShapeDtypeStruct((t, D_MODEL), F32), jax.ShapeDtypeStruct((t, D_MODEL), BF16),
                   jax.ShapeDtypeStruct((t, D_FF), BF16)],
        compiler_params=_params("parallel"),
    )(y, h0, w_out, g, w_up)


def _down_fwd(f, h1, w_down):
    t = h1.shape[0]
    tm = _row_tile(t, TM_MAT)

    def body(f_ref, h1_ref, wd_ref, h2_ref):
        acc = h1_ref[...]
        for c in range(D_FF // FF_CHUNK):
            cols = slice(c * FF_CHUNK, (c + 1) * FF_CHUNK)
            a = jnp.square(jnp.maximum(f_ref[:, cols].astype(F32), 0.0)).astype(BF16)
            acc = acc + _dot(a, wd_ref[cols, :])
        h2_ref[...] = acc

    return pl.pallas_call(
        body, name="down_fwd", grid=(t // tm,),
        in_specs=[pl.BlockSpec((tm, D_FF), lambda i: (i, 0)), pl.BlockSpec((tm, D_MODEL), lambda i: (i, 0)), _resident(w_down)],
        out_specs=pl.BlockSpec((tm, D_MODEL), lambda i: (i, 0)),
        out_shape=jax.ShapeDtypeStruct((t, D_MODEL), F32),
        compiler_params=_params("parallel"),
    )(f, h1, w_down)


def _loss_head(h, g, tgt, t_real):
    t = h.shape[0]
    tm = _row_tile(t, TM_MAT)

    def body(h_ref, g_ref, tgt_ref, loss_ref, dh_ref, dg_ref):
        i = pl.program_id(0)

        @pl.when(i == 0)
        def _():
            loss_ref[...] = jnp.zeros(loss_ref.shape, F32)
            dg_ref[...] = jnp.zeros(dg_ref.shape, F32)

        r, n = _rms(h_ref[...])
        row = lax.broadcasted_iota(jnp.int32, (tm, 1), 0) + i * tm
        valid = jnp.logical_and(row >= N_META, row < t_real)
        diff = jnp.where(valid, n * g_ref[...] - tgt_ref[...], 0.0)
        loss_ref[...] += 0.5 * jnp.sum(jnp.mean(diff * diff, axis=-1, keepdims=True))
        dy = diff * (1.0 / D_MODEL)
        dg_ref[...] += _colsum(dy * n)
        dh_ref[...] = _rms_bwd(dy, n, r, g_ref[...])

    return pl.pallas_call(
        body, name="loss_head", grid=(t // tm,),
        in_specs=[pl.BlockSpec((tm, D_MODEL), lambda i: (i, 0)), _full(g), pl.BlockSpec((tm, D_MODEL), lambda i: (i, 0))],
        out_specs=[pl.BlockSpec((8, 128), lambda i: (0, 0)), pl.BlockSpec((tm, D_MODEL), lambda i: (i, 0)),
                   pl.BlockSpec((1, D_MODEL), lambda i: (0, 0))],
        out_shape=[jax.ShapeDtypeStruct((8, 128), F32), jax.ShapeDtypeStruct((t, D_MODEL), F32),
                   jax.ShapeDtypeStruct((1, D_MODEL), F32)],
        compiler_params=_params("arbitrary"),
    )(h, g, tgt)


def _mlp_bwd(dh2, f, h1, g, w_up, w_down, deps=()):
    t = dh2.shape[0]
    tm = _row_tile(t, TM_MLP_BWD)

    def body(dh2_ref, f_ref, wd_ref, wu_ref, h1_ref, g_ref, df_ref, dh1_ref, dg_ref):
        @pl.when(pl.program_id(0) == 0)
        def _():
            dg_ref[...] = jnp.zeros(dg_ref.shape, F32)

        dh2 = dh2_ref[...]
        dhb = dh2.astype(BF16)
        du2 = None
        for c in range(D_FF // FF_CHUNK):
            cols = slice(c * FF_CHUNK, (c + 1) * FF_CHUNK)
            dact = _dot_nt(dhb, wd_ref[c])
            df = (dact * (2.0 * jnp.maximum(f_ref[:, cols].astype(F32), 0.0))).astype(BF16)
            df_ref[:, cols] = df
            part = _dot_nt(df, wu_ref[c])
            du2 = part if du2 is None else du2 + part
        r, n = _rms(h1_ref[...])
        dg_ref[...] += _colsum(du2 * n)
        dh1_ref[...] = dh2 + _rms_bwd(du2, n, r, g_ref[...])

    row = lambda w: pl.BlockSpec((tm, w), lambda i: (i, 0))
    return pl.pallas_call(
        _after(body, 6, deps), name="mlp_bwd", grid=(t // tm,),
        in_specs=[row(D_MODEL), row(D_FF), _resident(w_down), _resident(w_up), row(D_MODEL), _full(g)] + [ANY] * len(deps),
        out_specs=[row(D_FF), row(D_MODEL), pl.BlockSpec((1, D_MODEL), lambda i: (0, 0))],
        out_shape=[jax.ShapeDtypeStruct((t, D_FF), BF16), jax.ShapeDtypeStruct((t, D_MODEL), F32),
                   jax.ShapeDtypeStruct((1, D_MODEL), F32)],
        compiler_params=_params("arbitrary"),
    )(dh2, f, w_down, w_up, h1, g, *deps)


def _tn_matmul(a, b, kc, nc, relu2, name, deps=()):
    t, k = a.shape
    n = b.shape[1]
    tt = _row_tile(t, TM_MAT)
    gk, gn = k // kc, n // nc

    def body(a_ref, b_ref, o_ref):
        @pl.when(pl.program_id(2) == 0)
        def _():
            o_ref[...] = jnp.zeros(o_ref.shape, F32)

        av = a_ref[...]
        if relu2:
            av = jnp.square(jnp.maximum(av.astype(F32), 0.0))
        o_ref[...] += _dot_tn(av.astype(BF16), b_ref[...].astype(BF16))

    return pl.pallas_call(
        _after(body, 2, deps), name=name, grid=(gk, gn, t // tt),
        in_specs=[pl.BlockSpec((tt, kc), lambda ik, jn, it: (it, ik)), pl.BlockSpec((tt, nc), lambda ik, jn, it: (it, jn))]
        + [ANY] * len(deps),
        out_specs=pl.BlockSpec((None, kc, nc), lambda ik, jn, it: (ik * gn + jn, 0, 0)),
        out_shape=jax.ShapeDtypeStruct((gk * gn, kc, nc), F32),
        compiler_params=_params("parallel", "parallel", "arbitrary"),
    )(a, b, *deps)


def _block_diag(blocks):
    nb, hd, _ = blocks.shape
    eye = jnp.eye(nb, dtype=blocks.dtype)
    return (blocks[:, :, None, :] * eye[:, None, :, None]).reshape(nb * hd, nb * hd)


def _diag_blocks(m, nb):
    hd = m.shape[0] // nb
    eye = jnp.eye(nb, dtype=m.dtype)
    return jnp.sum(m.reshape(nb, hd, nb, hd) * eye[:, None, :, None], axis=2)


def _mixer_weights(w, l):
    row = lambda a: a.reshape(1, -1)
    return dict(
        wp=_block_diag(w["pool_w"][l]).astype(BF16), psc=row(w["pool_scale"][l]),
        dww=jnp.pad(w["convb_dw_w"][l], ((0, 32 - CONV_K), (0, 0))), dwb=row(w["convb_dw_b"][l]),
        lng=row(w["convb_ln_g"][l]), lnb=row(w["convb_ln_b"][l]), wpw=w["convb_pw_w"][l].astype(BF16),
        cw=jnp.pad(w["rg_conv_w"][l], ((0, 8 - RG_CONV_K), (0, 0))), cb=row(w["rg_conv_b"][l]),
        wa=_block_diag(w["rg_w_a"][l]).astype(BF16), ba=row(w["rg_b_a"][l]),
        wx=_block_diag(w["rg_w_x"][l]).astype(BF16), bx=row(w["rg_b_x"][l]), lam=row(w["rg_lambda"][l]))


def _local_step(h, tgt, t_real, w, fetch, hooks):
    depth = 2
    saved = []
    big = []
    for l in range(depth):
        mw = _mixer_weights(w, l)
        g1 = w["mix_norm_g"][l].reshape(1, -1)
        g2 = w["mlp_norm_g"][l].reshape(1, -1)
        wl = dict(w_in=fetch(l, "w_in", h))
        y, p, u, hs, conv = _mixer_fwd(h, g1, wl["w_in"], mw)
        wl["w_out"], wl["w_up"] = fetch(l, "w_out", y), fetch(l, "w_up", y)
        h1, u2, f = _mid_fwd(y, h, wl["w_out"], g2, wl["w_up"])
        wl["w_down"] = fetch(l, "w_down", f)
        h2 = _down_fwd(f, h1, wl["w_down"].reshape(D_FF, D_MODEL))
        saved.append(dict(mw=mw, g1=g1, g2=g2, h0=h, p=p, u=u, y=y, hs=hs, conv=conv, h1=h1, u2=u2, f=f))
        big.append(wl)
        h = h2
    gf = w["final_norm_g"].reshape(1, -1)
    loss, dh, dgf = _loss_head(h, gf, tgt, t_real)

    gs = {k: [None] * depth for k in ("mix_norm_g", "mlp_norm_g", "pool_w", "pool_scale", "convb_dw_w", "convb_dw_b",
                                      "convb_ln_g", "convb_ln_b", "convb_pw_w", "rg_conv_w", "rg_conv_b", "rg_w_a",
                                      "rg_b_a", "rg_w_x", "rg_b_x", "rg_lambda")}
    deps = ()
    for l in reversed(range(depth)):
        s, wl = saved[l], big[l]
        df, dh1, dg2 = _mlp_bwd(dh, s["f"], s["h1"], s["g2"], wl["w_up"], wl["w_down"], deps)
        deps = hooks.point(l, "mlp_bwd", dh1)
        g_down = _tn_matmul(s["f"], dh, FF_CHUNK, D_MODEL, True, "dw_down", deps)
        hooks.grad(l, "w_down", g_down)
        deps = hooks.point(l, "dw_down", g_down)
        g_up = _tn_matmul(s["u2"], df, D_MODEL, FF_CHUNK, False, "dw_up", deps)
        hooks.grad(l, "w_up", g_up)
        deps = hooks.point(l, "dw_up", g_up)
        g_out = _tn_matmul(s["y"], dh1, D_MODEL, D_MODEL, False, "dw_out", deps)
        hooks.grad(l, "w_out", g_out.reshape(N_CHIPS, D_MODEL // N_CHIPS, D_MODEL))
        deps = hooks.point(l, "dw_out", g_out)
        dp, dh, mg = _mixer_bwd(s["p"], dh1, s["hs"], s["conv"], s["h0"], s["g1"], wl["w_out"], wl["w_in"], s["mw"], deps)
        deps = hooks.point(l, "mixer_bwd", dh)
        g_in = _tn_matmul(s["u"], dp, D_MODEL, D_IN, False, "dw_in", deps)
        g_in = g_in[0].reshape(D_MODEL, N_CHIPS, D_IN // N_CHIPS).transpose(1, 0, 2)
        hooks.grad(l, "w_in", g_in)
        deps = hooks.point(l, "dw_in", g_in)
        gs["mix_norm_g"][l] = mg["g1"][0]
        gs["mlp_norm_g"][l] = dg2[0]
        gs["pool_w"][l] = _diag_blocks(mg["wp"], D_POOL // POOL_GW)
        gs["pool_scale"][l] = mg["psc"][0]
        gs["convb_dw_w"][l] = jnp.sum(mg["dww"][:CONV_K], axis=1)
        gs["convb_dw_b"][l] = mg["dwb"][0]
        gs["convb_ln_g"][l] = mg["lng"][0]
        gs["convb_ln_b"][l] = mg["lnb"][0]
        gs["convb_pw_w"][l] = mg["wpw"]
        gs["rg_conv_w"][l] = mg["cw"][:RG_CONV_K]
        gs["rg_conv_b"][l] = mg["cb"][0]
        gs["rg_w_a"][l] = _diag_blocks(mg["wa"], D_RNN // RG_HD)
        gs["rg_b_a"][l] = mg["ba"][0]
        gs["rg_w_x"][l] = _diag_blocks(mg["wx"], D_RNN // RG_HD)
        gs["rg_b_x"][l] = mg["bx"][0]
        gs["rg_lambda"][l] = mg["lam"][0]
    gsmall = {k: jnp.stack(v) for k, v in gs.items()}
    gsmall["final_norm_g"] = dgf[0]
    return loss[0, 0], dh, gsmall


def _place():
    return lax.axis_index("x"), lax.axis_index("y"), lax.axis_index("c")


def _other_chips(x, y):
    return [(1 - x, y), (x, 1 - y), (1 - x, 1 - y)]


def _gather_now(srcs):
    ns = len(srcs)
    out_shape = [jax.ShapeDtypeStruct((N_CHIPS,) + s.shape, s.dtype) for s in srcs]

    def body(*refs):
        src_refs, dst_refs = refs[:ns], refs[ns:2 * ns]
        send_sems, recv_sems, loc_sems = refs[2 * ns:]
        x, y, c = _place()
        me = 2 * x + y
        local, remote = [], []
        for n in range(ns):
            cp = pltpu.make_async_copy(src_refs[n], dst_refs[n].at[me], loc_sems.at[n])
            cp.start()
            local.append(cp)
            for j, (px, py) in enumerate(_other_chips(x, y)):
                out = pltpu.make_async_remote_copy(src_refs[n], dst_refs[n].at[me], send_sems.at[3 * n + j],
                                                   recv_sems.at[3 * n + j], device_id=(px, py, c), device_id_type=MESH)
                out.start()
                remote.append(pltpu.make_async_remote_copy(src_refs[n], dst_refs[n].at[2 * px + py], send_sems.at[3 * n + j],
                                                           recv_sems.at[3 * n + j], device_id=(px, py, c), device_id_type=MESH))
        for cp in remote:
            cp.wait()
        for cp in local:
            cp.wait()

    return pl.pallas_call(
        body, name="gather_now", in_specs=[ANY] * ns, out_specs=[ANY] * ns, out_shape=out_shape,
        scratch_shapes=[pltpu.SemaphoreType.DMA((3 * ns,)), pltpu.SemaphoreType.DMA((3 * ns,)), pltpu.SemaphoreType.DMA((ns,))],
    )(*srcs)


HBM_SPEC = pl.BlockSpec(memory_space=pltpu.HBM)
SEM_SPEC = pl.BlockSpec(memory_space=pltpu.SEMAPHORE)
DATAFLOW = pltpu.SideEffectType.DATAFLOW_SIDE_EFFECTING


def _gather_copies(src_refs, land_refs, send_sem, recv_sem, first):
    x, y, c = _place()
    me = 2 * x + y
    out = []
    for n in range(len(src_refs)):
        for j, (px, py) in enumerate(_other_chips(x, y)):
            out.append(pltpu.make_async_remote_copy(src_refs[n], land_refs[n].at[me], send_sem.at[first + 3 * n + j],
                                                    recv_sem.at[first + 3 * n + j], device_id=(px, py, c), device_id_type=MESH))
    return out


def _gather_start(groups, me):
    srcs = [pltpu.with_memory_space_constraint(s, pltpu.HBM) for g in groups for s in g]
    lands = [pltpu.with_memory_space_constraint(
        lax.dynamic_update_slice(jnp.zeros((N_CHIPS,) + s.shape, s.dtype), s[None], (me,) + (0,) * s.ndim), pltpu.HBM)
        for g in groups for s in g]
    n, ng = len(srcs), len(groups)
    first = [sum(len(g) for g in groups[:i]) for i in range(ng)]

    def body(*refs):
        src_refs, land_refs = refs[:n], refs[n:2 * n]
        sems = refs[2 * n:2 * n + 2 * ng]
        token = refs[-1]
        for gi, g in enumerate(groups):
            lo, hi = first[gi], first[gi] + len(g)
            for cp in _gather_copies(src_refs[lo:hi], land_refs[lo:hi], sems[2 * gi], sems[2 * gi + 1], 0):
                cp.start()
        token[...] = jnp.zeros(token.shape, token.dtype)

    sem_shapes = [pltpu.SemaphoreType.DMA((3 * len(g),)) for g in groups for _ in range(2)]
    outs = pl.pallas_call(
        body, name="gather_start",
        out_shape=sem_shapes + [pltpu.HBM(a.shape, a.dtype) for a in srcs + lands] + [jax.ShapeDtypeStruct((8, 128), F32)],
        in_specs=[HBM_SPEC] * (2 * n),
        out_specs=[SEM_SPEC] * (2 * ng) + [HBM_SPEC] * (2 * n) + [pl.BlockSpec(memory_space=pltpu.VMEM)],
        input_output_aliases={i: 2 * ng + i for i in range(2 * n)},
        compiler_params=pltpu.CompilerParams(has_side_effects=DATAFLOW),
    )(*srcs, *lands)
    sems, thru, token = outs[:2 * ng], outs[2 * ng:2 * ng + 2 * n], outs[-1]
    state = []
    for gi, g in enumerate(groups):
        lo, hi = first[gi], first[gi] + len(g)
        state.append((sems[2 * gi], sems[2 * gi + 1], thru[lo:hi], thru[n + lo:n + hi]))
    return state, token


def _gather_wait(state, after, name):
    send_sem, recv_sem, srcs, lands = state
    n = len(srcs)

    def body(*refs):
        src_refs, land_refs = refs[:n], refs[n:2 * n]
        send, recv = refs[2 * n], refs[2 * n + 1]
        for cp in _gather_copies(src_refs, land_refs, send, recv, 0):
            cp.wait_send()
            cp.wait_recv()

    outs = pl.pallas_call(
        body, name=name,
        out_shape=[pltpu.HBM(a.shape, a.dtype) for a in list(srcs) + list(lands)],
        in_specs=[HBM_SPEC] * (2 * n) + [SEM_SPEC, SEM_SPEC, ANY],
        out_specs=[HBM_SPEC] * (2 * n),
        input_output_aliases={i: i for i in range(2 * n)},
        compiler_params=pltpu.CompilerParams(has_side_effects=DATAFLOW),
    )(*srcs, *lands, send_sem, recv_sem, after)
    return outs[n:]


def _add_halves(g, recv, c1):
    nk, r, cd = g.shape
    r2 = r // 2

    def body(c_ref, g_ref, r_ref, pa_ref, pab_ref):
        s = g_ref[...] + r_ref[...]
        pa_ref[...] = s
        pab_ref[...] = s.astype(BF16)

    blk = pl.BlockSpec((None, r2, cd), lambda k, c_ref: (k, 0, 0))
    return pl.pallas_call(
        body, name="rs_add_halves",
        grid_spec=pltpu.PrefetchScalarGridSpec(
            num_scalar_prefetch=1, grid=(nk,),
            in_specs=[pl.BlockSpec((None, r2, cd), lambda k, c_ref: (k, c_ref[0], 0)), blk], out_specs=[blk, blk]),
        out_shape=[jax.ShapeDtypeStruct((nk, r2, cd), F32), jax.ShapeDtypeStruct((nk, r2, cd), BF16)],
        compiler_params=_params("parallel"),
    )(c1, g, recv)


def _sum_partials(pa, recv, me1):
    nk, r2, cd = pa.shape

    def body(me_ref, pa_ref, r_ref, s_ref):
        s_ref[...] = ((pa_ref[...] + r_ref[0].astype(F32)) + r_ref[1].astype(F32)) + r_ref[2].astype(F32)

    return pl.pallas_call(
        body, name="rs_sum_partials",
        grid_spec=pltpu.PrefetchScalarGridSpec(
            num_scalar_prefetch=1, grid=(1,),
            in_specs=[pl.BlockSpec((None, r2, cd), lambda i, me_ref: (me_ref[0], 0, 0)),
                      pl.BlockSpec((3, r2, cd), lambda i, me_ref: (0, 0, 0))],
            out_specs=pl.BlockSpec((r2, cd), lambda i, me_ref: (0, 0))),
        out_shape=jax.ShapeDtypeStruct((r2, cd), F32),
        compiler_params=_params("arbitrary"),
    )(me1, pa, recv)


def _split_start(name, srcs, lands, ncopies, make_copies):
    srcs = [pltpu.with_memory_space_constraint(s, pltpu.HBM) for s in srcs]
    lands = [pltpu.with_memory_space_constraint(a, pltpu.HBM) for a in lands]
    n, m = len(srcs), len(lands)

    def body(*refs):
        src_refs, land_refs = refs[:n], refs[n:n + m]
        send, recv, token = refs[n + m], refs[n + m + 1], refs[-1]
        for cp in make_copies(src_refs, land_refs, send, recv):
            cp.start()
        token[...] = jnp.zeros(token.shape, token.dtype)

    outs = pl.pallas_call(
        body, name=name,
        out_shape=[pltpu.SemaphoreType.DMA((ncopies,)), pltpu.SemaphoreType.DMA((ncopies,))]
        + [pltpu.HBM(a.shape, a.dtype) for a in srcs + lands] + [jax.ShapeDtypeStruct((8, 128), F32)],
        in_specs=[HBM_SPEC] * (n + m),
        out_specs=[SEM_SPEC, SEM_SPEC] + [HBM_SPEC] * (n + m) + [pl.BlockSpec(memory_space=pltpu.VMEM)],
        input_output_aliases={i: 2 + i for i in range(n + m)},
        compiler_params=pltpu.CompilerParams(has_side_effects=DATAFLOW),
    )(*srcs, *lands)
    return (outs[0], outs[1], outs[2:2 + n], outs[2 + n:2 + n + m], make_copies), outs[-1]


def _split_wait(name, state, after):
    send_sem, recv_sem, srcs, lands, make_copies = state
    n, m = len(srcs), len(lands)

    def body(*refs):
        src_refs, land_refs = refs[:n], refs[n:n + m]
        for cp in make_copies(src_refs, land_refs, refs[n + m], refs[n + m + 1]):
            cp.wait_send()
            cp.wait_recv()

    outs = pl.pallas_call(
        body, name=name,
        out_shape=[pltpu.HBM(a.shape, a.dtype) for a in list(srcs) + list(lands)],
        in_specs=[HBM_SPEC] * (n + m) + [SEM_SPEC, SEM_SPEC, ANY],
        out_specs=[HBM_SPEC] * (n + m),
        input_output_aliases={i: i for i in range(n + m)},
        compiler_params=pltpu.CompilerParams(has_side_effects=DATAFLOW),
    )(*srcs, *lands, send_sem, recv_sem, after)
    return outs[:n], outs[n:]


def _copies_to_sibling(src_of):
    def make(src_refs, land_refs, send, recv):
        x, y, c = _place()
        return [pltpu.make_async_remote_copy(src_of(src_refs[i], c), land_refs[i], send.at[i], recv.at[i],
                                             device_id=(x, y, 1 - c), device_id_type=MESH) for i in range(len(src_refs))]
    return make


def _copies_to_chips(src_refs, land_refs, send, recv):
    x, y, c = _place()
    return [pltpu.make_async_remote_copy(src_refs[i].at[2 * px + py], land_refs[i].at[j], send.at[3 * i + j], recv.at[3 * i + j],
                                         device_id=(px, py, c), device_id_type=MESH)
            for i in range(len(src_refs)) for j, (px, py) in enumerate(_other_chips(x, y))]


def _other_half_rows(ref, c):
    r2 = ref.shape[1] // 2
    return ref.at[:, pl.ds(pl.multiple_of((1 - c) * r2, 8), r2)]


class _ReduceScatter:
    def __init__(self, tag, grads, c1, me1):
        self.tag, self.grads, self.c1, self.me1 = tag, grads, c1, me1

    def start(self):
        lands = [lax.empty((g.shape[0], g.shape[1] // 2, g.shape[2]), F32) for g in self.grads]
        self.state, token = _split_start("rs_%s_a_start" % self.tag, self.grads, lands, len(self.grads),
                                         _copies_to_sibling(_other_half_rows))
        return token

    def to_chips(self, after):
        grads, recv = _split_wait("rs_%s_a_wait" % self.tag, self.state, after)
        added = [_add_halves(g, r, self.c1) for g, r in zip(grads, recv)]
        self.own = [pa for pa, _ in added]
        pabs = [pab for _, pab in added]
        lands = [lax.empty((3,) + p.shape[1:], BF16) for p in pabs]
        self.state, token = _split_start("rs_%s_b_start" % self.tag, pabs, lands, 3 * len(pabs), _copies_to_chips)
        return token

    def to_sibling(self, after):
        _, recv = _split_wait("rs_%s_b_wait" % self.tag, self.state, after)
        sums = [_sum_partials(pa, rb, self.me1) for pa, rb in zip(self.own, recv)]
        lands = [lax.empty(s.shape, F32) for s in sums]
        self.state, token = _split_start("rs_%s_c_start" % self.tag, sums, lands, len(sums),
                                         _copies_to_sibling(lambda ref, c: ref))
        return token

    def finish(self, after):
        return list(zip(*_split_wait("rs_%s_c_wait" % self.tag, self.state, after)))


def _allreduce_small(v):
    def body(v_ref, out_ref, rbuf, send_sems, recv_sems):
        x, y, c = _place()
        out_ref[...] = v_ref[...]
        for s, peer in enumerate([(x, y, 1 - c), (1 - x, y, c), (x, 1 - y, c)]):
            cp = pltpu.make_async_remote_copy(out_ref, rbuf.at[s], send_sems.at[s], recv_sems.at[s],
                                              device_id=peer, device_id_type=MESH)
            cp.start()
            cp.wait()
            out_ref[...] = out_ref[...] + rbuf[s]

    vm = pl.BlockSpec(memory_space=pltpu.VMEM)
    return pl.pallas_call(
        body, name="allreduce_small", in_specs=[vm], out_specs=vm, out_shape=jax.ShapeDtypeStruct(v.shape, v.dtype),
        scratch_shapes=[pltpu.VMEM((3,) + v.shape, v.dtype), pltpu.SemaphoreType.DMA((3,)), pltpu.SemaphoreType.DMA((3,))],
        compiler_params=pltpu.CompilerParams(vmem_limit_bytes=VMEM_LIMIT),
    )(v)


def _adamw_math(w, g, m, v):
    m = ADAM_B1 * m + (1.0 - ADAM_B1) * g
    v = ADAM_B2 * v + (1.0 - ADAM_B2) * jnp.square(g)
    m_hat = m / (1.0 - ADAM_B1 ** ADAM_STEP)
    v_hat = v / (1.0 - ADAM_B2 ** ADAM_STEP)
    return -ADAM_LR * (m_hat / (jnp.sqrt(v_hat) + ADAM_EPS) + ADAM_WD * w), m, v


def _adamw_big_layer(layer, w, m, v, own, sib, c1, prev):
    _, r, cd = w.shape
    r2 = r // 2

    def body(c_ref, w_ref, m_ref, v_ref, own_ref, sib_ref, *rest):
        g_ref, d_ref, mo_ref, vo_ref, token = rest[-5:]
        g = jnp.where(pl.program_id(0) == c_ref[0], own_ref[...], sib_ref[...])
        g_ref[...] = g
        d_ref[...], mo_ref[...], vo_ref[...] = _adamw_math(w_ref[...], g, m_ref[...], v_ref[...])
        token[...] = jnp.zeros(token.shape, F32)

    blk = pl.BlockSpec((None, r2, cd), lambda hh, c_ref: (layer, hh, 0))
    half = pl.BlockSpec((r2, cd), lambda hh, c_ref: (0, 0))
    prev = () if prev is None else tuple(prev)
    outs = pl.pallas_call(
        body, name="adamw_big",
        grid_spec=pltpu.PrefetchScalarGridSpec(
            num_scalar_prefetch=1, grid=(2,), in_specs=[blk, blk, blk, half, half] + [ANY] * len(prev),
            out_specs=[blk] * 4 + [pl.BlockSpec((8, 128), lambda hh, c_ref: (0, 0))]),
        out_shape=[jax.ShapeDtypeStruct(w.shape, F32)] * 4 + [jax.ShapeDtypeStruct((8, 128), F32)],
        input_output_aliases={6 + i: i for i in range(len(prev))},
        compiler_params=_params("arbitrary"),
    )(c1, w, m, v, own, sib, *prev)
    return outs[:4], outs[4]


def _adamw_small(ws, ms, vs, gpack, offsets, strides, me1):
    n = len(ws)

    def body(me_ref, *refs):
        w_refs, m_refs, v_refs = refs[:n], refs[n:2 * n], refs[2 * n:3 * n]
        g_ref = refs[3 * n]
        outs = refs[3 * n + 1:]
        for i in range(n):
            rows = w_refs[i].shape[0]
            if strides[i]:
                g = g_ref[pl.ds(pl.multiple_of(offsets[i] + me_ref[0] * strides[i], 8), rows), :]
            else:
                g = g_ref[offsets[i]:offsets[i] + rows, :]
            d, m, v = _adamw_math(w_refs[i][...], g, m_refs[i][...], v_refs[i][...])
            outs[4 * i][...] = g
            outs[4 * i + 1][...] = d
            outs[4 * i + 2][...] = m
            outs[4 * i + 3][...] = v

    full = lambda a: pl.BlockSpec(a.shape, lambda i, me_ref: (0, 0))
    ins = list(ws) + list(ms) + list(vs) + [gpack]
    out_arrs = [w for w in ws for _ in range(4)]
    outs = pl.pallas_call(
        body, name="adamw_small",
        grid_spec=pltpu.PrefetchScalarGridSpec(num_scalar_prefetch=1, grid=(1,), in_specs=[full(a) for a in ins],
                                               out_specs=[full(a) for a in out_arrs]),
        out_shape=[jax.ShapeDtypeStruct(a.shape, F32) for a in out_arrs],
        compiler_params=_params("arbitrary"),
    )(me1, *ins)
    return [outs[4 * i:4 * i + 4] for i in range(n)]


LANES = 128
SUBLANES = 8


def _rows_of(size):
    return -(-size // (LANES * SUBLANES)) * SUBLANES


def _as_rows(a, rows=None):
    flat = a.reshape(-1)
    rows = _rows_of(flat.size) if rows is None else rows
    return jnp.pad(flat, (0, rows * LANES - flat.size)).reshape(rows, LANES)


def _to_shard_major(name, full):
    if name == "meta_tokens":
        return full.reshape(N_META, N_CHIPS, -1).transpose(1, 0, 2)
    if name == "convb_pw_w":
        return full.reshape(2, N_CHIPS, -1, D_CONV).transpose(1, 0, 2, 3)
    return full.reshape(full.shape[0], full.shape[1], N_CHIPS, -1).transpose(2, 0, 1, 3)


class _GradientSchedule:
    GROUPS = {"l1": [(1, "w_down"), (1, "w_up"), (1, "w_out"), (1, "w_in")], "a0": [(0, "w_down"), (0, "w_up")],
              "b0": [(0, "w_out")], "c0": [(0, "w_in")]}
    PLAN = {
        (1, "dw_in"): [("l1", "start")],
        (0, "mlp_bwd"): [("l1", "to_chips")],
        (0, "dw_up"): [("l1", "to_sibling"), ("a0", "start")],
        (0, "dw_out"): [("l1", "finish"), ("a0", "to_chips"), ("b0", "start")],
        (0, "mixer_bwd"): [("a0", "to_sibling"), ("b0", "to_chips")],
        (0, "dw_in"): [("c0", "start"), ("a0", "finish"), ("b0", "to_sibling"), ("c0", "to_chips")],
    }

    def __init__(self, w, mom, var, c1, me1):
        self.w, self.mom, self.var, self.c1, self.me1 = w, mom, var, c1, me1
        self.grads, self.chains, self.out = {}, {}, {}

    def grad(self, layer, name, g):
        self.grads[layer, name] = g

    def point(self, layer, kernel_name, after):
        return self.run(self.PLAN.get((layer, kernel_name), ()), after)

    def run(self, actions, after):
        deps = []
        for tag, stage in actions:
            if stage == "start":
                self.chains[tag] = _ReduceScatter(tag, [self.grads[lk] for lk in self.GROUPS[tag]], self.c1, self.me1)
                deps.append(self.chains[tag].start())
            elif stage == "finish":
                for (layer, k), (own, sib) in zip(self.GROUPS[tag], self.chains[tag].finish(after)):
                    self.out[k], token = _adamw_big_layer(layer, self.w[k], self.mom[k], self.var[k], own, sib, self.c1,
                                                          self.out.get(k))
                    deps.append(token)
            else:
                deps.append(getattr(self.chains[tag], stage)(after))
            after = deps[-1]
        self.last = after
        return tuple(deps)


def _from_shard_major(name, sm):
    if name == "meta_tokens":
        return sm.transpose(1, 0, 2).reshape(N_META, -1)
    if name == "convb_pw_w":
        return sm.transpose(1, 0, 2, 3).reshape(2, -1, D_CONV)
    return sm.transpose(1, 2, 0, 3).reshape(sm.shape[1], sm.shape[2], -1)


def kernel(x, meta_tokens, mix_norm_g, w_in, pool_w, pool_scale, convb_dw_w, convb_dw_b, convb_ln_g, convb_ln_b, convb_pw_w, rg_conv_w, rg_conv_b, rg_w_a, rg_b_a, rg_w_x, rg_b_x, rg_lambda, w_out, mlp_norm_g, w_up, w_down, final_norm_g, loss_target, m_meta_tokens, m_mix_norm_g, m_w_in, m_pool_w, m_pool_scale, m_convb_dw_w, m_convb_dw_b, m_convb_ln_g, m_convb_ln_b, m_convb_pw_w, m_rg_conv_w, m_rg_conv_b, m_rg_w_a, m_rg_b_a, m_rg_w_x, m_rg_b_x, m_rg_lambda, m_w_out, m_mlp_norm_g, m_w_up, m_w_down, m_final_norm_g, v_meta_tokens, v_mix_norm_g, v_w_in, v_pool_w, v_pool_scale, v_convb_dw_w, v_convb_dw_b, v_convb_ln_g, v_convb_ln_b, v_convb_pw_w, v_rg_conv_w, v_rg_conv_b, v_rg_w_a, v_rg_b_a, v_rg_w_x, v_rg_b_x, v_rg_lambda, v_w_out, v_mlp_norm_g, v_w_up, v_w_down, v_final_norm_g):
    given = dict(locals())
    w = {k: given[k] for k in WEIGHTS}
    mom = {k: given["m_" + k] for k in WEIGHTS}
    var = {k: given["v_" + k] for k in WEIGHTS}
    xi, yi, ci = _place()
    me1 = (2 * xi + yi).astype(jnp.int32).reshape(1)
    c1 = ci.astype(jnp.int32).reshape(1)

    small_rows = [_rows_of(w[k].size) for k in SMALL_SHARDED]
    small_pack = jnp.concatenate([_as_rows(w[k]) for k in SMALL_SHARDED])
    shard = lambda l, k: w[k][l].astype(BF16)
    w_in0, small_all = _gather_now([shard(0, "w_in"), small_pack])
    order = [[(0, "w_out"), (0, "w_up")], [(0, "w_down")], [(1, "w_in")], [(1, "w_out"), (1, "w_up")], [(1, "w_down")]]
    state, token = _gather_start([[shard(l, k) for l, k in g] for g in order], me1[0])
    landed = {}

    def fetch(l, k, after):
        if (l, k) == (0, "w_in"):
            raw = w_in0
        else:
            gi = [i for i, g in enumerate(order) if (l, k) in g][0]
            if gi not in landed:
                landed[gi] = _gather_wait(state[gi], after, "gather_wait_%d" % gi)
            raw = landed[gi][order[gi].index((l, k))]
        if k == "w_in":
            return raw.transpose(1, 0, 2).reshape(D_MODEL, D_IN)
        return raw.reshape(D_MODEL, D_MODEL) if k == "w_out" else raw

    wfull = dict(w)
    off = 0
    for k, rows in zip(SMALL_SHARDED, small_rows):
        sm = small_all[:, off:off + rows].reshape(N_CHIPS, -1)[:, :w[k].size].reshape((N_CHIPS,) + w[k].shape)
        wfull[k] = _from_shard_major(k, sm)
        off += rows
    wfull["mix_norm_g"] = w["mix_norm_g"] + token[0, 0]

    seq = x.shape[1]
    t_real = N_META + seq
    t_pad = -(-t_real // ROW_ALIGN) * ROW_ALIGN
    tail = jnp.zeros((t_pad - t_real, D_MODEL), F32)
    h = jnp.concatenate([wfull["meta_tokens"], x[0], tail])
    tgt = jnp.concatenate([jnp.zeros((N_META, D_MODEL), F32), loss_target[0], tail])
    sched = _GradientSchedule(w, mom, var, c1, me1)
    loss, dh, gsmall = _local_step(h, tgt, t_real, wfull, fetch, sched)
    grad_x = dh[N_META:t_real][None]
    gsmall["meta_tokens"] = dh[:N_META]

    pieces, offsets, strides = [], {}, {}
    row = 0
    for k in SMALL_REPL:
        rows = _rows_of(w[k].size)
        pieces.append(_as_rows(gsmall[k], rows))
        offsets[k], strides[k] = row, 0
        row += rows
    for k in SMALL_SHARDED:
        rows = _rows_of(w[k].size)
        sm = _to_shard_major(k, gsmall[k]).reshape(N_CHIPS, -1)
        pieces.append(jnp.pad(sm, ((0, 0), (0, rows * LANES - sm.shape[1]))).reshape(N_CHIPS * rows, LANES))
        offsets[k], strides[k] = row, rows
        row += N_CHIPS * rows
    gpack = _allreduce_small(jnp.concatenate(pieces) + sched.last[0, 0])
    sched.run([("c0", "to_sibling")], gpack)

    out = {}
    names = SMALL_REPL + SMALL_SHARDED
    as_rows = lambda a: a.reshape(-1, LANES) if a.size % (LANES * SUBLANES) == 0 or a.size < LANES * SUBLANES else _as_rows(a)
    res = _adamw_small([as_rows(w[k]) for k in names], [as_rows(mom[k]) for k in names], [as_rows(var[k]) for k in names],
                       gpack, [offsets[k] for k in names], [strides[k] for k in names], me1)
    for k, r4 in zip(names, res):
        out[k] = tuple(o.reshape(-1)[:w[k].size].reshape(w[k].shape) for o in r4)
    sched.run([("b0", "finish"), ("c0", "finish")], res[0][0])
    out.update(sched.out)

    loss = lax.psum(loss, ("x", "y", "c"))
    return (loss, grad_x, *[out[k][0] for k in WEIGHTS], *[out[k][1] for k in WEIGHTS],
            *[out[k][2] for k in WEIGHTS], *[out[k][3] for k in WEIGHTS])
```

```python
import functools

import jax
import jax.numpy as jnp
from jax import lax
from jax.experimental import pallas as pl
from jax.experimental.pallas import tpu as pltpu

F32, BF16 = jnp.float32, jnp.bfloat16
MESH = pl.DeviceIdType.MESH
ANY = pl.BlockSpec(memory_space=pl.ANY)

D_MODEL = 1024
N_META = 16
D_POOL = 256
D_CONV = 256
D_RNN = 512
D_IN = D_POOL + 2 * D_CONV + 2 * D_RNN
D_FF = 4096
FF_CHUNK = 1024
POOL_GW = 64
CONV_K = 31
RG_CONV_K = 4
RG_HD = 64
RG_C = 8.0
EPS = 1e-6
ADAM_LR, ADAM_B1, ADAM_B2, ADAM_EPS, ADAM_WD, ADAM_STEP = 0.001, 0.9, 0.999, 1e-08, 0.01, 10

HALO = 32
ROW_ALIGN = 256
TM_MIX = 256
TM_MAT = 768
TM_MLP_BWD = 384
N_CHIPS = 4
VMEM_LIMIT = 56 * 1024 * 1024

BIG = ("w_in", "w_out", "w_up", "w_down")
SMALL_SHARDED = ("meta_tokens", "convb_dw_w", "convb_pw_w", "rg_conv_w")
SMALL_REPL = ("mix_norm_g", "pool_w", "pool_scale", "convb_dw_b", "convb_ln_g", "convb_ln_b", "rg_conv_b",
              "rg_w_a", "rg_b_a", "rg_w_x", "rg_b_x", "rg_lambda", "mlp_norm_g", "final_norm_g")
WEIGHTS = ("meta_tokens", "mix_norm_g", "w_in", "pool_w", "pool_scale", "convb_dw_w", "convb_dw_b", "convb_ln_g",
           "convb_ln_b", "convb_pw_w", "rg_conv_w", "rg_conv_b", "rg_w_a", "rg_b_a", "rg_w_x", "rg_b_x",
           "rg_lambda", "w_out", "mlp_norm_g", "w_up", "w_down", "final_norm_g")


def _params(*sem):
    return pltpu.CompilerParams(dimension_semantics=sem, vmem_limit_bytes=VMEM_LIMIT)


def _row_tile(t, cap):
    best = None
    for tm in range(128, cap + 1, 128):
        if t % tm == 0:
            best = tm
    assert best is not None, (t, cap)
    return best


def _dot(a, b):
    return jnp.dot(a, b, preferred_element_type=F32)


def _dot_nt(a, b):
    return lax.dot_general(a, b, (((1,), (1,)), ((), ())), preferred_element_type=F32)


def _dot_tn(a, b):
    return lax.dot_general(a, b, (((0,), (0,)), ((), ())), preferred_element_type=F32)


def _rms(x):
    r = lax.rsqrt(jnp.mean(x * x, axis=-1, keepdims=True) + EPS)
    return r, x * r


def _rms_bwd(du, n, r, g):
    dn = du * g
    return r * (dn - n * jnp.mean(dn * n, axis=-1, keepdims=True))


def _sig(x):
    return jax.nn.sigmoid(x)


def _colsum(x):
    return jnp.sum(x, axis=0, keepdims=True)


def _one_minus_sq(a, log_a):
    x = 2.0 * log_a
    series = -x * (1.0 + x * (0.5 + x * (1.0 / 6 + x * (1.0 / 24 + x * (1.0 / 120)))))
    return jnp.where(x > -0.05, series, 1.0 - a * a)


_GELU_K0 = 0.7978845608028654
_GELU_K1 = 0.044715


def _gelu_and_grad(x):
    th = jnp.tanh(_GELU_K0 * (x + _GELU_K1 * x * x * x))
    val = 0.5 * x * (1.0 + th)
    grad = 0.5 * (1.0 + th) + 0.5 * x * (1.0 - th * th) * _GELU_K0 * (1.0 + 3.0 * _GELU_K1 * x * x)
    return val, grad


def _full(a):
    nd = a.ndim
    return pl.BlockSpec(a.shape, lambda *_: (0,) * nd)


def _resident(a):
    nd = a.ndim
    return pl.BlockSpec(a.shape, lambda *_: (0,) * nd, pipeline_mode=pl.Buffered(1))


def _after(body, n_in, deps):
    def wrapped(*refs):
        return body(*refs[:n_in], *refs[n_in + len(deps):])
    return wrapped


def _lane_sel(lane, a2, a4, a8, a16):
    return jnp.where(lane < POOL_GW, a2, jnp.where(lane < 2 * POOL_GW, a4, jnp.where(lane < 3 * POOL_GW, a8, a16)))


def _window_sums_back(src, tmp_a, tmp_b, tm):
    n = HALO + tm
    rows = lambda ref, lo, back: ref[pl.ds(lo - back, n - lo), :]
    tmp_a[pl.ds(8, n - 8), :] = rows(src, 8, 0) + rows(src, 8, 1)
    tmp_b[pl.ds(16, n - 16), :] = rows(tmp_a, 16, 0) + rows(tmp_a, 16, 2)
    s2 = rows(tmp_a, HALO, 0)
    tmp_a[pl.ds(24, n - 24), :] = rows(tmp_b, 24, 0) + rows(tmp_b, 24, 4)
    s8 = rows(tmp_a, HALO, 0)
    return s2, rows(tmp_b, HALO, 0), s8, s8 + rows(tmp_a, HALO, 8)


def _window_sums_ahead(src, tmp_a, tmp_b, tm):
    rows = lambda ref, n, ahead: ref[pl.ds(ahead, n), :]
    tmp_a[pl.ds(0, tm + 24), :] = rows(src, tm + 24, 0) + rows(src, tm + 24, 1)
    tmp_b[pl.ds(0, tm + 16), :] = rows(tmp_a, tm + 16, 0) + rows(tmp_a, tm + 16, 2)
    s2 = rows(tmp_a, tm, 0)
    tmp_a[pl.ds(0, tm + 8), :] = rows(tmp_b, tm + 8, 0) + rows(tmp_b, tm + 8, 4)
    s8 = rows(tmp_a, tm, 0)
    return s2, rows(tmp_b, tm, 0), s8, s8 + rows(tmp_a, tm, 8)


def _pool_counts(tm, t0):
    lane = lax.broadcasted_iota(jnp.int32, (tm, D_POOL), 1)
    row = lax.broadcasted_iota(jnp.int32, (tm, D_POOL), 0) + t0
    cnt = jnp.minimum(row + 1, _lane_sel(lane, 2, 4, 8, 16)).astype(F32)
    return lane, cnt


def _pool_fwd(ext_q, tmp_a, tmp_b, tm, t0):
    lane, cnt = _pool_counts(tm, t0)
    q = ext_q[pl.ds(HALO, tm), :]
    pooled = _lane_sel(lane, *_window_sums_back(ext_q, tmp_a, tmp_b, tm)) / cnt - q
    return pooled, lane, cnt


def _taps(src, w_of, offs, tm, zbuf):
    acc = None
    for r in range(8):
        ks = [k for k in range(len(offs)) if offs[k] % 8 == r]
        if not ks:
            continue
        rows = tm + (8 if r else 0)
        z = w_of(ks[0]) * src[pl.ds(offs[ks[0]] - r, rows), :]
        for k in ks[1:]:
            z = z + w_of(k) * src[pl.ds(offs[k] - r, rows), :]
        if r:
            zbuf[...] = z
            z = zbuf[pl.ds(r, tm), :]
        acc = z if acc is None else acc + z
    return acc


def _tap_grads(d_pad, src, offs, tm, g_ref, zbuf):
    ch = src.shape[-1]
    for r in range(8):
        ks = [k for k in range(len(offs)) if offs[k] % 8 == r]
        if not ks:
            continue
        rows = tm + (8 if r else 0)
        if r:
            zbuf[...] = d_pad[pl.ds(8 - r, rows), :]
        for k in ks:
            d = zbuf[...] if r else d_pad[pl.ds(8, rows), :]
            prod = d * src[pl.ds(offs[k] - r, rows), :]
            g_ref[k] += jnp.sum(prod.reshape(rows // 8, 8, ch), axis=0)


_CONV_OFFS = [HALO - (CONV_K - 1) + k for k in range(CONV_K)]


def _conv_fwd(ext_u, dww_ref, dwb, tm, zbuf):
    return dwb + _taps(ext_u, lambda k: dww_ref[k:k + 1, :], _CONV_OFFS, tm, zbuf)


def _ln_silu(c, lng, lnb):
    mu = jnp.mean(c, axis=-1, keepdims=True)
    cc = c - mu
    rstd = lax.rsqrt(jnp.mean(cc * cc, axis=-1, keepdims=True) + EPS)
    z = cc * rstd
    l = z * lng + lnb
    sl = _sig(l)
    return z, rstd, l, sl, l * sl


def _rg_fwd(ext_x, cw_ref, cb, wa, ba, wx, bx, lam, tm):
    xc = cb + cw_ref[0:1, :] * ext_x[pl.ds(HALO - (RG_CONV_K - 1), tm), :]
    for k in range(1, RG_CONV_K):
        xc = xc + cw_ref[k:k + 1, :] * ext_x[pl.ds(HALO - (RG_CONV_K - 1) + k, tm), :]
    xcb = xc.astype(BF16)
    r = _sig(_dot(xcb, wa) + ba)
    ig = _sig(_dot(xcb, wx) + bx)
    sp = jnp.maximum(-lam, 0.0) + jnp.log(1.0 + jnp.exp(-jnp.abs(lam)))
    log_a = (-RG_C * r) * sp
    a = jnp.exp(log_a)
    m = jnp.sqrt(_one_minus_sq(a, log_a))
    return xc, xcb, r, ig, sp, a, m


def _scan_rows(a_ref, b_ref, out_ref, carry, tm, reverse):
    rows = lax.broadcasted_iota(jnp.int32, (8, D_RNN), 0)
    ngrp = tm // 8

    def grp(gi, hb):
        st = pl.multiple_of((ngrp - 1 - gi if reverse else gi) * 8, 8)
        a8 = a_ref[pl.ds(st, 8), :]
        b8 = b_ref[pl.ds(st, 8), :]
        out = jnp.zeros((8, D_RNN), F32)
        for j in (range(7, -1, -1) if reverse else range(8)):
            aj = jnp.broadcast_to(a8[j:j + 1, :], (8, D_RNN))
            bj = jnp.broadcast_to(b8[j:j + 1, :], (8, D_RNN))
            if reverse:
                cur = bj + hb
                hb = aj * cur
            else:
                cur = aj * hb + bj
                hb = cur
            out = jnp.where(rows == j, cur, out)
        out_ref[pl.ds(st, 8), :] = out
        return hb

    carry[...] = lax.fori_loop(0, ngrp, grp, carry[...])


_MIX_W = ("wp", "psc", "dww", "dwb", "lng", "lnb", "wpw", "cw", "cb", "wa", "ba", "wx", "bx", "lam")


def _mixer_fwd(h, g, w_in, mw):
    t = h.shape[0]
    tm = _row_tile(t, TM_MIX)

    def body(h_ref, g_ref, win_ref, wp, psc, dww, dwb, lng, lnb, wpw, cw, cb, wa, ba, wx, bx, lam,
             y_ref, p_ref, u_ref, hs_ref, conv_ref, ext_q, ext_u, ext_x, tmp_a, tmp_b, zbuf, a_s, b_s, hcar):
        i = pl.program_id(0)

        @pl.when(i == 0)
        def _():
            ext_q[0:HALO, :] = jnp.zeros((HALO, D_POOL), F32)
            ext_u[0:HALO, :] = jnp.zeros((HALO, D_CONV), F32)
            ext_x[0:HALO, :] = jnp.zeros((HALO, D_RNN), F32)
            hcar[...] = jnp.zeros((8, D_RNN), F32)

        u = (_rms(h_ref[...])[1] * g_ref[...]).astype(BF16)
        u_ref[...] = u
        p_ref[...] = _dot(u, win_ref[...])

        ext_q[pl.ds(HALO, tm), :] = p_ref[:, 0:256]
        pooled, _, _ = _pool_fwd(ext_q, tmp_a, tmp_b, tm, i * tm)
        y_ref[:, 0:256] = (_dot(pooled.astype(BF16), wp[...]) * psc[...]).astype(BF16)

        ext_u[pl.ds(HALO, tm), :] = p_ref[:, 256:512] * _sig(p_ref[:, 512:768])
        conv = _conv_fwd(ext_u, dww, dwb[...], tm, zbuf)
        conv_ref[...] = conv
        act = _ln_silu(conv, lng[...], lnb[...])[4]
        y_ref[:, 256:512] = _dot(act.astype(BF16), wpw[...]).astype(BF16)

        ext_x[pl.ds(HALO, tm), :] = p_ref[:, 1280:1792]
        xc, _, _, ig, _, a, m = _rg_fwd(ext_x, cw, cb[...], wa[...], ba[...], wx[...], bx[...], lam[...], tm)
        a_s[...] = a
        b_s[...] = m * (ig * xc)
        _scan_rows(a_s, b_s, hs_ref, hcar, tm, reverse=False)
        y_ref[:, 512:1024] = (_gelu_and_grad(p_ref[:, 768:1280])[0] * hs_ref[...]).astype(BF16)

        ext_q[0:HALO, :] = ext_q[pl.ds(tm, HALO), :]
        ext_u[0:HALO, :] = ext_u[pl.ds(tm, HALO), :]
        ext_x[0:HALO, :] = ext_x[pl.ds(tm, HALO), :]

    ws = [mw[k] for k in _MIX_W]
    row = lambda w: pl.BlockSpec((tm, w), lambda i: (i, 0))
    return pl.pallas_call(
        body, name="mixer_fwd", grid=(t // tm,),
        in_specs=[row(D_MODEL), _full(g), _resident(w_in)] + [_full(w) for w in ws],
        out_specs=[row(D_MODEL), row(D_IN), row(D_MODEL), row(D_RNN), row(D_CONV)],
        out_shape=[jax.ShapeDtypeStruct((t, D_MODEL), BF16), jax.ShapeDtypeStruct((t, D_IN), F32),
                   jax.ShapeDtypeStruct((t, D_MODEL), BF16), jax.ShapeDtypeStruct((t, D_RNN), F32),
                   jax.ShapeDtypeStruct((t, D_CONV), F32)],
        scratch_shapes=[pltpu.VMEM((HALO + tm, D_POOL), F32), pltpu.VMEM((HALO + tm, D_CONV), F32),
                        pltpu.VMEM((HALO + tm, D_RNN), F32), pltpu.VMEM((HALO + tm, D_POOL), F32),
                        pltpu.VMEM((HALO + tm, D_POOL), F32), pltpu.VMEM((tm + 8, D_CONV), F32),
                        pltpu.VMEM((tm, D_RNN), F32), pltpu.VMEM((tm, D_RNN), F32), pltpu.VMEM((8, D_RNN), F32)],
        compiler_params=_params("arbitrary"),
    )(h, g, w_in, *ws)


_MIX_G = (("wp", (D_POOL, D_POOL)), ("psc", (1, D_POOL)), ("dww", (32, 8, D_CONV)), ("dwb", (1, D_CONV)),
          ("lng", (1, D_CONV)), ("lnb", (1, D_CONV)), ("wpw", (D_CONV, D_CONV)), ("cw", (8, D_RNN)),
          ("cb", (1, D_RNN)), ("wa", (D_RNN, D_RNN)), ("ba", (1, D_RNN)), ("wx", (D_RNN, D_RNN)),
          ("bx", (1, D_RNN)), ("lam", (1, D_RNN)), ("g1", (1, D_MODEL)))


def _mixer_bwd(p, dh1, hs, conv, h0, g1, w_out, w_in, mw, deps=()):
    t = p.shape[0]
    tm = _row_tile(t, TM_MIX)
    nt = t // tm
    hb = tm // HALO

    def body(p_ref, ph_ref, dh1_ref, hs_ref, hsh_ref, conv_ref, h0_ref, g1_ref, wout_ref, win_ref,
             wp, psc, dww, dwb, lng, lnb, wpw, cw, cb, wa, ba, wx, bx, lam,
             dp_ref, dh0_ref, g_wp, g_psc, g_dww, g_dwb, g_lng, g_lnb, g_wpw, g_cw, g_cb, g_wa, g_ba, g_wx, g_bx, g_lam, g_g1,
             ext_q, ext_u, ext_x, ext_h, ee, dc_s, dx_s, tmp_a, tmp_b, zbuf, d_pad, a_s, b_s, g_s, gcar, dy_ref, dp_s):
        step = pl.program_id(0)
        i = nt - 1 - step
        grads = (g_wp, g_psc, g_dww, g_dwb, g_lng, g_lnb, g_wpw, g_cw, g_cb, g_wa, g_ba, g_wx, g_bx, g_lam, g_g1)
        dy_ref[...] = dh1_ref[...].astype(BF16)
        dy_cols = lambda lo, hi: _dot_nt(dy_ref[...], wout_ref[lo:hi, :])

        @pl.when(step == 0)
        def _():
            for gr in grads:
                gr[...] = jnp.zeros(gr.shape, F32)
            ee[pl.ds(tm, HALO), :] = jnp.zeros((HALO, D_POOL), F32)
            dc_s[pl.ds(tm, HALO), :] = jnp.zeros((HALO, D_CONV), F32)
            dx_s[pl.ds(tm, HALO), :] = jnp.zeros((HALO, D_RNN), F32)
            d_pad[0:8, :] = jnp.zeros((8, D_CONV), F32)
            d_pad[pl.ds(tm + 8, 8), :] = jnp.zeros((8, D_CONV), F32)
            gcar[...] = jnp.zeros((8, D_RNN), F32)

        hm = jnp.where(i == 0, 0.0, 1.0)

        ext_q[0:HALO, :] = ph_ref[:, 0:256] * hm
        ext_q[pl.ds(HALO, tm), :] = p_ref[:, 0:256]
        pooled, lane, cnt = _pool_fwd(ext_q, tmp_a, tmp_b, tm, i * tm)
        pooled_b = pooled.astype(BF16)
        dya = dy_cols(0, 256)
        g_psc[...] += _colsum(dya * _dot(pooled_b, wp[...]))
        dmixed_b = (dya * psc[...]).astype(BF16)
        dpooled = _dot_nt(dmixed_b, wp[...])
        g_wp[...] += _dot_tn(pooled_b, dmixed_b)
        ee[0:tm, :] = dpooled / cnt
        dp_s[:, 0:256] = _lane_sel(lane, *_window_sums_ahead(ee, tmp_a, tmp_b, tm)) - dpooled
        ee[pl.ds(tm, HALO), :] = ee[0:HALO, :]

        v = p_ref[:, 256:512]
        s = _sig(p_ref[:, 512:768])
        ext_u[0:HALO, :] = ph_ref[:, 256:512] * _sig(ph_ref[:, 512:768]) * hm
        ext_u[pl.ds(HALO, tm), :] = v * s
        z, rstd, l, sl, act = _ln_silu(conv_ref[...], lng[...], lnb[...])
        dyb_b = dy_cols(256, 512).astype(BF16)
        dact = _dot_nt(dyb_b, wpw[...])
        g_wpw[...] += _dot_tn(act.astype(BF16), dyb_b)
        dl = dact * (sl * (1.0 + l * (1.0 - sl)))
        g_lng[...] += _colsum(dl * z)
        g_lnb[...] += _colsum(dl)
        dz = dl * lng[...]
        dc = rstd * (dz - jnp.mean(dz, axis=-1, keepdims=True) - z * jnp.mean(dz * z, axis=-1, keepdims=True))
        g_dwb[...] += _colsum(dc)
        dc_s[0:tm, :] = dc
        d_pad[pl.ds(8, tm), :] = dc
        _tap_grads(d_pad, ext_u, _CONV_OFFS, tm, g_dww, zbuf)
        du0 = _taps(dc_s, lambda j: dww[CONV_K - 1 - j:CONV_K - j, :], list(range(CONV_K)), tm, zbuf)
        dp_s[:, 256:512] = du0 * s
        dp_s[:, 512:768] = du0 * v * (s * (1.0 - s))
        dc_s[pl.ds(tm, HALO), :] = dc_s[0:HALO, :]

        ext_x[0:HALO, :] = ph_ref[:, 1280:1792] * hm
        ext_x[pl.ds(HALO, tm), :] = p_ref[:, 1280:1792]
        xc, xcb, r, ig, sp, a, m = _rg_fwd(ext_x, cw, cb[...], wa[...], ba[...], wx[...], bx[...], lam[...], tm)
        ext_h[0:HALO, :] = hsh_ref[...] * hm
        ext_h[pl.ds(HALO, tm), :] = hs_ref[...]
        dyc = dy_cols(512, 1024)
        gl, dgl = _gelu_and_grad(p_ref[:, 768:1280])
        dp_s[:, 768:1280] = dyc * hs_ref[...] * dgl
        a_s[...] = a
        b_s[...] = dyc * gl
        _scan_rows(a_s, b_s, g_s, gcar, tm, reverse=True)
        g = g_s[...]
        da = g * ext_h[pl.ds(HALO - 1, tm), :]
        dm = g * (ig * xc)
        dig = g * (m * xc)
        dlog_a = da * a - dm * (a * a) / m
        g_lam[...] += _colsum(dlog_a * (-RG_C * r)) * (-_sig(-lam[...]))
        dra = (dlog_a * (-RG_C * sp)) * (r * (1.0 - r))
        dia = dig * (ig * (1.0 - ig))
        g_ba[...] += _colsum(dra)
        g_bx[...] += _colsum(dia)
        dra_b = dra.astype(BF16)
        dia_b = dia.astype(BF16)
        dxc = g * (m * ig) + _dot_nt(dra_b, wa[...]) + _dot_nt(dia_b, wx[...])
        g_wa[...] += _dot_tn(xcb, dra_b)
        g_wx[...] += _dot_tn(xcb, dia_b)
        g_cb[...] += _colsum(dxc)
        dx_s[0:tm, :] = dxc
        for k in range(RG_CONV_K):
            g_cw[k:k + 1, :] += _colsum(dxc * ext_x[pl.ds(HALO - (RG_CONV_K - 1) + k, tm), :])
        dxin = cw[RG_CONV_K - 1:RG_CONV_K, :] * dxc
        for j in range(1, RG_CONV_K):
            dxin = dxin + cw[RG_CONV_K - 1 - j:RG_CONV_K - j, :] * dx_s[pl.ds(j, tm), :]
        dp_s[:, 1280:1792] = dxin
        dx_s[pl.ds(tm, HALO), :] = dx_s[0:HALO, :]

        dpb = dp_s[...].astype(BF16)
        dp_ref[...] = dpb
        du = _dot_nt(dpb, win_ref[...])
        r, n = _rms(h0_ref[...])
        g_g1[...] += _colsum(du * n)
        dh0_ref[...] = dh1_ref[...] + _rms_bwd(du, n, r, g1_ref[...])

    ws = [mw[k] for k in _MIX_W]
    tile = lambda w: pl.BlockSpec((tm, w), lambda s: (nt - 1 - s, 0))
    halo = lambda w: pl.BlockSpec((HALO, w), lambda s: (jnp.maximum((nt - 1 - s) * hb - 1, 0), 0))
    outs = pl.pallas_call(
        _after(body, 10 + len(ws), deps), name="mixer_bwd", grid=(nt,),
        in_specs=[tile(D_IN), halo(D_IN), tile(D_MODEL), tile(D_RNN), halo(D_RNN), tile(D_CONV), tile(D_MODEL), _full(g1),
                  _resident(w_out), _resident(w_in)] + [_full(w) for w in ws] + [ANY] * len(deps),
        out_specs=[tile(D_IN), tile(D_MODEL)] + [pl.BlockSpec(shp, lambda s, nd=len(shp): (0,) * nd) for _, shp in _MIX_G],
        out_shape=[jax.ShapeDtypeStruct((t, D_IN), BF16), jax.ShapeDtypeStruct((t, D_MODEL), F32)]
        + [jax.ShapeDtypeStruct(shp, F32) for _, shp in _MIX_G],
        scratch_shapes=[pltpu.VMEM((HALO + tm, D_POOL), F32), pltpu.VMEM((HALO + tm, D_CONV), F32),
                        pltpu.VMEM((HALO + tm, D_RNN), F32), pltpu.VMEM((HALO + tm, D_RNN), F32),
                        pltpu.VMEM((tm + HALO, D_POOL), F32), pltpu.VMEM((tm + HALO, D_CONV), F32),
                        pltpu.VMEM((tm + HALO, D_RNN), F32), pltpu.VMEM((HALO + tm, D_POOL), F32),
                        pltpu.VMEM((HALO + tm, D_POOL), F32), pltpu.VMEM((tm + 8, D_CONV), F32),
                        pltpu.VMEM((tm + 16, D_CONV), F32), pltpu.VMEM((tm, D_RNN), F32),
                        pltpu.VMEM((tm, D_RNN), F32), pltpu.VMEM((tm, D_RNN), F32), pltpu.VMEM((8, D_RNN), F32),
                        pltpu.VMEM((tm, D_MODEL), BF16), pltpu.VMEM((tm, D_IN), F32)],
        compiler_params=_params("arbitrary"),
    )(p, p, dh1, hs, hs, conv, h0, g1, w_out, w_in, *ws, *deps)
    return outs[0], outs[1], {k: o for (k, _), o in zip(_MIX_G, outs[2:])}


def _mid_fwd(y, h0, w_out, g, w_up):
    t = h0.shape[0]
    tm = _row_tile(t, TM_MAT)

    def body(y_ref, h0_ref, wo_ref, g_ref, wu_ref, h1_ref, u2_ref, f_ref):
        h1 = h0_ref[...] + _dot(y_ref[...], wo_ref[...])
        h1_ref[...] = h1
        u2 = (_rms(h1)[1] * g_ref[...]).astype(BF16)
        u2_ref[...] = u2
        for c in range(D_FF // FF_CHUNK):
            f_ref[:, c * FF_CHUNK:(c + 1) * FF_CHUNK] = _dot(u2, wu_ref[c]).astype(BF16)

    row = lambda w: pl.BlockSpec((tm, w), lambda i: (i, 0))
    return pl.pallas_call(
        body, name="mid_fwd", grid=(t // tm,),
        in_specs=[row(D_MODEL), row(D_MODEL), _resident(w_out), _full(g), _resident(w_up)],
        out_specs=[row(D_MODEL), row(D_MODEL), row(D_FF)],
        out_shape=[jax.ShapeDtypeStruct((t, D_MODEL), F32), jax.ShapeDtypeStruct((t, D_MODEL), BF16),
                   jax.ShapeDtypeStruct((t, D_FF), BF16)],
        compiler_params=_params("parallel"),
    )(y, h0, w_out, g, w_up)


def _down_fwd(f, h1, w_down):
    t = h1.shape[0]
    tm = _row_tile(t, TM_MAT)

    def body(f_ref, h1_ref, wd_ref, h2_ref):
        acc = h1_ref[...]
        for c in range(D_FF // FF_CHUNK):
            cols = slice(c * FF_CHUNK, (c + 1) * FF_CHUNK)
            a = jnp.square(jnp.maximum(f_ref[:, cols].astype(F32), 0.0)).astype(BF16)
            acc = acc + _dot(a, wd_ref[cols, :])
        h2_ref[...] = acc

    return pl.pallas_call(
        body, name="down_fwd", grid=(t // tm,),
        in_specs=[pl.BlockSpec((tm, D_FF), lambda i: (i, 0)), pl.BlockSpec((tm, D_MODEL), lambda i: (i, 0)), _resident(w_down)],
        out_specs=pl.BlockSpec((tm, D_MODEL), lambda i: (i, 0)),
        out_shape=jax.ShapeDtypeStruct((t, D_MODEL), F32),
        compiler_params=_params("parallel"),
    )(f, h1, w_down)


def _loss_head(h, g, tgt, t_real):
    t = h.shape[0]
    tm = _row_tile(t, TM_MAT)

    def body(h_ref, g_ref, tgt_ref, loss_ref, dh_ref, dg_ref):
        i = pl.program_id(0)

        @pl.when(i == 0)
        def _():
            loss_ref[...] = jnp.zeros(loss_ref.shape, F32)
            dg_ref[...] = jnp.zeros(dg_ref.shape, F32)

        r, n = _rms(h_ref[...])
        row = lax.broadcasted_iota(jnp.int32, (tm, 1), 0) + i * tm
        valid = jnp.logical_and(row >= N_META, row < t_real)
        diff = jnp.where(valid, n * g_ref[...] - tgt_ref[...], 0.0)
        loss_ref[...] += 0.5 * jnp.sum(jnp.mean(diff * diff, axis=-1, keepdims=True))
        dy = diff * (1.0 / D_MODEL)
        dg_ref[...] += _colsum(dy * n)
        dh_ref[...] = _rms_bwd(dy, n, r, g_ref[...])

    return pl.pallas_call(
        body, name="loss_head", grid=(t // tm,),
        in_specs=[pl.BlockSpec((tm, D_MODEL), lambda i: (i, 0)), _full(g), pl.BlockSpec((tm, D_MODEL), lambda i: (i, 0))],
        out_specs=[pl.BlockSpec((8, 128), lambda i: (0, 0)), pl.BlockSpec((tm, D_MODEL), lambda i: (i, 0)),
                   pl.BlockSpec((1, D_MODEL), lambda i: (0, 0))],
        out_shape=[jax.ShapeDtypeStruct((8, 128), F32), jax.ShapeDtypeStruct((t, D_MODEL), F32),
                   jax.ShapeDtypeStruct((1, D_MODEL), F32)],
        compiler_params=_params("arbitrary"),
    )(h, g, tgt)


def _mlp_bwd(dh2, f, h1, g, w_up, w_down, deps=()):
    t = dh2.shape[0]
    tm = _row_tile(t, TM_MLP_BWD)

    def body(dh2_ref, f_ref, wd_ref, wu_ref, h1_ref, g_ref, df_ref, dh1_ref, dg_ref):
        @pl.when(pl.program_id(0) == 0)
        def _():
            dg_ref[...] = jnp.zeros(dg_ref.shape, F32)

        dh2 = dh2_ref[...]
        dhb = dh2.astype(BF16)
        du2 = None
        for c in range(D_FF // FF_CHUNK):
            cols = slice(c * FF_CHUNK, (c + 1) * FF_CHUNK)
            dact = _dot_nt(dhb, wd_ref[c])
            df = (dact * (2.0 * jnp.maximum(f_ref[:, cols].astype(F32), 0.0))).astype(BF16)
            df_ref[:, cols] = df
            part = _dot_nt(df, wu_ref[c])
            du2 = part if du2 is None else du2 + part
        r, n = _rms(h1_ref[...])
        dg_ref[...] += _colsum(du2 * n)
        dh1_ref[...] = dh2 + _rms_bwd(du2, n, r, g_ref[...])

    row = lambda w: pl.BlockSpec((tm, w), lambda i: (i, 0))
    return pl.pallas_call(
        _after(body, 6, deps), name="mlp_bwd", grid=(t // tm,),
        in_specs=[row(D_MODEL), row(D_FF), _resident(w_down), _resident(w_up), row(D_MODEL), _full(g)] + [ANY] * len(deps),
        out_specs=[row(D_FF), row(D_MODEL), pl.BlockSpec((1, D_MODEL), lambda i: (0, 0))],
        out_shape=[jax.ShapeDtypeStruct((t, D_FF), BF16), jax.ShapeDtypeStruct((t, D_MODEL), F32),
                   jax.ShapeDtypeStruct((1, D_MODEL), F32)],
        compiler_params=_params("arbitrary"),
    )(dh2, f, w_down, w_up, h1, g, *deps)


def _tn_matmul(a, b, kc, nc, relu2, name, deps=()):
    t, k = a.shape
    n = b.shape[1]
    tt = _row_tile(t, TM_MAT)
    gk, gn = k // kc, n // nc

    def body(a_ref, b_ref, o_ref):
        @pl.when(pl.program_id(2) == 0)
        def _():
            o_ref[...] = jnp.zeros(o_ref.shape, F32)

        av = a_ref[...]
        if relu2:
            av = jnp.square(jnp.maximum(av.astype(F32), 0.0))
        o_ref[...] += _dot_tn(av.astype(BF16), b_ref[...].astype(BF16))

    return pl.pallas_call(
        _after(body, 2, deps), name=name, grid=(gk, gn, t // tt),
        in_specs=[pl.BlockSpec((tt, kc), lambda ik, jn, it: (it, ik)), pl.BlockSpec((tt, nc), lambda ik, jn, it: (it, jn))]
        + [ANY] * len(deps),
        out_specs=pl.BlockSpec((None, kc, nc), lambda ik, jn, it: (ik * gn + jn, 0, 0)),
        out_shape=jax.ShapeDtypeStruct((gk * gn, kc, nc), F32),
        compiler_params=_params("parallel", "parallel", "arbitrary"),
    )(a, b, *deps)


def _block_diag(blocks):
    nb, hd, _ = blocks.shape
    eye = jnp.eye(nb, dtype=blocks.dtype)
    return (blocks[:, :, None, :] * eye[:, None, :, None]).reshape(nb * hd, nb * hd)


def _diag_blocks(m, nb):
    hd = m.shape[0] // nb
    eye = jnp.eye(nb, dtype=m.dtype)
    return jnp.sum(m.reshape(nb, hd, nb, hd) * eye[:, None, :, None], axis=2)


def _mixer_weights(w, l):
    row = lambda a: a.reshape(1, -1)
    return dict(
        wp=_block_diag(w["pool_w"][l]).astype(BF16), psc=row(w["pool_scale"][l]),
        dww=jnp.pad(w["convb_dw_w"][l], ((0, 32 - CONV_K), (0, 0))), dwb=row(w["convb_dw_b"][l]),
        lng=row(w["convb_ln_g"][l]), lnb=row(w["convb_ln_b"][l]), wpw=w["convb_pw_w"][l].astype(BF16),
        cw=jnp.pad(w["rg_conv_w"][l], ((0, 8 - RG_CONV_K), (0, 0))), cb=row(w["rg_conv_b"][l]),
        wa=_block_diag(w["rg_w_a"][l]).astype(BF16), ba=row(w["rg_b_a"][l]),
        wx=_block_diag(w["rg_w_x"][l]).astype(BF16), bx=row(w["rg_b_x"][l]), lam=row(w["rg_lambda"][l]))


def _local_step(h, tgt, t_real, w, fetch, hooks):
    depth = 2
    saved = []
    big = []
    for l in range(depth):
        mw = _mixer_weights(w, l)
        g1 = w["mix_norm_g"][l].reshape(1, -1)
        g2 = w["mlp_norm_g"][l].reshape(1, -1)
        wl = dict(w_in=fetch(l, "w_in", h))
        y, p, u, hs, conv = _mixer_fwd(h, g1, wl["w_in"], mw)
        wl["w_out"], wl["w_up"] = fetch(l, "w_out", y), fetch(l, "w_up", y)
        h1, u2, f = _mid_fwd(y, h, wl["w_out"], g2, wl["w_up"])
        wl["w_down"] = fetch(l, "w_down", f)
        h2 = _down_fwd(f, h1, wl["w_down"].reshape(D_FF, D_MODEL))
        saved.append(dict(mw=mw, g1=g1, g2=g2, h0=h, p=p, u=u, y=y, hs=hs, conv=conv, h1=h1, u2=u2, f=f))
        big.append(wl)
        h = h2
    gf = w["final_norm_g"].reshape(1, -1)
    loss, dh, dgf = _loss_head(h, gf, tgt, t_real)

    gs = {k: [None] * depth for k in ("mix_norm_g", "mlp_norm_g", "pool_w", "pool_scale", "convb_dw_w", "convb_dw_b",
                                      "convb_ln_g", "convb_ln_b", "convb_pw_w", "rg_conv_w", "rg_conv_b", "rg_w_a",
                                      "rg_b_a", "rg_w_x", "rg_b_x", "rg_lambda")}
    deps = ()
    for l in reversed(range(depth)):
        s, wl = saved[l], big[l]
        df, dh1, dg2 = _mlp_bwd(dh, s["f"], s["h1"], s["g2"], wl["w_up"], wl["w_down"], deps)
        deps = hooks.point(l, "mlp_bwd", dh1)
        g_down = _tn_matmul(s["f"], dh, FF_CHUNK, D_MODEL, True, "dw_down", deps)
        hooks.grad(l, "w_down", g_down)
        deps = hooks.point(l, "dw_down", g_down)
        g_up = _tn_matmul(s["u2"], df, D_MODEL, FF_CHUNK, False, "dw_up", deps)
        hooks.grad(l, "w_up", g_up)
        deps = hooks.point(l, "dw_up", g_up)
        g_out = _tn_matmul(s["y"], dh1, D_MODEL, D_MODEL, False, "dw_out", deps)
        hooks.grad(l, "w_out", g_out.reshape(N_CHIPS, D_MODEL // N_CHIPS, D_MODEL))
        deps = hooks.point(l, "dw_out", g_out)
        dp, dh, mg = _mixer_bwd(s["p"], dh1, s["hs"], s["conv"], s["h0"], s["g1"], wl["w_out"], wl["w_in"], s["mw"], deps)
        deps = hooks.point(l, "mixer_bwd", dh)
        g_in = _tn_matmul(s["u"], dp, D_MODEL, D_IN, False, "dw_in", deps)
        g_in = g_in[0].reshape(D_MODEL, N_CHIPS, D_IN // N_CHIPS).transpose(1, 0, 2)
        hooks.grad(l, "w_in", g_in)
        deps = hooks.point(l, "dw_in", g_in)
        gs["mix_norm_g"][l] = mg["g1"][0]
        gs["mlp_norm_g"][l] = dg2[0]
        gs["pool_w"][l] = _diag_blocks(mg["wp"], D_POOL // POOL_GW)
        gs["pool_scale"][l] = mg["psc"][0]
        gs["convb_dw_w"][l] = jnp.sum(mg["dww"][:CONV_K], axis=1)
        gs["convb_dw_b"][l] = mg["dwb"][0]
        gs["convb_ln_g"][l] = mg["lng"][0]
        gs["convb_ln_b"][l] = mg["lnb"][0]
        gs["convb_pw_w"][l] = mg["wpw"]
        gs["rg_conv_w"][l] = mg["cw"][:RG_CONV_K]
        gs["rg_conv_b"][l] = mg["cb"][0]
        gs["rg_w_a"][l] = _diag_blocks(mg["wa"], D_RNN // RG_HD)
        gs["rg_b_a"][l] = mg["ba"][0]
        gs["rg_w_x"][l] = _diag_blocks(mg["wx"], D_RNN // RG_HD)
        gs["rg_b_x"][l] = mg["bx"][0]
        gs["rg_lambda"][l] = mg["lam"][0]
    gsmall = {k: jnp.stack(v) for k, v in gs.items()}
    gsmall["final_norm_g"] = dgf[0]
    return loss[0, 0], dh, gsmall


def _place():
    return lax.axis_index("x"), lax.axis_index("y"), lax.axis_index("c")


def _other_chips(x, y):
    return [(1 - x, y), (x, 1 - y), (1 - x, 1 - y)]


def _gather_now(srcs):
    ns = len(srcs)
    out_shape = [jax.ShapeDtypeStruct((N_CHIPS,) + s.shape, s.dtype) for s in srcs]

    def body(*refs):
        src_refs, dst_refs = refs[:ns], refs[ns:2 * ns]
        send_sems, recv_sems, loc_sems = refs[2 * ns:]
        x, y, c = _place()
        me = 2 * x + y
        local, remote = [], []
        for n in range(ns):
            cp = pltpu.make_async_copy(src_refs[n], dst_refs[n].at[me], loc_sems.at[n])
            cp.start()
            local.append(cp)
            for j, (px, py) in enumerate(_other_chips(x, y)):
                out = pltpu.make_async_remote_copy(src_refs[n], dst_refs[n].at[me], send_sems.at[3 * n + j],
                                                   recv_sems.at[3 * n + j], device_id=(px, py, c), device_id_type=MESH)
                out.start()
                remote.append(pltpu.make_async_remote_copy(src_refs[n], dst_refs[n].at[2 * px + py], send_sems.at[3 * n + j],
                                                           recv_sems.at[3 * n + j], device_id=(px, py, c), device_id_type=MESH))
        for cp in remote:
            cp.wait()
        for cp in local:
            cp.wait()

    return pl.pallas_call(
        body, name="gather_now", in_specs=[ANY] * ns, out_specs=[ANY] * ns, out_shape=out_shape,
        scratch_shapes=[pltpu.SemaphoreType.DMA((3 * ns,)), pltpu.SemaphoreType.DMA((3 * ns,)), pltpu.SemaphoreType.DMA((ns,))],
    )(*srcs)


HBM_SPEC = pl.BlockSpec(memory_space=pltpu.HBM)
SEM_SPEC = pl.BlockSpec(memory_space=pltpu.SEMAPHORE)
DATAFLOW = pltpu.SideEffectType.DATAFLOW_SIDE_EFFECTING


def _gather_copies(src_refs, land_refs, send_sem, recv_sem, first):
    x, y, c = _place()
    me = 2 * x + y
    out = []
    for n in range(len(src_refs)):
        for j, (px, py) in enumerate(_other_chips(x, y)):
            out.append(pltpu.make_async_remote_copy(src_refs[n], land_refs[n].at[me], send_sem.at[first + 3 * n + j],
                                                    recv_sem.at[first + 3 * n + j], device_id=(px, py, c), device_id_type=MESH))
    return out


def _gather_start(groups, me):
    srcs = [pltpu.with_memory_space_constraint(s, pltpu.HBM) for g in groups for s in g]
    lands = [pltpu.with_memory_space_constraint(
        lax.dynamic_update_slice(jnp.zeros((N_CHIPS,) + s.shape, s.dtype), s[None], (me,) + (0,) * s.ndim), pltpu.HBM)
        for g in groups for s in g]
    n, ng = len(srcs), len(groups)
    first = [sum(len(g) for g in groups[:i]) for i in range(ng)]

    def body(*refs):
        src_refs, land_refs = refs[:n], refs[n:2 * n]
        sems = refs[2 * n:2 * n + 2 * ng]
        token = refs[-1]
        for gi, g in enumerate(groups):
            lo, hi = first[gi], first[gi] + len(g)
            for cp in _gather_copies(src_refs[lo:hi], land_refs[lo:hi], sems[2 * gi], sems[2 * gi + 1], 0):
                cp.start()
        token[...] = jnp.zeros(token.shape, token.dtype)

    sem_shapes = [pltpu.SemaphoreType.DMA((3 * len(g),)) for g in groups for _ in range(2)]
    outs = pl.pallas_call(
        body, name="gather_start",
        out_shape=sem_shapes + [pltpu.HBM(a.shape, a.dtype) for a in srcs + lands] + [jax.ShapeDtypeStruct((8, 128), F32)],
        in_specs=[HBM_SPEC] * (2 * n),
        out_specs=[SEM_SPEC] * (2 * ng) + [HBM_SPEC] * (2 * n) + [pl.BlockSpec(memory_space=pltpu.VMEM)],
        input_output_aliases={i: 2 * ng + i for i in range(2 * n)},
        compiler_params=pltpu.CompilerParams(has_side_effects=DATAFLOW),
    )(*srcs, *lands)
    sems, thru, token = outs[:2 * ng], outs[2 * ng:2 * ng + 2 * n], outs[-1]
    state = []
    for gi, g in enumerate(groups):
        lo, hi = first[gi], first[gi] + len(g)
        state.append((sems[2 * gi], sems[2 * gi + 1], thru[lo:hi], thru[n + lo:n + hi]))
    return state, token


def _gather_wait(state, after, name):
    send_sem, recv_sem, srcs, lands = state
    n = len(srcs)

    def body(*refs):
        src_refs, land_refs = refs[:n], refs[n:2 * n]
        send, recv = refs[2 * n], refs[2 * n + 1]
        for cp in _gather_copies(src_refs, land_refs, send, recv, 0):
            cp.wait_send()
            cp.wait_recv()

    outs = pl.pallas_call(
        body, name=name,
        out_shape=[pltpu.HBM(a.shape, a.dtype) for a in list(srcs) + list(lands)],
        in_specs=[HBM_SPEC] * (2 * n) + [SEM_SPEC, SEM_SPEC, ANY],
        out_specs=[HBM_SPEC] * (2 * n),
        input_output_aliases={i: i for i in range(2 * n)},
        compiler_params=pltpu.CompilerParams(has_side_effects=DATAFLOW),
    )(*srcs, *lands, send_sem, recv_sem, after)
    return outs[n:]


def _add_halves(g, recv, c1):
    nk, r, cd = g.shape
    r2 = r // 2

    def body(c_ref, g_ref, r_ref, pa_ref, pab_ref):
        s = g_ref[...] + r_ref[...]
        pa_ref[...] = s
        pab_ref[...] = s.astype(BF16)

    blk = pl.BlockSpec((None, r2, cd), lambda k, c_ref: (k, 0, 0))
    return pl.pallas_call(
        body, name="rs_add_halves",
        grid_spec=pltpu.PrefetchScalarGridSpec(
            num_scalar_prefetch=1, grid=(nk,),
            in_specs=[pl.BlockSpec((None, r2, cd), lambda k, c_ref: (k, c_ref[0], 0)), blk], out_specs=[blk, blk]),
        out_shape=[jax.ShapeDtypeStruct((nk, r2, cd), F32), jax.ShapeDtypeStruct((nk, r2, cd), BF16)],
        compiler_params=_params("parallel"),
    )(c1, g, recv)


def _sum_partials(pa, recv, me1):
    nk, r2, cd = pa.shape

    def body(me_ref, pa_ref, r_ref, s_ref):
        s_ref[...] = ((pa_ref[...] + r_ref[0].astype(F32)) + r_ref[1].astype(F32)) + r_ref[2].astype(F32)

    return pl.pallas_call(
        body, name="rs_sum_partials",
        grid_spec=pltpu.PrefetchScalarGridSpec(
            num_scalar_prefetch=1, grid=(1,),
            in_specs=[pl.BlockSpec((None, r2, cd), lambda i, me_ref: (me_ref[0], 0, 0)),
                      pl.BlockSpec((3, r2, cd), lambda i, me_ref: (0, 0, 0))],
            out_specs=pl.BlockSpec((r2, cd), lambda i, me_ref: (0, 0))),
        out_shape=jax.ShapeDtypeStruct((r2, cd), F32),
        compiler_params=_params("arbitrary"),
    )(me1, pa, recv)


def _split_start(name, srcs, lands, ncopies, make_copies):
    srcs = [pltpu.with_memory_space_constraint(s, pltpu.HBM) for s in srcs]
    lands = [pltpu.with_memory_space_constraint(a, pltpu.HBM) for a in lands]
    n, m = len(srcs), len(lands)

    def body(*refs):
        src_refs, land_refs = refs[:n], refs[n:n + m]
        send, recv, token = refs[n + m], refs[n + m + 1], refs[-1]
        for cp in make_copies(src_refs, land_refs, send, recv):
            cp.start()
        token[...] = jnp.zeros(token.shape, token.dtype)

    outs = pl.pallas_call(
        body, name=name,
        out_shape=[pltpu.SemaphoreType.DMA((ncopies,)), pltpu.SemaphoreType.DMA((ncopies,))]
        + [pltpu.HBM(a.shape, a.dtype) for a in srcs + lands] + [jax.ShapeDtypeStruct((8, 128), F32)],
        in_specs=[HBM_SPEC] * (n + m),
        out_specs=[SEM_SPEC, SEM_SPEC] + [HBM_SPEC] * (n + m) + [pl.BlockSpec(memory_space=pltpu.VMEM)],
        input_output_aliases={i: 2 + i for i in range(n + m)},
        compiler_params=pltpu.CompilerParams(has_side_effects=DATAFLOW),
    )(*srcs, *lands)
    return (outs[0], outs[1], outs[2:2 + n], outs[2 + n:2 + n + m], make_copies), outs[-1]


def _split_wait(name, state, after):
    send_sem, recv_sem, srcs, lands, make_copies = state
    n, m = len(srcs), len(lands)

    def body(*refs):
        src_refs, land_refs = refs[:n], refs[n:n + m]
        for cp in make_copies(src_refs, land_refs, refs[n + m], refs[n + m + 1]):
            cp.wait_send()
            cp.wait_recv()

    outs = pl.pallas_call(
        body, name=name,
        out_shape=[pltpu.HBM(a.shape, a.dtype) for a in list(srcs) + list(lands)],
        in_specs=[HBM_SPEC] * (n + m) + [SEM_SPEC, SEM_SPEC, ANY],
        out_specs=[HBM_SPEC] * (n + m),
        input_output_aliases={i: i for i in range(n + m)},
        compiler_params=pltpu.CompilerParams(has_side_effects=DATAFLOW),
    )(*srcs, *lands, send_sem, recv_sem, after)
    return outs[:n], outs[n:]


def _copies_to_sibling(src_of):
    def make(src_refs, land_refs, send, recv):
        x, y, c = _place()
        return [pltpu.make_async_remote_copy(src_of(src_refs[i], c), land_refs[i], send.at[i], recv.at[i],
                                             device_id=(x, y, 1 - c), device_id_type=MESH) for i in range(len(src_refs))]
    return make


def _copies_to_chips(src_refs, land_refs, send, recv):
    x, y, c = _place()
    return [pltpu.make_async_remote_copy(src_refs[i].at[2 * px + py], land_refs[i].at[j], send.at[3 * i + j], recv.at[3 * i + j],
                                         device_id=(px, py, c), device_id_type=MESH)
            for i in range(len(src_refs)) for j, (px, py) in enumerate(_other_chips(x, y))]


def _other_half_rows(ref, c):
    r2 = ref.shape[1] // 2
    return ref.at[:, pl.ds(pl.multiple_of((1 - c) * r2, 8), r2)]


class _ReduceScatter:
    def __init__(self, tag, grads, c1, me1):
        self.tag, self.grads, self.c1, self.me1 = tag, grads, c1, me1

    def start(self):
        lands = [lax.empty((g.shape[0], g.shape[1] // 2, g.shape[2]), F32) for g in self.grads]
        self.state, token = _split_start("rs_%s_a_start" % self.tag, self.grads, lands, len(self.grads),
                                         _copies_to_sibling(_other_half_rows))
        return token

    def to_chips(self, after):
        grads, recv = _split_wait("rs_%s_a_wait" % self.tag, self.state, after)
        added = [_add_halves(g, r, self.c1) for g, r in zip(grads, recv)]
        self.own = [pa for pa, _ in added]
        pabs = [pab for _, pab in added]
        lands = [lax.empty((3,) + p.shape[1:], BF16) for p in pabs]
        self.state, token = _split_start("rs_%s_b_start" % self.tag, pabs, lands, 3 * len(pabs), _copies_to_chips)
        return token

    def to_sibling(self, after):
        _, recv = _split_wait("rs_%s_b_wait" % self.tag, self.state, after)
        sums = [_sum_partials(pa, rb, self.me1) for pa, rb in zip(self.own, recv)]
        lands = [lax.empty(s.shape, F32) for s in sums]
        self.state, token = _split_start("rs_%s_c_start" % self.tag, sums, lands, len(sums),
                                         _copies_to_sibling(lambda ref, c: ref))
        return token

    def finish(self, after):
        return list(zip(*_split_wait("rs_%s_c_wait" % self.tag, self.state, after)))


def _allreduce_small(vs):
    n = len(vs)

    def body(*refs):
        v_refs, out_refs, rbufs = refs[:n], refs[n:2 * n], refs[2 * n:3 * n]
        send_sems, recv_sems = refs[3 * n:]
        x, y, c = _place()
        for i in range(n):
            out_refs[i][...] = v_refs[i][...]
        for s, peer in enumerate([(x, y, 1 - c), (1 - x, y, c), (x, 1 - y, c)]):
            copies = [pltpu.make_async_remote_copy(out_refs[i], rbufs[i].at[s], send_sems.at[s * n + i], recv_sems.at[s * n + i],
                                                   device_id=peer, device_id_type=MESH) for i in range(n)]
            for cp in copies:
                cp.start()
            for cp in copies:
                cp.wait()
            for i in range(n):
                out_refs[i][...] = out_refs[i][...] + rbufs[i][s]

    vm = pl.BlockSpec(memory_space=pltpu.VMEM)
    return pl.pallas_call(
        body, name="allreduce_small", in_specs=[vm] * n, out_specs=[vm] * n,
        out_shape=[jax.ShapeDtypeStruct(v.shape, v.dtype) for v in vs],
        scratch_shapes=[pltpu.VMEM((3,) + v.shape, v.dtype) for v in vs]
        + [pltpu.SemaphoreType.DMA((3 * n,)), pltpu.SemaphoreType.DMA((3 * n,))],
        compiler_params=pltpu.CompilerParams(vmem_limit_bytes=VMEM_LIMIT),
    )(*vs)


def _adamw_math(w, g, m, v):
    m = ADAM_B1 * m + (1.0 - ADAM_B1) * g
    v = ADAM_B2 * v + (1.0 - ADAM_B2) * jnp.square(g)
    m_hat = m / (1.0 - ADAM_B1 ** ADAM_STEP)
    v_hat = v / (1.0 - ADAM_B2 ** ADAM_STEP)
    return -ADAM_LR * (m_hat / (jnp.sqrt(v_hat) + ADAM_EPS) + ADAM_WD * w), m, v


def _adamw_big_layer(layer, w, m, v, own, sib, c1, prev):
    _, r, cd = w.shape
    r2 = r // 2

    def body(c_ref, w_ref, m_ref, v_ref, own_ref, sib_ref, *rest):
        g_ref, d_ref, mo_ref, vo_ref, token = rest[-5:]
        g = jnp.where(pl.program_id(0) == c_ref[0], own_ref[...], sib_ref[...])
        g_ref[...] = g
        d_ref[...], mo_ref[...], vo_ref[...] = _adamw_math(w_ref[...], g, m_ref[...], v_ref[...])
        token[...] = jnp.zeros(token.shape, F32)

    blk = pl.BlockSpec((None, r2, cd), lambda hh, c_ref: (layer, hh, 0))
    half = pl.BlockSpec((r2, cd), lambda hh, c_ref: (0, 0))
    prev = () if prev is None else tuple(prev)
    outs = pl.pallas_call(
        body, name="adamw_big",
        grid_spec=pltpu.PrefetchScalarGridSpec(
            num_scalar_prefetch=1, grid=(2,), in_specs=[blk, blk, blk, half, half] + [ANY] * len(prev),
            out_specs=[blk] * 4 + [pl.BlockSpec((8, 128), lambda hh, c_ref: (0, 0))]),
        out_shape=[jax.ShapeDtypeStruct(w.shape, F32)] * 4 + [jax.ShapeDtypeStruct((8, 128), F32)],
        input_output_aliases={6 + i: i for i in range(len(prev))},
        compiler_params=_params("arbitrary"),
    )(c1, w, m, v, own, sib, *prev)
    return outs[:4], outs[4]


def _adamw_small(ws, gs, ms, vs):
    n = len(ws)

    def body(*refs):
        w_refs, g_refs, m_refs, v_refs = refs[:n], refs[n:2 * n], refs[2 * n:3 * n], refs[3 * n:4 * n]
        outs = refs[4 * n:]
        for i in range(n):
            outs[3 * i][...], outs[3 * i + 1][...], outs[3 * i + 2][...] = _adamw_math(
                w_refs[i][...], g_refs[i][...], m_refs[i][...], v_refs[i][...])

    vm = pl.BlockSpec(memory_space=pltpu.VMEM)
    outs = pl.pallas_call(
        body, name="adamw_small", in_specs=[vm] * (4 * n), out_specs=[vm] * (3 * n),
        out_shape=[jax.ShapeDtypeStruct(w.shape, F32) for w in ws for _ in range(3)],
        compiler_params=pltpu.CompilerParams(vmem_limit_bytes=VMEM_LIMIT),
    )(*ws, *gs, *ms, *vs)
    return [outs[3 * i:3 * i + 3] for i in range(n)]


LANES = 128
SUBLANES = 8
SHARDED_AXIS = {"meta_tokens": 1, "convb_dw_w": 2, "convb_pw_w": 1, "rg_conv_w": 2}


def _rows_of(size):
    return -(-size // (LANES * SUBLANES)) * SUBLANES


def _as_rows(a, rows=None):
    flat = a.reshape(-1)
    rows = _rows_of(flat.size) if rows is None else rows
    return jnp.pad(flat, (0, rows * LANES - flat.size)).reshape(rows, LANES)


class _GradientSchedule:
    GROUPS = {"l1": [(1, "w_down"), (1, "w_up"), (1, "w_out"), (1, "w_in")], "a0": [(0, "w_down"), (0, "w_up")],
              "b0": [(0, "w_out")], "c0": [(0, "w_in")]}
    PLAN = {
        (1, "dw_in"): [("l1", "start")],
        (0, "mlp_bwd"): [("l1", "to_chips")],
        (0, "dw_up"): [("l1", "to_sibling"), ("a0", "start")],
        (0, "dw_out"): [("l1", "finish"), ("a0", "to_chips"), ("b0", "start")],
        (0, "mixer_bwd"): [("a0", "to_sibling"), ("b0", "to_chips")],
        (0, "dw_in"): [("c0", "start"), ("a0", "finish"), ("b0", "to_sibling"), ("c0", "to_chips")],
    }

    def __init__(self, w, mom, var, c1, me1):
        self.w, self.mom, self.var, self.c1, self.me1 = w, mom, var, c1, me1
        self.grads, self.chains, self.out = {}, {}, {}

    def grad(self, layer, name, g):
        self.grads[layer, name] = g

    def point(self, layer, kernel_name, after):
        return self.run(self.PLAN.get((layer, kernel_name), ()), after)

    def run(self, actions, after):
        deps = []
        for tag, stage in actions:
            if stage == "start":
                self.chains[tag] = _ReduceScatter(tag, [self.grads[lk] for lk in self.GROUPS[tag]], self.c1, self.me1)
                deps.append(self.chains[tag].start())
            elif stage == "finish":
                for (layer, k), (own, sib) in zip(self.GROUPS[tag], self.chains[tag].finish(after)):
                    self.out[k], token = _adamw_big_layer(layer, self.w[k], self.mom[k], self.var[k], own, sib, self.c1,
                                                          self.out.get(k))
                    deps.append(token)
            else:
                deps.append(getattr(self.chains[tag], stage)(after))
            after = deps[-1]
        self.last = after
        return tuple(deps)


def _from_shard_major(name, sm):
    if name == "meta_tokens":
        return sm.transpose(1, 0, 2).reshape(N_META, -1)
    if name == "convb_pw_w":
        return sm.transpose(1, 0, 2, 3).reshape(2, -1, D_CONV)
    return sm.transpose(1, 2, 0, 3).reshape(sm.shape[1], sm.shape[2], -1)


def kernel(x, meta_tokens, mix_norm_g, w_in, pool_w, pool_scale, convb_dw_w, convb_dw_b, convb_ln_g, convb_ln_b, convb_pw_w, rg_conv_w, rg_conv_b, rg_w_a, rg_b_a, rg_w_x, rg_b_x, rg_lambda, w_out, mlp_norm_g, w_up, w_down, final_norm_g, loss_target, m_meta_tokens, m_mix_norm_g, m_w_in, m_pool_w, m_pool_scale, m_convb_dw_w, m_convb_dw_b, m_convb_ln_g, m_convb_ln_b, m_convb_pw_w, m_rg_conv_w, m_rg_conv_b, m_rg_w_a, m_rg_b_a, m_rg_w_x, m_rg_b_x, m_rg_lambda, m_w_out, m_mlp_norm_g, m_w_up, m_w_down, m_final_norm_g, v_meta_tokens, v_mix_norm_g, v_w_in, v_pool_w, v_pool_scale, v_convb_dw_w, v_convb_dw_b, v_convb_ln_g, v_convb_ln_b, v_convb_pw_w, v_rg_conv_w, v_rg_conv_b, v_rg_w_a, v_rg_b_a, v_rg_w_x, v_rg_b_x, v_rg_lambda, v_w_out, v_mlp_norm_g, v_w_up, v_w_down, v_final_norm_g):
    given = dict(locals())
    w = {k: given[k] for k in WEIGHTS}
    mom = {k: given["m_" + k] for k in WEIGHTS}
    var = {k: given["v_" + k] for k in WEIGHTS}
    xi, yi, ci = _place()
    me1 = (2 * xi + yi).astype(jnp.int32).reshape(1)
    c1 = ci.astype(jnp.int32).reshape(1)

    small_rows = [_rows_of(w[k].size) for k in SMALL_SHARDED]
    small_pack = jnp.concatenate([_as_rows(w[k]) for k in SMALL_SHARDED])
    shard = lambda l, k: w[k][l].astype(BF16)
    w_in0, small_all = _gather_now([shard(0, "w_in"), small_pack])
    order = [[(0, "w_out"), (0, "w_up")], [(0, "w_down")], [(1, "w_in")], [(1, "w_out"), (1, "w_up")], [(1, "w_down")]]
    state, token = _gather_start([[shard(l, k) for l, k in g] for g in order], me1[0])
    landed = {}

    def fetch(l, k, after):
        if (l, k) == (0, "w_in"):
            raw = w_in0
        else:
            gi = [i for i, g in enumerate(order) if (l, k) in g][0]
            if gi not in landed:
                landed[gi] = _gather_wait(state[gi], after, "gather_wait_%d" % gi)
            raw = landed[gi][order[gi].index((l, k))]
        if k == "w_in":
            return raw.transpose(1, 0, 2).reshape(D_MODEL, D_IN)
        return raw.reshape(D_MODEL, D_MODEL) if k == "w_out" else raw

    wfull = dict(w)
    off = 0
    for k, rows in zip(SMALL_SHARDED, small_rows):
        sm = small_all[:, off:off + rows].reshape(N_CHIPS, -1)[:, :w[k].size].reshape((N_CHIPS,) + w[k].shape)
        wfull[k] = _from_shard_major(k, sm)
        off += rows
    wfull["mix_norm_g"] = w["mix_norm_g"] + token[0, 0]

    seq = x.shape[1]
    t_real = N_META + seq
    t_pad = -(-t_real // ROW_ALIGN) * ROW_ALIGN
    tail = jnp.zeros((t_pad - t_real, D_MODEL), F32)
    h = jnp.concatenate([wfull["meta_tokens"], x[0], tail])
    tgt = jnp.concatenate([jnp.zeros((N_META, D_MODEL), F32), loss_target[0], tail])
    sched = _GradientSchedule(w, mom, var, c1, me1)
    loss, dh, gsmall = _local_step(h, tgt, t_real, wfull, fetch, sched)
    grad_x = dh[N_META:t_real][None]
    gsmall["meta_tokens"] = dh[:N_META]

    names = SMALL_REPL + SMALL_SHARDED
    two_d = lambda a: a.reshape(1, -1) if a.ndim == 1 else a
    partial = [two_d(gsmall[k]) for k in names]
    partial[0] = partial[0] + sched.last[0, 0]
    summed = dict(zip(names, _allreduce_small(partial)))
    sched.run([("c0", "to_sibling")], summed[names[0]])
    for k in SMALL_SHARDED:
        ax = SHARDED_AXIS[k]
        summed[k] = lax.dynamic_slice_in_dim(summed[k], me1[0] * w[k].shape[ax], w[k].shape[ax], axis=ax)

    out = {}
    res = _adamw_small([two_d(w[k]) for k in names], [summed[k] for k in names], [two_d(mom[k]) for k in names],
                       [two_d(var[k]) for k in names])
    for k, (d, m2, v2) in zip(names, res):
        out[k] = tuple(o.reshape(w[k].shape) for o in (summed[k], d, m2, v2))
    sched.run([("b0", "finish"), ("c0", "finish")], res[0][0])
    out.update(sched.out)

    loss = lax.psum(loss, ("x", "y", "c"))
    return (loss, grad_x, *[out[k][0] for k in WEIGHTS], *[out[k][1] for k in WEIGHTS],
            *[out[k][2] for k in WEIGHTS], *[out[k][3] for k in WEIGHTS])
```

```python
import functools

import jax
import jax.numpy as jnp
from jax import lax
from jax.experimental import pallas as pl
from jax.experimental.pallas import tpu as pltpu

F32, BF16 = jnp.float32, jnp.bfloat16
MESH = pl.DeviceIdType.MESH
ANY = pl.BlockSpec(memory_space=pl.ANY)

D_MODEL = 1024
N_META = 16
D_POOL = 256
D_CONV = 256
D_RNN = 512
D_IN = D_POOL + 2 * D_CONV + 2 * D_RNN
D_FF = 4096
FF_CHUNK = 1024
POOL_GW = 64
CONV_K = 31
RG_CONV_K = 4
RG_HD = 64
RG_C = 8.0
EPS = 1e-6
ADAM_LR, ADAM_B1, ADAM_B2, ADAM_EPS, ADAM_WD, ADAM_STEP = 0.001, 0.9, 0.999, 1e-08, 0.01, 10

HALO = 32
ROW_ALIGN = 256
TM_MIX = 256
TM_MAT = 768
TM_MLP_BWD = 384
N_CHIPS = 4
VMEM_LIMIT = 56 * 1024 * 1024

BIG = ("w_in", "w_out", "w_up", "w_down")
SMALL_SHARDED = ("meta_tokens", "convb_dw_w", "convb_pw_w", "rg_conv_w")
SMALL_REPL = ("mix_norm_g", "pool_w", "pool_scale", "convb_dw_b", "convb_ln_g", "convb_ln_b", "rg_conv_b",
              "rg_w_a", "rg_b_a", "rg_w_x", "rg_b_x", "rg_lambda", "mlp_norm_g", "final_norm_g")
WEIGHTS = ("meta_tokens", "mix_norm_g", "w_in", "pool_w", "pool_scale", "convb_dw_w", "convb_dw_b", "convb_ln_g",
           "convb_ln_b", "convb_pw_w", "rg_conv_w", "rg_conv_b", "rg_w_a", "rg_b_a", "rg_w_x", "rg_b_x",
           "rg_lambda", "w_out", "mlp_norm_g", "w_up", "w_down", "final_norm_g")


def _params(*sem):
    return pltpu.CompilerParams(dimension_semantics=sem, vmem_limit_bytes=VMEM_LIMIT)


def _row_tile(t, cap):
    best = None
    for tm in range(128, cap + 1, 128):
        if t % tm == 0:
            best = tm
    assert best is not None, (t, cap)
    return best


def _dot(a, b):
    return jnp.dot(a, b, preferred_element_type=F32)


def _dot_nt(a, b):
    return lax.dot_general(a, b, (((1,), (1,)), ((), ())), preferred_element_type=F32)


def _dot_tn(a, b):
    return lax.dot_general(a, b, (((0,), (0,)), ((), ())), preferred_element_type=F32)


def _rms(x):
    r = lax.rsqrt(jnp.mean(x * x, axis=-1, keepdims=True) + EPS)
    return r, x * r


def _rms_bwd(du, n, r, g):
    dn = du * g
    return r * (dn - n * jnp.mean(dn * n, axis=-1, keepdims=True))


def _sig(x):
    return jax.nn.sigmoid(x)


def _colsum(x):
    return jnp.sum(x, axis=0, keepdims=True)


def _one_minus_sq(a, log_a):
    x = 2.0 * log_a
    series = -x * (1.0 + x * (0.5 + x * (1.0 / 6 + x * (1.0 / 24 + x * (1.0 / 120)))))
    return jnp.where(x > -0.05, series, 1.0 - a * a)


_GELU_K0 = 0.7978845608028654
_GELU_K1 = 0.044715


def _gelu_and_grad(x):
    th = jnp.tanh(_GELU_K0 * (x + _GELU_K1 * x * x * x))
    val = 0.5 * x * (1.0 + th)
    grad = 0.5 * (1.0 + th) + 0.5 * x * (1.0 - th * th) * _GELU_K0 * (1.0 + 3.0 * _GELU_K1 * x * x)
    return val, grad


def _full(a):
    nd = a.ndim
    return pl.BlockSpec(a.shape, lambda *_: (0,) * nd)


def _resident(a):
    nd = a.ndim
    return pl.BlockSpec(a.shape, lambda *_: (0,) * nd, pipeline_mode=pl.Buffered(1))


def _after(body, n_in, deps):
    def wrapped(*refs):
        return body(*refs[:n_in], *refs[n_in + len(deps):])
    return wrapped


def _lane_sel(lane, a2, a4, a8, a16):
    return jnp.where(lane < POOL_GW, a2, jnp.where(lane < 2 * POOL_GW, a4, jnp.where(lane < 3 * POOL_GW, a8, a16)))


def _window_sums_back(src, tmp_a, tmp_b, tm):
    n = HALO + tm
    rows = lambda ref, lo, back: ref[pl.ds(lo - back, n - lo), :]
    tmp_a[pl.ds(8, n - 8), :] = rows(src, 8, 0) + rows(src, 8, 1)
    tmp_b[pl.ds(16, n - 16), :] = rows(tmp_a, 16, 0) + rows(tmp_a, 16, 2)
    s2 = rows(tmp_a, HALO, 0)
    tmp_a[pl.ds(24, n - 24), :] = rows(tmp_b, 24, 0) + rows(tmp_b, 24, 4)
    s8 = rows(tmp_a, HALO, 0)
    return s2, rows(tmp_b, HALO, 0), s8, s8 + rows(tmp_a, HALO, 8)


def _window_sums_ahead(src, tmp_a, tmp_b, tm):
    rows = lambda ref, n, ahead: ref[pl.ds(ahead, n), :]
    tmp_a[pl.ds(0, tm + 24), :] = rows(src, tm + 24, 0) + rows(src, tm + 24, 1)
    tmp_b[pl.ds(0, tm + 16), :] = rows(tmp_a, tm + 16, 0) + rows(tmp_a, tm + 16, 2)
    s2 = rows(tmp_a, tm, 0)
    tmp_a[pl.ds(0, tm + 8), :] = rows(tmp_b, tm + 8, 0) + rows(tmp_b, tm + 8, 4)
    s8 = rows(tmp_a, tm, 0)
    return s2, rows(tmp_b, tm, 0), s8, s8 + rows(tmp_a, tm, 8)


def _pool_counts(tm, t0):
    lane = lax.broadcasted_iota(jnp.int32, (tm, D_POOL), 1)
    row = lax.broadcasted_iota(jnp.int32, (tm, D_POOL), 0) + t0
    cnt = jnp.minimum(row + 1, _lane_sel(lane, 2, 4, 8, 16)).astype(F32)
    return lane, cnt


def _pool_fwd(ext_q, tmp_a, tmp_b, tm, t0):
    lane, cnt = _pool_counts(tm, t0)
    q = ext_q[pl.ds(HALO, tm), :]
    pooled = _lane_sel(lane, *_window_sums_back(ext_q, tmp_a, tmp_b, tm)) / cnt - q
    return pooled, lane, cnt


def _taps(src, w_of, offs, tm, zbuf):
    acc = None
    for r in range(8):
        ks = [k for k in range(len(offs)) if offs[k] % 8 == r]
        if not ks:
            continue
        rows = tm + (8 if r else 0)
        z = w_of(ks[0]) * src[pl.ds(offs[ks[0]] - r, rows), :]
        for k in ks[1:]:
            z = z + w_of(k) * src[pl.ds(offs[k] - r, rows), :]
        if r:
            zbuf[...] = z
            z = zbuf[pl.ds(r, tm), :]
        acc = z if acc is None else acc + z
    return acc


def _tap_grads(d_pad, src, offs, tm, g_ref, zbuf):
    ch = src.shape[-1]
    for r in range(8):
        ks = [k for k in range(len(offs)) if offs[k] % 8 == r]
        if not ks:
            continue
        rows = tm + (8 if r else 0)
        if r:
            zbuf[...] = d_pad[pl.ds(8 - r, rows), :]
        for k in ks:
            d = zbuf[...] if r else d_pad[pl.ds(8, rows), :]
            prod = d * src[pl.ds(offs[k] - r, rows), :]
            g_ref[k] += jnp.sum(prod.reshape(rows // 8, 8, ch), axis=0)


_CONV_OFFS = [HALO - (CONV_K - 1) + k for k in range(CONV_K)]


def _conv_fwd(ext_u, dww_ref, dwb, tm, zbuf):
    return dwb + _taps(ext_u, lambda k: dww_ref[k:k + 1, :], _CONV_OFFS, tm, zbuf)


def _ln_silu(c, lng, lnb):
    mu = jnp.mean(c, axis=-1, keepdims=True)
    cc = c - mu
    rstd = lax.rsqrt(jnp.mean(cc * cc, axis=-1, keepdims=True) + EPS)
    z = cc * rstd
    l = z * lng + lnb
    sl = _sig(l)
    return z, rstd, l, sl, l * sl


def _rg_fwd(ext_x, cw_ref, cb, wa, ba, wx, bx, lam, tm):
    xc = cb + cw_ref[0:1, :] * ext_x[pl.ds(HALO - (RG_CONV_K - 1), tm), :]
    for k in range(1, RG_CONV_K):
        xc = xc + cw_ref[k:k + 1, :] * ext_x[pl.ds(HALO - (RG_CONV_K - 1) + k, tm), :]
    xcb = xc.astype(BF16)
    r = _sig(_dot(xcb, wa) + ba)
    ig = _sig(_dot(xcb, wx) + bx)
    sp = jnp.maximum(-lam, 0.0) + jnp.log(1.0 + jnp.exp(-jnp.abs(lam)))
    log_a = (-RG_C * r) * sp
    a = jnp.exp(log_a)
    m = jnp.sqrt(_one_minus_sq(a, log_a))
    return xc, xcb, r, ig, sp, a, m


def _scan_rows(a_ref, b_ref, out_ref, carry, tm, reverse):
    rows = lax.broadcasted_iota(jnp.int32, (8, D_RNN), 0)
    ngrp = tm // 8

    def grp(gi, hb):
        st = pl.multiple_of((ngrp - 1 - gi if reverse else gi) * 8, 8)
        a8 = a_ref[pl.ds(st, 8), :]
        b8 = b_ref[pl.ds(st, 8), :]
        out = jnp.zeros((8, D_RNN), F32)
        for j in (range(7, -1, -1) if reverse else range(8)):
            aj = jnp.broadcast_to(a8[j:j + 1, :], (8, D_RNN))
            bj = jnp.broadcast_to(b8[j:j + 1, :], (8, D_RNN))
            if reverse:
                cur = bj + hb
                hb = aj * cur
            else:
                cur = aj * hb + bj
                hb = cur
            out = jnp.where(rows == j, cur, out)
        out_ref[pl.ds(st, 8), :] = out
        return hb

    carry[...] = lax.fori_loop(0, ngrp, grp, carry[...])


_MIX_W = ("wp", "psc", "dww", "dwb", "lng", "lnb", "wpw", "cw", "cb", "wa", "ba", "wx", "bx", "lam")


def _mixer_fwd(h, g, w_in, mw):
    t = h.shape[0]
    tm = _row_tile(t, TM_MIX)

    def body(h_ref, g_ref, win_ref, wp, psc, dww, dwb, lng, lnb, wpw, cw, cb, wa, ba, wx, bx, lam,
             y_ref, p_ref, u_ref, hs_ref, conv_ref, ext_q, ext_u, ext_x, tmp_a, tmp_b, zbuf, a_s, b_s, hcar):
        i = pl.program_id(0)

        @pl.when(i == 0)
        def _():
            ext_q[0:HALO, :] = jnp.zeros((HALO, D_POOL), F32)
            ext_u[0:HALO, :] = jnp.zeros((HALO, D_CONV), F32)
            ext_x[0:HALO, :] = jnp.zeros((HALO, D_RNN), F32)
            hcar[...] = jnp.zeros((8, D_RNN), F32)

        u = (_rms(h_ref[...])[1] * g_ref[...]).astype(BF16)
        u_ref[...] = u
        p_ref[...] = _dot_nt(u, win_ref[...])

        ext_q[pl.ds(HALO, tm), :] = p_ref[:, 0:256]
        pooled, _, _ = _pool_fwd(ext_q, tmp_a, tmp_b, tm, i * tm)
        y_ref[:, 0:256] = (_dot(pooled.astype(BF16), wp[...]) * psc[...]).astype(BF16)

        ext_u[pl.ds(HALO, tm), :] = p_ref[:, 256:512] * _sig(p_ref[:, 512:768])
        conv = _conv_fwd(ext_u, dww, dwb[...], tm, zbuf)
        conv_ref[...] = conv
        act = _ln_silu(conv, lng[...], lnb[...])[4]
        y_ref[:, 256:512] = _dot(act.astype(BF16), wpw[...]).astype(BF16)

        ext_x[pl.ds(HALO, tm), :] = p_ref[:, 1280:1792]
        xc, _, _, ig, _, a, m = _rg_fwd(ext_x, cw, cb[...], wa[...], ba[...], wx[...], bx[...], lam[...], tm)
        a_s[...] = a
        b_s[...] = m * (ig * xc)
        _scan_rows(a_s, b_s, hs_ref, hcar, tm, reverse=False)
        y_ref[:, 512:1024] = (_gelu_and_grad(p_ref[:, 768:1280])[0] * hs_ref[...]).astype(BF16)

        ext_q[0:HALO, :] = ext_q[pl.ds(tm, HALO), :]
        ext_u[0:HALO, :] = ext_u[pl.ds(tm, HALO), :]
        ext_x[0:HALO, :] = ext_x[pl.ds(tm, HALO), :]

    ws = [mw[k] for k in _MIX_W]
    row = lambda w: pl.BlockSpec((tm, w), lambda i: (i, 0))
    return pl.pallas_call(
        body, name="mixer_fwd", grid=(t // tm,),
        in_specs=[row(D_MODEL), _full(g), _resident(w_in)] + [_full(w) for w in ws],
        out_specs=[row(D_MODEL), row(D_IN), row(D_MODEL), row(D_RNN), row(D_CONV)],
        out_shape=[jax.ShapeDtypeStruct((t, D_MODEL), BF16), jax.ShapeDtypeStruct((t, D_IN), F32),
                   jax.ShapeDtypeStruct((t, D_MODEL), BF16), jax.ShapeDtypeStruct((t, D_RNN), F32),
                   jax.ShapeDtypeStruct((t, D_CONV), F32)],
        scratch_shapes=[pltpu.VMEM((HALO + tm, D_POOL), F32), pltpu.VMEM((HALO + tm, D_CONV), F32),
                        pltpu.VMEM((HALO + tm, D_RNN), F32), pltpu.VMEM((HALO + tm, D_POOL), F32),
                        pltpu.VMEM((HALO + tm, D_POOL), F32), pltpu.VMEM((tm + 8, D_CONV), F32),
                        pltpu.VMEM((tm, D_RNN), F32), pltpu.VMEM((tm, D_RNN), F32), pltpu.VMEM((8, D_RNN), F32)],
        compiler_params=_params("arbitrary"),
    )(h, g, w_in, *ws)


_MIX_G = (("wp", (D_POOL, D_POOL)), ("psc", (1, D_POOL)), ("dww", (32, 8, D_CONV)), ("dwb", (1, D_CONV)),
          ("lng", (1, D_CONV)), ("lnb", (1, D_CONV)), ("wpw", (D_CONV, D_CONV)), ("cw", (8, D_RNN)),
          ("cb", (1, D_RNN)), ("wa", (D_RNN, D_RNN)), ("ba", (1, D_RNN)), ("wx", (D_RNN, D_RNN)),
          ("bx", (1, D_RNN)), ("lam", (1, D_RNN)), ("g1", (1, D_MODEL)))


def _mixer_bwd(p, dh1, hs, conv, h0, g1, w_out, w_in, mw, deps=()):
    t = p.shape[0]
    tm = _row_tile(t, TM_MIX)
    nt = t // tm
    hb = tm // HALO

    def body(p_ref, ph_ref, dh1_ref, hs_ref, hsh_ref, conv_ref, h0_ref, g1_ref, wout_ref, win_ref,
             wp, psc, dww, dwb, lng, lnb, wpw, cw, cb, wa, ba, wx, bx, lam,
             dp_ref, dh0_ref, g_wp, g_psc, g_dww, g_dwb, g_lng, g_lnb, g_wpw, g_cw, g_cb, g_wa, g_ba, g_wx, g_bx, g_lam, g_g1,
             ext_q, ext_u, ext_x, ext_h, ee, dc_s, dx_s, tmp_a, tmp_b, zbuf, d_pad, a_s, b_s, g_s, gcar, dy_ref, dp_s):
        step = pl.program_id(0)
        i = nt - 1 - step
        grads = (g_wp, g_psc, g_dww, g_dwb, g_lng, g_lnb, g_wpw, g_cw, g_cb, g_wa, g_ba, g_wx, g_bx, g_lam, g_g1)
        dy_ref[...] = dh1_ref[...].astype(BF16)
        dy_cols = lambda lo, hi: _dot_nt(dy_ref[...], wout_ref[lo:hi, :])

        @pl.when(step == 0)
        def _():
            for gr in grads:
                gr[...] = jnp.zeros(gr.shape, F32)
            ee[pl.ds(tm, HALO), :] = jnp.zeros((HALO, D_POOL), F32)
            dc_s[pl.ds(tm, HALO), :] = jnp.zeros((HALO, D_CONV), F32)
            dx_s[pl.ds(tm, HALO), :] = jnp.zeros((HALO, D_RNN), F32)
            d_pad[0:8, :] = jnp.zeros((8, D_CONV), F32)
            d_pad[pl.ds(tm + 8, 8), :] = jnp.zeros((8, D_CONV), F32)
            gcar[...] = jnp.zeros((8, D_RNN), F32)

        hm = jnp.where(i == 0, 0.0, 1.0)

        ext_q[0:HALO, :] = ph_ref[:, 0:256] * hm
        ext_q[pl.ds(HALO, tm), :] = p_ref[:, 0:256]
        pooled, lane, cnt = _pool_fwd(ext_q, tmp_a, tmp_b, tm, i * tm)
        pooled_b = pooled.astype(BF16)
        dya = dy_cols(0, 256)
        g_psc[...] += _colsum(dya * _dot(pooled_b, wp[...]))
        dmixed_b = (dya * psc[...]).astype(BF16)
        dpooled = _dot_nt(dmixed_b, wp[...])
        g_wp[...] += _dot_tn(pooled_b, dmixed_b)
        ee[0:tm, :] = dpooled / cnt
        dp_s[:, 0:256] = _lane_sel(lane, *_window_sums_ahead(ee, tmp_a, tmp_b, tm)) - dpooled
        ee[pl.ds(tm, HALO), :] = ee[0:HALO, :]

        v = p_ref[:, 256:512]
        s = _sig(p_ref[:, 512:768])
        ext_u[0:HALO, :] = ph_ref[:, 256:512] * _sig(ph_ref[:, 512:768]) * hm
        ext_u[pl.ds(HALO, tm), :] = v * s
        z, rstd, l, sl, act = _ln_silu(conv_ref[...], lng[...], lnb[...])
        dyb_b = dy_cols(256, 512).astype(BF16)
        dact = _dot_nt(dyb_b, wpw[...])
        g_wpw[...] += _dot_tn(act.astype(BF16), dyb_b)
        dl = dact * (sl * (1.0 + l * (1.0 - sl)))
        g_lng[...] += _colsum(dl * z)
        g_lnb[...] += _colsum(dl)
        dz = dl * lng[...]
        dc = rstd * (dz - jnp.mean(dz, axis=-1, keepdims=True) - z * jnp.mean(dz * z, axis=-1, keepdims=True))
        g_dwb[...] += _colsum(dc)
        dc_s[0:tm, :] = dc
        d_pad[pl.ds(8, tm), :] = dc
        _tap_grads(d_pad, ext_u, _CONV_OFFS, tm, g_dww, zbuf)
        du0 = _taps(dc_s, lambda j: dww[CONV_K - 1 - j:CONV_K - j, :], list(range(CONV_K)), tm, zbuf)
        dp_s[:, 256:512] = du0 * s
        dp_s[:, 512:768] = du0 * v * (s * (1.0 - s))
        dc_s[pl.ds(tm, HALO), :] = dc_s[0:HALO, :]

        ext_x[0:HALO, :] = ph_ref[:, 1280:1792] * hm
        ext_x[pl.ds(HALO, tm), :] = p_ref[:, 1280:1792]
        xc, xcb, r, ig, sp, a, m = _rg_fwd(ext_x, cw, cb[...], wa[...], ba[...], wx[...], bx[...], lam[...], tm)
        ext_h[0:HALO, :] = hsh_ref[...] * hm
        ext_h[pl.ds(HALO, tm), :] = hs_ref[...]
        dyc = dy_cols(512, 1024)
        gl, dgl = _gelu_and_grad(p_ref[:, 768:1280])
        dp_s[:, 768:1280] = dyc * hs_ref[...] * dgl
        a_s[...] = a
        b_s[...] = dyc * gl
        _scan_rows(a_s, b_s, g_s, gcar, tm, reverse=True)
        g = g_s[...]
        da = g * ext_h[pl.ds(HALO - 1, tm), :]
        dm = g * (ig * xc)
        dig = g * (m * xc)
        dlog_a = da * a - dm * (a * a) / m
        g_lam[...] += _colsum(dlog_a * (-RG_C * r)) * (-_sig(-lam[...]))
        dra = (dlog_a * (-RG_C * sp)) * (r * (1.0 - r))
        dia = dig * (ig * (1.0 - ig))
        g_ba[...] += _colsum(dra)
        g_bx[...] += _colsum(dia)
        dra_b = dra.astype(BF16)
        dia_b = dia.astype(BF16)
        dxc = g * (m * ig) + _dot_nt(dra_b, wa[...]) + _dot_nt(dia_b, wx[...])
        g_wa[...] += _dot_tn(xcb, dra_b)
        g_wx[...] += _dot_tn(xcb, dia_b)
        g_cb[...] += _colsum(dxc)
        dx_s[0:tm, :] = dxc
        for k in range(RG_CONV_K):
            g_cw[k:k + 1, :] += _colsum(dxc * ext_x[pl.ds(HALO - (RG_CONV_K - 1) + k, tm), :])
        dxin = cw[RG_CONV_K - 1:RG_CONV_K, :] * dxc
        for j in range(1, RG_CONV_K):
            dxin = dxin + cw[RG_CONV_K - 1 - j:RG_CONV_K - j, :] * dx_s[pl.ds(j, tm), :]
        dp_s[:, 1280:1792] = dxin
        dx_s[pl.ds(tm, HALO), :] = dx_s[0:HALO, :]

        dpb = dp_s[...].astype(BF16)
        dp_ref[...] = dpb
        du = _dot(dpb, win_ref[...])
        r, n = _rms(h0_ref[...])
        g_g1[...] += _colsum(du * n)
        dh0_ref[...] = dh1_ref[...] + _rms_bwd(du, n, r, g1_ref[...])

    ws = [mw[k] for k in _MIX_W]
    tile = lambda w: pl.BlockSpec((tm, w), lambda s: (nt - 1 - s, 0))
    halo = lambda w: pl.BlockSpec((HALO, w), lambda s: (jnp.maximum((nt - 1 - s) * hb - 1, 0), 0))
    outs = pl.pallas_call(
        _after(body, 10 + len(ws), deps), name="mixer_bwd", grid=(nt,),
        in_specs=[tile(D_IN), halo(D_IN), tile(D_MODEL), tile(D_RNN), halo(D_RNN), tile(D_CONV), tile(D_MODEL), _full(g1),
                  _resident(w_out), _resident(w_in)] + [_full(w) for w in ws] + [ANY] * len(deps),
        out_specs=[tile(D_IN), tile(D_MODEL)] + [pl.BlockSpec(shp, lambda s, nd=len(shp): (0,) * nd) for _, shp in _MIX_G],
        out_shape=[jax.ShapeDtypeStruct((t, D_IN), BF16), jax.ShapeDtypeStruct((t, D_MODEL), F32)]
        + [jax.ShapeDtypeStruct(shp, F32) for _, shp in _MIX_G],
        scratch_shapes=[pltpu.VMEM((HALO + tm, D_POOL), F32), pltpu.VMEM((HALO + tm, D_CONV), F32),
                        pltpu.VMEM((HALO + tm, D_RNN), F32), pltpu.VMEM((HALO + tm, D_RNN), F32),
                        pltpu.VMEM((tm + HALO, D_POOL), F32), pltpu.VMEM((tm + HALO, D_CONV), F32),
                        pltpu.VMEM((tm + HALO, D_RNN), F32), pltpu.VMEM((HALO + tm, D_POOL), F32),
                        pltpu.VMEM((HALO + tm, D_POOL), F32), pltpu.VMEM((tm + 8, D_CONV), F32),
                        pltpu.VMEM((tm + 16, D_CONV), F32), pltpu.VMEM((tm, D_RNN), F32),
                        pltpu.VMEM((tm, D_RNN), F32), pltpu.VMEM((tm, D_RNN), F32), pltpu.VMEM((8, D_RNN), F32),
                        pltpu.VMEM((tm, D_MODEL), BF16), pltpu.VMEM((tm, D_IN), F32)],
        compiler_params=_params("arbitrary"),
    )(p, p, dh1, hs, hs, conv, h0, g1, w_out, w_in, *ws, *deps)
    return outs[0], outs[1], {k: o for (k, _), o in zip(_MIX_G, outs[2:])}


def _mid_fwd(y, h0, w_out, g, w_up):
    t = h0.shape[0]
    tm = _row_tile(t, TM_MAT)

    def body(y_ref, h0_ref, wo_ref, g_ref, wu_ref, h1_ref, u2_ref, f_ref):
        h1 = h0_ref[...] + _dot(y_ref[...], wo_ref[...])
        h1_ref[...] = h1
        u2 = (_rms(h1)[1] * g_ref[...]).astype(BF16)
        u2_ref[...] = u2
        for c in range(D_FF // FF_CHUNK):
            f_ref[:, c * FF_CHUNK:(c + 1) * FF_CHUNK] = _dot(u2, wu_ref[c]).astype(BF16)

    row = lambda w: pl.BlockSpec((tm, w), lambda i: (i, 0))
    return pl.pallas_call(
        body, name="mid_fwd", grid=(t // tm,),
        in_specs=[row(D_MODEL), row(D_MODEL), _resident(w_out), _full(g), _resident(w_up)],
        out_specs=[row(D_MODEL), row(D_MODEL), row(D_FF)],
        out_shape=[jax.ShapeDtypeStruct((t, D_MODEL), F32), jax.ShapeDtypeStruct((t, D_MODEL), BF16),
                   jax.ShapeDtypeStruct((t, D_FF), BF16)],
        compiler_params=_params("parallel"),
    )(y, h0, w_out, g, w_up)


def _down_fwd(f, h1, w_down):
    t = h1.shape[0]
    tm = _row_tile(t, TM_MAT)

    def body(f_ref, h1_ref, wd_ref, h2_ref):
        acc = h1_ref[...]
        for c in range(D_FF // FF_CHUNK):
            cols = slice(c * FF_CHUNK, (c + 1) * FF_CHUNK)
            a = jnp.square(jnp.maximum(f_ref[:, cols].astype(F32), 0.0)).astype(BF16)
            acc = acc + _dot(a, wd_ref[cols, :])
        h2_ref[...] = acc

    return pl.pallas_call(
        body, name="down_fwd", grid=(t // tm,),
        in_specs=[pl.BlockSpec((tm, D_FF), lambda i: (i, 0)), pl.BlockSpec((tm, D_MODEL), lambda i: (i, 0)), _resident(w_down)],
        out_specs=pl.BlockSpec((tm, D_MODEL), lambda i: (i, 0)),
        out_shape=jax.ShapeDtypeStruct((t, D_MODEL), F32),
        compiler_params=_params("parallel"),
    )(f, h1, w_down)


def _loss_head(h, g, tgt, t_real):
    t = h.shape[0]
    tm = _row_tile(t, TM_MAT)

    def body(h_ref, g_ref, tgt_ref, loss_ref, dh_ref, dg_ref):
        i = pl.program_id(0)

        @pl.when(i == 0)
        def _():
            loss_ref[...] = jnp.zeros(loss_ref.shape, F32)
            dg_ref[...] = jnp.zeros(dg_ref.shape, F32)

        r, n = _rms(h_ref[...])
        row = lax.broadcasted_iota(jnp.int32, (tm, 1), 0) + i * tm
        valid = jnp.logical_and(row >= N_META, row < t_real)
        diff = jnp.where(valid, n * g_ref[...] - tgt_ref[...], 0.0)
        loss_ref[...] += 0.5 * jnp.sum(jnp.mean(diff * diff, axis=-1, keepdims=True))
        dy = diff * (1.0 / D_MODEL)
        dg_ref[...] += _colsum(dy * n)
        dh_ref[...] = _rms_bwd(dy, n, r, g_ref[...])

    return pl.pallas_call(
        body, name="loss_head", grid=(t // tm,),
        in_specs=[pl.BlockSpec((tm, D_MODEL), lambda i: (i, 0)), _full(g), pl.BlockSpec((tm, D_MODEL), lambda i: (i, 0))],
        out_specs=[pl.BlockSpec((8, 128), lambda i: (0, 0)), pl.BlockSpec((tm, D_MODEL), lambda i: (i, 0)),
                   pl.BlockSpec((1, D_MODEL), lambda i: (0, 0))],
        out_shape=[jax.ShapeDtypeStruct((8, 128), F32), jax.ShapeDtypeStruct((t, D_MODEL), F32),
                   jax.ShapeDtypeStruct((1, D_MODEL), F32)],
        compiler_params=_params("arbitrary"),
    )(h, g, tgt)


def _mlp_bwd(dh2, f, h1, g, w_up, w_down, deps=()):
    t = dh2.shape[0]
    tm = _row_tile(t, TM_MLP_BWD)

    def body(dh2_ref, f_ref, wd_ref, wu_ref, h1_ref, g_ref, df_ref, dh1_ref, dg_ref):
        @pl.when(pl.program_id(0) == 0)
        def _():
            dg_ref[...] = jnp.zeros(dg_ref.shape, F32)

        dh2 = dh2_ref[...]
        dhb = dh2.astype(BF16)
        du2 = None
        for c in range(D_FF // FF_CHUNK):
            cols = slice(c * FF_CHUNK, (c + 1) * FF_CHUNK)
            dact = _dot_nt(dhb, wd_ref[c])
            df = (dact * (2.0 * jnp.maximum(f_ref[:, cols].astype(F32), 0.0))).astype(BF16)
            df_ref[:, cols] = df
            part = _dot_nt(df, wu_ref[c])
            du2 = part if du2 is None else du2 + part
        r, n = _rms(h1_ref[...])
        dg_ref[...] += _colsum(du2 * n)
        dh1_ref[...] = dh2 + _rms_bwd(du2, n, r, g_ref[...])

    row = lambda w: pl.BlockSpec((tm, w), lambda i: (i, 0))
    return pl.pallas_call(
        _after(body, 6, deps), name="mlp_bwd", grid=(t // tm,),
        in_specs=[row(D_MODEL), row(D_FF), _resident(w_down), _resident(w_up), row(D_MODEL), _full(g)] + [ANY] * len(deps),
        out_specs=[row(D_FF), row(D_MODEL), pl.BlockSpec((1, D_MODEL), lambda i: (0, 0))],
        out_shape=[jax.ShapeDtypeStruct((t, D_FF), BF16), jax.ShapeDtypeStruct((t, D_MODEL), F32),
                   jax.ShapeDtypeStruct((1, D_MODEL), F32)],
        compiler_params=_params("arbitrary"),
    )(dh2, f, w_down, w_up, h1, g, *deps)


def _tn_matmul(a, b, kc, nc, relu2, name, deps=()):
    t, k = a.shape
    n = b.shape[1]
    tt = _row_tile(t, TM_MAT)
    gk, gn = k // kc, n // nc

    def body(a_ref, b_ref, o_ref):
        @pl.when(pl.program_id(2) == 0)
        def _():
            o_ref[...] = jnp.zeros(o_ref.shape, F32)

        av = a_ref[...]
        if relu2:
            av = jnp.square(jnp.maximum(av.astype(F32), 0.0))
        o_ref[...] += _dot_tn(av.astype(BF16), b_ref[...].astype(BF16))

    return pl.pallas_call(
        _after(body, 2, deps), name=name, grid=(gk, gn, t // tt),
        in_specs=[pl.BlockSpec((tt, kc), lambda ik, jn, it: (it, ik)), pl.BlockSpec((tt, nc), lambda ik, jn, it: (it, jn))]
        + [ANY] * len(deps),
        out_specs=pl.BlockSpec((None, kc, nc), lambda ik, jn, it: (ik * gn + jn, 0, 0)),
        out_shape=jax.ShapeDtypeStruct((gk * gn, kc, nc), F32),
        compiler_params=_params("parallel", "parallel", "arbitrary"),
    )(a, b, *deps)


def _block_diag(blocks):
    nb, hd, _ = blocks.shape
    eye = jnp.eye(nb, dtype=blocks.dtype)
    return (blocks[:, :, None, :] * eye[:, None, :, None]).reshape(nb * hd, nb * hd)


def _diag_blocks(m, nb):
    hd = m.shape[0] // nb
    eye = jnp.eye(nb, dtype=m.dtype)
    return jnp.sum(m.reshape(nb, hd, nb, hd) * eye[:, None, :, None], axis=2)


def _mixer_weights(w, l):
    row = lambda a: a.reshape(1, -1)
    return dict(
        wp=_block_diag(w["pool_w"][l]).astype(BF16), psc=row(w["pool_scale"][l]),
        dww=jnp.pad(w["convb_dw_w"][l], ((0, 32 - CONV_K), (0, 0))), dwb=row(w["convb_dw_b"][l]),
        lng=row(w["convb_ln_g"][l]), lnb=row(w["convb_ln_b"][l]), wpw=w["convb_pw_w"][l].astype(BF16),
        cw=jnp.pad(w["rg_conv_w"][l], ((0, 8 - RG_CONV_K), (0, 0))), cb=row(w["rg_conv_b"][l]),
        wa=_block_diag(w["rg_w_a"][l]).astype(BF16), ba=row(w["rg_b_a"][l]),
        wx=_block_diag(w["rg_w_x"][l]).astype(BF16), bx=row(w["rg_b_x"][l]), lam=row(w["rg_lambda"][l]))


def _local_step(h, tgt, t_real, w, fetch, hooks):
    depth = 2
    saved = []
    big = []
    for l in range(depth):
        mw = _mixer_weights(w, l)
        g1 = w["mix_norm_g"][l].reshape(1, -1)
        g2 = w["mlp_norm_g"][l].reshape(1, -1)
        wl = dict(w_in=fetch(l, "w_in", h))
        y, p, u, hs, conv = _mixer_fwd(h, g1, wl["w_in"], mw)
        wl["w_out"], wl["w_up"] = fetch(l, "w_out", y), fetch(l, "w_up", y)
        h1, u2, f = _mid_fwd(y, h, wl["w_out"], g2, wl["w_up"])
        wl["w_down"] = fetch(l, "w_down", f)
        h2 = _down_fwd(f, h1, wl["w_down"].reshape(D_FF, D_MODEL))
        saved.append(dict(mw=mw, g1=g1, g2=g2, h0=h, p=p, u=u, y=y, hs=hs, conv=conv, h1=h1, u2=u2, f=f))
        big.append(wl)
        h = h2
    gf = w["final_norm_g"].reshape(1, -1)
    loss, dh, dgf = _loss_head(h, gf, tgt, t_real)

    gs = {k: [None] * depth for k in ("mix_norm_g", "mlp_norm_g", "pool_w", "pool_scale", "convb_dw_w", "convb_dw_b",
                                      "convb_ln_g", "convb_ln_b", "convb_pw_w", "rg_conv_w", "rg_conv_b", "rg_w_a",
                                      "rg_b_a", "rg_w_x", "rg_b_x", "rg_lambda")}
    deps = ()
    for l in reversed(range(depth)):
        s, wl = saved[l], big[l]
        df, dh1, dg2 = _mlp_bwd(dh, s["f"], s["h1"], s["g2"], wl["w_up"], wl["w_down"], deps)
        deps = hooks.point(l, "mlp_bwd", dh1)
        g_down = _tn_matmul(s["f"], dh, FF_CHUNK, D_MODEL, True, "dw_down", deps)
        hooks.grad(l, "w_down", g_down)
        deps = hooks.point(l, "dw_down", g_down)
        g_up = _tn_matmul(s["u2"], df, D_MODEL, FF_CHUNK, False, "dw_up", deps)
        hooks.grad(l, "w_up", g_up)
        deps = hooks.point(l, "dw_up", g_up)
        g_out = _tn_matmul(s["y"], dh1, D_MODEL, D_MODEL, False, "dw_out", deps)
        hooks.grad(l, "w_out", g_out.reshape(N_CHIPS, D_MODEL // N_CHIPS, D_MODEL))
        deps = hooks.point(l, "dw_out", g_out)
        dp, dh, mg = _mixer_bwd(s["p"], dh1, s["hs"], s["conv"], s["h0"], s["g1"], wl["w_out"], wl["w_in"], s["mw"], deps)
        deps = hooks.point(l, "mixer_bwd", dh)
        g_in = _tn_matmul(dp, s["u"], D_IN, D_MODEL, False, "dw_in", deps).reshape(N_CHIPS, D_IN // N_CHIPS, D_MODEL)
        hooks.grad(l, "w_in", g_in)
        deps = hooks.point(l, "dw_in", g_in)
        gs["mix_norm_g"][l] = mg["g1"][0]
        gs["mlp_norm_g"][l] = dg2[0]
        gs["pool_w"][l] = _diag_blocks(mg["wp"], D_POOL // POOL_GW)
        gs["pool_scale"][l] = mg["psc"][0]
        gs["convb_dw_w"][l] = jnp.sum(mg["dww"][:CONV_K], axis=1)
        gs["convb_dw_b"][l] = mg["dwb"][0]
        gs["convb_ln_g"][l] = mg["lng"][0]
        gs["convb_ln_b"][l] = mg["lnb"][0]
        gs["convb_pw_w"][l] = mg["wpw"]
        gs["rg_conv_w"][l] = mg["cw"][:RG_CONV_K]
        gs["rg_conv_b"][l] = mg["cb"][0]
        gs["rg_w_a"][l] = _diag_blocks(mg["wa"], D_RNN // RG_HD)
        gs["rg_b_a"][l] = mg["ba"][0]
        gs["rg_w_x"][l] = _diag_blocks(mg["wx"], D_RNN // RG_HD)
        gs["rg_b_x"][l] = mg["bx"][0]
        gs["rg_lambda"][l] = mg["lam"][0]
    gsmall = {k: jnp.stack(v) for k, v in gs.items()}
    gsmall["final_norm_g"] = dgf[0]
    return loss[0, 0], dh, gsmall


def _place():
    return lax.axis_index("x"), lax.axis_index("y"), lax.axis_index("c")


def _other_chips(x, y):
    return [(1 - x, y), (x, 1 - y), (1 - x, 1 - y)]


HBM_SPEC = pl.BlockSpec(memory_space=pltpu.HBM)
SEM_SPEC = pl.BlockSpec(memory_space=pltpu.SEMAPHORE)
DATAFLOW = pltpu.SideEffectType.DATAFLOW_SIDE_EFFECTING


def _gather_copies(src_refs, land_refs, send_sem, recv_sem, first):
    x, y, c = _place()
    me = 2 * x + y
    out = []
    for n in range(len(src_refs)):
        for j, (px, py) in enumerate(_other_chips(x, y)):
            out.append(pltpu.make_async_remote_copy(src_refs[n], land_refs[n].at[me], send_sem.at[first + 3 * n + j],
                                                    recv_sem.at[first + 3 * n + j], device_id=(px, py, c), device_id_type=MESH))
    return out


def _gather_start(groups, me):
    srcs = [pltpu.with_memory_space_constraint(s, pltpu.HBM) for g in groups for s in g]
    lands = [pltpu.with_memory_space_constraint(
        lax.dynamic_update_slice(jnp.zeros((N_CHIPS,) + s.shape, s.dtype), s[None], (me,) + (0,) * s.ndim), pltpu.HBM)
        for g in groups for s in g]
    n, ng = len(srcs), len(groups)
    first = [sum(len(g) for g in groups[:i]) for i in range(ng)]

    def body(*refs):
        src_refs, land_refs = refs[:n], refs[n:2 * n]
        sems = refs[2 * n:2 * n + 2 * ng]
        token = refs[-1]
        for gi, g in enumerate(groups):
            lo, hi = first[gi], first[gi] + len(g)
            for cp in _gather_copies(src_refs[lo:hi], land_refs[lo:hi], sems[2 * gi], sems[2 * gi + 1], 0):
                cp.start()
        token[...] = jnp.zeros(token.shape, token.dtype)

    sem_shapes = [pltpu.SemaphoreType.DMA((3 * len(g),)) for g in groups for _ in range(2)]
    outs = pl.pallas_call(
        body, name="gather_start",
        out_shape=sem_shapes + [pltpu.HBM(a.shape, a.dtype) for a in srcs + lands] + [jax.ShapeDtypeStruct((8, 128), F32)],
        in_specs=[HBM_SPEC] * (2 * n),
        out_specs=[SEM_SPEC] * (2 * ng) + [HBM_SPEC] * (2 * n) + [pl.BlockSpec(memory_space=pltpu.VMEM)],
        input_output_aliases={i: 2 * ng + i for i in range(2 * n)},
        compiler_params=pltpu.CompilerParams(has_side_effects=DATAFLOW),
    )(*srcs, *lands)
    sems, thru, token = outs[:2 * ng], outs[2 * ng:2 * ng + 2 * n], outs[-1]
    state = []
    for gi, g in enumerate(groups):
        lo, hi = first[gi], first[gi] + len(g)
        state.append((sems[2 * gi], sems[2 * gi + 1], thru[lo:hi], thru[n + lo:n + hi]))
    return state, token


def _gather_wait(state, after, name):
    send_sem, recv_sem, srcs, lands = state
    n = len(srcs)

    def body(*refs):
        src_refs, land_refs = refs[:n], refs[n:2 * n]
        send, recv = refs[2 * n], refs[2 * n + 1]
        for cp in _gather_copies(src_refs, land_refs, send, recv, 0):
            cp.wait_send()
            cp.wait_recv()

    outs = pl.pallas_call(
        body, name=name,
        out_shape=[pltpu.HBM(a.shape, a.dtype) for a in list(srcs) + list(lands)],
        in_specs=[HBM_SPEC] * (2 * n) + [SEM_SPEC, SEM_SPEC, ANY],
        out_specs=[HBM_SPEC] * (2 * n),
        input_output_aliases={i: i for i in range(2 * n)},
        compiler_params=pltpu.CompilerParams(has_side_effects=DATAFLOW),
    )(*srcs, *lands, send_sem, recv_sem, after)
    return outs[n:]


def _add_halves(g, recv, c1):
    nk, r, cd = g.shape
    r2 = r // 2

    def body(c_ref, g_ref, r_ref, pa_ref, pab_ref):
        s = g_ref[...] + r_ref[...]
        pa_ref[...] = s
        pab_ref[...] = s.astype(BF16)

    blk = pl.BlockSpec((None, r2, cd), lambda k, c_ref: (k, 0, 0))
    return pl.pallas_call(
        body, name="rs_add_halves",
        grid_spec=pltpu.PrefetchScalarGridSpec(
            num_scalar_prefetch=1, grid=(nk,),
            in_specs=[pl.BlockSpec((None, r2, cd), lambda k, c_ref: (k, c_ref[0], 0)), blk], out_specs=[blk, blk]),
        out_shape=[jax.ShapeDtypeStruct((nk, r2, cd), F32), jax.ShapeDtypeStruct((nk, r2, cd), BF16)],
        compiler_params=_params("parallel"),
    )(c1, g, recv)


def _sum_partials(pa, recv, me1):
    nk, r2, cd = pa.shape

    def body(me_ref, pa_ref, r_ref, s_ref):
        s_ref[...] = ((pa_ref[...] + r_ref[0].astype(F32)) + r_ref[1].astype(F32)) + r_ref[2].astype(F32)

    return pl.pallas_call(
        body, name="rs_sum_partials",
        grid_spec=pltpu.PrefetchScalarGridSpec(
            num_scalar_prefetch=1, grid=(1,),
            in_specs=[pl.BlockSpec((None, r2, cd), lambda i, me_ref: (me_ref[0], 0, 0)),
                      pl.BlockSpec((3, r2, cd), lambda i, me_ref: (0, 0, 0))],
            out_specs=pl.BlockSpec((r2, cd), lambda i, me_ref: (0, 0))),
        out_shape=jax.ShapeDtypeStruct((r2, cd), F32),
        compiler_params=_params("arbitrary"),
    )(me1, pa, recv)


def _split_start(name, srcs, lands, ncopies, make_copies):
    srcs = [pltpu.with_memory_space_constraint(s, pltpu.HBM) for s in srcs]
    lands = [pltpu.with_memory_space_constraint(a, pltpu.HBM) for a in lands]
    n, m = len(srcs), len(lands)

    def body(*refs):
        src_refs, land_refs = refs[:n], refs[n:n + m]
        send, recv, token = refs[n + m], refs[n + m + 1], refs[-1]
        for cp in make_copies(src_refs, land_refs, send, recv):
            cp.start()
        token[...] = jnp.zeros(token.shape, token.dtype)

    outs = pl.pallas_call(
        body, name=name,
        out_shape=[pltpu.SemaphoreType.DMA((ncopies,)), pltpu.SemaphoreType.DMA((ncopies,))]
        + [pltpu.HBM(a.shape, a.dtype) for a in srcs + lands] + [jax.ShapeDtypeStruct((8, 128), F32)],
        in_specs=[HBM_SPEC] * (n + m),
        out_specs=[SEM_SPEC, SEM_SPEC] + [HBM_SPEC] * (n + m) + [pl.BlockSpec(memory_space=pltpu.VMEM)],
        input_output_aliases={i: 2 + i for i in range(n + m)},
        compiler_params=pltpu.CompilerParams(has_side_effects=DATAFLOW),
    )(*srcs, *lands)
    return (outs[0], outs[1], outs[2:2 + n], outs[2 + n:2 + n + m], make_copies), outs[-1]


def _split_wait(name, state, after):
    send_sem, recv_sem, srcs, lands, make_copies = state
    n, m = len(srcs), len(lands)

    def body(*refs):
        src_refs, land_refs = refs[:n], refs[n:n + m]
        for cp in make_copies(src_refs, land_refs, refs[n + m], refs[n + m + 1]):
            cp.wait_send()
            cp.wait_recv()

    outs = pl.pallas_call(
        body, name=name,
        out_shape=[pltpu.HBM(a.shape, a.dtype) for a in list(srcs) + list(lands)],
        in_specs=[HBM_SPEC] * (n + m) + [SEM_SPEC, SEM_SPEC, ANY],
        out_specs=[HBM_SPEC] * (n + m),
        input_output_aliases={i: i for i in range(n + m)},
        compiler_params=pltpu.CompilerParams(has_side_effects=DATAFLOW),
    )(*srcs, *lands, send_sem, recv_sem, after)
    return outs[:n], outs[n:]


def _copies_to_sibling(src_of):
    def make(src_refs, land_refs, send, recv):
        x, y, c = _place()
        return [pltpu.make_async_remote_copy(src_of(src_refs[i], c), land_refs[i], send.at[i], recv.at[i],
                                             device_id=(x, y, 1 - c), device_id_type=MESH) for i in range(len(src_refs))]
    return make


def _copies_to_chips(src_refs, land_refs, send, recv):
    x, y, c = _place()
    return [pltpu.make_async_remote_copy(src_refs[i].at[2 * px + py], land_refs[i].at[j], send.at[3 * i + j], recv.at[3 * i + j],
                                         device_id=(px, py, c), device_id_type=MESH)
            for i in range(len(src_refs)) for j, (px, py) in enumerate(_other_chips(x, y))]


def _other_half_rows(ref, c):
    r2 = ref.shape[1] // 2
    return ref.at[:, pl.ds(pl.multiple_of((1 - c) * r2, 8), r2)]


class _ReduceScatter:
    def __init__(self, tag, grads, c1, me1):
        self.tag, self.grads, self.c1, self.me1 = tag, grads, c1, me1

    def start(self):
        lands = [lax.empty((g.shape[0], g.shape[1] // 2, g.shape[2]), F32) for g in self.grads]
        self.state, token = _split_start("rs_%s_a_start" % self.tag, self.grads, lands, len(self.grads),
                                         _copies_to_sibling(_other_half_rows))
        return token

    def to_chips(self, after):
        grads, recv = _split_wait("rs_%s_a_wait" % self.tag, self.state, after)
        added = [_add_halves(g, r, self.c1) for g, r in zip(grads, recv)]
        self.own = [pa for pa, _ in added]
        pabs = [pab for _, pab in added]
        lands = [lax.empty((3,) + p.shape[1:], BF16) for p in pabs]
        self.state, token = _split_start("rs_%s_b_start" % self.tag, pabs, lands, 3 * len(pabs), _copies_to_chips)
        return token

    def to_sibling(self, after):
        _, recv = _split_wait("rs_%s_b_wait" % self.tag, self.state, after)
        sums = [_sum_partials(pa, rb, self.me1) for pa, rb in zip(self.own, recv)]
        lands = [lax.empty(s.shape, F32) for s in sums]
        self.state, token = _split_start("rs_%s_c_start" % self.tag, sums, lands, len(sums),
                                         _copies_to_sibling(lambda ref, c: ref))
        return token

    def finish(self, after):
        return list(zip(*_split_wait("rs_%s_c_wait" % self.tag, self.state, after)))


def _allreduce_small(vs):
    n = len(vs)

    def body(*refs):
        v_refs, out_refs, rbufs = refs[:n], refs[n:2 * n], refs[2 * n:3 * n]
        send_sems, recv_sems = refs[3 * n:]
        x, y, c = _place()
        for i in range(n):
            out_refs[i][...] = v_refs[i][...]
        for s, peer in enumerate([(x, y, 1 - c), (1 - x, y, c), (x, 1 - y, c)]):
            copies = [pltpu.make_async_remote_copy(out_refs[i], rbufs[i].at[s], send_sems.at[s * n + i], recv_sems.at[s * n + i],
                                                   device_id=peer, device_id_type=MESH) for i in range(n)]
            for cp in copies:
                cp.start()
            for cp in copies:
                cp.wait()
            for i in range(n):
                out_refs[i][...] = out_refs[i][...] + rbufs[i][s]

    vm = pl.BlockSpec(memory_space=pltpu.VMEM)
    return pl.pallas_call(
        body, name="allreduce_small", in_specs=[vm] * n, out_specs=[vm] * n,
        out_shape=[jax.ShapeDtypeStruct(v.shape, v.dtype) for v in vs],
        scratch_shapes=[pltpu.VMEM((3,) + v.shape, v.dtype) for v in vs]
        + [pltpu.SemaphoreType.DMA((3 * n,)), pltpu.SemaphoreType.DMA((3 * n,))],
        compiler_params=pltpu.CompilerParams(vmem_limit_bytes=VMEM_LIMIT),
    )(*vs)


def _adamw_math(w, g, m, v):
    m = ADAM_B1 * m + (1.0 - ADAM_B1) * g
    v = ADAM_B2 * v + (1.0 - ADAM_B2) * jnp.square(g)
    m_hat = m / (1.0 - ADAM_B1 ** ADAM_STEP)
    v_hat = v / (1.0 - ADAM_B2 ** ADAM_STEP)
    return -ADAM_LR * (m_hat / (jnp.sqrt(v_hat) + ADAM_EPS) + ADAM_WD * w), m, v


def _adamw_big_layer(layer, w, m, v, own, sib, c1, prev):
    _, r, cd = w.shape
    r2 = r // 2

    def body(c_ref, w_ref, m_ref, v_ref, own_ref, sib_ref, *rest):
        g_ref, d_ref, mo_ref, vo_ref, token = rest[-5:]
        g = jnp.where(pl.program_id(0) == c_ref[0], own_ref[...], sib_ref[...])
        g_ref[...] = g
        d_ref[...], mo_ref[...], vo_ref[...] = _adamw_math(w_ref[...], g, m_ref[...], v_ref[...])
        token[...] = jnp.zeros(token.shape, F32)

    blk = pl.BlockSpec((None, r2, cd), lambda hh, c_ref: (layer, hh, 0))
    half = pl.BlockSpec((r2, cd), lambda hh, c_ref: (0, 0))
    prev = () if prev is None else tuple(prev)
    outs = pl.pallas_call(
        body, name="adamw_big",
        grid_spec=pltpu.PrefetchScalarGridSpec(
            num_scalar_prefetch=1, grid=(2,), in_specs=[blk, blk, blk, half, half] + [ANY] * len(prev),
            out_specs=[blk] * 4 + [pl.BlockSpec((8, 128), lambda hh, c_ref: (0, 0))]),
        out_shape=[jax.ShapeDtypeStruct(w.shape, F32)] * 4 + [jax.ShapeDtypeStruct((8, 128), F32)],
        input_output_aliases={6 + i: i for i in range(len(prev))},
        compiler_params=_params("arbitrary"),
    )(c1, w, m, v, own, sib, *prev)
    return outs[:4], outs[4]


def _adamw_small(ws, gs, ms, vs):
    n = len(ws)

    def body(*refs):
        w_refs, g_refs, m_refs, v_refs = refs[:n], refs[n:2 * n], refs[2 * n:3 * n], refs[3 * n:4 * n]
        outs = refs[4 * n:]
        for i in range(n):
            outs[3 * i][...], outs[3 * i + 1][...], outs[3 * i + 2][...] = _adamw_math(
                w_refs[i][...], g_refs[i][...], m_refs[i][...], v_refs[i][...])

    vm = pl.BlockSpec(memory_space=pltpu.VMEM)
    outs = pl.pallas_call(
        body, name="adamw_small", in_specs=[vm] * (4 * n), out_specs=[vm] * (3 * n),
        out_shape=[jax.ShapeDtypeStruct(w.shape, F32) for w in ws for _ in range(3)],
        compiler_params=pltpu.CompilerParams(vmem_limit_bytes=VMEM_LIMIT),
    )(*ws, *gs, *ms, *vs)
    return [outs[3 * i:3 * i + 3] for i in range(n)]


LANES = 128
SUBLANES = 8
SHARDED_AXIS = {"meta_tokens": 1, "convb_dw_w": 2, "convb_pw_w": 1, "rg_conv_w": 2}


def _rows_of(size):
    return -(-size // (LANES * SUBLANES)) * SUBLANES


def _as_rows(a, rows=None):
    flat = a.reshape(-1)
    rows = _rows_of(flat.size) if rows is None else rows
    return jnp.pad(flat, (0, rows * LANES - flat.size)).reshape(rows, LANES)


class _GradientSchedule:
    GROUPS = {"l1": [(1, "w_down"), (1, "w_up"), (1, "w_out"), (1, "w_in")], "a0": [(0, "w_down"), (0, "w_up")],
              "b0": [(0, "w_out")], "c0": [(0, "w_in")]}
    PLAN = {
        (1, "dw_in"): [("l1", "start")],
        (0, "mlp_bwd"): [("l1", "to_chips")],
        (0, "dw_up"): [("l1", "to_sibling"), ("a0", "start")],
        (0, "dw_out"): [("l1", "finish"), ("a0", "to_chips"), ("b0", "start")],
        (0, "mixer_bwd"): [("a0", "to_sibling"), ("b0", "to_chips")],
        (0, "dw_in"): [("c0", "start"), ("a0", "finish"), ("b0", "to_sibling"), ("c0", "to_chips")],
    }

    def __init__(self, w, mom, var, c1, me1):
        self.w, self.mom, self.var, self.c1, self.me1 = w, mom, var, c1, me1
        self.grads, self.chains, self.out = {}, {}, {}

    def grad(self, layer, name, g):
        self.grads[layer, name] = g

    def point(self, layer, kernel_name, after):
        return self.run(self.PLAN.get((layer, kernel_name), ()), after)

    def run(self, actions, after):
        deps = []
        for tag, stage in actions:
            if stage == "start":
                self.chains[tag] = _ReduceScatter(tag, [self.grads[lk] for lk in self.GROUPS[tag]], self.c1, self.me1)
                deps.append(self.chains[tag].start())
            elif stage == "finish":
                for (layer, k), (own, sib) in zip(self.GROUPS[tag], self.chains[tag].finish(after)):
                    self.out[k], token = _adamw_big_layer(layer, self.w[k], self.mom[k], self.var[k], own, sib, self.c1,
                                                          self.out.get(k))
                    deps.append(token)
            else:
                deps.append(getattr(self.chains[tag], stage)(after))
            after = deps[-1]
        self.last = after
        return tuple(deps)


def _from_shard_major(name, sm):
    if name == "meta_tokens":
        return sm.transpose(1, 0, 2).reshape(N_META, -1)
    if name == "convb_pw_w":
        return sm.transpose(1, 0, 2, 3).reshape(2, -1, D_CONV)
    return sm.transpose(1, 2, 0, 3).reshape(sm.shape[1], sm.shape[2], -1)


def kernel(x, meta_tokens, mix_norm_g, w_in, pool_w, pool_scale, convb_dw_w, convb_dw_b, convb_ln_g, convb_ln_b, convb_pw_w, rg_conv_w, rg_conv_b, rg_w_a, rg_b_a, rg_w_x, rg_b_x, rg_lambda, w_out, mlp_norm_g, w_up, w_down, final_norm_g, loss_target, m_meta_tokens, m_mix_norm_g, m_w_in, m_pool_w, m_pool_scale, m_convb_dw_w, m_convb_dw_b, m_convb_ln_g, m_convb_ln_b, m_convb_pw_w, m_rg_conv_w, m_rg_conv_b, m_rg_w_a, m_rg_b_a, m_rg_w_x, m_rg_b_x, m_rg_lambda, m_w_out, m_mlp_norm_g, m_w_up, m_w_down, m_final_norm_g, v_meta_tokens, v_mix_norm_g, v_w_in, v_pool_w, v_pool_scale, v_convb_dw_w, v_convb_dw_b, v_convb_ln_g, v_convb_ln_b, v_convb_pw_w, v_rg_conv_w, v_rg_conv_b, v_rg_w_a, v_rg_b_a, v_rg_w_x, v_rg_b_x, v_rg_lambda, v_w_out, v_mlp_norm_g, v_w_up, v_w_down, v_final_norm_g):
    given = dict(locals())
    w = {k: given[k] for k in WEIGHTS}
    mom = {k: given["m_" + k] for k in WEIGHTS}
    var = {k: given["v_" + k] for k in WEIGHTS}
    xi, yi, ci = _place()
    me1 = (2 * xi + yi).astype(jnp.int32).reshape(1)
    c1 = ci.astype(jnp.int32).reshape(1)

    small_rows = [_rows_of(w[k].size) for k in SMALL_SHARDED]
    small_pack = jnp.concatenate([_as_rows(w[k]) for k in SMALL_SHARDED])
    transposed = lambda d: {**d, "w_in": d["w_in"].transpose(0, 2, 1)}
    wt, momt, vart = transposed(w), transposed(mom), transposed(var)
    shard = lambda l, k: wt[k][l].astype(BF16)
    order = [[(0, "w_in"), "small"], [(0, "w_out"), (0, "w_up")], [(0, "w_down")], [(1, "w_in")], [(1, "w_out"), (1, "w_up")],
             [(1, "w_down")]]
    state, token = _gather_start([[small_pack if lk == "small" else shard(*lk) for lk in g] for g in order], me1[0])
    landed = {}

    def fetch(l, k, after):
        gi = [i for i, g in enumerate(order) if (l, k) in g][0]
        if gi not in landed:
            landed[gi] = _gather_wait(state[gi], after, "gather_wait_%d" % gi)
        raw = landed[gi][order[gi].index((l, k))]
        if k == "w_in":
            return raw.reshape(D_IN, D_MODEL)
        return raw.reshape(D_MODEL, D_MODEL) if k == "w_out" else raw

    seq = x.shape[1]
    t_real = N_META + seq
    t_pad = -(-t_real // ROW_ALIGN) * ROW_ALIGN
    tail = jnp.zeros((t_pad - t_real, D_MODEL), F32)
    front = jnp.zeros((N_META, D_MODEL), F32)
    h = jnp.concatenate([front + token[0, 0], x[0], tail])
    tgt = jnp.concatenate([front, loss_target[0], tail])
    landed[0] = _gather_wait(state[0], h, "gather_wait_0")
    wfull = dict(w)
    off = 0
    for k, rows in zip(SMALL_SHARDED, small_rows):
        sm = landed[0][1][:, off:off + rows].reshape(N_CHIPS, -1)[:, :w[k].size].reshape((N_CHIPS,) + w[k].shape)
        wfull[k] = _from_shard_major(k, sm)
        off += rows
    h = lax.dynamic_update_slice(h, wfull["meta_tokens"], (0, 0))
    sched = _GradientSchedule(wt, momt, vart, c1, me1)
    loss, dh, gsmall = _local_step(h, tgt, t_real, wfull, fetch, sched)
    grad_x = dh[N_META:t_real][None]
    gsmall["meta_tokens"] = dh[:N_META]

    names = SMALL_REPL + SMALL_SHARDED
    two_d = lambda a: a.reshape(1, -1) if a.ndim == 1 else a
    partial = [two_d(gsmall[k]) for k in names]
    partial[0] = partial[0] + sched.last[0, 0]
    summed = dict(zip(names, _allreduce_small(partial)))
    sched.run([("c0", "to_sibling")], summed[names[0]])
    for k in SMALL_SHARDED:
        ax = SHARDED_AXIS[k]
        summed[k] = lax.dynamic_slice_in_dim(summed[k], me1[0] * w[k].shape[ax], w[k].shape[ax], axis=ax)

    out = {}
    res = _adamw_small([two_d(w[k]) for k in names], [summed[k] for k in names], [two_d(mom[k]) for k in names],
                       [two_d(var[k]) for k in names])
    for k, (d, m2, v2) in zip(names, res):
        out[k] = tuple(o.reshape(w[k].shape) for o in (summed[k], d, m2, v2))
    sched.run([("b0", "finish"), ("c0", "finish")], res[0][0])
    out.update(sched.out)
    out["w_in"] = tuple(o.transpose(0, 2, 1) for o in out["w_in"])

    loss = lax.psum(loss, ("x", "y", "c"))
    return (loss, grad_x, *[out[k][0] for k in WEIGHTS], *[out[k][1] for k in WEIGHTS],
            *[out[k][2] for k in WEIGHTS], *[out[k][3] for k in WEIGHTS])
```

```python
import functools

import jax
import jax.numpy as jnp
from jax import lax
from jax.experimental import pallas as pl
from jax.experimental.pallas import tpu as pltpu

F32, BF16 = jnp.float32, jnp.bfloat16
MESH = pl.DeviceIdType.MESH
ANY = pl.BlockSpec(memory_space=pl.ANY)

D_MODEL = 1024
N_META = 16
D_POOL = 256
D_CONV = 256
D_RNN = 512
D_IN = D_POOL + 2 * D_CONV + 2 * D_RNN
D_FF = 4096
FF_CHUNK = 1024
POOL_GW = 64
CONV_K = 31
RG_CONV_K = 4
RG_HD = 64
RG_C = 8.0
EPS = 1e-6
ADAM_LR, ADAM_B1, ADAM_B2, ADAM_EPS, ADAM_WD, ADAM_STEP = 0.001, 0.9, 0.999, 1e-08, 0.01, 10

HALO = 32
ROW_ALIGN = 256
TM_MIX = 384
TM_MAT = 768
TM_MLP_BWD = 384
N_CHIPS = 4
VMEM_LIMIT = 56 * 1024 * 1024

BIG = ("w_in", "w_out", "w_up", "w_down")
SMALL_SHARDED = ("meta_tokens", "convb_dw_w", "convb_pw_w", "rg_conv_w")
SMALL_REPL = ("mix_norm_g", "pool_w", "pool_scale", "convb_dw_b", "convb_ln_g", "convb_ln_b", "rg_conv_b",
              "rg_w_a", "rg_b_a", "rg_w_x", "rg_b_x", "rg_lambda", "mlp_norm_g", "final_norm_g")
WEIGHTS = ("meta_tokens", "mix_norm_g", "w_in", "pool_w", "pool_scale", "convb_dw_w", "convb_dw_b", "convb_ln_g",
           "convb_ln_b", "convb_pw_w", "rg_conv_w", "rg_conv_b", "rg_w_a", "rg_b_a", "rg_w_x", "rg_b_x",
           "rg_lambda", "w_out", "mlp_norm_g", "w_up", "w_down", "final_norm_g")


def _params(*sem):
    return pltpu.CompilerParams(dimension_semantics=sem, vmem_limit_bytes=VMEM_LIMIT)


def _row_tile(t, cap):
    best = None
    for tm in range(128, cap + 1, 128):
        if t % tm == 0:
            best = tm
    assert best is not None, (t, cap)
    return best


def _dot(a, b):
    return jnp.dot(a, b, preferred_element_type=F32)


def _dot_nt(a, b):
    return lax.dot_general(a, b, (((1,), (1,)), ((), ())), preferred_element_type=F32)


def _dot_tn(a, b):
    return lax.dot_general(a, b, (((0,), (0,)), ((), ())), preferred_element_type=F32)


def _rms(x):
    r = lax.rsqrt(jnp.mean(x * x, axis=-1, keepdims=True) + EPS)
    return r, x * r


def _rms_bwd(du, n, r, g):
    dn = du * g
    return r * (dn - n * jnp.mean(dn * n, axis=-1, keepdims=True))


def _sig(x):
    return jax.nn.sigmoid(x)


def _colsum(x):
    return jnp.sum(x, axis=0, keepdims=True)


def _one_minus_sq(a, log_a):
    x = 2.0 * log_a
    series = -x * (1.0 + x * (0.5 + x * (1.0 / 6)))
    return jnp.where(x > -0.01, series, 1.0 - a * a)


_GELU_K0 = 0.7978845608028654
_GELU_K1 = 0.044715


def _gelu_and_grad(x):
    th = jnp.tanh(_GELU_K0 * (x + _GELU_K1 * x * x * x))
    val = 0.5 * x * (1.0 + th)
    grad = 0.5 * (1.0 + th) + 0.5 * x * (1.0 - th * th) * _GELU_K0 * (1.0 + 3.0 * _GELU_K1 * x * x)
    return val, grad


def _full(a):
    nd = a.ndim
    return pl.BlockSpec(a.shape, lambda *_: (0,) * nd)


def _resident(a):
    nd = a.ndim
    return pl.BlockSpec(a.shape, lambda *_: (0,) * nd, pipeline_mode=pl.Buffered(1))


def _after(body, n_in, deps):
    def wrapped(*refs):
        return body(*refs[:n_in], *refs[n_in + len(deps):])
    return wrapped


def _lane_sel(lane, a2, a4, a8, a16):
    return jnp.where(lane < POOL_GW, a2, jnp.where(lane < 2 * POOL_GW, a4, jnp.where(lane < 3 * POOL_GW, a8, a16)))


def _window_sums_back(src, tmp_a, tmp_b, tm):
    n = HALO + tm
    rows = lambda ref, lo, back: ref[pl.ds(lo - back, n - lo), :]
    tmp_a[pl.ds(8, n - 8), :] = rows(src, 8, 0) + rows(src, 8, 1)
    tmp_b[pl.ds(16, n - 16), :] = rows(tmp_a, 16, 0) + rows(tmp_a, 16, 2)
    s2 = rows(tmp_a, HALO, 0)
    tmp_a[pl.ds(24, n - 24), :] = rows(tmp_b, 24, 0) + rows(tmp_b, 24, 4)
    s8 = rows(tmp_a, HALO, 0)
    return s2, rows(tmp_b, HALO, 0), s8, s8 + rows(tmp_a, HALO, 8)


def _window_sums_ahead(src, tmp_a, tmp_b, tm):
    rows = lambda ref, n, ahead: ref[pl.ds(ahead, n), :]
    tmp_a[pl.ds(0, tm + 24), :] = rows(src, tm + 24, 0) + rows(src, tm + 24, 1)
    tmp_b[pl.ds(0, tm + 16), :] = rows(tmp_a, tm + 16, 0) + rows(tmp_a, tm + 16, 2)
    s2 = rows(tmp_a, tm, 0)
    tmp_a[pl.ds(0, tm + 8), :] = rows(tmp_b, tm + 8, 0) + rows(tmp_b, tm + 8, 4)
    s8 = rows(tmp_a, tm, 0)
    return s2, rows(tmp_b, tm, 0), s8, s8 + rows(tmp_a, tm, 8)


def _pool_counts(tm, t0):
    lane = lax.broadcasted_iota(jnp.int32, (tm, D_POOL), 1)
    row = lax.broadcasted_iota(jnp.int32, (tm, D_POOL), 0) + t0
    cnt = jnp.minimum(row + 1, _lane_sel(lane, 2, 4, 8, 16)).astype(F32)
    return lane, cnt


def _pool_fwd(ext_q, tmp_a, tmp_b, tm, t0):
    lane, cnt = _pool_counts(tm, t0)
    q = ext_q[pl.ds(HALO, tm), :]
    pooled = _lane_sel(lane, *_window_sums_back(ext_q, tmp_a, tmp_b, tm)) / cnt - q
    return pooled, lane, cnt


def _taps(src, w_of, offs, tm, zbuf):
    acc = None
    for r in range(8):
        ks = [k for k in range(len(offs)) if offs[k] % 8 == r]
        if not ks:
            continue
        rows = tm + (8 if r else 0)
        z = w_of(ks[0]) * src[pl.ds(offs[ks[0]] - r, rows), :]
        for k in ks[1:]:
            z = z + w_of(k) * src[pl.ds(offs[k] - r, rows), :]
        if r:
            zbuf[...] = z
            z = zbuf[pl.ds(r, tm), :]
        acc = z if acc is None else acc + z
    return acc


def _tap_grads(d_pad, src, offs, tm, g_ref, zbuf):
    ch = src.shape[-1]
    for r in range(8):
        ks = [k for k in range(len(offs)) if offs[k] % 8 == r]
        if not ks:
            continue
        rows = tm + (8 if r else 0)
        if r:
            zbuf[...] = d_pad[pl.ds(8 - r, rows), :]
        for k in ks:
            d = zbuf[...] if r else d_pad[pl.ds(8, rows), :]
            prod = d * src[pl.ds(offs[k] - r, rows), :]
            g_ref[k] += jnp.sum(prod.reshape(rows // 8, 8, ch), axis=0)


_CONV_OFFS = [HALO - (CONV_K - 1) + k for k in range(CONV_K)]


def _conv_fwd(ext_u, dww_ref, dwb, tm, zbuf):
    return dwb + _taps(ext_u, lambda k: dww_ref[k:k + 1, :], _CONV_OFFS, tm, zbuf)


def _ln_silu(c, lng, lnb):
    mu = jnp.mean(c, axis=-1, keepdims=True)
    cc = c - mu
    rstd = lax.rsqrt(jnp.mean(cc * cc, axis=-1, keepdims=True) + EPS)
    z = cc * rstd
    l = z * lng + lnb
    sl = _sig(l)
    return z, rstd, l, sl, l * sl


def _rg_conv(ext_x, cw_ref, cb, tm):
    xc = cb + cw_ref[0:1, :] * ext_x[pl.ds(HALO - (RG_CONV_K - 1), tm), :]
    for k in range(1, RG_CONV_K):
        xc = xc + cw_ref[k:k + 1, :] * ext_x[pl.ds(HALO - (RG_CONV_K - 1) + k, tm), :]
    return xc


def _rg_gates(xc, wa, ba, wx, bx, lam):
    xcb = xc.astype(BF16)
    r = _sig(_dot(xcb, wa) + ba)
    ig = _sig(_dot(xcb, wx) + bx)
    sp = jnp.maximum(-lam, 0.0) + jnp.log(1.0 + jnp.exp(-jnp.abs(lam)))
    log_a = (-RG_C * r) * sp
    a = jnp.exp(log_a)
    m = jnp.sqrt(_one_minus_sq(a, log_a))
    return xcb, r, ig, sp, a, m


def _scan_rows(a_ref, b_ref, out_ref, carry, tm, reverse):
    rows = lax.broadcasted_iota(jnp.int32, (8, D_RNN), 0)
    ngrp = tm // 8

    def grp(gi, hb):
        st = pl.multiple_of((ngrp - 1 - gi if reverse else gi) * 8, 8)
        a8 = a_ref[pl.ds(st, 8), :]
        b8 = b_ref[pl.ds(st, 8), :]
        out = jnp.zeros((8, D_RNN), F32)
        for j in (range(7, -1, -1) if reverse else range(8)):
            aj = jnp.broadcast_to(a8[j:j + 1, :], (8, D_RNN))
            bj = jnp.broadcast_to(b8[j:j + 1, :], (8, D_RNN))
            if reverse:
                cur = bj + hb
                hb = aj * cur
            else:
                cur = aj * hb + bj
                hb = cur
            out = jnp.where(rows == j, cur, out)
        out_ref[pl.ds(st, 8), :] = out
        return hb

    carry[...] = lax.fori_loop(0, ngrp, grp, carry[...])


_MIX_W = ("wp", "psc", "dww", "dwb", "lng", "lnb", "wpw", "cw", "cb", "wa", "ba", "wx", "bx", "lam")


def _mixer_fwd(h, g, w_in, mw):
    t = h.shape[0]
    tm = _row_tile(t, TM_MIX)

    def body(h_ref, g_ref, win_ref, wp, psc, dww, dwb, lng, lnb, wpw, cw, cb, wa, ba, wx, bx, lam,
             y_ref, p_ref, u_ref, hs_ref, conv_ref, xc_ref, ext_q, ext_u, ext_x, tmp_a, tmp_b, zbuf, a_s, b_s, hcar):
        i = pl.program_id(0)

        @pl.when(i == 0)
        def _():
            ext_q[0:HALO, :] = jnp.zeros((HALO, D_POOL), F32)
            ext_u[0:HALO, :] = jnp.zeros((HALO, D_CONV), F32)
            ext_x[0:HALO, :] = jnp.zeros((HALO, D_RNN), F32)
            hcar[...] = jnp.zeros((8, D_RNN), F32)

        u = (_rms(h_ref[...])[1] * g_ref[...]).astype(BF16)
        u_ref[...] = u
        p_ref[...] = _dot_nt(u, win_ref[...])

        ext_q[pl.ds(HALO, tm), :] = p_ref[:, 0:256]
        pooled, _, _ = _pool_fwd(ext_q, tmp_a, tmp_b, tm, i * tm)
        y_ref[:, 0:256] = (_dot(pooled.astype(BF16), wp[...]) * psc[...]).astype(BF16)

        ext_u[pl.ds(HALO, tm), :] = p_ref[:, 256:512] * _sig(p_ref[:, 512:768])
        conv = _conv_fwd(ext_u, dww, dwb[...], tm, zbuf)
        conv_ref[...] = conv
        act = _ln_silu(conv, lng[...], lnb[...])[4]
        y_ref[:, 256:512] = _dot(act.astype(BF16), wpw[...]).astype(BF16)

        ext_x[pl.ds(HALO, tm), :] = p_ref[:, 1280:1792]
        xc = _rg_conv(ext_x, cw, cb[...], tm)
        xc_ref[...] = xc
        _, _, ig, _, a, m = _rg_gates(xc, wa[...], ba[...], wx[...], bx[...], lam[...])
        a_s[...] = a
        b_s[...] = m * (ig * xc)
        _scan_rows(a_s, b_s, hs_ref, hcar, tm, reverse=False)
        y_ref[:, 512:1024] = (_gelu_and_grad(p_ref[:, 768:1280])[0] * hs_ref[...]).astype(BF16)

        ext_q[0:HALO, :] = ext_q[pl.ds(tm, HALO), :]
        ext_u[0:HALO, :] = ext_u[pl.ds(tm, HALO), :]
        ext_x[0:HALO, :] = ext_x[pl.ds(tm, HALO), :]

    ws = [mw[k] for k in _MIX_W]
    row = lambda w: pl.BlockSpec((tm, w), lambda i: (i, 0))
    return pl.pallas_call(
        body, name="mixer_fwd", grid=(t // tm,),
        in_specs=[row(D_MODEL), _full(g), _resident(w_in)] + [_full(w) for w in ws],
        out_specs=[row(D_MODEL), row(D_IN), row(D_MODEL), row(D_RNN), row(D_CONV), row(D_RNN)],
        out_shape=[jax.ShapeDtypeStruct((t, D_MODEL), BF16), jax.ShapeDtypeStruct((t, D_IN), F32),
                   jax.ShapeDtypeStruct((t, D_MODEL), BF16), jax.ShapeDtypeStruct((t, D_RNN), F32),
                   jax.ShapeDtypeStruct((t, D_CONV), F32), jax.ShapeDtypeStruct((t, D_RNN), F32)],
        scratch_shapes=[pltpu.VMEM((HALO + tm, D_POOL), F32), pltpu.VMEM((HALO + tm, D_CONV), F32),
                        pltpu.VMEM((HALO + tm, D_RNN), F32), pltpu.VMEM((HALO + tm, D_POOL), F32),
                        pltpu.VMEM((HALO + tm, D_POOL), F32), pltpu.VMEM((tm + 8, D_CONV), F32),
                        pltpu.VMEM((tm, D_RNN), F32), pltpu.VMEM((tm, D_RNN), F32), pltpu.VMEM((8, D_RNN), F32)],
        compiler_params=_params("arbitrary"),
    )(h, g, w_in, *ws)


_MIX_G = (("wp", (D_POOL, D_POOL)), ("psc", (1, D_POOL)), ("dww", (32, 8, D_CONV)), ("dwb", (1, D_CONV)),
          ("lng", (1, D_CONV)), ("lnb", (1, D_CONV)), ("wpw", (D_CONV, D_CONV)), ("cw", (8, D_RNN)),
          ("cb", (1, D_RNN)), ("wa", (D_RNN, D_RNN)), ("ba", (1, D_RNN)), ("wx", (D_RNN, D_RNN)),
          ("bx", (1, D_RNN)), ("lam", (1, D_RNN)), ("g1", (1, D_MODEL)))


def _mixer_bwd(p, dh1, hs, conv, xc, h0, g1, w_out, w_in, mw, deps=()):
    t = p.shape[0]
    tm = _row_tile(t, TM_MIX)
    nt = t // tm
    hb = tm // HALO

    def body(p_ref, ph_ref, dh1_ref, hs_ref, hsh_ref, conv_ref, xc_ref, h0_ref, g1_ref, wout_ref, win_ref,
             wp, psc, dww, dwb, lng, lnb, wpw, cw, cb, wa, ba, wx, bx, lam,
             dp_ref, dh0_ref, g_wp, g_psc, g_dww, g_dwb, g_lng, g_lnb, g_wpw, g_cw, g_cb, g_wa, g_ba, g_wx, g_bx, g_lam, g_g1,
             ext_q, ext_u, ext_x, ext_h, ee, dc_s, dx_s, tmp_a, tmp_b, zbuf, d_pad, a_s, b_s, g_s, gcar, dy_ref, dp_s):
        step = pl.program_id(0)
        i = nt - 1 - step
        grads = (g_wp, g_psc, g_dww, g_dwb, g_lng, g_lnb, g_wpw, g_cw, g_cb, g_wa, g_ba, g_wx, g_bx, g_lam, g_g1)
        dy_ref[...] = dh1_ref[...].astype(BF16)
        dy_cols = lambda lo, hi: _dot_nt(dy_ref[...], wout_ref[lo:hi, :])

        @pl.when(step == 0)
        def _():
            for gr in grads:
                gr[...] = jnp.zeros(gr.shape, F32)
            ee[pl.ds(tm, HALO), :] = jnp.zeros((HALO, D_POOL), F32)
            dc_s[pl.ds(tm, HALO), :] = jnp.zeros((HALO, D_CONV), F32)
            dx_s[pl.ds(tm, HALO), :] = jnp.zeros((HALO, D_RNN), F32)
            d_pad[0:8, :] = jnp.zeros((8, D_CONV), F32)
            d_pad[pl.ds(tm + 8, 8), :] = jnp.zeros((8, D_CONV), F32)
            gcar[...] = jnp.zeros((8, D_RNN), F32)

        hm = jnp.where(i == 0, 0.0, 1.0)

        ext_q[0:HALO, :] = ph_ref[:, 0:256] * hm
        ext_q[pl.ds(HALO, tm), :] = p_ref[:, 0:256]
        pooled, lane, cnt = _pool_fwd(ext_q, tmp_a, tmp_b, tm, i * tm)
        pooled_b = pooled.astype(BF16)
        dya = dy_cols(0, 256)
        g_psc[...] += _colsum(dya * _dot(pooled_b, wp[...]))
        dmixed_b = (dya * psc[...]).astype(BF16)
        dpooled = _dot_nt(dmixed_b, wp[...])
        g_wp[...] += _dot_tn(pooled_b, dmixed_b)
        ee[0:tm, :] = dpooled / cnt
        dp_s[:, 0:256] = _lane_sel(lane, *_window_sums_ahead(ee, tmp_a, tmp_b, tm)) - dpooled
        ee[pl.ds(tm, HALO), :] = ee[0:HALO, :]

        v = p_ref[:, 256:512]
        s = _sig(p_ref[:, 512:768])
        ext_u[0:HALO, :] = ph_ref[:, 256:512] * _sig(ph_ref[:, 512:768]) * hm
        ext_u[pl.ds(HALO, tm), :] = v * s
        z, rstd, l, sl, act = _ln_silu(conv_ref[...], lng[...], lnb[...])
        dyb_b = dy_cols(256, 512).astype(BF16)
        dact = _dot_nt(dyb_b, wpw[...])
        g_wpw[...] += _dot_tn(act.astype(BF16), dyb_b)
        dl = dact * (sl * (1.0 + l * (1.0 - sl)))
        g_lng[...] += _colsum(dl * z)
        g_lnb[...] += _colsum(dl)
        dz = dl * lng[...]
        dc = rstd * (dz - jnp.mean(dz, axis=-1, keepdims=True) - z * jnp.mean(dz * z, axis=-1, keepdims=True))
        g_dwb[...] += _colsum(dc)
        dc_s[0:tm, :] = dc
        d_pad[pl.ds(8, tm), :] = dc
        _tap_grads(d_pad, ext_u, _CONV_OFFS, tm, g_dww, zbuf)
        du0 = _taps(dc_s, lambda j: dww[CONV_K - 1 - j:CONV_K - j, :], list(range(CONV_K)), tm, zbuf)
        dp_s[:, 256:512] = du0 * s
        dp_s[:, 512:768] = du0 * v * (s * (1.0 - s))
        dc_s[pl.ds(tm, HALO), :] = dc_s[0:HALO, :]

        ext_x[0:HALO, :] = ph_ref[:, 1280:1792] * hm
        ext_x[pl.ds(HALO, tm), :] = p_ref[:, 1280:1792]
        xc = xc_ref[...]
        xcb, r, ig, sp, a, m = _rg_gates(xc, wa[...], ba[...], wx[...], bx[...], lam[...])
        ext_h[0:HALO, :] = hsh_ref[...] * hm
        ext_h[pl.ds(HALO, tm), :] = hs_ref[...]
        dyc = dy_cols(512, 1024)
        gl, dgl = _gelu_and_grad(p_ref[:, 768:1280])
        dp_s[:, 768:1280] = dyc * hs_ref[...] * dgl
        a_s[...] = a
        b_s[...] = dyc * gl
        _scan_rows(a_s, b_s, g_s, gcar, tm, reverse=True)
        g = g_s[...]
        da = g * ext_h[pl.ds(HALO - 1, tm), :]
        dm = g * (ig * xc)
        dig = g * (m * xc)
        dlog_a = da * a - dm * (a * a) / m
        g_lam[...] += _colsum(dlog_a * (-RG_C * r)) * (-_sig(-lam[...]))
        dra = (dlog_a * (-RG_C * sp)) * (r * (1.0 - r))
        dia = dig * (ig * (1.0 - ig))
        g_ba[...] += _colsum(dra)
        g_bx[...] += _colsum(dia)
        dra_b = dra.astype(BF16)
        dia_b = dia.astype(BF16)
        dxc = g * (m * ig) + _dot_nt(dra_b, wa[...]) + _dot_nt(dia_b, wx[...])
        g_wa[...] += _dot_tn(xcb, dra_b)
        g_wx[...] += _dot_tn(xcb, dia_b)
        g_cb[...] += _colsum(dxc)
        dx_s[0:tm, :] = dxc
        for k in range(RG_CONV_K):
            g_cw[k:k + 1, :] += _colsum(dxc * ext_x[pl.ds(HALO - (RG_CONV_K - 1) + k, tm), :])
        dxin = cw[RG_CONV_K - 1:RG_CONV_K, :] * dxc
        for j in range(1, RG_CONV_K):
            dxin = dxin + cw[RG_CONV_K - 1 - j:RG_CONV_K - j, :] * dx_s[pl.ds(j, tm), :]
        dp_s[:, 1280:1792] = dxin
        dx_s[pl.ds(tm, HALO), :] = dx_s[0:HALO, :]

        dpb = dp_s[...].astype(BF16)
        dp_ref[...] = dpb
        du = _dot(dpb, win_ref[...])
        r, n = _rms(h0_ref[...])
        g_g1[...] += _colsum(du * n)
        dh0_ref[...] = dh1_ref[...] + _rms_bwd(du, n, r, g1_ref[...])

    ws = [mw[k] for k in _MIX_W]
    tile = lambda w: pl.BlockSpec((tm, w), lambda s: (nt - 1 - s, 0))
    halo = lambda w: pl.BlockSpec((HALO, w), lambda s: (jnp.maximum((nt - 1 - s) * hb - 1, 0), 0))
    outs = pl.pallas_call(
        _after(body, 11 + len(ws), deps), name="mixer_bwd", grid=(nt,),
        in_specs=[tile(D_IN), halo(D_IN), tile(D_MODEL), tile(D_RNN), halo(D_RNN), tile(D_CONV), tile(D_RNN), tile(D_MODEL), _full(g1),
                  _resident(w_out), _resident(w_in)] + [_full(w) for w in ws] + [ANY] * len(deps),
        out_specs=[tile(D_IN), tile(D_MODEL)] + [pl.BlockSpec(shp, lambda s, nd=len(shp): (0,) * nd) for _, shp in _MIX_G],
        out_shape=[jax.ShapeDtypeStruct((t, D_IN), BF16), jax.ShapeDtypeStruct((t, D_MODEL), F32)]
        + [jax.ShapeDtypeStruct(shp, F32) for _, shp in _MIX_G],
        scratch_shapes=[pltpu.VMEM((HALO + tm, D_POOL), F32), pltpu.VMEM((HALO + tm, D_CONV), F32),
                        pltpu.VMEM((HALO + tm, D_RNN), F32), pltpu.VMEM((HALO + tm, D_RNN), F32),
                        pltpu.VMEM((tm + HALO, D_POOL), F32), pltpu.VMEM((tm + HALO, D_CONV), F32),
                        pltpu.VMEM((tm + HALO, D_RNN), F32), pltpu.VMEM((HALO + tm, D_POOL), F32),
                        pltpu.VMEM((HALO + tm, D_POOL), F32), pltpu.VMEM((tm + 8, D_CONV), F32),
                        pltpu.VMEM((tm + 16, D_CONV), F32), pltpu.VMEM((tm, D_RNN), F32),
                        pltpu.VMEM((tm, D_RNN), F32), pltpu.VMEM((tm, D_RNN), F32), pltpu.VMEM((8, D_RNN), F32),
                        pltpu.VMEM((tm, D_MODEL), BF16), pltpu.VMEM((tm, D_IN), F32)],
        compiler_params=_params("arbitrary"),
    )(p, p, dh1, hs, hs, conv, xc, h0, g1, w_out, w_in, *ws, *deps)
    return outs[0], outs[1], {k: o for (k, _), o in zip(_MIX_G, outs[2:])}


def _mid_fwd(y, h0, w_out, g, w_up):
    t = h0.shape[0]
    tm = _row_tile(t, TM_MAT)

    def body(y_ref, h0_ref, wo_ref, g_ref, wu_ref, h1_ref, u2_ref, f_ref):
        h1 = h0_ref[...] + _dot(y_ref[...], wo_ref[...])
        h1_ref[...] = h1
        u2 = (_rms(h1)[1] * g_ref[...]).astype(BF16)
        u2_ref[...] = u2
        for c in range(D_FF // FF_CHUNK):
            f_ref[:, c * FF_CHUNK:(c + 1) * FF_CHUNK] = _dot(u2, wu_ref[c]).astype(BF16)

    row = lambda w: pl.BlockSpec((tm, w), lambda i: (i, 0))
    return pl.pallas_call(
        body, name="mid_fwd", grid=(t // tm,),
        in_specs=[row(D_MODEL), row(D_MODEL), _resident(w_out), _full(g), _resident(w_up)],
        out_specs=[row(D_MODEL), row(D_MODEL), row(D_FF)],
        out_shape=[jax.ShapeDtypeStruct((t, D_MODEL), F32), jax.ShapeDtypeStruct((t, D_MODEL), BF16),
                   jax.ShapeDtypeStruct((t, D_FF), BF16)],
        compiler_params=_params("parallel"),
    )(y, h0, w_out, g, w_up)


def _down_fwd(f, h1, w_down):
    t = h1.shape[0]
    tm = _row_tile(t, TM_MAT)

    def body(f_ref, h1_ref, wd_ref, h2_ref):
        acc = h1_ref[...]
        for c in range(D_FF // FF_CHUNK):
            cols = slice(c * FF_CHUNK, (c + 1) * FF_CHUNK)
            a = jnp.square(jnp.maximum(f_ref[:, cols].astype(F32), 0.0)).astype(BF16)
            acc = acc + _dot(a, wd_ref[cols, :])
        h2_ref[...] = acc

    return pl.pallas_call(
        body, name="down_fwd", grid=(t // tm,),
        in_specs=[pl.BlockSpec((tm, D_FF), lambda i: (i, 0)), pl.BlockSpec((tm, D_MODEL), lambda i: (i, 0)), _resident(w_down)],
        out_specs=pl.BlockSpec((tm, D_MODEL), lambda i: (i, 0)),
        out_shape=jax.ShapeDtypeStruct((t, D_MODEL), F32),
        compiler_params=_params("parallel"),
    )(f, h1, w_down)


def _loss_head(h, g, tgt, t_real):
    t = h.shape[0]
    tm = _row_tile(t, TM_MAT)

    def body(h_ref, g_ref, tgt_ref, loss_ref, dh_ref, dg_ref):
        i = pl.program_id(0)

        @pl.when(i == 0)
        def _():
            loss_ref[...] = jnp.zeros(loss_ref.shape, F32)
            dg_ref[...] = jnp.zeros(dg_ref.shape, F32)

        r, n = _rms(h_ref[...])
        row = lax.broadcasted_iota(jnp.int32, (tm, 1), 0) + i * tm
        valid = jnp.logical_and(row >= N_META, row < t_real)
        diff = jnp.where(valid, n * g_ref[...] - tgt_ref[...], 0.0)
        loss_ref[...] += 0.5 * jnp.sum(jnp.mean(diff * diff, axis=-1, keepdims=True))
        dy = diff * (1.0 / D_MODEL)
        dg_ref[...] += _colsum(dy * n)
        dh_ref[...] = _rms_bwd(dy, n, r, g_ref[...])

    return pl.pallas_call(
        body, name="loss_head", grid=(t // tm,),
        in_specs=[pl.BlockSpec((tm, D_MODEL), lambda i: (i, 0)), _full(g), pl.BlockSpec((tm, D_MODEL), lambda i: (i, 0))],
        out_specs=[pl.BlockSpec((8, 128), lambda i: (0, 0)), pl.BlockSpec((tm, D_MODEL), lambda i: (i, 0)),
                   pl.BlockSpec((1, D_MODEL), lambda i: (0, 0))],
        out_shape=[jax.ShapeDtypeStruct((8, 128), F32), jax.ShapeDtypeStruct((t, D_MODEL), F32),
                   jax.ShapeDtypeStruct((1, D_MODEL), F32)],
        compiler_params=_params("arbitrary"),
    )(h, g, tgt)


def _mlp_bwd(dh2, f, h1, g, w_up, w_down, deps=()):
    t = dh2.shape[0]
    tm = _row_tile(t, TM_MLP_BWD)

    def body(dh2_ref, f_ref, wd_ref, wu_ref, h1_ref, g_ref, df_ref, dh1_ref, dg_ref):
        @pl.when(pl.program_id(0) == 0)
        def _():
            dg_ref[...] = jnp.zeros(dg_ref.shape, F32)

        dh2 = dh2_ref[...]
        dhb = dh2.astype(BF16)
        du2 = None
        for c in range(D_FF // FF_CHUNK):
            cols = slice(c * FF_CHUNK, (c + 1) * FF_CHUNK)
            dact = _dot_nt(dhb, wd_ref[c])
            df = (dact * (2.0 * jnp.maximum(f_ref[:, cols].astype(F32), 0.0))).astype(BF16)
            df_ref[:, cols] = df
            part = _dot_nt(df, wu_ref[c])
            du2 = part if du2 is None else du2 + part
        r, n = _rms(h1_ref[...])
        dg_ref[...] += _colsum(du2 * n)
        dh1_ref[...] = dh2 + _rms_bwd(du2, n, r, g_ref[...])

    row = lambda w: pl.BlockSpec((tm, w), lambda i: (i, 0))
    return pl.pallas_call(
        _after(body, 6, deps), name="mlp_bwd", grid=(t // tm,),
        in_specs=[row(D_MODEL), row(D_FF), _resident(w_down), _resident(w_up), row(D_MODEL), _full(g)] + [ANY] * len(deps),
        out_specs=[row(D_FF), row(D_MODEL), pl.BlockSpec((1, D_MODEL), lambda i: (0, 0))],
        out_shape=[jax.ShapeDtypeStruct((t, D_FF), BF16), jax.ShapeDtypeStruct((t, D_MODEL), F32),
                   jax.ShapeDtypeStruct((1, D_MODEL), F32)],
        compiler_params=_params("arbitrary"),
    )(dh2, f, w_down, w_up, h1, g, *deps)


def _tn_matmul(a, b, kc, nc, relu2, name, deps=()):
    t, k = a.shape
    n = b.shape[1]
    tt = _row_tile(t, TM_MAT)
    gk, gn = k // kc, n // nc

    def body(a_ref, b_ref, o_ref):
        @pl.when(pl.program_id(2) == 0)
        def _():
            o_ref[...] = jnp.zeros(o_ref.shape, F32)

        av = a_ref[...]
        if relu2:
            av = jnp.square(jnp.maximum(av.astype(F32), 0.0))
        o_ref[...] += _dot_tn(av.astype(BF16), b_ref[...].astype(BF16))

    return pl.pallas_call(
        _after(body, 2, deps), name=name, grid=(gk, gn, t // tt),
        in_specs=[pl.BlockSpec((tt, kc), lambda ik, jn, it: (it, ik)), pl.BlockSpec((tt, nc), lambda ik, jn, it: (it, jn))]
        + [ANY] * len(deps),
        out_specs=pl.BlockSpec((None, kc, nc), lambda ik, jn, it: (ik * gn + jn, 0, 0)),
        out_shape=jax.ShapeDtypeStruct((gk * gn, kc, nc), F32),
        compiler_params=_params("parallel", "parallel", "arbitrary"),
    )(a, b, *deps)


def _block_diag(blocks):
    nb, hd, _ = blocks.shape
    eye = jnp.eye(nb, dtype=blocks.dtype)
    return (blocks[:, :, None, :] * eye[:, None, :, None]).reshape(nb * hd, nb * hd)


def _diag_blocks(m, nb):
    hd = m.shape[0] // nb
    eye = jnp.eye(nb, dtype=m.dtype)
    return jnp.sum(m.reshape(nb, hd, nb, hd) * eye[:, None, :, None], axis=2)


def _mixer_weights(w, l):
    row = lambda a: a.reshape(1, -1)
    return dict(
        wp=_block_diag(w["pool_w"][l]).astype(BF16), psc=row(w["pool_scale"][l]),
        dww=jnp.pad(w["convb_dw_w"][l], ((0, 32 - CONV_K), (0, 0))), dwb=row(w["convb_dw_b"][l]),
        lng=row(w["convb_ln_g"][l]), lnb=row(w["convb_ln_b"][l]), wpw=w["convb_pw_w"][l].astype(BF16),
        cw=jnp.pad(w["rg_conv_w"][l], ((0, 8 - RG_CONV_K), (0, 0))), cb=row(w["rg_conv_b"][l]),
        wa=_block_diag(w["rg_w_a"][l]).astype(BF16), ba=row(w["rg_b_a"][l]),
        wx=_block_diag(w["rg_w_x"][l]).astype(BF16), bx=row(w["rg_b_x"][l]), lam=row(w["rg_lambda"][l]))


def _local_step(h, tgt, t_real, w, fetch, hooks):
    depth = 2
    saved = []
    big = []
    for l in range(depth):
        mw = _mixer_weights(w, l)
        g1 = w["mix_norm_g"][l].reshape(1, -1)
        g2 = w["mlp_norm_g"][l].reshape(1, -1)
        wl = dict(w_in=fetch(l, "w_in", h))
        y, p, u, hs, conv, xc = _mixer_fwd(h, g1, wl["w_in"], mw)
        wl["w_out"], wl["w_up"] = fetch(l, "w_out", y), fetch(l, "w_up", y)
        h1, u2, f = _mid_fwd(y, h, wl["w_out"], g2, wl["w_up"])
        wl["w_down"] = fetch(l, "w_down", f)
        h2 = _down_fwd(f, h1, wl["w_down"].reshape(D_FF, D_MODEL))
        saved.append(dict(mw=mw, g1=g1, g2=g2, h0=h, p=p, u=u, y=y, hs=hs, conv=conv, xc=xc, h1=h1, u2=u2, f=f))
        big.append(wl)
        h = h2
    gf = w["final_norm_g"].reshape(1, -1)
    loss, dh, dgf = _loss_head(h, gf, tgt, t_real)

    gs = {k: [None] * depth for k in ("mix_norm_g", "mlp_norm_g", "pool_w", "pool_scale", "convb_dw_w", "convb_dw_b",
                                      "convb_ln_g", "convb_ln_b", "convb_pw_w", "rg_conv_w", "rg_conv_b", "rg_w_a",
                                      "rg_b_a", "rg_w_x", "rg_b_x", "rg_lambda")}
    deps = ()
    for l in reversed(range(depth)):
        s, wl = saved[l], big[l]
        df, dh1, dg2 = _mlp_bwd(dh, s["f"], s["h1"], s["g2"], wl["w_up"], wl["w_down"], deps)
        deps = hooks.point(l, "mlp_bwd", dh1)
        g_down = _tn_matmul(s["f"], dh, FF_CHUNK, D_MODEL, True, "dw_down", deps)
        hooks.grad(l, "w_down", g_down)
        deps = hooks.point(l, "dw_down", g_down)
        g_up = _tn_matmul(s["u2"], df, D_MODEL, FF_CHUNK, False, "dw_up", deps)
        hooks.grad(l, "w_up", g_up)
        deps = hooks.point(l, "dw_up", g_up)
        g_out = _tn_matmul(s["y"], dh1, D_MODEL, D_MODEL, False, "dw_out", deps)
        hooks.grad(l, "w_out", g_out.reshape(N_CHIPS, D_MODEL // N_CHIPS, D_MODEL))
        deps = hooks.point(l, "dw_out", g_out)
        dp, dh, mg = _mixer_bwd(s["p"], dh1, s["hs"], s["conv"], s["xc"], s["h0"], s["g1"], wl["w_out"], wl["w_in"], s["mw"],
                                deps)
        deps = hooks.point(l, "mixer_bwd", dh)
        g_in = _tn_matmul(dp, s["u"], D_IN, D_MODEL, False, "dw_in", deps).reshape(N_CHIPS, D_IN // N_CHIPS, D_MODEL)
        hooks.grad(l, "w_in", g_in)
        deps = hooks.point(l, "dw_in", g_in)
        gs["mix_norm_g"][l] = mg["g1"][0]
        gs["mlp_norm_g"][l] = dg2[0]
        gs["pool_w"][l] = _diag_blocks(mg["wp"], D_POOL // POOL_GW)
        gs["pool_scale"][l] = mg["psc"][0]
        gs["convb_dw_w"][l] = jnp.sum(mg["dww"][:CONV_K], axis=1)
        gs["convb_dw_b"][l] = mg["dwb"][0]
        gs["convb_ln_g"][l] = mg["lng"][0]
        gs["convb_ln_b"][l] = mg["lnb"][0]
        gs["convb_pw_w"][l] = mg["wpw"]
        gs["rg_conv_w"][l] = mg["cw"][:RG_CONV_K]
        gs["rg_conv_b"][l] = mg["cb"][0]
        gs["rg_w_a"][l] = _diag_blocks(mg["wa"], D_RNN // RG_HD)
        gs["rg_b_a"][l] = mg["ba"][0]
        gs["rg_w_x"][l] = _diag_blocks(mg["wx"], D_RNN // RG_HD)
        gs["rg_b_x"][l] = mg["bx"][0]
        gs["rg_lambda"][l] = mg["lam"][0]
    gsmall = {k: jnp.stack(v) for k, v in gs.items()}
    gsmall["final_norm_g"] = dgf[0]
    return loss[0, 0], dh, gsmall


def _place():
    return lax.axis_index("x"), lax.axis_index("y"), lax.axis_index("c")


def _other_chips(x, y):
    return [(1 - x, y), (x, 1 - y), (1 - x, 1 - y)]


HBM_SPEC = pl.BlockSpec(memory_space=pltpu.HBM)
SEM_SPEC = pl.BlockSpec(memory_space=pltpu.SEMAPHORE)
DATAFLOW = pltpu.SideEffectType.DATAFLOW_SIDE_EFFECTING


def _gather_copies(src_refs, land_refs, send_sem, recv_sem, first):
    x, y, c = _place()
    me = 2 * x + y
    out = []
    for n in range(len(src_refs)):
        for j, (px, py) in enumerate(_other_chips(x, y)):
            out.append(pltpu.make_async_remote_copy(src_refs[n], land_refs[n].at[me], send_sem.at[first + 3 * n + j],
                                                    recv_sem.at[first + 3 * n + j], device_id=(px, py, c), device_id_type=MESH))
    return out


def _gather_start(groups, me):
    srcs = [pltpu.with_memory_space_constraint(s, pltpu.HBM) for g in groups for s in g]
    lands = [pltpu.with_memory_space_constraint(
        lax.dynamic_update_slice(jnp.zeros((N_CHIPS,) + s.shape, s.dtype), s[None], (me,) + (0,) * s.ndim), pltpu.HBM)
        for g in groups for s in g]
    n, ng = len(srcs), len(groups)
    first = [sum(len(g) for g in groups[:i]) for i in range(ng)]

    def body(*refs):
        src_refs, land_refs = refs[:n], refs[n:2 * n]
        sems = refs[2 * n:2 * n + 2 * ng]
        token = refs[-1]
        for gi, g in enumerate(groups):
            lo, hi = first[gi], first[gi] + len(g)
            for cp in _gather_copies(src_refs[lo:hi], land_refs[lo:hi], sems[2 * gi], sems[2 * gi + 1], 0):
                cp.start()
        token[...] = jnp.zeros(token.shape, token.dtype)

    sem_shapes = [pltpu.SemaphoreType.DMA((3 * len(g),)) for g in groups for _ in range(2)]
    outs = pl.pallas_call(
        body, name="gather_start",
        out_shape=sem_shapes + [pltpu.HBM(a.shape, a.dtype) for a in srcs + lands] + [jax.ShapeDtypeStruct((8, 128), F32)],
        in_specs=[HBM_SPEC] * (2 * n),
        out_specs=[SEM_SPEC] * (2 * ng) + [HBM_SPEC] * (2 * n) + [pl.BlockSpec(memory_space=pltpu.VMEM)],
        input_output_aliases={i: 2 * ng + i for i in range(2 * n)},
        compiler_params=pltpu.CompilerParams(has_side_effects=DATAFLOW),
    )(*srcs, *lands)
    sems, thru, token = outs[:2 * ng], outs[2 * ng:2 * ng + 2 * n], outs[-1]
    state = []
    for gi, g in enumerate(groups):
        lo, hi = first[gi], first[gi] + len(g)
        state.append((sems[2 * gi], sems[2 * gi + 1], thru[lo:hi], thru[n + lo:n + hi]))
    return state, token


def _gather_wait(state, after, name):
    send_sem, recv_sem, srcs, lands = state
    n = len(srcs)

    def body(*refs):
        src_refs, land_refs = refs[:n], refs[n:2 * n]
        send, recv = refs[2 * n], refs[2 * n + 1]
        for cp in _gather_copies(src_refs, land_refs, send, recv, 0):
            cp.wait_send()
            cp.wait_recv()

    outs = pl.pallas_call(
        body, name=name,
        out_shape=[pltpu.HBM(a.shape, a.dtype) for a in list(srcs) + list(lands)],
        in_specs=[HBM_SPEC] * (2 * n) + [SEM_SPEC, SEM_SPEC, ANY],
        out_specs=[HBM_SPEC] * (2 * n),
        input_output_aliases={i: i for i in range(2 * n)},
        compiler_params=pltpu.CompilerParams(has_side_effects=DATAFLOW),
    )(*srcs, *lands, send_sem, recv_sem, after)
    return outs[n:]


def _add_halves(g, recv, c1):
    nk, r, cd = g.shape
    r2 = r // 2

    def body(c_ref, g_ref, r_ref, pa_ref, pab_ref):
        s = g_ref[...] + r_ref[...]
        pa_ref[...] = s
        pab_ref[...] = s.astype(BF16)

    blk = pl.BlockSpec((None, r2, cd), lambda k, c_ref: (k, 0, 0))
    return pl.pallas_call(
        body, name="rs_add_halves",
        grid_spec=pltpu.PrefetchScalarGridSpec(
            num_scalar_prefetch=1, grid=(nk,),
            in_specs=[pl.BlockSpec((None, r2, cd), lambda k, c_ref: (k, c_ref[0], 0)), blk], out_specs=[blk, blk]),
        out_shape=[jax.ShapeDtypeStruct((nk, r2, cd), F32), jax.ShapeDtypeStruct((nk, r2, cd), BF16)],
        compiler_params=_params("parallel"),
    )(c1, g, recv)


def _sum_partials(pa, recv, me1):
    nk, r2, cd = pa.shape

    def body(me_ref, pa_ref, r_ref, s_ref):
        s_ref[...] = ((pa_ref[...] + r_ref[0].astype(F32)) + r_ref[1].astype(F32)) + r_ref[2].astype(F32)

    return pl.pallas_call(
        body, name="rs_sum_partials",
        grid_spec=pltpu.PrefetchScalarGridSpec(
            num_scalar_prefetch=1, grid=(1,),
            in_specs=[pl.BlockSpec((None, r2, cd), lambda i, me_ref: (me_ref[0], 0, 0)),
                      pl.BlockSpec((3, r2, cd), lambda i, me_ref: (0, 0, 0))],
            out_specs=pl.BlockSpec((r2, cd), lambda i, me_ref: (0, 0))),
        out_shape=jax.ShapeDtypeStruct((r2, cd), F32),
        compiler_params=_params("arbitrary"),
    )(me1, pa, recv)


def _split_start(name, srcs, lands, ncopies, make_copies):
    srcs = [pltpu.with_memory_space_constraint(s, pltpu.HBM) for s in srcs]
    lands = [pltpu.with_memory_space_constraint(a, pltpu.HBM) for a in lands]
    n, m = len(srcs), len(lands)

    def body(*refs):
        src_refs, land_refs = refs[:n], refs[n:n + m]
        send, recv, token = refs[n + m], refs[n + m + 1], refs[-1]
        for cp in make_copies(src_refs, land_refs, send, recv):
            cp.start()
        token[...] = jnp.zeros(token.shape, token.dtype)

    outs = pl.pallas_call(
        body, name=name,
        out_shape=[pltpu.SemaphoreType.DMA((ncopies,)), pltpu.SemaphoreType.DMA((ncopies,))]
        + [pltpu.HBM(a.shape, a.dtype) for a in srcs + lands] + [jax.ShapeDtypeStruct((8, 128), F32)],
        in_specs=[HBM_SPEC] * (n + m),
        out_specs=[SEM_SPEC, SEM_SPEC] + [HBM_SPEC] * (n + m) + [pl.BlockSpec(memory_space=pltpu.VMEM)],
        input_output_aliases={i: 2 + i for i in range(n + m)},
        compiler_params=pltpu.CompilerParams(has_side_effects=DATAFLOW),
    )(*srcs, *lands)
    return (outs[0], outs[1], outs[2:2 + n], outs[2 + n:2 + n + m], make_copies), outs[-1]


def _split_wait(name, state, after):
    send_sem, recv_sem, srcs, lands, make_copies = state
    n, m = len(srcs), len(lands)

    def body(*refs):
        src_refs, land_refs = refs[:n], refs[n:n + m]
        for cp in make_copies(src_refs, land_refs, refs[n + m], refs[n + m + 1]):
            cp.wait_send()
            cp.wait_recv()

    outs = pl.pallas_call(
        body, name=name,
        out_shape=[pltpu.HBM(a.shape, a.dtype) for a in list(srcs) + list(lands)],
        in_specs=[HBM_SPEC] * (n + m) + [SEM_SPEC, SEM_SPEC, ANY],
        out_specs=[HBM_SPEC] * (n + m),
        input_output_aliases={i: i for i in range(n + m)},
        compiler_params=pltpu.CompilerParams(has_side_effects=DATAFLOW),
    )(*srcs, *lands, send_sem, recv_sem, after)
    return outs[:n], outs[n:]


def _copies_to_sibling(src_of):
    def make(src_refs, land_refs, send, recv):
        x, y, c = _place()
        return [pltpu.make_async_remote_copy(src_of(src_refs[i], c), land_refs[i], send.at[i], recv.at[i],
                                             device_id=(x, y, 1 - c), device_id_type=MESH) for i in range(len(src_refs))]
    return make


def _copies_to_chips(src_refs, land_refs, send, recv):
    x, y, c = _place()
    return [pltpu.make_async_remote_copy(src_refs[i].at[2 * px + py], land_refs[i].at[j], send.at[3 * i + j], recv.at[3 * i + j],
                                         device_id=(px, py, c), device_id_type=MESH)
            for i in range(len(src_refs)) for j, (px, py) in enumerate(_other_chips(x, y))]


def _other_half_rows(ref, c):
    r2 = ref.shape[1] // 2
    return ref.at[:, pl.ds(pl.multiple_of((1 - c) * r2, 8), r2)]


class _ReduceScatter:
    def __init__(self, tag, grads, c1, me1):
        self.tag, self.grads, self.c1, self.me1 = tag, grads, c1, me1

    def start(self):
        lands = [lax.empty((g.shape[0], g.shape[1] // 2, g.shape[2]), F32) for g in self.grads]
        self.state, token = _split_start("rs_%s_a_start" % self.tag, self.grads, lands, len(self.grads),
                                         _copies_to_sibling(_other_half_rows))
        return token

    def to_chips(self, after):
        grads, recv = _split_wait("rs_%s_a_wait" % self.tag, self.state, after)
        added = [_add_halves(g, r, self.c1) for g, r in zip(grads, recv)]
        self.own = [pa for pa, _ in added]
        pabs = [pab for _, pab in added]
        lands = [lax.empty((3,) + p.shape[1:], BF16) for p in pabs]
        self.state, token = _split_start("rs_%s_b_start" % self.tag, pabs, lands, 3 * len(pabs), _copies_to_chips)
        return token

    def to_sibling(self, after):
        _, recv = _split_wait("rs_%s_b_wait" % self.tag, self.state, after)
        sums = [_sum_partials(pa, rb, self.me1) for pa, rb in zip(self.own, recv)]
        lands = [lax.empty(s.shape, F32) for s in sums]
        self.state, token = _split_start("rs_%s_c_start" % self.tag, sums, lands, len(sums),
                                         _copies_to_sibling(lambda ref, c: ref))
        return token

    def finish(self, after):
        return list(zip(*_split_wait("rs_%s_c_wait" % self.tag, self.state, after)))


def _allreduce_small(vs):
    n = len(vs)

    def body(*refs):
        v_refs, out_refs, rbufs = refs[:n], refs[n:2 * n], refs[2 * n:3 * n]
        send_sems, recv_sems = refs[3 * n:]
        x, y, c = _place()
        for i in range(n):
            out_refs[i][...] = v_refs[i][...]
        for s, peer in enumerate([(x, y, 1 - c), (1 - x, y, c), (x, 1 - y, c)]):
            copies = [pltpu.make_async_remote_copy(out_refs[i], rbufs[i].at[s], send_sems.at[s * n + i], recv_sems.at[s * n + i],
                                                   device_id=peer, device_id_type=MESH) for i in range(n)]
            for cp in copies:
                cp.start()
            for cp in copies:
                cp.wait()
            for i in range(n):
                out_refs[i][...] = out_refs[i][...] + rbufs[i][s]

    vm = pl.BlockSpec(memory_space=pltpu.VMEM)
    return pl.pallas_call(
        body, name="allreduce_small", in_specs=[vm] * n, out_specs=[vm] * n,
        out_shape=[jax.ShapeDtypeStruct(v.shape, v.dtype) for v in vs],
        scratch_shapes=[pltpu.VMEM((3,) + v.shape, v.dtype) for v in vs]
        + [pltpu.SemaphoreType.DMA((3 * n,)), pltpu.SemaphoreType.DMA((3 * n,))],
        compiler_params=pltpu.CompilerParams(vmem_limit_bytes=VMEM_LIMIT),
    )(*vs)


def _adamw_math(w, g, m, v):
    m = ADAM_B1 * m + (1.0 - ADAM_B1) * g
    v = ADAM_B2 * v + (1.0 - ADAM_B2) * jnp.square(g)
    m_hat = m / (1.0 - ADAM_B1 ** ADAM_STEP)
    v_hat = v / (1.0 - ADAM_B2 ** ADAM_STEP)
    return -ADAM_LR * (m_hat / (jnp.sqrt(v_hat) + ADAM_EPS) + ADAM_WD * w), m, v


def _adamw_big_layer(layer, w, m, v, own, sib, c1, prev):
    _, r, cd = w.shape
    r2 = r // 2

    def body(c_ref, w_ref, m_ref, v_ref, own_ref, sib_ref, *rest):
        g_ref, d_ref, mo_ref, vo_ref, token = rest[-5:]
        g = jnp.where(pl.program_id(0) == c_ref[0], own_ref[...], sib_ref[...])
        g_ref[...] = g
        d_ref[...], mo_ref[...], vo_ref[...] = _adamw_math(w_ref[...], g, m_ref[...], v_ref[...])
        token[...] = jnp.zeros(token.shape, F32)

    blk = pl.BlockSpec((None, r2, cd), lambda hh, c_ref: (layer, hh, 0))
    half = pl.BlockSpec((r2, cd), lambda hh, c_ref: (0, 0))
    prev = () if prev is None else tuple(prev)
    outs = pl.pallas_call(
        body, name="adamw_big",
        grid_spec=pltpu.PrefetchScalarGridSpec(
            num_scalar_prefetch=1, grid=(2,), in_specs=[blk, blk, blk, half, half] + [ANY] * len(prev),
            out_specs=[blk] * 4 + [pl.BlockSpec((8, 128), lambda hh, c_ref: (0, 0))]),
        out_shape=[jax.ShapeDtypeStruct(w.shape, F32)] * 4 + [jax.ShapeDtypeStruct((8, 128), F32)],
        input_output_aliases={6 + i: i for i in range(len(prev))},
        compiler_params=_params("arbitrary"),
    )(c1, w, m, v, own, sib, *prev)
    return outs[:4], outs[4]


def _adamw_small(ws, gs, ms, vs):
    n = len(ws)

    def body(*refs):
        w_refs, g_refs, m_refs, v_refs = refs[:n], refs[n:2 * n], refs[2 * n:3 * n], refs[3 * n:4 * n]
        outs = refs[4 * n:]
        for i in range(n):
            outs[3 * i][...], outs[3 * i + 1][...], outs[3 * i + 2][...] = _adamw_math(
                w_refs[i][...], g_refs[i][...], m_refs[i][...], v_refs[i][...])

    vm = pl.BlockSpec(memory_space=pltpu.VMEM)
    outs = pl.pallas_call(
        body, name="adamw_small", in_specs=[vm] * (4 * n), out_specs=[vm] * (3 * n),
        out_shape=[jax.ShapeDtypeStruct(w.shape, F32) for w in ws for _ in range(3)],
        compiler_params=pltpu.CompilerParams(vmem_limit_bytes=VMEM_LIMIT),
    )(*ws, *gs, *ms, *vs)
    return [outs[3 * i:3 * i + 3] for i in range(n)]


LANES = 128
SUBLANES = 8
SHARDED_AXIS = {"meta_tokens": 1, "convb_dw_w": 2, "convb_pw_w": 1, "rg_conv_w": 2}


def _rows_of(size):
    return -(-size // (LANES * SUBLANES)) * SUBLANES


def _as_rows(a, rows=None):
    flat = a.reshape(-1)
    rows = _rows_of(flat.size) if rows is None else rows
    return jnp.pad(flat, (0, rows * LANES - flat.size)).reshape(rows, LANES)


class _GradientSchedule:
    GROUPS = {"l1": [(1, "w_down"), (1, "w_up"), (1, "w_out"), (1, "w_in")], "a0": [(0, "w_down"), (0, "w_up")],
              "b0": [(0, "w_out")], "c0": [(0, "w_in")]}
    PLAN = {
        (1, "dw_in"): [("l1", "start")],
        (0, "mlp_bwd"): [("l1", "to_chips")],
        (0, "dw_up"): [("l1", "to_sibling"), ("a0", "start")],
        (0, "dw_out"): [("l1", "finish"), ("a0", "to_chips"), ("b0", "start")],
        (0, "mixer_bwd"): [("a0", "to_sibling"), ("b0", "to_chips")],
        (0, "dw_in"): [("c0", "start"), ("a0", "finish"), ("b0", "to_sibling"), ("c0", "to_chips")],
    }

    def __init__(self, w, mom, var, c1, me1):
        self.w, self.mom, self.var, self.c1, self.me1 = w, mom, var, c1, me1
        self.grads, self.chains, self.out = {}, {}, {}

    def grad(self, layer, name, g):
        self.grads[layer, name] = g

    def point(self, layer, kernel_name, after):
        return self.run(self.PLAN.get((layer, kernel_name), ()), after) + (after,)

    def run(self, actions, after):
        deps = []
        for tag, stage in actions:
            if stage == "start":
                self.chains[tag] = _ReduceScatter(tag, [self.grads[lk] for lk in self.GROUPS[tag]], self.c1, self.me1)
                deps.append(self.chains[tag].start())
            elif stage == "finish":
                for (layer, k), (own, sib) in zip(self.GROUPS[tag], self.chains[tag].finish(after)):
                    self.out[k], token = _adamw_big_layer(layer, self.w[k], self.mom[k], self.var[k], own, sib, self.c1,
                                                          self.out.get(k))
                    deps.append(token)
            else:
                deps.append(getattr(self.chains[tag], stage)(after))
            after = deps[-1]
        self.last = after
        return tuple(deps)


def _from_shard_major(name, sm):
    if name == "meta_tokens":
        return sm.transpose(1, 0, 2).reshape(N_META, -1)
    if name == "convb_pw_w":
        return sm.transpose(1, 0, 2, 3).reshape(2, -1, D_CONV)
    return sm.transpose(1, 2, 0, 3).reshape(sm.shape[1], sm.shape[2], -1)


def kernel(x, meta_tokens, mix_norm_g, w_in, pool_w, pool_scale, convb_dw_w, convb_dw_b, convb_ln_g, convb_ln_b, convb_pw_w, rg_conv_w, rg_conv_b, rg_w_a, rg_b_a, rg_w_x, rg_b_x, rg_lambda, w_out, mlp_norm_g, w_up, w_down, final_norm_g, loss_target, m_meta_tokens, m_mix_norm_g, m_w_in, m_pool_w, m_pool_scale, m_convb_dw_w, m_convb_dw_b, m_convb_ln_g, m_convb_ln_b, m_convb_pw_w, m_rg_conv_w, m_rg_conv_b, m_rg_w_a, m_rg_b_a, m_rg_w_x, m_rg_b_x, m_rg_lambda, m_w_out, m_mlp_norm_g, m_w_up, m_w_down, m_final_norm_g, v_meta_tokens, v_mix_norm_g, v_w_in, v_pool_w, v_pool_scale, v_convb_dw_w, v_convb_dw_b, v_convb_ln_g, v_convb_ln_b, v_convb_pw_w, v_rg_conv_w, v_rg_conv_b, v_rg_w_a, v_rg_b_a, v_rg_w_x, v_rg_b_x, v_rg_lambda, v_w_out, v_mlp_norm_g, v_w_up, v_w_down, v_final_norm_g):
    given = dict(locals())
    w = {k: given[k] for k in WEIGHTS}
    mom = {k: given["m_" + k] for k in WEIGHTS}
    var = {k: given["v_" + k] for k in WEIGHTS}
    xi, yi, ci = _place()
    me1 = (2 * xi + yi).astype(jnp.int32).reshape(1)
    c1 = ci.astype(jnp.int32).reshape(1)

    small_rows = [_rows_of(w[k].size) for k in SMALL_SHARDED]
    small_pack = jnp.concatenate([_as_rows(w[k]) for k in SMALL_SHARDED])
    transposed = lambda d: {**d, "w_in": d["w_in"].transpose(0, 2, 1)}
    wt, momt, vart = transposed(w), transposed(mom), transposed(var)
    shard = lambda l, k: wt[k][l].astype(BF16)
    order = [[(0, "w_in"), "small"], [(0, "w_out"), (0, "w_up")], [(0, "w_down")], [(1, "w_in")], [(1, "w_out"), (1, "w_up")],
             [(1, "w_down")]]
    state, token = _gather_start([[small_pack if lk == "small" else shard(*lk) for lk in g] for g in order], me1[0])
    landed = {}

    def fetch(l, k, after):
        gi = [i for i, g in enumerate(order) if (l, k) in g][0]
        if gi not in landed:
            landed[gi] = _gather_wait(state[gi], after, "gather_wait_%d" % gi)
        raw = landed[gi][order[gi].index((l, k))]
        if k == "w_in":
            return raw.reshape(D_IN, D_MODEL)
        return raw.reshape(D_MODEL, D_MODEL) if k == "w_out" else raw

    seq = x.shape[1]
    t_real = N_META + seq
    t_pad = -(-t_real // ROW_ALIGN) * ROW_ALIGN
    tail = jnp.zeros((t_pad - t_real, D_MODEL), F32)
    front = jnp.zeros((N_META, D_MODEL), F32)
    h = jnp.concatenate([front + token[0, 0], x[0], tail])
    tgt = jnp.concatenate([front, loss_target[0], tail])
    landed[0] = _gather_wait(state[0], h, "gather_wait_0")
    wfull = dict(w)
    off = 0
    for k, rows in zip(SMALL_SHARDED, small_rows):
        sm = landed[0][1][:, off:off + rows].reshape(N_CHIPS, -1)[:, :w[k].size].reshape((N_CHIPS,) + w[k].shape)
        wfull[k] = _from_shard_major(k, sm)
        off += rows
    h = lax.dynamic_update_slice(h, wfull["meta_tokens"], (0, 0))
    sched = _GradientSchedule(wt, momt, vart, c1, me1)
    loss, dh, gsmall = _local_step(h, tgt, t_real, wfull, fetch, sched)
    grad_x = dh[N_META:t_real][None]
    gsmall["meta_tokens"] = dh[:N_META]

    names = SMALL_REPL + SMALL_SHARDED
    two_d = lambda a: a.reshape(1, -1) if a.ndim == 1 else a
    partial = [two_d(gsmall[k]) for k in names]
    partial[0] = partial[0] + sched.last[0, 0]
    summed = dict(zip(names, _allreduce_small(partial)))
    sched.run([("c0", "to_sibling")], summed[names[0]])
    for k in SMALL_SHARDED:
        ax = SHARDED_AXIS[k]
        summed[k] = lax.dynamic_slice_in_dim(summed[k], me1[0] * w[k].shape[ax], w[k].shape[ax], axis=ax)

    out = {}
    res = _adamw_small([two_d(w[k]) for k in names], [summed[k] for k in names], [two_d(mom[k]) for k in names],
                       [two_d(var[k]) for k in names])
    for k, (d, m2, v2) in zip(names, res):
        out[k] = tuple(o.reshape(w[k].shape) for o in (summed[k], d, m2, v2))
    sched.run([("b0", "finish"), ("c0", "finish")], res[0][0])
    out.update(sched.out)
    out["w_in"] = tuple(o.transpose(0, 2, 1) for o in out["w_in"])

    loss = lax.psum(loss, ("x", "y", "c"))
    return (loss, grad_x, *[out[k][0] for k in WEIGHTS], *[out[k][1] for k in WEIGHTS],
            *[out[k][2] for k in WEIGHTS], *[out[k][3] for k in WEIGHTS])
```

```python
import functools

import jax
import jax.numpy as jnp
from jax import lax
from jax.experimental import pallas as pl
from jax.experimental.pallas import tpu as pltpu

F32, BF16 = jnp.float32, jnp.bfloat16
MESH = pl.DeviceIdType.MESH
ANY = pl.BlockSpec(memory_space=pl.ANY)

D_MODEL = 1024
N_META = 16
D_POOL = 256
D_CONV = 256
D_RNN = 512
D_IN = D_POOL + 2 * D_CONV + 2 * D_RNN
D_FF = 4096
FF_CHUNK = 1024
POOL_GW = 64
CONV_K = 31
RG_CONV_K = 4
RG_HD = 64
RG_C = 8.0
EPS = 1e-6
ADAM_LR, ADAM_B1, ADAM_B2, ADAM_EPS, ADAM_WD, ADAM_STEP = 0.001, 0.9, 0.999, 1e-08, 0.01, 10

HALO = 32
ROW_ALIGN = 256
TM_MIX = 384
TM_MAT = 768
TM_MLP_BWD = 384
N_CHIPS = 4
VMEM_LIMIT = 56 * 1024 * 1024

BIG = ("w_in", "w_out", "w_up", "w_down")
SMALL_SHARDED = ("meta_tokens", "convb_dw_w", "convb_pw_w", "rg_conv_w")
SMALL_REPL = ("mix_norm_g", "pool_w", "pool_scale", "convb_dw_b", "convb_ln_g", "convb_ln_b", "rg_conv_b",
              "rg_w_a", "rg_b_a", "rg_w_x", "rg_b_x", "rg_lambda", "mlp_norm_g", "final_norm_g")
WEIGHTS = ("meta_tokens", "mix_norm_g", "w_in", "pool_w", "pool_scale", "convb_dw_w", "convb_dw_b", "convb_ln_g",
           "convb_ln_b", "convb_pw_w", "rg_conv_w", "rg_conv_b", "rg_w_a", "rg_b_a", "rg_w_x", "rg_b_x",
           "rg_lambda", "w_out", "mlp_norm_g", "w_up", "w_down", "final_norm_g")


def _params(*sem):
    return pltpu.CompilerParams(dimension_semantics=sem, vmem_limit_bytes=VMEM_LIMIT)


def _row_tile(t, cap):
    best = None
    for tm in range(128, cap + 1, 128):
        if t % tm == 0:
            best = tm
    assert best is not None, (t, cap)
    return best


def _dot(a, b):
    return jnp.dot(a, b, preferred_element_type=F32)


def _dot_nt(a, b):
    return lax.dot_general(a, b, (((1,), (1,)), ((), ())), preferred_element_type=F32)


def _dot_tn(a, b):
    return lax.dot_general(a, b, (((0,), (0,)), ((), ())), preferred_element_type=F32)


def _rms(x):
    r = lax.rsqrt(jnp.mean(x * x, axis=-1, keepdims=True) + EPS)
    return r, x * r


def _rms_bwd(du, n, r, g):
    dn = du * g
    return r * (dn - n * jnp.mean(dn * n, axis=-1, keepdims=True))


def _sig(x):
    return jax.nn.sigmoid(x)


def _colsum(x):
    return jnp.sum(x, axis=0, keepdims=True)


def _one_minus_sq(a, log_a):
    x = 2.0 * log_a
    series = -x * (1.0 + x * (0.5 + x * (1.0 / 6)))
    return jnp.where(x > -0.01, series, 1.0 - a * a)


_GELU_K0 = 0.7978845608028654
_GELU_K1 = 0.044715


def _gelu_and_grad(x):
    th = jnp.tanh(_GELU_K0 * (x + _GELU_K1 * x * x * x))
    val = 0.5 * x * (1.0 + th)
    grad = 0.5 * (1.0 + th) + 0.5 * x * (1.0 - th * th) * _GELU_K0 * (1.0 + 3.0 * _GELU_K1 * x * x)
    return val, grad


def _full(a):
    nd = a.ndim
    return pl.BlockSpec(a.shape, lambda *_: (0,) * nd)


def _resident(a):
    nd = a.ndim
    return pl.BlockSpec(a.shape, lambda *_: (0,) * nd, pipeline_mode=pl.Buffered(1))


def _after(body, n_in, deps):
    def wrapped(*refs):
        return body(*refs[:n_in], *refs[n_in + len(deps):])
    return wrapped


def _lane_sel(lane, a2, a4, a8, a16):
    return jnp.where(lane < POOL_GW, a2, jnp.where(lane < 2 * POOL_GW, a4, jnp.where(lane < 3 * POOL_GW, a8, a16)))


def _window_sums_back(src, tmp_a, tmp_b, tm):
    n = HALO + tm
    rows = lambda ref, lo, back: ref[pl.ds(lo - back, n - lo), :]
    tmp_a[pl.ds(8, n - 8), :] = rows(src, 8, 0) + rows(src, 8, 1)
    tmp_b[pl.ds(16, n - 16), :] = rows(tmp_a, 16, 0) + rows(tmp_a, 16, 2)
    s2 = rows(tmp_a, HALO, 0)
    tmp_a[pl.ds(24, n - 24), :] = rows(tmp_b, 24, 0) + rows(tmp_b, 24, 4)
    s8 = rows(tmp_a, HALO, 0)
    return s2, rows(tmp_b, HALO, 0), s8, s8 + rows(tmp_a, HALO, 8)


def _window_sums_ahead(src, tmp_a, tmp_b, tm):
    rows = lambda ref, n, ahead: ref[pl.ds(ahead, n), :]
    tmp_a[pl.ds(0, tm + 24), :] = rows(src, tm + 24, 0) + rows(src, tm + 24, 1)
    tmp_b[pl.ds(0, tm + 16), :] = rows(tmp_a, tm + 16, 0) + rows(tmp_a, tm + 16, 2)
    s2 = rows(tmp_a, tm, 0)
    tmp_a[pl.ds(0, tm + 8), :] = rows(tmp_b, tm + 8, 0) + rows(tmp_b, tm + 8, 4)
    s8 = rows(tmp_a, tm, 0)
    return s2, rows(tmp_b, tm, 0), s8, s8 + rows(tmp_a, tm, 8)


def _pool_counts(tm, t0):
    lane = lax.broadcasted_iota(jnp.int32, (tm, D_POOL), 1)
    row = lax.broadcasted_iota(jnp.int32, (tm, D_POOL), 0) + t0
    cnt = jnp.minimum(row + 1, _lane_sel(lane, 2, 4, 8, 16)).astype(F32)
    return lane, cnt


def _pool_fwd(ext_q, tmp_a, tmp_b, tm, t0):
    lane, cnt = _pool_counts(tm, t0)
    q = ext_q[pl.ds(HALO, tm), :]
    pooled = _lane_sel(lane, *_window_sums_back(ext_q, tmp_a, tmp_b, tm)) / cnt - q
    return pooled, lane, cnt


def _taps(src, w_of, offs, tm, zbuf):
    acc = None
    for r in range(8):
        ks = [k for k in range(len(offs)) if offs[k] % 8 == r]
        if not ks:
            continue
        rows = tm + (8 if r else 0)
        z = w_of(ks[0]) * src[pl.ds(offs[ks[0]] - r, rows), :]
        for k in ks[1:]:
            z = z + w_of(k) * src[pl.ds(offs[k] - r, rows), :]
        if r:
            zbuf[...] = z
            z = zbuf[pl.ds(r, tm), :]
        acc = z if acc is None else acc + z
    return acc


def _tap_grads(d_pad, src, offs, tm, g_ref, zbuf):
    ch = src.shape[-1]
    for r in range(8):
        ks = [k for k in range(len(offs)) if offs[k] % 8 == r]
        if not ks:
            continue
        rows = tm + (8 if r else 0)
        if r:
            zbuf[...] = d_pad[pl.ds(8 - r, rows), :]
        for k in ks:
            d = zbuf[...] if r else d_pad[pl.ds(8, rows), :]
            prod = d * src[pl.ds(offs[k] - r, rows), :]
            g_ref[k] += jnp.sum(prod.reshape(rows // 8, 8, ch), axis=0)


_CONV_OFFS = [HALO - (CONV_K - 1) + k for k in range(CONV_K)]


def _conv_fwd(ext_u, dww_ref, dwb, tm, zbuf):
    return dwb + _taps(ext_u, lambda k: dww_ref[k:k + 1, :], _CONV_OFFS, tm, zbuf)


def _ln_silu(c, lng, lnb):
    mu = jnp.mean(c, axis=-1, keepdims=True)
    cc = c - mu
    rstd = lax.rsqrt(jnp.mean(cc * cc, axis=-1, keepdims=True) + EPS)
    z = cc * rstd
    l = z * lng + lnb
    sl = _sig(l)
    return z, rstd, l, sl, l * sl


def _rg_conv(ext_x, cw_ref, cb, tm):
    xc = cb + cw_ref[0:1, :] * ext_x[pl.ds(HALO - (RG_CONV_K - 1), tm), :]
    for k in range(1, RG_CONV_K):
        xc = xc + cw_ref[k:k + 1, :] * ext_x[pl.ds(HALO - (RG_CONV_K - 1) + k, tm), :]
    return xc


def _rg_gates(xc, wa, ba, wx, bx, lam):
    xcb = xc.astype(BF16)
    r = _sig(_dot(xcb, wa) + ba)
    ig = _sig(_dot(xcb, wx) + bx)
    sp = jnp.maximum(-lam, 0.0) + jnp.log(1.0 + jnp.exp(-jnp.abs(lam)))
    log_a = (-RG_C * r) * sp
    a = jnp.exp(log_a)
    m = jnp.sqrt(_one_minus_sq(a, log_a))
    return xcb, r, ig, sp, a, m


def _scan_rows(a_ref, b_ref, out_ref, carry, tm, reverse):
    rows = lax.broadcasted_iota(jnp.int32, (8, D_RNN), 0)
    ngrp = tm // 8

    def grp(gi, hb):
        st = pl.multiple_of((ngrp - 1 - gi if reverse else gi) * 8, 8)
        a8 = a_ref[pl.ds(st, 8), :]
        b8 = b_ref[pl.ds(st, 8), :]
        out = jnp.zeros((8, D_RNN), F32)
        for j in (range(7, -1, -1) if reverse else range(8)):
            aj = jnp.broadcast_to(a8[j:j + 1, :], (8, D_RNN))
            bj = jnp.broadcast_to(b8[j:j + 1, :], (8, D_RNN))
            if reverse:
                cur = bj + hb
                hb = aj * cur
            else:
                cur = aj * hb + bj
                hb = cur
            out = jnp.where(rows == j, cur, out)
        out_ref[pl.ds(st, 8), :] = out
        return hb

    carry[...] = lax.fori_loop(0, ngrp, grp, carry[...])


_MIX_W = ("wp", "psc", "dww", "dwb", "lng", "lnb", "wpw", "cw", "cb", "wa", "ba", "wx", "bx", "lam")


def _mixer_fwd(h, g, w_in, mw):
    t = h.shape[0]
    tm = _row_tile(t, TM_MIX)

    def body(h_ref, g_ref, win_ref, wp, psc, dww, dwb, lng, lnb, wpw, cw, cb, wa, ba, wx, bx, lam,
             y_ref, p_ref, u_ref, hs_ref, conv_ref, xc_ref, ext_q, ext_u, ext_x, tmp_a, tmp_b, zbuf, a_s, b_s, hcar):
        i = pl.program_id(0)

        @pl.when(i == 0)
        def _():
            ext_q[0:HALO, :] = jnp.zeros((HALO, D_POOL), F32)
            ext_u[0:HALO, :] = jnp.zeros((HALO, D_CONV), F32)
            ext_x[0:HALO, :] = jnp.zeros((HALO, D_RNN), F32)
            hcar[...] = jnp.zeros((8, D_RNN), F32)

        u = (_rms(h_ref[...])[1] * g_ref[...]).astype(BF16)
        u_ref[...] = u
        p_ref[...] = _dot_nt(u, win_ref[...])

        ext_q[pl.ds(HALO, tm), :] = p_ref[:, 0:256]
        pooled, _, _ = _pool_fwd(ext_q, tmp_a, tmp_b, tm, i * tm)
        y_ref[:, 0:256] = (_dot(pooled.astype(BF16), wp[...]) * psc[...]).astype(BF16)

        ext_u[pl.ds(HALO, tm), :] = p_ref[:, 256:512] * _sig(p_ref[:, 512:768])
        conv = _conv_fwd(ext_u, dww, dwb[...], tm, zbuf)
        conv_ref[...] = conv
        act = _ln_silu(conv, lng[...], lnb[...])[4]
        y_ref[:, 256:512] = _dot(act.astype(BF16), wpw[...]).astype(BF16)

        ext_x[pl.ds(HALO, tm), :] = p_ref[:, 1280:1792]
        xc = _rg_conv(ext_x, cw, cb[...], tm)
        xc_ref[...] = xc
        _, _, ig, _, a, m = _rg_gates(xc, wa[...], ba[...], wx[...], bx[...], lam[...])
        a_s[...] = a
        b_s[...] = m * (ig * xc)
        _scan_rows(a_s, b_s, hs_ref, hcar, tm, reverse=False)
        y_ref[:, 512:1024] = (_gelu_and_grad(p_ref[:, 768:1280])[0] * hs_ref[...]).astype(BF16)

        ext_q[0:HALO, :] = ext_q[pl.ds(tm, HALO), :]
        ext_u[0:HALO, :] = ext_u[pl.ds(tm, HALO), :]
        ext_x[0:HALO, :] = ext_x[pl.ds(tm, HALO), :]

    ws = [mw[k] for k in _MIX_W]
    row = lambda w: pl.BlockSpec((tm, w), lambda i: (i, 0))
    return pl.pallas_call(
        body, name="mixer_fwd", grid=(t // tm,),
        in_specs=[row(D_MODEL), _full(g), _resident(w_in)] + [_full(w) for w in ws],
        out_specs=[row(D_MODEL), row(D_IN), row(D_MODEL), row(D_RNN), row(D_CONV), row(D_RNN)],
        out_shape=[jax.ShapeDtypeStruct((t, D_MODEL), BF16), jax.ShapeDtypeStruct((t, D_IN), F32),
                   jax.ShapeDtypeStruct((t, D_MODEL), BF16), jax.ShapeDtypeStruct((t, D_RNN), F32),
                   jax.ShapeDtypeStruct((t, D_CONV), F32), jax.ShapeDtypeStruct((t, D_RNN), F32)],
        scratch_shapes=[pltpu.VMEM((HALO + tm, D_POOL), F32), pltpu.VMEM((HALO + tm, D_CONV), F32),
                        pltpu.VMEM((HALO + tm, D_RNN), F32), pltpu.VMEM((HALO + tm, D_POOL), F32),
                        pltpu.VMEM((HALO + tm, D_POOL), F32), pltpu.VMEM((tm + 8, D_CONV), F32),
                        pltpu.VMEM((tm, D_RNN), F32), pltpu.VMEM((tm, D_RNN), F32), pltpu.VMEM((8, D_RNN), F32)],
        compiler_params=_params("arbitrary"),
    )(h, g, w_in, *ws)


_MIX_G = (("wp", (D_POOL, D_POOL)), ("psc", (1, D_POOL)), ("dww", (32, 8, D_CONV)), ("dwb", (1, D_CONV)),
          ("lng", (1, D_CONV)), ("lnb", (1, D_CONV)), ("wpw", (D_CONV, D_CONV)), ("cw", (8, D_RNN)),
          ("cb", (1, D_RNN)), ("wa", (D_RNN, D_RNN)), ("ba", (1, D_RNN)), ("wx", (D_RNN, D_RNN)),
          ("bx", (1, D_RNN)), ("lam", (1, D_RNN)), ("g1", (1, D_MODEL)))


def _mixer_bwd(p, dh1, hs, conv, xc, h0, g1, w_out, w_in, mw, deps=()):
    t = p.shape[0]
    tm = _row_tile(t, TM_MIX)
    nt = t // tm
    hb = tm // HALO

    def body(p_ref, ph_ref, dh1_ref, hs_ref, hsh_ref, conv_ref, xc_ref, h0_ref, g1_ref, wout_ref, win_ref,
             wp, psc, dww, dwb, lng, lnb, wpw, cw, cb, wa, ba, wx, bx, lam,
             dp_ref, dh0_ref, g_wp, g_psc, g_dww, g_dwb, g_lng, g_lnb, g_wpw, g_cw, g_cb, g_wa, g_ba, g_wx, g_bx, g_lam, g_g1,
             ext_q, ext_u, ext_x, ext_h, ee, dc_s, dx_s, tmp_a, tmp_b, zbuf, d_pad, a_s, b_s, g_s, gcar, dy_ref, dp_s):
        step = pl.program_id(0)
        i = nt - 1 - step
        grads = (g_wp, g_psc, g_dww, g_dwb, g_lng, g_lnb, g_wpw, g_cw, g_cb, g_wa, g_ba, g_wx, g_bx, g_lam, g_g1)
        dy_ref[...] = dh1_ref[...].astype(BF16)
        dy_cols = lambda lo, hi: _dot_nt(dy_ref[...], wout_ref[lo:hi, :])

        @pl.when(step == 0)
        def _():
            for gr in grads:
                gr[...] = jnp.zeros(gr.shape, F32)
            ee[pl.ds(tm, HALO), :] = jnp.zeros((HALO, D_POOL), F32)
            dc_s[pl.ds(tm, HALO), :] = jnp.zeros((HALO, D_CONV), F32)
            dx_s[pl.ds(tm, HALO), :] = jnp.zeros((HALO, D_RNN), F32)
            d_pad[0:8, :] = jnp.zeros((8, D_CONV), F32)
            d_pad[pl.ds(tm + 8, 8), :] = jnp.zeros((8, D_CONV), F32)
            gcar[...] = jnp.zeros((8, D_RNN), F32)

        hm = jnp.where(i == 0, 0.0, 1.0)

        ext_q[0:HALO, :] = ph_ref[:, 0:256] * hm
        ext_q[pl.ds(HALO, tm), :] = p_ref[:, 0:256]
        pooled, lane, cnt = _pool_fwd(ext_q, tmp_a, tmp_b, tm, i * tm)
        pooled_b = pooled.astype(BF16)
        dya = dy_cols(0, 256)
        g_psc[...] += _colsum(dya * _dot(pooled_b, wp[...]))
        dmixed_b = (dya * psc[...]).astype(BF16)
        dpooled = _dot_nt(dmixed_b, wp[...])
        g_wp[...] += _dot_tn(pooled_b, dmixed_b)
        ee[0:tm, :] = dpooled / cnt
        dp_s[:, 0:256] = _lane_sel(lane, *_window_sums_ahead(ee, tmp_a, tmp_b, tm)) - dpooled
        ee[pl.ds(tm, HALO), :] = ee[0:HALO, :]

        v = p_ref[:, 256:512]
        s = _sig(p_ref[:, 512:768])
        ext_u[0:HALO, :] = ph_ref[:, 256:512] * _sig(ph_ref[:, 512:768]) * hm
        ext_u[pl.ds(HALO, tm), :] = v * s
        z, rstd, l, sl, act = _ln_silu(conv_ref[...], lng[...], lnb[...])
        dyb_b = dy_cols(256, 512).astype(BF16)
        dact = _dot_nt(dyb_b, wpw[...])
        g_wpw[...] += _dot_tn(act.astype(BF16), dyb_b)
        dl = dact * (sl * (1.0 + l * (1.0 - sl)))
        g_lng[...] += _colsum(dl * z)
        g_lnb[...] += _colsum(dl)
        dz = dl * lng[...]
        dc = rstd * (dz - jnp.mean(dz, axis=-1, keepdims=True) - z * jnp.mean(dz * z, axis=-1, keepdims=True))
        g_dwb[...] += _colsum(dc)
        dc_s[0:tm, :] = dc
        d_pad[pl.ds(8, tm), :] = dc
        _tap_grads(d_pad, ext_u, _CONV_OFFS, tm, g_dww, zbuf)
        du0 = _taps(dc_s, lambda j: dww[CONV_K - 1 - j:CONV_K - j, :], list(range(CONV_K)), tm, zbuf)
        dp_s[:, 256:512] = du0 * s
        dp_s[:, 512:768] = du0 * v * (s * (1.0 - s))
        dc_s[pl.ds(tm, HALO), :] = dc_s[0:HALO, :]

        ext_x[0:HALO, :] = ph_ref[:, 1280:1792] * hm
        ext_x[pl.ds(HALO, tm), :] = p_ref[:, 1280:1792]
        xc = xc_ref[...]
        xcb, r, ig, sp, a, m = _rg_gates(xc, wa[...], ba[...], wx[...], bx[...], lam[...])
        ext_h[0:HALO, :] = hsh_ref[...] * hm
        ext_h[pl.ds(HALO, tm), :] = hs_ref[...]
        dyc = dy_cols(512, 1024)
        gl, dgl = _gelu_and_grad(p_ref[:, 768:1280])
        dp_s[:, 768:1280] = dyc * hs_ref[...] * dgl
        a_s[...] = a
        b_s[...] = dyc * gl
        _scan_rows(a_s, b_s, g_s, gcar, tm, reverse=True)
        g = g_s[...]
        da = g * ext_h[pl.ds(HALO - 1, tm), :]
        dm = g * (ig * xc)
        dig = g * (m * xc)
        dlog_a = da * a - dm * (a * a) / m
        g_lam[...] += _colsum(dlog_a * (-RG_C * r)) * (-_sig(-lam[...]))
        dra = (dlog_a * (-RG_C * sp)) * (r * (1.0 - r))
        dia = dig * (ig * (1.0 - ig))
        g_ba[...] += _colsum(dra)
        g_bx[...] += _colsum(dia)
        dra_b = dra.astype(BF16)
        dia_b = dia.astype(BF16)
        dxc = g * (m * ig) + _dot_nt(dra_b, wa[...]) + _dot_nt(dia_b, wx[...])
        g_wa[...] += _dot_tn(xcb, dra_b)
        g_wx[...] += _dot_tn(xcb, dia_b)
        g_cb[...] += _colsum(dxc)
        dx_s[0:tm, :] = dxc
        for k in range(RG_CONV_K):
            g_cw[k:k + 1, :] += _colsum(dxc * ext_x[pl.ds(HALO - (RG_CONV_K - 1) + k, tm), :])
        dxin = cw[RG_CONV_K - 1:RG_CONV_K, :] * dxc
        for j in range(1, RG_CONV_K):
            dxin = dxin + cw[RG_CONV_K - 1 - j:RG_CONV_K - j, :] * dx_s[pl.ds(j, tm), :]
        dp_s[:, 1280:1792] = dxin
        dx_s[pl.ds(tm, HALO), :] = dx_s[0:HALO, :]

        dpb = dp_s[...].astype(BF16)
        dp_ref[...] = dpb
        du = _dot(dpb, win_ref[...])
        r, n = _rms(h0_ref[...])
        g_g1[...] += _colsum(du * n)
        dh0_ref[...] = dh1_ref[...] + _rms_bwd(du, n, r, g1_ref[...])

    ws = [mw[k] for k in _MIX_W]
    tile = lambda w: pl.BlockSpec((tm, w), lambda s: (nt - 1 - s, 0))
    halo = lambda w: pl.BlockSpec((HALO, w), lambda s: (jnp.maximum((nt - 1 - s) * hb - 1, 0), 0))
    outs = pl.pallas_call(
        _after(body, 11 + len(ws), deps), name="mixer_bwd", grid=(nt,),
        in_specs=[tile(D_IN), halo(D_IN), tile(D_MODEL), tile(D_RNN), halo(D_RNN), tile(D_CONV), tile(D_RNN), tile(D_MODEL), _full(g1),
                  _resident(w_out), _resident(w_in)] + [_full(w) for w in ws] + [ANY] * len(deps),
        out_specs=[tile(D_IN), tile(D_MODEL)] + [pl.BlockSpec(shp, lambda s, nd=len(shp): (0,) * nd) for _, shp in _MIX_G],
        out_shape=[jax.ShapeDtypeStruct((t, D_IN), BF16), jax.ShapeDtypeStruct((t, D_MODEL), F32)]
        + [jax.ShapeDtypeStruct(shp, F32) for _, shp in _MIX_G],
        scratch_shapes=[pltpu.VMEM((HALO + tm, D_POOL), F32), pltpu.VMEM((HALO + tm, D_CONV), F32),
                        pltpu.VMEM((HALO + tm, D_RNN), F32), pltpu.VMEM((HALO + tm, D_RNN), F32),
                        pltpu.VMEM((tm + HALO, D_POOL), F32), pltpu.VMEM((tm + HALO, D_CONV), F32),
                        pltpu.VMEM((tm + HALO, D_RNN), F32), pltpu.VMEM((HALO + tm, D_POOL), F32),
                        pltpu.VMEM((HALO + tm, D_POOL), F32), pltpu.VMEM((tm + 8, D_CONV), F32),
                        pltpu.VMEM((tm + 16, D_CONV), F32), pltpu.VMEM((tm, D_RNN), F32),
                        pltpu.VMEM((tm, D_RNN), F32), pltpu.VMEM((tm, D_RNN), F32), pltpu.VMEM((8, D_RNN), F32),
                        pltpu.VMEM((tm, D_MODEL), BF16), pltpu.VMEM((tm, D_IN), F32)],
        compiler_params=_params("arbitrary"),
    )(p, p, dh1, hs, hs, conv, xc, h0, g1, w_out, w_in, *ws, *deps)
    return outs[0], outs[1], {k: o for (k, _), o in zip(_MIX_G, outs[2:])}


def _mid_fwd(y, h0, w_out, g, w_up):
    t = h0.shape[0]
    tm = _row_tile(t, TM_MAT)

    def body(y_ref, h0_ref, wo_ref, g_ref, wu_ref, h1_ref, u2_ref, f_ref):
        h1 = h0_ref[...] + _dot(y_ref[...], wo_ref[...])
        h1_ref[...] = h1
        u2 = (_rms(h1)[1] * g_ref[...]).astype(BF16)
        u2_ref[...] = u2
        for c in range(D_FF // FF_CHUNK):
            f_ref[:, c * FF_CHUNK:(c + 1) * FF_CHUNK] = _dot(u2, wu_ref[c]).astype(BF16)

    row = lambda w: pl.BlockSpec((tm, w), lambda i: (i, 0))
    return pl.pallas_call(
        body, name="mid_fwd", grid=(t // tm,),
        in_specs=[row(D_MODEL), row(D_MODEL), _resident(w_out), _full(g), _resident(w_up)],
        out_specs=[row(D_MODEL), row(D_MODEL), row(D_FF)],
        out_shape=[jax.ShapeDtypeStruct((t, D_MODEL), F32), jax.ShapeDtypeStruct((t, D_MODEL), BF16),
                   jax.ShapeDtypeStruct((t, D_FF), BF16)],
        compiler_params=_params("parallel"),
    )(y, h0, w_out, g, w_up)


def _down_proj(f_ref, h1_ref, wd_ref):
    acc = h1_ref[...]
    for c in range(D_FF // FF_CHUNK):
        cols = slice(c * FF_CHUNK, (c + 1) * FF_CHUNK)
        a = jnp.square(jnp.maximum(f_ref[:, cols].astype(F32), 0.0)).astype(BF16)
        acc = acc + _dot(a, wd_ref[cols, :])
    return acc


def _down_fwd(f, h1, w_down):
    t = h1.shape[0]
    tm = _row_tile(t, TM_MAT)

    def body(f_ref, h1_ref, wd_ref, h2_ref):
        h2_ref[...] = _down_proj(f_ref, h1_ref, wd_ref)

    row = lambda w: pl.BlockSpec((tm, w), lambda i: (i, 0))
    return pl.pallas_call(
        body, name="down_fwd", grid=(t // tm,),
        in_specs=[row(D_FF), row(D_MODEL), _resident(w_down)], out_specs=row(D_MODEL),
        out_shape=jax.ShapeDtypeStruct((t, D_MODEL), F32),
        compiler_params=_params("parallel"),
    )(f, h1, w_down)


def _down_fwd_loss(f, h1, w_down, g, tgt, t_real):
    t = h1.shape[0]
    tm = _row_tile(t, TM_MAT)

    def body(f_ref, h1_ref, wd_ref, g_ref, tgt_ref, loss_ref, dh_ref, dg_ref):
        i = pl.program_id(0)

        @pl.when(i == 0)
        def _():
            loss_ref[...] = jnp.zeros(loss_ref.shape, F32)
            dg_ref[...] = jnp.zeros(dg_ref.shape, F32)

        r, n = _rms(_down_proj(f_ref, h1_ref, wd_ref))
        row = lax.broadcasted_iota(jnp.int32, (tm, 1), 0) + i * tm
        valid = jnp.logical_and(row >= N_META, row < t_real)
        diff = jnp.where(valid, n * g_ref[...] - tgt_ref[...], 0.0)
        loss_ref[...] += 0.5 * jnp.sum(jnp.mean(diff * diff, axis=-1, keepdims=True))
        dy = diff * (1.0 / D_MODEL)
        dg_ref[...] += _colsum(dy * n)
        dh_ref[...] = _rms_bwd(dy, n, r, g_ref[...])

    row = lambda w: pl.BlockSpec((tm, w), lambda i: (i, 0))
    return pl.pallas_call(
        body, name="down_fwd_loss", grid=(t // tm,),
        in_specs=[row(D_FF), row(D_MODEL), _resident(w_down), _full(g), row(D_MODEL)],
        out_specs=[pl.BlockSpec((8, 128), lambda i: (0, 0)), row(D_MODEL), pl.BlockSpec((1, D_MODEL), lambda i: (0, 0))],
        out_shape=[jax.ShapeDtypeStruct((8, 128), F32), jax.ShapeDtypeStruct((t, D_MODEL), F32),
                   jax.ShapeDtypeStruct((1, D_MODEL), F32)],
        compiler_params=_params("arbitrary"),
    )(f, h1, w_down, g, tgt)


def _mlp_bwd(dh2, f, h1, g, w_up, w_down, deps=()):
    t = dh2.shape[0]
    tm = _row_tile(t, TM_MLP_BWD)

    def body(dh2_ref, f_ref, wd_ref, wu_ref, h1_ref, g_ref, df_ref, dh1_ref, dg_ref):
        @pl.when(pl.program_id(0) == 0)
        def _():
            dg_ref[...] = jnp.zeros(dg_ref.shape, F32)

        dh2 = dh2_ref[...]
        dhb = dh2.astype(BF16)
        du2 = None
        for c in range(D_FF // FF_CHUNK):
            cols = slice(c * FF_CHUNK, (c + 1) * FF_CHUNK)
            dact = _dot_nt(dhb, wd_ref[c])
            df = (dact * (2.0 * jnp.maximum(f_ref[:, cols].astype(F32), 0.0))).astype(BF16)
            df_ref[:, cols] = df
            part = _dot_nt(df, wu_ref[c])
            du2 = part if du2 is None else du2 + part
        r, n = _rms(h1_ref[...])
        dg_ref[...] += _colsum(du2 * n)
        dh1_ref[...] = dh2 + _rms_bwd(du2, n, r, g_ref[...])

    row = lambda w: pl.BlockSpec((tm, w), lambda i: (i, 0))
    return pl.pallas_call(
        _after(body, 6, deps), name="mlp_bwd", grid=(t // tm,),
        in_specs=[row(D_MODEL), row(D_FF), _resident(w_down), _resident(w_up), row(D_MODEL), _full(g)] + [ANY] * len(deps),
        out_specs=[row(D_FF), row(D_MODEL), pl.BlockSpec((1, D_MODEL), lambda i: (0, 0))],
        out_shape=[jax.ShapeDtypeStruct((t, D_FF), BF16), jax.ShapeDtypeStruct((t, D_MODEL), F32),
                   jax.ShapeDtypeStruct((1, D_MODEL), F32)],
        compiler_params=_params("arbitrary"),
    )(dh2, f, w_down, w_up, h1, g, *deps)


def _tn_matmul(a, b, kc, nc, relu2, name, deps=()):
    t, k = a.shape
    n = b.shape[1]
    tt = _row_tile(t, TM_MAT)
    gk, gn = k // kc, n // nc

    def body(a_ref, b_ref, o_ref):
        @pl.when(pl.program_id(2) == 0)
        def _():
            o_ref[...] = jnp.zeros(o_ref.shape, F32)

        av = a_ref[...]
        if relu2:
            av = jnp.square(jnp.maximum(av.astype(F32), 0.0))
        o_ref[...] += _dot_tn(av.astype(BF16), b_ref[...].astype(BF16))

    return pl.pallas_call(
        _after(body, 2, deps), name=name, grid=(gk, gn, t // tt),
        in_specs=[pl.BlockSpec((tt, kc), lambda ik, jn, it: (it, ik)), pl.BlockSpec((tt, nc), lambda ik, jn, it: (it, jn))]
        + [ANY] * len(deps),
        out_specs=pl.BlockSpec((None, kc, nc), lambda ik, jn, it: (ik * gn + jn, 0, 0)),
        out_shape=jax.ShapeDtypeStruct((gk * gn, kc, nc), F32),
        compiler_params=_params("parallel", "parallel", "arbitrary"),
    )(a, b, *deps)


def _block_diag(blocks):
    nb, hd, _ = blocks.shape
    eye = jnp.eye(nb, dtype=blocks.dtype)
    return (blocks[:, :, None, :] * eye[:, None, :, None]).reshape(nb * hd, nb * hd)


def _diag_blocks(m, nb):
    hd = m.shape[0] // nb
    eye = jnp.eye(nb, dtype=m.dtype)
    return jnp.sum(m.reshape(nb, hd, nb, hd) * eye[:, None, :, None], axis=2)


def _mixer_weights(w, l):
    row = lambda a: a.reshape(1, -1)
    return dict(
        wp=_block_diag(w["pool_w"][l]).astype(BF16), psc=row(w["pool_scale"][l]),
        dww=jnp.pad(w["convb_dw_w"][l], ((0, 32 - CONV_K), (0, 0))), dwb=row(w["convb_dw_b"][l]),
        lng=row(w["convb_ln_g"][l]), lnb=row(w["convb_ln_b"][l]), wpw=w["convb_pw_w"][l].astype(BF16),
        cw=jnp.pad(w["rg_conv_w"][l], ((0, 8 - RG_CONV_K), (0, 0))), cb=row(w["rg_conv_b"][l]),
        wa=_block_diag(w["rg_w_a"][l]).astype(BF16), ba=row(w["rg_b_a"][l]),
        wx=_block_diag(w["rg_w_x"][l]).astype(BF16), bx=row(w["rg_b_x"][l]), lam=row(w["rg_lambda"][l]))


def _local_step(h, tgt, t_real, w, fetch, hooks):
    depth = 2
    saved = []
    big = []
    for l in range(depth):
        mw = _mixer_weights(w, l)
        g1 = w["mix_norm_g"][l].reshape(1, -1)
        g2 = w["mlp_norm_g"][l].reshape(1, -1)
        wl = dict(w_in=fetch(l, "w_in", h))
        y, p, u, hs, conv, xc = _mixer_fwd(h, g1, wl["w_in"], mw)
        wl["w_out"], wl["w_up"] = fetch(l, "w_out", y), fetch(l, "w_up", y)
        h1, u2, f = _mid_fwd(y, h, wl["w_out"], g2, wl["w_up"])
        wl["w_down"] = fetch(l, "w_down", f)
        if l == depth - 1:
            loss, dh, dgf = _down_fwd_loss(f, h1, wl["w_down"].reshape(D_FF, D_MODEL), w["final_norm_g"].reshape(1, -1), tgt,
                                           t_real)
            h2 = None
        else:
            h2 = _down_fwd(f, h1, wl["w_down"].reshape(D_FF, D_MODEL))
        saved.append(dict(mw=mw, g1=g1, g2=g2, h0=h, p=p, u=u, y=y, hs=hs, conv=conv, xc=xc, h1=h1, u2=u2, f=f))
        big.append(wl)
        h = h2

    gs = {k: [None] * depth for k in ("mix_norm_g", "mlp_norm_g", "pool_w", "pool_scale", "convb_dw_w", "convb_dw_b",
                                      "convb_ln_g", "convb_ln_b", "convb_pw_w", "rg_conv_w", "rg_conv_b", "rg_w_a",
                                      "rg_b_a", "rg_w_x", "rg_b_x", "rg_lambda")}
    deps = ()
    for l in reversed(range(depth)):
        s, wl = saved[l], big[l]
        df, dh1, dg2 = _mlp_bwd(dh, s["f"], s["h1"], s["g2"], wl["w_up"], wl["w_down"], deps)
        deps = hooks.point(l, "mlp_bwd", dh1)
        g_down = _tn_matmul(s["f"], dh, FF_CHUNK, D_MODEL, True, "dw_down", deps)
        hooks.grad(l, "w_down", g_down)
        deps = hooks.point(l, "dw_down", g_down)
        g_up = _tn_matmul(s["u2"], df, D_MODEL, FF_CHUNK, False, "dw_up", deps)
        hooks.grad(l, "w_up", g_up)
        deps = hooks.point(l, "dw_up", g_up)
        g_out = _tn_matmul(s["y"], dh1, D_MODEL, D_MODEL, False, "dw_out", deps)
        hooks.grad(l, "w_out", g_out.reshape(N_CHIPS, D_MODEL // N_CHIPS, D_MODEL))
        deps = hooks.point(l, "dw_out", g_out)
        dp, dh, mg = _mixer_bwd(s["p"], dh1, s["hs"], s["conv"], s["xc"], s["h0"], s["g1"], wl["w_out"], wl["w_in"], s["mw"],
                                deps)
        deps = hooks.point(l, "mixer_bwd", dh)
        g_in = _tn_matmul(dp, s["u"], D_IN, D_MODEL, False, "dw_in", deps).reshape(N_CHIPS, D_IN // N_CHIPS, D_MODEL)
        hooks.grad(l, "w_in", g_in)
        deps = hooks.point(l, "dw_in", g_in)
        gs["mix_norm_g"][l] = mg["g1"][0]
        gs["mlp_norm_g"][l] = dg2[0]
        gs["pool_w"][l] = _diag_blocks(mg["wp"], D_POOL // POOL_GW)
        gs["pool_scale"][l] = mg["psc"][0]
        gs["convb_dw_w"][l] = jnp.sum(mg["dww"][:CONV_K], axis=1)
        gs["convb_dw_b"][l] = mg["dwb"][0]
        gs["convb_ln_g"][l] = mg["lng"][0]
        gs["convb_ln_b"][l] = mg["lnb"][0]
        gs["convb_pw_w"][l] = mg["wpw"]
        gs["rg_conv_w"][l] = mg["cw"][:RG_CONV_K]
        gs["rg_conv_b"][l] = mg["cb"][0]
        gs["rg_w_a"][l] = _diag_blocks(mg["wa"], D_RNN // RG_HD)
        gs["rg_b_a"][l] = mg["ba"][0]
        gs["rg_w_x"][l] = _diag_blocks(mg["wx"], D_RNN // RG_HD)
        gs["rg_b_x"][l] = mg["bx"][0]
        gs["rg_lambda"][l] = mg["lam"][0]
    gsmall = {k: jnp.stack(v) for k, v in gs.items()}
    gsmall["final_norm_g"] = dgf[0]
    return loss[0, 0], dh, gsmall


def _place():
    return lax.axis_index("x"), lax.axis_index("y"), lax.axis_index("c")


def _other_chips(x, y):
    return [(1 - x, y), (x, 1 - y), (1 - x, 1 - y)]


HBM_SPEC = pl.BlockSpec(memory_space=pltpu.HBM)
SEM_SPEC = pl.BlockSpec(memory_space=pltpu.SEMAPHORE)
DATAFLOW = pltpu.SideEffectType.DATAFLOW_SIDE_EFFECTING


def _gather_copies(src_refs, land_refs, send_sem, recv_sem, first):
    x, y, c = _place()
    me = 2 * x + y
    out = []
    for n in range(len(src_refs)):
        for j, (px, py) in enumerate(_other_chips(x, y)):
            out.append(pltpu.make_async_remote_copy(src_refs[n], land_refs[n].at[me], send_sem.at[first + 3 * n + j],
                                                    recv_sem.at[first + 3 * n + j], device_id=(px, py, c), device_id_type=MESH))
    return out


def _gather_start(groups, me):
    srcs = [pltpu.with_memory_space_constraint(s, pltpu.HBM) for g in groups for s in g]
    lands = [pltpu.with_memory_space_constraint(
        lax.dynamic_update_slice(jnp.zeros((N_CHIPS,) + s.shape, s.dtype), s[None], (me,) + (0,) * s.ndim), pltpu.HBM)
        for g in groups for s in g]
    n, ng = len(srcs), len(groups)
    first = [sum(len(g) for g in groups[:i]) for i in range(ng)]

    def body(*refs):
        src_refs, land_refs = refs[:n], refs[n:2 * n]
        sems = refs[2 * n:2 * n + 2 * ng]
        token = refs[-1]
        for gi, g in enumerate(groups):
            lo, hi = first[gi], first[gi] + len(g)
            for cp in _gather_copies(src_refs[lo:hi], land_refs[lo:hi], sems[2 * gi], sems[2 * gi + 1], 0):
                cp.start()
        token[...] = jnp.zeros(token.shape, token.dtype)

    sem_shapes = [pltpu.SemaphoreType.DMA((3 * len(g),)) for g in groups for _ in range(2)]
    outs = pl.pallas_call(
        body, name="gather_start",
        out_shape=sem_shapes + [pltpu.HBM(a.shape, a.dtype) for a in srcs + lands] + [jax.ShapeDtypeStruct((8, 128), F32)],
        in_specs=[HBM_SPEC] * (2 * n),
        out_specs=[SEM_SPEC] * (2 * ng) + [HBM_SPEC] * (2 * n) + [pl.BlockSpec(memory_space=pltpu.VMEM)],
        input_output_aliases={i: 2 * ng + i for i in range(2 * n)},
        compiler_params=pltpu.CompilerParams(has_side_effects=DATAFLOW),
    )(*srcs, *lands)
    sems, thru, token = outs[:2 * ng], outs[2 * ng:2 * ng + 2 * n], outs[-1]
    state = []
    for gi, g in enumerate(groups):
        lo, hi = first[gi], first[gi] + len(g)
        state.append((sems[2 * gi], sems[2 * gi + 1], thru[lo:hi], thru[n + lo:n + hi]))
    return state, token


def _gather_wait(state, after, name):
    send_sem, recv_sem, srcs, lands = state
    n = len(srcs)

    def body(*refs):
        src_refs, land_refs = refs[:n], refs[n:2 * n]
        send, recv = refs[2 * n], refs[2 * n + 1]
        for cp in _gather_copies(src_refs, land_refs, send, recv, 0):
            cp.wait_send()
            cp.wait_recv()

    outs = pl.pallas_call(
        body, name=name,
        out_shape=[pltpu.HBM(a.shape, a.dtype) for a in list(srcs) + list(lands)],
        in_specs=[HBM_SPEC] * (2 * n) + [SEM_SPEC, SEM_SPEC, ANY],
        out_specs=[HBM_SPEC] * (2 * n),
        input_output_aliases={i: i for i in range(2 * n)},
        compiler_params=pltpu.CompilerParams(has_side_effects=DATAFLOW),
    )(*srcs, *lands, send_sem, recv_sem, after)
    return outs[n:]


def _add_halves(g, recv, c1):
    nk, r, cd = g.shape
    r2 = r // 2

    def body(c_ref, g_ref, r_ref, pab_ref):
        pab_ref[...] = (g_ref[...] + r_ref[...]).astype(BF16)

    blk = pl.BlockSpec((None, r2, cd), lambda k, c_ref: (k, 0, 0))
    return pl.pallas_call(
        body, name="rs_add_halves",
        grid_spec=pltpu.PrefetchScalarGridSpec(
            num_scalar_prefetch=1, grid=(nk,),
            in_specs=[pl.BlockSpec((None, r2, cd), lambda k, c_ref: (k, c_ref[0], 0)), blk], out_specs=blk),
        out_shape=jax.ShapeDtypeStruct((nk, r2, cd), BF16),
        compiler_params=_params("parallel"),
    )(c1, g, recv)


def _sum_partials(g, recv_sibling, recv_chips, c_me):
    nk, r, cd = g.shape
    r2 = r // 2

    def body(cm_ref, g_ref, a_ref, r_ref, s_ref):
        own = g_ref[...] + a_ref[...]
        s_ref[...] = ((own + r_ref[0].astype(F32)) + r_ref[1].astype(F32)) + r_ref[2].astype(F32)

    return pl.pallas_call(
        body, name="rs_sum_partials",
        grid_spec=pltpu.PrefetchScalarGridSpec(
            num_scalar_prefetch=1, grid=(1,),
            in_specs=[pl.BlockSpec((None, r2, cd), lambda i, cm: (cm[1], cm[0], 0)),
                      pl.BlockSpec((None, r2, cd), lambda i, cm: (cm[1], 0, 0)),
                      pl.BlockSpec((3, r2, cd), lambda i, cm: (0, 0, 0))],
            out_specs=pl.BlockSpec((r2, cd), lambda i, cm: (0, 0))),
        out_shape=jax.ShapeDtypeStruct((r2, cd), F32),
        compiler_params=_params("arbitrary"),
    )(c_me, g, recv_sibling, recv_chips)


def _split_start(name, srcs, lands, ncopies, make_copies):
    srcs = [pltpu.with_memory_space_constraint(s, pltpu.HBM) for s in srcs]
    lands = [pltpu.with_memory_space_constraint(a, pltpu.HBM) for a in lands]
    n, m = len(srcs), len(lands)

    def body(*refs):
        src_refs, land_refs = refs[:n], refs[n:n + m]
        send, recv, token = refs[n + m], refs[n + m + 1], refs[-1]
        for cp in make_copies(src_refs, land_refs, send, recv):
            cp.start()
        token[...] = jnp.zeros(token.shape, token.dtype)

    outs = pl.pallas_call(
        body, name=name,
        out_shape=[pltpu.SemaphoreType.DMA((ncopies,)), pltpu.SemaphoreType.DMA((ncopies,))]
        + [pltpu.HBM(a.shape, a.dtype) for a in srcs + lands] + [jax.ShapeDtypeStruct((8, 128), F32)],
        in_specs=[HBM_SPEC] * (n + m),
        out_specs=[SEM_SPEC, SEM_SPEC] + [HBM_SPEC] * (n + m) + [pl.BlockSpec(memory_space=pltpu.VMEM)],
        input_output_aliases={i: 2 + i for i in range(n + m)},
        compiler_params=pltpu.CompilerParams(has_side_effects=DATAFLOW),
    )(*srcs, *lands)
    return (outs[0], outs[1], outs[2:2 + n], outs[2 + n:2 + n + m], make_copies), outs[-1]


def _split_wait(name, state, after):
    send_sem, recv_sem, srcs, lands, make_copies = state
    n, m = len(srcs), len(lands)

    def body(*refs):
        src_refs, land_refs = refs[:n], refs[n:n + m]
        for cp in make_copies(src_refs, land_refs, refs[n + m], refs[n + m + 1]):
            cp.wait_send()
            cp.wait_recv()

    outs = pl.pallas_call(
        body, name=name,
        out_shape=[pltpu.HBM(a.shape, a.dtype) for a in list(srcs) + list(lands)],
        in_specs=[HBM_SPEC] * (n + m) + [SEM_SPEC, SEM_SPEC, ANY],
        out_specs=[HBM_SPEC] * (n + m),
        input_output_aliases={i: i for i in range(n + m)},
        compiler_params=pltpu.CompilerParams(has_side_effects=DATAFLOW),
    )(*srcs, *lands, send_sem, recv_sem, after)
    return outs[:n], outs[n:]


def _copies_to_sibling(src_of):
    def make(src_refs, land_refs, send, recv):
        x, y, c = _place()
        return [pltpu.make_async_remote_copy(src_of(src_refs[i], c), land_refs[i], send.at[i], recv.at[i],
                                             device_id=(x, y, 1 - c), device_id_type=MESH) for i in range(len(src_refs))]
    return make


def _copies_to_chips(src_refs, land_refs, send, recv):
    x, y, c = _place()
    return [pltpu.make_async_remote_copy(src_refs[i].at[2 * px + py], land_refs[i].at[j], send.at[3 * i + j], recv.at[3 * i + j],
                                         device_id=(px, py, c), device_id_type=MESH)
            for i in range(len(src_refs)) for j, (px, py) in enumerate(_other_chips(x, y))]


def _other_half_rows(ref, c):
    r2 = ref.shape[1] // 2
    return ref.at[:, pl.ds(pl.multiple_of((1 - c) * r2, 8), r2)]


class _ReduceScatter:
    def __init__(self, tag, grads, c1, me1):
        self.tag, self.grads, self.c1, self.me1 = tag, grads, c1, me1

    def start(self):
        lands = [lax.empty((g.shape[0], g.shape[1] // 2, g.shape[2]), F32) for g in self.grads]
        self.state, token = _split_start("rs_%s_a_start" % self.tag, self.grads, lands, len(self.grads),
                                         _copies_to_sibling(_other_half_rows))
        return token

    def to_chips(self, after):
        self.halves = _split_wait("rs_%s_a_wait" % self.tag, self.state, after)
        pabs = [_add_halves(g, r, self.c1) for g, r in zip(*self.halves)]
        lands = [lax.empty((3,) + p.shape[1:], BF16) for p in pabs]
        self.state, token = _split_start("rs_%s_b_start" % self.tag, pabs, lands, 3 * len(pabs), _copies_to_chips)
        return token

    def to_sibling(self, after):
        _, recv = _split_wait("rs_%s_b_wait" % self.tag, self.state, after)
        c_me = jnp.concatenate([self.c1, self.me1])
        sums = [_sum_partials(g, ra, rb, c_me) for g, ra, rb in zip(*self.halves, recv)]
        lands = [lax.empty(s.shape, F32) for s in sums]
        self.state, token = _split_start("rs_%s_c_start" % self.tag, sums, lands, len(sums),
                                         _copies_to_sibling(lambda ref, c: ref))
        return token

    def finish(self, after):
        return list(zip(*_split_wait("rs_%s_c_wait" % self.tag, self.state, after)))


def _allreduce_small(vs):
    n = len(vs)

    def body(*refs):
        v_refs, out_refs, rbufs = refs[:n], refs[n:2 * n], refs[2 * n:3 * n]
        send_sems, recv_sems = refs[3 * n:]
        x, y, c = _place()
        for i in range(n):
            out_refs[i][...] = v_refs[i][...]
        for s, peer in enumerate([(x, y, 1 - c), (1 - x, y, c), (x, 1 - y, c)]):
            copies = [pltpu.make_async_remote_copy(out_refs[i], rbufs[i].at[s], send_sems.at[s * n + i], recv_sems.at[s * n + i],
                                                   device_id=peer, device_id_type=MESH) for i in range(n)]
            for cp in copies:
                cp.start()
            for cp in copies:
                cp.wait()
            for i in range(n):
                out_refs[i][...] = out_refs[i][...] + rbufs[i][s]

    vm = pl.BlockSpec(memory_space=pltpu.VMEM)
    return pl.pallas_call(
        body, name="allreduce_small", in_specs=[vm] * n, out_specs=[vm] * n,
        out_shape=[jax.ShapeDtypeStruct(v.shape, v.dtype) for v in vs],
        scratch_shapes=[pltpu.VMEM((3,) + v.shape, v.dtype) for v in vs]
        + [pltpu.SemaphoreType.DMA((3 * n,)), pltpu.SemaphoreType.DMA((3 * n,))],
        compiler_params=pltpu.CompilerParams(vmem_limit_bytes=VMEM_LIMIT),
    )(*vs)


def _adamw_math(w, g, m, v):
    m = ADAM_B1 * m + (1.0 - ADAM_B1) * g
    v = ADAM_B2 * v + (1.0 - ADAM_B2) * jnp.square(g)
    m_hat = m / (1.0 - ADAM_B1 ** ADAM_STEP)
    v_hat = v / (1.0 - ADAM_B2 ** ADAM_STEP)
    return -ADAM_LR * (m_hat / (jnp.sqrt(v_hat) + ADAM_EPS) + ADAM_WD * w), m, v


def _adamw_big_layer(layer, w, m, v, own, sib, c1, prev):
    _, r, cd = w.shape
    r2 = r // 2

    def body(c_ref, w_ref, m_ref, v_ref, own_ref, sib_ref, *rest):
        g_ref, d_ref, mo_ref, vo_ref, token = rest[-5:]
        g = jnp.where(pl.program_id(0) == c_ref[0], own_ref[...], sib_ref[...])
        g_ref[...] = g
        d_ref[...], mo_ref[...], vo_ref[...] = _adamw_math(w_ref[...], g, m_ref[...], v_ref[...])
        token[...] = jnp.zeros(token.shape, F32)

    blk = pl.BlockSpec((None, r2, cd), lambda hh, c_ref: (layer, hh, 0))
    half = pl.BlockSpec((r2, cd), lambda hh, c_ref: (0, 0))
    prev = () if prev is None else tuple(prev)
    outs = pl.pallas_call(
        body, name="adamw_big",
        grid_spec=pltpu.PrefetchScalarGridSpec(
            num_scalar_prefetch=1, grid=(2,), in_specs=[blk, blk, blk, half, half] + [ANY] * len(prev),
            out_specs=[blk] * 4 + [pl.BlockSpec((8, 128), lambda hh, c_ref: (0, 0))]),
        out_shape=[jax.ShapeDtypeStruct(w.shape, F32)] * 4 + [jax.ShapeDtypeStruct((8, 128), F32)],
        input_output_aliases={6 + i: i for i in range(len(prev))},
        compiler_params=_params("arbitrary"),
    )(c1, w, m, v, own, sib, *prev)
    return outs[:4], outs[4]


def _adamw_small(ws, gs, ms, vs):
    n = len(ws)

    def body(*refs):
        w_refs, g_refs, m_refs, v_refs = refs[:n], refs[n:2 * n], refs[2 * n:3 * n], refs[3 * n:4 * n]
        outs = refs[4 * n:]
        for i in range(n):
            outs[3 * i][...], outs[3 * i + 1][...], outs[3 * i + 2][...] = _adamw_math(
                w_refs[i][...], g_refs[i][...], m_refs[i][...], v_refs[i][...])

    vm = pl.BlockSpec(memory_space=pltpu.VMEM)
    outs = pl.pallas_call(
        body, name="adamw_small", in_specs=[vm] * (4 * n), out_specs=[vm] * (3 * n),
        out_shape=[jax.ShapeDtypeStruct(w.shape, F32) for w in ws for _ in range(3)],
        compiler_params=pltpu.CompilerParams(vmem_limit_bytes=VMEM_LIMIT),
    )(*ws, *gs, *ms, *vs)
    return [outs[3 * i:3 * i + 3] for i in range(n)]


LANES = 128
SUBLANES = 8
SHARDED_AXIS = {"meta_tokens": 1, "convb_dw_w": 2, "convb_pw_w": 1, "rg_conv_w": 2}


def _rows_of(size):
    return -(-size // (LANES * SUBLANES)) * SUBLANES


def _as_rows(a, rows=None):
    flat = a.reshape(-1)
    rows = _rows_of(flat.size) if rows is None else rows
    return jnp.pad(flat, (0, rows * LANES - flat.size)).reshape(rows, LANES)


class _GradientSchedule:
    GROUPS = {"l1": [(1, "w_down"), (1, "w_up"), (1, "w_out"), (1, "w_in")], "a0": [(0, "w_down"), (0, "w_up")],
              "b0": [(0, "w_out")], "c0": [(0, "w_in")]}
    PLAN = {
        (1, "dw_in"): [("l1", "start")],
        (0, "mlp_bwd"): [("l1", "to_chips")],
        (0, "dw_up"): [("l1", "to_sibling"), ("a0", "start")],
        (0, "dw_out"): [("l1", "finish"), ("a0", "to_chips"), ("b0", "start")],
        (0, "mixer_bwd"): [("a0", "to_sibling"), ("b0", "to_chips")],
        (0, "dw_in"): [("c0", "start"), ("a0", "finish"), ("b0", "to_sibling"), ("c0", "to_chips")],
    }

    def __init__(self, w, mom, var, c1, me1):
        self.w, self.mom, self.var, self.c1, self.me1 = w, mom, var, c1, me1
        self.grads, self.chains, self.out = {}, {}, {}

    def grad(self, layer, name, g):
        self.grads[layer, name] = g

    def point(self, layer, kernel_name, after):
        return self.run(self.PLAN.get((layer, kernel_name), ()), after) or (after,)

    def run(self, actions, after):
        deps = []
        for tag, stage in actions:
            if stage == "start":
                self.chains[tag] = _ReduceScatter(tag, [self.grads[lk] for lk in self.GROUPS[tag]], self.c1, self.me1)
                deps.append(self.chains[tag].start())
            elif stage == "finish":
                for (layer, k), (own, sib) in zip(self.GROUPS[tag], self.chains[tag].finish(after)):
                    self.out[k], token = _adamw_big_layer(layer, self.w[k], self.mom[k], self.var[k], own, sib, self.c1,
                                                          self.out.get(k))
                    deps.append(token)
            else:
                deps.append(getattr(self.chains[tag], stage)(after))
            after = deps[-1]
        self.last = after
        return tuple(deps)


def _from_shard_major(name, sm):
    if name == "meta_tokens":
        return sm.transpose(1, 0, 2).reshape(N_META, -1)
    if name == "convb_pw_w":
        return sm.transpose(1, 0, 2, 3).reshape(2, -1, D_CONV)
    return sm.transpose(1, 2, 0, 3).reshape(sm.shape[1], sm.shape[2], -1)


def kernel(x, meta_tokens, mix_norm_g, w_in, pool_w, pool_scale, convb_dw_w, convb_dw_b, convb_ln_g, convb_ln_b, convb_pw_w, rg_conv_w, rg_conv_b, rg_w_a, rg_b_a, rg_w_x, rg_b_x, rg_lambda, w_out, mlp_norm_g, w_up, w_down, final_norm_g, loss_target, m_meta_tokens, m_mix_norm_g, m_w_in, m_pool_w, m_pool_scale, m_convb_dw_w, m_convb_dw_b, m_convb_ln_g, m_convb_ln_b, m_convb_pw_w, m_rg_conv_w, m_rg_conv_b, m_rg_w_a, m_rg_b_a, m_rg_w_x, m_rg_b_x, m_rg_lambda, m_w_out, m_mlp_norm_g, m_w_up, m_w_down, m_final_norm_g, v_meta_tokens, v_mix_norm_g, v_w_in, v_pool_w, v_pool_scale, v_convb_dw_w, v_convb_dw_b, v_convb_ln_g, v_convb_ln_b, v_convb_pw_w, v_rg_conv_w, v_rg_conv_b, v_rg_w_a, v_rg_b_a, v_rg_w_x, v_rg_b_x, v_rg_lambda, v_w_out, v_mlp_norm_g, v_w_up, v_w_down, v_final_norm_g):
    given = dict(locals())
    w = {k: given[k] for k in WEIGHTS}
    mom = {k: given["m_" + k] for k in WEIGHTS}
    var = {k: given["v_" + k] for k in WEIGHTS}
    xi, yi, ci = _place()
    me1 = (2 * xi + yi).astype(jnp.int32).reshape(1)
    c1 = ci.astype(jnp.int32).reshape(1)

    small_rows = [_rows_of(w[k].size) for k in SMALL_SHARDED]
    small_pack = jnp.concatenate([_as_rows(w[k]) for k in SMALL_SHARDED])
    transposed = lambda d: {**d, "w_in": d["w_in"].transpose(0, 2, 1)}
    wt, momt, vart = transposed(w), transposed(mom), transposed(var)
    shard = lambda l, k: wt[k][l].astype(BF16)
    order = [[(0, "w_in"), "small"], [(0, "w_out"), (0, "w_up")], [(0, "w_down")], [(1, "w_in")], [(1, "w_out"), (1, "w_up")],
             [(1, "w_down")]]
    state, token = _gather_start([[small_pack if lk == "small" else shard(*lk) for lk in g] for g in order], me1[0])
    landed = {}

    def fetch(l, k, after):
        gi = [i for i, g in enumerate(order) if (l, k) in g][0]
        if gi not in landed:
            landed[gi] = _gather_wait(state[gi], after, "gather_wait_%d" % gi)
        raw = landed[gi][order[gi].index((l, k))]
        if k == "w_in":
            return raw.reshape(D_IN, D_MODEL)
        return raw.reshape(D_MODEL, D_MODEL) if k == "w_out" else raw

    seq = x.shape[1]
    t_real = N_META + seq
    t_pad = -(-t_real // ROW_ALIGN) * ROW_ALIGN
    tail = jnp.zeros((t_pad - t_real, D_MODEL), F32)
    front = jnp.zeros((N_META, D_MODEL), F32)
    h = jnp.concatenate([front + token[0, 0], x[0], tail])
    tgt = jnp.concatenate([front, loss_target[0], tail])
    landed[0] = _gather_wait(state[0], h, "gather_wait_0")
    wfull = {k: (w[k] + token[0, 0] if k in ("pool_w", "rg_w_a", "rg_w_x") else w[k]) for k in WEIGHTS}
    off = 0
    for k, rows in zip(SMALL_SHARDED, small_rows):
        sm = landed[0][1][:, off:off + rows].reshape(N_CHIPS, -1)[:, :w[k].size].reshape((N_CHIPS,) + w[k].shape)
        wfull[k] = _from_shard_major(k, sm)
        off += rows
    h = lax.dynamic_update_slice(h, wfull["meta_tokens"], (0, 0))
    sched = _GradientSchedule(wt, momt, vart, c1, me1)
    loss, dh, gsmall = _local_step(h, tgt, t_real, wfull, fetch, sched)
    grad_x = dh[N_META:t_real][None]
    gsmall["meta_tokens"] = dh[:N_META]

    names = SMALL_REPL + SMALL_SHARDED
    two_d = lambda a: a.reshape(1, -1) if a.ndim == 1 else a
    partial = [two_d(gsmall[k]) for k in names]
    partial[0] = partial[0] + sched.last[0, 0]
    summed = dict(zip(names, _allreduce_small(partial)))
    sched.run([("c0", "to_sibling")], summed[names[0]])
    for k in SMALL_SHARDED:
        ax = SHARDED_AXIS[k]
        summed[k] = lax.dynamic_slice_in_dim(summed[k], me1[0] * w[k].shape[ax], w[k].shape[ax], axis=ax)

    out = {}
    res = _adamw_small([two_d(w[k]) for k in names], [summed[k] for k in names], [two_d(mom[k]) for k in names],
                       [two_d(var[k]) for k in names])
    for k, (d, m2, v2) in zip(names, res):
        out[k] = tuple(o.reshape(w[k].shape) for o in (summed[k], d, m2, v2))
    sched.run([("b0", "finish"), ("c0", "finish")], res[0][0])
    out.update(sched.out)
    out["w_in"] = tuple(o.transpose(0, 2, 1) for o in out["w_in"])

    loss = lax.psum(loss, ("x", "y", "c"))
    return (loss, grad_x, *[out[k][0] for k in WEIGHTS], *[out[k][1] for k in WEIGHTS],
            *[out[k][2] for k in WEIGHTS], *[out[k][3] for k in WEIGHTS])
```

```python
import functools

import jax
import jax.numpy as jnp
from jax import lax
from jax.experimental import pallas as pl
from jax.experimental.pallas import tpu as pltpu

F32, BF16 = jnp.float32, jnp.bfloat16
MESH = pl.DeviceIdType.MESH
ANY = pl.BlockSpec(memory_space=pl.ANY)

D_MODEL = 1024
N_META = 16
D_POOL = 256
D_CONV = 256
D_RNN = 512
D_IN = D_POOL + 2 * D_CONV + 2 * D_RNN
D_FF = 4096
FF_CHUNK = 1024
POOL_GW = 64
CONV_K = 31
RG_CONV_K = 4
RG_HD = 64
RG_C = 8.0
EPS = 1e-6
ADAM_LR, ADAM_B1, ADAM_B2, ADAM_EPS, ADAM_WD, ADAM_STEP = 0.001, 0.9, 0.999, 1e-08, 0.01, 10

HALO = 32
ROW_ALIGN = 256
TM_MIX = 384
TM_MAT = 768
TM_MLP_BWD = 384
N_CHIPS = 4
VMEM_LIMIT = 56 * 1024 * 1024

BIG = ("w_in", "w_out", "w_up", "w_down")
SMALL_SHARDED = ("meta_tokens", "convb_dw_w", "convb_pw_w", "rg_conv_w")
SMALL_REPL = ("mix_norm_g", "pool_w", "pool_scale", "convb_dw_b", "convb_ln_g", "convb_ln_b", "rg_conv_b",
              "rg_w_a", "rg_b_a", "rg_w_x", "rg_b_x", "rg_lambda", "mlp_norm_g", "final_norm_g")
WEIGHTS = ("meta_tokens", "mix_norm_g", "w_in", "pool_w", "pool_scale", "convb_dw_w", "convb_dw_b", "convb_ln_g",
           "convb_ln_b", "convb_pw_w", "rg_conv_w", "rg_conv_b", "rg_w_a", "rg_b_a", "rg_w_x", "rg_b_x",
           "rg_lambda", "w_out", "mlp_norm_g", "w_up", "w_down", "final_norm_g")


def _params(*sem):
    return pltpu.CompilerParams(dimension_semantics=sem, vmem_limit_bytes=VMEM_LIMIT)


def _row_tile(t, cap):
    best = None
    for tm in range(128, cap + 1, 128):
        if t % tm == 0:
            best = tm
    assert best is not None, (t, cap)
    return best


def _dot(a, b):
    return jnp.dot(a, b, preferred_element_type=F32)


def _dot_nt(a, b):
    return lax.dot_general(a, b, (((1,), (1,)), ((), ())), preferred_element_type=F32)


def _dot_tn(a, b):
    return lax.dot_general(a, b, (((0,), (0,)), ((), ())), preferred_element_type=F32)


def _rms(x):
    r = lax.rsqrt(jnp.mean(x * x, axis=-1, keepdims=True) + EPS)
    return r, x * r


def _rms_bwd(du, n, r, g):
    dn = du * g
    return r * (dn - n * jnp.mean(dn * n, axis=-1, keepdims=True))


def _sig(x):
    return jax.nn.sigmoid(x)


def _colsum(x):
    return jnp.sum(x, axis=0, keepdims=True)


def _one_minus_sq(a, log_a):
    x = 2.0 * log_a
    series = -x * (1.0 + x * (0.5 + x * (1.0 / 6)))
    return jnp.where(x > -0.01, series, 1.0 - a * a)


_GELU_K0 = 0.7978845608028654
_GELU_K1 = 0.044715


def _gelu_and_grad(x):
    th = jnp.tanh(_GELU_K0 * (x + _GELU_K1 * x * x * x))
    val = 0.5 * x * (1.0 + th)
    grad = 0.5 * (1.0 + th) + 0.5 * x * (1.0 - th * th) * _GELU_K0 * (1.0 + 3.0 * _GELU_K1 * x * x)
    return val, grad


def _full(a):
    nd = a.ndim
    return pl.BlockSpec(a.shape, lambda *_: (0,) * nd)


def _resident(a):
    nd = a.ndim
    return pl.BlockSpec(a.shape, lambda *_: (0,) * nd, pipeline_mode=pl.Buffered(1))


def _after(body, n_in, deps):
    def wrapped(*refs):
        return body(*refs[:n_in], *refs[n_in + len(deps):])
    return wrapped


def _lane_sel(lane, a2, a4, a8, a16):
    return jnp.where(lane < POOL_GW, a2, jnp.where(lane < 2 * POOL_GW, a4, jnp.where(lane < 3 * POOL_GW, a8, a16)))


def _window_sums_back(src, tmp_a, tmp_b, tm):
    n = HALO + tm
    rows = lambda ref, lo, back: ref[pl.ds(lo - back, n - lo), :]
    tmp_a[pl.ds(8, n - 8), :] = rows(src, 8, 0) + rows(src, 8, 1)
    tmp_b[pl.ds(16, n - 16), :] = rows(tmp_a, 16, 0) + rows(tmp_a, 16, 2)
    s2 = rows(tmp_a, HALO, 0)
    tmp_a[pl.ds(24, n - 24), :] = rows(tmp_b, 24, 0) + rows(tmp_b, 24, 4)
    s8 = rows(tmp_a, HALO, 0)
    return s2, rows(tmp_b, HALO, 0), s8, s8 + rows(tmp_a, HALO, 8)


def _window_sums_ahead(src, tmp_a, tmp_b, tm):
    rows = lambda ref, n, ahead: ref[pl.ds(ahead, n), :]
    tmp_a[pl.ds(0, tm + 24), :] = rows(src, tm + 24, 0) + rows(src, tm + 24, 1)
    tmp_b[pl.ds(0, tm + 16), :] = rows(tmp_a, tm + 16, 0) + rows(tmp_a, tm + 16, 2)
    s2 = rows(tmp_a, tm, 0)
    tmp_a[pl.ds(0, tm + 8), :] = rows(tmp_b, tm + 8, 0) + rows(tmp_b, tm + 8, 4)
    s8 = rows(tmp_a, tm, 0)
    return s2, rows(tmp_b, tm, 0), s8, s8 + rows(tmp_a, tm, 8)


def _pool_counts(tm, t0):
    lane = lax.broadcasted_iota(jnp.int32, (tm, D_POOL), 1)
    row = lax.broadcasted_iota(jnp.int32, (tm, D_POOL), 0) + t0
    cnt = jnp.minimum(row + 1, _lane_sel(lane, 2, 4, 8, 16)).astype(F32)
    return lane, cnt


def _pool_fwd(ext_q, tmp_a, tmp_b, tm, t0):
    lane, cnt = _pool_counts(tm, t0)
    q = ext_q[pl.ds(HALO, tm), :]
    pooled = _lane_sel(lane, *_window_sums_back(ext_q, tmp_a, tmp_b, tm)) / cnt - q
    return pooled, lane, cnt


def _taps(src, w_of, offs, tm, zbuf):
    acc = None
    for r in range(8):
        ks = [k for k in range(len(offs)) if offs[k] % 8 == r]
        if not ks:
            continue
        rows = tm + (8 if r else 0)
        z = w_of(ks[0]) * src[pl.ds(offs[ks[0]] - r, rows), :]
        for k in ks[1:]:
            z = z + w_of(k) * src[pl.ds(offs[k] - r, rows), :]
        if r:
            zbuf[...] = z
            z = zbuf[pl.ds(r, tm), :]
        acc = z if acc is None else acc + z
    return acc


def _tap_grads(d_pad, src, offs, tm, g_ref, zbuf):
    ch = src.shape[-1]
    for r in range(8):
        ks = [k for k in range(len(offs)) if offs[k] % 8 == r]
        if not ks:
            continue
        rows = tm + (8 if r else 0)
        if r:
            zbuf[...] = d_pad[pl.ds(8 - r, rows), :]
        for k in ks:
            d = zbuf[...] if r else d_pad[pl.ds(8, rows), :]
            prod = d * src[pl.ds(offs[k] - r, rows), :]
            g_ref[k] += jnp.sum(prod.reshape(rows // 8, 8, ch), axis=0)


_CONV_OFFS = [HALO - (CONV_K - 1) + k for k in range(CONV_K)]


def _conv_fwd(ext_u, dww_ref, dwb, tm, zbuf):
    return dwb + _taps(ext_u, lambda k: dww_ref[k:k + 1, :], _CONV_OFFS, tm, zbuf)


def _ln_silu(c, lng, lnb):
    mu = jnp.mean(c, axis=-1, keepdims=True)
    cc = c - mu
    rstd = lax.rsqrt(jnp.mean(cc * cc, axis=-1, keepdims=True) + EPS)
    z = cc * rstd
    l = z * lng + lnb
    sl = _sig(l)
    return z, rstd, l, sl, l * sl


def _rg_conv(ext_x, cw_ref, cb, tm):
    xc = cb + cw_ref[0:1, :] * ext_x[pl.ds(HALO - (RG_CONV_K - 1), tm), :]
    for k in range(1, RG_CONV_K):
        xc = xc + cw_ref[k:k + 1, :] * ext_x[pl.ds(HALO - (RG_CONV_K - 1) + k, tm), :]
    return xc


def _softplus_neg(lam):
    return jnp.maximum(-lam, 0.0) + jnp.log(1.0 + jnp.exp(-jnp.abs(lam)))


def _rg_gates(xc, wa, ba, wx, bx, lam):
    xcb = xc.astype(BF16)
    r = _sig(_dot(xcb, wa) + ba)
    ig = _sig(_dot(xcb, wx) + bx)
    log_a = (-RG_C * r) * _softplus_neg(lam)
    a = jnp.exp(log_a)
    return r, ig, a, jnp.sqrt(_one_minus_sq(a, log_a))


def _scan_rows(a_ref, b_ref, out_ref, carry, tm, reverse):
    rows = lax.broadcasted_iota(jnp.int32, (8, D_RNN), 0)
    ngrp = tm // 8

    def grp(gi, hb):
        st = pl.multiple_of((ngrp - 1 - gi if reverse else gi) * 8, 8)
        a8 = a_ref[pl.ds(st, 8), :]
        b8 = b_ref[pl.ds(st, 8), :]
        out = jnp.zeros((8, D_RNN), F32)
        for j in (range(7, -1, -1) if reverse else range(8)):
            aj = jnp.broadcast_to(a8[j:j + 1, :], (8, D_RNN))
            bj = jnp.broadcast_to(b8[j:j + 1, :], (8, D_RNN))
            if reverse:
                cur = bj + hb
                hb = aj * cur
            else:
                cur = aj * hb + bj
                hb = cur
            out = jnp.where(rows == j, cur, out)
        out_ref[pl.ds(st, 8), :] = out
        return hb

    carry[...] = lax.fori_loop(0, ngrp, grp, carry[...])


_MIX_W = ("wp", "psc", "dww", "dwb", "lng", "lnb", "wpw", "cw", "cb", "wa", "ba", "wx", "bx", "lam")


def _mixer_fwd(h, g, w_in, mw):
    t = h.shape[0]
    tm = _row_tile(t, TM_MIX)

    def body(h_ref, g_ref, win_ref, wp, psc, dww, dwb, lng, lnb, wpw, cw, cb, wa, ba, wx, bx, lam,
             y_ref, p_ref, u_ref, hs_ref, conv_ref, xc_ref, gates_ref, ext_q, ext_u, ext_x, tmp_a, tmp_b, zbuf, a_s, b_s, hcar):
        i = pl.program_id(0)

        @pl.when(i == 0)
        def _():
            ext_q[0:HALO, :] = jnp.zeros((HALO, D_POOL), F32)
            ext_u[0:HALO, :] = jnp.zeros((HALO, D_CONV), F32)
            ext_x[0:HALO, :] = jnp.zeros((HALO, D_RNN), F32)
            hcar[...] = jnp.zeros((8, D_RNN), F32)

        u = (_rms(h_ref[...])[1] * g_ref[...]).astype(BF16)
        u_ref[...] = u
        p_ref[...] = _dot_nt(u, win_ref[...])

        ext_q[pl.ds(HALO, tm), :] = p_ref[:, 0:256]
        pooled, _, _ = _pool_fwd(ext_q, tmp_a, tmp_b, tm, i * tm)
        y_ref[:, 0:256] = (_dot(pooled.astype(BF16), wp[...]) * psc[...]).astype(BF16)

        ext_u[pl.ds(HALO, tm), :] = p_ref[:, 256:512] * _sig(p_ref[:, 512:768])
        conv = _conv_fwd(ext_u, dww, dwb[...], tm, zbuf)
        conv_ref[...] = conv
        act = _ln_silu(conv, lng[...], lnb[...])[4]
        y_ref[:, 256:512] = _dot(act.astype(BF16), wpw[...]).astype(BF16)

        ext_x[pl.ds(HALO, tm), :] = p_ref[:, 1280:1792]
        xc = _rg_conv(ext_x, cw, cb[...], tm)
        xc_ref[...] = xc
        r, ig, a, m = _rg_gates(xc, wa[...], ba[...], wx[...], bx[...], lam[...])
        for j, gate in enumerate((r, ig, a, m)):
            gates_ref[:, j * D_RNN:(j + 1) * D_RNN] = gate
        a_s[...] = a
        b_s[...] = m * (ig * xc)
        _scan_rows(a_s, b_s, hs_ref, hcar, tm, reverse=False)
        y_ref[:, 512:1024] = (_gelu_and_grad(p_ref[:, 768:1280])[0] * hs_ref[...]).astype(BF16)

        ext_q[0:HALO, :] = ext_q[pl.ds(tm, HALO), :]
        ext_u[0:HALO, :] = ext_u[pl.ds(tm, HALO), :]
        ext_x[0:HALO, :] = ext_x[pl.ds(tm, HALO), :]

    ws = [mw[k] for k in _MIX_W]
    row = lambda w: pl.BlockSpec((tm, w), lambda i: (i, 0))
    return pl.pallas_call(
        body, name="mixer_fwd", grid=(t // tm,),
        in_specs=[row(D_MODEL), _full(g), _resident(w_in)] + [_full(w) for w in ws],
        out_specs=[row(D_MODEL), row(D_IN), row(D_MODEL), row(D_RNN), row(D_CONV), row(D_RNN), row(4 * D_RNN)],
        out_shape=[jax.ShapeDtypeStruct((t, D_MODEL), BF16), jax.ShapeDtypeStruct((t, D_IN), F32),
                   jax.ShapeDtypeStruct((t, D_MODEL), BF16), jax.ShapeDtypeStruct((t, D_RNN), F32),
                   jax.ShapeDtypeStruct((t, D_CONV), F32), jax.ShapeDtypeStruct((t, D_RNN), F32),
                   jax.ShapeDtypeStruct((t, 4 * D_RNN), F32)],
        scratch_shapes=[pltpu.VMEM((HALO + tm, D_POOL), F32), pltpu.VMEM((HALO + tm, D_CONV), F32),
                        pltpu.VMEM((HALO + tm, D_RNN), F32), pltpu.VMEM((HALO + tm, D_POOL), F32),
                        pltpu.VMEM((HALO + tm, D_POOL), F32), pltpu.VMEM((tm + 8, D_CONV), F32),
                        pltpu.VMEM((tm, D_RNN), F32), pltpu.VMEM((tm, D_RNN), F32), pltpu.VMEM((8, D_RNN), F32)],
        compiler_params=_params("arbitrary"),
    )(h, g, w_in, *ws)


_MIX_G = (("wp", (D_POOL, D_POOL)), ("psc", (1, D_POOL)), ("dww", (32, 8, D_CONV)), ("dwb", (1, D_CONV)),
          ("lng", (1, D_CONV)), ("lnb", (1, D_CONV)), ("wpw", (D_CONV, D_CONV)), ("cw", (8, D_RNN)),
          ("cb", (1, D_RNN)), ("wa", (D_RNN, D_RNN)), ("ba", (1, D_RNN)), ("wx", (D_RNN, D_RNN)),
          ("bx", (1, D_RNN)), ("lam", (1, D_RNN)), ("g1", (1, D_MODEL)))


def _mixer_bwd(p, dh1, hs, conv, xc, gates, h0, g1, w_out, w_in, mw, deps=()):
    t = p.shape[0]
    tm = _row_tile(t, TM_MIX)
    nt = t // tm
    hb = tm // HALO

    def body(p_ref, ph_ref, dh1_ref, hs_ref, hsh_ref, conv_ref, xc_ref, gates_ref, h0_ref, g1_ref, wout_ref, win_ref,
             wp, psc, dww, dwb, lng, lnb, wpw, cw, cb, wa, ba, wx, bx, lam,
             dp_ref, dh0_ref, g_wp, g_psc, g_dww, g_dwb, g_lng, g_lnb, g_wpw, g_cw, g_cb, g_wa, g_ba, g_wx, g_bx, g_lam, g_g1,
             ext_q, ext_u, ext_x, ext_h, ee, dc_s, dx_s, tmp_a, tmp_b, zbuf, d_pad, a_s, b_s, g_s, gcar, dy_ref, dp_s):
        step = pl.program_id(0)
        i = nt - 1 - step
        grads = (g_wp, g_psc, g_dww, g_dwb, g_lng, g_lnb, g_wpw, g_cw, g_cb, g_wa, g_ba, g_wx, g_bx, g_lam, g_g1)
        dy_ref[...] = dh1_ref[...].astype(BF16)
        dy_cols = lambda lo, hi: _dot_nt(dy_ref[...], wout_ref[lo:hi, :])

        @pl.when(step == 0)
        def _():
            for gr in grads:
                gr[...] = jnp.zeros(gr.shape, F32)
            ee[pl.ds(tm, HALO), :] = jnp.zeros((HALO, D_POOL), F32)
            dc_s[pl.ds(tm, HALO), :] = jnp.zeros((HALO, D_CONV), F32)
            dx_s[pl.ds(tm, HALO), :] = jnp.zeros((HALO, D_RNN), F32)
            d_pad[0:8, :] = jnp.zeros((8, D_CONV), F32)
            d_pad[pl.ds(tm + 8, 8), :] = jnp.zeros((8, D_CONV), F32)
            gcar[...] = jnp.zeros((8, D_RNN), F32)

        hm = jnp.where(i == 0, 0.0, 1.0)

        ext_q[0:HALO, :] = ph_ref[:, 0:256] * hm
        ext_q[pl.ds(HALO, tm), :] = p_ref[:, 0:256]
        pooled, lane, cnt = _pool_fwd(ext_q, tmp_a, tmp_b, tm, i * tm)
        pooled_b = pooled.astype(BF16)
        dya = dy_cols(0, 256)
        g_psc[...] += _colsum(dya * _dot(pooled_b, wp[...]))
        dmixed_b = (dya * psc[...]).astype(BF16)
        dpooled = _dot_nt(dmixed_b, wp[...])
        g_wp[...] += _dot_tn(pooled_b, dmixed_b)
        ee[0:tm, :] = dpooled / cnt
        dp_s[:, 0:256] = _lane_sel(lane, *_window_sums_ahead(ee, tmp_a, tmp_b, tm)) - dpooled
        ee[pl.ds(tm, HALO), :] = ee[0:HALO, :]

        v = p_ref[:, 256:512]
        s = _sig(p_ref[:, 512:768])
        ext_u[0:HALO, :] = ph_ref[:, 256:512] * _sig(ph_ref[:, 512:768]) * hm
        ext_u[pl.ds(HALO, tm), :] = v * s
        z, rstd, l, sl, act = _ln_silu(conv_ref[...], lng[...], lnb[...])
        dyb_b = dy_cols(256, 512).astype(BF16)
        dact = _dot_nt(dyb_b, wpw[...])
        g_wpw[...] += _dot_tn(act.astype(BF16), dyb_b)
        dl = dact * (sl * (1.0 + l * (1.0 - sl)))
        g_lng[...] += _colsum(dl * z)
        g_lnb[...] += _colsum(dl)
        dz = dl * lng[...]
        dc = rstd * (dz - jnp.mean(dz, axis=-1, keepdims=True) - z * jnp.mean(dz * z, axis=-1, keepdims=True))
        g_dwb[...] += _colsum(dc)
        dc_s[0:tm, :] = dc
        d_pad[pl.ds(8, tm), :] = dc
        _tap_grads(d_pad, ext_u, _CONV_OFFS, tm, g_dww, zbuf)
        du0 = _taps(dc_s, lambda j: dww[CONV_K - 1 - j:CONV_K - j, :], list(range(CONV_K)), tm, zbuf)
        dp_s[:, 256:512] = du0 * s
        dp_s[:, 512:768] = du0 * v * (s * (1.0 - s))
        dc_s[pl.ds(tm, HALO), :] = dc_s[0:HALO, :]

        ext_x[0:HALO, :] = ph_ref[:, 1280:1792] * hm
        ext_x[pl.ds(HALO, tm), :] = p_ref[:, 1280:1792]
        xc = xc_ref[...]
        xcb = xc.astype(BF16)
        r, ig, a, m = (gates_ref[:, j * D_RNN:(j + 1) * D_RNN] for j in range(4))
        sp = _softplus_neg(lam[...])
        ext_h[0:HALO, :] = hsh_ref[...] * hm
        ext_h[pl.ds(HALO, tm), :] = hs_ref[...]
        dyc = dy_cols(512, 1024)
        gl, dgl = _gelu_and_grad(p_ref[:, 768:1280])
        dp_s[:, 768:1280] = dyc * hs_ref[...] * dgl
        a_s[...] = a
        b_s[...] = dyc * gl
        _scan_rows(a_s, b_s, g_s, gcar, tm, reverse=True)
        g = g_s[...]
        da = g * ext_h[pl.ds(HALO - 1, tm), :]
        dm = g * (ig * xc)
        dig = g * (m * xc)
        dlog_a = da * a - dm * (a * a) / m
        g_lam[...] += _colsum(dlog_a * (-RG_C * r)) * (-_sig(-lam[...]))
        dra = (dlog_a * (-RG_C * sp)) * (r * (1.0 - r))
        dia = dig * (ig * (1.0 - ig))
        g_ba[...] += _colsum(dra)
        g_bx[...] += _colsum(dia)
        dra_b = dra.astype(BF16)
        dia_b = dia.astype(BF16)
        dxc = g * (m * ig) + _dot_nt(dra_b, wa[...]) + _dot_nt(dia_b, wx[...])
        g_wa[...] += _dot_tn(xcb, dra_b)
        g_wx[...] += _dot_tn(xcb, dia_b)
        g_cb[...] += _colsum(dxc)
        dx_s[0:tm, :] = dxc
        for k in range(RG_CONV_K):
            g_cw[k:k + 1, :] += _colsum(dxc * ext_x[pl.ds(HALO - (RG_CONV_K - 1) + k, tm), :])
        dxin = cw[RG_CONV_K - 1:RG_CONV_K, :] * dxc
        for j in range(1, RG_CONV_K):
            dxin = dxin + cw[RG_CONV_K - 1 - j:RG_CONV_K - j, :] * dx_s[pl.ds(j, tm), :]
        dp_s[:, 1280:1792] = dxin
        dx_s[pl.ds(tm, HALO), :] = dx_s[0:HALO, :]

        dpb = dp_s[...].astype(BF16)
        dp_ref[...] = dpb
        du = _dot(dpb, win_ref[...])
        r, n = _rms(h0_ref[...])
        g_g1[...] += _colsum(du * n)
        dh0_ref[...] = dh1_ref[...] + _rms_bwd(du, n, r, g1_ref[...])

    ws = [mw[k] for k in _MIX_W]
    tile = lambda w: pl.BlockSpec((tm, w), lambda s: (nt - 1 - s, 0))
    halo = lambda w: pl.BlockSpec((HALO, w), lambda s: (jnp.maximum((nt - 1 - s) * hb - 1, 0), 0))
    outs = pl.pallas_call(
        _after(body, 12 + len(ws), deps), name="mixer_bwd", grid=(nt,),
        in_specs=[tile(D_IN), halo(D_IN), tile(D_MODEL), tile(D_RNN), halo(D_RNN), tile(D_CONV), tile(D_RNN), tile(4 * D_RNN),
                  tile(D_MODEL), _full(g1),
                  _resident(w_out), _resident(w_in)] + [_full(w) for w in ws] + [ANY] * len(deps),
        out_specs=[tile(D_IN), tile(D_MODEL)] + [pl.BlockSpec(shp, lambda s, nd=len(shp): (0,) * nd) for _, shp in _MIX_G],
        out_shape=[jax.ShapeDtypeStruct((t, D_IN), BF16), jax.ShapeDtypeStruct((t, D_MODEL), F32)]
        + [jax.ShapeDtypeStruct(shp, F32) for _, shp in _MIX_G],
        scratch_shapes=[pltpu.VMEM((HALO + tm, D_POOL), F32), pltpu.VMEM((HALO + tm, D_CONV), F32),
                        pltpu.VMEM((HALO + tm, D_RNN), F32), pltpu.VMEM((HALO + tm, D_RNN), F32),
                        pltpu.VMEM((tm + HALO, D_POOL), F32), pltpu.VMEM((tm + HALO, D_CONV), F32),
                        pltpu.VMEM((tm + HALO, D_RNN), F32), pltpu.VMEM((HALO + tm, D_POOL), F32),
                        pltpu.VMEM((HALO + tm, D_POOL), F32), pltpu.VMEM((tm + 8, D_CONV), F32),
                        pltpu.VMEM((tm + 16, D_CONV), F32), pltpu.VMEM((tm, D_RNN), F32),
                        pltpu.VMEM((tm, D_RNN), F32), pltpu.VMEM((tm, D_RNN), F32), pltpu.VMEM((8, D_RNN), F32),
                        pltpu.VMEM((tm, D_MODEL), BF16), pltpu.VMEM((tm, D_IN), F32)],
        compiler_params=_params("arbitrary"),
    )(p, p, dh1, hs, hs, conv, xc, gates, h0, g1, w_out, w_in, *ws, *deps)
    return outs[0], outs[1], {k: o for (k, _), o in zip(_MIX_G, outs[2:])}


def _mid_fwd(y, h0, w_out, g, w_up):
    t = h0.shape[0]
    tm = _row_tile(t, TM_MAT)

    def body(y_ref, h0_ref, wo_ref, g_ref, wu_ref, h1_ref, u2_ref, f_ref):
        h1 = h0_ref[...] + _dot(y_ref[...], wo_ref[...])
        h1_ref[...] = h1
        u2 = (_rms(h1)[1] * g_ref[...]).astype(BF16)
        u2_ref[...] = u2
        for c in range(D_FF // FF_CHUNK):
            f_ref[:, c * FF_CHUNK:(c + 1) * FF_CHUNK] = _dot(u2, wu_ref[c]).astype(BF16)

    row = lambda w: pl.BlockSpec((tm, w), lambda i: (i, 0))
    return pl.pallas_call(
        body, name="mid_fwd", grid=(t // tm,),
        in_specs=[row(D_MODEL), row(D_MODEL), _resident(w_out), _full(g), _resident(w_up)],
        out_specs=[row(D_MODEL), row(D_MODEL), row(D_FF)],
        out_shape=[jax.ShapeDtypeStruct((t, D_MODEL), F32), jax.ShapeDtypeStruct((t, D_MODEL), BF16),
                   jax.ShapeDtypeStruct((t, D_FF), BF16)],
        compiler_params=_params("parallel"),
    )(y, h0, w_out, g, w_up)


def _down_proj(f_ref, h1_ref, wd_ref):
    acc = h1_ref[...]
    for c in range(D_FF // FF_CHUNK):
        cols = slice(c * FF_CHUNK, (c + 1) * FF_CHUNK)
        a = jnp.square(jnp.maximum(f_ref[:, cols].astype(F32), 0.0)).astype(BF16)
        acc = acc + _dot(a, wd_ref[cols, :])
    return acc


def _down_fwd(f, h1, w_down):
    t = h1.shape[0]
    tm = _row_tile(t, TM_MAT)

    def body(f_ref, h1_ref, wd_ref, h2_ref):
        h2_ref[...] = _down_proj(f_ref, h1_ref, wd_ref)

    row = lambda w: pl.BlockSpec((tm, w), lambda i: (i, 0))
    return pl.pallas_call(
        body, name="down_fwd", grid=(t // tm,),
        in_specs=[row(D_FF), row(D_MODEL), _resident(w_down)], out_specs=row(D_MODEL),
        out_shape=jax.ShapeDtypeStruct((t, D_MODEL), F32),
        compiler_params=_params("parallel"),
    )(f, h1, w_down)


def _down_fwd_loss(f, h1, w_down, g, tgt, t_real):
    t = h1.shape[0]
    tm = _row_tile(t, TM_MAT)

    def body(f_ref, h1_ref, wd_ref, g_ref, tgt_ref, loss_ref, dh_ref, dg_ref):
        i = pl.program_id(0)

        @pl.when(i == 0)
        def _():
            loss_ref[...] = jnp.zeros(loss_ref.shape, F32)
            dg_ref[...] = jnp.zeros(dg_ref.shape, F32)

        r, n = _rms(_down_proj(f_ref, h1_ref, wd_ref))
        row = lax.broadcasted_iota(jnp.int32, (tm, 1), 0) + i * tm
        valid = jnp.logical_and(row >= N_META, row < t_real)
        diff = jnp.where(valid, n * g_ref[...] - tgt_ref[...], 0.0)
        loss_ref[...] += 0.5 * jnp.sum(jnp.mean(diff * diff, axis=-1, keepdims=True))
        dy = diff * (1.0 / D_MODEL)
        dg_ref[...] += _colsum(dy * n)
        dh_ref[...] = _rms_bwd(dy, n, r, g_ref[...])

    row = lambda w: pl.BlockSpec((tm, w), lambda i: (i, 0))
    return pl.pallas_call(
        body, name="down_fwd_loss", grid=(t // tm,),
        in_specs=[row(D_FF), row(D_MODEL), _resident(w_down), _full(g), row(D_MODEL)],
        out_specs=[pl.BlockSpec((8, 128), lambda i: (0, 0)), row(D_MODEL), pl.BlockSpec((1, D_MODEL), lambda i: (0, 0))],
        out_shape=[jax.ShapeDtypeStruct((8, 128), F32), jax.ShapeDtypeStruct((t, D_MODEL), F32),
                   jax.ShapeDtypeStruct((1, D_MODEL), F32)],
        compiler_params=_params("arbitrary"),
    )(f, h1, w_down, g, tgt)


def _mlp_bwd(dh2, f, h1, g, w_up, w_down, deps=()):
    t = dh2.shape[0]
    tm = _row_tile(t, TM_MLP_BWD)

    def body(dh2_ref, f_ref, wd_ref, wu_ref, h1_ref, g_ref, df_ref, dh1_ref, dg_ref):
        @pl.when(pl.program_id(0) == 0)
        def _():
            dg_ref[...] = jnp.zeros(dg_ref.shape, F32)

        dh2 = dh2_ref[...]
        dhb = dh2.astype(BF16)
        du2 = None
        for c in range(D_FF // FF_CHUNK):
            cols = slice(c * FF_CHUNK, (c + 1) * FF_CHUNK)
            dact = _dot_nt(dhb, wd_ref[c])
            df = (dact * (2.0 * jnp.maximum(f_ref[:, cols].astype(F32), 0.0))).astype(BF16)
            df_ref[:, cols] = df
            part = _dot_nt(df, wu_ref[c])
            du2 = part if du2 is None else du2 + part
        r, n = _rms(h1_ref[...])
        dg_ref[...] += _colsum(du2 * n)
        dh1_ref[...] = dh2 + _rms_bwd(du2, n, r, g_ref[...])

    row = lambda w: pl.BlockSpec((tm, w), lambda i: (i, 0))
    return pl.pallas_call(
        _after(body, 6, deps), name="mlp_bwd", grid=(t // tm,),
        in_specs=[row(D_MODEL), row(D_FF), _resident(w_down), _resident(w_up), row(D_MODEL), _full(g)] + [ANY] * len(deps),
        out_specs=[row(D_FF), row(D_MODEL), pl.BlockSpec((1, D_MODEL), lambda i: (0, 0))],
        out_shape=[jax.ShapeDtypeStruct((t, D_FF), BF16), jax.ShapeDtypeStruct((t, D_MODEL), F32),
                   jax.ShapeDtypeStruct((1, D_MODEL), F32)],
        compiler_params=_params("arbitrary"),
    )(dh2, f, w_down, w_up, h1, g, *deps)


def _tn_matmul(a, b, kc, nc, relu2, name, deps=()):
    t, k = a.shape
    n = b.shape[1]
    tt = _row_tile(t, TM_MAT)
    gk, gn = k // kc, n // nc

    def body(a_ref, b_ref, o_ref):
        @pl.when(pl.program_id(2) == 0)
        def _():
            o_ref[...] = jnp.zeros(o_ref.shape, F32)

        av = a_ref[...]
        if relu2:
            av = jnp.square(jnp.maximum(av.astype(F32), 0.0))
        o_ref[...] += _dot_tn(av.astype(BF16), b_ref[...].astype(BF16))

    return pl.pallas_call(
        _after(body, 2, deps), name=name, grid=(gk, gn, t // tt),
        in_specs=[pl.BlockSpec((tt, kc), lambda ik, jn, it: (it, ik)), pl.BlockSpec((tt, nc), lambda ik, jn, it: (it, jn))]
        + [ANY] * len(deps),
        out_specs=pl.BlockSpec((None, kc, nc), lambda ik, jn, it: (ik * gn + jn, 0, 0)),
        out_shape=jax.ShapeDtypeStruct((gk * gn, kc, nc), F32),
        compiler_params=_params("parallel", "parallel", "arbitrary"),
    )(a, b, *deps)


def _block_diag(blocks):
    nb, hd, _ = blocks.shape
    eye = jnp.eye(nb, dtype=blocks.dtype)
    return (blocks[:, :, None, :] * eye[:, None, :, None]).reshape(nb * hd, nb * hd)


def _diag_blocks(m, nb):
    hd = m.shape[0] // nb
    eye = jnp.eye(nb, dtype=m.dtype)
    return jnp.sum(m.reshape(nb, hd, nb, hd) * eye[:, None, :, None], axis=2)


def _mixer_weights(w, l):
    row = lambda a: a.reshape(1, -1)
    return dict(
        wp=_block_diag(w["pool_w"][l]).astype(BF16), psc=row(w["pool_scale"][l]),
        dww=jnp.pad(w["convb_dw_w"][l], ((0, 32 - CONV_K), (0, 0))), dwb=row(w["convb_dw_b"][l]),
        lng=row(w["convb_ln_g"][l]), lnb=row(w["convb_ln_b"][l]), wpw=w["convb_pw_w"][l].astype(BF16),
        cw=jnp.pad(w["rg_conv_w"][l], ((0, 8 - RG_CONV_K), (0, 0))), cb=row(w["rg_conv_b"][l]),
        wa=_block_diag(w["rg_w_a"][l]).astype(BF16), ba=row(w["rg_b_a"][l]),
        wx=_block_diag(w["rg_w_x"][l]).astype(BF16), bx=row(w["rg_b_x"][l]), lam=row(w["rg_lambda"][l]))


def _local_step(h, tgt, t_real, w, fetch, hooks):
    depth = 2
    saved = []
    big = []
    for l in range(depth):
        mw = _mixer_weights(w, l)
        g1 = w["mix_norm_g"][l].reshape(1, -1)
        g2 = w["mlp_norm_g"][l].reshape(1, -1)
        wl = dict(w_in=fetch(l, "w_in", h))
        y, p, u, hs, conv, xc, gates = _mixer_fwd(h, g1, wl["w_in"], mw)
        wl["w_out"], wl["w_up"] = fetch(l, "w_out", y), fetch(l, "w_up", y)
        h1, u2, f = _mid_fwd(y, h, wl["w_out"], g2, wl["w_up"])
        wl["w_down"] = fetch(l, "w_down", f)
        if l == depth - 1:
            loss, dh, dgf = _down_fwd_loss(f, h1, wl["w_down"].reshape(D_FF, D_MODEL), w["final_norm_g"].reshape(1, -1), tgt,
                                           t_real)
            h2 = None
        else:
            h2 = _down_fwd(f, h1, wl["w_down"].reshape(D_FF, D_MODEL))
        saved.append(dict(mw=mw, g1=g1, g2=g2, h0=h, p=p, u=u, y=y, hs=hs, conv=conv, xc=xc, gates=gates, h1=h1, u2=u2, f=f))
        big.append(wl)
        h = h2

    gs = {k: [None] * depth for k in ("mix_norm_g", "mlp_norm_g", "pool_w", "pool_scale", "convb_dw_w", "convb_dw_b",
                                      "convb_ln_g", "convb_ln_b", "convb_pw_w", "rg_conv_w", "rg_conv_b", "rg_w_a",
                                      "rg_b_a", "rg_w_x", "rg_b_x", "rg_lambda")}
    deps = ()
    for l in reversed(range(depth)):
        s, wl = saved[l], big[l]
        df, dh1, dg2 = _mlp_bwd(dh, s["f"], s["h1"], s["g2"], wl["w_up"], wl["w_down"], deps)
        deps = hooks.point(l, "mlp_bwd", dh1)
        g_down = _tn_matmul(s["f"], dh, FF_CHUNK, D_MODEL, True, "dw_down", deps)
        hooks.grad(l, "w_down", g_down)
        deps = hooks.point(l, "dw_down", g_down)
        g_up = _tn_matmul(s["u2"], df, D_MODEL, FF_CHUNK, False, "dw_up", deps)
        hooks.grad(l, "w_up", g_up)
        deps = hooks.point(l, "dw_up", g_up)
        g_out = _tn_matmul(s["y"], dh1, D_MODEL, D_MODEL, False, "dw_out", deps)
        hooks.grad(l, "w_out", g_out.reshape(N_CHIPS, D_MODEL // N_CHIPS, D_MODEL))
        deps = hooks.point(l, "dw_out", g_out)
        dp, dh, mg = _mixer_bwd(s["p"], dh1, s["hs"], s["conv"], s["xc"], s["gates"], s["h0"], s["g1"], wl["w_out"],
                                wl["w_in"], s["mw"], deps)
        deps = hooks.point(l, "mixer_bwd", dh)
        g_in = _tn_matmul(dp, s["u"], D_IN, D_MODEL, False, "dw_in", deps).reshape(N_CHIPS, D_IN // N_CHIPS, D_MODEL)
        hooks.grad(l, "w_in", g_in)
        deps = hooks.point(l, "dw_in", g_in)
        gs["mix_norm_g"][l] = mg["g1"][0]
        gs["mlp_norm_g"][l] = dg2[0]
        gs["pool_w"][l] = _diag_blocks(mg["wp"], D_POOL // POOL_GW)
        gs["pool_scale"][l] = mg["psc"][0]
        gs["convb_dw_w"][l] = jnp.sum(mg["dww"][:CONV_K], axis=1)
        gs["convb_dw_b"][l] = mg["dwb"][0]
        gs["convb_ln_g"][l] = mg["lng"][0]
        gs["convb_ln_b"][l] = mg["lnb"][0]
        gs["convb_pw_w"][l] = mg["wpw"]
        gs["rg_conv_w"][l] = mg["cw"][:RG_CONV_K]
        gs["rg_conv_b"][l] = mg["cb"][0]
        gs["rg_w_a"][l] = _diag_blocks(mg["wa"], D_RNN // RG_HD)
        gs["rg_b_a"][l] = mg["ba"][0]
        gs["rg_w_x"][l] = _diag_blocks(mg["wx"], D_RNN // RG_HD)
        gs["rg_b_x"][l] = mg["bx"][0]
        gs["rg_lambda"][l] = mg["lam"][0]
    gsmall = {k: jnp.stack(v) for k, v in gs.items()}
    gsmall["final_norm_g"] = dgf[0]
    return loss[0, 0], dh, gsmall


def _place():
    return lax.axis_index("x"), lax.axis_index("y"), lax.axis_index("c")


def _other_chips(x, y):
    return [(1 - x, y), (x, 1 - y), (1 - x, 1 - y)]


HBM_SPEC = pl.BlockSpec(memory_space=pltpu.HBM)
SEM_SPEC = pl.BlockSpec(memory_space=pltpu.SEMAPHORE)
DATAFLOW = pltpu.SideEffectType.DATAFLOW_SIDE_EFFECTING


def _gather_copies(src_refs, land_refs, send_sem, recv_sem, first):
    x, y, c = _place()
    me = 2 * x + y
    out = []
    for n in range(len(src_refs)):
        for j, (px, py) in enumerate(_other_chips(x, y)):
            out.append(pltpu.make_async_remote_copy(src_refs[n], land_refs[n].at[me], send_sem.at[first + 3 * n + j],
                                                    recv_sem.at[first + 3 * n + j], device_id=(px, py, c), device_id_type=MESH))
    return out


def _gather_start(groups, me):
    srcs = [pltpu.with_memory_space_constraint(s, pltpu.HBM) for g in groups for s in g]
    lands = [pltpu.with_memory_space_constraint(
        lax.dynamic_update_slice(jnp.zeros((N_CHIPS,) + s.shape, s.dtype), s[None], (me,) + (0,) * s.ndim), pltpu.HBM)
        for g in groups for s in g]
    n, ng = len(srcs), len(groups)
    first = [sum(len(g) for g in groups[:i]) for i in range(ng)]

    def body(*refs):
        src_refs, land_refs = refs[:n], refs[n:2 * n]
        sems = refs[2 * n:2 * n + 2 * ng]
        token = refs[-1]
        for gi, g in enumerate(groups):
            lo, hi = first[gi], first[gi] + len(g)
            for cp in _gather_copies(src_refs[lo:hi], land_refs[lo:hi], sems[2 * gi], sems[2 * gi + 1], 0):
                cp.start()
        token[...] = jnp.zeros(token.shape, token.dtype)

    sem_shapes = [pltpu.SemaphoreType.DMA((3 * len(g),)) for g in groups for _ in range(2)]
    outs = pl.pallas_call(
        body, name="gather_start",
        out_shape=sem_shapes + [pltpu.HBM(a.shape, a.dtype) for a in srcs + lands] + [jax.ShapeDtypeStruct((8, 128), F32)],
        in_specs=[HBM_SPEC] * (2 * n),
        out_specs=[SEM_SPEC] * (2 * ng) + [HBM_SPEC] * (2 * n) + [pl.BlockSpec(memory_space=pltpu.VMEM)],
        input_output_aliases={i: 2 * ng + i for i in range(2 * n)},
        compiler_params=pltpu.CompilerParams(has_side_effects=DATAFLOW),
    )(*srcs, *lands)
    sems, thru, token = outs[:2 * ng], outs[2 * ng:2 * ng + 2 * n], outs[-1]
    state = []
    for gi, g in enumerate(groups):
        lo, hi = first[gi], first[gi] + len(g)
        state.append((sems[2 * gi], sems[2 * gi + 1], thru[lo:hi], thru[n + lo:n + hi]))
    return state, token


def _gather_wait(state, after, name):
    send_sem, recv_sem, srcs, lands = state
    n = len(srcs)

    def body(*refs):
        src_refs, land_refs = refs[:n], refs[n:2 * n]
        send, recv = refs[2 * n], refs[2 * n + 1]
        for cp in _gather_copies(src_refs, land_refs, send, recv, 0):
            cp.wait_send()
            cp.wait_recv()

    outs = pl.pallas_call(
        body, name=name,
        out_shape=[pltpu.HBM(a.shape, a.dtype) for a in list(srcs) + list(lands)],
        in_specs=[HBM_SPEC] * (2 * n) + [SEM_SPEC, SEM_SPEC, ANY],
        out_specs=[HBM_SPEC] * (2 * n),
        input_output_aliases={i: i for i in range(2 * n)},
        compiler_params=pltpu.CompilerParams(has_side_effects=DATAFLOW),
    )(*srcs, *lands, send_sem, recv_sem, after)
    return outs[n:]


def _add_halves(g, recv, c1):
    nk, r, cd = g.shape
    r2 = r // 2

    def body(c_ref, g_ref, r_ref, pab_ref):
        pab_ref[...] = (g_ref[...] + r_ref[...]).astype(BF16)

    blk = pl.BlockSpec((None, r2, cd), lambda k, c_ref: (k, 0, 0))
    return pl.pallas_call(
        body, name="rs_add_halves",
        grid_spec=pltpu.PrefetchScalarGridSpec(
            num_scalar_prefetch=1, grid=(nk,),
            in_specs=[pl.BlockSpec((None, r2, cd), lambda k, c_ref: (k, c_ref[0], 0)), blk], out_specs=blk),
        out_shape=jax.ShapeDtypeStruct((nk, r2, cd), BF16),
        compiler_params=_params("parallel"),
    )(c1, g, recv)


def _sum_partials(g, recv_sibling, recv_chips, c_me):
    nk, r, cd = g.shape
    r2 = r // 2

    def body(cm_ref, g_ref, a_ref, r_ref, s_ref):
        own = g_ref[...] + a_ref[...]
        s_ref[...] = ((own + r_ref[0].astype(F32)) + r_ref[1].astype(F32)) + r_ref[2].astype(F32)

    return pl.pallas_call(
        body, name="rs_sum_partials",
        grid_spec=pltpu.PrefetchScalarGridSpec(
            num_scalar_prefetch=1, grid=(1,),
            in_specs=[pl.BlockSpec((None, r2, cd), lambda i, cm: (cm[1], cm[0], 0)),
                      pl.BlockSpec((None, r2, cd), lambda i, cm: (cm[1], 0, 0)),
                      pl.BlockSpec((3, r2, cd), lambda i, cm: (0, 0, 0))],
            out_specs=pl.BlockSpec((r2, cd), lambda i, cm: (0, 0))),
        out_shape=jax.ShapeDtypeStruct((r2, cd), F32),
        compiler_params=_params("arbitrary"),
    )(c_me, g, recv_sibling, recv_chips)


def _split_start(name, srcs, lands, ncopies, make_copies):
    srcs = [pltpu.with_memory_space_constraint(s, pltpu.HBM) for s in srcs]
    lands = [pltpu.with_memory_space_constraint(a, pltpu.HBM) for a in lands]
    n, m = len(srcs), len(lands)

    def body(*refs):
        src_refs, land_refs = refs[:n], refs[n:n + m]
        send, recv, token = refs[n + m], refs[n + m + 1], refs[-1]
        for cp in make_copies(src_refs, land_refs, send, recv):
            cp.start()
        token[...] = jnp.zeros(token.shape, token.dtype)

    outs = pl.pallas_call(
        body, name=name,
        out_shape=[pltpu.SemaphoreType.DMA((ncopies,)), pltpu.SemaphoreType.DMA((ncopies,))]
        + [pltpu.HBM(a.shape, a.dtype) for a in srcs + lands] + [jax.ShapeDtypeStruct((8, 128), F32)],
        in_specs=[HBM_SPEC] * (n + m),
        out_specs=[SEM_SPEC, SEM_SPEC] + [HBM_SPEC] * (n + m) + [pl.BlockSpec(memory_space=pltpu.VMEM)],
        input_output_aliases={i: 2 + i for i in range(n + m)},
        compiler_params=pltpu.CompilerParams(has_side_effects=DATAFLOW),
    )(*srcs, *lands)
    return (outs[0], outs[1], outs[2:2 + n], outs[2 + n:2 + n + m], make_copies), outs[-1]


def _split_wait(name, state, after):
    send_sem, recv_sem, srcs, lands, make_copies = state
    n, m = len(srcs), len(lands)

    def body(*refs):
        src_refs, land_refs = refs[:n], refs[n:n + m]
        for cp in make_copies(src_refs, land_refs, refs[n + m], refs[n + m + 1]):
            cp.wait_send()
            cp.wait_recv()

    outs = pl.pallas_call(
        body, name=name,
        out_shape=[pltpu.HBM(a.shape, a.dtype) for a in list(srcs) + list(lands)],
        in_specs=[HBM_SPEC] * (n + m) + [SEM_SPEC, SEM_SPEC, ANY],
        out_specs=[HBM_SPEC] * (n + m),
        input_output_aliases={i: i for i in range(n + m)},
        compiler_params=pltpu.CompilerParams(has_side_effects=DATAFLOW),
    )(*srcs, *lands, send_sem, recv_sem, after)
    return outs[:n], outs[n:]


def _copies_to_sibling(src_of):
    def make(src_refs, land_refs, send, recv):
        x, y, c = _place()
        return [pltpu.make_async_remote_copy(src_of(src_refs[i], c), land_refs[i], send.at[i], recv.at[i],
                                             device_id=(x, y, 1 - c), device_id_type=MESH) for i in range(len(src_refs))]
    return make


def _copies_to_chips(src_refs, land_refs, send, recv):
    x, y, c = _place()
    return [pltpu.make_async_remote_copy(src_refs[i].at[2 * px + py], land_refs[i].at[j], send.at[3 * i + j], recv.at[3 * i + j],
                                         device_id=(px, py, c), device_id_type=MESH)
            for i in range(len(src_refs)) for j, (px, py) in enumerate(_other_chips(x, y))]


def _other_half_rows(ref, c):
    r2 = ref.shape[1] // 2
    return ref.at[:, pl.ds(pl.multiple_of((1 - c) * r2, 8), r2)]


class _ReduceScatter:
    def __init__(self, tag, grads, c1, me1):
        self.tag, self.grads, self.c1, self.me1 = tag, grads, c1, me1

    def start(self):
        lands = [lax.empty((g.shape[0], g.shape[1] // 2, g.shape[2]), F32) for g in self.grads]
        self.state, token = _split_start("rs_%s_a_start" % self.tag, self.grads, lands, len(self.grads),
                                         _copies_to_sibling(_other_half_rows))
        return token

    def to_chips(self, after):
        self.halves = _split_wait("rs_%s_a_wait" % self.tag, self.state, after)
        pabs = [_add_halves(g, r, self.c1) for g, r in zip(*self.halves)]
        lands = [lax.empty((3,) + p.shape[1:], BF16) for p in pabs]
        self.state, token = _split_start("rs_%s_b_start" % self.tag, pabs, lands, 3 * len(pabs), _copies_to_chips)
        return token

    def to_sibling(self, after):
        _, recv = _split_wait("rs_%s_b_wait" % self.tag, self.state, after)
        c_me = jnp.concatenate([self.c1, self.me1])
        sums = [_sum_partials(g, ra, rb, c_me) for g, ra, rb in zip(*self.halves, recv)]
        lands = [lax.empty(s.shape, F32) for s in sums]
        self.state, token = _split_start("rs_%s_c_start" % self.tag, sums, lands, len(sums),
                                         _copies_to_sibling(lambda ref, c: ref))
        return token

    def finish(self, after):
        return list(zip(*_split_wait("rs_%s_c_wait" % self.tag, self.state, after)))


def _allreduce_small(vs):
    n = len(vs)

    def body(*refs):
        v_refs, out_refs, rbufs = refs[:n], refs[n:2 * n], refs[2 * n:3 * n]
        send_sems, recv_sems = refs[3 * n:]
        x, y, c = _place()
        for i in range(n):
            out_refs[i][...] = v_refs[i][...]
        for s, peer in enumerate([(x, y, 1 - c), (1 - x, y, c), (x, 1 - y, c)]):
            copies = [pltpu.make_async_remote_copy(out_refs[i], rbufs[i].at[s], send_sems.at[s * n + i], recv_sems.at[s * n + i],
                                                   device_id=peer, device_id_type=MESH) for i in range(n)]
            for cp in copies:
                cp.start()
            for cp in copies:
                cp.wait()
            for i in range(n):
                out_refs[i][...] = out_refs[i][...] + rbufs[i][s]

    vm = pl.BlockSpec(memory_space=pltpu.VMEM)
    return pl.pallas_call(
        body, name="allreduce_small", in_specs=[vm] * n, out_specs=[vm] * n,
        out_shape=[jax.ShapeDtypeStruct(v.shape, v.dtype) for v in vs],
        scratch_shapes=[pltpu.VMEM((3,) + v.shape, v.dtype) for v in vs]
        + [pltpu.SemaphoreType.DMA((3 * n,)), pltpu.SemaphoreType.DMA((3 * n,))],
        compiler_params=pltpu.CompilerParams(vmem_limit_bytes=VMEM_LIMIT),
    )(*vs)


def _adamw_math(w, g, m, v):
    m = ADAM_B1 * m + (1.0 - ADAM_B1) * g
    v = ADAM_B2 * v + (1.0 - ADAM_B2) * jnp.square(g)
    m_hat = m / (1.0 - ADAM_B1 ** ADAM_STEP)
    v_hat = v / (1.0 - ADAM_B2 ** ADAM_STEP)
    return -ADAM_LR * (m_hat / (jnp.sqrt(v_hat) + ADAM_EPS) + ADAM_WD * w), m, v


def _adamw_big_layer(layer, w, m, v, own, sib, c1, prev):
    _, r, cd = w.shape
    r2 = r // 2

    def body(c_ref, w_ref, m_ref, v_ref, own_ref, sib_ref, *rest):
        g_ref, d_ref, mo_ref, vo_ref, token = rest[-5:]
        g = jnp.where(pl.program_id(0) == c_ref[0], own_ref[...], sib_ref[...])
        g_ref[...] = g
        d_ref[...], mo_ref[...], vo_ref[...] = _adamw_math(w_ref[...], g, m_ref[...], v_ref[...])
        token[...] = jnp.zeros(token.shape, F32)

    blk = pl.BlockSpec((None, r2, cd), lambda hh, c_ref: (layer, hh, 0))
    half = pl.BlockSpec((r2, cd), lambda hh, c_ref: (0, 0))
    prev = () if prev is None else tuple(prev)
    outs = pl.pallas_call(
        body, name="adamw_big",
        grid_spec=pltpu.PrefetchScalarGridSpec(
            num_scalar_prefetch=1, grid=(2,), in_specs=[blk, blk, blk, half, half] + [ANY] * len(prev),
            out_specs=[blk] * 4 + [pl.BlockSpec((8, 128), lambda hh, c_ref: (0, 0))]),
        out_shape=[jax.ShapeDtypeStruct(w.shape, F32)] * 4 + [jax.ShapeDtypeStruct((8, 128), F32)],
        input_output_aliases={6 + i: i for i in range(len(prev))},
        compiler_params=_params("arbitrary"),
    )(c1, w, m, v, own, sib, *prev)
    return outs[:4], outs[4]


def _adamw_small(ws, gs, ms, vs):
    n = len(ws)

    def body(*refs):
        w_refs, g_refs, m_refs, v_refs = refs[:n], refs[n:2 * n], refs[2 * n:3 * n], refs[3 * n:4 * n]
        outs = refs[4 * n:]
        for i in range(n):
            outs[3 * i][...], outs[3 * i + 1][...], outs[3 * i + 2][...] = _adamw_math(
                w_refs[i][...], g_refs[i][...], m_refs[i][...], v_refs[i][...])

    vm = pl.BlockSpec(memory_space=pltpu.VMEM)
    outs = pl.pallas_call(
        body, name="adamw_small", in_specs=[vm] * (4 * n), out_specs=[vm] * (3 * n),
        out_shape=[jax.ShapeDtypeStruct(w.shape, F32) for w in ws for _ in range(3)],
        compiler_params=pltpu.CompilerParams(vmem_limit_bytes=VMEM_LIMIT),
    )(*ws, *gs, *ms, *vs)
    return [outs[3 * i:3 * i + 3] for i in range(n)]


LANES = 128
SUBLANES = 8
SHARDED_AXIS = {"meta_tokens": 1, "convb_dw_w": 2, "convb_pw_w": 1, "rg_conv_w": 2}


def _rows_of(size):
    return -(-size // (LANES * SUBLANES)) * SUBLANES


def _as_rows(a, rows=None):
    flat = a.reshape(-1)
    rows = _rows_of(flat.size) if rows is None else rows
    return jnp.pad(flat, (0, rows * LANES - flat.size)).reshape(rows, LANES)


class _GradientSchedule:
    GROUPS = {"l1": [(1, "w_down"), (1, "w_up"), (1, "w_out"), (1, "w_in")], "a0": [(0, "w_down"), (0, "w_up")],
              "b0": [(0, "w_out")], "c0": [(0, "w_in")]}
    PLAN = {
        (1, "dw_in"): [("l1", "start")],
        (0, "mlp_bwd"): [("l1", "to_chips")],
        (0, "dw_up"): [("l1", "to_sibling"), ("a0", "start")],
        (0, "dw_out"): [("l1", "finish"), ("a0", "to_chips"), ("b0", "start")],
        (0, "mixer_bwd"): [("a0", "to_sibling"), ("b0", "to_chips")],
        (0, "dw_in"): [("c0", "start"), ("a0", "finish"), ("b0", "to_sibling"), ("c0", "to_chips")],
    }

    def __init__(self, w, mom, var, c1, me1):
        self.w, self.mom, self.var, self.c1, self.me1 = w, mom, var, c1, me1
        self.grads, self.chains, self.out = {}, {}, {}

    def grad(self, layer, name, g):
        self.grads[layer, name] = g

    def point(self, layer, kernel_name, after):
        return self.run(self.PLAN.get((layer, kernel_name), ()), after) or (after,)

    def run(self, actions, after):
        deps = []
        for tag, stage in actions:
            if stage == "start":
                self.chains[tag] = _ReduceScatter(tag, [self.grads[lk] for lk in self.GROUPS[tag]], self.c1, self.me1)
                deps.append(self.chains[tag].start())
            elif stage == "finish":
                for (layer, k), (own, sib) in zip(self.GROUPS[tag], self.chains[tag].finish(after)):
                    self.out[k], token = _adamw_big_layer(layer, self.w[k], self.mom[k], self.var[k], own, sib, self.c1,
                                                          self.out.get(k))
                    deps.append(token)
            else:
                deps.append(getattr(self.chains[tag], stage)(after))
            after = deps[-1]
        self.last = after
        return tuple(deps)


def _from_shard_major(name, sm):
    if name == "meta_tokens":
        return sm.transpose(1, 0, 2).reshape(N_META, -1)
    if name == "convb_pw_w":
        return sm.transpose(1, 0, 2, 3).reshape(2, -1, D_CONV)
    return sm.transpose(1, 2, 0, 3).reshape(sm.shape[1], sm.shape[2], -1)


def kernel(x, meta_tokens, mix_norm_g, w_in, pool_w, pool_scale, convb_dw_w, convb_dw_b, convb_ln_g, convb_ln_b, convb_pw_w, rg_conv_w, rg_conv_b, rg_w_a, rg_b_a, rg_w_x, rg_b_x, rg_lambda, w_out, mlp_norm_g, w_up, w_down, final_norm_g, loss_target, m_meta_tokens, m_mix_norm_g, m_w_in, m_pool_w, m_pool_scale, m_convb_dw_w, m_convb_dw_b, m_convb_ln_g, m_convb_ln_b, m_convb_pw_w, m_rg_conv_w, m_rg_conv_b, m_rg_w_a, m_rg_b_a, m_rg_w_x, m_rg_b_x, m_rg_lambda, m_w_out, m_mlp_norm_g, m_w_up, m_w_down, m_final_norm_g, v_meta_tokens, v_mix_norm_g, v_w_in, v_pool_w, v_pool_scale, v_convb_dw_w, v_convb_dw_b, v_convb_ln_g, v_convb_ln_b, v_convb_pw_w, v_rg_conv_w, v_rg_conv_b, v_rg_w_a, v_rg_b_a, v_rg_w_x, v_rg_b_x, v_rg_lambda, v_w_out, v_mlp_norm_g, v_w_up, v_w_down, v_final_norm_g):
    given = dict(locals())
    w = {k: given[k] for k in WEIGHTS}
    mom = {k: given["m_" + k] for k in WEIGHTS}
    var = {k: given["v_" + k] for k in WEIGHTS}
    xi, yi, ci = _place()
    me1 = (2 * xi + yi).astype(jnp.int32).reshape(1)
    c1 = ci.astype(jnp.int32).reshape(1)

    small_rows = [_rows_of(w[k].size) for k in SMALL_SHARDED]
    small_pack = jnp.concatenate([_as_rows(w[k]) for k in SMALL_SHARDED])
    transposed = lambda d: {**d, "w_in": d["w_in"].transpose(0, 2, 1)}
    wt, momt, vart = transposed(w), transposed(mom), transposed(var)
    shard = lambda l, k: wt[k][l].astype(BF16)
    order = [[(0, "w_in"), "small"], [(0, "w_out"), (0, "w_up")], [(0, "w_down")], [(1, "w_in")], [(1, "w_out"), (1, "w_up")],
             [(1, "w_down")]]
    state, token = _gather_start([[small_pack if lk == "small" else shard(*lk) for lk in g] for g in order], me1[0])
    landed = {}

    def fetch(l, k, after):
        gi = [i for i, g in enumerate(order) if (l, k) in g][0]
        if gi not in landed:
            landed[gi] = _gather_wait(state[gi], after, "gather_wait_%d" % gi)
        raw = landed[gi][order[gi].index((l, k))]
        if k == "w_in":
            return raw.reshape(D_IN, D_MODEL)
        return raw.reshape(D_MODEL, D_MODEL) if k == "w_out" else raw

    seq = x.shape[1]
    t_real = N_META + seq
    t_pad = -(-t_real // ROW_ALIGN) * ROW_ALIGN
    tail = jnp.zeros((t_pad - t_real, D_MODEL), F32)
    front = jnp.zeros((N_META, D_MODEL), F32)
    h = jnp.concatenate([front + token[0, 0], x[0], tail])
    tgt = jnp.concatenate([front, loss_target[0], tail])
    landed[0] = _gather_wait(state[0], h, "gather_wait_0")
    wfull = {k: (w[k] + token[0, 0] if k in ("pool_w", "rg_w_a", "rg_w_x") else w[k]) for k in WEIGHTS}
    off = 0
    for k, rows in zip(SMALL_SHARDED, small_rows):
        sm = landed[0][1][:, off:off + rows].reshape(N_CHIPS, -1)[:, :w[k].size].reshape((N_CHIPS,) + w[k].shape)
        wfull[k] = _from_shard_major(k, sm)
        off += rows
    h = lax.dynamic_update_slice(h, wfull["meta_tokens"], (0, 0))
    sched = _GradientSchedule(wt, momt, vart, c1, me1)
    loss, dh, gsmall = _local_step(h, tgt, t_real, wfull, fetch, sched)
    grad_x = dh[N_META:t_real][None]
    gsmall["meta_tokens"] = dh[:N_META]

    names = SMALL_REPL + SMALL_SHARDED
    two_d = lambda a: a.reshape(1, -1) if a.ndim == 1 else a
    partial = [two_d(gsmall[k]) for k in names]
    partial[0] = partial[0] + sched.last[0, 0]
    summed = dict(zip(names, _allreduce_small(partial)))
    sched.run([("c0", "to_sibling")], summed[names[0]])
    for k in SMALL_SHARDED:
        ax = SHARDED_AXIS[k]
        summed[k] = lax.dynamic_slice_in_dim(summed[k], me1[0] * w[k].shape[ax], w[k].shape[ax], axis=ax)

    out = {}
    res = _adamw_small([two_d(w[k]) for k in names], [summed[k] for k in names], [two_d(mom[k]) for k in names],
                       [two_d(var[k]) for k in names])
    for k, (d, m2, v2) in zip(names, res):
        out[k] = tuple(o.reshape(w[k].shape) for o in (summed[k], d, m2, v2))
    sched.run([("b0", "finish"), ("c0", "finish")], res[0][0])
    out.update(sched.out)
    out["w_in"] = tuple(o.transpose(0, 2, 1) for o in out["w_in"])

    loss = lax.psum(loss, ("x", "y", "c"))
    return (loss, grad_x, *[out[k][0] for k in WEIGHTS], *[out[k][1] for k in WEIGHTS],
            *[out[k][2] for k in WEIGHTS], *[out[k][3] for k in WEIGHTS])
```

```python
import functools

import jax
import jax.numpy as jnp
from jax import lax
from jax.experimental import pallas as pl
from jax.experimental.pallas import tpu as pltpu

F32, BF16 = jnp.float32, jnp.bfloat16
MESH = pl.DeviceIdType.MESH
ANY = pl.BlockSpec(memory_space=pl.ANY)

D_MODEL = 1024
N_META = 16
D_POOL = 256
D_CONV = 256
D_RNN = 512
D_IN = D_POOL + 2 * D_CONV + 2 * D_RNN
D_FF = 4096
FF_CHUNK = 1024
POOL_GW = 64
CONV_K = 31
RG_CONV_K = 4
RG_HD = 64
RG_C = 8.0
EPS = 1e-6
ADAM_LR, ADAM_B1, ADAM_B2, ADAM_EPS, ADAM_WD, ADAM_STEP = 0.001, 0.9, 0.999, 1e-08, 0.01, 10

HALO = 32
ROW_ALIGN = 256
TM_MIX = 256
TM_MAT = 768
TM_MLP_BWD = 384
N_CHIPS = 4
VMEM_LIMIT = 56 * 1024 * 1024

BIG = ("w_in", "w_out", "w_up", "w_down")
SMALL_SHARDED = ("meta_tokens", "convb_dw_w", "convb_pw_w", "rg_conv_w")
SMALL_REPL = ("mix_norm_g", "pool_w", "pool_scale", "convb_dw_b", "convb_ln_g", "convb_ln_b", "rg_conv_b",
              "rg_w_a", "rg_b_a", "rg_w_x", "rg_b_x", "rg_lambda", "mlp_norm_g", "final_norm_g")
WEIGHTS = ("meta_tokens", "mix_norm_g", "w_in", "pool_w", "pool_scale", "convb_dw_w", "convb_dw_b", "convb_ln_g",
           "convb_ln_b", "convb_pw_w", "rg_conv_w", "rg_conv_b", "rg_w_a", "rg_b_a", "rg_w_x", "rg_b_x",
           "rg_lambda", "w_out", "mlp_norm_g", "w_up", "w_down", "final_norm_g")


def _params(*sem):
    return pltpu.CompilerParams(dimension_semantics=sem, vmem_limit_bytes=VMEM_LIMIT)


def _row_tile(t, cap):
    best = None
    for tm in range(128, cap + 1, 128):
        if t % tm == 0:
            best = tm
    assert best is not None, (t, cap)
    return best


def _dot(a, b):
    return jnp.dot(a, b, preferred_element_type=F32)


def _dot_nt(a, b):
    return lax.dot_general(a, b, (((1,), (1,)), ((), ())), preferred_element_type=F32)


def _dot_tn(a, b):
    return lax.dot_general(a, b, (((0,), (0,)), ((), ())), preferred_element_type=F32)


def _rms(x):
    r = lax.rsqrt(jnp.mean(x * x, axis=-1, keepdims=True) + EPS)
    return r, x * r


def _rms_bwd(du, n, r, g):
    dn = du * g
    return r * (dn - n * jnp.mean(dn * n, axis=-1, keepdims=True))


def _sig(x):
    return jax.nn.sigmoid(x)


def _colsum(x):
    return jnp.sum(x, axis=0, keepdims=True)


def _one_minus_sq(a, log_a):
    x = 2.0 * log_a
    series = -x * (1.0 + x * (0.5 + x * (1.0 / 6)))
    return jnp.where(x > -0.01, series, 1.0 - a * a)


_GELU_K0 = 0.7978845608028654
_GELU_K1 = 0.044715


def _gelu_and_grad(x):
    th = jnp.tanh(_GELU_K0 * (x + _GELU_K1 * x * x * x))
    val = 0.5 * x * (1.0 + th)
    grad = 0.5 * (1.0 + th) + 0.5 * x * (1.0 - th * th) * _GELU_K0 * (1.0 + 3.0 * _GELU_K1 * x * x)
    return val, grad


def _full(a):
    nd = a.ndim
    return pl.BlockSpec(a.shape, lambda *_: (0,) * nd)


def _resident(a):
    nd = a.ndim
    return pl.BlockSpec(a.shape, lambda *_: (0,) * nd, pipeline_mode=pl.Buffered(1))


def _after(body, n_in, deps):
    def wrapped(*refs):
        return body(*refs[:n_in], *refs[n_in + len(deps):])
    return wrapped


def _lane_sel(lane, a2, a4, a8, a16):
    return jnp.where(lane < POOL_GW, a2, jnp.where(lane < 2 * POOL_GW, a4, jnp.where(lane < 3 * POOL_GW, a8, a16)))


def _window_sums_back(src, tmp_a, tmp_b, tm):
    n = HALO + tm
    rows = lambda ref, lo, back: ref[pl.ds(lo - back, n - lo), :]
    tmp_a[pl.ds(8, n - 8), :] = rows(src, 8, 0) + rows(src, 8, 1)
    tmp_b[pl.ds(16, n - 16), :] = rows(tmp_a, 16, 0) + rows(tmp_a, 16, 2)
    s2 = rows(tmp_a, HALO, 0)
    tmp_a[pl.ds(24, n - 24), :] = rows(tmp_b, 24, 0) + rows(tmp_b, 24, 4)
    s8 = rows(tmp_a, HALO, 0)
    return s2, rows(tmp_b, HALO, 0), s8, s8 + rows(tmp_a, HALO, 8)


def _window_sums_ahead(src, tmp_a, tmp_b, tm):
    rows = lambda ref, n, ahead: ref[pl.ds(ahead, n), :]
    tmp_a[pl.ds(0, tm + 24), :] = rows(src, tm + 24, 0) + rows(src, tm + 24, 1)
    tmp_b[pl.ds(0, tm + 16), :] = rows(tmp_a, tm + 16, 0) + rows(tmp_a, tm + 16, 2)
    s2 = rows(tmp_a, tm, 0)
    tmp_a[pl.ds(0, tm + 8), :] = rows(tmp_b, tm + 8, 0) + rows(tmp_b, tm + 8, 4)
    s8 = rows(tmp_a, tm, 0)
    return s2, rows(tmp_b, tm, 0), s8, s8 + rows(tmp_a, tm, 8)


def _pool_counts(tm, t0):
    lane = lax.broadcasted_iota(jnp.int32, (tm, D_POOL), 1)
    row = lax.broadcasted_iota(jnp.int32, (tm, D_POOL), 0) + t0
    cnt = jnp.minimum(row + 1, _lane_sel(lane, 2, 4, 8, 16)).astype(F32)
    return lane, cnt


def _pool_fwd(ext_q, tmp_a, tmp_b, tm, t0):
    lane, cnt = _pool_counts(tm, t0)
    q = ext_q[pl.ds(HALO, tm), :]
    pooled = _lane_sel(lane, *_window_sums_back(ext_q, tmp_a, tmp_b, tm)) / cnt - q
    return pooled, lane, cnt


def _taps(src, w_of, offs, tm, zbuf):
    acc = None
    for r in range(8):
        ks = [k for k in range(len(offs)) if offs[k] % 8 == r]
        if not ks:
            continue
        rows = tm + (8 if r else 0)
        z = w_of(ks[0]) * src[pl.ds(offs[ks[0]] - r, rows), :]
        for k in ks[1:]:
            z = z + w_of(k) * src[pl.ds(offs[k] - r, rows), :]
        if r:
            zbuf[...] = z
            z = zbuf[pl.ds(r, tm), :]
        acc = z if acc is None else acc + z
    return acc


def _tap_grads(d_pad, src, offs, tm, g_ref, zbuf):
    ch = src.shape[-1]
    for r in range(8):
        ks = [k for k in range(len(offs)) if offs[k] % 8 == r]
        if not ks:
            continue
        rows = tm + (8 if r else 0)
        if r:
            zbuf[...] = d_pad[pl.ds(8 - r, rows), :]
        for k in ks:
            d = zbuf[...] if r else d_pad[pl.ds(8, rows), :]
            prod = d * src[pl.ds(offs[k] - r, rows), :]
            g_ref[k] += jnp.sum(prod.reshape(rows // 8, 8, ch), axis=0)


_CONV_OFFS = [HALO - (CONV_K - 1) + k for k in range(CONV_K)]


def _conv_fwd(ext_u, dww_ref, dwb, tm, zbuf):
    return dwb + _taps(ext_u, lambda k: dww_ref[k:k + 1, :], _CONV_OFFS, tm, zbuf)


def _ln_silu(c, lng, lnb):
    mu = jnp.mean(c, axis=-1, keepdims=True)
    cc = c - mu
    rstd = lax.rsqrt(jnp.mean(cc * cc, axis=-1, keepdims=True) + EPS)
    z = cc * rstd
    l = z * lng + lnb
    sl = _sig(l)
    return z, rstd, l, sl, l * sl


def _rg_conv(ext_x, cw_ref, cb, tm):
    xc = cb + cw_ref[0:1, :] * ext_x[pl.ds(HALO - (RG_CONV_K - 1), tm), :]
    for k in range(1, RG_CONV_K):
        xc = xc + cw_ref[k:k + 1, :] * ext_x[pl.ds(HALO - (RG_CONV_K - 1) + k, tm), :]
    return xc


def _softplus_neg(lam):
    return jnp.maximum(-lam, 0.0) + jnp.log(1.0 + jnp.exp(-jnp.abs(lam)))


def _rg_gates(xc, wa, ba, wx, bx, lam):
    xcb = xc.astype(BF16)
    r = _sig(_dot(xcb, wa) + ba)
    ig = _sig(_dot(xcb, wx) + bx)
    log_a = (-RG_C * r) * _softplus_neg(lam)
    a = jnp.exp(log_a)
    return r, ig, a, jnp.sqrt(_one_minus_sq(a, log_a))


def _scan_rows(a_ref, b_ref, out_ref, carry, tm, reverse):
    rows = lax.broadcasted_iota(jnp.int32, (8, D_RNN), 0)
    ngrp = tm // 8

    def grp(gi, hb):
        st = pl.multiple_of((ngrp - 1 - gi if reverse else gi) * 8, 8)
        a8 = a_ref[pl.ds(st, 8), :]
        b8 = b_ref[pl.ds(st, 8), :]
        out = jnp.zeros((8, D_RNN), F32)
        for j in (range(7, -1, -1) if reverse else range(8)):
            aj = jnp.broadcast_to(a8[j:j + 1, :], (8, D_RNN))
            bj = jnp.broadcast_to(b8[j:j + 1, :], (8, D_RNN))
            if reverse:
                cur = bj + hb
                hb = aj * cur
            else:
                cur = aj * hb + bj
                hb = cur
            out = jnp.where(rows == j, cur, out)
        out_ref[pl.ds(st, 8), :] = out
        return hb

    carry[...] = lax.fori_loop(0, ngrp, grp, carry[...])


_MIX_W = ("wp", "psc", "dww", "dwb", "lng", "lnb", "wpw", "cw", "cb", "wa", "ba", "wx", "bx", "lam")


def _mixer_fwd(h, g, w_in, mw):
    t = h.shape[0]
    tm = _row_tile(t, TM_MIX)

    def body(h_ref, g_ref, win_ref, wp, psc, dww, dwb, lng, lnb, wpw, cw, cb, wa, ba, wx, bx, lam,
             y_ref, p_ref, u_ref, hs_ref, conv_ref, xc_ref, gates_ref, ext_q, ext_u, ext_x, tmp_a, tmp_b, zbuf, a_s, b_s, hcar):
        i = pl.program_id(0)

        @pl.when(i == 0)
        def _():
            ext_q[0:HALO, :] = jnp.zeros((HALO, D_POOL), F32)
            ext_u[0:HALO, :] = jnp.zeros((HALO, D_CONV), F32)
            ext_x[0:HALO, :] = jnp.zeros((HALO, D_RNN), F32)
            hcar[...] = jnp.zeros((8, D_RNN), F32)

        u = (_rms(h_ref[...])[1] * g_ref[...]).astype(BF16)
        u_ref[...] = u
        p_ref[...] = _dot_nt(u, win_ref[...])

        ext_q[pl.ds(HALO, tm), :] = p_ref[:, 0:256]
        pooled, _, _ = _pool_fwd(ext_q, tmp_a, tmp_b, tm, i * tm)
        y_ref[:, 0:256] = (_dot(pooled.astype(BF16), wp[...]) * psc[...]).astype(BF16)

        ext_u[pl.ds(HALO, tm), :] = p_ref[:, 256:512] * _sig(p_ref[:, 512:768])
        conv = _conv_fwd(ext_u, dww, dwb[...], tm, zbuf)
        conv_ref[...] = conv
        act = _ln_silu(conv, lng[...], lnb[...])[4]
        y_ref[:, 256:512] = _dot(act.astype(BF16), wpw[...]).astype(BF16)

        ext_x[pl.ds(HALO, tm), :] = p_ref[:, 1280:1792]
        xc = _rg_conv(ext_x, cw, cb[...], tm)
        xc_ref[...] = xc
        r, ig, a, m = _rg_gates(xc, wa[...], ba[...], wx[...], bx[...], lam[...])
        for j, gate in enumerate((r, ig, a, m)):
            gates_ref[:, j * D_RNN:(j + 1) * D_RNN] = gate
        a_s[...] = a
        b_s[...] = m * (ig * xc)
        _scan_rows(a_s, b_s, hs_ref, hcar, tm, reverse=False)
        y_ref[:, 512:1024] = (_gelu_and_grad(p_ref[:, 768:1280])[0] * hs_ref[...]).astype(BF16)

        ext_q[0:HALO, :] = ext_q[pl.ds(tm, HALO), :]
        ext_u[0:HALO, :] = ext_u[pl.ds(tm, HALO), :]
        ext_x[0:HALO, :] = ext_x[pl.ds(tm, HALO), :]

    ws = [mw[k] for k in _MIX_W]
    row = lambda w: pl.BlockSpec((tm, w), lambda i: (i, 0))
    return pl.pallas_call(
        body, name="mixer_fwd", grid=(t // tm,),
        in_specs=[row(D_MODEL), _full(g), _resident(w_in)] + [_full(w) for w in ws],
        out_specs=[row(D_MODEL), row(D_IN), row(D_MODEL), row(D_RNN), row(D_CONV), row(D_RNN), row(4 * D_RNN)],
        out_shape=[jax.ShapeDtypeStruct((t, D_MODEL), BF16), jax.ShapeDtypeStruct((t, D_IN), F32),
                   jax.ShapeDtypeStruct((t, D_MODEL), BF16), jax.ShapeDtypeStruct((t, D_RNN), F32),
                   jax.ShapeDtypeStruct((t, D_CONV), F32), jax.ShapeDtypeStruct((t, D_RNN), F32),
                   jax.ShapeDtypeStruct((t, 4 * D_RNN), F32)],
        scratch_shapes=[pltpu.VMEM((HALO + tm, D_POOL), F32), pltpu.VMEM((HALO + tm, D_CONV), F32),
                        pltpu.VMEM((HALO + tm, D_RNN), F32), pltpu.VMEM((HALO + tm, D_POOL), F32),
                        pltpu.VMEM((HALO + tm, D_POOL), F32), pltpu.VMEM((tm + 8, D_CONV), F32),
                        pltpu.VMEM((tm, D_RNN), F32), pltpu.VMEM((tm, D_RNN), F32), pltpu.VMEM((8, D_RNN), F32)],
        compiler_params=_params("arbitrary"),
    )(h, g, w_in, *ws)


_MIX_G = (("wp", (D_POOL, D_POOL)), ("psc", (1, D_POOL)), ("dww", (32, 8, D_CONV)), ("dwb", (1, D_CONV)),
          ("lng", (1, D_CONV)), ("lnb", (1, D_CONV)), ("wpw", (D_CONV, D_CONV)), ("cw", (8, D_RNN)),
          ("cb", (1, D_RNN)), ("wa", (D_RNN, D_RNN)), ("ba", (1, D_RNN)), ("wx", (D_RNN, D_RNN)),
          ("bx", (1, D_RNN)), ("lam", (1, D_RNN)), ("g1", (1, D_MODEL)), ("wout", (D_MODEL, D_MODEL)),
          ("win", (D_IN, D_MODEL)))


def _mixer_bwd(p, dh1, hs, conv, xc, gates, h0, y, u, g1, w_out, w_in, mw, deps=()):
    t = p.shape[0]
    tm = _row_tile(t, TM_MIX)
    nt = t // tm
    hb = tm // HALO

    def body(p_ref, ph_ref, dh1_ref, hs_ref, hsh_ref, conv_ref, xc_ref, gates_ref, h0_ref, y_ref, u_ref, g1_ref, wout_ref, win_ref,
             wp, psc, dww, dwb, lng, lnb, wpw, cw, cb, wa, ba, wx, bx, lam,
             dh0_ref, g_wp, g_psc, g_dww, g_dwb, g_lng, g_lnb, g_wpw, g_cw, g_cb, g_wa, g_ba, g_wx, g_bx, g_lam, g_g1,
             g_wout, g_win,
             ext_q, ext_u, ext_x, ext_h, ee, dc_s, dx_s, tmp_a, tmp_b, zbuf, d_pad, a_s, b_s, g_s, gcar, dy_ref, dp_s):
        step = pl.program_id(0)
        i = nt - 1 - step
        grads = (g_wp, g_psc, g_dww, g_dwb, g_lng, g_lnb, g_wpw, g_cw, g_cb, g_wa, g_ba, g_wx, g_bx, g_lam, g_g1, g_wout, g_win)
        dy_ref[...] = dh1_ref[...].astype(BF16)
        dy_cols = lambda lo, hi: _dot_nt(dy_ref[...], wout_ref[lo:hi, :])

        @pl.when(step == 0)
        def _():
            for gr in grads:
                gr[...] = jnp.zeros(gr.shape, F32)
            ee[pl.ds(tm, HALO), :] = jnp.zeros((HALO, D_POOL), F32)
            dc_s[pl.ds(tm, HALO), :] = jnp.zeros((HALO, D_CONV), F32)
            dx_s[pl.ds(tm, HALO), :] = jnp.zeros((HALO, D_RNN), F32)
            d_pad[0:8, :] = jnp.zeros((8, D_CONV), F32)
            d_pad[pl.ds(tm + 8, 8), :] = jnp.zeros((8, D_CONV), F32)
            gcar[...] = jnp.zeros((8, D_RNN), F32)

        hm = jnp.where(i == 0, 0.0, 1.0)

        ext_q[0:HALO, :] = ph_ref[:, 0:256] * hm
        ext_q[pl.ds(HALO, tm), :] = p_ref[:, 0:256]
        pooled, lane, cnt = _pool_fwd(ext_q, tmp_a, tmp_b, tm, i * tm)
        pooled_b = pooled.astype(BF16)
        dya = dy_cols(0, 256)
        g_psc[...] += _colsum(dya * _dot(pooled_b, wp[...]))
        dmixed_b = (dya * psc[...]).astype(BF16)
        dpooled = _dot_nt(dmixed_b, wp[...])
        g_wp[...] += _dot_tn(pooled_b, dmixed_b)
        ee[0:tm, :] = dpooled / cnt
        dp_s[:, 0:256] = _lane_sel(lane, *_window_sums_ahead(ee, tmp_a, tmp_b, tm)) - dpooled
        ee[pl.ds(tm, HALO), :] = ee[0:HALO, :]

        g_wout[...] += _dot_tn(y_ref[...], dy_ref[...])

        v = p_ref[:, 256:512]
        s = _sig(p_ref[:, 512:768])
        ext_u[0:HALO, :] = ph_ref[:, 256:512] * _sig(ph_ref[:, 512:768]) * hm
        ext_u[pl.ds(HALO, tm), :] = v * s
        z, rstd, l, sl, act = _ln_silu(conv_ref[...], lng[...], lnb[...])
        dyb_b = dy_cols(256, 512).astype(BF16)
        dact = _dot_nt(dyb_b, wpw[...])
        g_wpw[...] += _dot_tn(act.astype(BF16), dyb_b)
        dl = dact * (sl * (1.0 + l * (1.0 - sl)))
        g_lng[...] += _colsum(dl * z)
        g_lnb[...] += _colsum(dl)
        dz = dl * lng[...]
        dc = rstd * (dz - jnp.mean(dz, axis=-1, keepdims=True) - z * jnp.mean(dz * z, axis=-1, keepdims=True))
        g_dwb[...] += _colsum(dc)
        dc_s[0:tm, :] = dc
        d_pad[pl.ds(8, tm), :] = dc
        _tap_grads(d_pad, ext_u, _CONV_OFFS, tm, g_dww, zbuf)
        du0 = _taps(dc_s, lambda j: dww[CONV_K - 1 - j:CONV_K - j, :], list(range(CONV_K)), tm, zbuf)
        dp_s[:, 256:512] = du0 * s
        dp_s[:, 512:768] = du0 * v * (s * (1.0 - s))
        dc_s[pl.ds(tm, HALO), :] = dc_s[0:HALO, :]

        ext_x[0:HALO, :] = ph_ref[:, 1280:1792] * hm
        ext_x[pl.ds(HALO, tm), :] = p_ref[:, 1280:1792]
        xc = xc_ref[...]
        xcb = xc.astype(BF16)
        r, ig, a, m = (gates_ref[:, j * D_RNN:(j + 1) * D_RNN] for j in range(4))
        sp = _softplus_neg(lam[...])
        ext_h[0:HALO, :] = hsh_ref[...] * hm
        ext_h[pl.ds(HALO, tm), :] = hs_ref[...]
        dyc = dy_cols(512, 1024)
        gl, dgl = _gelu_and_grad(p_ref[:, 768:1280])
        dp_s[:, 768:1280] = dyc * hs_ref[...] * dgl
        a_s[...] = a
        b_s[...] = dyc * gl
        _scan_rows(a_s, b_s, g_s, gcar, tm, reverse=True)
        g = g_s[...]
        da = g * ext_h[pl.ds(HALO - 1, tm), :]
        dm = g * (ig * xc)
        dig = g * (m * xc)
        dlog_a = da * a - dm * (a * a) / m
        g_lam[...] += _colsum(dlog_a * (-RG_C * r)) * (-_sig(-lam[...]))
        dra = (dlog_a * (-RG_C * sp)) * (r * (1.0 - r))
        dia = dig * (ig * (1.0 - ig))
        g_ba[...] += _colsum(dra)
        g_bx[...] += _colsum(dia)
        dra_b = dra.astype(BF16)
        dia_b = dia.astype(BF16)
        dxc = g * (m * ig) + _dot_nt(dra_b, wa[...]) + _dot_nt(dia_b, wx[...])
        g_wa[...] += _dot_tn(xcb, dra_b)
        g_wx[...] += _dot_tn(xcb, dia_b)
        g_cb[...] += _colsum(dxc)
        dx_s[0:tm, :] = dxc
        for k in range(RG_CONV_K):
            g_cw[k:k + 1, :] += _colsum(dxc * ext_x[pl.ds(HALO - (RG_CONV_K - 1) + k, tm), :])
        dxin = cw[RG_CONV_K - 1:RG_CONV_K, :] * dxc
        for j in range(1, RG_CONV_K):
            dxin = dxin + cw[RG_CONV_K - 1 - j:RG_CONV_K - j, :] * dx_s[pl.ds(j, tm), :]
        dp_s[:, 1280:1792] = dxin
        dx_s[pl.ds(tm, HALO), :] = dx_s[0:HALO, :]

        dpb = dp_s[...].astype(BF16)
        g_win[...] += _dot_tn(dpb, u_ref[...])
        du = _dot(dpb, win_ref[...])
        r, n = _rms(h0_ref[...])
        g_g1[...] += _colsum(du * n)
        dh0_ref[...] = dh1_ref[...] + _rms_bwd(du, n, r, g1_ref[...])

    ws = [mw[k] for k in _MIX_W]
    tile = lambda w: pl.BlockSpec((tm, w), lambda s: (nt - 1 - s, 0))
    halo = lambda w: pl.BlockSpec((HALO, w), lambda s: (jnp.maximum((nt - 1 - s) * hb - 1, 0), 0))
    outs = pl.pallas_call(
        _after(body, 14 + len(ws), deps), name="mixer_bwd", grid=(nt,),
        in_specs=[tile(D_IN), halo(D_IN), tile(D_MODEL), tile(D_RNN), halo(D_RNN), tile(D_CONV), tile(D_RNN), tile(4 * D_RNN),
                  tile(D_MODEL), tile(D_MODEL), tile(D_MODEL), _full(g1),
                  _resident(w_out), _resident(w_in)] + [_full(w) for w in ws] + [ANY] * len(deps),
        out_specs=[tile(D_MODEL)] + [pl.BlockSpec(shp, lambda s, nd=len(shp): (0,) * nd) for _, shp in _MIX_G],
        out_shape=[jax.ShapeDtypeStruct((t, D_MODEL), F32)]
        + [jax.ShapeDtypeStruct(shp, F32) for _, shp in _MIX_G],
        scratch_shapes=[pltpu.VMEM((HALO + tm, D_POOL), F32), pltpu.VMEM((HALO + tm, D_CONV), F32),
                        pltpu.VMEM((HALO + tm, D_RNN), F32), pltpu.VMEM((HALO + tm, D_RNN), F32),
                        pltpu.VMEM((tm + HALO, D_POOL), F32), pltpu.VMEM((tm + HALO, D_CONV), F32),
                        pltpu.VMEM((tm + HALO, D_RNN), F32), pltpu.VMEM((HALO + tm, D_POOL), F32),
                        pltpu.VMEM((HALO + tm, D_POOL), F32), pltpu.VMEM((tm + 8, D_CONV), F32),
                        pltpu.VMEM((tm + 16, D_CONV), F32), pltpu.VMEM((tm, D_RNN), F32),
                        pltpu.VMEM((tm, D_RNN), F32), pltpu.VMEM((tm, D_RNN), F32), pltpu.VMEM((8, D_RNN), F32),
                        pltpu.VMEM((tm, D_MODEL), BF16), pltpu.VMEM((tm, D_IN), F32)],
        compiler_params=_params("arbitrary"),
    )(p, p, dh1, hs, hs, conv, xc, gates, h0, y, u, g1, w_out, w_in, *ws, *deps)
    return outs[0], {k: o for (k, _), o in zip(_MIX_G, outs[1:])}


def _mid_fwd(y, h0, w_out, g, w_up):
    t = h0.shape[0]
    tm = _row_tile(t, TM_MAT)

    def body(y_ref, h0_ref, wo_ref, g_ref, wu_ref, h1_ref, u2_ref, f_ref):
        h1 = h0_ref[...] + _dot(y_ref[...], wo_ref[...])
        h1_ref[...] = h1
        u2 = (_rms(h1)[1] * g_ref[...]).astype(BF16)
        u2_ref[...] = u2
        for c in range(D_FF // FF_CHUNK):
            f_ref[:, c * FF_CHUNK:(c + 1) * FF_CHUNK] = _dot(u2, wu_ref[c]).astype(BF16)

    row = lambda w: pl.BlockSpec((tm, w), lambda i: (i, 0))
    return pl.pallas_call(
        body, name="mid_fwd", grid=(t // tm,),
        in_specs=[row(D_MODEL), row(D_MODEL), _resident(w_out), _full(g), _resident(w_up)],
        out_specs=[row(D_MODEL), row(D_MODEL), row(D_FF)],
        out_shape=[jax.ShapeDtypeStruct((t, D_MODEL), F32), jax.ShapeDtypeStruct((t, D_MODEL), BF16),
                   jax.ShapeDtypeStruct((t, D_FF), BF16)],
        compiler_params=_params("parallel"),
    )(y, h0, w_out, g, w_up)


def _down_proj(f_ref, h1_ref, wd_ref):
    acc = h1_ref[...]
    for c in range(D_FF // FF_CHUNK):
        cols = slice(c * FF_CHUNK, (c + 1) * FF_CHUNK)
        a = jnp.square(jnp.maximum(f_ref[:, cols].astype(F32), 0.0)).astype(BF16)
        acc = acc + _dot(a, wd_ref[cols, :])
    return acc


def _down_fwd(f, h1, w_down):
    t = h1.shape[0]
    tm = _row_tile(t, TM_MAT)

    def body(f_ref, h1_ref, wd_ref, h2_ref):
        h2_ref[...] = _down_proj(f_ref, h1_ref, wd_ref)

    row = lambda w: pl.BlockSpec((tm, w), lambda i: (i, 0))
    return pl.pallas_call(
        body, name="down_fwd", grid=(t // tm,),
        in_specs=[row(D_FF), row(D_MODEL), _resident(w_down)], out_specs=row(D_MODEL),
        out_shape=jax.ShapeDtypeStruct((t, D_MODEL), F32),
        compiler_params=_params("parallel"),
    )(f, h1, w_down)


def _down_fwd_loss(f, h1, w_down, g, tgt, t_real):
    t = h1.shape[0]
    tm = _row_tile(t, TM_MAT)

    def body(f_ref, h1_ref, wd_ref, g_ref, tgt_ref, loss_ref, dh_ref, dg_ref):
        i = pl.program_id(0)

        @pl.when(i == 0)
        def _():
            loss_ref[...] = jnp.zeros(loss_ref.shape, F32)
            dg_ref[...] = jnp.zeros(dg_ref.shape, F32)

        r, n = _rms(_down_proj(f_ref, h1_ref, wd_ref))
        row = lax.broadcasted_iota(jnp.int32, (tm, 1), 0) + i * tm
        valid = jnp.logical_and(row >= N_META, row < t_real)
        diff = jnp.where(valid, n * g_ref[...] - tgt_ref[...], 0.0)
        loss_ref[...] += 0.5 * jnp.sum(jnp.mean(diff * diff, axis=-1, keepdims=True))
        dy = diff * (1.0 / D_MODEL)
        dg_ref[...] += _colsum(dy * n)
        dh_ref[...] = _rms_bwd(dy, n, r, g_ref[...])

    row = lambda w: pl.BlockSpec((tm, w), lambda i: (i, 0))
    return pl.pallas_call(
        body, name="down_fwd_loss", grid=(t // tm,),
        in_specs=[row(D_FF), row(D_MODEL), _resident(w_down), _full(g), row(D_MODEL)],
        out_specs=[pl.BlockSpec((8, 128), lambda i: (0, 0)), row(D_MODEL), pl.BlockSpec((1, D_MODEL), lambda i: (0, 0))],
        out_shape=[jax.ShapeDtypeStruct((8, 128), F32), jax.ShapeDtypeStruct((t, D_MODEL), F32),
                   jax.ShapeDtypeStruct((1, D_MODEL), F32)],
        compiler_params=_params("arbitrary"),
    )(f, h1, w_down, g, tgt)


def _mlp_bwd(dh2, f, h1, g, w_up, w_down, deps=()):
    t = dh2.shape[0]
    tm = _row_tile(t, TM_MLP_BWD)

    def body(dh2_ref, f_ref, wd_ref, wu_ref, h1_ref, g_ref, df_ref, dh1_ref, dg_ref):
        @pl.when(pl.program_id(0) == 0)
        def _():
            dg_ref[...] = jnp.zeros(dg_ref.shape, F32)

        dh2 = dh2_ref[...]
        dhb = dh2.astype(BF16)
        du2 = None
        for c in range(D_FF // FF_CHUNK):
            cols = slice(c * FF_CHUNK, (c + 1) * FF_CHUNK)
            dact = _dot_nt(dhb, wd_ref[c])
            df = (dact * (2.0 * jnp.maximum(f_ref[:, cols].astype(F32), 0.0))).astype(BF16)
            df_ref[:, cols] = df
            part = _dot_nt(df, wu_ref[c])
            du2 = part if du2 is None else du2 + part
        r, n = _rms(h1_ref[...])
        dg_ref[...] += _colsum(du2 * n)
        dh1_ref[...] = dh2 + _rms_bwd(du2, n, r, g_ref[...])

    row = lambda w: pl.BlockSpec((tm, w), lambda i: (i, 0))
    return pl.pallas_call(
        _after(body, 6, deps), name="mlp_bwd", grid=(t // tm,),
        in_specs=[row(D_MODEL), row(D_FF), _resident(w_down), _resident(w_up), row(D_MODEL), _full(g)] + [ANY] * len(deps),
        out_specs=[row(D_FF), row(D_MODEL), pl.BlockSpec((1, D_MODEL), lambda i: (0, 0))],
        out_shape=[jax.ShapeDtypeStruct((t, D_FF), BF16), jax.ShapeDtypeStruct((t, D_MODEL), F32),
                   jax.ShapeDtypeStruct((1, D_MODEL), F32)],
        compiler_params=_params("arbitrary"),
    )(dh2, f, w_down, w_up, h1, g, *deps)


def _tn_matmul(a, b, kc, nc, relu2, name, deps=()):
    t, k = a.shape
    n = b.shape[1]
    tt = _row_tile(t, TM_MAT)
    gk, gn = k // kc, n // nc

    def body(a_ref, b_ref, o_ref):
        @pl.when(pl.program_id(2) == 0)
        def _():
            o_ref[...] = jnp.zeros(o_ref.shape, F32)

        av = a_ref[...]
        if relu2:
            av = jnp.square(jnp.maximum(av.astype(F32), 0.0))
        o_ref[...] += _dot_tn(av.astype(BF16), b_ref[...].astype(BF16))

    return pl.pallas_call(
        _after(body, 2, deps), name=name, grid=(gk, gn, t // tt),
        in_specs=[pl.BlockSpec((tt, kc), lambda ik, jn, it: (it, ik)), pl.BlockSpec((tt, nc), lambda ik, jn, it: (it, jn))]
        + [ANY] * len(deps),
        out_specs=pl.BlockSpec((None, kc, nc), lambda ik, jn, it: (ik * gn + jn, 0, 0)),
        out_shape=jax.ShapeDtypeStruct((gk * gn, kc, nc), F32),
        compiler_params=_params("parallel", "parallel", "arbitrary"),
    )(a, b, *deps)


def _block_diag(blocks):
    nb, hd, _ = blocks.shape
    eye = jnp.eye(nb, dtype=blocks.dtype)
    return (blocks[:, :, None, :] * eye[:, None, :, None]).reshape(nb * hd, nb * hd)


def _diag_blocks(m, nb):
    hd = m.shape[0] // nb
    eye = jnp.eye(nb, dtype=m.dtype)
    return jnp.sum(m.reshape(nb, hd, nb, hd) * eye[:, None, :, None], axis=2)


def _mixer_weights(w, l):
    row = lambda a: a.reshape(1, -1)
    return dict(
        wp=_block_diag(w["pool_w"][l]).astype(BF16), psc=row(w["pool_scale"][l]),
        dww=jnp.pad(w["convb_dw_w"][l], ((0, 32 - CONV_K), (0, 0))), dwb=row(w["convb_dw_b"][l]),
        lng=row(w["convb_ln_g"][l]), lnb=row(w["convb_ln_b"][l]), wpw=w["convb_pw_w"][l].astype(BF16),
        cw=jnp.pad(w["rg_conv_w"][l], ((0, 8 - RG_CONV_K), (0, 0))), cb=row(w["rg_conv_b"][l]),
        wa=_block_diag(w["rg_w_a"][l]).astype(BF16), ba=row(w["rg_b_a"][l]),
        wx=_block_diag(w["rg_w_x"][l]).astype(BF16), bx=row(w["rg_b_x"][l]), lam=row(w["rg_lambda"][l]))


def _local_step(h, tgt, t_real, w, fetch, hooks):
    depth = 2
    saved = []
    big = []
    for l in range(depth):
        mw = _mixer_weights(w, l)
        g1 = w["mix_norm_g"][l].reshape(1, -1)
        g2 = w["mlp_norm_g"][l].reshape(1, -1)
        wl = dict(w_in=fetch(l, "w_in", h))
        y, p, u, hs, conv, xc, gates = _mixer_fwd(h, g1, wl["w_in"], mw)
        wl["w_out"], wl["w_up"] = fetch(l, "w_out", y), fetch(l, "w_up", y)
        h1, u2, f = _mid_fwd(y, h, wl["w_out"], g2, wl["w_up"])
        wl["w_down"] = fetch(l, "w_down", f)
        if l == depth - 1:
            loss, dh, dgf = _down_fwd_loss(f, h1, wl["w_down"].reshape(D_FF, D_MODEL), w["final_norm_g"].reshape(1, -1), tgt,
                                           t_real)
            h2 = None
        else:
            h2 = _down_fwd(f, h1, wl["w_down"].reshape(D_FF, D_MODEL))
        saved.append(dict(mw=mw, g1=g1, g2=g2, h0=h, p=p, u=u, y=y, hs=hs, conv=conv, xc=xc, gates=gates, h1=h1, u2=u2, f=f))
        big.append(wl)
        h = h2

    gs = {k: [None] * depth for k in ("mix_norm_g", "mlp_norm_g", "pool_w", "pool_scale", "convb_dw_w", "convb_dw_b",
                                      "convb_ln_g", "convb_ln_b", "convb_pw_w", "rg_conv_w", "rg_conv_b", "rg_w_a",
                                      "rg_b_a", "rg_w_x", "rg_b_x", "rg_lambda")}
    deps = ()
    for l in reversed(range(depth)):
        s, wl = saved[l], big[l]
        df, dh1, dg2 = _mlp_bwd(dh, s["f"], s["h1"], s["g2"], wl["w_up"], wl["w_down"], deps)
        deps = hooks.point(l, "mlp_bwd", dh1)
        g_down = _tn_matmul(s["f"], dh, FF_CHUNK, D_MODEL, True, "dw_down", deps)
        hooks.grad(l, "w_down", g_down)
        deps = hooks.point(l, "dw_down", g_down)
        g_up = _tn_matmul(s["u2"], df, D_MODEL, FF_CHUNK, False, "dw_up", deps)
        hooks.grad(l, "w_up", g_up)
        deps = hooks.point(l, "dw_up", g_up)
        dh, mg = _mixer_bwd(s["p"], dh1, s["hs"], s["conv"], s["xc"], s["gates"], s["h0"], s["y"], s["u"], s["g1"],
                               wl["w_out"], wl["w_in"], s["mw"], deps)
        hooks.grad(l, "w_out", mg["wout"].reshape(N_CHIPS, D_MODEL // N_CHIPS, D_MODEL))
        hooks.grad(l, "w_in", mg["win"].reshape(N_CHIPS, D_IN // N_CHIPS, D_MODEL))
        deps = hooks.point(l, "mixer_bwd", dh)
        gs["mix_norm_g"][l] = mg["g1"][0]
        gs["mlp_norm_g"][l] = dg2[0]
        gs["pool_w"][l] = _diag_blocks(mg["wp"], D_POOL // POOL_GW)
        gs["pool_scale"][l] = mg["psc"][0]
        gs["convb_dw_w"][l] = jnp.sum(mg["dww"][:CONV_K], axis=1)
        gs["convb_dw_b"][l] = mg["dwb"][0]
        gs["convb_ln_g"][l] = mg["lng"][0]
        gs["convb_ln_b"][l] = mg["lnb"][0]
        gs["convb_pw_w"][l] = mg["wpw"]
        gs["rg_conv_w"][l] = mg["cw"][:RG_CONV_K]
        gs["rg_conv_b"][l] = mg["cb"][0]
        gs["rg_w_a"][l] = _diag_blocks(mg["wa"], D_RNN // RG_HD)
        gs["rg_b_a"][l] = mg["ba"][0]
        gs["rg_w_x"][l] = _diag_blocks(mg["wx"], D_RNN // RG_HD)
        gs["rg_b_x"][l] = mg["bx"][0]
        gs["rg_lambda"][l] = mg["lam"][0]
    gsmall = {k: jnp.stack(v) for k, v in gs.items()}
    gsmall["final_norm_g"] = dgf[0]
    return loss[0, 0], dh, gsmall


def _place():
    return lax.axis_index("x"), lax.axis_index("y"), lax.axis_index("c")


def _other_chips(x, y):
    return [(1 - x, y), (x, 1 - y), (1 - x, 1 - y)]


HBM_SPEC = pl.BlockSpec(memory_space=pltpu.HBM)
SEM_SPEC = pl.BlockSpec(memory_space=pltpu.SEMAPHORE)
DATAFLOW = pltpu.SideEffectType.DATAFLOW_SIDE_EFFECTING


def _gather_copies(src_refs, land_refs, send_sem, recv_sem, first):
    x, y, c = _place()
    me = 2 * x + y
    out = []
    for n in range(len(src_refs)):
        for j, (px, py) in enumerate(_other_chips(x, y)):
            out.append(pltpu.make_async_remote_copy(src_refs[n], land_refs[n].at[me], send_sem.at[first + 3 * n + j],
                                                    recv_sem.at[first + 3 * n + j], device_id=(px, py, c), device_id_type=MESH))
    return out


def _gather_start(groups, me):
    srcs = [pltpu.with_memory_space_constraint(s, pltpu.HBM) for g in groups for s in g]
    lands = [pltpu.with_memory_space_constraint(
        lax.dynamic_update_slice(jnp.zeros((N_CHIPS,) + s.shape, s.dtype), s[None], (me,) + (0,) * s.ndim), pltpu.HBM)
        for g in groups for s in g]
    n, ng = len(srcs), len(groups)
    first = [sum(len(g) for g in groups[:i]) for i in range(ng)]

    def body(*refs):
        src_refs, land_refs = refs[:n], refs[n:2 * n]
        sems = refs[2 * n:2 * n + 2 * ng]
        token = refs[-1]
        for gi, g in enumerate(groups):
            lo, hi = first[gi], first[gi] + len(g)
            for cp in _gather_copies(src_refs[lo:hi], land_refs[lo:hi], sems[2 * gi], sems[2 * gi + 1], 0):
                cp.start()
        token[...] = jnp.zeros(token.shape, token.dtype)

    sem_shapes = [pltpu.SemaphoreType.DMA((3 * len(g),)) for g in groups for _ in range(2)]
    outs = pl.pallas_call(
        body, name="gather_start",
        out_shape=sem_shapes + [pltpu.HBM(a.shape, a.dtype) for a in srcs + lands] + [jax.ShapeDtypeStruct((8, 128), F32)],
        in_specs=[HBM_SPEC] * (2 * n),
        out_specs=[SEM_SPEC] * (2 * ng) + [HBM_SPEC] * (2 * n) + [pl.BlockSpec(memory_space=pltpu.VMEM)],
        input_output_aliases={i: 2 * ng + i for i in range(2 * n)},
        compiler_params=pltpu.CompilerParams(has_side_effects=DATAFLOW),
    )(*srcs, *lands)
    sems, thru, token = outs[:2 * ng], outs[2 * ng:2 * ng + 2 * n], outs[-1]
    state = []
    for gi, g in enumerate(groups):
        lo, hi = first[gi], first[gi] + len(g)
        state.append((sems[2 * gi], sems[2 * gi + 1], thru[lo:hi], thru[n + lo:n + hi]))
    return state, token


def _gather_wait(state, after, name):
    send_sem, recv_sem, srcs, lands = state
    n = len(srcs)

    def body(*refs):
        src_refs, land_refs = refs[:n], refs[n:2 * n]
        send, recv = refs[2 * n], refs[2 * n + 1]
        for cp in _gather_copies(src_refs, land_refs, send, recv, 0):
            cp.wait_send()
            cp.wait_recv()

    outs = pl.pallas_call(
        body, name=name,
        out_shape=[pltpu.HBM(a.shape, a.dtype) for a in list(srcs) + list(lands)],
        in_specs=[HBM_SPEC] * (2 * n) + [SEM_SPEC, SEM_SPEC, ANY],
        out_specs=[HBM_SPEC] * (2 * n),
        input_output_aliases={i: i for i in range(2 * n)},
        compiler_params=pltpu.CompilerParams(has_side_effects=DATAFLOW),
    )(*srcs, *lands, send_sem, recv_sem, after)
    return outs[n:]


def _add_halves(g, recv, c1):
    nk, r, cd = g.shape
    r2 = r // 2

    def body(c_ref, g_ref, r_ref, pab_ref):
        pab_ref[...] = (g_ref[...] + r_ref[...]).astype(BF16)

    blk = pl.BlockSpec((None, r2, cd), lambda k, c_ref: (k, 0, 0))
    return pl.pallas_call(
        body, name="rs_add_halves",
        grid_spec=pltpu.PrefetchScalarGridSpec(
            num_scalar_prefetch=1, grid=(nk,),
            in_specs=[pl.BlockSpec((None, r2, cd), lambda k, c_ref: (k, c_ref[0], 0)), blk], out_specs=blk),
        out_shape=jax.ShapeDtypeStruct((nk, r2, cd), BF16),
        compiler_params=_params("parallel"),
    )(c1, g, recv)


def _sum_partials(g, recv_sibling, recv_chips, c_me):
    nk, r, cd = g.shape
    r2 = r // 2

    def body(cm_ref, g_ref, a_ref, r_ref, s_ref):
        own = g_ref[...] + a_ref[...]
        s_ref[...] = ((own + r_ref[0].astype(F32)) + r_ref[1].astype(F32)) + r_ref[2].astype(F32)

    return pl.pallas_call(
        body, name="rs_sum_partials",
        grid_spec=pltpu.PrefetchScalarGridSpec(
            num_scalar_prefetch=1, grid=(1,),
            in_specs=[pl.BlockSpec((None, r2, cd), lambda i, cm: (cm[1], cm[0], 0)),
                      pl.BlockSpec((None, r2, cd), lambda i, cm: (cm[1], 0, 0)),
                      pl.BlockSpec((3, r2, cd), lambda i, cm: (0, 0, 0))],
            out_specs=pl.BlockSpec((r2, cd), lambda i, cm: (0, 0))),
        out_shape=jax.ShapeDtypeStruct((r2, cd), F32),
        compiler_params=_params("arbitrary"),
    )(c_me, g, recv_sibling, recv_chips)


def _split_start(name, srcs, lands, ncopies, make_copies):
    srcs = [pltpu.with_memory_space_constraint(s, pltpu.HBM) for s in srcs]
    lands = [pltpu.with_memory_space_constraint(a, pltpu.HBM) for a in lands]
    n, m = len(srcs), len(lands)

    def body(*refs):
        src_refs, land_refs = refs[:n], refs[n:n + m]
        send, recv, token = refs[n + m], refs[n + m + 1], refs[-1]
        for cp in make_copies(src_refs, land_refs, send, recv):
            cp.start()
        token[...] = jnp.zeros(token.shape, token.dtype)

    outs = pl.pallas_call(
        body, name=name,
        out_shape=[pltpu.SemaphoreType.DMA((ncopies,)), pltpu.SemaphoreType.DMA((ncopies,))]
        + [pltpu.HBM(a.shape, a.dtype) for a in srcs + lands] + [jax.ShapeDtypeStruct((8, 128), F32)],
        in_specs=[HBM_SPEC] * (n + m),
        out_specs=[SEM_SPEC, SEM_SPEC] + [HBM_SPEC] * (n + m) + [pl.BlockSpec(memory_space=pltpu.VMEM)],
        input_output_aliases={i: 2 + i for i in range(n + m)},
        compiler_params=pltpu.CompilerParams(has_side_effects=DATAFLOW),
    )(*srcs, *lands)
    return (outs[0], outs[1], outs[2:2 + n], outs[2 + n:2 + n + m], make_copies), outs[-1]


def _split_wait(name, state, after):
    send_sem, recv_sem, srcs, lands, make_copies = state
    n, m = len(srcs), len(lands)

    def body(*refs):
        src_refs, land_refs = refs[:n], refs[n:n + m]
        for cp in make_copies(src_refs, land_refs, refs[n + m], refs[n + m + 1]):
            cp.wait_send()
            cp.wait_recv()

    outs = pl.pallas_call(
        body, name=name,
        out_shape=[pltpu.HBM(a.shape, a.dtype) for a in list(srcs) + list(lands)],
        in_specs=[HBM_SPEC] * (n + m) + [SEM_SPEC, SEM_SPEC, ANY],
        out_specs=[HBM_SPEC] * (n + m),
        input_output_aliases={i: i for i in range(n + m)},
        compiler_params=pltpu.CompilerParams(has_side_effects=DATAFLOW),
    )(*srcs, *lands, send_sem, recv_sem, after)
    return outs[:n], outs[n:]


def _copies_to_sibling(src_of):
    def make(src_refs, land_refs, send, recv):
        x, y, c = _place()
        return [pltpu.make_async_remote_copy(src_of(src_refs[i], c), land_refs[i], send.at[i], recv.at[i],
                                             device_id=(x, y, 1 - c), device_id_type=MESH) for i in range(len(src_refs))]
    return make


def _copies_to_chips(src_refs, land_refs, send, recv):
    x, y, c = _place()
    return [pltpu.make_async_remote_copy(src_refs[i].at[2 * px + py], land_refs[i].at[j], send.at[3 * i + j], recv.at[3 * i + j],
                                         device_id=(px, py, c), device_id_type=MESH)
            for i in range(len(src_refs)) for j, (px, py) in enumerate(_other_chips(x, y))]


def _other_half_rows(ref, c):
    r2 = ref.shape[1] // 2
    return ref.at[:, pl.ds(pl.multiple_of((1 - c) * r2, 8), r2)]


class _ReduceScatter:
    def __init__(self, tag, grads, c1, me1):
        self.tag, self.grads, self.c1, self.me1 = tag, grads, c1, me1

    def start(self):
        lands = [lax.empty((g.shape[0], g.shape[1] // 2, g.shape[2]), F32) for g in self.grads]
        self.state, token = _split_start("rs_%s_a_start" % self.tag, self.grads, lands, len(self.grads),
                                         _copies_to_sibling(_other_half_rows))
        return token

    def to_chips(self, after):
        self.halves = _split_wait("rs_%s_a_wait" % self.tag, self.state, after)
        pabs = [_add_halves(g, r, self.c1) for g, r in zip(*self.halves)]
        lands = [lax.empty((3,) + p.shape[1:], BF16) for p in pabs]
        self.state, token = _split_start("rs_%s_b_start" % self.tag, pabs, lands, 3 * len(pabs), _copies_to_chips)
        return token

    def to_sibling(self, after):
        _, recv = _split_wait("rs_%s_b_wait" % self.tag, self.state, after)
        c_me = jnp.concatenate([self.c1, self.me1])
        sums = [_sum_partials(g, ra, rb, c_me) for g, ra, rb in zip(*self.halves, recv)]
        lands = [lax.empty(s.shape, F32) for s in sums]
        self.state, token = _split_start("rs_%s_c_start" % self.tag, sums, lands, len(sums),
                                         _copies_to_sibling(lambda ref, c: ref))
        return token

    def finish(self, after):
        return list(zip(*_split_wait("rs_%s_c_wait" % self.tag, self.state, after)))


def _allreduce_small(vs):
    n = len(vs)

    def body(*refs):
        v_refs, out_refs, rbufs = refs[:n], refs[n:2 * n], refs[2 * n:3 * n]
        send_sems, recv_sems = refs[3 * n:]
        x, y, c = _place()
        for i in range(n):
            out_refs[i][...] = v_refs[i][...]
        for s, peer in enumerate([(x, y, 1 - c), (1 - x, y, c), (x, 1 - y, c)]):
            copies = [pltpu.make_async_remote_copy(out_refs[i], rbufs[i].at[s], send_sems.at[s * n + i], recv_sems.at[s * n + i],
                                                   device_id=peer, device_id_type=MESH) for i in range(n)]
            for cp in copies:
                cp.start()
            for cp in copies:
                cp.wait()
            for i in range(n):
                out_refs[i][...] = out_refs[i][...] + rbufs[i][s]

    vm = pl.BlockSpec(memory_space=pltpu.VMEM)
    return pl.pallas_call(
        body, name="allreduce_small", in_specs=[vm] * n, out_specs=[vm] * n,
        out_shape=[jax.ShapeDtypeStruct(v.shape, v.dtype) for v in vs],
        scratch_shapes=[pltpu.VMEM((3,) + v.shape, v.dtype) for v in vs]
        + [pltpu.SemaphoreType.DMA((3 * n,)), pltpu.SemaphoreType.DMA((3 * n,))],
        compiler_params=pltpu.CompilerParams(vmem_limit_bytes=VMEM_LIMIT),
    )(*vs)


def _adamw_math(w, g, m, v):
    m = ADAM_B1 * m + (1.0 - ADAM_B1) * g
    v = ADAM_B2 * v + (1.0 - ADAM_B2) * jnp.square(g)
    m_hat = m / (1.0 - ADAM_B1 ** ADAM_STEP)
    v_hat = v / (1.0 - ADAM_B2 ** ADAM_STEP)
    return -ADAM_LR * (m_hat / (jnp.sqrt(v_hat) + ADAM_EPS) + ADAM_WD * w), m, v


def _adamw_big_layer(layer, w, m, v, own, sib, c1, prev):
    _, r, cd = w.shape
    r2 = r // 2

    def body(c_ref, w_ref, m_ref, v_ref, own_ref, sib_ref, *rest):
        g_ref, d_ref, mo_ref, vo_ref, token = rest[-5:]
        g = jnp.where(pl.program_id(0) == c_ref[0], own_ref[...], sib_ref[...])
        g_ref[...] = g
        d_ref[...], mo_ref[...], vo_ref[...] = _adamw_math(w_ref[...], g, m_ref[...], v_ref[...])
        token[...] = jnp.zeros(token.shape, F32)

    blk = pl.BlockSpec((None, r2, cd), lambda hh, c_ref: (layer, hh, 0))
    half = pl.BlockSpec((r2, cd), lambda hh, c_ref: (0, 0))
    prev = () if prev is None else tuple(prev)
    outs = pl.pallas_call(
        body, name="adamw_big",
        grid_spec=pltpu.PrefetchScalarGridSpec(
            num_scalar_prefetch=1, grid=(2,), in_specs=[blk, blk, blk, half, half] + [ANY] * len(prev),
            out_specs=[blk] * 4 + [pl.BlockSpec((8, 128), lambda hh, c_ref: (0, 0))]),
        out_shape=[jax.ShapeDtypeStruct(w.shape, F32)] * 4 + [jax.ShapeDtypeStruct((8, 128), F32)],
        input_output_aliases={6 + i: i for i in range(len(prev))},
        compiler_params=_params("arbitrary"),
    )(c1, w, m, v, own, sib, *prev)
    return outs[:4], outs[4]


def _adamw_small(ws, gs, ms, vs):
    n = len(ws)

    def body(*refs):
        w_refs, g_refs, m_refs, v_refs = refs[:n], refs[n:2 * n], refs[2 * n:3 * n], refs[3 * n:4 * n]
        outs = refs[4 * n:]
        for i in range(n):
            outs[3 * i][...], outs[3 * i + 1][...], outs[3 * i + 2][...] = _adamw_math(
                w_refs[i][...], g_refs[i][...], m_refs[i][...], v_refs[i][...])

    vm = pl.BlockSpec(memory_space=pltpu.VMEM)
    outs = pl.pallas_call(
        body, name="adamw_small", in_specs=[vm] * (4 * n), out_specs=[vm] * (3 * n),
        out_shape=[jax.ShapeDtypeStruct(w.shape, F32) for w in ws for _ in range(3)],
        compiler_params=pltpu.CompilerParams(vmem_limit_bytes=VMEM_LIMIT),
    )(*ws, *gs, *ms, *vs)
    return [outs[3 * i:3 * i + 3] for i in range(n)]


LANES = 128
SUBLANES = 8
SHARDED_AXIS = {"meta_tokens": 1, "convb_dw_w": 2, "convb_pw_w": 1, "rg_conv_w": 2}


def _rows_of(size):
    return -(-size // (LANES * SUBLANES)) * SUBLANES


def _as_rows(a, rows=None):
    flat = a.reshape(-1)
    rows = _rows_of(flat.size) if rows is None else rows
    return jnp.pad(flat, (0, rows * LANES - flat.size)).reshape(rows, LANES)


class _GradientSchedule:
    GROUPS = {"l1": [(1, "w_down"), (1, "w_up"), (1, "w_out"), (1, "w_in")], "a0": [(0, "w_down"), (0, "w_up")],
              "b0": [(0, "w_out"), (0, "w_in")]}
    PLAN = {
        (1, "mixer_bwd"): [("l1", "start")],
        (0, "mlp_bwd"): [("l1", "to_chips")],
        (0, "dw_up"): [("a0", "start"), ("l1", "to_sibling"), ("a0", "to_chips"), ("l1", "finish")],
        (0, "mixer_bwd"): [("b0", "start"), ("a0", "to_sibling"), ("b0", "to_chips"), ("a0", "finish")],
    }

    def __init__(self, w, mom, var, c1, me1):
        self.w, self.mom, self.var, self.c1, self.me1 = w, mom, var, c1, me1
        self.grads, self.chains, self.out = {}, {}, {}

    def grad(self, layer, name, g):
        self.grads[layer, name] = g

    def point(self, layer, kernel_name, after):
        return self.run(self.PLAN.get((layer, kernel_name), ()), after) or (after,)

    def run(self, actions, after):
        deps = []
        for tag, stage in actions:
            if stage == "start":
                self.chains[tag] = _ReduceScatter(tag, [self.grads[lk] for lk in self.GROUPS[tag]], self.c1, self.me1)
                deps.append(self.chains[tag].start())
            elif stage == "finish":
                for (layer, k), (own, sib) in zip(self.GROUPS[tag], self.chains[tag].finish(after)):
                    self.out[k], token = _adamw_big_layer(layer, self.w[k], self.mom[k], self.var[k], own, sib, self.c1,
                                                          self.out.get(k))
                    deps.append(token)
            else:
                deps.append(getattr(self.chains[tag], stage)(after))
            after = deps[-1]
        self.last = after
        return tuple(deps)


def _from_shard_major(name, sm):
    if name == "meta_tokens":
        return sm.transpose(1, 0, 2).reshape(N_META, -1)
    if name == "convb_pw_w":
        return sm.transpose(1, 0, 2, 3).reshape(2, -1, D_CONV)
    return sm.transpose(1, 2, 0, 3).reshape(sm.shape[1], sm.shape[2], -1)


def kernel(x, meta_tokens, mix_norm_g, w_in, pool_w, pool_scale, convb_dw_w, convb_dw_b, convb_ln_g, convb_ln_b, convb_pw_w, rg_conv_w, rg_conv_b, rg_w_a, rg_b_a, rg_w_x, rg_b_x, rg_lambda, w_out, mlp_norm_g, w_up, w_down, final_norm_g, loss_target, m_meta_tokens, m_mix_norm_g, m_w_in, m_pool_w, m_pool_scale, m_convb_dw_w, m_convb_dw_b, m_convb_ln_g, m_convb_ln_b, m_convb_pw_w, m_rg_conv_w, m_rg_conv_b, m_rg_w_a, m_rg_b_a, m_rg_w_x, m_rg_b_x, m_rg_lambda, m_w_out, m_mlp_norm_g, m_w_up, m_w_down, m_final_norm_g, v_meta_tokens, v_mix_norm_g, v_w_in, v_pool_w, v_pool_scale, v_convb_dw_w, v_convb_dw_b, v_convb_ln_g, v_convb_ln_b, v_convb_pw_w, v_rg_conv_w, v_rg_conv_b, v_rg_w_a, v_rg_b_a, v_rg_w_x, v_rg_b_x, v_rg_lambda, v_w_out, v_mlp_norm_g, v_w_up, v_w_down, v_final_norm_g):
    given = dict(locals())
    w = {k: given[k] for k in WEIGHTS}
    mom = {k: given["m_" + k] for k in WEIGHTS}
    var = {k: given["v_" + k] for k in WEIGHTS}
    xi, yi, ci = _place()
    me1 = (2 * xi + yi).astype(jnp.int32).reshape(1)
    c1 = ci.astype(jnp.int32).reshape(1)

    small_rows = [_rows_of(w[k].size) for k in SMALL_SHARDED]
    small_pack = jnp.concatenate([_as_rows(w[k]) for k in SMALL_SHARDED])
    transposed = lambda d: {**d, "w_in": d["w_in"].transpose(0, 2, 1)}
    wt, momt, vart = transposed(w), transposed(mom), transposed(var)
    shard = lambda l, k: wt[k][l].astype(BF16)
    order = [[(0, "w_in"), "small"], [(0, "w_out"), (0, "w_up")], [(0, "w_down")], [(1, "w_in")], [(1, "w_out"), (1, "w_up")],
             [(1, "w_down")]]
    state, token = _gather_start([[small_pack if lk == "small" else shard(*lk) for lk in g] for g in order], me1[0])
    landed = {}

    def fetch(l, k, after):
        gi = [i for i, g in enumerate(order) if (l, k) in g][0]
        if gi not in landed:
            landed[gi] = _gather_wait(state[gi], after, "gather_wait_%d" % gi)
        raw = landed[gi][order[gi].index((l, k))]
        if k == "w_in":
            return raw.reshape(D_IN, D_MODEL)
        return raw.reshape(D_MODEL, D_MODEL) if k == "w_out" else raw

    seq = x.shape[1]
    t_real = N_META + seq
    t_pad = -(-t_real // ROW_ALIGN) * ROW_ALIGN
    tail = jnp.zeros((t_pad - t_real, D_MODEL), F32)
    front = jnp.zeros((N_META, D_MODEL), F32)
    h = jnp.concatenate([front + token[0, 0], x[0], tail])
    tgt = jnp.concatenate([front, loss_target[0], tail])
    landed[0] = _gather_wait(state[0], h, "gather_wait_0")
    wfull = {k: (w[k] + token[0, 0] if k in ("pool_w", "rg_w_a", "rg_w_x") else w[k]) for k in WEIGHTS}
    off = 0
    for k, rows in zip(SMALL_SHARDED, small_rows):
        sm = landed[0][1][:, off:off + rows].reshape(N_CHIPS, -1)[:, :w[k].size].reshape((N_CHIPS,) + w[k].shape)
        wfull[k] = _from_shard_major(k, sm)
        off += rows
    h = lax.dynamic_update_slice(h, wfull["meta_tokens"], (0, 0))
    sched = _GradientSchedule(wt, momt, vart, c1, me1)
    loss, dh, gsmall = _local_step(h, tgt, t_real, wfull, fetch, sched)
    grad_x = dh[N_META:t_real][None]
    gsmall["meta_tokens"] = dh[:N_META]

    names = SMALL_REPL + SMALL_SHARDED
    two_d = lambda a: a.reshape(1, -1) if a.ndim == 1 else a
    partial = [two_d(gsmall[k]) for k in names]
    partial[0] = partial[0] + sched.last[0, 0]
    summed = dict(zip(names, _allreduce_small(partial)))
    sched.run([("b0", "to_sibling")], summed[names[0]])
    for k in SMALL_SHARDED:
        ax = SHARDED_AXIS[k]
        summed[k] = lax.dynamic_slice_in_dim(summed[k], me1[0] * w[k].shape[ax], w[k].shape[ax], axis=ax)

    out = {}
    res = _adamw_small([two_d(w[k]) for k in names], [summed[k] for k in names], [two_d(mom[k]) for k in names],
                       [two_d(var[k]) for k in names])
    for k, (d, m2, v2) in zip(names, res):
        out[k] = tuple(o.reshape(w[k].shape) for o in (summed[k], d, m2, v2))
    sched.run([("b0", "finish")], res[0][0])
    out.update(sched.out)
    out["w_in"] = tuple(o.transpose(0, 2, 1) for o in out["w_in"])

    loss = lax.psum(loss, ("x", "y", "c"))
    return (loss, grad_x, *[out[k][0] for k in WEIGHTS], *[out[k][1] for k in WEIGHTS],
            *[out[k][2] for k in WEIGHTS], *[out[k][3] for k in WEIGHTS])
```

```python
import functools

import jax
import jax.numpy as jnp
from jax import lax
from jax.experimental import pallas as pl
from jax.experimental.pallas import tpu as pltpu

F32, BF16 = jnp.float32, jnp.bfloat16
MESH = pl.DeviceIdType.MESH
ANY = pl.BlockSpec(memory_space=pl.ANY)

D_MODEL = 1024
N_META = 16
D_POOL = 256
D_CONV = 256
D_RNN = 512
D_IN = D_POOL + 2 * D_CONV + 2 * D_RNN
D_FF = 4096
FF_CHUNK = 1024
POOL_GW = 64
CONV_K = 31
RG_CONV_K = 4
RG_HD = 64
RG_C = 8.0
EPS = 1e-6
ADAM_LR, ADAM_B1, ADAM_B2, ADAM_EPS, ADAM_WD, ADAM_STEP = 0.001, 0.9, 0.999, 1e-08, 0.01, 10

HALO = 32
ROW_ALIGN = 256
TM_MIX = 384
TM_MAT = 768
TM_MLP_BWD = 384
N_CHIPS = 4
VMEM_LIMIT = 56 * 1024 * 1024

BIG = ("w_in", "w_out", "w_up", "w_down")
SMALL_SHARDED = ("meta_tokens", "convb_dw_w", "convb_pw_w", "rg_conv_w")
SMALL_REPL = ("mix_norm_g", "pool_w", "pool_scale", "convb_dw_b", "convb_ln_g", "convb_ln_b", "rg_conv_b",
              "rg_w_a", "rg_b_a", "rg_w_x", "rg_b_x", "rg_lambda", "mlp_norm_g", "final_norm_g")
SMALL = SMALL_REPL + SMALL_SHARDED
WEIGHTS = ("meta_tokens", "mix_norm_g", "w_in", "pool_w", "pool_scale", "convb_dw_w", "convb_dw_b", "convb_ln_g",
           "convb_ln_b", "convb_pw_w", "rg_conv_w", "rg_conv_b", "rg_w_a", "rg_b_a", "rg_w_x", "rg_b_x",
           "rg_lambda", "w_out", "mlp_norm_g", "w_up", "w_down", "final_norm_g")


def _params(*sem):
    return pltpu.CompilerParams(dimension_semantics=sem, vmem_limit_bytes=VMEM_LIMIT)


def _row_tile(t, cap):
    best = None
    for tm in range(128, cap + 1, 128):
        if t % tm == 0:
            best = tm
    assert best is not None, (t, cap)
    return best


def _dot(a, b):
    return jnp.dot(a, b, preferred_element_type=F32)


def _dot_nt(a, b):
    return lax.dot_general(a, b, (((1,), (1,)), ((), ())), preferred_element_type=F32)


def _dot_tn(a, b):
    return lax.dot_general(a, b, (((0,), (0,)), ((), ())), preferred_element_type=F32)


def _rms(x):
    r = lax.rsqrt(jnp.mean(x * x, axis=-1, keepdims=True) + EPS)
    return r, x * r


def _rms_bwd(du, n, r, g):
    dn = du * g
    return r * (dn - n * jnp.mean(dn * n, axis=-1, keepdims=True))


def _sig(x):
    return jax.nn.sigmoid(x)


def _colsum(x):
    return jnp.sum(x, axis=0, keepdims=True)


def _one_minus_sq(a, log_a):
    x = 2.0 * log_a
    series = -x * (1.0 + x * (0.5 + x * (1.0 / 6)))
    return jnp.where(x > -0.01, series, 1.0 - a * a)


_GELU_K0 = 0.7978845608028654
_GELU_K1 = 0.044715


def _gelu_and_grad(x):
    th = jnp.tanh(_GELU_K0 * (x + _GELU_K1 * x * x * x))
    val = 0.5 * x * (1.0 + th)
    grad = 0.5 * (1.0 + th) + 0.5 * x * (1.0 - th * th) * _GELU_K0 * (1.0 + 3.0 * _GELU_K1 * x * x)
    return val, grad


def _full(a):
    nd = a.ndim
    return pl.BlockSpec(a.shape, lambda *_: (0,) * nd)


def _resident(a):
    nd = a.ndim
    return pl.BlockSpec(a.shape, lambda *_: (0,) * nd, pipeline_mode=pl.Buffered(1))


def _after(body, n_in, deps):
    def wrapped(*refs):
        return body(*refs[:n_in], *refs[n_in + len(deps):])
    return wrapped


def _lane_sel(lane, a2, a4, a8, a16):
    return jnp.where(lane < POOL_GW, a2, jnp.where(lane < 2 * POOL_GW, a4, jnp.where(lane < 3 * POOL_GW, a8, a16)))


def _window_sums_back(src, tmp_a, tmp_b, tm):
    n = HALO + tm
    rows = lambda ref, lo, back: ref[pl.ds(lo - back, n - lo), :]
    tmp_a[pl.ds(8, n - 8), :] = rows(src, 8, 0) + rows(src, 8, 1)
    tmp_b[pl.ds(16, n - 16), :] = rows(tmp_a, 16, 0) + rows(tmp_a, 16, 2)
    s2 = rows(tmp_a, HALO, 0)
    tmp_a[pl.ds(24, n - 24), :] = rows(tmp_b, 24, 0) + rows(tmp_b, 24, 4)
    s8 = rows(tmp_a, HALO, 0)
    return s2, rows(tmp_b, HALO, 0), s8, s8 + rows(tmp_a, HALO, 8)


def _window_sums_ahead(src, tmp_a, tmp_b, tm):
    rows = lambda ref, n, ahead: ref[pl.ds(ahead, n), :]
    tmp_a[pl.ds(0, tm + 24), :] = rows(src, tm + 24, 0) + rows(src, tm + 24, 1)
    tmp_b[pl.ds(0, tm + 16), :] = rows(tmp_a, tm + 16, 0) + rows(tmp_a, tm + 16, 2)
    s2 = rows(tmp_a, tm, 0)
    tmp_a[pl.ds(0, tm + 8), :] = rows(tmp_b, tm + 8, 0) + rows(tmp_b, tm + 8, 4)
    s8 = rows(tmp_a, tm, 0)
    return s2, rows(tmp_b, tm, 0), s8, s8 + rows(tmp_a, tm, 8)


def _pool_counts(tm, t0):
    lane = lax.broadcasted_iota(jnp.int32, (tm, D_POOL), 1)
    row = lax.broadcasted_iota(jnp.int32, (tm, D_POOL), 0) + t0
    cnt = jnp.minimum(row + 1, _lane_sel(lane, 2, 4, 8, 16)).astype(F32)
    return lane, cnt


def _pool_fwd(ext_q, tmp_a, tmp_b, tm, t0):
    lane, cnt = _pool_counts(tm, t0)
    q = ext_q[pl.ds(HALO, tm), :]
    pooled = _lane_sel(lane, *_window_sums_back(ext_q, tmp_a, tmp_b, tm)) / cnt - q
    return pooled, lane, cnt


def _taps(src, w_of, offs, tm, zbuf):
    acc = None
    for r in range(8):
        ks = [k for k in range(len(offs)) if offs[k] % 8 == r]
        if not ks:
            continue
        rows = tm + (8 if r else 0)
        z = w_of(ks[0]) * src[pl.ds(offs[ks[0]] - r, rows), :]
        for k in ks[1:]:
            z = z + w_of(k) * src[pl.ds(offs[k] - r, rows), :]
        if r:
            zbuf[...] = z
            z = zbuf[pl.ds(r, tm), :]
        acc = z if acc is None else acc + z
    return acc


def _tap_grads(d_pad, src, offs, tm, g_ref, zbuf):
    ch = src.shape[-1]
    for r in range(8):
        ks = [k for k in range(len(offs)) if offs[k] % 8 == r]
        if not ks:
            continue
        rows = tm + (8 if r else 0)
        if r:
            zbuf[...] = d_pad[pl.ds(8 - r, rows), :]
        for k in ks:
            d = zbuf[...] if r else d_pad[pl.ds(8, rows), :]
            prod = d * src[pl.ds(offs[k] - r, rows), :]
            g_ref[k] += jnp.sum(prod.reshape(rows // 8, 8, ch), axis=0)


_CONV_OFFS = [HALO - (CONV_K - 1) + k for k in range(CONV_K)]


def _conv_fwd(ext_u, dww_ref, dwb, tm, zbuf):
    return dwb + _taps(ext_u, lambda k: dww_ref[k:k + 1, :], _CONV_OFFS, tm, zbuf)


def _ln_silu(c, lng, lnb):
    mu = jnp.mean(c, axis=-1, keepdims=True)
    cc = c - mu
    rstd = lax.rsqrt(jnp.mean(cc * cc, axis=-1, keepdims=True) + EPS)
    z = cc * rstd
    l = z * lng + lnb
    sl = _sig(l)
    return z, rstd, l, sl, l * sl


def _rg_conv(ext_x, cw_ref, cb, tm):
    xc = cb + cw_ref[0:1, :] * ext_x[pl.ds(HALO - (RG_CONV_K - 1), tm), :]
    for k in range(1, RG_CONV_K):
        xc = xc + cw_ref[k:k + 1, :] * ext_x[pl.ds(HALO - (RG_CONV_K - 1) + k, tm), :]
    return xc


def _softplus_neg(lam):
    return jnp.maximum(-lam, 0.0) + jnp.log(1.0 + jnp.exp(-jnp.abs(lam)))


def _rg_gates(xc, wa, ba, wx, bx, lam):
    xcb = xc.astype(BF16)
    r = _sig(_dot(xcb, wa) + ba)
    ig = _sig(_dot(xcb, wx) + bx)
    log_a = (-RG_C * r) * _softplus_neg(lam)
    a = jnp.exp(log_a)
    return r, ig, a, jnp.sqrt(_one_minus_sq(a, log_a))


def _scan_rows(a_ref, b_ref, out_ref, carry, tm, reverse):
    rows = lax.broadcasted_iota(jnp.int32, (8, D_RNN), 0)
    ngrp = tm // 8

    def grp(gi, hb):
        st = pl.multiple_of((ngrp - 1 - gi if reverse else gi) * 8, 8)
        a8 = a_ref[pl.ds(st, 8), :]
        b8 = b_ref[pl.ds(st, 8), :]
        out = jnp.zeros((8, D_RNN), F32)
        for j in (range(7, -1, -1) if reverse else range(8)):
            aj = jnp.broadcast_to(a8[j:j + 1, :], (8, D_RNN))
            bj = jnp.broadcast_to(b8[j:j + 1, :], (8, D_RNN))
            if reverse:
                cur = bj + hb
                hb = aj * cur
            else:
                cur = aj * hb + bj
                hb = cur
            out = jnp.where(rows == j, cur, out)
        out_ref[pl.ds(st, 8), :] = out
        return hb

    carry[...] = lax.fori_loop(0, ngrp, grp, carry[...])


_MIX_W = ("wp", "psc", "dww", "dwb", "lng", "lnb", "wpw", "cw", "cb", "wa", "ba", "wx", "bx", "lam")


def _mixer_fwd(h, g, w_in, mw):
    t = h.shape[0]
    tm = _row_tile(t, TM_MIX)

    def body(h_ref, g_ref, win_ref, wp, psc, dww, dwb, lng, lnb, wpw, cw, cb, wa, ba, wx, bx, lam,
             y_ref, p_ref, u_ref, hs_ref, conv_ref, xc_ref, gates_ref, ext_q, ext_u, ext_x, tmp_a, tmp_b, zbuf, a_s, b_s, hcar):
        i = pl.program_id(0)

        @pl.when(i == 0)
        def _():
            ext_q[0:HALO, :] = jnp.zeros((HALO, D_POOL), F32)
            ext_u[0:HALO, :] = jnp.zeros((HALO, D_CONV), F32)
            ext_x[0:HALO, :] = jnp.zeros((HALO, D_RNN), F32)
            hcar[...] = jnp.zeros((8, D_RNN), F32)

        u = (_rms(h_ref[...])[1] * g_ref[...]).astype(BF16)
        u_ref[...] = u
        p_ref[...] = _dot_nt(u, win_ref[...])

        ext_q[pl.ds(HALO, tm), :] = p_ref[:, 0:256]
        pooled, _, _ = _pool_fwd(ext_q, tmp_a, tmp_b, tm, i * tm)
        y_ref[:, 0:256] = (_dot(pooled.astype(BF16), wp[...]) * psc[...]).astype(BF16)

        ext_u[pl.ds(HALO, tm), :] = p_ref[:, 256:512] * _sig(p_ref[:, 512:768])
        conv = _conv_fwd(ext_u, dww, dwb[...], tm, zbuf)
        conv_ref[...] = conv
        act = _ln_silu(conv, lng[...], lnb[...])[4]
        y_ref[:, 256:512] = _dot(act.astype(BF16), wpw[...]).astype(BF16)

        ext_x[pl.ds(HALO, tm), :] = p_ref[:, 1280:1792]
        xc = _rg_conv(ext_x, cw, cb[...], tm)
        xc_ref[...] = xc
        r, ig, a, m = _rg_gates(xc, wa[...], ba[...], wx[...], bx[...], lam[...])
        for j, gate in enumerate((r, ig, a, m)):
            gates_ref[:, j * D_RNN:(j + 1) * D_RNN] = gate
        a_s[...] = a
        b_s[...] = m * (ig * xc)
        _scan_rows(a_s, b_s, hs_ref, hcar, tm, reverse=False)
        y_ref[:, 512:1024] = (_gelu_and_grad(p_ref[:, 768:1280])[0] * hs_ref[...]).astype(BF16)

        ext_q[0:HALO, :] = ext_q[pl.ds(tm, HALO), :]
        ext_u[0:HALO, :] = ext_u[pl.ds(tm, HALO), :]
        ext_x[0:HALO, :] = ext_x[pl.ds(tm, HALO), :]

    ws = [mw[k] for k in _MIX_W]
    row = lambda w: pl.BlockSpec((tm, w), lambda i: (i, 0))
    return pl.pallas_call(
        body, name="mixer_fwd", grid=(t // tm,),
        in_specs=[row(D_MODEL), _full(g), _resident(w_in)] + [_full(w) for w in ws],
        out_specs=[row(D_MODEL), row(D_IN), row(D_MODEL), row(D_RNN), row(D_CONV), row(D_RNN), row(4 * D_RNN)],
        out_shape=[jax.ShapeDtypeStruct((t, D_MODEL), BF16), jax.ShapeDtypeStruct((t, D_IN), F32),
                   jax.ShapeDtypeStruct((t, D_MODEL), BF16), jax.ShapeDtypeStruct((t, D_RNN), F32),
                   jax.ShapeDtypeStruct((t, D_CONV), F32), jax.ShapeDtypeStruct((t, D_RNN), F32),
                   jax.ShapeDtypeStruct((t, 4 * D_RNN), F32)],
        scratch_shapes=[pltpu.VMEM((HALO + tm, D_POOL), F32), pltpu.VMEM((HALO + tm, D_CONV), F32),
                        pltpu.VMEM((HALO + tm, D_RNN), F32), pltpu.VMEM((HALO + tm, D_POOL), F32),
                        pltpu.VMEM((HALO + tm, D_POOL), F32), pltpu.VMEM((tm + 8, D_CONV), F32),
                        pltpu.VMEM((tm, D_RNN), F32), pltpu.VMEM((tm, D_RNN), F32), pltpu.VMEM((8, D_RNN), F32)],
        compiler_params=_params("arbitrary"),
    )(h, g, w_in, *ws)


_MIX_G = (("wp", (D_POOL, D_POOL)), ("psc", (1, D_POOL)), ("dww", (32, 8, D_CONV)), ("dwb", (1, D_CONV)),
          ("lng", (1, D_CONV)), ("lnb", (1, D_CONV)), ("wpw", (D_CONV, D_CONV)), ("cw", (8, D_RNN)),
          ("cb", (1, D_RNN)), ("wa", (D_RNN, D_RNN)), ("ba", (1, D_RNN)), ("wx", (D_RNN, D_RNN)),
          ("bx", (1, D_RNN)), ("lam", (1, D_RNN)), ("g1", (1, D_MODEL)))


def _mixer_bwd(p, dh1, hs, conv, xc, gates, h0, g1, w_out, w_in, mw, deps=()):
    t = p.shape[0]
    tm = _row_tile(t, TM_MIX)
    nt = t // tm
    hb = tm // HALO

    def body(p_ref, ph_ref, dh1_ref, hs_ref, hsh_ref, conv_ref, xc_ref, gates_ref, h0_ref, g1_ref, wout_ref, win_ref,
             wp, psc, dww, dwb, lng, lnb, wpw, cw, cb, wa, ba, wx, bx, lam,
             dp_ref, dh0_ref, g_wp, g_psc, g_dww, g_dwb, g_lng, g_lnb, g_wpw, g_cw, g_cb, g_wa, g_ba, g_wx, g_bx, g_lam, g_g1,
             ext_q, ext_u, ext_x, ext_h, ee, dc_s, dx_s, tmp_a, tmp_b, zbuf, d_pad, a_s, b_s, g_s, gcar, dy_ref, dp_s):
        step = pl.program_id(0)
        i = nt - 1 - step
        grads = (g_wp, g_psc, g_dww, g_dwb, g_lng, g_lnb, g_wpw, g_cw, g_cb, g_wa, g_ba, g_wx, g_bx, g_lam, g_g1)
        dy_ref[...] = dh1_ref[...].astype(BF16)
        dy_cols = lambda lo, hi: _dot_nt(dy_ref[...], wout_ref[lo:hi, :])

        @pl.when(step == 0)
        def _():
            for gr in grads:
                gr[...] = jnp.zeros(gr.shape, F32)
            ee[pl.ds(tm, HALO), :] = jnp.zeros((HALO, D_POOL), F32)
            dc_s[pl.ds(tm, HALO), :] = jnp.zeros((HALO, D_CONV), F32)
            dx_s[pl.ds(tm, HALO), :] = jnp.zeros((HALO, D_RNN), F32)
            d_pad[0:8, :] = jnp.zeros((8, D_CONV), F32)
            d_pad[pl.ds(tm + 8, 8), :] = jnp.zeros((8, D_CONV), F32)
            gcar[...] = jnp.zeros((8, D_RNN), F32)

        hm = jnp.where(i == 0, 0.0, 1.0)

        ext_q[0:HALO, :] = ph_ref[:, 0:256] * hm
        ext_q[pl.ds(HALO, tm), :] = p_ref[:, 0:256]
        pooled, lane, cnt = _pool_fwd(ext_q, tmp_a, tmp_b, tm, i * tm)
        pooled_b = pooled.astype(BF16)
        dya = dy_cols(0, 256)
        g_psc[...] += _colsum(dya * _dot(pooled_b, wp[...]))
        dmixed_b = (dya * psc[...]).astype(BF16)
        dpooled = _dot_nt(dmixed_b, wp[...])
        g_wp[...] += _dot_tn(pooled_b, dmixed_b)
        ee[0:tm, :] = dpooled / cnt
        dp_s[:, 0:256] = _lane_sel(lane, *_window_sums_ahead(ee, tmp_a, tmp_b, tm)) - dpooled
        ee[pl.ds(tm, HALO), :] = ee[0:HALO, :]

        v = p_ref[:, 256:512]
        s = _sig(p_ref[:, 512:768])
        ext_u[0:HALO, :] = ph_ref[:, 256:512] * _sig(ph_ref[:, 512:768]) * hm
        ext_u[pl.ds(HALO, tm), :] = v * s
        z, rstd, l, sl, act = _ln_silu(conv_ref[...], lng[...], lnb[...])
        dyb_b = dy_cols(256, 512).astype(BF16)
        dact = _dot_nt(dyb_b, wpw[...])
        g_wpw[...] += _dot_tn(act.astype(BF16), dyb_b)
        dl = dact * (sl * (1.0 + l * (1.0 - sl)))
        g_lng[...] += _colsum(dl * z)
        g_lnb[...] += _colsum(dl)
        dz = dl * lng[...]
        dc = rstd * (dz - jnp.mean(dz, axis=-1, keepdims=True) - z * jnp.mean(dz * z, axis=-1, keepdims=True))
        g_dwb[...] += _colsum(dc)
        dc_s[0:tm, :] = dc
        d_pad[pl.ds(8, tm), :] = dc
        _tap_grads(d_pad, ext_u, _CONV_OFFS, tm, g_dww, zbuf)
        du0 = _taps(dc_s, lambda j: dww[CONV_K - 1 - j:CONV_K - j, :], list(range(CONV_K)), tm, zbuf)
        dp_s[:, 256:512] = du0 * s
        dp_s[:, 512:768] = du0 * v * (s * (1.0 - s))
        dc_s[pl.ds(tm, HALO), :] = dc_s[0:HALO, :]

        ext_x[0:HALO, :] = ph_ref[:, 1280:1792] * hm
        ext_x[pl.ds(HALO, tm), :] = p_ref[:, 1280:1792]
        xc = xc_ref[...]
        xcb = xc.astype(BF16)
        r, ig, a, m = (gates_ref[:, j * D_RNN:(j + 1) * D_RNN] for j in range(4))
        sp = _softplus_neg(lam[...])
        ext_h[0:HALO, :] = hsh_ref[...] * hm
        ext_h[pl.ds(HALO, tm), :] = hs_ref[...]
        dyc = dy_cols(512, 1024)
        gl, dgl = _gelu_and_grad(p_ref[:, 768:1280])
        dp_s[:, 768:1280] = dyc * hs_ref[...] * dgl
        a_s[...] = a
        b_s[...] = dyc * gl
        _scan_rows(a_s, b_s, g_s, gcar, tm, reverse=True)
        g = g_s[...]
        da = g * ext_h[pl.ds(HALO - 1, tm), :]
        dm = g * (ig * xc)
        dig = g * (m * xc)
        dlog_a = da * a - dm * (a * a) / m
        g_lam[...] += _colsum(dlog_a * (-RG_C * r)) * (-_sig(-lam[...]))
        dra = (dlog_a * (-RG_C * sp)) * (r * (1.0 - r))
        dia = dig * (ig * (1.0 - ig))
        g_ba[...] += _colsum(dra)
        g_bx[...] += _colsum(dia)
        dra_b = dra.astype(BF16)
        dia_b = dia.astype(BF16)
        dxc = g * (m * ig) + _dot_nt(dra_b, wa[...]) + _dot_nt(dia_b, wx[...])
        g_wa[...] += _dot_tn(xcb, dra_b)
        g_wx[...] += _dot_tn(xcb, dia_b)
        g_cb[...] += _colsum(dxc)
        dx_s[0:tm, :] = dxc
        for k in range(RG_CONV_K):
            g_cw[k:k + 1, :] += _colsum(dxc * ext_x[pl.ds(HALO - (RG_CONV_K - 1) + k, tm), :])
        dxin = cw[RG_CONV_K - 1:RG_CONV_K, :] * dxc
        for j in range(1, RG_CONV_K):
            dxin = dxin + cw[RG_CONV_K - 1 - j:RG_CONV_K - j, :] * dx_s[pl.ds(j, tm), :]
        dp_s[:, 1280:1792] = dxin
        dx_s[pl.ds(tm, HALO), :] = dx_s[0:HALO, :]

        dpb = dp_s[...].astype(BF16)
        dp_ref[...] = dpb
        du = _dot(dpb, win_ref[...])
        r, n = _rms(h0_ref[...])
        g_g1[...] += _colsum(du * n)
        dh0_ref[...] = dh1_ref[...] + _rms_bwd(du, n, r, g1_ref[...])

    ws = [mw[k] for k in _MIX_W]
    tile = lambda w: pl.BlockSpec((tm, w), lambda s: (nt - 1 - s, 0))
    halo = lambda w: pl.BlockSpec((HALO, w), lambda s: (jnp.maximum((nt - 1 - s) * hb - 1, 0), 0))
    outs = pl.pallas_call(
        _after(body, 12 + len(ws), deps), name="mixer_bwd", grid=(nt,),
        in_specs=[tile(D_IN), halo(D_IN), tile(D_MODEL), tile(D_RNN), halo(D_RNN), tile(D_CONV), tile(D_RNN), tile(4 * D_RNN),
                  tile(D_MODEL), _full(g1),
                  _resident(w_out), _resident(w_in)] + [_full(w) for w in ws] + [ANY] * len(deps),
        out_specs=[tile(D_IN), tile(D_MODEL)] + [pl.BlockSpec(shp, lambda s, nd=len(shp): (0,) * nd) for _, shp in _MIX_G],
        out_shape=[jax.ShapeDtypeStruct((t, D_IN), BF16), jax.ShapeDtypeStruct((t, D_MODEL), F32)]
        + [jax.ShapeDtypeStruct(shp, F32) for _, shp in _MIX_G],
        scratch_shapes=[pltpu.VMEM((HALO + tm, D_POOL), F32), pltpu.VMEM((HALO + tm, D_CONV), F32),
                        pltpu.VMEM((HALO + tm, D_RNN), F32), pltpu.VMEM((HALO + tm, D_RNN), F32),
                        pltpu.VMEM((tm + HALO, D_POOL), F32), pltpu.VMEM((tm + HALO, D_CONV), F32),
                        pltpu.VMEM((tm + HALO, D_RNN), F32), pltpu.VMEM((HALO + tm, D_POOL), F32),
                        pltpu.VMEM((HALO + tm, D_POOL), F32), pltpu.VMEM((tm + 8, D_CONV), F32),
                        pltpu.VMEM((tm + 16, D_CONV), F32), pltpu.VMEM((tm, D_RNN), F32),
                        pltpu.VMEM((tm, D_RNN), F32), pltpu.VMEM((tm, D_RNN), F32), pltpu.VMEM((8, D_RNN), F32),
                        pltpu.VMEM((tm, D_MODEL), BF16), pltpu.VMEM((tm, D_IN), F32)],
        compiler_params=_params("arbitrary"),
    )(p, p, dh1, hs, hs, conv, xc, gates, h0, g1, w_out, w_in, *ws, *deps)
    return outs[0], outs[1], {k: o for (k, _), o in zip(_MIX_G, outs[2:])}


def _mid_fwd(y, h0, w_out, g, w_up):
    t = h0.shape[0]
    tm = _row_tile(t, TM_MAT)

    def body(y_ref, h0_ref, wo_ref, g_ref, wu_ref, h1_ref, u2_ref, f_ref):
        h1 = h0_ref[...] + _dot(y_ref[...], wo_ref[...])
        h1_ref[...] = h1
        u2 = (_rms(h1)[1] * g_ref[...]).astype(BF16)
        u2_ref[...] = u2
        for c in range(D_FF // FF_CHUNK):
            f_ref[:, c * FF_CHUNK:(c + 1) * FF_CHUNK] = _dot(u2, wu_ref[c]).astype(BF16)

    row = lambda w: pl.BlockSpec((tm, w), lambda i: (i, 0))
    return pl.pallas_call(
        body, name="mid_fwd", grid=(t // tm,),
        in_specs=[row(D_MODEL), row(D_MODEL), _resident(w_out), _full(g), _resident(w_up)],
        out_specs=[row(D_MODEL), row(D_MODEL), row(D_FF)],
        out_shape=[jax.ShapeDtypeStruct((t, D_MODEL), F32), jax.ShapeDtypeStruct((t, D_MODEL), BF16),
                   jax.ShapeDtypeStruct((t, D_FF), BF16)],
        compiler_params=_params("parallel"),
    )(y, h0, w_out, g, w_up)


def _down_proj(f_ref, h1_ref, wd_ref):
    acc = h1_ref[...]
    for c in range(D_FF // FF_CHUNK):
        cols = slice(c * FF_CHUNK, (c + 1) * FF_CHUNK)
        a = jnp.square(jnp.maximum(f_ref[:, cols].astype(F32), 0.0)).astype(BF16)
        acc = acc + _dot(a, wd_ref[cols, :])
    return acc


def _down_fwd(f, h1, w_down):
    t = h1.shape[0]
    tm = _row_tile(t, TM_MAT)

    def body(f_ref, h1_ref, wd_ref, h2_ref):
        h2_ref[...] = _down_proj(f_ref, h1_ref, wd_ref)

    row = lambda w: pl.BlockSpec((tm, w), lambda i: (i, 0))
    return pl.pallas_call(
        body, name="down_fwd", grid=(t // tm,),
        in_specs=[row(D_FF), row(D_MODEL), _resident(w_down)], out_specs=row(D_MODEL),
        out_shape=jax.ShapeDtypeStruct((t, D_MODEL), F32),
        compiler_params=_params("parallel"),
    )(f, h1, w_down)


def _down_fwd_loss(f, h1, w_down, g, tgt, t_real):
    t = h1.shape[0]
    tm = _row_tile(t, TM_MAT)

    def body(f_ref, h1_ref, wd_ref, g_ref, tgt_ref, loss_ref, dh_ref, dg_ref):
        i = pl.program_id(0)

        @pl.when(i == 0)
        def _():
            loss_ref[...] = jnp.zeros(loss_ref.shape, F32)
            dg_ref[...] = jnp.zeros(dg_ref.shape, F32)

        r, n = _rms(_down_proj(f_ref, h1_ref, wd_ref))
        row = lax.broadcasted_iota(jnp.int32, (tm, 1), 0) + i * tm
        valid = jnp.logical_and(row >= N_META, row < t_real)
        diff = jnp.where(valid, n * g_ref[...] - tgt_ref[...], 0.0)
        loss_ref[...] += 0.5 * jnp.sum(jnp.mean(diff * diff, axis=-1, keepdims=True))
        dy = diff * (1.0 / D_MODEL)
        dg_ref[...] += _colsum(dy * n)
        dh_ref[...] = _rms_bwd(dy, n, r, g_ref[...])

    row = lambda w: pl.BlockSpec((tm, w), lambda i: (i, 0))
    return pl.pallas_call(
        body, name="down_fwd_loss", grid=(t // tm,),
        in_specs=[row(D_FF), row(D_MODEL), _resident(w_down), _full(g), row(D_MODEL)],
        out_specs=[pl.BlockSpec((8, 128), lambda i: (0, 0)), row(D_MODEL), pl.BlockSpec((1, D_MODEL), lambda i: (0, 0))],
        out_shape=[jax.ShapeDtypeStruct((8, 128), F32), jax.ShapeDtypeStruct((t, D_MODEL), F32),
                   jax.ShapeDtypeStruct((1, D_MODEL), F32)],
        compiler_params=_params("arbitrary"),
    )(f, h1, w_down, g, tgt)


def _mlp_bwd(dh2, f, h1, g, w_up, w_down, deps=()):
    t = dh2.shape[0]
    tm = _row_tile(t, TM_MLP_BWD)

    def body(dh2_ref, f_ref, wd_ref, wu_ref, h1_ref, g_ref, df_ref, dh1_ref, dg_ref):
        @pl.when(pl.program_id(0) == 0)
        def _():
            dg_ref[...] = jnp.zeros(dg_ref.shape, F32)

        dh2 = dh2_ref[...]
        dhb = dh2.astype(BF16)
        du2 = None
        for c in range(D_FF // FF_CHUNK):
            cols = slice(c * FF_CHUNK, (c + 1) * FF_CHUNK)
            dact = _dot_nt(dhb, wd_ref[c])
            df = (dact * (2.0 * jnp.maximum(f_ref[:, cols].astype(F32), 0.0))).astype(BF16)
            df_ref[:, cols] = df
            part = _dot_nt(df, wu_ref[c])
            du2 = part if du2 is None else du2 + part
        r, n = _rms(h1_ref[...])
        dg_ref[...] += _colsum(du2 * n)
        dh1_ref[...] = dh2 + _rms_bwd(du2, n, r, g_ref[...])

    row = lambda w: pl.BlockSpec((tm, w), lambda i: (i, 0))
    return pl.pallas_call(
        _after(body, 6, deps), name="mlp_bwd", grid=(t // tm,),
        in_specs=[row(D_MODEL), row(D_FF), _resident(w_down), _resident(w_up), row(D_MODEL), _full(g)] + [ANY] * len(deps),
        out_specs=[row(D_FF), row(D_MODEL), pl.BlockSpec((1, D_MODEL), lambda i: (0, 0))],
        out_shape=[jax.ShapeDtypeStruct((t, D_FF), BF16), jax.ShapeDtypeStruct((t, D_MODEL), F32),
                   jax.ShapeDtypeStruct((1, D_MODEL), F32)],
        compiler_params=_params("arbitrary"),
    )(dh2, f, w_down, w_up, h1, g, *deps)


def _tn_matmul(a, b, kc, nc, relu2, name, deps=()):
    t, k = a.shape
    n = b.shape[1]
    tt = _row_tile(t, TM_MAT)
    gk, gn = k // kc, n // nc

    def body(a_ref, b_ref, o_ref):
        @pl.when(pl.program_id(2) == 0)
        def _():
            o_ref[...] = jnp.zeros(o_ref.shape, F32)

        av = a_ref[...]
        if relu2:
            av = jnp.square(jnp.maximum(av.astype(F32), 0.0))
        o_ref[...] += _dot_tn(av.astype(BF16), b_ref[...].astype(BF16))

    return pl.pallas_call(
        _after(body, 2, deps), name=name, grid=(gk, gn, t // tt),
        in_specs=[pl.BlockSpec((tt, kc), lambda ik, jn, it: (it, ik)), pl.BlockSpec((tt, nc), lambda ik, jn, it: (it, jn))]
        + [ANY] * len(deps),
        out_specs=pl.BlockSpec((None, kc, nc), lambda ik, jn, it: (ik * gn + jn, 0, 0)),
        out_shape=jax.ShapeDtypeStruct((gk * gn, kc, nc), F32),
        compiler_params=_params("parallel", "parallel", "arbitrary"),
    )(a, b, *deps)


def _block_diag(blocks):
    nb, hd, _ = blocks.shape
    eye = jnp.eye(nb, dtype=blocks.dtype)
    return (blocks[:, :, None, :] * eye[:, None, :, None]).reshape(nb * hd, nb * hd)


def _diag_blocks(m, nb):
    hd = m.shape[0] // nb
    eye = jnp.eye(nb, dtype=m.dtype)
    return jnp.sum(m.reshape(nb, hd, nb, hd) * eye[:, None, :, None], axis=2)


def _mixer_weights(w, l):
    row = lambda a: a.reshape(1, -1)
    return dict(
        wp=_block_diag(w["pool_w"][l]).astype(BF16), psc=row(w["pool_scale"][l]),
        dww=jnp.pad(w["convb_dw_w"][l], ((0, 32 - CONV_K), (0, 0))), dwb=row(w["convb_dw_b"][l]),
        lng=row(w["convb_ln_g"][l]), lnb=row(w["convb_ln_b"][l]), wpw=w["convb_pw_w"][l].astype(BF16),
        cw=jnp.pad(w["rg_conv_w"][l], ((0, 8 - RG_CONV_K), (0, 0))), cb=row(w["rg_conv_b"][l]),
        wa=_block_diag(w["rg_w_a"][l]).astype(BF16), ba=row(w["rg_b_a"][l]),
        wx=_block_diag(w["rg_w_x"][l]).astype(BF16), bx=row(w["rg_b_x"][l]), lam=row(w["rg_lambda"][l]))


def _local_step(h, tgt, t_real, w, fetch, hooks):
    depth = 2
    saved = []
    big = []
    for l in range(depth):
        mw = _mixer_weights(w, l)
        g1 = w["mix_norm_g"][l].reshape(1, -1)
        g2 = w["mlp_norm_g"][l].reshape(1, -1)
        wl = dict(w_in=fetch(l, "w_in", h))
        y, p, u, hs, conv, xc, gates = _mixer_fwd(h, g1, wl["w_in"], mw)
        wl["w_out"], wl["w_up"] = fetch(l, "w_out", y), fetch(l, "w_up", y)
        h1, u2, f = _mid_fwd(y, h, wl["w_out"], g2, wl["w_up"])
        wl["w_down"] = fetch(l, "w_down", f)
        if l == depth - 1:
            loss, dh, dgf = _down_fwd_loss(f, h1, wl["w_down"].reshape(D_FF, D_MODEL), w["final_norm_g"].reshape(1, -1), tgt,
                                           t_real)
            h2 = None
        else:
            h2 = _down_fwd(f, h1, wl["w_down"].reshape(D_FF, D_MODEL))
        saved.append(dict(mw=mw, g1=g1, g2=g2, h0=h, p=p, u=u, y=y, hs=hs, conv=conv, xc=xc, gates=gates, h1=h1, u2=u2, f=f))
        big.append(wl)
        h = h2

    gs = {k: [None] * depth for k in ("mix_norm_g", "mlp_norm_g", "pool_w", "pool_scale", "convb_dw_w", "convb_dw_b",
                                      "convb_ln_g", "convb_ln_b", "convb_pw_w", "rg_conv_w", "rg_conv_b", "rg_w_a",
                                      "rg_b_a", "rg_w_x", "rg_b_x", "rg_lambda")}
    deps = ()
    for l in reversed(range(depth)):
        s, wl = saved[l], big[l]
        df, dh1, dg2 = _mlp_bwd(dh, s["f"], s["h1"], s["g2"], wl["w_up"], wl["w_down"], deps)
        deps = hooks.point(l, "mlp_bwd", dh1)
        g_down = _tn_matmul(s["f"], dh, FF_CHUNK, D_MODEL, True, "dw_down", deps)
        hooks.grad(l, "w_down", g_down)
        deps = hooks.point(l, "dw_down", g_down)
        g_up = _tn_matmul(s["u2"], df, D_MODEL, FF_CHUNK, False, "dw_up", deps)
        hooks.grad(l, "w_up", g_up)
        deps = hooks.point(l, "dw_up", g_up)
        g_out = _tn_matmul(s["y"], dh1, D_MODEL, D_MODEL, False, "dw_out", deps)
        hooks.grad(l, "w_out", g_out.reshape(N_CHIPS, D_MODEL // N_CHIPS, D_MODEL))
        deps = hooks.point(l, "dw_out", g_out)
        dp, dh, mg = _mixer_bwd(s["p"], dh1, s["hs"], s["conv"], s["xc"], s["gates"], s["h0"], s["g1"], wl["w_out"],
                                wl["w_in"], s["mw"], deps)
        deps = hooks.point(l, "mixer_bwd", dh)
        gs["mix_norm_g"][l] = mg["g1"][0]
        gs["mlp_norm_g"][l] = dg2[0]
        gs["pool_w"][l] = _diag_blocks(mg["wp"], D_POOL // POOL_GW)
        gs["pool_scale"][l] = mg["psc"][0]
        gs["convb_dw_w"][l] = jnp.sum(mg["dww"][:CONV_K], axis=1)
        gs["convb_dw_b"][l] = mg["dwb"][0]
        gs["convb_ln_g"][l] = mg["lng"][0]
        gs["convb_ln_b"][l] = mg["lnb"][0]
        gs["convb_pw_w"][l] = mg["wpw"]
        gs["rg_conv_w"][l] = mg["cw"][:RG_CONV_K]
        gs["rg_conv_b"][l] = mg["cb"][0]
        gs["rg_w_a"][l] = _diag_blocks(mg["wa"], D_RNN // RG_HD)
        gs["rg_b_a"][l] = mg["ba"][0]
        gs["rg_w_x"][l] = _diag_blocks(mg["wx"], D_RNN // RG_HD)
        gs["rg_b_x"][l] = mg["bx"][0]
        gs["rg_lambda"][l] = mg["lam"][0]
        if l == 0:
            gsmall = {k: jnp.stack(v) for k, v in gs.items()}
            gsmall["final_norm_g"] = dgf[0]
            gsmall["meta_tokens"] = dh[:N_META]
            deps = deps + hooks.small(gsmall)
        g_in = _tn_matmul(dp, s["u"], D_IN, D_MODEL, False, "dw_in", deps).reshape(N_CHIPS, D_IN // N_CHIPS, D_MODEL)
        hooks.grad(l, "w_in", g_in)
        deps = hooks.point(l, "dw_in", g_in)
    return loss[0, 0], dh


def _place():
    return lax.axis_index("x"), lax.axis_index("y"), lax.axis_index("c")


def _other_chips(x, y):
    return [(1 - x, y), (x, 1 - y), (1 - x, 1 - y)]


HBM_SPEC = pl.BlockSpec(memory_space=pltpu.HBM)
SEM_SPEC = pl.BlockSpec(memory_space=pltpu.SEMAPHORE)
DATAFLOW = pltpu.SideEffectType.DATAFLOW_SIDE_EFFECTING


def _gather_copies(src_refs, land_refs, send_sem, recv_sem, first):
    x, y, c = _place()
    me = 2 * x + y
    out = []
    for n in range(len(src_refs)):
        for j, (px, py) in enumerate(_other_chips(x, y)):
            out.append(pltpu.make_async_remote_copy(src_refs[n], land_refs[n].at[me], send_sem.at[first + 3 * n + j],
                                                    recv_sem.at[first + 3 * n + j], device_id=(px, py, c), device_id_type=MESH))
    return out


def _gather_start(groups, me):
    srcs = [pltpu.with_memory_space_constraint(s, pltpu.HBM) for g in groups for s in g]
    lands = [pltpu.with_memory_space_constraint(
        lax.dynamic_update_slice(jnp.zeros((N_CHIPS,) + s.shape, s.dtype), s[None], (me,) + (0,) * s.ndim), pltpu.HBM)
        for g in groups for s in g]
    n, ng = len(srcs), len(groups)
    first = [sum(len(g) for g in groups[:i]) for i in range(ng)]

    def body(*refs):
        src_refs, land_refs = refs[:n], refs[n:2 * n]
        sems = refs[2 * n:2 * n + 2 * ng]
        token = refs[-1]
        for gi, g in enumerate(groups):
            lo, hi = first[gi], first[gi] + len(g)
            for cp in _gather_copies(src_refs[lo:hi], land_refs[lo:hi], sems[2 * gi], sems[2 * gi + 1], 0):
                cp.start()
        token[...] = jnp.zeros(token.shape, token.dtype)

    sem_shapes = [pltpu.SemaphoreType.DMA((3 * len(g),)) for g in groups for _ in range(2)]
    outs = pl.pallas_call(
        body, name="gather_start",
        out_shape=sem_shapes + [pltpu.HBM(a.shape, a.dtype) for a in srcs + lands] + [jax.ShapeDtypeStruct((8, 128), F32)],
        in_specs=[HBM_SPEC] * (2 * n),
        out_specs=[SEM_SPEC] * (2 * ng) + [HBM_SPEC] * (2 * n) + [pl.BlockSpec(memory_space=pltpu.VMEM)],
        input_output_aliases={i: 2 * ng + i for i in range(2 * n)},
        compiler_params=pltpu.CompilerParams(has_side_effects=DATAFLOW),
    )(*srcs, *lands)
    sems, thru, token = outs[:2 * ng], outs[2 * ng:2 * ng + 2 * n], outs[-1]
    state = []
    for gi, g in enumerate(groups):
        lo, hi = first[gi], first[gi] + len(g)
        state.append((sems[2 * gi], sems[2 * gi + 1], thru[lo:hi], thru[n + lo:n + hi]))
    return state, token


def _gather_wait(state, after, name):
    send_sem, recv_sem, srcs, lands = state
    n = len(srcs)

    def body(*refs):
        src_refs, land_refs = refs[:n], refs[n:2 * n]
        send, recv = refs[2 * n], refs[2 * n + 1]
        for cp in _gather_copies(src_refs, land_refs, send, recv, 0):
            cp.wait_send()
            cp.wait_recv()

    outs = pl.pallas_call(
        body, name=name,
        out_shape=[pltpu.HBM(a.shape, a.dtype) for a in list(srcs) + list(lands)],
        in_specs=[HBM_SPEC] * (2 * n) + [SEM_SPEC, SEM_SPEC, ANY],
        out_specs=[HBM_SPEC] * (2 * n),
        input_output_aliases={i: i for i in range(2 * n)},
        compiler_params=pltpu.CompilerParams(has_side_effects=DATAFLOW),
    )(*srcs, *lands, send_sem, recv_sem, after)
    return outs[n:]


def _add_halves(g, recv, c1):
    nk, r, cd = g.shape
    r2 = r // 2

    def body(c_ref, g_ref, r_ref, pab_ref):
        pab_ref[...] = (g_ref[...] + r_ref[...]).astype(BF16)

    blk = pl.BlockSpec((None, r2, cd), lambda k, c_ref: (k, 0, 0))
    return pl.pallas_call(
        body, name="rs_add_halves",
        grid_spec=pltpu.PrefetchScalarGridSpec(
            num_scalar_prefetch=1, grid=(nk,),
            in_specs=[pl.BlockSpec((None, r2, cd), lambda k, c_ref: (k, c_ref[0], 0)), blk], out_specs=blk),
        out_shape=jax.ShapeDtypeStruct((nk, r2, cd), BF16),
        compiler_params=_params("parallel"),
    )(c1, g, recv)


def _sum_partials(g, recv_sibling, recv_chips, c_me):
    nk, r, cd = g.shape
    r2 = r // 2

    def body(cm_ref, g_ref, a_ref, r_ref, s_ref):
        own = g_ref[...] + a_ref[...]
        s_ref[...] = ((own + r_ref[0].astype(F32)) + r_ref[1].astype(F32)) + r_ref[2].astype(F32)

    return pl.pallas_call(
        body, name="rs_sum_partials",
        grid_spec=pltpu.PrefetchScalarGridSpec(
            num_scalar_prefetch=1, grid=(1,),
            in_specs=[pl.BlockSpec((None, r2, cd), lambda i, cm: (cm[1], cm[0], 0)),
                      pl.BlockSpec((None, r2, cd), lambda i, cm: (cm[1], 0, 0)),
                      pl.BlockSpec((3, r2, cd), lambda i, cm: (0, 0, 0))],
            out_specs=pl.BlockSpec((r2, cd), lambda i, cm: (0, 0))),
        out_shape=jax.ShapeDtypeStruct((r2, cd), F32),
        compiler_params=_params("arbitrary"),
    )(c_me, g, recv_sibling, recv_chips)


def _split_start(name, srcs, lands, ncopies, make_copies):
    srcs = [pltpu.with_memory_space_constraint(s, pltpu.HBM) for s in srcs]
    lands = [pltpu.with_memory_space_constraint(a, pltpu.HBM) for a in lands]
    n, m = len(srcs), len(lands)

    def body(*refs):
        src_refs, land_refs = refs[:n], refs[n:n + m]
        send, recv, token = refs[n + m], refs[n + m + 1], refs[-1]
        for cp in make_copies(src_refs, land_refs, send, recv):
            cp.start()
        token[...] = jnp.zeros(token.shape, token.dtype)

    outs = pl.pallas_call(
        body, name=name,
        out_shape=[pltpu.SemaphoreType.DMA((ncopies,)), pltpu.SemaphoreType.DMA((ncopies,))]
        + [pltpu.HBM(a.shape, a.dtype) for a in srcs + lands] + [jax.ShapeDtypeStruct((8, 128), F32)],
        in_specs=[HBM_SPEC] * (n + m),
        out_specs=[SEM_SPEC, SEM_SPEC] + [HBM_SPEC] * (n + m) + [pl.BlockSpec(memory_space=pltpu.VMEM)],
        input_output_aliases={i: 2 + i for i in range(n + m)},
        compiler_params=pltpu.CompilerParams(has_side_effects=DATAFLOW),
    )(*srcs, *lands)
    return (outs[0], outs[1], outs[2:2 + n], outs[2 + n:2 + n + m], make_copies), outs[-1]


def _split_wait(name, state, after):
    send_sem, recv_sem, srcs, lands, make_copies = state
    n, m = len(srcs), len(lands)

    def body(*refs):
        src_refs, land_refs = refs[:n], refs[n:n + m]
        for cp in make_copies(src_refs, land_refs, refs[n + m], refs[n + m + 1]):
            cp.wait_send()
            cp.wait_recv()

    outs = pl.pallas_call(
        body, name=name,
        out_shape=[pltpu.HBM(a.shape, a.dtype) for a in list(srcs) + list(lands)],
        in_specs=[HBM_SPEC] * (n + m) + [SEM_SPEC, SEM_SPEC, ANY],
        out_specs=[HBM_SPEC] * (n + m),
        input_output_aliases={i: i for i in range(n + m)},
        compiler_params=pltpu.CompilerParams(has_side_effects=DATAFLOW),
    )(*srcs, *lands, send_sem, recv_sem, after)
    return outs[:n], outs[n:]


def _copies_to_sibling(src_of):
    def make(src_refs, land_refs, send, recv):
        x, y, c = _place()
        return [pltpu.make_async_remote_copy(src_of(src_refs[i], c), land_refs[i], send.at[i], recv.at[i],
                                             device_id=(x, y, 1 - c), device_id_type=MESH) for i in range(len(src_refs))]
    return make


def _copies_to_chips(src_refs, land_refs, send, recv):
    x, y, c = _place()
    return [pltpu.make_async_remote_copy(src_refs[i].at[2 * px + py], land_refs[i].at[j], send.at[3 * i + j], recv.at[3 * i + j],
                                         device_id=(px, py, c), device_id_type=MESH)
            for i in range(len(src_refs)) for j, (px, py) in enumerate(_other_chips(x, y))]


def _other_half_rows(ref, c):
    r2 = ref.shape[1] // 2
    return ref.at[:, pl.ds(pl.multiple_of((1 - c) * r2, 8), r2)]


class _ReduceScatter:
    def __init__(self, tag, grads, c1, me1):
        self.tag, self.grads, self.c1, self.me1 = tag, grads, c1, me1

    def start(self):
        lands = [lax.empty((g.shape[0], g.shape[1] // 2, g.shape[2]), F32) for g in self.grads]
        self.state, token = _split_start("rs_%s_a_start" % self.tag, self.grads, lands, len(self.grads),
                                         _copies_to_sibling(_other_half_rows))
        return token

    def to_chips(self, after):
        self.halves = _split_wait("rs_%s_a_wait" % self.tag, self.state, after)
        pabs = [_add_halves(g, r, self.c1) for g, r in zip(*self.halves)]
        lands = [lax.empty((3,) + p.shape[1:], BF16) for p in pabs]
        self.state, token = _split_start("rs_%s_b_start" % self.tag, pabs, lands, 3 * len(pabs), _copies_to_chips)
        return token

    def to_sibling(self, after):
        _, recv = _split_wait("rs_%s_b_wait" % self.tag, self.state, after)
        c_me = jnp.concatenate([self.c1, self.me1])
        sums = [_sum_partials(g, ra, rb, c_me) for g, ra, rb in zip(*self.halves, recv)]
        lands = [lax.empty(s.shape, F32) for s in sums]
        self.state, token = _split_start("rs_%s_c_start" % self.tag, sums, lands, len(sums),
                                         _copies_to_sibling(lambda ref, c: ref))
        return token

    def finish(self, after):
        return list(zip(*_split_wait("rs_%s_c_wait" % self.tag, self.state, after)))


def _add_lists(a_list, b_list):
    n = len(a_list)

    def body(*refs):
        for i in range(n):
            refs[2 * n + i][...] = refs[i][...] + refs[n + i][...]

    vm = pl.BlockSpec(memory_space=pltpu.VMEM)
    return pl.pallas_call(
        body, name="add_lists", in_specs=[vm] * (2 * n), out_specs=[vm] * n,
        out_shape=[jax.ShapeDtypeStruct(a.shape, a.dtype) for a in a_list],
        compiler_params=pltpu.CompilerParams(vmem_limit_bytes=VMEM_LIMIT),
    )(*a_list, *b_list)


def _copies_to_peer(stage):
    def make(src_refs, land_refs, send, recv):
        x, y, c = _place()
        peer = [(x, y, 1 - c), (1 - x, y, c), (x, 1 - y, c)][stage]
        return [pltpu.make_async_remote_copy(src_refs[i], land_refs[i], send.at[i], recv.at[i], device_id=peer, device_id_type=MESH)
                for i in range(len(src_refs))]
    return make


class _AllReduceSmall:
    def __init__(self, vs):
        self.vs, self.stage = list(vs), 0

    def _start(self):
        lands = [lax.empty(v.shape, v.dtype) for v in self.vs]
        self.state, token = _split_start("ar_small_start_%d" % self.stage, self.vs, lands, len(self.vs), _copies_to_peer(self.stage))
        return token

    def start(self):
        return self._start()

    def step(self, after):
        mine, theirs = _split_wait("ar_small_wait_%d" % self.stage, self.state, after)
        self.vs = _add_lists(mine, theirs)
        self.stage += 1
        return self._start() if self.stage < 3 else self.vs[0]


def _adamw_math(w, g, m, v):
    m = ADAM_B1 * m + (1.0 - ADAM_B1) * g
    v = ADAM_B2 * v + (1.0 - ADAM_B2) * jnp.square(g)
    m_hat = m / (1.0 - ADAM_B1 ** ADAM_STEP)
    v_hat = v / (1.0 - ADAM_B2 ** ADAM_STEP)
    return -ADAM_LR * (m_hat / (jnp.sqrt(v_hat) + ADAM_EPS) + ADAM_WD * w), m, v


def _adamw_big_layer(layer, w, m, v, own, sib, c1, prev):
    _, r, cd = w.shape
    r2 = r // 2

    def body(c_ref, w_ref, m_ref, v_ref, own_ref, sib_ref, *rest):
        g_ref, d_ref, mo_ref, vo_ref, token = rest[-5:]
        g = jnp.where(pl.program_id(0) == c_ref[0], own_ref[...], sib_ref[...])
        g_ref[...] = g
        d_ref[...], mo_ref[...], vo_ref[...] = _adamw_math(w_ref[...], g, m_ref[...], v_ref[...])
        token[...] = jnp.zeros(token.shape, F32)

    blk = pl.BlockSpec((None, r2, cd), lambda hh, c_ref: (layer, hh, 0))
    half = pl.BlockSpec((r2, cd), lambda hh, c_ref: (0, 0))
    prev = () if prev is None else tuple(prev)
    outs = pl.pallas_call(
        body, name="adamw_big",
        grid_spec=pltpu.PrefetchScalarGridSpec(
            num_scalar_prefetch=1, grid=(2,), in_specs=[blk, blk, blk, half, half] + [ANY] * len(prev),
            out_specs=[blk] * 4 + [pl.BlockSpec((8, 128), lambda hh, c_ref: (0, 0))]),
        out_shape=[jax.ShapeDtypeStruct(w.shape, F32)] * 4 + [jax.ShapeDtypeStruct((8, 128), F32)],
        input_output_aliases={6 + i: i for i in range(len(prev))},
        compiler_params=_params("arbitrary"),
    )(c1, w, m, v, own, sib, *prev)
    return outs[:4], outs[4]


def _adamw_small(ws, gs, ms, vs):
    n = len(ws)

    def body(*refs):
        w_refs, g_refs, m_refs, v_refs = refs[:n], refs[n:2 * n], refs[2 * n:3 * n], refs[3 * n:4 * n]
        outs = refs[4 * n:]
        for i in range(n):
            outs[3 * i][...], outs[3 * i + 1][...], outs[3 * i + 2][...] = _adamw_math(
                w_refs[i][...], g_refs[i][...], m_refs[i][...], v_refs[i][...])

    vm = pl.BlockSpec(memory_space=pltpu.VMEM)
    outs = pl.pallas_call(
        body, name="adamw_small", in_specs=[vm] * (4 * n), out_specs=[vm] * (3 * n),
        out_shape=[jax.ShapeDtypeStruct(w.shape, F32) for w in ws for _ in range(3)],
        compiler_params=pltpu.CompilerParams(vmem_limit_bytes=VMEM_LIMIT),
    )(*ws, *gs, *ms, *vs)
    return [outs[3 * i:3 * i + 3] for i in range(n)]


LANES = 128
SUBLANES = 8
SHARDED_AXIS = {"meta_tokens": 1, "convb_dw_w": 2, "convb_pw_w": 1, "rg_conv_w": 2}


def _rows_of(size):
    return -(-size // (LANES * SUBLANES)) * SUBLANES


def _as_rows(a, rows=None):
    flat = a.reshape(-1)
    rows = _rows_of(flat.size) if rows is None else rows
    return jnp.pad(flat, (0, rows * LANES - flat.size)).reshape(rows, LANES)


class _GradientSchedule:
    GROUPS = {"l1": [(1, "w_down"), (1, "w_up"), (1, "w_out"), (1, "w_in")], "a0": [(0, "w_down"), (0, "w_up")],
              "b0": [(0, "w_out")], "c0": [(0, "w_in")]}
    PLAN = {
        (1, "dw_in"): [("l1", "start")],
        (0, "mlp_bwd"): [("l1", "to_chips")],
        (0, "dw_up"): [("l1", "to_sibling"), ("a0", "start")],
        (0, "dw_out"): [("l1", "finish"), ("a0", "to_chips"), ("b0", "start")],
        (0, "mixer_bwd"): [("a0", "to_sibling"), ("b0", "to_chips")],
        (0, "dw_in"): [("c0", "start"), ("small", "step"), ("a0", "finish"), ("b0", "to_sibling"), ("c0", "to_chips"),
                       ("small", "step")],
    }

    def __init__(self, w, mom, var, c1, me1):
        self.w, self.mom, self.var, self.c1, self.me1 = w, mom, var, c1, me1
        self.grads, self.chains, self.out = {}, {}, {}

    def grad(self, layer, name, g):
        self.grads[layer, name] = g

    def small(self, gsmall):
        self.small_sum = _AllReduceSmall([g.reshape(1, -1) if g.ndim == 1 else g for g in (gsmall[k] for k in SMALL)])
        return (self.small_sum.start(),)

    def point(self, layer, kernel_name, after):
        return self.run(self.PLAN.get((layer, kernel_name), ()), after) or (after,)

    def run(self, actions, after):
        deps = []
        for tag, stage in actions:
            if tag == "small":
                deps.append(self.small_sum.step(after))
            elif stage == "start":
                self.chains[tag] = _ReduceScatter(tag, [self.grads[lk] for lk in self.GROUPS[tag]], self.c1, self.me1)
                deps.append(self.chains[tag].start())
            elif stage == "finish":
                for (layer, k), (own, sib) in zip(self.GROUPS[tag], self.chains[tag].finish(after)):
                    self.out[k], token = _adamw_big_layer(layer, self.w[k], self.mom[k], self.var[k], own, sib, self.c1,
                                                          self.out.get(k))
                    deps.append(token)
            else:
                deps.append(getattr(self.chains[tag], stage)(after))
            after = deps[-1]
        self.last = after
        return tuple(deps)


def _from_shard_major(name, sm):
    if name == "meta_tokens":
        return sm.transpose(1, 0, 2).reshape(N_META, -1)
    if name == "convb_pw_w":
        return sm.transpose(1, 0, 2, 3).reshape(2, -1, D_CONV)
    return sm.transpose(1, 2, 0, 3).reshape(sm.shape[1], sm.shape[2], -1)


def kernel(x, meta_tokens, mix_norm_g, w_in, pool_w, pool_scale, convb_dw_w, convb_dw_b, convb_ln_g, convb_ln_b, convb_pw_w, rg_conv_w, rg_conv_b, rg_w_a, rg_b_a, rg_w_x, rg_b_x, rg_lambda, w_out, mlp_norm_g, w_up, w_down, final_norm_g, loss_target, m_meta_tokens, m_mix_norm_g, m_w_in, m_pool_w, m_pool_scale, m_convb_dw_w, m_convb_dw_b, m_convb_ln_g, m_convb_ln_b, m_convb_pw_w, m_rg_conv_w, m_rg_conv_b, m_rg_w_a, m_rg_b_a, m_rg_w_x, m_rg_b_x, m_rg_lambda, m_w_out, m_mlp_norm_g, m_w_up, m_w_down, m_final_norm_g, v_meta_tokens, v_mix_norm_g, v_w_in, v_pool_w, v_pool_scale, v_convb_dw_w, v_convb_dw_b, v_convb_ln_g, v_convb_ln_b, v_convb_pw_w, v_rg_conv_w, v_rg_conv_b, v_rg_w_a, v_rg_b_a, v_rg_w_x, v_rg_b_x, v_rg_lambda, v_w_out, v_mlp_norm_g, v_w_up, v_w_down, v_final_norm_g):
    given = dict(locals())
    w = {k: given[k] for k in WEIGHTS}
    mom = {k: given["m_" + k] for k in WEIGHTS}
    var = {k: given["v_" + k] for k in WEIGHTS}
    xi, yi, ci = _place()
    me1 = (2 * xi + yi).astype(jnp.int32).reshape(1)
    c1 = ci.astype(jnp.int32).reshape(1)

    small_rows = [_rows_of(w[k].size) for k in SMALL_SHARDED]
    small_pack = jnp.concatenate([_as_rows(w[k]) for k in SMALL_SHARDED])
    transposed = lambda d: {**d, "w_in": d["w_in"].transpose(0, 2, 1)}
    wt, momt, vart = transposed(w), transposed(mom), transposed(var)
    shard = lambda l, k: wt[k][l].astype(BF16)
    order = [[(0, "w_in"), "small"], [(0, "w_out"), (0, "w_up")], [(0, "w_down")], [(1, "w_in")], [(1, "w_out"), (1, "w_up")],
             [(1, "w_down")]]
    state, token = _gather_start([[small_pack if lk == "small" else shard(*lk) for lk in g] for g in order], me1[0])
    landed = {}

    def fetch(l, k, after):
        gi = [i for i, g in enumerate(order) if (l, k) in g][0]
        if gi not in landed:
            landed[gi] = _gather_wait(state[gi], after, "gather_wait_%d" % gi)
        raw = landed[gi][order[gi].index((l, k))]
        if k == "w_in":
            return raw.reshape(D_IN, D_MODEL)
        return raw.reshape(D_MODEL, D_MODEL) if k == "w_out" else raw

    seq = x.shape[1]
    t_real = N_META + seq
    t_pad = -(-t_real // ROW_ALIGN) * ROW_ALIGN
    tail = jnp.zeros((t_pad - t_real, D_MODEL), F32)
    front = jnp.zeros((N_META, D_MODEL), F32)
    h = jnp.concatenate([front + token[0, 0], x[0], tail])
    tgt = jnp.concatenate([front, loss_target[0], tail])
    landed[0] = _gather_wait(state[0], h, "gather_wait_0")
    wfull = {k: (w[k] + token[0, 0] if k in ("pool_w", "rg_w_a", "rg_w_x") else w[k]) for k in WEIGHTS}
    off = 0
    for k, rows in zip(SMALL_SHARDED, small_rows):
        sm = landed[0][1][:, off:off + rows].reshape(N_CHIPS, -1)[:, :w[k].size].reshape((N_CHIPS,) + w[k].shape)
        wfull[k] = _from_shard_major(k, sm)
        off += rows
    h = lax.dynamic_update_slice(h, wfull["meta_tokens"], (0, 0))
    sched = _GradientSchedule(wt, momt, vart, c1, me1)
    loss, dh = _local_step(h, tgt, t_real, wfull, fetch, sched)
    grad_x = dh[N_META:t_real][None]

    names = SMALL
    two_d = lambda a: a.reshape(1, -1) if a.ndim == 1 else a
    sched.run([("c0", "to_sibling"), ("small", "step")], sched.last)
    summed = dict(zip(names, sched.small_sum.vs))
    for k in SMALL_SHARDED:
        ax = SHARDED_AXIS[k]
        summed[k] = lax.dynamic_slice_in_dim(summed[k], me1[0] * w[k].shape[ax], w[k].shape[ax], axis=ax)

    out = {}
    res = _adamw_small([two_d(w[k]) for k in names], [summed[k] for k in names], [two_d(mom[k]) for k in names],
                       [two_d(var[k]) for k in names])
    for k, (d, m2, v2) in zip(names, res):
        out[k] = tuple(o.reshape(w[k].shape) for o in (summed[k], d, m2, v2))
    sched.run([("b0", "finish"), ("c0", "finish")], res[0][0])
    out.update(sched.out)
    out["w_in"] = tuple(o.transpose(0, 2, 1) for o in out["w_in"])

    loss = lax.psum(loss, ("x", "y", "c"))
    return (loss, grad_x, *[out[k][0] for k in WEIGHTS], *[out[k][1] for k in WEIGHTS],
            *[out[k][2] for k in WEIGHTS], *[out[k][3] for k in WEIGHTS])
```

```python
import functools

import jax
import jax.numpy as jnp
from jax import lax
from jax.experimental import pallas as pl
from jax.experimental.pallas import tpu as pltpu

F32, BF16 = jnp.float32, jnp.bfloat16
MESH = pl.DeviceIdType.MESH
ANY = pl.BlockSpec(memory_space=pl.ANY)

D_MODEL = 1024
N_META = 16
D_POOL = 256
D_CONV = 256
D_RNN = 512
D_IN = D_POOL + 2 * D_CONV + 2 * D_RNN
D_FF = 4096
FF_CHUNK = 1024
POOL_GW = 64
CONV_K = 31
RG_CONV_K = 4
RG_HD = 64
RG_C = 8.0
EPS = 1e-6
ADAM_LR, ADAM_B1, ADAM_B2, ADAM_EPS, ADAM_WD, ADAM_STEP = 0.001, 0.9, 0.999, 1e-08, 0.01, 10

HALO = 32
ROW_ALIGN = 256
TM_MIX = 384
TM_MAT = 768
TM_MLP_BWD = 384
N_CHIPS = 4
VMEM_LIMIT = 56 * 1024 * 1024

BIG = ("w_in", "w_out", "w_up", "w_down")
SMALL_SHARDED = ("meta_tokens", "convb_dw_w", "convb_pw_w", "rg_conv_w")
SMALL_REPL = ("mix_norm_g", "pool_w", "pool_scale", "convb_dw_b", "convb_ln_g", "convb_ln_b", "rg_conv_b",
              "rg_w_a", "rg_b_a", "rg_w_x", "rg_b_x", "rg_lambda", "mlp_norm_g", "final_norm_g")
SMALL = SMALL_REPL + SMALL_SHARDED
WEIGHTS = ("meta_tokens", "mix_norm_g", "w_in", "pool_w", "pool_scale", "convb_dw_w", "convb_dw_b", "convb_ln_g",
           "convb_ln_b", "convb_pw_w", "rg_conv_w", "rg_conv_b", "rg_w_a", "rg_b_a", "rg_w_x", "rg_b_x",
           "rg_lambda", "w_out", "mlp_norm_g", "w_up", "w_down", "final_norm_g")


def _params(*sem):
    return pltpu.CompilerParams(dimension_semantics=sem, vmem_limit_bytes=VMEM_LIMIT)


def _row_tile(t, cap):
    best = None
    for tm in range(128, cap + 1, 128):
        if t % tm == 0:
            best = tm
    assert best is not None, (t, cap)
    return best


def _dot(a, b):
    return jnp.dot(a, b, preferred_element_type=F32)


def _dot_nt(a, b):
    return lax.dot_general(a, b, (((1,), (1,)), ((), ())), preferred_element_type=F32)


def _dot_tn(a, b):
    return lax.dot_general(a, b, (((0,), (0,)), ((), ())), preferred_element_type=F32)


def _rms(x):
    r = lax.rsqrt(jnp.mean(x * x, axis=-1, keepdims=True) + EPS)
    return r, x * r


def _rms_bwd(du, n, r, g):
    dn = du * g
    return r * (dn - n * jnp.mean(dn * n, axis=-1, keepdims=True))


def _sig(x):
    return jax.nn.sigmoid(x)


def _colsum(x):
    return jnp.sum(x, axis=0, keepdims=True)


def _one_minus_sq(a, log_a):
    x = 2.0 * log_a
    series = -x * (1.0 + x * (0.5 + x * (1.0 / 6)))
    return jnp.where(x > -0.01, series, 1.0 - a * a)


_GELU_K0 = 0.7978845608028654
_GELU_K1 = 0.044715


def _gelu_and_grad(x):
    th = jnp.tanh(_GELU_K0 * (x + _GELU_K1 * x * x * x))
    val = 0.5 * x * (1.0 + th)
    grad = 0.5 * (1.0 + th) + 0.5 * x * (1.0 - th * th) * _GELU_K0 * (1.0 + 3.0 * _GELU_K1 * x * x)
    return val, grad


def _full(a):
    nd = a.ndim
    return pl.BlockSpec(a.shape, lambda *_: (0,) * nd)


def _resident(a):
    nd = a.ndim
    return pl.BlockSpec(a.shape, lambda *_: (0,) * nd, pipeline_mode=pl.Buffered(1))


def _after(body, n_in, deps):
    def wrapped(*refs):
        return body(*refs[:n_in], *refs[n_in + len(deps):])
    return wrapped


def _lane_sel(lane, a2, a4, a8, a16):
    return jnp.where(lane < POOL_GW, a2, jnp.where(lane < 2 * POOL_GW, a4, jnp.where(lane < 3 * POOL_GW, a8, a16)))


def _window_sums_back(src, tmp_a, tmp_b, tm):
    n = HALO + tm
    rows = lambda ref, lo, back: ref[pl.ds(lo - back, n - lo), :]
    tmp_a[pl.ds(8, n - 8), :] = rows(src, 8, 0) + rows(src, 8, 1)
    tmp_b[pl.ds(16, n - 16), :] = rows(tmp_a, 16, 0) + rows(tmp_a, 16, 2)
    s2 = rows(tmp_a, HALO, 0)
    tmp_a[pl.ds(24, n - 24), :] = rows(tmp_b, 24, 0) + rows(tmp_b, 24, 4)
    s8 = rows(tmp_a, HALO, 0)
    return s2, rows(tmp_b, HALO, 0), s8, s8 + rows(tmp_a, HALO, 8)


def _window_sums_ahead(src, tmp_a, tmp_b, tm):
    rows = lambda ref, n, ahead: ref[pl.ds(ahead, n), :]
    tmp_a[pl.ds(0, tm + 24), :] = rows(src, tm + 24, 0) + rows(src, tm + 24, 1)
    tmp_b[pl.ds(0, tm + 16), :] = rows(tmp_a, tm + 16, 0) + rows(tmp_a, tm + 16, 2)
    s2 = rows(tmp_a, tm, 0)
    tmp_a[pl.ds(0, tm + 8), :] = rows(tmp_b, tm + 8, 0) + rows(tmp_b, tm + 8, 4)
    s8 = rows(tmp_a, tm, 0)
    return s2, rows(tmp_b, tm, 0), s8, s8 + rows(tmp_a, tm, 8)


def _pool_counts(tm, t0):
    lane = lax.broadcasted_iota(jnp.int32, (tm, D_POOL), 1)
    row = lax.broadcasted_iota(jnp.int32, (tm, D_POOL), 0) + t0
    cnt = jnp.minimum(row + 1, _lane_sel(lane, 2, 4, 8, 16)).astype(F32)
    return lane, cnt


def _pool_fwd(ext_q, tmp_a, tmp_b, tm, t0):
    lane, cnt = _pool_counts(tm, t0)
    q = ext_q[pl.ds(HALO, tm), :]
    pooled = _lane_sel(lane, *_window_sums_back(ext_q, tmp_a, tmp_b, tm)) / cnt - q
    return pooled, lane, cnt


def _taps(src, w_of, offs, tm, zbuf):
    acc = None
    for r in range(8):
        ks = [k for k in range(len(offs)) if offs[k] % 8 == r]
        if not ks:
            continue
        rows = tm + (8 if r else 0)
        z = w_of(ks[0]) * src[pl.ds(offs[ks[0]] - r, rows), :]
        for k in ks[1:]:
            z = z + w_of(k) * src[pl.ds(offs[k] - r, rows), :]
        if r:
            zbuf[...] = z
            z = zbuf[pl.ds(r, tm), :]
        acc = z if acc is None else acc + z
    return acc


def _tap_grads(d_pad, src, offs, tm, g_ref, zbuf):
    ch = src.shape[-1]
    for r in range(8):
        ks = [k for k in range(len(offs)) if offs[k] % 8 == r]
        if not ks:
            continue
        rows = tm + (8 if r else 0)
        if r:
            zbuf[...] = d_pad[pl.ds(8 - r, rows), :]
        for k in ks:
            d = zbuf[...] if r else d_pad[pl.ds(8, rows), :]
            prod = d * src[pl.ds(offs[k] - r, rows), :]
            g_ref[k] += jnp.sum(prod.reshape(rows // 8, 8, ch), axis=0)


_CONV_OFFS = [HALO - (CONV_K - 1) + k for k in range(CONV_K)]


def _conv_fwd(ext_u, dww_ref, dwb, tm, zbuf):
    return dwb + _taps(ext_u, lambda k: dww_ref[k:k + 1, :], _CONV_OFFS, tm, zbuf)


def _ln_silu(c, lng, lnb):
    mu = jnp.mean(c, axis=-1, keepdims=True)
    cc = c - mu
    rstd = lax.rsqrt(jnp.mean(cc * cc, axis=-1, keepdims=True) + EPS)
    z = cc * rstd
    l = z * lng + lnb
    sl = _sig(l)
    return z, rstd, l, sl, l * sl


def _rg_conv(ext_x, cw_ref, cb, tm):
    xc = cb + cw_ref[0:1, :] * ext_x[pl.ds(HALO - (RG_CONV_K - 1), tm), :]
    for k in range(1, RG_CONV_K):
        xc = xc + cw_ref[k:k + 1, :] * ext_x[pl.ds(HALO - (RG_CONV_K - 1) + k, tm), :]
    return xc


def _softplus_neg(lam):
    return jnp.maximum(-lam, 0.0) + jnp.log(1.0 + jnp.exp(-jnp.abs(lam)))


def _rg_gates(xc, wa, ba, wx, bx, lam):
    xcb = xc.astype(BF16)
    r = _sig(_dot(xcb, wa) + ba)
    ig = _sig(_dot(xcb, wx) + bx)
    log_a = (-RG_C * r) * _softplus_neg(lam)
    a = jnp.exp(log_a)
    return r, ig, a, jnp.sqrt(_one_minus_sq(a, log_a))


def _scan_rows(a_ref, b_ref, out_ref, carry, tm, reverse):
    rows = lax.broadcasted_iota(jnp.int32, (8, D_RNN), 0)
    ngrp = tm // 8

    def grp(gi, hb):
        st = pl.multiple_of((ngrp - 1 - gi if reverse else gi) * 8, 8)
        a8 = a_ref[pl.ds(st, 8), :]
        b8 = b_ref[pl.ds(st, 8), :]
        out = jnp.zeros((8, D_RNN), F32)
        for j in (range(7, -1, -1) if reverse else range(8)):
            aj = jnp.broadcast_to(a8[j:j + 1, :], (8, D_RNN))
            bj = jnp.broadcast_to(b8[j:j + 1, :], (8, D_RNN))
            if reverse:
                cur = bj + hb
                hb = aj * cur
            else:
                cur = aj * hb + bj
                hb = cur
            out = jnp.where(rows == j, cur, out)
        out_ref[pl.ds(st, 8), :] = out
        return hb

    carry[...] = lax.fori_loop(0, ngrp, grp, carry[...])


_MIX_W = ("wp", "psc", "dww", "dwb", "lng", "lnb", "wpw", "cw", "cb", "wa", "ba", "wx", "bx", "lam")


def _mixer_fwd(h, g, w_in, mw):
    t = h.shape[0]
    tm = _row_tile(t, TM_MIX)

    def body(h_ref, g_ref, win_ref, wp, psc, dww, dwb, lng, lnb, wpw, cw, cb, wa, ba, wx, bx, lam,
             y_ref, p_ref, u_ref, hs_ref, conv_ref, xc_ref, gates_ref, ext_q, ext_u, ext_x, tmp_a, tmp_b, zbuf, a_s, b_s, hcar):
        i = pl.program_id(0)

        @pl.when(i == 0)
        def _():
            ext_q[0:HALO, :] = jnp.zeros((HALO, D_POOL), F32)
            ext_u[0:HALO, :] = jnp.zeros((HALO, D_CONV), F32)
            ext_x[0:HALO, :] = jnp.zeros((HALO, D_RNN), F32)
            hcar[...] = jnp.zeros((8, D_RNN), F32)

        u = (_rms(h_ref[...])[1] * g_ref[...]).astype(BF16)
        u_ref[...] = u
        p_ref[...] = _dot_nt(u, win_ref[...])

        ext_q[pl.ds(HALO, tm), :] = p_ref[:, 0:256]
        pooled, _, _ = _pool_fwd(ext_q, tmp_a, tmp_b, tm, i * tm)
        y_ref[:, 0:256] = (_dot(pooled.astype(BF16), wp[...]) * psc[...]).astype(BF16)

        ext_u[pl.ds(HALO, tm), :] = p_ref[:, 256:512] * _sig(p_ref[:, 512:768])
        conv = _conv_fwd(ext_u, dww, dwb[...], tm, zbuf)
        conv_ref[...] = conv
        act = _ln_silu(conv, lng[...], lnb[...])[4]
        y_ref[:, 256:512] = _dot(act.astype(BF16), wpw[...]).astype(BF16)

        ext_x[pl.ds(HALO, tm), :] = p_ref[:, 1280:1792]
        xc = _rg_conv(ext_x, cw, cb[...], tm)
        xc_ref[...] = xc
        r, ig, a, m = _rg_gates(xc, wa[...], ba[...], wx[...], bx[...], lam[...])
        for j, gate in enumerate((r, ig, a, m)):
            gates_ref[:, j * D_RNN:(j + 1) * D_RNN] = gate
        a_s[...] = a
        b_s[...] = m * (ig * xc)
        _scan_rows(a_s, b_s, hs_ref, hcar, tm, reverse=False)
        y_ref[:, 512:1024] = (_gelu_and_grad(p_ref[:, 768:1280])[0] * hs_ref[...]).astype(BF16)

        ext_q[0:HALO, :] = ext_q[pl.ds(tm, HALO), :]
        ext_u[0:HALO, :] = ext_u[pl.ds(tm, HALO), :]
        ext_x[0:HALO, :] = ext_x[pl.ds(tm, HALO), :]

    ws = [mw[k] for k in _MIX_W]
    row = lambda w: pl.BlockSpec((tm, w), lambda i: (i, 0))
    return pl.pallas_call(
        body, name="mixer_fwd", grid=(t // tm,),
        in_specs=[row(D_MODEL), _full(g), _resident(w_in)] + [_full(w) for w in ws],
        out_specs=[row(D_MODEL), row(D_IN), row(D_MODEL), row(D_RNN), row(D_CONV), row(D_RNN), row(4 * D_RNN)],
        out_shape=[jax.ShapeDtypeStruct((t, D_MODEL), BF16), jax.ShapeDtypeStruct((t, D_IN), F32),
                   jax.ShapeDtypeStruct((t, D_MODEL), BF16), jax.ShapeDtypeStruct((t, D_RNN), F32),
                   jax.ShapeDtypeStruct((t, D_CONV), F32), jax.ShapeDtypeStruct((t, D_RNN), F32),
                   jax.ShapeDtypeStruct((t, 4 * D_RNN), F32)],
        scratch_shapes=[pltpu.VMEM((HALO + tm, D_POOL), F32), pltpu.VMEM((HALO + tm, D_CONV), F32),
                        pltpu.VMEM((HALO + tm, D_RNN), F32), pltpu.VMEM((HALO + tm, D_POOL), F32),
                        pltpu.VMEM((HALO + tm, D_POOL), F32), pltpu.VMEM((tm + 8, D_CONV), F32),
                        pltpu.VMEM((tm, D_RNN), F32), pltpu.VMEM((tm, D_RNN), F32), pltpu.VMEM((8, D_RNN), F32)],
        compiler_params=_params("arbitrary"),
    )(h, g, w_in, *ws)


_MIX_G = (("wp", (D_POOL, D_POOL)), ("psc", (1, D_POOL)), ("dww", (32, 8, D_CONV)), ("dwb", (1, D_CONV)),
          ("lng", (1, D_CONV)), ("lnb", (1, D_CONV)), ("wpw", (D_CONV, D_CONV)), ("cw", (8, D_RNN)),
          ("cb", (1, D_RNN)), ("wa", (D_RNN, D_RNN)), ("ba", (1, D_RNN)), ("wx", (D_RNN, D_RNN)),
          ("bx", (1, D_RNN)), ("lam", (1, D_RNN)), ("g1", (1, D_MODEL)))


def _mixer_bwd(p, dh1, hs, conv, xc, gates, h0, g1, w_out, w_in, mw, deps=()):
    t = p.shape[0]
    tm = _row_tile(t, TM_MIX)
    nt = t // tm
    hb = tm // HALO

    def body(p_ref, ph_ref, dh1_ref, hs_ref, hsh_ref, conv_ref, xc_ref, gates_ref, h0_ref, g1_ref, wout_ref, win_ref,
             wp, psc, dww, dwb, lng, lnb, wpw, cw, cb, wa, ba, wx, bx, lam,
             dp_ref, dh0_ref, g_wp, g_psc, g_dww, g_dwb, g_lng, g_lnb, g_wpw, g_cw, g_cb, g_wa, g_ba, g_wx, g_bx, g_lam, g_g1,
             ext_q, ext_u, ext_x, ext_h, ee, dc_s, dx_s, tmp_a, tmp_b, zbuf, d_pad, a_s, b_s, g_s, gcar, dy_ref, dp_s):
        step = pl.program_id(0)
        i = nt - 1 - step
        grads = (g_wp, g_psc, g_dww, g_dwb, g_lng, g_lnb, g_wpw, g_cw, g_cb, g_wa, g_ba, g_wx, g_bx, g_lam, g_g1)
        dy_ref[...] = dh1_ref[...].astype(BF16)
        dy_cols = lambda lo, hi: _dot_nt(dy_ref[...], wout_ref[lo:hi, :])

        @pl.when(step == 0)
        def _():
            for gr in grads:
                gr[...] = jnp.zeros(gr.shape, F32)
            ee[pl.ds(tm, HALO), :] = jnp.zeros((HALO, D_POOL), F32)
            dc_s[pl.ds(tm, HALO), :] = jnp.zeros((HALO, D_CONV), F32)
            dx_s[pl.ds(tm, HALO), :] = jnp.zeros((HALO, D_RNN), F32)
            d_pad[0:8, :] = jnp.zeros((8, D_CONV), F32)
            d_pad[pl.ds(tm + 8, 8), :] = jnp.zeros((8, D_CONV), F32)
            gcar[...] = jnp.zeros((8, D_RNN), F32)

        hm = jnp.where(i == 0, 0.0, 1.0)

        ext_q[0:HALO, :] = ph_ref[:, 0:256] * hm
        ext_q[pl.ds(HALO, tm), :] = p_ref[:, 0:256]
        pooled, lane, cnt = _pool_fwd(ext_q, tmp_a, tmp_b, tm, i * tm)
        pooled_b = pooled.astype(BF16)
        dya = dy_cols(0, 256)
        g_psc[...] += _colsum(dya * _dot(pooled_b, wp[...]))
        dmixed_b = (dya * psc[...]).astype(BF16)
        dpooled = _dot_nt(dmixed_b, wp[...])
        g_wp[...] += _dot_tn(pooled_b, dmixed_b)
        ee[0:tm, :] = dpooled / cnt
        dp_s[:, 0:256] = _lane_sel(lane, *_window_sums_ahead(ee, tmp_a, tmp_b, tm)) - dpooled
        ee[pl.ds(tm, HALO), :] = ee[0:HALO, :]

        v = p_ref[:, 256:512]
        s = _sig(p_ref[:, 512:768])
        ext_u[0:HALO, :] = ph_ref[:, 256:512] * _sig(ph_ref[:, 512:768]) * hm
        ext_u[pl.ds(HALO, tm), :] = v * s
        z, rstd, l, sl, act = _ln_silu(conv_ref[...], lng[...], lnb[...])
        dyb_b = dy_cols(256, 512).astype(BF16)
        dact = _dot_nt(dyb_b, wpw[...])
        g_wpw[...] += _dot_tn(act.astype(BF16), dyb_b)
        dl = dact * (sl * (1.0 + l * (1.0 - sl)))
        g_lng[...] += _colsum(dl * z)
        g_lnb[...] += _colsum(dl)
        dz = dl * lng[...]
        dc = rstd * (dz - jnp.mean(dz, axis=-1, keepdims=True) - z * jnp.mean(dz * z, axis=-1, keepdims=True))
        g_dwb[...] += _colsum(dc)
        dc_s[0:tm, :] = dc
        d_pad[pl.ds(8, tm), :] = dc
        _tap_grads(d_pad, ext_u, _CONV_OFFS, tm, g_dww, zbuf)
        du0 = _taps(dc_s, lambda j: dww[CONV_K - 1 - j:CONV_K - j, :], list(range(CONV_K)), tm, zbuf)
        dp_s[:, 256:512] = du0 * s
        dp_s[:, 512:768] = du0 * v * (s * (1.0 - s))
        dc_s[pl.ds(tm, HALO), :] = dc_s[0:HALO, :]

        ext_x[0:HALO, :] = ph_ref[:, 1280:1792] * hm
        ext_x[pl.ds(HALO, tm), :] = p_ref[:, 1280:1792]
        xc = xc_ref[...]
        xcb = xc.astype(BF16)
        r, ig, a, m = (gates_ref[:, j * D_RNN:(j + 1) * D_RNN] for j in range(4))
        sp = _softplus_neg(lam[...])
        ext_h[0:HALO, :] = hsh_ref[...] * hm
        ext_h[pl.ds(HALO, tm), :] = hs_ref[...]
        dyc = dy_cols(512, 1024)
        gl, dgl = _gelu_and_grad(p_ref[:, 768:1280])
        dp_s[:, 768:1280] = dyc * hs_ref[...] * dgl
        a_s[...] = a
        b_s[...] = dyc * gl
        _scan_rows(a_s, b_s, g_s, gcar, tm, reverse=True)
        g = g_s[...]
        da = g * ext_h[pl.ds(HALO - 1, tm), :]
        dm = g * (ig * xc)
        dig = g * (m * xc)
        dlog_a = da * a - dm * (a * a) / m
        g_lam[...] += _colsum(dlog_a * (-RG_C * r)) * (-_sig(-lam[...]))
        dra = (dlog_a * (-RG_C * sp)) * (r * (1.0 - r))
        dia = dig * (ig * (1.0 - ig))
        g_ba[...] += _colsum(dra)
        g_bx[...] += _colsum(dia)
        dra_b = dra.astype(BF16)
        dia_b = dia.astype(BF16)
        dxc = g * (m * ig) + _dot_nt(dra_b, wa[...]) + _dot_nt(dia_b, wx[...])
        g_wa[...] += _dot_tn(xcb, dra_b)
        g_wx[...] += _dot_tn(xcb, dia_b)
        g_cb[...] += _colsum(dxc)
        dx_s[0:tm, :] = dxc
        for k in range(RG_CONV_K):
            g_cw[k:k + 1, :] += _colsum(dxc * ext_x[pl.ds(HALO - (RG_CONV_K - 1) + k, tm), :])
        dxin = cw[RG_CONV_K - 1:RG_CONV_K, :] * dxc
        for j in range(1, RG_CONV_K):
            dxin = dxin + cw[RG_CONV_K - 1 - j:RG_CONV_K - j, :] * dx_s[pl.ds(j, tm), :]
        dp_s[:, 1280:1792] = dxin
        dx_s[pl.ds(tm, HALO), :] = dx_s[0:HALO, :]

        dpb = dp_s[...].astype(BF16)
        dp_ref[...] = dpb
        du = _dot(dpb, win_ref[...])
        r, n = _rms(h0_ref[...])
        g_g1[...] += _colsum(du * n)
        dh0_ref[...] = dh1_ref[...] + _rms_bwd(du, n, r, g1_ref[...])

    ws = [mw[k] for k in _MIX_W]
    tile = lambda w: pl.BlockSpec((tm, w), lambda s: (nt - 1 - s, 0))
    halo = lambda w: pl.BlockSpec((HALO, w), lambda s: (jnp.maximum((nt - 1 - s) * hb - 1, 0), 0))
    outs = pl.pallas_call(
        _after(body, 12 + len(ws), deps), name="mixer_bwd", grid=(nt,),
        in_specs=[tile(D_IN), halo(D_IN), tile(D_MODEL), tile(D_RNN), halo(D_RNN), tile(D_CONV), tile(D_RNN), tile(4 * D_RNN),
                  tile(D_MODEL), _full(g1),
                  _resident(w_out), _resident(w_in)] + [_full(w) for w in ws] + [ANY] * len(deps),
        out_specs=[tile(D_IN), tile(D_MODEL)] + [pl.BlockSpec(shp, lambda s, nd=len(shp): (0,) * nd) for _, shp in _MIX_G],
        out_shape=[jax.ShapeDtypeStruct((t, D_IN), BF16), jax.ShapeDtypeStruct((t, D_MODEL), F32)]
        + [jax.ShapeDtypeStruct(shp, F32) for _, shp in _MIX_G],
        scratch_shapes=[pltpu.VMEM((HALO + tm, D_POOL), F32), pltpu.VMEM((HALO + tm, D_CONV), F32),
                        pltpu.VMEM((HALO + tm, D_RNN), F32), pltpu.VMEM((HALO + tm, D_RNN), F32),
                        pltpu.VMEM((tm + HALO, D_POOL), F32), pltpu.VMEM((tm + HALO, D_CONV), F32),
                        pltpu.VMEM((tm + HALO, D_RNN), F32), pltpu.VMEM((HALO + tm, D_POOL), F32),
                        pltpu.VMEM((HALO + tm, D_POOL), F32), pltpu.VMEM((tm + 8, D_CONV), F32),
                        pltpu.VMEM((tm + 16, D_CONV), F32), pltpu.VMEM((tm, D_RNN), F32),
                        pltpu.VMEM((tm, D_RNN), F32), pltpu.VMEM((tm, D_RNN), F32), pltpu.VMEM((8, D_RNN), F32),
                        pltpu.VMEM((tm, D_MODEL), BF16), pltpu.VMEM((tm, D_IN), F32)],
        compiler_params=_params("arbitrary"),
    )(p, p, dh1, hs, hs, conv, xc, gates, h0, g1, w_out, w_in, *ws, *deps)
    return outs[0], outs[1], {k: o for (k, _), o in zip(_MIX_G, outs[2:])}


def _mid_fwd(y, h0, w_out, g, w_up):
    t = h0.shape[0]
    tm = _row_tile(t, TM_MAT)

    def body(y_ref, h0_ref, wo_ref, g_ref, wu_ref, h1_ref, u2_ref, f_ref):
        h1 = h0_ref[...] + _dot(y_ref[...], wo_ref[...])
        h1_ref[...] = h1
        u2 = (_rms(h1)[1] * g_ref[...]).astype(BF16)
        u2_ref[...] = u2
        for c in range(D_FF // FF_CHUNK):
            f_ref[:, c * FF_CHUNK:(c + 1) * FF_CHUNK] = _dot(u2, wu_ref[c]).astype(BF16)

    row = lambda w: pl.BlockSpec((tm, w), lambda i: (i, 0))
    return pl.pallas_call(
        body, name="mid_fwd", grid=(t // tm,),
        in_specs=[row(D_MODEL), row(D_MODEL), _resident(w_out), _full(g), _resident(w_up)],
        out_specs=[row(D_MODEL), row(D_MODEL), row(D_FF)],
        out_shape=[jax.ShapeDtypeStruct((t, D_MODEL), F32), jax.ShapeDtypeStruct((t, D_MODEL), BF16),
                   jax.ShapeDtypeStruct((t, D_FF), BF16)],
        compiler_params=_params("parallel"),
    )(y, h0, w_out, g, w_up)


def _down_proj(f_ref, h1_ref, wd_ref):
    acc = h1_ref[...]
    for c in range(D_FF // FF_CHUNK):
        cols = slice(c * FF_CHUNK, (c + 1) * FF_CHUNK)
        a = jnp.square(jnp.maximum(f_ref[:, cols].astype(F32), 0.0)).astype(BF16)
        acc = acc + _dot(a, wd_ref[cols, :])
    return acc


def _down_fwd(f, h1, w_down):
    t = h1.shape[0]
    tm = _row_tile(t, TM_MAT)

    def body(f_ref, h1_ref, wd_ref, h2_ref):
        h2_ref[...] = _down_proj(f_ref, h1_ref, wd_ref)

    row = lambda w: pl.BlockSpec((tm, w), lambda i: (i, 0))
    return pl.pallas_call(
        body, name="down_fwd", grid=(t // tm,),
        in_specs=[row(D_FF), row(D_MODEL), _resident(w_down)], out_specs=row(D_MODEL),
        out_shape=jax.ShapeDtypeStruct((t, D_MODEL), F32),
        compiler_params=_params("parallel"),
    )(f, h1, w_down)


def _down_fwd_loss(f, h1, w_down, g, tgt, t_real):
    t = h1.shape[0]
    tm = _row_tile(t, TM_MAT)

    def body(f_ref, h1_ref, wd_ref, g_ref, tgt_ref, loss_ref, dh_ref, dg_ref):
        i = pl.program_id(0)

        @pl.when(i == 0)
        def _():
            loss_ref[...] = jnp.zeros(loss_ref.shape, F32)
            dg_ref[...] = jnp.zeros(dg_ref.shape, F32)

        r, n = _rms(_down_proj(f_ref, h1_ref, wd_ref))
        row = lax.broadcasted_iota(jnp.int32, (tm, 1), 0) + i * tm
        valid = jnp.logical_and(row >= N_META, row < t_real)
        diff = jnp.where(valid, n * g_ref[...] - tgt_ref[...], 0.0)
        loss_ref[...] += 0.5 * jnp.sum(jnp.mean(diff * diff, axis=-1, keepdims=True))
        dy = diff * (1.0 / D_MODEL)
        dg_ref[...] += _colsum(dy * n)
        dh_ref[...] = _rms_bwd(dy, n, r, g_ref[...])

    row = lambda w: pl.BlockSpec((tm, w), lambda i: (i, 0))
    return pl.pallas_call(
        body, name="down_fwd_loss", grid=(t // tm,),
        in_specs=[row(D_FF), row(D_MODEL), _resident(w_down), _full(g), row(D_MODEL)],
        out_specs=[pl.BlockSpec((8, 128), lambda i: (0, 0)), row(D_MODEL), pl.BlockSpec((1, D_MODEL), lambda i: (0, 0))],
        out_shape=[jax.ShapeDtypeStruct((8, 128), F32), jax.ShapeDtypeStruct((t, D_MODEL), F32),
                   jax.ShapeDtypeStruct((1, D_MODEL), F32)],
        compiler_params=_params("arbitrary"),
    )(f, h1, w_down, g, tgt)


def _mlp_bwd(dh2, f, h1, g, w_up, w_down, deps=()):
    t = dh2.shape[0]
    tm = _row_tile(t, TM_MLP_BWD)

    def body(dh2_ref, f_ref, wd_ref, wu_ref, h1_ref, g_ref, df_ref, dh1_ref, dg_ref):
        @pl.when(pl.program_id(0) == 0)
        def _():
            dg_ref[...] = jnp.zeros(dg_ref.shape, F32)

        dh2 = dh2_ref[...]
        dhb = dh2.astype(BF16)
        du2 = None
        for c in range(D_FF // FF_CHUNK):
            cols = slice(c * FF_CHUNK, (c + 1) * FF_CHUNK)
            dact = _dot_nt(dhb, wd_ref[c])
            df = (dact * (2.0 * jnp.maximum(f_ref[:, cols].astype(F32), 0.0))).astype(BF16)
            df_ref[:, cols] = df
            part = _dot_nt(df, wu_ref[c])
            du2 = part if du2 is None else du2 + part
        r, n = _rms(h1_ref[...])
        dg_ref[...] += _colsum(du2 * n)
        dh1_ref[...] = dh2 + _rms_bwd(du2, n, r, g_ref[...])

    row = lambda w: pl.BlockSpec((tm, w), lambda i: (i, 0))
    return pl.pallas_call(
        _after(body, 6, deps), name="mlp_bwd", grid=(t // tm,),
        in_specs=[row(D_MODEL), row(D_FF), _resident(w_down), _resident(w_up), row(D_MODEL), _full(g)] + [ANY] * len(deps),
        out_specs=[row(D_FF), row(D_MODEL), pl.BlockSpec((1, D_MODEL), lambda i: (0, 0))],
        out_shape=[jax.ShapeDtypeStruct((t, D_FF), BF16), jax.ShapeDtypeStruct((t, D_MODEL), F32),
                   jax.ShapeDtypeStruct((1, D_MODEL), F32)],
        compiler_params=_params("arbitrary"),
    )(dh2, f, w_down, w_up, h1, g, *deps)


def _tn_matmul(a, b, kc, nc, relu2, name, deps=()):
    t, k = a.shape
    n = b.shape[1]
    tt = _row_tile(t, TM_MAT)
    gk, gn = k // kc, n // nc

    def body(a_ref, b_ref, o_ref):
        @pl.when(pl.program_id(2) == 0)
        def _():
            o_ref[...] = jnp.zeros(o_ref.shape, F32)

        av = a_ref[...]
        if relu2:
            av = jnp.square(jnp.maximum(av.astype(F32), 0.0))
        o_ref[...] += _dot_tn(av.astype(BF16), b_ref[...].astype(BF16))

    return pl.pallas_call(
        _after(body, 2, deps), name=name, grid=(gk, gn, t // tt),
        in_specs=[pl.BlockSpec((tt, kc), lambda ik, jn, it: (it, ik)), pl.BlockSpec((tt, nc), lambda ik, jn, it: (it, jn))]
        + [ANY] * len(deps),
        out_specs=pl.BlockSpec((None, kc, nc), lambda ik, jn, it: (ik * gn + jn, 0, 0)),
        out_shape=jax.ShapeDtypeStruct((gk * gn, kc, nc), F32),
        compiler_params=_params("parallel", "parallel", "arbitrary"),
    )(a, b, *deps)


def _block_diag(blocks):
    nb, hd, _ = blocks.shape
    eye = jnp.eye(nb, dtype=blocks.dtype)
    return (blocks[:, :, None, :] * eye[:, None, :, None]).reshape(nb * hd, nb * hd)


def _diag_blocks(m, nb):
    hd = m.shape[0] // nb
    eye = jnp.eye(nb, dtype=m.dtype)
    return jnp.sum(m.reshape(nb, hd, nb, hd) * eye[:, None, :, None], axis=2)


def _mixer_weights(w, l):
    row = lambda a: a.reshape(1, -1)
    return dict(
        wp=_block_diag(w["pool_w"][l]).astype(BF16), psc=row(w["pool_scale"][l]),
        dww=jnp.pad(w["convb_dw_w"][l], ((0, 32 - CONV_K), (0, 0))), dwb=row(w["convb_dw_b"][l]),
        lng=row(w["convb_ln_g"][l]), lnb=row(w["convb_ln_b"][l]), wpw=w["convb_pw_w"][l].astype(BF16),
        cw=jnp.pad(w["rg_conv_w"][l], ((0, 8 - RG_CONV_K), (0, 0))), cb=row(w["rg_conv_b"][l]),
        wa=_block_diag(w["rg_w_a"][l]).astype(BF16), ba=row(w["rg_b_a"][l]),
        wx=_block_diag(w["rg_w_x"][l]).astype(BF16), bx=row(w["rg_b_x"][l]), lam=row(w["rg_lambda"][l]))


def _local_step(h, tgt, t_real, w, fetch, hooks):
    depth = 2
    saved = []
    big = []
    for l in range(depth):
        mw = _mixer_weights(w, l)
        g1 = w["mix_norm_g"][l].reshape(1, -1)
        g2 = w["mlp_norm_g"][l].reshape(1, -1)
        wl = dict(w_in=fetch(l, "w_in", h))
        y, p, u, hs, conv, xc, gates = _mixer_fwd(h, g1, wl["w_in"], mw)
        wl["w_out"], wl["w_up"] = fetch(l, "w_out", y), fetch(l, "w_up", y)
        h1, u2, f = _mid_fwd(y, h, wl["w_out"], g2, wl["w_up"])
        wl["w_down"] = fetch(l, "w_down", f)
        if l == depth - 1:
            loss, dh, dgf = _down_fwd_loss(f, h1, wl["w_down"].reshape(D_FF, D_MODEL), w["final_norm_g"].reshape(1, -1), tgt,
                                           t_real)
            h2 = None
        else:
            h2 = _down_fwd(f, h1, wl["w_down"].reshape(D_FF, D_MODEL))
        saved.append(dict(mw=mw, g1=g1, g2=g2, h0=h, p=p, u=u, y=y, hs=hs, conv=conv, xc=xc, gates=gates, h1=h1, u2=u2, f=f))
        big.append(wl)
        h = h2

    gs = {k: [None] * depth for k in ("mix_norm_g", "mlp_norm_g", "pool_w", "pool_scale", "convb_dw_w", "convb_dw_b",
                                      "convb_ln_g", "convb_ln_b", "convb_pw_w", "rg_conv_w", "rg_conv_b", "rg_w_a",
                                      "rg_b_a", "rg_w_x", "rg_b_x", "rg_lambda")}
    deps = ()
    for l in reversed(range(depth)):
        s, wl = saved[l], big[l]
        df, dh1, dg2 = _mlp_bwd(dh, s["f"], s["h1"], s["g2"], wl["w_up"], wl["w_down"], deps)
        deps = hooks.point(l, "mlp_bwd", dh1)
        g_down = _tn_matmul(s["f"], dh, FF_CHUNK, D_MODEL, True, "dw_down", deps)
        hooks.grad(l, "w_down", g_down)
        deps = hooks.point(l, "dw_down", g_down)
        g_up = _tn_matmul(s["u2"], df, D_MODEL, FF_CHUNK, False, "dw_up", deps)
        hooks.grad(l, "w_up", g_up)
        deps = hooks.point(l, "dw_up", g_up)
        g_out = _tn_matmul(s["y"], dh1, D_MODEL, D_MODEL, False, "dw_out", deps)
        hooks.grad(l, "w_out", g_out.reshape(N_CHIPS, D_MODEL // N_CHIPS, D_MODEL))
        deps = hooks.point(l, "dw_out", g_out)
        dp, dh, mg = _mixer_bwd(s["p"], dh1, s["hs"], s["conv"], s["xc"], s["gates"], s["h0"], s["g1"], wl["w_out"],
                                wl["w_in"], s["mw"], deps)
        gs["mix_norm_g"][l] = mg["g1"][0]
        gs["mlp_norm_g"][l] = dg2[0]
        gs["pool_w"][l] = _diag_blocks(mg["wp"], D_POOL // POOL_GW)
        gs["pool_scale"][l] = mg["psc"][0]
        gs["convb_dw_w"][l] = jnp.sum(mg["dww"][:CONV_K], axis=1)
        gs["convb_dw_b"][l] = mg["dwb"][0]
        gs["convb_ln_g"][l] = mg["lng"][0]
        gs["convb_ln_b"][l] = mg["lnb"][0]
        gs["convb_pw_w"][l] = mg["wpw"]
        gs["rg_conv_w"][l] = mg["cw"][:RG_CONV_K]
        gs["rg_conv_b"][l] = mg["cb"][0]
        gs["rg_w_a"][l] = _diag_blocks(mg["wa"], D_RNN // RG_HD)
        gs["rg_b_a"][l] = mg["ba"][0]
        gs["rg_w_x"][l] = _diag_blocks(mg["wx"], D_RNN // RG_HD)
        gs["rg_b_x"][l] = mg["bx"][0]
        gs["rg_lambda"][l] = mg["lam"][0]
        if l == 0:
            gsmall = {k: jnp.stack(v) for k, v in gs.items()}
            gsmall["final_norm_g"] = dgf[0]
            gsmall["meta_tokens"] = dh[:N_META]
            started = hooks.small(gsmall)
        deps = hooks.point(l, "mixer_bwd", started[0] if l == 0 and started else dh)
        g_in = _tn_matmul(dp, s["u"], D_IN, D_MODEL, False, "dw_in", deps).reshape(N_CHIPS, D_IN // N_CHIPS, D_MODEL)
        hooks.grad(l, "w_in", g_in)
        deps = hooks.point(l, "dw_in", g_in)
    return loss[0, 0], dh


def _place():
    return lax.axis_index("x"), lax.axis_index("y"), lax.axis_index("c")


def _other_chips(x, y):
    return [(1 - x, y), (x, 1 - y), (1 - x, 1 - y)]


HBM_SPEC = pl.BlockSpec(memory_space=pltpu.HBM)
SEM_SPEC = pl.BlockSpec(memory_space=pltpu.SEMAPHORE)
DATAFLOW = pltpu.SideEffectType.DATAFLOW_SIDE_EFFECTING


def _gather_copies(src_refs, land_refs, send_sem, recv_sem, first):
    x, y, c = _place()
    me = 2 * x + y
    out = []
    for n in range(len(src_refs)):
        for j, (px, py) in enumerate(_other_chips(x, y)):
            out.append(pltpu.make_async_remote_copy(src_refs[n], land_refs[n].at[me], send_sem.at[first + 3 * n + j],
                                                    recv_sem.at[first + 3 * n + j], device_id=(px, py, c), device_id_type=MESH))
    return out


def _gather_start(groups, me):
    srcs = [pltpu.with_memory_space_constraint(s, pltpu.HBM) for g in groups for s in g]
    lands = [pltpu.with_memory_space_constraint(
        lax.dynamic_update_slice(jnp.zeros((N_CHIPS,) + s.shape, s.dtype), s[None], (me,) + (0,) * s.ndim), pltpu.HBM)
        for g in groups for s in g]
    n, ng = len(srcs), len(groups)
    first = [sum(len(g) for g in groups[:i]) for i in range(ng)]

    def body(*refs):
        src_refs, land_refs = refs[:n], refs[n:2 * n]
        sems = refs[2 * n:2 * n + 2 * ng]
        token = refs[-1]
        for gi, g in enumerate(groups):
            lo, hi = first[gi], first[gi] + len(g)
            for cp in _gather_copies(src_refs[lo:hi], land_refs[lo:hi], sems[2 * gi], sems[2 * gi + 1], 0):
                cp.start()
        token[...] = jnp.zeros(token.shape, token.dtype)

    sem_shapes = [pltpu.SemaphoreType.DMA((3 * len(g),)) for g in groups for _ in range(2)]
    outs = pl.pallas_call(
        body, name="gather_start",
        out_shape=sem_shapes + [pltpu.HBM(a.shape, a.dtype) for a in srcs + lands] + [jax.ShapeDtypeStruct((8, 128), F32)],
        in_specs=[HBM_SPEC] * (2 * n),
        out_specs=[SEM_SPEC] * (2 * ng) + [HBM_SPEC] * (2 * n) + [pl.BlockSpec(memory_space=pltpu.VMEM)],
        input_output_aliases={i: 2 * ng + i for i in range(2 * n)},
        compiler_params=pltpu.CompilerParams(has_side_effects=DATAFLOW),
    )(*srcs, *lands)
    sems, thru, token = outs[:2 * ng], outs[2 * ng:2 * ng + 2 * n], outs[-1]
    state = []
    for gi, g in enumerate(groups):
        lo, hi = first[gi], first[gi] + len(g)
        state.append((sems[2 * gi], sems[2 * gi + 1], thru[lo:hi], thru[n + lo:n + hi]))
    return state, token


def _gather_wait(state, after, name):
    send_sem, recv_sem, srcs, lands = state
    n = len(srcs)

    def body(*refs):
        src_refs, land_refs = refs[:n], refs[n:2 * n]
        send, recv = refs[2 * n], refs[2 * n + 1]
        for cp in _gather_copies(src_refs, land_refs, send, recv, 0):
            cp.wait_send()
            cp.wait_recv()

    outs = pl.pallas_call(
        body, name=name,
        out_shape=[pltpu.HBM(a.shape, a.dtype) for a in list(srcs) + list(lands)],
        in_specs=[HBM_SPEC] * (2 * n) + [SEM_SPEC, SEM_SPEC, ANY],
        out_specs=[HBM_SPEC] * (2 * n),
        input_output_aliases={i: i for i in range(2 * n)},
        compiler_params=pltpu.CompilerParams(has_side_effects=DATAFLOW),
    )(*srcs, *lands, send_sem, recv_sem, after)
    return outs[n:]


def _add_halves(g, recv, c1):
    nk, r, cd = g.shape
    r2 = r // 2

    def body(c_ref, g_ref, r_ref, pab_ref):
        pab_ref[...] = (g_ref[...] + r_ref[...]).astype(BF16)

    blk = pl.BlockSpec((None, r2, cd), lambda k, c_ref: (k, 0, 0))
    return pl.pallas_call(
        body, name="rs_add_halves",
        grid_spec=pltpu.PrefetchScalarGridSpec(
            num_scalar_prefetch=1, grid=(nk,),
            in_specs=[pl.BlockSpec((None, r2, cd), lambda k, c_ref: (k, c_ref[0], 0)), blk], out_specs=blk),
        out_shape=jax.ShapeDtypeStruct((nk, r2, cd), BF16),
        compiler_params=_params("parallel"),
    )(c1, g, recv)


def _sum_partials(g, recv_sibling, recv_chips, c_me):
    nk, r, cd = g.shape
    r2 = r // 2

    def body(cm_ref, g_ref, a_ref, r_ref, s_ref):
        own = g_ref[...] + a_ref[...]
        s_ref[...] = ((own + r_ref[0].astype(F32)) + r_ref[1].astype(F32)) + r_ref[2].astype(F32)

    return pl.pallas_call(
        body, name="rs_sum_partials",
        grid_spec=pltpu.PrefetchScalarGridSpec(
            num_scalar_prefetch=1, grid=(1,),
            in_specs=[pl.BlockSpec((None, r2, cd), lambda i, cm: (cm[1], cm[0], 0)),
                      pl.BlockSpec((None, r2, cd), lambda i, cm: (cm[1], 0, 0)),
                      pl.BlockSpec((3, r2, cd), lambda i, cm: (0, 0, 0))],
            out_specs=pl.BlockSpec((r2, cd), lambda i, cm: (0, 0))),
        out_shape=jax.ShapeDtypeStruct((r2, cd), F32),
        compiler_params=_params("arbitrary"),
    )(c_me, g, recv_sibling, recv_chips)


def _split_start(name, srcs, lands, ncopies, make_copies):
    srcs = [pltpu.with_memory_space_constraint(s, pltpu.HBM) for s in srcs]
    lands = [pltpu.with_memory_space_constraint(a, pltpu.HBM) for a in lands]
    n, m = len(srcs), len(lands)

    def body(*refs):
        src_refs, land_refs = refs[:n], refs[n:n + m]
        send, recv, token = refs[n + m], refs[n + m + 1], refs[-1]
        for cp in make_copies(src_refs, land_refs, send, recv):
            cp.start()
        token[...] = jnp.zeros(token.shape, token.dtype)

    outs = pl.pallas_call(
        body, name=name,
        out_shape=[pltpu.SemaphoreType.DMA((ncopies,)), pltpu.SemaphoreType.DMA((ncopies,))]
        + [pltpu.HBM(a.shape, a.dtype) for a in srcs + lands] + [jax.ShapeDtypeStruct((8, 128), F32)],
        in_specs=[HBM_SPEC] * (n + m),
        out_specs=[SEM_SPEC, SEM_SPEC] + [HBM_SPEC] * (n + m) + [pl.BlockSpec(memory_space=pltpu.VMEM)],
        input_output_aliases={i: 2 + i for i in range(n + m)},
        compiler_params=pltpu.CompilerParams(has_side_effects=DATAFLOW),
    )(*srcs, *lands)
    return (outs[0], outs[1], outs[2:2 + n], outs[2 + n:2 + n + m], make_copies), outs[-1]


def _split_wait(name, state, after):
    send_sem, recv_sem, srcs, lands, make_copies = state
    n, m = len(srcs), len(lands)

    def body(*refs):
        src_refs, land_refs = refs[:n], refs[n:n + m]
        for cp in make_copies(src_refs, land_refs, refs[n + m], refs[n + m + 1]):
            cp.wait_send()
            cp.wait_recv()

    outs = pl.pallas_call(
        body, name=name,
        out_shape=[pltpu.HBM(a.shape, a.dtype) for a in list(srcs) + list(lands)],
        in_specs=[HBM_SPEC] * (n + m) + [SEM_SPEC, SEM_SPEC, ANY],
        out_specs=[HBM_SPEC] * (n + m),
        input_output_aliases={i: i for i in range(n + m)},
        compiler_params=pltpu.CompilerParams(has_side_effects=DATAFLOW),
    )(*srcs, *lands, send_sem, recv_sem, after)
    return outs[:n], outs[n:]


def _copies_to_sibling(src_of):
    def make(src_refs, land_refs, send, recv):
        x, y, c = _place()
        return [pltpu.make_async_remote_copy(src_of(src_refs[i], c), land_refs[i], send.at[i], recv.at[i],
                                             device_id=(x, y, 1 - c), device_id_type=MESH) for i in range(len(src_refs))]
    return make


def _copies_to_chips(src_refs, land_refs, send, recv):
    x, y, c = _place()
    return [pltpu.make_async_remote_copy(src_refs[i].at[2 * px + py], land_refs[i].at[j], send.at[3 * i + j], recv.at[3 * i + j],
                                         device_id=(px, py, c), device_id_type=MESH)
            for i in range(len(src_refs)) for j, (px, py) in enumerate(_other_chips(x, y))]


def _other_half_rows(ref, c):
    r2 = ref.shape[1] // 2
    return ref.at[:, pl.ds(pl.multiple_of((1 - c) * r2, 8), r2)]


class _ReduceScatter:
    def __init__(self, tag, grads, c1, me1):
        self.tag, self.grads, self.c1, self.me1 = tag, grads, c1, me1

    def start(self):
        lands = [lax.empty((g.shape[0], g.shape[1] // 2, g.shape[2]), F32) for g in self.grads]
        self.state, token = _split_start("rs_%s_a_start" % self.tag, self.grads, lands, len(self.grads),
                                         _copies_to_sibling(_other_half_rows))
        return token

    def to_chips(self, after):
        self.halves = _split_wait("rs_%s_a_wait" % self.tag, self.state, after)
        pabs = [_add_halves(g, r, self.c1) for g, r in zip(*self.halves)]
        lands = [lax.empty((3,) + p.shape[1:], BF16) for p in pabs]
        self.state, token = _split_start("rs_%s_b_start" % self.tag, pabs, lands, 3 * len(pabs), _copies_to_chips)
        return token

    def to_sibling(self, after):
        _, recv = _split_wait("rs_%s_b_wait" % self.tag, self.state, after)
        c_me = jnp.concatenate([self.c1, self.me1])
        sums = [_sum_partials(g, ra, rb, c_me) for g, ra, rb in zip(*self.halves, recv)]
        lands = [lax.empty(s.shape, F32) for s in sums]
        self.state, token = _split_start("rs_%s_c_start" % self.tag, sums, lands, len(sums),
                                         _copies_to_sibling(lambda ref, c: ref))
        return token

    def finish(self, after):
        return list(zip(*_split_wait("rs_%s_c_wait" % self.tag, self.state, after)))


def _add_lists(a_list, b_list):
    n = len(a_list)

    def body(*refs):
        for i in range(n):
            refs[2 * n + i][...] = refs[i][...] + refs[n + i][...]

    vm = pl.BlockSpec(memory_space=pltpu.VMEM)
    return pl.pallas_call(
        body, name="add_lists", in_specs=[vm] * (2 * n), out_specs=[vm] * n,
        out_shape=[jax.ShapeDtypeStruct(a.shape, a.dtype) for a in a_list],
        compiler_params=pltpu.CompilerParams(vmem_limit_bytes=VMEM_LIMIT),
    )(*a_list, *b_list)


def _copies_to_peer(stage):
    def make(src_refs, land_refs, send, recv):
        x, y, c = _place()
        peer = [(x, y, 1 - c), (1 - x, y, c), (x, 1 - y, c)][stage]
        return [pltpu.make_async_remote_copy(src_refs[i], land_refs[i], send.at[i], recv.at[i], device_id=peer, device_id_type=MESH)
                for i in range(len(src_refs))]
    return make


class _AllReduceSmall:
    def __init__(self, vs):
        self.vs, self.stage = list(vs), 0

    def _start(self):
        lands = [lax.empty(v.shape, v.dtype) for v in self.vs]
        self.state, token = _split_start("ar_small_start_%d" % self.stage, self.vs, lands, len(self.vs), _copies_to_peer(self.stage))
        return token

    def start(self):
        return self._start()

    def step(self, after):
        mine, theirs = _split_wait("ar_small_wait_%d" % self.stage, self.state, after)
        self.vs = _add_lists(mine, theirs)
        self.stage += 1
        return self._start() if self.stage < 3 else self.vs[0]


def _adamw_math(w, g, m, v):
    m = ADAM_B1 * m + (1.0 - ADAM_B1) * g
    v = ADAM_B2 * v + (1.0 - ADAM_B2) * jnp.square(g)
    m_hat = m / (1.0 - ADAM_B1 ** ADAM_STEP)
    v_hat = v / (1.0 - ADAM_B2 ** ADAM_STEP)
    return -ADAM_LR * (m_hat / (jnp.sqrt(v_hat) + ADAM_EPS) + ADAM_WD * w), m, v


def _adamw_big_layer(layer, w, m, v, own, sib, c1, prev):
    _, r, cd = w.shape
    r2 = r // 2

    def body(c_ref, w_ref, m_ref, v_ref, own_ref, sib_ref, *rest):
        g_ref, d_ref, mo_ref, vo_ref, token = rest[-5:]
        g = jnp.where(pl.program_id(0) == c_ref[0], own_ref[...], sib_ref[...])
        g_ref[...] = g
        d_ref[...], mo_ref[...], vo_ref[...] = _adamw_math(w_ref[...], g, m_ref[...], v_ref[...])
        token[...] = jnp.zeros(token.shape, F32)

    blk = pl.BlockSpec((None, r2, cd), lambda hh, c_ref: (layer, hh, 0))
    half = pl.BlockSpec((r2, cd), lambda hh, c_ref: (0, 0))
    prev = () if prev is None else tuple(prev)
    outs = pl.pallas_call(
        body, name="adamw_big",
        grid_spec=pltpu.PrefetchScalarGridSpec(
            num_scalar_prefetch=1, grid=(2,), in_specs=[blk, blk, blk, half, half] + [ANY] * len(prev),
            out_specs=[blk] * 4 + [pl.BlockSpec((8, 128), lambda hh, c_ref: (0, 0))]),
        out_shape=[jax.ShapeDtypeStruct(w.shape, F32)] * 4 + [jax.ShapeDtypeStruct((8, 128), F32)],
        input_output_aliases={6 + i: i for i in range(len(prev))},
        compiler_params=_params("arbitrary"),
    )(c1, w, m, v, own, sib, *prev)
    return outs[:4], outs[4]


def _adamw_small(ws, gs, ms, vs):
    n = len(ws)

    def body(*refs):
        w_refs, g_refs, m_refs, v_refs = refs[:n], refs[n:2 * n], refs[2 * n:3 * n], refs[3 * n:4 * n]
        outs = refs[4 * n:]
        for i in range(n):
            outs[3 * i][...], outs[3 * i + 1][...], outs[3 * i + 2][...] = _adamw_math(
                w_refs[i][...], g_refs[i][...], m_refs[i][...], v_refs[i][...])

    vm = pl.BlockSpec(memory_space=pltpu.VMEM)
    outs = pl.pallas_call(
        body, name="adamw_small", in_specs=[vm] * (4 * n), out_specs=[vm] * (3 * n),
        out_shape=[jax.ShapeDtypeStruct(w.shape, F32) for w in ws for _ in range(3)],
        compiler_params=pltpu.CompilerParams(vmem_limit_bytes=VMEM_LIMIT),
    )(*ws, *gs, *ms, *vs)
    return [outs[3 * i:3 * i + 3] for i in range(n)]


LANES = 128
SUBLANES = 8
SHARDED_AXIS = {"meta_tokens": 1, "convb_dw_w": 2, "convb_pw_w": 1, "rg_conv_w": 2}


def _rows_of(size):
    return -(-size // (LANES * SUBLANES)) * SUBLANES


def _as_rows(a, rows=None):
    flat = a.reshape(-1)
    rows = _rows_of(flat.size) if rows is None else rows
    return jnp.pad(flat, (0, rows * LANES - flat.size)).reshape(rows, LANES)


class _GradientSchedule:
    GROUPS = {"l1": [(1, "w_down"), (1, "w_up"), (1, "w_out"), (1, "w_in")], "a0": [(0, "w_down"), (0, "w_up")],
              "b0": [(0, "w_out")], "c0": [(0, "w_in")]}
    PLAN = {
        (1, "dw_in"): [("l1", "start")],
        (0, "mlp_bwd"): [("l1", "to_chips")],
        (0, "dw_up"): [("l1", "to_sibling"), ("a0", "start")],
        (0, "dw_out"): [("l1", "finish"), ("a0", "to_chips"), ("b0", "start")],
        (0, "mixer_bwd"): [("a0", "to_sibling"), ("b0", "to_chips"), ("small", "step")],
        (0, "dw_in"): [("c0", "start"), ("small", "step"), ("a0", "finish"), ("b0", "to_sibling"), ("c0", "to_chips")],
    }

    def __init__(self, w, mom, var, c1, me1):
        self.w, self.mom, self.var, self.c1, self.me1 = w, mom, var, c1, me1
        self.grads, self.chains, self.out = {}, {}, {}

    def grad(self, layer, name, g):
        self.grads[layer, name] = g

    def small(self, gsmall):
        self.small_sum = _AllReduceSmall([g.reshape(1, -1) if g.ndim == 1 else g for g in (gsmall[k] for k in SMALL)])
        return (self.small_sum.start(),)

    def point(self, layer, kernel_name, after):
        return self.run(self.PLAN.get((layer, kernel_name), ()), after) or (after,)

    def run(self, actions, after):
        deps = []
        for tag, stage in actions:
            if tag == "small":
                deps.append(self.small_sum.step(after))
            elif stage == "start":
                self.chains[tag] = _ReduceScatter(tag, [self.grads[lk] for lk in self.GROUPS[tag]], self.c1, self.me1)
                deps.append(self.chains[tag].start())
            elif stage == "finish":
                for (layer, k), (own, sib) in zip(self.GROUPS[tag], self.chains[tag].finish(after)):
                    self.out[k], token = _adamw_big_layer(layer, self.w[k], self.mom[k], self.var[k], own, sib, self.c1,
                                                          self.out.get(k))
                    deps.append(token)
            else:
                deps.append(getattr(self.chains[tag], stage)(after))
            after = deps[-1]
        self.last = after
        return tuple(deps)


def _from_shard_major(name, sm):
    if name == "meta_tokens":
        return sm.transpose(1, 0, 2).reshape(N_META, -1)
    if name == "convb_pw_w":
        return sm.transpose(1, 0, 2, 3).reshape(2, -1, D_CONV)
    return sm.transpose(1, 2, 0, 3).reshape(sm.shape[1], sm.shape[2], -1)


def kernel(x, meta_tokens, mix_norm_g, w_in, pool_w, pool_scale, convb_dw_w, convb_dw_b, convb_ln_g, convb_ln_b, convb_pw_w, rg_conv_w, rg_conv_b, rg_w_a, rg_b_a, rg_w_x, rg_b_x, rg_lambda, w_out, mlp_norm_g, w_up, w_down, final_norm_g, loss_target, m_meta_tokens, m_mix_norm_g, m_w_in, m_pool_w, m_pool_scale, m_convb_dw_w, m_convb_dw_b, m_convb_ln_g, m_convb_ln_b, m_convb_pw_w, m_rg_conv_w, m_rg_conv_b, m_rg_w_a, m_rg_b_a, m_rg_w_x, m_rg_b_x, m_rg_lambda, m_w_out, m_mlp_norm_g, m_w_up, m_w_down, m_final_norm_g, v_meta_tokens, v_mix_norm_g, v_w_in, v_pool_w, v_pool_scale, v_convb_dw_w, v_convb_dw_b, v_convb_ln_g, v_convb_ln_b, v_convb_pw_w, v_rg_conv_w, v_rg_conv_b, v_rg_w_a, v_rg_b_a, v_rg_w_x, v_rg_b_x, v_rg_lambda, v_w_out, v_mlp_norm_g, v_w_up, v_w_down, v_final_norm_g):
    given = dict(locals())
    w = {k: given[k] for k in WEIGHTS}
    mom = {k: given["m_" + k] for k in WEIGHTS}
    var = {k: given["v_" + k] for k in WEIGHTS}
    xi, yi, ci = _place()
    me1 = (2 * xi + yi).astype(jnp.int32).reshape(1)
    c1 = ci.astype(jnp.int32).reshape(1)

    small_rows = [_rows_of(w[k].size) for k in SMALL_SHARDED]
    small_pack = jnp.concatenate([_as_rows(w[k]) for k in SMALL_SHARDED])
    transposed = lambda d: {**d, "w_in": d["w_in"].transpose(0, 2, 1)}
    wt, momt, vart = transposed(w), transposed(mom), transposed(var)
    shard = lambda l, k: wt[k][l].astype(BF16)
    order = [[(0, "w_in"), "small"], [(0, "w_out"), (0, "w_up")], [(0, "w_down")], [(1, "w_in")], [(1, "w_out"), (1, "w_up")],
             [(1, "w_down")]]
    state, token = _gather_start([[small_pack if lk == "small" else shard(*lk) for lk in g] for g in order], me1[0])
    landed = {}

    def fetch(l, k, after):
        gi = [i for i, g in enumerate(order) if (l, k) in g][0]
        if gi not in landed:
            landed[gi] = _gather_wait(state[gi], after, "gather_wait_%d" % gi)
        raw = landed[gi][order[gi].index((l, k))]
        if k == "w_in":
            return raw.reshape(D_IN, D_MODEL)
        return raw.reshape(D_MODEL, D_MODEL) if k == "w_out" else raw

    seq = x.shape[1]
    t_real = N_META + seq
    t_pad = -(-t_real // ROW_ALIGN) * ROW_ALIGN
    tail = jnp.zeros((t_pad - t_real, D_MODEL), F32)
    front = jnp.zeros((N_META, D_MODEL), F32)
    h = jnp.concatenate([front + token[0, 0], x[0], tail])
    tgt = jnp.concatenate([front, loss_target[0], tail])
    landed[0] = _gather_wait(state[0], h, "gather_wait_0")
    wfull = {k: (w[k] + token[0, 0] if k in ("pool_w", "rg_w_a", "rg_w_x") else w[k]) for k in WEIGHTS}
    off = 0
    for k, rows in zip(SMALL_SHARDED, small_rows):
        sm = landed[0][1][:, off:off + rows].reshape(N_CHIPS, -1)[:, :w[k].size].reshape((N_CHIPS,) + w[k].shape)
        wfull[k] = _from_shard_major(k, sm)
        off += rows
    h = lax.dynamic_update_slice(h, wfull["meta_tokens"], (0, 0))
    sched = _GradientSchedule(wt, momt, vart, c1, me1)
    loss, dh = _local_step(h, tgt, t_real, wfull, fetch, sched)
    grad_x = dh[N_META:t_real][None]

    names = SMALL
    two_d = lambda a: a.reshape(1, -1) if a.ndim == 1 else a
    sched.run([("c0", "to_sibling"), ("small", "step")], sched.last)
    summed = dict(zip(names, sched.small_sum.vs))
    for k in SMALL_SHARDED:
        ax = SHARDED_AXIS[k]
        summed[k] = lax.dynamic_slice_in_dim(summed[k], me1[0] * w[k].shape[ax], w[k].shape[ax], axis=ax)

    out = {}
    res = _adamw_small([two_d(w[k]) for k in names], [summed[k] for k in names], [two_d(mom[k]) for k in names],
                       [two_d(var[k]) for k in names])
    for k, (d, m2, v2) in zip(names, res):
        out[k] = tuple(o.reshape(w[k].shape) for o in (summed[k], d, m2, v2))
    sched.run([("b0", "finish"), ("c0", "finish")], res[0][0])
    out.update(sched.out)
    out["w_in"] = tuple(o.transpose(0, 2, 1) for o in out["w_in"])

    loss = lax.psum(loss, ("x", "y", "c"))
    return (loss, grad_x, *[out[k][0] for k in WEIGHTS], *[out[k][1] for k in WEIGHTS],
            *[out[k][2] for k in WEIGHTS], *[out[k][3] for k in WEIGHTS])
```

```python
import functools

import jax
import jax.numpy as jnp
from jax import lax
from jax.experimental import pallas as pl
from jax.experimental.pallas import tpu as pltpu

F32, BF16 = jnp.float32, jnp.bfloat16
MESH = pl.DeviceIdType.MESH
ANY = pl.BlockSpec(memory_space=pl.ANY)

D_MODEL = 1024
N_META = 16
D_POOL = 256
D_CONV = 256
D_RNN = 512
D_IN = D_POOL + 2 * D_CONV + 2 * D_RNN
D_FF = 4096
FF_CHUNK = 1024
POOL_GW = 64
CONV_K = 31
RG_CONV_K = 4
RG_HD = 64
RG_C = 8.0
EPS = 1e-6
ADAM_LR, ADAM_B1, ADAM_B2, ADAM_EPS, ADAM_WD, ADAM_STEP = 0.001, 0.9, 0.999, 1e-08, 0.01, 10

HALO = 32
ROW_ALIGN = 256
TM_MIX = 384
TM_MAT = 768
TM_MLP_BWD = 384
N_CHIPS = 4
VMEM_LIMIT = 56 * 1024 * 1024

BIG = ("w_in", "w_out", "w_up", "w_down")
SMALL_SHARDED = ("meta_tokens", "convb_dw_w", "convb_pw_w", "rg_conv_w")
SMALL_REPL = ("mix_norm_g", "pool_w", "pool_scale", "convb_dw_b", "convb_ln_g", "convb_ln_b", "rg_conv_b",
              "rg_w_a", "rg_b_a", "rg_w_x", "rg_b_x", "rg_lambda", "mlp_norm_g", "final_norm_g")
SMALL = SMALL_REPL + SMALL_SHARDED
WEIGHTS = ("meta_tokens", "mix_norm_g", "w_in", "pool_w", "pool_scale", "convb_dw_w", "convb_dw_b", "convb_ln_g",
           "convb_ln_b", "convb_pw_w", "rg_conv_w", "rg_conv_b", "rg_w_a", "rg_b_a", "rg_w_x", "rg_b_x",
           "rg_lambda", "w_out", "mlp_norm_g", "w_up", "w_down", "final_norm_g")


def _params(*sem):
    return pltpu.CompilerParams(dimension_semantics=sem, vmem_limit_bytes=VMEM_LIMIT)


def _row_tile(t, cap):
    best = None
    for tm in range(128, cap + 1, 128):
        if t % tm == 0:
            best = tm
    assert best is not None, (t, cap)
    return best


def _dot(a, b):
    return jnp.dot(a, b, preferred_element_type=F32)


def _dot_nt(a, b):
    return lax.dot_general(a, b, (((1,), (1,)), ((), ())), preferred_element_type=F32)


def _dot_tn(a, b):
    return lax.dot_general(a, b, (((0,), (0,)), ((), ())), preferred_element_type=F32)


def _rms(x):
    r = lax.rsqrt(jnp.mean(x * x, axis=-1, keepdims=True) + EPS)
    return r, x * r


def _rms_bwd(du, n, r, g):
    dn = du * g
    return r * (dn - n * jnp.mean(dn * n, axis=-1, keepdims=True))


def _sig(x):
    return jax.nn.sigmoid(x)


def _colsum(x):
    return jnp.sum(x, axis=0, keepdims=True)


def _one_minus_sq(a, log_a):
    x = 2.0 * log_a
    series = -x * (1.0 + x * (0.5 + x * (1.0 / 6)))
    return jnp.where(x > -0.01, series, 1.0 - a * a)


_GELU_K0 = 0.7978845608028654
_GELU_K1 = 0.044715


def _gelu_and_grad(x):
    th = jnp.tanh(_GELU_K0 * (x + _GELU_K1 * x * x * x))
    val = 0.5 * x * (1.0 + th)
    grad = 0.5 * (1.0 + th) + 0.5 * x * (1.0 - th * th) * _GELU_K0 * (1.0 + 3.0 * _GELU_K1 * x * x)
    return val, grad


def _full(a):
    nd = a.ndim
    return pl.BlockSpec(a.shape, lambda *_: (0,) * nd)


def _resident(a):
    nd = a.ndim
    return pl.BlockSpec(a.shape, lambda *_: (0,) * nd, pipeline_mode=pl.Buffered(1))


def _after(body, n_in, deps):
    def wrapped(*refs):
        return body(*refs[:n_in], *refs[n_in + len(deps):])
    return wrapped


def _lane_sel(lane, a2, a4, a8, a16):
    return jnp.where(lane < POOL_GW, a2, jnp.where(lane < 2 * POOL_GW, a4, jnp.where(lane < 3 * POOL_GW, a8, a16)))


def _window_sums_back(src, tmp_a, tmp_b, tm):
    n = HALO + tm
    rows = lambda ref, lo, back: ref[pl.ds(lo - back, n - lo), :]
    tmp_a[pl.ds(8, n - 8), :] = rows(src, 8, 0) + rows(src, 8, 1)
    tmp_b[pl.ds(16, n - 16), :] = rows(tmp_a, 16, 0) + rows(tmp_a, 16, 2)
    s2 = rows(tmp_a, HALO, 0)
    tmp_a[pl.ds(24, n - 24), :] = rows(tmp_b, 24, 0) + rows(tmp_b, 24, 4)
    s8 = rows(tmp_a, HALO, 0)
    return s2, rows(tmp_b, HALO, 0), s8, s8 + rows(tmp_a, HALO, 8)


def _window_sums_ahead(src, tmp_a, tmp_b, tm):
    rows = lambda ref, n, ahead: ref[pl.ds(ahead, n), :]
    tmp_a[pl.ds(0, tm + 24), :] = rows(src, tm + 24, 0) + rows(src, tm + 24, 1)
    tmp_b[pl.ds(0, tm + 16), :] = rows(tmp_a, tm + 16, 0) + rows(tmp_a, tm + 16, 2)
    s2 = rows(tmp_a, tm, 0)
    tmp_a[pl.ds(0, tm + 8), :] = rows(tmp_b, tm + 8, 0) + rows(tmp_b, tm + 8, 4)
    s8 = rows(tmp_a, tm, 0)
    return s2, rows(tmp_b, tm, 0), s8, s8 + rows(tmp_a, tm, 8)


def _pool_counts(tm, t0):
    lane = lax.broadcasted_iota(jnp.int32, (tm, D_POOL), 1)
    row = lax.broadcasted_iota(jnp.int32, (tm, D_POOL), 0) + t0
    cnt = jnp.minimum(row + 1, _lane_sel(lane, 2, 4, 8, 16)).astype(F32)
    return lane, cnt


def _pool_fwd(ext_q, tmp_a, tmp_b, tm, t0):
    lane, cnt = _pool_counts(tm, t0)
    q = ext_q[pl.ds(HALO, tm), :]
    pooled = _lane_sel(lane, *_window_sums_back(ext_q, tmp_a, tmp_b, tm)) / cnt - q
    return pooled, lane, cnt


def _taps(src, w_of, offs, tm, zbuf):
    acc = None
    for r in range(8):
        ks = [k for k in range(len(offs)) if offs[k] % 8 == r]
        if not ks:
            continue
        rows = tm + (8 if r else 0)
        z = w_of(ks[0]) * src[pl.ds(offs[ks[0]] - r, rows), :]
        for k in ks[1:]:
            z = z + w_of(k) * src[pl.ds(offs[k] - r, rows), :]
        if r:
            zbuf[...] = z
            z = zbuf[pl.ds(r, tm), :]
        acc = z if acc is None else acc + z
    return acc


def _tap_grads(d_pad, src, offs, tm, g_ref, zbuf):
    ch = src.shape[-1]
    for r in range(8):
        ks = [k for k in range(len(offs)) if offs[k] % 8 == r]
        if not ks:
            continue
        rows = tm + (8 if r else 0)
        if r:
            zbuf[...] = d_pad[pl.ds(8 - r, rows), :]
        for k in ks:
            d = zbuf[...] if r else d_pad[pl.ds(8, rows), :]
            prod = d * src[pl.ds(offs[k] - r, rows), :]
            g_ref[k] += jnp.sum(prod.reshape(rows // 8, 8, ch), axis=0)


_CONV_OFFS = [HALO - (CONV_K - 1) + k for k in range(CONV_K)]


def _conv_fwd(ext_u, dww_ref, dwb, tm, zbuf):
    return dwb + _taps(ext_u, lambda k: dww_ref[k:k + 1, :], _CONV_OFFS, tm, zbuf)


def _ln_silu(c, lng, lnb):
    mu = jnp.mean(c, axis=-1, keepdims=True)
    cc = c - mu
    rstd = lax.rsqrt(jnp.mean(cc * cc, axis=-1, keepdims=True) + EPS)
    z = cc * rstd
    l = z * lng + lnb
    sl = _sig(l)
    return z, rstd, l, sl, l * sl


def _rg_conv(ext_x, cw_ref, cb, tm):
    xc = cb + cw_ref[0:1, :] * ext_x[pl.ds(HALO - (RG_CONV_K - 1), tm), :]
    for k in range(1, RG_CONV_K):
        xc = xc + cw_ref[k:k + 1, :] * ext_x[pl.ds(HALO - (RG_CONV_K - 1) + k, tm), :]
    return xc


def _softplus_neg(lam):
    return jnp.maximum(-lam, 0.0) + jnp.log(1.0 + jnp.exp(-jnp.abs(lam)))


def _rg_gates(xc, wa, ba, wx, bx, lam):
    xcb = xc.astype(BF16)
    r = _sig(_dot(xcb, wa) + ba)
    ig = _sig(_dot(xcb, wx) + bx)
    log_a = (-RG_C * r) * _softplus_neg(lam)
    a = jnp.exp(log_a)
    return r, ig, a, jnp.sqrt(_one_minus_sq(a, log_a))


def _scan_rows(a_ref, b_ref, out_ref, carry, tm, reverse):
    rows = lax.broadcasted_iota(jnp.int32, (8, D_RNN), 0)
    ngrp = tm // 8

    def grp(gi, hb):
        st = pl.multiple_of((ngrp - 1 - gi if reverse else gi) * 8, 8)
        a8 = a_ref[pl.ds(st, 8), :]
        b8 = b_ref[pl.ds(st, 8), :]
        out = jnp.zeros((8, D_RNN), F32)
        for j in (range(7, -1, -1) if reverse else range(8)):
            aj = jnp.broadcast_to(a8[j:j + 1, :], (8, D_RNN))
            bj = jnp.broadcast_to(b8[j:j + 1, :], (8, D_RNN))
            if reverse:
                cur = bj + hb
                hb = aj * cur
            else:
                cur = aj * hb + bj
                hb = cur
            out = jnp.where(rows == j, cur, out)
        out_ref[pl.ds(st, 8), :] = out
        return hb

    carry[...] = lax.fori_loop(0, ngrp, grp, carry[...])


_MIX_W = ("wp", "psc", "dww", "dwb", "lng", "lnb", "wpw", "cw", "cb", "wa", "ba", "wx", "bx", "lam")


def _mixer_fwd(h, g, w_in, mw):
    t = h.shape[0]
    tm = _row_tile(t, TM_MIX)

    def body(h_ref, g_ref, win_ref, wp, psc, dww, dwb, lng, lnb, wpw, cw, cb, wa, ba, wx, bx, lam,
             y_ref, p_ref, u_ref, hs_ref, conv_ref, xc_ref, gates_ref, ext_q, ext_u, ext_x, tmp_a, tmp_b, zbuf, a_s, b_s, hcar):
        i = pl.program_id(0)

        @pl.when(i == 0)
        def _():
            ext_q[0:HALO, :] = jnp.zeros((HALO, D_POOL), F32)
            ext_u[0:HALO, :] = jnp.zeros((HALO, D_CONV), F32)
            ext_x[0:HALO, :] = jnp.zeros((HALO, D_RNN), F32)
            hcar[...] = jnp.zeros((8, D_RNN), F32)

        u = (_rms(h_ref[...])[1] * g_ref[...]).astype(BF16)
        u_ref[...] = u
        p_ref[...] = _dot_nt(u, win_ref[...])

        ext_q[pl.ds(HALO, tm), :] = p_ref[:, 0:256]
        pooled, _, _ = _pool_fwd(ext_q, tmp_a, tmp_b, tm, i * tm)
        y_ref[:, 0:256] = (_dot(pooled.astype(BF16), wp[...]) * psc[...]).astype(BF16)

        ext_u[pl.ds(HALO, tm), :] = p_ref[:, 256:512] * _sig(p_ref[:, 512:768])
        conv = _conv_fwd(ext_u, dww, dwb[...], tm, zbuf)
        conv_ref[...] = conv
        act = _ln_silu(conv, lng[...], lnb[...])[4]
        y_ref[:, 256:512] = _dot(act.astype(BF16), wpw[...]).astype(BF16)

        ext_x[pl.ds(HALO, tm), :] = p_ref[:, 1280:1792]
        xc = _rg_conv(ext_x, cw, cb[...], tm)
        xc_ref[...] = xc
        r, ig, a, m = _rg_gates(xc, wa[...], ba[...], wx[...], bx[...], lam[...])
        for j, gate in enumerate((r, ig, a, m)):
            gates_ref[:, j * D_RNN:(j + 1) * D_RNN] = gate
        a_s[...] = a
        b_s[...] = m * (ig * xc)
        _scan_rows(a_s, b_s, hs_ref, hcar, tm, reverse=False)
        y_ref[:, 512:1024] = (_gelu_and_grad(p_ref[:, 768:1280])[0] * hs_ref[...]).astype(BF16)

        ext_q[0:HALO, :] = ext_q[pl.ds(tm, HALO), :]
        ext_u[0:HALO, :] = ext_u[pl.ds(tm, HALO), :]
        ext_x[0:HALO, :] = ext_x[pl.ds(tm, HALO), :]

    ws = [mw[k] for k in _MIX_W]
    row = lambda w: pl.BlockSpec((tm, w), lambda i: (i, 0))
    return pl.pallas_call(
        body, name="mixer_fwd", grid=(t // tm,),
        in_specs=[row(D_MODEL), _full(g), _resident(w_in)] + [_full(w) for w in ws],
        out_specs=[row(D_MODEL), row(D_IN), row(D_MODEL), row(D_RNN), row(D_CONV), row(D_RNN), row(4 * D_RNN)],
        out_shape=[jax.ShapeDtypeStruct((t, D_MODEL), BF16), jax.ShapeDtypeStruct((t, D_IN), F32),
                   jax.ShapeDtypeStruct((t, D_MODEL), BF16), jax.ShapeDtypeStruct((t, D_RNN), F32),
                   jax.ShapeDtypeStruct((t, D_CONV), F32), jax.ShapeDtypeStruct((t, D_RNN), F32),
                   jax.ShapeDtypeStruct((t, 4 * D_RNN), F32)],
        scratch_shapes=[pltpu.VMEM((HALO + tm, D_POOL), F32), pltpu.VMEM((HALO + tm, D_CONV), F32),
                        pltpu.VMEM((HALO + tm, D_RNN), F32), pltpu.VMEM((HALO + tm, D_POOL), F32),
                        pltpu.VMEM((HALO + tm, D_POOL), F32), pltpu.VMEM((tm + 8, D_CONV), F32),
                        pltpu.VMEM((tm, D_RNN), F32), pltpu.VMEM((tm, D_RNN), F32), pltpu.VMEM((8, D_RNN), F32)],
        compiler_params=_params("arbitrary"),
    )(h, g, w_in, *ws)


_MIX_G = (("wp", (D_POOL, D_POOL)), ("psc", (1, D_POOL)), ("dww", (32, 8, D_CONV)), ("dwb", (1, D_CONV)),
          ("lng", (1, D_CONV)), ("lnb", (1, D_CONV)), ("wpw", (D_CONV, D_CONV)), ("cw", (8, D_RNN)),
          ("cb", (1, D_RNN)), ("wa", (D_RNN, D_RNN)), ("ba", (1, D_RNN)), ("wx", (D_RNN, D_RNN)),
          ("bx", (1, D_RNN)), ("lam", (1, D_RNN)), ("g1", (1, D_MODEL)))


def _mixer_bwd(p, dh1, hs, conv, xc, gates, h0, g1, w_out, w_in, mw, deps=()):
    t = p.shape[0]
    tm = _row_tile(t, TM_MIX)
    nt = t // tm
    hb = tm // HALO

    def body(p_ref, ph_ref, dh1_ref, hs_ref, hsh_ref, conv_ref, xc_ref, gates_ref, h0_ref, g1_ref, wout_ref, win_ref,
             wp, psc, dww, dwb, lng, lnb, wpw, cw, cb, wa, ba, wx, bx, lam,
             dp_ref, dh0_ref, g_wp, g_psc, g_dww, g_dwb, g_lng, g_lnb, g_wpw, g_cw, g_cb, g_wa, g_ba, g_wx, g_bx, g_lam, g_g1,
             ext_q, ext_u, ext_x, ext_h, ee, dc_s, dx_s, tmp_a, tmp_b, zbuf, d_pad, a_s, b_s, g_s, gcar, dy_ref, dp_s):
        step = pl.program_id(0)
        i = nt - 1 - step
        grads = (g_wp, g_psc, g_dww, g_dwb, g_lng, g_lnb, g_wpw, g_cw, g_cb, g_wa, g_ba, g_wx, g_bx, g_lam, g_g1)
        dy_ref[...] = dh1_ref[...].astype(BF16)
        dy_cols = lambda lo, hi: _dot_nt(dy_ref[...], wout_ref[lo:hi, :])

        @pl.when(step == 0)
        def _():
            for gr in grads:
                gr[...] = jnp.zeros(gr.shape, F32)
            ee[pl.ds(tm, HALO), :] = jnp.zeros((HALO, D_POOL), F32)
            dc_s[pl.ds(tm, HALO), :] = jnp.zeros((HALO, D_CONV), F32)
            dx_s[pl.ds(tm, HALO), :] = jnp.zeros((HALO, D_RNN), F32)
            d_pad[0:8, :] = jnp.zeros((8, D_CONV), F32)
            d_pad[pl.ds(tm + 8, 8), :] = jnp.zeros((8, D_CONV), F32)
            gcar[...] = jnp.zeros((8, D_RNN), F32)

        hm = jnp.where(i == 0, 0.0, 1.0)

        ext_q[0:HALO, :] = ph_ref[:, 0:256] * hm
        ext_q[pl.ds(HALO, tm), :] = p_ref[:, 0:256]
        pooled, lane, cnt = _pool_fwd(ext_q, tmp_a, tmp_b, tm, i * tm)
        pooled_b = pooled.astype(BF16)
        dya = dy_cols(0, 256)
        g_psc[...] += _colsum(dya * _dot(pooled_b, wp[...]))
        dmixed_b = (dya * psc[...]).astype(BF16)
        dpooled = _dot_nt(dmixed_b, wp[...])
        g_wp[...] += _dot_tn(pooled_b, dmixed_b)
        ee[0:tm, :] = dpooled / cnt
        dp_s[:, 0:256] = _lane_sel(lane, *_window_sums_ahead(ee, tmp_a, tmp_b, tm)) - dpooled
        ee[pl.ds(tm, HALO), :] = ee[0:HALO, :]

        v = p_ref[:, 256:512]
        s = _sig(p_ref[:, 512:768])
        ext_u[0:HALO, :] = ph_ref[:, 256:512] * _sig(ph_ref[:, 512:768]) * hm
        ext_u[pl.ds(HALO, tm), :] = v * s
        z, rstd, l, sl, act = _ln_silu(conv_ref[...], lng[...], lnb[...])
        dyb_b = dy_cols(256, 512).astype(BF16)
        dact = _dot_nt(dyb_b, wpw[...])
        g_wpw[...] += _dot_tn(act.astype(BF16), dyb_b)
        dl = dact * (sl * (1.0 + l * (1.0 - sl)))
        g_lng[...] += _colsum(dl * z)
        g_lnb[...] += _colsum(dl)
        dz = dl * lng[...]
        dc = rstd * (dz - jnp.mean(dz, axis=-1, keepdims=True) - z * jnp.mean(dz * z, axis=-1, keepdims=True))
        g_dwb[...] += _colsum(dc)
        dc_s[0:tm, :] = dc
        d_pad[pl.ds(8, tm), :] = dc
        _tap_grads(d_pad, ext_u, _CONV_OFFS, tm, g_dww, zbuf)
        du0 = _taps(dc_s, lambda j: dww[CONV_K - 1 - j:CONV_K - j, :], list(range(CONV_K)), tm, zbuf)
        dp_s[:, 256:512] = du0 * s
        dp_s[:, 512:768] = du0 * v * (s * (1.0 - s))
        dc_s[pl.ds(tm, HALO), :] = dc_s[0:HALO, :]

        ext_x[0:HALO, :] = ph_ref[:, 1280:1792] * hm
        ext_x[pl.ds(HALO, tm), :] = p_ref[:, 1280:1792]
        xc = xc_ref[...]
        xcb = xc.astype(BF16)
        r, ig, a, m = (gates_ref[:, j * D_RNN:(j + 1) * D_RNN] for j in range(4))
        sp = _softplus_neg(lam[...])
        ext_h[0:HALO, :] = hsh_ref[...] * hm
        ext_h[pl.ds(HALO, tm), :] = hs_ref[...]
        dyc = dy_cols(512, 1024)
        gl, dgl = _gelu_and_grad(p_ref[:, 768:1280])
        dp_s[:, 768:1280] = dyc * hs_ref[...] * dgl
        a_s[...] = a
        b_s[...] = dyc * gl
        _scan_rows(a_s, b_s, g_s, gcar, tm, reverse=True)
        g = g_s[...]
        da = g * ext_h[pl.ds(HALO - 1, tm), :]
        dm = g * (ig * xc)
        dig = g * (m * xc)
        dlog_a = da * a - dm * (a * a) / m
        g_lam[...] += _colsum(dlog_a * (-RG_C * r)) * (-_sig(-lam[...]))
        dra = (dlog_a * (-RG_C * sp)) * (r * (1.0 - r))
        dia = dig * (ig * (1.0 - ig))
        g_ba[...] += _colsum(dra)
        g_bx[...] += _colsum(dia)
        dra_b = dra.astype(BF16)
        dia_b = dia.astype(BF16)
        dxc = g * (m * ig) + _dot_nt(dra_b, wa[...]) + _dot_nt(dia_b, wx[...])
        g_wa[...] += _dot_tn(xcb, dra_b)
        g_wx[...] += _dot_tn(xcb, dia_b)
        g_cb[...] += _colsum(dxc)
        dx_s[0:tm, :] = dxc
        for k in range(RG_CONV_K):
            g_cw[k:k + 1, :] += _colsum(dxc * ext_x[pl.ds(HALO - (RG_CONV_K - 1) + k, tm), :])
        dxin = cw[RG_CONV_K - 1:RG_CONV_K, :] * dxc
        for j in range(1, RG_CONV_K):
            dxin = dxin + cw[RG_CONV_K - 1 - j:RG_CONV_K - j, :] * dx_s[pl.ds(j, tm), :]
        dp_s[:, 1280:1792] = dxin
        dx_s[pl.ds(tm, HALO), :] = dx_s[0:HALO, :]

        dpb = dp_s[...].astype(BF16)
        dp_ref[...] = dpb
        du = _dot(dpb, win_ref[...])
        r, n = _rms(h0_ref[...])
        g_g1[...] += _colsum(du * n)
        dh0_ref[...] = dh1_ref[...] + _rms_bwd(du, n, r, g1_ref[...])

    ws = [mw[k] for k in _MIX_W]
    tile = lambda w: pl.BlockSpec((tm, w), lambda s: (nt - 1 - s, 0))
    halo = lambda w: pl.BlockSpec((HALO, w), lambda s: (jnp.maximum((nt - 1 - s) * hb - 1, 0), 0))
    outs = pl.pallas_call(
        _after(body, 12 + len(ws), deps), name="mixer_bwd", grid=(nt,),
        in_specs=[tile(D_IN), halo(D_IN), tile(D_MODEL), tile(D_RNN), halo(D_RNN), tile(D_CONV), tile(D_RNN), tile(4 * D_RNN),
                  tile(D_MODEL), _full(g1),
                  _resident(w_out), _resident(w_in)] + [_full(w) for w in ws] + [ANY] * len(deps),
        out_specs=[tile(D_IN), tile(D_MODEL)] + [pl.BlockSpec(shp, lambda s, nd=len(shp): (0,) * nd) for _, shp in _MIX_G],
        out_shape=[jax.ShapeDtypeStruct((t, D_IN), BF16), jax.ShapeDtypeStruct((t, D_MODEL), F32)]
        + [jax.ShapeDtypeStruct(shp, F32) for _, shp in _MIX_G],
        scratch_shapes=[pltpu.VMEM((HALO + tm, D_POOL), F32), pltpu.VMEM((HALO + tm, D_CONV), F32),
                        pltpu.VMEM((HALO + tm, D_RNN), F32), pltpu.VMEM((HALO + tm, D_RNN), F32),
                        pltpu.VMEM((tm + HALO, D_POOL), F32), pltpu.VMEM((tm + HALO, D_CONV), F32),
                        pltpu.VMEM((tm + HALO, D_RNN), F32), pltpu.VMEM((HALO + tm, D_POOL), F32),
                        pltpu.VMEM((HALO + tm, D_POOL), F32), pltpu.VMEM((tm + 8, D_CONV), F32),
                        pltpu.VMEM((tm + 16, D_CONV), F32), pltpu.VMEM((tm, D_RNN), F32),
                        pltpu.VMEM((tm, D_RNN), F32), pltpu.VMEM((tm, D_RNN), F32), pltpu.VMEM((8, D_RNN), F32),
                        pltpu.VMEM((tm, D_MODEL), BF16), pltpu.VMEM((tm, D_IN), F32)],
        compiler_params=_params("arbitrary"),
    )(p, p, dh1, hs, hs, conv, xc, gates, h0, g1, w_out, w_in, *ws, *deps)
    return outs[0], outs[1], {k: o for (k, _), o in zip(_MIX_G, outs[2:])}


def _mid_fwd(y, h0, w_out, g, w_up):
    t = h0.shape[0]
    tm = _row_tile(t, TM_MAT)

    def body(y_ref, h0_ref, wo_ref, g_ref, wu_ref, h1_ref, u2_ref, f_ref):
        h1 = h0_ref[...] + _dot(y_ref[...], wo_ref[...])
        h1_ref[...] = h1
        u2 = (_rms(h1)[1] * g_ref[...]).astype(BF16)
        u2_ref[...] = u2
        for c in range(D_FF // FF_CHUNK):
            f_ref[:, c * FF_CHUNK:(c + 1) * FF_CHUNK] = _dot(u2, wu_ref[c]).astype(BF16)

    row = lambda w: pl.BlockSpec((tm, w), lambda i: (i, 0))
    return pl.pallas_call(
        body, name="mid_fwd", grid=(t // tm,),
        in_specs=[row(D_MODEL), row(D_MODEL), _resident(w_out), _full(g), _resident(w_up)],
        out_specs=[row(D_MODEL), row(D_MODEL), row(D_FF)],
        out_shape=[jax.ShapeDtypeStruct((t, D_MODEL), F32), jax.ShapeDtypeStruct((t, D_MODEL), BF16),
                   jax.ShapeDtypeStruct((t, D_FF), BF16)],
        compiler_params=_params("parallel"),
    )(y, h0, w_out, g, w_up)


def _down_proj(f_ref, h1_ref, wd_ref):
    acc = h1_ref[...]
    for c in range(D_FF // FF_CHUNK):
        cols = slice(c * FF_CHUNK, (c + 1) * FF_CHUNK)
        a = jnp.square(jnp.maximum(f_ref[:, cols].astype(F32), 0.0)).astype(BF16)
        acc = acc + _dot(a, wd_ref[cols, :])
    return acc


def _down_fwd(f, h1, w_down):
    t = h1.shape[0]
    tm = _row_tile(t, TM_MAT)

    def body(f_ref, h1_ref, wd_ref, h2_ref):
        h2_ref[...] = _down_proj(f_ref, h1_ref, wd_ref)

    row = lambda w: pl.BlockSpec((tm, w), lambda i: (i, 0))
    return pl.pallas_call(
        body, name="down_fwd", grid=(t // tm,),
        in_specs=[row(D_FF), row(D_MODEL), _resident(w_down)], out_specs=row(D_MODEL),
        out_shape=jax.ShapeDtypeStruct((t, D_MODEL), F32),
        compiler_params=_params("parallel"),
    )(f, h1, w_down)


def _down_fwd_loss(f, h1, w_down, g, tgt, t_real):
    t = h1.shape[0]
    tm = _row_tile(t, TM_MAT)

    def body(f_ref, h1_ref, wd_ref, g_ref, tgt_ref, loss_ref, dh_ref, dg_ref):
        i = pl.program_id(0)

        @pl.when(i == 0)
        def _():
            loss_ref[...] = jnp.zeros(loss_ref.shape, F32)
            dg_ref[...] = jnp.zeros(dg_ref.shape, F32)

        r, n = _rms(_down_proj(f_ref, h1_ref, wd_ref))
        row = lax.broadcasted_iota(jnp.int32, (tm, 1), 0) + i * tm
        valid = jnp.logical_and(row >= N_META, row < t_real)
        diff = jnp.where(valid, n * g_ref[...] - tgt_ref[...], 0.0)
        loss_ref[...] += 0.5 * jnp.sum(jnp.mean(diff * diff, axis=-1, keepdims=True))
        dy = diff * (1.0 / D_MODEL)
        dg_ref[...] += _colsum(dy * n)
        dh_ref[...] = _rms_bwd(dy, n, r, g_ref[...])

    row = lambda w: pl.BlockSpec((tm, w), lambda i: (i, 0))
    return pl.pallas_call(
        body, name="down_fwd_loss", grid=(t // tm,),
        in_specs=[row(D_FF), row(D_MODEL), _resident(w_down), _full(g), row(D_MODEL)],
        out_specs=[pl.BlockSpec((8, 128), lambda i: (0, 0)), row(D_MODEL), pl.BlockSpec((1, D_MODEL), lambda i: (0, 0))],
        out_shape=[jax.ShapeDtypeStruct((8, 128), F32), jax.ShapeDtypeStruct((t, D_MODEL), F32),
                   jax.ShapeDtypeStruct((1, D_MODEL), F32)],
        compiler_params=_params("arbitrary"),
    )(f, h1, w_down, g, tgt)


def _mlp_bwd(dh2, f, h1, g, w_up, w_down, deps=()):
    t = dh2.shape[0]
    tm = _row_tile(t, TM_MLP_BWD)

    def body(dh2_ref, f_ref, wd_ref, wu_ref, h1_ref, g_ref, df_ref, dh1_ref, dg_ref):
        @pl.when(pl.program_id(0) == 0)
        def _():
            dg_ref[...] = jnp.zeros(dg_ref.shape, F32)

        dh2 = dh2_ref[...]
        dhb = dh2.astype(BF16)
        du2 = None
        for c in range(D_FF // FF_CHUNK):
            cols = slice(c * FF_CHUNK, (c + 1) * FF_CHUNK)
            dact = _dot_nt(dhb, wd_ref[c])
            df = (dact * (2.0 * jnp.maximum(f_ref[:, cols].astype(F32), 0.0))).astype(BF16)
            df_ref[:, cols] = df
            part = _dot_nt(df, wu_ref[c])
            du2 = part if du2 is None else du2 + part
        r, n = _rms(h1_ref[...])
        dg_ref[...] += _colsum(du2 * n)
        dh1_ref[...] = dh2 + _rms_bwd(du2, n, r, g_ref[...])

    row = lambda w: pl.BlockSpec((tm, w), lambda i: (i, 0))
    return pl.pallas_call(
        _after(body, 6, deps), name="mlp_bwd", grid=(t // tm,),
        in_specs=[row(D_MODEL), row(D_FF), _resident(w_down), _resident(w_up), row(D_MODEL), _full(g)] + [ANY] * len(deps),
        out_specs=[row(D_FF), row(D_MODEL), pl.BlockSpec((1, D_MODEL), lambda i: (0, 0))],
        out_shape=[jax.ShapeDtypeStruct((t, D_FF), BF16), jax.ShapeDtypeStruct((t, D_MODEL), F32),
                   jax.ShapeDtypeStruct((1, D_MODEL), F32)],
        compiler_params=_params("arbitrary"),
    )(dh2, f, w_down, w_up, h1, g, *deps)


def _tn_matmul(a, b, kc, nc, relu2, name, deps=()):
    t, k = a.shape
    n = b.shape[1]
    tt = _row_tile(t, TM_MAT)
    gk, gn = k // kc, n // nc

    def body(a_ref, b_ref, o_ref):
        @pl.when(pl.program_id(2) == 0)
        def _():
            o_ref[...] = jnp.zeros(o_ref.shape, F32)

        av = a_ref[...]
        if relu2:
            av = jnp.square(jnp.maximum(av.astype(F32), 0.0))
        o_ref[...] += _dot_tn(av.astype(BF16), b_ref[...].astype(BF16))

    return pl.pallas_call(
        _after(body, 2, deps), name=name, grid=(gk, gn, t // tt),
        in_specs=[pl.BlockSpec((tt, kc), lambda ik, jn, it: (it, ik)), pl.BlockSpec((tt, nc), lambda ik, jn, it: (it, jn))]
        + [ANY] * len(deps),
        out_specs=pl.BlockSpec((None, kc, nc), lambda ik, jn, it: (ik * gn + jn, 0, 0)),
        out_shape=jax.ShapeDtypeStruct((gk * gn, kc, nc), F32),
        compiler_params=_params("parallel", "parallel", "arbitrary"),
    )(a, b, *deps)


def _block_diag(blocks):
    nb, hd, _ = blocks.shape
    eye = jnp.eye(nb, dtype=blocks.dtype)
    return (blocks[:, :, None, :] * eye[:, None, :, None]).reshape(nb * hd, nb * hd)


def _diag_blocks(m, nb):
    hd = m.shape[0] // nb
    eye = jnp.eye(nb, dtype=m.dtype)
    return jnp.sum(m.reshape(nb, hd, nb, hd) * eye[:, None, :, None], axis=2)


def _mixer_weights(w, l):
    row = lambda a: a.reshape(1, -1)
    return dict(
        wp=_block_diag(w["pool_w"][l]).astype(BF16), psc=row(w["pool_scale"][l]),
        dww=jnp.pad(w["convb_dw_w"][l], ((0, 32 - CONV_K), (0, 0))), dwb=row(w["convb_dw_b"][l]),
        lng=row(w["convb_ln_g"][l]), lnb=row(w["convb_ln_b"][l]), wpw=w["convb_pw_w"][l].astype(BF16),
        cw=jnp.pad(w["rg_conv_w"][l], ((0, 8 - RG_CONV_K), (0, 0))), cb=row(w["rg_conv_b"][l]),
        wa=_block_diag(w["rg_w_a"][l]).astype(BF16), ba=row(w["rg_b_a"][l]),
        wx=_block_diag(w["rg_w_x"][l]).astype(BF16), bx=row(w["rg_b_x"][l]), lam=row(w["rg_lambda"][l]))


def _local_step(h, tgt, t_real, w, fetch, hooks):
    depth = 2
    saved = []
    big = []
    for l in range(depth):
        mw = _mixer_weights(w, l)
        g1 = w["mix_norm_g"][l].reshape(1, -1)
        g2 = w["mlp_norm_g"][l].reshape(1, -1)
        wl = dict(w_in=fetch(l, "w_in", h))
        y, p, u, hs, conv, xc, gates = _mixer_fwd(h, g1, wl["w_in"], mw)
        wl["w_out"], wl["w_up"] = fetch(l, "w_out", y), fetch(l, "w_up", y)
        h1, u2, f = _mid_fwd(y, h, wl["w_out"], g2, wl["w_up"])
        wl["w_down"] = fetch(l, "w_down", f)
        if l == depth - 1:
            loss, dh, dgf = _down_fwd_loss(f, h1, wl["w_down"].reshape(D_FF, D_MODEL), w["final_norm_g"].reshape(1, -1), tgt,
                                           t_real)
            h2 = None
        else:
            h2 = _down_fwd(f, h1, wl["w_down"].reshape(D_FF, D_MODEL))
        saved.append(dict(mw=mw, g1=g1, g2=g2, h0=h, p=p, u=u, y=y, hs=hs, conv=conv, xc=xc, gates=gates, h1=h1, u2=u2, f=f))
        big.append(wl)
        h = h2

    gs = {k: [None] * depth for k in ("mix_norm_g", "mlp_norm_g", "pool_w", "pool_scale", "convb_dw_w", "convb_dw_b",
                                      "convb_ln_g", "convb_ln_b", "convb_pw_w", "rg_conv_w", "rg_conv_b", "rg_w_a",
                                      "rg_b_a", "rg_w_x", "rg_b_x", "rg_lambda")}
    deps = ()
    for l in reversed(range(depth)):
        s, wl = saved[l], big[l]
        df, dh1, dg2 = _mlp_bwd(dh, s["f"], s["h1"], s["g2"], wl["w_up"], wl["w_down"], deps)
        deps = hooks.point(l, "mlp_bwd", dh1)
        g_down = _tn_matmul(s["f"], dh, FF_CHUNK, D_MODEL, True, "dw_down", deps)
        hooks.grad(l, "w_down", g_down)
        deps = hooks.point(l, "dw_down", g_down)
        g_up = _tn_matmul(s["u2"], df, D_MODEL, FF_CHUNK, False, "dw_up", deps)
        hooks.grad(l, "w_up", g_up)
        deps = hooks.point(l, "dw_up", g_up)
        g_out = _tn_matmul(s["y"], dh1, D_MODEL, D_MODEL, False, "dw_out", deps)
        hooks.grad(l, "w_out", g_out.reshape(N_CHIPS, D_MODEL // N_CHIPS, D_MODEL))
        deps = hooks.point(l, "dw_out", g_out)
        dp, dh, mg = _mixer_bwd(s["p"], dh1, s["hs"], s["conv"], s["xc"], s["gates"], s["h0"], s["g1"], wl["w_out"],
                                wl["w_in"], s["mw"], deps)
        gs["mix_norm_g"][l] = mg["g1"][0]
        gs["mlp_norm_g"][l] = dg2[0]
        gs["pool_w"][l] = _diag_blocks(mg["wp"], D_POOL // POOL_GW)
        gs["pool_scale"][l] = mg["psc"][0]
        gs["convb_dw_w"][l] = jnp.sum(mg["dww"][:CONV_K], axis=1)
        gs["convb_dw_b"][l] = mg["dwb"][0]
        gs["convb_ln_g"][l] = mg["lng"][0]
        gs["convb_ln_b"][l] = mg["lnb"][0]
        gs["convb_pw_w"][l] = mg["wpw"]
        gs["rg_conv_w"][l] = mg["cw"][:RG_CONV_K]
        gs["rg_conv_b"][l] = mg["cb"][0]
        gs["rg_w_a"][l] = _diag_blocks(mg["wa"], D_RNN // RG_HD)
        gs["rg_b_a"][l] = mg["ba"][0]
        gs["rg_w_x"][l] = _diag_blocks(mg["wx"], D_RNN // RG_HD)
        gs["rg_b_x"][l] = mg["bx"][0]
        gs["rg_lambda"][l] = mg["lam"][0]
        if l == 0:
            gsmall = {k: jnp.stack(v) for k, v in gs.items()}
            gsmall["final_norm_g"] = dgf[0]
            gsmall["meta_tokens"] = dh[:N_META]
            started = hooks.small(gsmall)
        deps = hooks.point(l, "mixer_bwd", started[0] if l == 0 and started else dh)
        g_in = _tn_matmul(dp, s["u"], D_IN, D_MODEL, False, "dw_in", deps).reshape(N_CHIPS, D_IN // N_CHIPS, D_MODEL)
        hooks.grad(l, "w_in", g_in)
        deps = hooks.point(l, "dw_in", g_in)
    return loss[0, 0], dh


def _place():
    return lax.axis_index("x"), lax.axis_index("y"), lax.axis_index("c")


def _other_chips(x, y):
    return [(1 - x, y), (x, 1 - y), (1 - x, 1 - y)]


HBM_SPEC = pl.BlockSpec(memory_space=pltpu.HBM)
SEM_SPEC = pl.BlockSpec(memory_space=pltpu.SEMAPHORE)
DATAFLOW = pltpu.SideEffectType.DATAFLOW_SIDE_EFFECTING


def _gather_copies(src_refs, land_refs, send_sem, recv_sem, first):
    x, y, c = _place()
    me = 2 * x + y
    out = []
    for n in range(len(src_refs)):
        for j, (px, py) in enumerate(_other_chips(x, y)):
            out.append(pltpu.make_async_remote_copy(src_refs[n], land_refs[n].at[me], send_sem.at[first + 3 * n + j],
                                                    recv_sem.at[first + 3 * n + j], device_id=(px, py, c), device_id_type=MESH))
    return out


def _gather_start(groups, me):
    srcs = [pltpu.with_memory_space_constraint(s, pltpu.HBM) for g in groups for s in g]
    lands = [pltpu.with_memory_space_constraint(
        lax.dynamic_update_slice(jnp.zeros((N_CHIPS,) + s.shape, s.dtype), s[None], (me,) + (0,) * s.ndim), pltpu.HBM)
        for g in groups for s in g]
    n, ng = len(srcs), len(groups)
    first = [sum(len(g) for g in groups[:i]) for i in range(ng)]

    def body(*refs):
        src_refs, land_refs = refs[:n], refs[n:2 * n]
        sems = refs[2 * n:2 * n + 2 * ng]
        token = refs[-1]
        for gi, g in enumerate(groups):
            lo, hi = first[gi], first[gi] + len(g)
            for cp in _gather_copies(src_refs[lo:hi], land_refs[lo:hi], sems[2 * gi], sems[2 * gi + 1], 0):
                cp.start()
        token[...] = jnp.zeros(token.shape, token.dtype)

    sem_shapes = [pltpu.SemaphoreType.DMA((3 * len(g),)) for g in groups for _ in range(2)]
    outs = pl.pallas_call(
        body, name="gather_start",
        out_shape=sem_shapes + [pltpu.HBM(a.shape, a.dtype) for a in srcs + lands] + [jax.ShapeDtypeStruct((8, 128), F32)],
        in_specs=[HBM_SPEC] * (2 * n),
        out_specs=[SEM_SPEC] * (2 * ng) + [HBM_SPEC] * (2 * n) + [pl.BlockSpec(memory_space=pltpu.VMEM)],
        input_output_aliases={i: 2 * ng + i for i in range(2 * n)},
        compiler_params=pltpu.CompilerParams(has_side_effects=DATAFLOW),
    )(*srcs, *lands)
    sems, thru, token = outs[:2 * ng], outs[2 * ng:2 * ng + 2 * n], outs[-1]
    state = []
    for gi, g in enumerate(groups):
        lo, hi = first[gi], first[gi] + len(g)
        state.append((sems[2 * gi], sems[2 * gi + 1], thru[lo:hi], thru[n + lo:n + hi]))
    return state, token


def _gather_wait(state, after, name):
    send_sem, recv_sem, srcs, lands = state
    n = len(srcs)

    def body(*refs):
        src_refs, land_refs = refs[:n], refs[n:2 * n]
        send, recv = refs[2 * n], refs[2 * n + 1]
        for cp in _gather_copies(src_refs, land_refs, send, recv, 0):
            cp.wait_send()
            cp.wait_recv()

    outs = pl.pallas_call(
        body, name=name,
        out_shape=[pltpu.HBM(a.shape, a.dtype) for a in list(srcs) + list(lands)],
        in_specs=[HBM_SPEC] * (2 * n) + [SEM_SPEC, SEM_SPEC, ANY],
        out_specs=[HBM_SPEC] * (2 * n),
        input_output_aliases={i: i for i in range(2 * n)},
        compiler_params=pltpu.CompilerParams(has_side_effects=DATAFLOW),
    )(*srcs, *lands, send_sem, recv_sem, after)
    return outs[n:]


def _add_halves(g, recv, c1):
    nk, r, cd = g.shape
    r2 = r // 2

    def body(c_ref, g_ref, r_ref, pab_ref):
        pab_ref[...] = (g_ref[...] + r_ref[...]).astype(BF16)

    blk = pl.BlockSpec((None, r2, cd), lambda k, c_ref: (k, 0, 0))
    return pl.pallas_call(
        body, name="rs_add_halves",
        grid_spec=pltpu.PrefetchScalarGridSpec(
            num_scalar_prefetch=1, grid=(nk,),
            in_specs=[pl.BlockSpec((None, r2, cd), lambda k, c_ref: (k, c_ref[0], 0)), blk], out_specs=blk),
        out_shape=jax.ShapeDtypeStruct((nk, r2, cd), BF16),
        compiler_params=_params("parallel"),
    )(c1, g, recv)


def _sum_partials(g, recv_sibling, recv_chips, c_me):
    nk, r, cd = g.shape
    r2 = r // 2

    def body(cm_ref, g_ref, a_ref, r_ref, s_ref):
        own = g_ref[...] + a_ref[...]
        s_ref[...] = ((own + r_ref[0].astype(F32)) + r_ref[1].astype(F32)) + r_ref[2].astype(F32)

    return pl.pallas_call(
        body, name="rs_sum_partials",
        grid_spec=pltpu.PrefetchScalarGridSpec(
            num_scalar_prefetch=1, grid=(1,),
            in_specs=[pl.BlockSpec((None, r2, cd), lambda i, cm: (cm[1], cm[0], 0)),
                      pl.BlockSpec((None, r2, cd), lambda i, cm: (cm[1], 0, 0)),
                      pl.BlockSpec((3, r2, cd), lambda i, cm: (0, 0, 0))],
            out_specs=pl.BlockSpec((r2, cd), lambda i, cm: (0, 0))),
        out_shape=jax.ShapeDtypeStruct((r2, cd), F32),
        compiler_params=_params("arbitrary"),
    )(c_me, g, recv_sibling, recv_chips)


def _split_start(name, srcs, lands, ncopies, make_copies):
    srcs = [pltpu.with_memory_space_constraint(s, pltpu.HBM) for s in srcs]
    lands = [pltpu.with_memory_space_constraint(a, pltpu.HBM) for a in lands]
    n, m = len(srcs), len(lands)

    def body(*refs):
        src_refs, land_refs = refs[:n], refs[n:n + m]
        send, recv, token = refs[n + m], refs[n + m + 1], refs[-1]
        for cp in make_copies(src_refs, land_refs, send, recv):
            cp.start()
        token[...] = jnp.zeros(token.shape, token.dtype)

    outs = pl.pallas_call(
        body, name=name,
        out_shape=[pltpu.SemaphoreType.DMA((ncopies,)), pltpu.SemaphoreType.DMA((ncopies,))]
        + [pltpu.HBM(a.shape, a.dtype) for a in srcs + lands] + [jax.ShapeDtypeStruct((8, 128), F32)],
        in_specs=[HBM_SPEC] * (n + m),
        out_specs=[SEM_SPEC, SEM_SPEC] + [HBM_SPEC] * (n + m) + [pl.BlockSpec(memory_space=pltpu.VMEM)],
        input_output_aliases={i: 2 + i for i in range(n + m)},
        compiler_params=pltpu.CompilerParams(has_side_effects=DATAFLOW),
    )(*srcs, *lands)
    return (outs[0], outs[1], outs[2:2 + n], outs[2 + n:2 + n + m], make_copies), outs[-1]


def _split_wait(name, state, after):
    send_sem, recv_sem, srcs, lands, make_copies = state
    n, m = len(srcs), len(lands)

    def body(*refs):
        src_refs, land_refs = refs[:n], refs[n:n + m]
        for cp in make_copies(src_refs, land_refs, refs[n + m], refs[n + m + 1]):
            cp.wait_send()
            cp.wait_recv()

    outs = pl.pallas_call(
        body, name=name,
        out_shape=[pltpu.HBM(a.shape, a.dtype) for a in list(srcs) + list(lands)],
        in_specs=[HBM_SPEC] * (n + m) + [SEM_SPEC, SEM_SPEC, ANY],
        out_specs=[HBM_SPEC] * (n + m),
        input_output_aliases={i: i for i in range(n + m)},
        compiler_params=pltpu.CompilerParams(has_side_effects=DATAFLOW),
    )(*srcs, *lands, send_sem, recv_sem, after)
    return outs[:n], outs[n:]


def _copies_to_sibling(src_of):
    def make(src_refs, land_refs, send, recv):
        x, y, c = _place()
        return [pltpu.make_async_remote_copy(src_of(src_refs[i], c), land_refs[i], send.at[i], recv.at[i],
                                             device_id=(x, y, 1 - c), device_id_type=MESH) for i in range(len(src_refs))]
    return make


def _copies_to_chips(src_refs, land_refs, send, recv):
    x, y, c = _place()
    return [pltpu.make_async_remote_copy(src_refs[i].at[2 * px + py], land_refs[i].at[j], send.at[3 * i + j], recv.at[3 * i + j],
                                         device_id=(px, py, c), device_id_type=MESH)
            for i in range(len(src_refs)) for j, (px, py) in enumerate(_other_chips(x, y))]


def _other_half_rows(ref, c):
    r2 = ref.shape[1] // 2
    return ref.at[:, pl.ds(pl.multiple_of((1 - c) * r2, 8), r2)]


class _ReduceScatter:
    def __init__(self, tag, grads, c1, me1):
        self.tag, self.grads, self.c1, self.me1 = tag, grads, c1, me1

    def start(self):
        lands = [lax.empty((g.shape[0], g.shape[1] // 2, g.shape[2]), F32) for g in self.grads]
        self.state, token = _split_start("rs_%s_a_start" % self.tag, self.grads, lands, len(self.grads),
                                         _copies_to_sibling(_other_half_rows))
        return token

    def to_chips(self, after):
        self.halves = _split_wait("rs_%s_a_wait" % self.tag, self.state, after)
        pabs = [_add_halves(g, r, self.c1) for g, r in zip(*self.halves)]
        lands = [lax.empty((3,) + p.shape[1:], BF16) for p in pabs]
        self.state, token = _split_start("rs_%s_b_start" % self.tag, pabs, lands, 3 * len(pabs), _copies_to_chips)
        return token

    def to_sibling(self, after):
        _, recv = _split_wait("rs_%s_b_wait" % self.tag, self.state, after)
        c_me = jnp.concatenate([self.c1, self.me1])
        sums = [_sum_partials(g, ra, rb, c_me) for g, ra, rb in zip(*self.halves, recv)]
        lands = [lax.empty(s.shape, F32) for s in sums]
        self.state, token = _split_start("rs_%s_c_start" % self.tag, sums, lands, len(sums),
                                         _copies_to_sibling(lambda ref, c: ref))
        return token

    def finish(self, after):
        return list(zip(*_split_wait("rs_%s_c_wait" % self.tag, self.state, after)))


def _add_lists(a_list, b_list):
    n = len(a_list)

    def body(*refs):
        for i in range(n):
            refs[2 * n + i][...] = refs[i][...] + refs[n + i][...]

    vm = pl.BlockSpec(memory_space=pltpu.VMEM)
    return pl.pallas_call(
        body, name="add_lists", in_specs=[vm] * (2 * n), out_specs=[vm] * n,
        out_shape=[jax.ShapeDtypeStruct(a.shape, a.dtype) for a in a_list],
        compiler_params=pltpu.CompilerParams(vmem_limit_bytes=VMEM_LIMIT),
    )(*a_list, *b_list)


def _copies_to_peer(stage):
    def make(src_refs, land_refs, send, recv):
        x, y, c = _place()
        peer = [(x, y, 1 - c), (1 - x, y, c), (x, 1 - y, c)][stage]
        return [pltpu.make_async_remote_copy(src_refs[i], land_refs[i], send.at[i], recv.at[i], device_id=peer, device_id_type=MESH)
                for i in range(len(src_refs))]
    return make


class _AllReduceSmall:
    def __init__(self, vs):
        self.vs, self.stage = list(vs), 0

    def _start(self):
        lands = [lax.empty(v.shape, v.dtype) for v in self.vs]
        self.state, token = _split_start("ar_small_start_%d" % self.stage, self.vs, lands, len(self.vs), _copies_to_peer(self.stage))
        return token

    def start(self):
        return self._start()

    def step(self, after):
        mine, theirs = _split_wait("ar_small_wait_%d" % self.stage, self.state, after)
        self.vs = _add_lists(mine, theirs)
        self.stage += 1
        return self._start() if self.stage < 3 else self.vs[0]


def _adamw_math(w, g, m, v):
    m = ADAM_B1 * m + (1.0 - ADAM_B1) * g
    v = ADAM_B2 * v + (1.0 - ADAM_B2) * jnp.square(g)
    m_hat = m / (1.0 - ADAM_B1 ** ADAM_STEP)
    v_hat = v / (1.0 - ADAM_B2 ** ADAM_STEP)
    return -ADAM_LR * (m_hat / (jnp.sqrt(v_hat) + ADAM_EPS) + ADAM_WD * w), m, v


def _adamw_big_layer(layer, w, m, v, own, sib, c1, prev):
    _, r, cd = w.shape
    r2 = r // 2

    def body(c_ref, w_ref, m_ref, v_ref, own_ref, sib_ref, *rest):
        g_ref, d_ref, mo_ref, vo_ref, token = rest[-5:]
        g = jnp.where(pl.program_id(0) == c_ref[0], own_ref[...], sib_ref[...])
        g_ref[...] = g
        d_ref[...], mo_ref[...], vo_ref[...] = _adamw_math(w_ref[...], g, m_ref[...], v_ref[...])
        token[...] = jnp.zeros(token.shape, F32)

    blk = pl.BlockSpec((None, r2, cd), lambda hh, c_ref: (layer, hh, 0))
    half = pl.BlockSpec((r2, cd), lambda hh, c_ref: (0, 0))
    prev = () if prev is None else tuple(prev)
    outs = pl.pallas_call(
        body, name="adamw_big",
        grid_spec=pltpu.PrefetchScalarGridSpec(
            num_scalar_prefetch=1, grid=(2,), in_specs=[blk, blk, blk, half, half] + [ANY] * len(prev),
            out_specs=[blk] * 4 + [pl.BlockSpec((8, 128), lambda hh, c_ref: (0, 0))]),
        out_shape=[jax.ShapeDtypeStruct(w.shape, F32)] * 4 + [jax.ShapeDtypeStruct((8, 128), F32)],
        input_output_aliases={6 + i: i for i in range(len(prev))},
        compiler_params=_params("arbitrary"),
    )(c1, w, m, v, own, sib, *prev)
    return outs[:4], outs[4]


def _adamw_small(ws, gs, ms, vs):
    n = len(ws)

    def body(*refs):
        w_refs, g_refs, m_refs, v_refs = refs[:n], refs[n:2 * n], refs[2 * n:3 * n], refs[3 * n:4 * n]
        outs = refs[4 * n:]
        for i in range(n):
            outs[3 * i][...], outs[3 * i + 1][...], outs[3 * i + 2][...] = _adamw_math(
                w_refs[i][...], g_refs[i][...], m_refs[i][...], v_refs[i][...])

    vm = pl.BlockSpec(memory_space=pltpu.VMEM)
    outs = pl.pallas_call(
        body, name="adamw_small", in_specs=[vm] * (4 * n), out_specs=[vm] * (3 * n),
        out_shape=[jax.ShapeDtypeStruct(w.shape, F32) for w in ws for _ in range(3)],
        compiler_params=pltpu.CompilerParams(vmem_limit_bytes=VMEM_LIMIT),
    )(*ws, *gs, *ms, *vs)
    return [outs[3 * i:3 * i + 3] for i in range(n)]


LANES = 128
SUBLANES = 8
SHARDED_AXIS = {"meta_tokens": 1, "convb_dw_w": 2, "convb_pw_w": 1, "rg_conv_w": 2}


def _rows_of(size):
    return -(-size // (LANES * SUBLANES)) * SUBLANES


def _as_rows(a, rows=None):
    flat = a.reshape(-1)
    rows = _rows_of(flat.size) if rows is None else rows
    return jnp.pad(flat, (0, rows * LANES - flat.size)).reshape(rows, LANES)


class _GradientSchedule:
    GROUPS = {"l1": [(1, "w_down"), (1, "w_up"), (1, "w_out"), (1, "w_in")], "a0": [(0, "w_down"), (0, "w_up")],
              "b0": [(0, "w_out")], "c0": [(0, "w_in")]}
    PLAN = {
        (1, "dw_in"): [("l1", "start")],
        (0, "mlp_bwd"): [("l1", "to_chips")],
        (0, "dw_up"): [("l1", "to_sibling"), ("a0", "start")],
        (0, "dw_out"): [("l1", "finish"), ("a0", "to_chips"), ("b0", "start")],
        (0, "mixer_bwd"): [("a0", "to_sibling"), ("b0", "to_chips"), ("small", "step")],
        (0, "dw_in"): [("c0", "start"), ("small", "step"), ("c0", "to_chips"), ("a0", "finish"), ("b0", "to_sibling")],
    }

    def __init__(self, w, mom, var, c1, me1):
        self.w, self.mom, self.var, self.c1, self.me1 = w, mom, var, c1, me1
        self.grads, self.chains, self.out = {}, {}, {}

    def grad(self, layer, name, g):
        self.grads[layer, name] = g

    def small(self, gsmall):
        self.small_sum = _AllReduceSmall([g.reshape(1, -1) if g.ndim == 1 else g for g in (gsmall[k] for k in SMALL)])
        return (self.small_sum.start(),)

    def point(self, layer, kernel_name, after):
        return self.run(self.PLAN.get((layer, kernel_name), ()), after) or (after,)

    def run(self, actions, after):
        deps = []
        for tag, stage in actions:
            if tag == "small":
                deps.append(self.small_sum.step(after))
            elif stage == "start":
                self.chains[tag] = _ReduceScatter(tag, [self.grads[lk] for lk in self.GROUPS[tag]], self.c1, self.me1)
                deps.append(self.chains[tag].start())
            elif stage == "finish":
                for (layer, k), (own, sib) in zip(self.GROUPS[tag], self.chains[tag].finish(after)):
                    self.out[k], token = _adamw_big_layer(layer, self.w[k], self.mom[k], self.var[k], own, sib, self.c1,
                                                          self.out.get(k))
                    deps.append(token)
            else:
                deps.append(getattr(self.chains[tag], stage)(after))
            after = deps[-1]
        self.last = after
        return tuple(deps)


def _from_shard_major(name, sm):
    if name == "meta_tokens":
        return sm.transpose(1, 0, 2).reshape(N_META, -1)
    if name == "convb_pw_w":
        return sm.transpose(1, 0, 2, 3).reshape(2, -1, D_CONV)
    return sm.transpose(1, 2, 0, 3).reshape(sm.shape[1], sm.shape[2], -1)


def kernel(x, meta_tokens, mix_norm_g, w_in, pool_w, pool_scale, convb_dw_w, convb_dw_b, convb_ln_g, convb_ln_b, convb_pw_w, rg_conv_w, rg_conv_b, rg_w_a, rg_b_a, rg_w_x, rg_b_x, rg_lambda, w_out, mlp_norm_g, w_up, w_down, final_norm_g, loss_target, m_meta_tokens, m_mix_norm_g, m_w_in, m_pool_w, m_pool_scale, m_convb_dw_w, m_convb_dw_b, m_convb_ln_g, m_convb_ln_b, m_convb_pw_w, m_rg_conv_w, m_rg_conv_b, m_rg_w_a, m_rg_b_a, m_rg_w_x, m_rg_b_x, m_rg_lambda, m_w_out, m_mlp_norm_g, m_w_up, m_w_down, m_final_norm_g, v_meta_tokens, v_mix_norm_g, v_w_in, v_pool_w, v_pool_scale, v_convb_dw_w, v_convb_dw_b, v_convb_ln_g, v_convb_ln_b, v_convb_pw_w, v_rg_conv_w, v_rg_conv_b, v_rg_w_a, v_rg_b_a, v_rg_w_x, v_rg_b_x, v_rg_lambda, v_w_out, v_mlp_norm_g, v_w_up, v_w_down, v_final_norm_g):
    given = dict(locals())
    w = {k: given[k] for k in WEIGHTS}
    mom = {k: given["m_" + k] for k in WEIGHTS}
    var = {k: given["v_" + k] for k in WEIGHTS}
    xi, yi, ci = _place()
    me1 = (2 * xi + yi).astype(jnp.int32).reshape(1)
    c1 = ci.astype(jnp.int32).reshape(1)

    small_rows = [_rows_of(w[k].size) for k in SMALL_SHARDED]
    small_pack = jnp.concatenate([_as_rows(w[k]) for k in SMALL_SHARDED])
    transposed = lambda d: {**d, "w_in": d["w_in"].transpose(0, 2, 1)}
    wt, momt, vart = transposed(w), transposed(mom), transposed(var)
    shard = lambda l, k: wt[k][l].astype(BF16)
    order = [[(0, "w_in"), "small"], [(0, "w_out"), (0, "w_up")], [(0, "w_down")], [(1, "w_in")], [(1, "w_out"), (1, "w_up")],
             [(1, "w_down")]]
    state, token = _gather_start([[small_pack if lk == "small" else shard(*lk) for lk in g] for g in order], me1[0])
    landed = {}

    def fetch(l, k, after):
        gi = [i for i, g in enumerate(order) if (l, k) in g][0]
        if gi not in landed:
            landed[gi] = _gather_wait(state[gi], after, "gather_wait_%d" % gi)
        raw = landed[gi][order[gi].index((l, k))]
        if k == "w_in":
            return raw.reshape(D_IN, D_MODEL)
        return raw.reshape(D_MODEL, D_MODEL) if k == "w_out" else raw

    seq = x.shape[1]
    t_real = N_META + seq
    t_pad = -(-t_real // ROW_ALIGN) * ROW_ALIGN
    tail = jnp.zeros((t_pad - t_real, D_MODEL), F32)
    front = jnp.zeros((N_META, D_MODEL), F32)
    h = jnp.concatenate([front + token[0, 0], x[0], tail])
    tgt = jnp.concatenate([front, loss_target[0], tail])
    landed[0] = _gather_wait(state[0], h, "gather_wait_0")
    wfull = {k: (w[k] + token[0, 0] if k in ("pool_w", "rg_w_a", "rg_w_x") else w[k]) for k in WEIGHTS}
    off = 0
    for k, rows in zip(SMALL_SHARDED, small_rows):
        sm = landed[0][1][:, off:off + rows].reshape(N_CHIPS, -1)[:, :w[k].size].reshape((N_CHIPS,) + w[k].shape)
        wfull[k] = _from_shard_major(k, sm)
        off += rows
    h = lax.dynamic_update_slice(h, wfull["meta_tokens"], (0, 0))
    sched = _GradientSchedule(wt, momt, vart, c1, me1)
    loss, dh = _local_step(h, tgt, t_real, wfull, fetch, sched)
    grad_x = dh[N_META:t_real][None]

    names = SMALL
    two_d = lambda a: a.reshape(1, -1) if a.ndim == 1 else a
    sched.run([("small", "step")], sched.last)
    summed = dict(zip(names, sched.small_sum.vs))
    for k in SMALL_SHARDED:
        ax = SHARDED_AXIS[k]
        summed[k] = lax.dynamic_slice_in_dim(summed[k], me1[0] * w[k].shape[ax], w[k].shape[ax], axis=ax)

    out = {}
    res = _adamw_small([two_d(w[k]) for k in names], [summed[k] for k in names], [two_d(mom[k]) for k in names],
                       [two_d(var[k]) for k in names])
    for k, (d, m2, v2) in zip(names, res):
        out[k] = tuple(o.reshape(w[k].shape) for o in (summed[k], d, m2, v2))
    sched.run([("b0", "finish"), ("c0", "to_sibling"), ("c0", "finish")], res[0][0])
    out.update(sched.out)
    out["w_in"] = tuple(o.transpose(0, 2, 1) for o in out["w_in"])

    loss = lax.psum(loss, ("x", "y", "c"))
    return (loss, grad_x, *[out[k][0] for k in WEIGHTS], *[out[k][1] for k in WEIGHTS],
            *[out[k][2] for k in WEIGHTS], *[out[k][3] for k in WEIGHTS])
```

```python
import functools

import jax
import jax.numpy as jnp
from jax import lax
from jax.experimental import pallas as pl
from jax.experimental.pallas import tpu as pltpu

F32, BF16 = jnp.float32, jnp.bfloat16
MESH = pl.DeviceIdType.MESH
ANY = pl.BlockSpec(memory_space=pl.ANY)

D_MODEL = 1024
N_META = 16
D_POOL = 256
D_CONV = 256
D_RNN = 512
D_IN = D_POOL + 2 * D_CONV + 2 * D_RNN
D_FF = 4096
FF_CHUNK = 1024
POOL_GW = 64
CONV_K = 31
RG_CONV_K = 4
RG_HD = 64
RG_C = 8.0
EPS = 1e-6
ADAM_LR, ADAM_B1, ADAM_B2, ADAM_EPS, ADAM_WD, ADAM_STEP = 0.001, 0.9, 0.999, 1e-08, 0.01, 10

HALO = 32
ROW_ALIGN = 256
TM_MIX = 384
TM_MAT = 768
TM_MLP_BWD = 384
N_CHIPS = 4
ROW_CHUNKS = 4
VMEM_LIMIT = 56 * 1024 * 1024

BIG = ("w_in", "w_out", "w_up", "w_down")
SMALL_SHARDED = ("meta_tokens", "convb_dw_w", "convb_pw_w", "rg_conv_w")
SMALL_REPL = ("mix_norm_g", "pool_w", "pool_scale", "convb_dw_b", "convb_ln_g", "convb_ln_b", "rg_conv_b",
              "rg_w_a", "rg_b_a", "rg_w_x", "rg_b_x", "rg_lambda", "mlp_norm_g", "final_norm_g")
SMALL = SMALL_REPL + SMALL_SHARDED
WEIGHTS = ("meta_tokens", "mix_norm_g", "w_in", "pool_w", "pool_scale", "convb_dw_w", "convb_dw_b", "convb_ln_g",
           "convb_ln_b", "convb_pw_w", "rg_conv_w", "rg_conv_b", "rg_w_a", "rg_b_a", "rg_w_x", "rg_b_x",
           "rg_lambda", "w_out", "mlp_norm_g", "w_up", "w_down", "final_norm_g")


def _params(*sem):
    return pltpu.CompilerParams(dimension_semantics=sem, vmem_limit_bytes=VMEM_LIMIT)


def _row_tile(t, cap):
    best = None
    for tm in range(128, cap + 1, 128):
        if t % tm == 0:
            best = tm
    assert best is not None, (t, cap)
    return best


def _dot(a, b):
    return jnp.dot(a, b, preferred_element_type=F32)


def _dot_nt(a, b):
    return lax.dot_general(a, b, (((1,), (1,)), ((), ())), preferred_element_type=F32)


def _dot_tn(a, b):
    return lax.dot_general(a, b, (((0,), (0,)), ((), ())), preferred_element_type=F32)


def _rms(x):
    r = lax.rsqrt(jnp.mean(x * x, axis=-1, keepdims=True) + EPS)
    return r, x * r


def _rms_bwd(du, n, r, g):
    dn = du * g
    return r * (dn - n * jnp.mean(dn * n, axis=-1, keepdims=True))


def _sig(x):
    return jax.nn.sigmoid(x)


def _colsum(x):
    return jnp.sum(x, axis=0, keepdims=True)


def _one_minus_sq(a, log_a):
    x = 2.0 * log_a
    series = -x * (1.0 + x * (0.5 + x * (1.0 / 6)))
    return jnp.where(x > -0.01, series, 1.0 - a * a)


_GELU_K0 = 0.7978845608028654
_GELU_K1 = 0.044715


def _gelu_and_grad(x):
    th = jnp.tanh(_GELU_K0 * (x + _GELU_K1 * x * x * x))
    val = 0.5 * x * (1.0 + th)
    grad = 0.5 * (1.0 + th) + 0.5 * x * (1.0 - th * th) * _GELU_K0 * (1.0 + 3.0 * _GELU_K1 * x * x)
    return val, grad


def _full(a):
    nd = a.ndim
    return pl.BlockSpec(a.shape, lambda *_: (0,) * nd)


def _resident(a):
    nd = a.ndim
    return pl.BlockSpec(a.shape, lambda *_: (0,) * nd, pipeline_mode=pl.Buffered(1))


def _after(body, n_in, deps):
    def wrapped(*refs):
        return body(*refs[:n_in], *refs[n_in + len(deps):])
    return wrapped


def _lane_sel(lane, a2, a4, a8, a16):
    return jnp.where(lane < POOL_GW, a2, jnp.where(lane < 2 * POOL_GW, a4, jnp.where(lane < 3 * POOL_GW, a8, a16)))


def _window_sums_back(src, tmp_a, tmp_b, tm):
    n = HALO + tm
    rows = lambda ref, lo, back: ref[pl.ds(lo - back, n - lo), :]
    tmp_a[pl.ds(8, n - 8), :] = rows(src, 8, 0) + rows(src, 8, 1)
    tmp_b[pl.ds(16, n - 16), :] = rows(tmp_a, 16, 0) + rows(tmp_a, 16, 2)
    s2 = rows(tmp_a, HALO, 0)
    tmp_a[pl.ds(24, n - 24), :] = rows(tmp_b, 24, 0) + rows(tmp_b, 24, 4)
    s8 = rows(tmp_a, HALO, 0)
    return s2, rows(tmp_b, HALO, 0), s8, s8 + rows(tmp_a, HALO, 8)


def _window_sums_ahead(src, tmp_a, tmp_b, tm):
    rows = lambda ref, n, ahead: ref[pl.ds(ahead, n), :]
    tmp_a[pl.ds(0, tm + 24), :] = rows(src, tm + 24, 0) + rows(src, tm + 24, 1)
    tmp_b[pl.ds(0, tm + 16), :] = rows(tmp_a, tm + 16, 0) + rows(tmp_a, tm + 16, 2)
    s2 = rows(tmp_a, tm, 0)
    tmp_a[pl.ds(0, tm + 8), :] = rows(tmp_b, tm + 8, 0) + rows(tmp_b, tm + 8, 4)
    s8 = rows(tmp_a, tm, 0)
    return s2, rows(tmp_b, tm, 0), s8, s8 + rows(tmp_a, tm, 8)


def _pool_counts(tm, t0):
    lane = lax.broadcasted_iota(jnp.int32, (tm, D_POOL), 1)
    row = lax.broadcasted_iota(jnp.int32, (tm, D_POOL), 0) + t0
    cnt = jnp.minimum(row + 1, _lane_sel(lane, 2, 4, 8, 16)).astype(F32)
    return lane, cnt


def _pool_fwd(ext_q, tmp_a, tmp_b, tm, t0):
    lane, cnt = _pool_counts(tm, t0)
    q = ext_q[pl.ds(HALO, tm), :]
    pooled = _lane_sel(lane, *_window_sums_back(ext_q, tmp_a, tmp_b, tm)) / cnt - q
    return pooled, lane, cnt


def _taps(src, w_of, offs, tm, zbuf):
    acc = None
    for r in range(8):
        ks = [k for k in range(len(offs)) if offs[k] % 8 == r]
        if not ks:
            continue
        rows = tm + (8 if r else 0)
        z = w_of(ks[0]) * src[pl.ds(offs[ks[0]] - r, rows), :]
        for k in ks[1:]:
            z = z + w_of(k) * src[pl.ds(offs[k] - r, rows), :]
        if r:
            zbuf[...] = z
            z = zbuf[pl.ds(r, tm), :]
        acc = z if acc is None else acc + z
    return acc


def _tap_grads(d_pad, src, offs, tm, g_ref, zbuf):
    ch = src.shape[-1]
    for r in range(8):
        ks = [k for k in range(len(offs)) if offs[k] % 8 == r]
        if not ks:
            continue
        rows = tm + (8 if r else 0)
        if r:
            zbuf[...] = d_pad[pl.ds(8 - r, rows), :]
        for k in ks:
            d = zbuf[...] if r else d_pad[pl.ds(8, rows), :]
            prod = d * src[pl.ds(offs[k] - r, rows), :]
            g_ref[k] += jnp.sum(prod.reshape(rows // 8, 8, ch), axis=0)


_CONV_OFFS = [HALO - (CONV_K - 1) + k for k in range(CONV_K)]


def _conv_fwd(ext_u, dww_ref, dwb, tm, zbuf):
    return dwb + _taps(ext_u, lambda k: dww_ref[k:k + 1, :], _CONV_OFFS, tm, zbuf)


def _ln_silu(c, lng, lnb):
    mu = jnp.mean(c, axis=-1, keepdims=True)
    cc = c - mu
    rstd = lax.rsqrt(jnp.mean(cc * cc, axis=-1, keepdims=True) + EPS)
    z = cc * rstd
    l = z * lng + lnb
    sl = _sig(l)
    return z, rstd, l, sl, l * sl


def _rg_conv(ext_x, cw_ref, cb, tm):
    xc = cb + cw_ref[0:1, :] * ext_x[pl.ds(HALO - (RG_CONV_K - 1), tm), :]
    for k in range(1, RG_CONV_K):
        xc = xc + cw_ref[k:k + 1, :] * ext_x[pl.ds(HALO - (RG_CONV_K - 1) + k, tm), :]
    return xc


def _softplus_neg(lam):
    return jnp.maximum(-lam, 0.0) + jnp.log(1.0 + jnp.exp(-jnp.abs(lam)))


def _rg_gates(xc, wa, ba, wx, bx, lam):
    xcb = xc.astype(BF16)
    r = _sig(_dot(xcb, wa) + ba)
    ig = _sig(_dot(xcb, wx) + bx)
    log_a = (-RG_C * r) * _softplus_neg(lam)
    a = jnp.exp(log_a)
    return r, ig, a, jnp.sqrt(_one_minus_sq(a, log_a))


def _scan_rows(a_ref, b_ref, out_ref, carry, tm, reverse):
    rows = lax.broadcasted_iota(jnp.int32, (8, D_RNN), 0)
    ngrp = tm // 8

    def grp(gi, hb):
        st = pl.multiple_of((ngrp - 1 - gi if reverse else gi) * 8, 8)
        a8 = a_ref[pl.ds(st, 8), :]
        b8 = b_ref[pl.ds(st, 8), :]
        out = jnp.zeros((8, D_RNN), F32)
        for j in (range(7, -1, -1) if reverse else range(8)):
            aj = jnp.broadcast_to(a8[j:j + 1, :], (8, D_RNN))
            bj = jnp.broadcast_to(b8[j:j + 1, :], (8, D_RNN))
            if reverse:
                cur = bj + hb
                hb = aj * cur
            else:
                cur = aj * hb + bj
                hb = cur
            out = jnp.where(rows == j, cur, out)
        out_ref[pl.ds(st, 8), :] = out
        return hb

    carry[...] = lax.fori_loop(0, ngrp, grp, carry[...])


_MIX_W = ("wp", "psc", "dww", "dwb", "lng", "lnb", "wpw", "cw", "cb", "wa", "ba", "wx", "bx", "lam")


def _mixer_fwd(h, g, w_in, mw):
    t = h.shape[0]
    tm = _row_tile(t, TM_MIX)

    def body(h_ref, g_ref, win_ref, wp, psc, dww, dwb, lng, lnb, wpw, cw, cb, wa, ba, wx, bx, lam,
             y_ref, p_ref, u_ref, hs_ref, conv_ref, xc_ref, gates_ref, ext_q, ext_u, ext_x, tmp_a, tmp_b, zbuf, a_s, b_s, hcar):
        i = pl.program_id(0)

        @pl.when(i == 0)
        def _():
            ext_q[0:HALO, :] = jnp.zeros((HALO, D_POOL), F32)
            ext_u[0:HALO, :] = jnp.zeros((HALO, D_CONV), F32)
            ext_x[0:HALO, :] = jnp.zeros((HALO, D_RNN), F32)
            hcar[...] = jnp.zeros((8, D_RNN), F32)

        u = (_rms(h_ref[...])[1] * g_ref[...]).astype(BF16)
        u_ref[...] = u
        p_ref[...] = _dot_nt(u, win_ref[...])

        ext_q[pl.ds(HALO, tm), :] = p_ref[:, 0:256]
        pooled, _, _ = _pool_fwd(ext_q, tmp_a, tmp_b, tm, i * tm)
        y_ref[:, 0:256] = (_dot(pooled.astype(BF16), wp[...]) * psc[...]).astype(BF16)

        ext_u[pl.ds(HALO, tm), :] = p_ref[:, 256:512] * _sig(p_ref[:, 512:768])
        conv = _conv_fwd(ext_u, dww, dwb[...], tm, zbuf)
        conv_ref[...] = conv
        act = _ln_silu(conv, lng[...], lnb[...])[4]
        y_ref[:, 256:512] = _dot(act.astype(BF16), wpw[...]).astype(BF16)

        ext_x[pl.ds(HALO, tm), :] = p_ref[:, 1280:1792]
        xc = _rg_conv(ext_x, cw, cb[...], tm)
        xc_ref[...] = xc
        r, ig, a, m = _rg_gates(xc, wa[...], ba[...], wx[...], bx[...], lam[...])
        for j, gate in enumerate((r, ig, a, m)):
            gates_ref[:, j * D_RNN:(j + 1) * D_RNN] = gate
        a_s[...] = a
        b_s[...] = m * (ig * xc)
        _scan_rows(a_s, b_s, hs_ref, hcar, tm, reverse=False)
        y_ref[:, 512:1024] = (_gelu_and_grad(p_ref[:, 768:1280])[0] * hs_ref[...]).astype(BF16)

        ext_q[0:HALO, :] = ext_q[pl.ds(tm, HALO), :]
        ext_u[0:HALO, :] = ext_u[pl.ds(tm, HALO), :]
        ext_x[0:HALO, :] = ext_x[pl.ds(tm, HALO), :]

    ws = [mw[k] for k in _MIX_W]
    row = lambda w: pl.BlockSpec((tm, w), lambda i: (i, 0))
    return pl.pallas_call(
        body, name="mixer_fwd", grid=(t // tm,),
        in_specs=[row(D_MODEL), _full(g), _resident(w_in)] + [_full(w) for w in ws],
        out_specs=[row(D_MODEL), row(D_IN), row(D_MODEL), row(D_RNN), row(D_CONV), row(D_RNN), row(4 * D_RNN)],
        out_shape=[jax.ShapeDtypeStruct((t, D_MODEL), BF16), jax.ShapeDtypeStruct((t, D_IN), F32),
                   jax.ShapeDtypeStruct((t, D_MODEL), BF16), jax.ShapeDtypeStruct((t, D_RNN), F32),
                   jax.ShapeDtypeStruct((t, D_CONV), F32), jax.ShapeDtypeStruct((t, D_RNN), F32),
                   jax.ShapeDtypeStruct((t, 4 * D_RNN), F32)],
        scratch_shapes=[pltpu.VMEM((HALO + tm, D_POOL), F32), pltpu.VMEM((HALO + tm, D_CONV), F32),
                        pltpu.VMEM((HALO + tm, D_RNN), F32), pltpu.VMEM((HALO + tm, D_POOL), F32),
                        pltpu.VMEM((HALO + tm, D_POOL), F32), pltpu.VMEM((tm + 8, D_CONV), F32),
                        pltpu.VMEM((tm, D_RNN), F32), pltpu.VMEM((tm, D_RNN), F32), pltpu.VMEM((8, D_RNN), F32)],
        compiler_params=_params("arbitrary"),
    )(h, g, w_in, *ws)


_MIX_G = (("wp", (D_POOL, D_POOL)), ("psc", (1, D_POOL)), ("dww", (32, 8, D_CONV)), ("dwb", (1, D_CONV)),
          ("lng", (1, D_CONV)), ("lnb", (1, D_CONV)), ("wpw", (D_CONV, D_CONV)), ("cw", (8, D_RNN)),
          ("cb", (1, D_RNN)), ("wa", (D_RNN, D_RNN)), ("ba", (1, D_RNN)), ("wx", (D_RNN, D_RNN)),
          ("bx", (1, D_RNN)), ("lam", (1, D_RNN)), ("g1", (1, D_MODEL)))


def _mixer_bwd(p, dh1, hs, conv, xc, gates, h0, g1, w_out, w_in, mw, deps=()):
    t = p.shape[0]
    tm = _row_tile(t, TM_MIX)
    nt = t // tm
    hb = tm // HALO

    def body(p_ref, ph_ref, dh1_ref, hs_ref, hsh_ref, conv_ref, xc_ref, gates_ref, h0_ref, g1_ref, wout_ref, win_ref,
             wp, psc, dww, dwb, lng, lnb, wpw, cw, cb, wa, ba, wx, bx, lam,
             dp_ref, dh0_ref, g_wp, g_psc, g_dww, g_dwb, g_lng, g_lnb, g_wpw, g_cw, g_cb, g_wa, g_ba, g_wx, g_bx, g_lam, g_g1,
             ext_q, ext_u, ext_x, ext_h, ee, dc_s, dx_s, tmp_a, tmp_b, zbuf, d_pad, a_s, b_s, g_s, gcar, dy_ref, dp_s):
        step = pl.program_id(0)
        i = nt - 1 - step
        grads = (g_wp, g_psc, g_dww, g_dwb, g_lng, g_lnb, g_wpw, g_cw, g_cb, g_wa, g_ba, g_wx, g_bx, g_lam, g_g1)
        dy_ref[...] = dh1_ref[...].astype(BF16)
        dy_cols = lambda lo, hi: _dot_nt(dy_ref[...], wout_ref[lo:hi, :])

        @pl.when(step == 0)
        def _():
            for gr in grads:
                gr[...] = jnp.zeros(gr.shape, F32)
            ee[pl.ds(tm, HALO), :] = jnp.zeros((HALO, D_POOL), F32)
            dc_s[pl.ds(tm, HALO), :] = jnp.zeros((HALO, D_CONV), F32)
            dx_s[pl.ds(tm, HALO), :] = jnp.zeros((HALO, D_RNN), F32)
            d_pad[0:8, :] = jnp.zeros((8, D_CONV), F32)
            d_pad[pl.ds(tm + 8, 8), :] = jnp.zeros((8, D_CONV), F32)
            gcar[...] = jnp.zeros((8, D_RNN), F32)

        hm = jnp.where(i == 0, 0.0, 1.0)

        ext_q[0:HALO, :] = ph_ref[:, 0:256] * hm
        ext_q[pl.ds(HALO, tm), :] = p_ref[:, 0:256]
        pooled, lane, cnt = _pool_fwd(ext_q, tmp_a, tmp_b, tm, i * tm)
        pooled_b = pooled.astype(BF16)
        dya = dy_cols(0, 256)
        g_psc[...] += _colsum(dya * _dot(pooled_b, wp[...]))
        dmixed_b = (dya * psc[...]).astype(BF16)
        dpooled = _dot_nt(dmixed_b, wp[...])
        g_wp[...] += _dot_tn(pooled_b, dmixed_b)
        ee[0:tm, :] = dpooled / cnt
        dp_s[:, 0:256] = _lane_sel(lane, *_window_sums_ahead(ee, tmp_a, tmp_b, tm)) - dpooled
        ee[pl.ds(tm, HALO), :] = ee[0:HALO, :]

        v = p_ref[:, 256:512]
        s = _sig(p_ref[:, 512:768])
        ext_u[0:HALO, :] = ph_ref[:, 256:512] * _sig(ph_ref[:, 512:768]) * hm
        ext_u[pl.ds(HALO, tm), :] = v * s
        z, rstd, l, sl, act = _ln_silu(conv_ref[...], lng[...], lnb[...])
        dyb_b = dy_cols(256, 512).astype(BF16)
        dact = _dot_nt(dyb_b, wpw[...])
        g_wpw[...] += _dot_tn(act.astype(BF16), dyb_b)
        dl = dact * (sl * (1.0 + l * (1.0 - sl)))
        g_lng[...] += _colsum(dl * z)
        g_lnb[...] += _colsum(dl)
        dz = dl * lng[...]
        dc = rstd * (dz - jnp.mean(dz, axis=-1, keepdims=True) - z * jnp.mean(dz * z, axis=-1, keepdims=True))
        g_dwb[...] += _colsum(dc)
        dc_s[0:tm, :] = dc
        d_pad[pl.ds(8, tm), :] = dc
        _tap_grads(d_pad, ext_u, _CONV_OFFS, tm, g_dww, zbuf)
        du0 = _taps(dc_s, lambda j: dww[CONV_K - 1 - j:CONV_K - j, :], list(range(CONV_K)), tm, zbuf)
        dp_s[:, 256:512] = du0 * s
        dp_s[:, 512:768] = du0 * v * (s * (1.0 - s))
        dc_s[pl.ds(tm, HALO), :] = dc_s[0:HALO, :]

        ext_x[0:HALO, :] = ph_ref[:, 1280:1792] * hm
        ext_x[pl.ds(HALO, tm), :] = p_ref[:, 1280:1792]
        xc = xc_ref[...]
        xcb = xc.astype(BF16)
        r, ig, a, m = (gates_ref[:, j * D_RNN:(j + 1) * D_RNN] for j in range(4))
        sp = _softplus_neg(lam[...])
        ext_h[0:HALO, :] = hsh_ref[...] * hm
        ext_h[pl.ds(HALO, tm), :] = hs_ref[...]
        dyc = dy_cols(512, 1024)
        gl, dgl = _gelu_and_grad(p_ref[:, 768:1280])
        dp_s[:, 768:1280] = dyc * hs_ref[...] * dgl
        a_s[...] = a
        b_s[...] = dyc * gl
        _scan_rows(a_s, b_s, g_s, gcar, tm, reverse=True)
        g = g_s[...]
        da = g * ext_h[pl.ds(HALO - 1, tm), :]
        dm = g * (ig * xc)
        dig = g * (m * xc)
        dlog_a = da * a - dm * (a * a) / m
        g_lam[...] += _colsum(dlog_a * (-RG_C * r)) * (-_sig(-lam[...]))
        dra = (dlog_a * (-RG_C * sp)) * (r * (1.0 - r))
        dia = dig * (ig * (1.0 - ig))
        g_ba[...] += _colsum(dra)
        g_bx[...] += _colsum(dia)
        dra_b = dra.astype(BF16)
        dia_b = dia.astype(BF16)
        dxc = g * (m * ig) + _dot_nt(dra_b, wa[...]) + _dot_nt(dia_b, wx[...])
        g_wa[...] += _dot_tn(xcb, dra_b)
        g_wx[...] += _dot_tn(xcb, dia_b)
        g_cb[...] += _colsum(dxc)
        dx_s[0:tm, :] = dxc
        for k in range(RG_CONV_K):
            g_cw[k:k + 1, :] += _colsum(dxc * ext_x[pl.ds(HALO - (RG_CONV_K - 1) + k, tm), :])
        dxin = cw[RG_CONV_K - 1:RG_CONV_K, :] * dxc
        for j in range(1, RG_CONV_K):
            dxin = dxin + cw[RG_CONV_K - 1 - j:RG_CONV_K - j, :] * dx_s[pl.ds(j, tm), :]
        dp_s[:, 1280:1792] = dxin
        dx_s[pl.ds(tm, HALO), :] = dx_s[0:HALO, :]

        dpb = dp_s[...].astype(BF16)
        dp_ref[...] = dpb
        du = _dot(dpb, win_ref[...])
        r, n = _rms(h0_ref[...])
        g_g1[...] += _colsum(du * n)
        dh0_ref[...] = dh1_ref[...] + _rms_bwd(du, n, r, g1_ref[...])

    ws = [mw[k] for k in _MIX_W]
    tile = lambda w: pl.BlockSpec((tm, w), lambda s: (nt - 1 - s, 0))
    halo = lambda w: pl.BlockSpec((HALO, w), lambda s: (jnp.maximum((nt - 1 - s) * hb - 1, 0), 0))
    outs = pl.pallas_call(
        _after(body, 12 + len(ws), deps), name="mixer_bwd", grid=(nt,),
        in_specs=[tile(D_IN), halo(D_IN), tile(D_MODEL), tile(D_RNN), halo(D_RNN), tile(D_CONV), tile(D_RNN), tile(4 * D_RNN),
                  tile(D_MODEL), _full(g1),
                  _resident(w_out), _resident(w_in)] + [_full(w) for w in ws] + [ANY] * len(deps),
        out_specs=[tile(D_IN), tile(D_MODEL)] + [pl.BlockSpec(shp, lambda s, nd=len(shp): (0,) * nd) for _, shp in _MIX_G],
        out_shape=[jax.ShapeDtypeStruct((t, D_IN), BF16), jax.ShapeDtypeStruct((t, D_MODEL), F32)]
        + [jax.ShapeDtypeStruct(shp, F32) for _, shp in _MIX_G],
        scratch_shapes=[pltpu.VMEM((HALO + tm, D_POOL), F32), pltpu.VMEM((HALO + tm, D_CONV), F32),
                        pltpu.VMEM((HALO + tm, D_RNN), F32), pltpu.VMEM((HALO + tm, D_RNN), F32),
                        pltpu.VMEM((tm + HALO, D_POOL), F32), pltpu.VMEM((tm + HALO, D_CONV), F32),
                        pltpu.VMEM((tm + HALO, D_RNN), F32), pltpu.VMEM((HALO + tm, D_POOL), F32),
                        pltpu.VMEM((HALO + tm, D_POOL), F32), pltpu.VMEM((tm + 8, D_CONV), F32),
                        pltpu.VMEM((tm + 16, D_CONV), F32), pltpu.VMEM((tm, D_RNN), F32),
                        pltpu.VMEM((tm, D_RNN), F32), pltpu.VMEM((tm, D_RNN), F32), pltpu.VMEM((8, D_RNN), F32),
                        pltpu.VMEM((tm, D_MODEL), BF16), pltpu.VMEM((tm, D_IN), F32)],
        compiler_params=_params("arbitrary"),
    )(p, p, dh1, hs, hs, conv, xc, gates, h0, g1, w_out, w_in, *ws, *deps)
    return outs[0], outs[1], {k: o for (k, _), o in zip(_MIX_G, outs[2:])}


def _mid_fwd(y, h0, w_out, g, w_up):
    t = h0.shape[0]
    tm = _row_tile(t, TM_MAT)

    def body(y_ref, h0_ref, wo_ref, g_ref, wu_ref, h1_ref, u2_ref, f_ref):
        h1 = h0_ref[...] + _dot(y_ref[...], wo_ref[...])
        h1_ref[...] = h1
        u2 = (_rms(h1)[1] * g_ref[...]).astype(BF16)
        u2_ref[...] = u2
        for c in range(D_FF // FF_CHUNK):
            f_ref[:, c * FF_CHUNK:(c + 1) * FF_CHUNK] = _dot(u2, wu_ref[c]).astype(BF16)

    row = lambda w: pl.BlockSpec((tm, w), lambda i: (i, 0))
    return pl.pallas_call(
        body, name="mid_fwd", grid=(t // tm,),
        in_specs=[row(D_MODEL), row(D_MODEL), _resident(w_out), _full(g), _resident(w_up)],
        out_specs=[row(D_MODEL), row(D_MODEL), row(D_FF)],
        out_shape=[jax.ShapeDtypeStruct((t, D_MODEL), F32), jax.ShapeDtypeStruct((t, D_MODEL), BF16),
                   jax.ShapeDtypeStruct((t, D_FF), BF16)],
        compiler_params=_params("parallel"),
    )(y, h0, w_out, g, w_up)


def _down_proj(f_ref, h1_ref, wd_ref):
    acc = h1_ref[...]
    for c in range(D_FF // FF_CHUNK):
        cols = slice(c * FF_CHUNK, (c + 1) * FF_CHUNK)
        a = jnp.square(jnp.maximum(f_ref[:, cols].astype(F32), 0.0)).astype(BF16)
        acc = acc + _dot(a, wd_ref[cols, :])
    return acc


def _down_fwd(f, h1, w_down):
    t = h1.shape[0]
    tm = _row_tile(t, TM_MAT)

    def body(f_ref, h1_ref, wd_ref, h2_ref):
        h2_ref[...] = _down_proj(f_ref, h1_ref, wd_ref)

    row = lambda w: pl.BlockSpec((tm, w), lambda i: (i, 0))
    return pl.pallas_call(
        body, name="down_fwd", grid=(t // tm,),
        in_specs=[row(D_FF), row(D_MODEL), _resident(w_down)], out_specs=row(D_MODEL),
        out_shape=jax.ShapeDtypeStruct((t, D_MODEL), F32),
        compiler_params=_params("parallel"),
    )(f, h1, w_down)


def _down_fwd_loss(f, h1, w_down, g, tgt, t_real):
    t = h1.shape[0]
    tm = _row_tile(t, TM_MAT)

    def body(f_ref, h1_ref, wd_ref, g_ref, tgt_ref, loss_ref, dh_ref, dg_ref):
        i = pl.program_id(0)

        @pl.when(i == 0)
        def _():
            loss_ref[...] = jnp.zeros(loss_ref.shape, F32)
            dg_ref[...] = jnp.zeros(dg_ref.shape, F32)

        r, n = _rms(_down_proj(f_ref, h1_ref, wd_ref))
        row = lax.broadcasted_iota(jnp.int32, (tm, 1), 0) + i * tm
        valid = jnp.logical_and(row >= N_META, row < t_real)
        diff = jnp.where(valid, n * g_ref[...] - tgt_ref[...], 0.0)
        loss_ref[...] += 0.5 * jnp.sum(jnp.mean(diff * diff, axis=-1, keepdims=True))
        dy = diff * (1.0 / D_MODEL)
        dg_ref[...] += _colsum(dy * n)
        dh_ref[...] = _rms_bwd(dy, n, r, g_ref[...])

    row = lambda w: pl.BlockSpec((tm, w), lambda i: (i, 0))
    return pl.pallas_call(
        body, name="down_fwd_loss", grid=(t // tm,),
        in_specs=[row(D_FF), row(D_MODEL), _resident(w_down), _full(g), row(D_MODEL)],
        out_specs=[pl.BlockSpec((8, 128), lambda i: (0, 0)), row(D_MODEL), pl.BlockSpec((1, D_MODEL), lambda i: (0, 0))],
        out_shape=[jax.ShapeDtypeStruct((8, 128), F32), jax.ShapeDtypeStruct((t, D_MODEL), F32),
                   jax.ShapeDtypeStruct((1, D_MODEL), F32)],
        compiler_params=_params("arbitrary"),
    )(f, h1, w_down, g, tgt)


def _mlp_bwd(dh2, f, h1, g, w_up, w_down, deps=()):
    t = dh2.shape[0]
    tm = _row_tile(t, TM_MLP_BWD)

    def body(dh2_ref, f_ref, wd_ref, wu_ref, h1_ref, g_ref, df_ref, dh1_ref, dg_ref):
        @pl.when(pl.program_id(0) == 0)
        def _():
            dg_ref[...] = jnp.zeros(dg_ref.shape, F32)

        dh2 = dh2_ref[...]
        dhb = dh2.astype(BF16)
        du2 = None
        for c in range(D_FF // FF_CHUNK):
            cols = slice(c * FF_CHUNK, (c + 1) * FF_CHUNK)
            dact = _dot_nt(dhb, wd_ref[c])
            df = (dact * (2.0 * jnp.maximum(f_ref[:, cols].astype(F32), 0.0))).astype(BF16)
            df_ref[:, cols] = df
            part = _dot_nt(df, wu_ref[c])
            du2 = part if du2 is None else du2 + part
        r, n = _rms(h1_ref[...])
        dg_ref[...] += _colsum(du2 * n)
        dh1_ref[...] = dh2 + _rms_bwd(du2, n, r, g_ref[...])

    row = lambda w: pl.BlockSpec((tm, w), lambda i: (i, 0))
    return pl.pallas_call(
        _after(body, 6, deps), name="mlp_bwd", grid=(t // tm,),
        in_specs=[row(D_MODEL), row(D_FF), _resident(w_down), _resident(w_up), row(D_MODEL), _full(g)] + [ANY] * len(deps),
        out_specs=[row(D_FF), row(D_MODEL), pl.BlockSpec((1, D_MODEL), lambda i: (0, 0))],
        out_shape=[jax.ShapeDtypeStruct((t, D_FF), BF16), jax.ShapeDtypeStruct((t, D_MODEL), F32),
                   jax.ShapeDtypeStruct((1, D_MODEL), F32)],
        compiler_params=_params("arbitrary"),
    )(dh2, f, w_down, w_up, h1, g, *deps)


def _tn_matmul(a, b, kc, nc, relu2, name, deps=()):
    t, k = a.shape
    n = b.shape[1]
    tt = _row_tile(t, TM_MAT)
    gk, gn = k // kc, n // nc

    def body(a_ref, b_ref, o_ref):
        @pl.when(pl.program_id(2) == 0)
        def _():
            o_ref[...] = jnp.zeros(o_ref.shape, F32)

        av = a_ref[...]
        if relu2:
            av = jnp.square(jnp.maximum(av.astype(F32), 0.0))
        o_ref[...] += _dot_tn(av.astype(BF16), b_ref[...].astype(BF16))

    return pl.pallas_call(
        _after(body, 2, deps), name=name, grid=(gk, gn, t // tt),
        in_specs=[pl.BlockSpec((tt, kc), lambda ik, jn, it: (it, ik)), pl.BlockSpec((tt, nc), lambda ik, jn, it: (it, jn))]
        + [ANY] * len(deps),
        out_specs=pl.BlockSpec((None, kc, nc), lambda ik, jn, it: (ik * gn + jn, 0, 0)),
        out_shape=jax.ShapeDtypeStruct((gk * gn, kc, nc), F32),
        compiler_params=_params("parallel", "parallel", "arbitrary"),
    )(a, b, *deps)


def _block_diag(blocks):
    nb, hd, _ = blocks.shape
    eye = jnp.eye(nb, dtype=blocks.dtype)
    return (blocks[:, :, None, :] * eye[:, None, :, None]).reshape(nb * hd, nb * hd)


def _diag_blocks(m, nb):
    hd = m.shape[0] // nb
    eye = jnp.eye(nb, dtype=m.dtype)
    return jnp.sum(m.reshape(nb, hd, nb, hd) * eye[:, None, :, None], axis=2)


def _mixer_weights(w, l):
    row = lambda a: a.reshape(1, -1)
    return dict(
        wp=_block_diag(w["pool_w"][l]).astype(BF16), psc=row(w["pool_scale"][l]),
        dww=jnp.pad(w["convb_dw_w"][l], ((0, 32 - CONV_K), (0, 0))), dwb=row(w["convb_dw_b"][l]),
        lng=row(w["convb_ln_g"][l]), lnb=row(w["convb_ln_b"][l]), wpw=w["convb_pw_w"][l].astype(BF16),
        cw=jnp.pad(w["rg_conv_w"][l], ((0, 8 - RG_CONV_K), (0, 0))), cb=row(w["rg_conv_b"][l]),
        wa=_block_diag(w["rg_w_a"][l]).astype(BF16), ba=row(w["rg_b_a"][l]),
        wx=_block_diag(w["rg_w_x"][l]).astype(BF16), bx=row(w["rg_b_x"][l]), lam=row(w["rg_lambda"][l]))


def _local_step(h, tgt, t_real, w, fetch, hooks):
    depth = 2
    saved = []
    big = []
    for l in range(depth):
        mw = _mixer_weights(w, l)
        g1 = w["mix_norm_g"][l].reshape(1, -1)
        g2 = w["mlp_norm_g"][l].reshape(1, -1)
        wl = dict(w_in=fetch(l, "w_in", h))
        y, p, u, hs, conv, xc, gates = _mixer_fwd(h, g1, wl["w_in"], mw)
        wl["w_out"], wl["w_up"] = fetch(l, "w_out", y), fetch(l, "w_up", y)
        h1, u2, f = _mid_fwd(y, h, wl["w_out"], g2, wl["w_up"])
        wl["w_down"] = fetch(l, "w_down", f)
        if l == depth - 1:
            loss, dh, dgf = _down_fwd_loss(f, h1, wl["w_down"].reshape(D_FF, D_MODEL), w["final_norm_g"].reshape(1, -1), tgt,
                                           t_real)
            h2 = None
        else:
            h2 = _down_fwd(f, h1, wl["w_down"].reshape(D_FF, D_MODEL))
        saved.append(dict(mw=mw, g1=g1, g2=g2, h0=h, p=p, u=u, y=y, hs=hs, conv=conv, xc=xc, gates=gates, h1=h1, u2=u2, f=f))
        big.append(wl)
        h = h2

    gs = {k: [None] * depth for k in ("mix_norm_g", "mlp_norm_g", "pool_w", "pool_scale", "convb_dw_w", "convb_dw_b",
                                      "convb_ln_g", "convb_ln_b", "convb_pw_w", "rg_conv_w", "rg_conv_b", "rg_w_a",
                                      "rg_b_a", "rg_w_x", "rg_b_x", "rg_lambda")}
    deps = ()
    for l in reversed(range(depth)):
        s, wl = saved[l], big[l]
        df, dh1, dg2 = _mlp_bwd(dh, s["f"], s["h1"], s["g2"], wl["w_up"], wl["w_down"], deps)
        deps = hooks.point(l, "mlp_bwd", dh1)
        g_down = _tn_matmul(s["f"], dh, FF_CHUNK, D_MODEL, True, "dw_down", deps)
        hooks.grad(l, "w_down", g_down)
        deps = hooks.point(l, "dw_down", g_down)
        g_up = _tn_matmul(s["u2"], df, D_MODEL, FF_CHUNK, False, "dw_up", deps)
        hooks.grad(l, "w_up", g_up)
        deps = hooks.point(l, "dw_up", g_up)
        g_out = _tn_matmul(s["y"], dh1, D_MODEL, D_MODEL, False, "dw_out", deps)
        hooks.grad(l, "w_out", g_out.reshape(N_CHIPS, D_MODEL // N_CHIPS, D_MODEL))
        deps = hooks.point(l, "dw_out", g_out)
        dp, dh, mg = _mixer_bwd(s["p"], dh1, s["hs"], s["conv"], s["xc"], s["gates"], s["h0"], s["g1"], wl["w_out"],
                                wl["w_in"], s["mw"], deps)
        gs["mix_norm_g"][l] = mg["g1"][0]
        gs["mlp_norm_g"][l] = dg2[0]
        gs["pool_w"][l] = _diag_blocks(mg["wp"], D_POOL // POOL_GW)
        gs["pool_scale"][l] = mg["psc"][0]
        gs["convb_dw_w"][l] = jnp.sum(mg["dww"][:CONV_K], axis=1)
        gs["convb_dw_b"][l] = mg["dwb"][0]
        gs["convb_ln_g"][l] = mg["lng"][0]
        gs["convb_ln_b"][l] = mg["lnb"][0]
        gs["convb_pw_w"][l] = mg["wpw"]
        gs["rg_conv_w"][l] = mg["cw"][:RG_CONV_K]
        gs["rg_conv_b"][l] = mg["cb"][0]
        gs["rg_w_a"][l] = _diag_blocks(mg["wa"], D_RNN // RG_HD)
        gs["rg_b_a"][l] = mg["ba"][0]
        gs["rg_w_x"][l] = _diag_blocks(mg["wx"], D_RNN // RG_HD)
        gs["rg_b_x"][l] = mg["bx"][0]
        gs["rg_lambda"][l] = mg["lam"][0]
        if l == 0:
            gsmall = {k: jnp.stack(v) for k, v in gs.items()}
            gsmall["final_norm_g"] = dgf[0]
            gsmall["meta_tokens"] = dh[:N_META]
            started = hooks.small(gsmall)
        deps = hooks.point(l, "mixer_bwd", started[0] if l == 0 and started else dh)
        g_in = _tn_matmul(dp, s["u"], D_IN, D_MODEL, False, "dw_in", deps).reshape(N_CHIPS, D_IN // N_CHIPS, D_MODEL)
        hooks.grad(l, "w_in", g_in)
        deps = hooks.point(l, "dw_in", g_in)
    return loss[0, 0], dh


def _place():
    return lax.axis_index("x"), lax.axis_index("y"), lax.axis_index("c")


def _other_chips(x, y):
    return [(1 - x, y), (x, 1 - y), (1 - x, 1 - y)]


HBM_SPEC = pl.BlockSpec(memory_space=pltpu.HBM)
SEM_SPEC = pl.BlockSpec(memory_space=pltpu.SEMAPHORE)
DATAFLOW = pltpu.SideEffectType.DATAFLOW_SIDE_EFFECTING


def _gather_copies(src_refs, land_refs, send_sem, recv_sem, first):
    x, y, c = _place()
    me = 2 * x + y
    out = []
    for n in range(len(src_refs)):
        for j, (px, py) in enumerate(_other_chips(x, y)):
            out.append(pltpu.make_async_remote_copy(src_refs[n], land_refs[n].at[me], send_sem.at[first + 3 * n + j],
                                                    recv_sem.at[first + 3 * n + j], device_id=(px, py, c), device_id_type=MESH))
    return out


def _gather_start(groups, me):
    srcs = [pltpu.with_memory_space_constraint(s, pltpu.HBM) for g in groups for s in g]
    lands = [pltpu.with_memory_space_constraint(
        lax.dynamic_update_slice(jnp.zeros((N_CHIPS,) + s.shape, s.dtype), s[None], (me,) + (0,) * s.ndim), pltpu.HBM)
        for g in groups for s in g]
    n, ng = len(srcs), len(groups)
    first = [sum(len(g) for g in groups[:i]) for i in range(ng)]

    def body(*refs):
        src_refs, land_refs = refs[:n], refs[n:2 * n]
        sems = refs[2 * n:2 * n + 2 * ng]
        token = refs[-1]
        for gi, g in enumerate(groups):
            lo, hi = first[gi], first[gi] + len(g)
            for cp in _gather_copies(src_refs[lo:hi], land_refs[lo:hi], sems[2 * gi], sems[2 * gi + 1], 0):
                cp.start()
        token[...] = jnp.zeros(token.shape, token.dtype)

    sem_shapes = [pltpu.SemaphoreType.DMA((3 * len(g),)) for g in groups for _ in range(2)]
    outs = pl.pallas_call(
        body, name="gather_start",
        out_shape=sem_shapes + [pltpu.HBM(a.shape, a.dtype) for a in srcs + lands] + [jax.ShapeDtypeStruct((8, 128), F32)],
        in_specs=[HBM_SPEC] * (2 * n),
        out_specs=[SEM_SPEC] * (2 * ng) + [HBM_SPEC] * (2 * n) + [pl.BlockSpec(memory_space=pltpu.VMEM)],
        input_output_aliases={i: 2 * ng + i for i in range(2 * n)},
        compiler_params=pltpu.CompilerParams(has_side_effects=DATAFLOW),
    )(*srcs, *lands)
    sems, thru, token = outs[:2 * ng], outs[2 * ng:2 * ng + 2 * n], outs[-1]
    state = []
    for gi, g in enumerate(groups):
        lo, hi = first[gi], first[gi] + len(g)
        state.append((sems[2 * gi], sems[2 * gi + 1], thru[lo:hi], thru[n + lo:n + hi]))
    return state, token


def _gather_wait(state, after, name):
    send_sem, recv_sem, srcs, lands = state
    n = len(srcs)

    def body(*refs):
        src_refs, land_refs = refs[:n], refs[n:2 * n]
        send, recv = refs[2 * n], refs[2 * n + 1]
        for cp in _gather_copies(src_refs, land_refs, send, recv, 0):
            cp.wait_send()
            cp.wait_recv()

    outs = pl.pallas_call(
        body, name=name,
        out_shape=[pltpu.HBM(a.shape, a.dtype) for a in list(srcs) + list(lands)],
        in_specs=[HBM_SPEC] * (2 * n) + [SEM_SPEC, SEM_SPEC, ANY],
        out_specs=[HBM_SPEC] * (2 * n),
        input_output_aliases={i: i for i in range(2 * n)},
        compiler_params=pltpu.CompilerParams(has_side_effects=DATAFLOW),
    )(*srcs, *lands, send_sem, recv_sem, after)
    return outs[n:]


def _add_halves(g, recv, c1):
    nk, r, cd = g.shape
    r2 = r // 2
    rc = r2 // ROW_CHUNKS

    def body(c_ref, g_ref, r_ref, pab_ref):
        pab_ref[...] = (g_ref[...] + r_ref[...]).astype(BF16)

    blk = pl.BlockSpec((None, rc, cd), lambda k, j, c_ref: (k, j, 0))
    return pl.pallas_call(
        body, name="rs_add_halves",
        grid_spec=pltpu.PrefetchScalarGridSpec(
            num_scalar_prefetch=1, grid=(nk, ROW_CHUNKS),
            in_specs=[pl.BlockSpec((None, rc, cd), lambda k, j, c_ref: (k, c_ref[0] * ROW_CHUNKS + j, 0)), blk], out_specs=blk),
        out_shape=jax.ShapeDtypeStruct((nk, r2, cd), BF16),
        compiler_params=_params("parallel", "parallel"),
    )(c1, g, recv)


def _sum_partials(g, recv_sibling, recv_chips, c_me):
    nk, r, cd = g.shape
    r2 = r // 2
    rc = r2 // ROW_CHUNKS

    def body(cm_ref, g_ref, a_ref, r_ref, s_ref):
        own = g_ref[...] + a_ref[...]
        s_ref[...] = ((own + r_ref[0].astype(F32)) + r_ref[1].astype(F32)) + r_ref[2].astype(F32)

    return pl.pallas_call(
        body, name="rs_sum_partials",
        grid_spec=pltpu.PrefetchScalarGridSpec(
            num_scalar_prefetch=1, grid=(ROW_CHUNKS,),
            in_specs=[pl.BlockSpec((None, rc, cd), lambda j, cm: (cm[1], cm[0] * ROW_CHUNKS + j, 0)),
                      pl.BlockSpec((None, rc, cd), lambda j, cm: (cm[1], j, 0)),
                      pl.BlockSpec((3, rc, cd), lambda j, cm: (0, j, 0))],
            out_specs=pl.BlockSpec((rc, cd), lambda j, cm: (j, 0))),
        out_shape=jax.ShapeDtypeStruct((r2, cd), F32),
        compiler_params=_params("parallel"),
    )(c_me, g, recv_sibling, recv_chips)


def _split_start(name, srcs, lands, ncopies, make_copies):
    srcs = [pltpu.with_memory_space_constraint(s, pltpu.HBM) for s in srcs]
    lands = [pltpu.with_memory_space_constraint(a, pltpu.HBM) for a in lands]
    n, m = len(srcs), len(lands)

    def body(*refs):
        src_refs, land_refs = refs[:n], refs[n:n + m]
        send, recv, token = refs[n + m], refs[n + m + 1], refs[-1]
        for cp in make_copies(src_refs, land_refs, send, recv):
            cp.start()
        token[...] = jnp.zeros(token.shape, token.dtype)

    outs = pl.pallas_call(
        body, name=name,
        out_shape=[pltpu.SemaphoreType.DMA((ncopies,)), pltpu.SemaphoreType.DMA((ncopies,))]
        + [pltpu.HBM(a.shape, a.dtype) for a in srcs + lands] + [jax.ShapeDtypeStruct((8, 128), F32)],
        in_specs=[HBM_SPEC] * (n + m),
        out_specs=[SEM_SPEC, SEM_SPEC] + [HBM_SPEC] * (n + m) + [pl.BlockSpec(memory_space=pltpu.VMEM)],
        input_output_aliases={i: 2 + i for i in range(n + m)},
        compiler_params=pltpu.CompilerParams(has_side_effects=DATAFLOW),
    )(*srcs, *lands)
    return (outs[0], outs[1], outs[2:2 + n], outs[2 + n:2 + n + m], make_copies), outs[-1]


def _split_wait(name, state, after):
    send_sem, recv_sem, srcs, lands, make_copies = state
    n, m = len(srcs), len(lands)

    def body(*refs):
        src_refs, land_refs = refs[:n], refs[n:n + m]
        for cp in make_copies(src_refs, land_refs, refs[n + m], refs[n + m + 1]):
            cp.wait_send()
            cp.wait_recv()

    outs = pl.pallas_call(
        body, name=name,
        out_shape=[pltpu.HBM(a.shape, a.dtype) for a in list(srcs) + list(lands)],
        in_specs=[HBM_SPEC] * (n + m) + [SEM_SPEC, SEM_SPEC, ANY],
        out_specs=[HBM_SPEC] * (n + m),
        input_output_aliases={i: i for i in range(n + m)},
        compiler_params=pltpu.CompilerParams(has_side_effects=DATAFLOW),
    )(*srcs, *lands, send_sem, recv_sem, after)
    return outs[:n], outs[n:]


def _copies_to_sibling(src_of):
    def make(src_refs, land_refs, send, recv):
        x, y, c = _place()
        return [pltpu.make_async_remote_copy(src_of(src_refs[i], c), land_refs[i], send.at[i], recv.at[i],
                                             device_id=(x, y, 1 - c), device_id_type=MESH) for i in range(len(src_refs))]
    return make


def _copies_to_chips(src_refs, land_refs, send, recv):
    x, y, c = _place()
    return [pltpu.make_async_remote_copy(src_refs[i].at[2 * px + py], land_refs[i].at[j], send.at[3 * i + j], recv.at[3 * i + j],
                                         device_id=(px, py, c), device_id_type=MESH)
            for i in range(len(src_refs)) for j, (px, py) in enumerate(_other_chips(x, y))]


def _other_half_rows(ref, c):
    r2 = ref.shape[1] // 2
    return ref.at[:, pl.ds(pl.multiple_of((1 - c) * r2, 8), r2)]


class _ReduceScatter:
    def __init__(self, tag, grads, c1, me1):
        self.tag, self.grads, self.c1, self.me1 = tag, grads, c1, me1

    def start(self):
        lands = [lax.empty((g.shape[0], g.shape[1] // 2, g.shape[2]), F32) for g in self.grads]
        self.state, token = _split_start("rs_%s_a_start" % self.tag, self.grads, lands, len(self.grads),
                                         _copies_to_sibling(_other_half_rows))
        return token

    def to_chips(self, after):
        self.halves = _split_wait("rs_%s_a_wait" % self.tag, self.state, after)
        pabs = [_add_halves(g, r, self.c1) for g, r in zip(*self.halves)]
        lands = [lax.empty((3,) + p.shape[1:], BF16) for p in pabs]
        self.state, token = _split_start("rs_%s_b_start" % self.tag, pabs, lands, 3 * len(pabs), _copies_to_chips)
        return token

    def to_sibling(self, after):
        _, recv = _split_wait("rs_%s_b_wait" % self.tag, self.state, after)
        c_me = jnp.concatenate([self.c1, self.me1])
        sums = [_sum_partials(g, ra, rb, c_me) for g, ra, rb in zip(*self.halves, recv)]
        lands = [lax.empty(s.shape, F32) for s in sums]
        self.state, token = _split_start("rs_%s_c_start" % self.tag, sums, lands, len(sums),
                                         _copies_to_sibling(lambda ref, c: ref))
        return token

    def finish(self, after):
        return list(zip(*_split_wait("rs_%s_c_wait" % self.tag, self.state, after)))


def _add_lists(a_list, b_list):
    n = len(a_list)

    def body(*refs):
        for i in range(n):
            refs[2 * n + i][...] = refs[i][...] + refs[n + i][...]

    vm = pl.BlockSpec(memory_space=pltpu.VMEM)
    return pl.pallas_call(
        body, name="add_lists", in_specs=[vm] * (2 * n), out_specs=[vm] * n,
        out_shape=[jax.ShapeDtypeStruct(a.shape, a.dtype) for a in a_list],
        compiler_params=pltpu.CompilerParams(vmem_limit_bytes=VMEM_LIMIT),
    )(*a_list, *b_list)


def _copies_to_peer(stage):
    def make(src_refs, land_refs, send, recv):
        x, y, c = _place()
        peer = [(x, y, 1 - c), (1 - x, y, c), (x, 1 - y, c)][stage]
        return [pltpu.make_async_remote_copy(src_refs[i], land_refs[i], send.at[i], recv.at[i], device_id=peer, device_id_type=MESH)
                for i in range(len(src_refs))]
    return make


class _AllReduceSmall:
    def __init__(self, vs):
        self.vs, self.stage = list(vs), 0

    def _start(self):
        lands = [lax.empty(v.shape, v.dtype) for v in self.vs]
        self.state, token = _split_start("ar_small_start_%d" % self.stage, self.vs, lands, len(self.vs), _copies_to_peer(self.stage))
        return token

    def start(self):
        return self._start()

    def step(self, after):
        mine, theirs = _split_wait("ar_small_wait_%d" % self.stage, self.state, after)
        self.vs = _add_lists(mine, theirs)
        self.stage += 1
        return self._start() if self.stage < 3 else self.vs[0]


def _adamw_math(w, g, m, v):
    m = ADAM_B1 * m + (1.0 - ADAM_B1) * g
    v = ADAM_B2 * v + (1.0 - ADAM_B2) * jnp.square(g)
    m_hat = m / (1.0 - ADAM_B1 ** ADAM_STEP)
    v_hat = v / (1.0 - ADAM_B2 ** ADAM_STEP)
    return -ADAM_LR * (m_hat / (jnp.sqrt(v_hat) + ADAM_EPS) + ADAM_WD * w), m, v


def _adamw_big_layer(layer, w, m, v, own, sib, c1, prev):
    _, r, cd = w.shape
    rc = r // 2 // ROW_CHUNKS

    def body(c_ref, w_ref, m_ref, v_ref, own_ref, sib_ref, *rest):
        g_ref, d_ref, mo_ref, vo_ref, token = rest[-5:]
        g = jnp.where(pl.program_id(0) == c_ref[0], own_ref[...], sib_ref[...])
        g_ref[...] = g
        d_ref[...], mo_ref[...], vo_ref[...] = _adamw_math(w_ref[...], g, m_ref[...], v_ref[...])
        token[...] = jnp.zeros(token.shape, F32)

    blk = pl.BlockSpec((None, rc, cd), lambda hh, j, c_ref: (layer, hh * ROW_CHUNKS + j, 0))
    half = pl.BlockSpec((rc, cd), lambda hh, j, c_ref: (j, 0))
    prev = () if prev is None else tuple(prev)
    outs = pl.pallas_call(
        body, name="adamw_big",
        grid_spec=pltpu.PrefetchScalarGridSpec(
            num_scalar_prefetch=1, grid=(2, ROW_CHUNKS), in_specs=[blk, blk, blk, half, half] + [ANY] * len(prev),
            out_specs=[blk] * 4 + [pl.BlockSpec((8, 128), lambda hh, j, c_ref: (0, 0))]),
        out_shape=[jax.ShapeDtypeStruct(w.shape, F32)] * 4 + [jax.ShapeDtypeStruct((8, 128), F32)],
        input_output_aliases={6 + i: i for i in range(len(prev))},
        compiler_params=_params("arbitrary", "arbitrary"),
    )(c1, w, m, v, own, sib, *prev)
    return outs[:4], outs[4]


def _adamw_small(ws, gs, ms, vs):
    n = len(ws)

    def body(*refs):
        w_refs, g_refs, m_refs, v_refs = refs[:n], refs[n:2 * n], refs[2 * n:3 * n], refs[3 * n:4 * n]
        outs = refs[4 * n:]
        for i in range(n):
            outs[3 * i][...], outs[3 * i + 1][...], outs[3 * i + 2][...] = _adamw_math(
                w_refs[i][...], g_refs[i][...], m_refs[i][...], v_refs[i][...])

    vm = pl.BlockSpec(memory_space=pltpu.VMEM)
    outs = pl.pallas_call(
        body, name="adamw_small", in_specs=[vm] * (4 * n), out_specs=[vm] * (3 * n),
        out_shape=[jax.ShapeDtypeStruct(w.shape, F32) for w in ws for _ in range(3)],
        compiler_params=pltpu.CompilerParams(vmem_limit_bytes=VMEM_LIMIT),
    )(*ws, *gs, *ms, *vs)
    return [outs[3 * i:3 * i + 3] for i in range(n)]


LANES = 128
SUBLANES = 8
SHARDED_AXIS = {"meta_tokens": 1, "convb_dw_w": 2, "convb_pw_w": 1, "rg_conv_w": 2}


def _rows_of(size):
    return -(-size // (LANES * SUBLANES)) * SUBLANES


def _as_rows(a, rows=None):
    flat = a.reshape(-1)
    rows = _rows_of(flat.size) if rows is None else rows
    return jnp.pad(flat, (0, rows * LANES - flat.size)).reshape(rows, LANES)


class _GradientSchedule:
    GROUPS = {"l1": [(1, "w_down"), (1, "w_up"), (1, "w_out"), (1, "w_in")], "a0": [(0, "w_down"), (0, "w_up")],
              "b0": [(0, "w_out")], "c0": [(0, "w_in")]}
    PLAN = {
        (1, "dw_in"): [("l1", "start")],
        (0, "mlp_bwd"): [("l1", "to_chips")],
        (0, "dw_up"): [("l1", "to_sibling"), ("a0", "start")],
        (0, "dw_out"): [("l1", "finish"), ("a0", "to_chips"), ("b0", "start")],
        (0, "mixer_bwd"): [("a0", "to_sibling"), ("b0", "to_chips"), ("small", "step")],
        (0, "dw_in"): [("c0", "start"), ("small", "step"), ("c0", "to_chips"), ("a0", "finish"), ("b0", "to_sibling")],
    }

    def __init__(self, w, mom, var, c1, me1):
        self.w, self.mom, self.var, self.c1, self.me1 = w, mom, var, c1, me1
        self.grads, self.chains, self.out = {}, {}, {}

    def grad(self, layer, name, g):
        self.grads[layer, name] = g

    def small(self, gsmall):
        self.small_sum = _AllReduceSmall([g.reshape(1, -1) if g.ndim == 1 else g for g in (gsmall[k] for k in SMALL)])
        return (self.small_sum.start(),)

    def point(self, layer, kernel_name, after):
        return self.run(self.PLAN.get((layer, kernel_name), ()), after) or (after,)

    def run(self, actions, after):
        deps = []
        for tag, stage in actions:
            if tag == "small":
                deps.append(self.small_sum.step(after))
            elif stage == "start":
                self.chains[tag] = _ReduceScatter(tag, [self.grads[lk] for lk in self.GROUPS[tag]], self.c1, self.me1)
                deps.append(self.chains[tag].start())
            elif stage == "finish":
                for (layer, k), (own, sib) in zip(self.GROUPS[tag], self.chains[tag].finish(after)):
                    self.out[k], token = _adamw_big_layer(layer, self.w[k], self.mom[k], self.var[k], own, sib, self.c1,
                                                          self.out.get(k))
                    deps.append(token)
            else:
                deps.append(getattr(self.chains[tag], stage)(after))
            after = deps[-1]
        self.last = after
        return tuple(deps)


def _from_shard_major(name, sm):
    if name == "meta_tokens":
        return sm.transpose(1, 0, 2).reshape(N_META, -1)
    if name == "convb_pw_w":
        return sm.transpose(1, 0, 2, 3).reshape(2, -1, D_CONV)
    return sm.transpose(1, 2, 0, 3).reshape(sm.shape[1], sm.shape[2], -1)


def kernel(x, meta_tokens, mix_norm_g, w_in, pool_w, pool_scale, convb_dw_w, convb_dw_b, convb_ln_g, convb_ln_b, convb_pw_w, rg_conv_w, rg_conv_b, rg_w_a, rg_b_a, rg_w_x, rg_b_x, rg_lambda, w_out, mlp_norm_g, w_up, w_down, final_norm_g, loss_target, m_meta_tokens, m_mix_norm_g, m_w_in, m_pool_w, m_pool_scale, m_convb_dw_w, m_convb_dw_b, m_convb_ln_g, m_convb_ln_b, m_convb_pw_w, m_rg_conv_w, m_rg_conv_b, m_rg_w_a, m_rg_b_a, m_rg_w_x, m_rg_b_x, m_rg_lambda, m_w_out, m_mlp_norm_g, m_w_up, m_w_down, m_final_norm_g, v_meta_tokens, v_mix_norm_g, v_w_in, v_pool_w, v_pool_scale, v_convb_dw_w, v_convb_dw_b, v_convb_ln_g, v_convb_ln_b, v_convb_pw_w, v_rg_conv_w, v_rg_conv_b, v_rg_w_a, v_rg_b_a, v_rg_w_x, v_rg_b_x, v_rg_lambda, v_w_out, v_mlp_norm_g, v_w_up, v_w_down, v_final_norm_g):
    given = dict(locals())
    w = {k: given[k] for k in WEIGHTS}
    mom = {k: given["m_" + k] for k in WEIGHTS}
    var = {k: given["v_" + k] for k in WEIGHTS}
    xi, yi, ci = _place()
    me1 = (2 * xi + yi).astype(jnp.int32).reshape(1)
    c1 = ci.astype(jnp.int32).reshape(1)

    small_rows = [_rows_of(w[k].size) for k in SMALL_SHARDED]
    small_pack = jnp.concatenate([_as_rows(w[k]) for k in SMALL_SHARDED])
    transposed = lambda d: {**d, "w_in": d["w_in"].transpose(0, 2, 1)}
    wt, momt, vart = transposed(w), transposed(mom), transposed(var)
    shard = lambda l, k: wt[k][l].astype(BF16)
    order = [[(0, "w_in"), "small"], [(0, "w_out"), (0, "w_up")], [(0, "w_down")], [(1, "w_in")], [(1, "w_out"), (1, "w_up")],
             [(1, "w_down")]]
    state, token = _gather_start([[small_pack if lk == "small" else shard(*lk) for lk in g] for g in order], me1[0])
    landed = {}

    def fetch(l, k, after):
        gi = [i for i, g in enumerate(order) if (l, k) in g][0]
        if gi not in landed:
            landed[gi] = _gather_wait(state[gi], after, "gather_wait_%d" % gi)
        raw = landed[gi][order[gi].index((l, k))]
        if k == "w_in":
            return raw.reshape(D_IN, D_MODEL)
        return raw.reshape(D_MODEL, D_MODEL) if k == "w_out" else raw

    seq = x.shape[1]
    t_real = N_META + seq
    t_pad = -(-t_real // ROW_ALIGN) * ROW_ALIGN
    tail = jnp.zeros((t_pad - t_real, D_MODEL), F32)
    front = jnp.zeros((N_META, D_MODEL), F32)
    h = jnp.concatenate([front + token[0, 0], x[0], tail])
    tgt = jnp.concatenate([front, loss_target[0], tail])
    landed[0] = _gather_wait(state[0], h, "gather_wait_0")
    wfull = {k: (w[k] + token[0, 0] if k in ("pool_w", "rg_w_a", "rg_w_x") else w[k]) for k in WEIGHTS}
    off = 0
    for k, rows in zip(SMALL_SHARDED, small_rows):
        sm = landed[0][1][:, off:off + rows].reshape(N_CHIPS, -1)[:, :w[k].size].reshape((N_CHIPS,) + w[k].shape)
        wfull[k] = _from_shard_major(k, sm)
        off += rows
    h = lax.dynamic_update_slice(h, wfull["meta_tokens"], (0, 0))
    sched = _GradientSchedule(wt, momt, vart, c1, me1)
    loss, dh = _local_step(h, tgt, t_real, wfull, fetch, sched)
    grad_x = dh[N_META:t_real][None]

    names = SMALL
    two_d = lambda a: a.reshape(1, -1) if a.ndim == 1 else a
    sched.run([("small", "step")], sched.last)
    summed = dict(zip(names, sched.small_sum.vs))
    for k in SMALL_SHARDED:
        ax = SHARDED_AXIS[k]
        summed[k] = lax.dynamic_slice_in_dim(summed[k], me1[0] * w[k].shape[ax], w[k].shape[ax], axis=ax)

    out = {}
    res = _adamw_small([two_d(w[k]) for k in names], [summed[k] for k in names], [two_d(mom[k]) for k in names],
                       [two_d(var[k]) for k in names])
    for k, (d, m2, v2) in zip(names, res):
        out[k] = tuple(o.reshape(w[k].shape) for o in (summed[k], d, m2, v2))
    sched.run([("b0", "finish"), ("c0", "to_sibling"), ("c0", "finish")], res[0][0])
    out.update(sched.out)
    out["w_in"] = tuple(o.transpose(0, 2, 1) for o in out["w_in"])

    loss = lax.psum(loss, ("x", "y", "c"))
    return (loss, grad_x, *[out[k][0] for k in WEIGHTS], *[out[k][1] for k in WEIGHTS],
            *[out[k][2] for k in WEIGHTS], *[out[k][3] for k in WEIGHTS])
```

```python
import functools

import jax
import jax.numpy as jnp
from jax import lax
from jax.experimental import pallas as pl
from jax.experimental.pallas import tpu as pltpu

F32, BF16 = jnp.float32, jnp.bfloat16
MESH = pl.DeviceIdType.MESH
ANY = pl.BlockSpec(memory_space=pl.ANY)

D_MODEL = 1024
N_META = 16
D_POOL = 256
D_CONV = 256
D_RNN = 512
D_IN = D_POOL + 2 * D_CONV + 2 * D_RNN
D_FF = 4096
FF_CHUNK = 1024
POOL_GW = 64
CONV_K = 31
RG_CONV_K = 4
RG_HD = 64
RG_C = 8.0
EPS = 1e-6
ADAM_LR, ADAM_B1, ADAM_B2, ADAM_EPS, ADAM_WD, ADAM_STEP = 0.001, 0.9, 0.999, 1e-08, 0.01, 10

HALO = 32
ROW_ALIGN = 256
TM_MIX = 384
TM_MAT = 768
TM_MLP_BWD = 384
N_CHIPS = 4
ROW_CHUNKS = 1
VMEM_LIMIT = 56 * 1024 * 1024

BIG = ("w_in", "w_out", "w_up", "w_down")
SMALL_SHARDED = ("meta_tokens", "convb_dw_w", "convb_pw_w", "rg_conv_w")
SMALL_REPL = ("mix_norm_g", "pool_w", "pool_scale", "convb_dw_b", "convb_ln_g", "convb_ln_b", "rg_conv_b",
              "rg_w_a", "rg_b_a", "rg_w_x", "rg_b_x", "rg_lambda", "mlp_norm_g", "final_norm_g")
SMALL = SMALL_REPL + SMALL_SHARDED
WEIGHTS = ("meta_tokens", "mix_norm_g", "w_in", "pool_w", "pool_scale", "convb_dw_w", "convb_dw_b", "convb_ln_g",
           "convb_ln_b", "convb_pw_w", "rg_conv_w", "rg_conv_b", "rg_w_a", "rg_b_a", "rg_w_x", "rg_b_x",
           "rg_lambda", "w_out", "mlp_norm_g", "w_up", "w_down", "final_norm_g")


def _params(*sem):
    return pltpu.CompilerParams(dimension_semantics=sem, vmem_limit_bytes=VMEM_LIMIT)


def _row_tile(t, cap):
    best = None
    for tm in range(128, cap + 1, 128):
        if t % tm == 0:
            best = tm
    assert best is not None, (t, cap)
    return best


def _dot(a, b):
    return jnp.dot(a, b, preferred_element_type=F32)


def _dot_nt(a, b):
    return lax.dot_general(a, b, (((1,), (1,)), ((), ())), preferred_element_type=F32)


def _dot_tn(a, b):
    return lax.dot_general(a, b, (((0,), (0,)), ((), ())), preferred_element_type=F32)


def _rms(x):
    r = lax.rsqrt(jnp.mean(x * x, axis=-1, keepdims=True) + EPS)
    return r, x * r


def _rms_bwd(du, n, r, g):
    dn = du * g
    return r * (dn - n * jnp.mean(dn * n, axis=-1, keepdims=True))


def _sig(x):
    return jax.nn.sigmoid(x)


def _colsum(x):
    return jnp.sum(x, axis=0, keepdims=True)


def _one_minus_sq(a, log_a):
    x = 2.0 * log_a
    series = -x * (1.0 + x * (0.5 + x * (1.0 / 6)))
    return jnp.where(x > -0.01, series, 1.0 - a * a)


_GELU_K0 = 0.7978845608028654
_GELU_K1 = 0.044715


def _gelu_and_grad(x):
    th = jnp.tanh(_GELU_K0 * (x + _GELU_K1 * x * x * x))
    val = 0.5 * x * (1.0 + th)
    grad = 0.5 * (1.0 + th) + 0.5 * x * (1.0 - th * th) * _GELU_K0 * (1.0 + 3.0 * _GELU_K1 * x * x)
    return val, grad


def _full(a):
    nd = a.ndim
    return pl.BlockSpec(a.shape, lambda *_: (0,) * nd)


def _resident(a):
    nd = a.ndim
    return pl.BlockSpec(a.shape, lambda *_: (0,) * nd, pipeline_mode=pl.Buffered(1))


def _after(body, n_in, deps):
    def wrapped(*refs):
        return body(*refs[:n_in], *refs[n_in + len(deps):])
    return wrapped


def _lane_sel(lane, a2, a4, a8, a16):
    return jnp.where(lane < POOL_GW, a2, jnp.where(lane < 2 * POOL_GW, a4, jnp.where(lane < 3 * POOL_GW, a8, a16)))


def _window_sums_back(src, tmp_a, tmp_b, tm):
    n = HALO + tm
    rows = lambda ref, lo, back: ref[pl.ds(lo - back, n - lo), :]
    tmp_a[pl.ds(8, n - 8), :] = rows(src, 8, 0) + rows(src, 8, 1)
    tmp_b[pl.ds(16, n - 16), :] = rows(tmp_a, 16, 0) + rows(tmp_a, 16, 2)
    s2 = rows(tmp_a, HALO, 0)
    tmp_a[pl.ds(24, n - 24), :] = rows(tmp_b, 24, 0) + rows(tmp_b, 24, 4)
    s8 = rows(tmp_a, HALO, 0)
    return s2, rows(tmp_b, HALO, 0), s8, s8 + rows(tmp_a, HALO, 8)


def _window_sums_ahead(src, tmp_a, tmp_b, tm):
    rows = lambda ref, n, ahead: ref[pl.ds(ahead, n), :]
    tmp_a[pl.ds(0, tm + 24), :] = rows(src, tm + 24, 0) + rows(src, tm + 24, 1)
    tmp_b[pl.ds(0, tm + 16), :] = rows(tmp_a, tm + 16, 0) + rows(tmp_a, tm + 16, 2)
    s2 = rows(tmp_a, tm, 0)
    tmp_a[pl.ds(0, tm + 8), :] = rows(tmp_b, tm + 8, 0) + rows(tmp_b, tm + 8, 4)
    s8 = rows(tmp_a, tm, 0)
    return s2, rows(tmp_b, tm, 0), s8, s8 + rows(tmp_a, tm, 8)


def _pool_counts(tm, t0):
    lane = lax.broadcasted_iota(jnp.int32, (tm, D_POOL), 1)
    row = lax.broadcasted_iota(jnp.int32, (tm, D_POOL), 0) + t0
    cnt = jnp.minimum(row + 1, _lane_sel(lane, 2, 4, 8, 16)).astype(F32)
    return lane, cnt


def _pool_fwd(ext_q, tmp_a, tmp_b, tm, t0):
    lane, cnt = _pool_counts(tm, t0)
    q = ext_q[pl.ds(HALO, tm), :]
    pooled = _lane_sel(lane, *_window_sums_back(ext_q, tmp_a, tmp_b, tm)) / cnt - q
    return pooled, lane, cnt


def _taps(src, w_of, offs, tm, zbuf):
    acc = None
    for r in range(8):
        ks = [k for k in range(len(offs)) if offs[k] % 8 == r]
        if not ks:
            continue
        rows = tm + (8 if r else 0)
        z = w_of(ks[0]) * src[pl.ds(offs[ks[0]] - r, rows), :]
        for k in ks[1:]:
            z = z + w_of(k) * src[pl.ds(offs[k] - r, rows), :]
        if r:
            zbuf[...] = z
            z = zbuf[pl.ds(r, tm), :]
        acc = z if acc is None else acc + z
    return acc


def _tap_grads(d_pad, src, offs, tm, g_ref, zbuf):
    ch = src.shape[-1]
    for r in range(8):
        ks = [k for k in range(len(offs)) if offs[k] % 8 == r]
        if not ks:
            continue
        rows = tm + (8 if r else 0)
        if r:
            zbuf[...] = d_pad[pl.ds(8 - r, rows), :]
        for k in ks:
            d = zbuf[...] if r else d_pad[pl.ds(8, rows), :]
            prod = d * src[pl.ds(offs[k] - r, rows), :]
            g_ref[k] += jnp.sum(prod.reshape(rows // 8, 8, ch), axis=0)


_CONV_OFFS = [HALO - (CONV_K - 1) + k for k in range(CONV_K)]


def _conv_fwd(ext_u, dww_ref, dwb, tm, zbuf):
    return dwb + _taps(ext_u, lambda k: dww_ref[k:k + 1, :], _CONV_OFFS, tm, zbuf)


def _ln_silu(c, lng, lnb):
    mu = jnp.mean(c, axis=-1, keepdims=True)
    cc = c - mu
    rstd = lax.rsqrt(jnp.mean(cc * cc, axis=-1, keepdims=True) + EPS)
    z = cc * rstd
    l = z * lng + lnb
    sl = _sig(l)
    return z, rstd, l, sl, l * sl


def _rg_conv(ext_x, cw_ref, cb, tm):
    xc = cb + cw_ref[0:1, :] * ext_x[pl.ds(HALO - (RG_CONV_K - 1), tm), :]
    for k in range(1, RG_CONV_K):
        xc = xc + cw_ref[k:k + 1, :] * ext_x[pl.ds(HALO - (RG_CONV_K - 1) + k, tm), :]
    return xc


def _softplus_neg(lam):
    return jnp.maximum(-lam, 0.0) + jnp.log(1.0 + jnp.exp(-jnp.abs(lam)))


def _rg_gates(xc, wa, ba, wx, bx, lam):
    xcb = xc.astype(BF16)
    r = _sig(_dot(xcb, wa) + ba)
    ig = _sig(_dot(xcb, wx) + bx)
    log_a = (-RG_C * r) * _softplus_neg(lam)
    a = jnp.exp(log_a)
    return r, ig, a, jnp.sqrt(_one_minus_sq(a, log_a))


def _scan_rows(a_ref, b_ref, out_ref, carry, tm, reverse):
    rows = lax.broadcasted_iota(jnp.int32, (8, D_RNN), 0)
    ngrp = tm // 8

    def grp(gi, hb):
        st = pl.multiple_of((ngrp - 1 - gi if reverse else gi) * 8, 8)
        a8 = a_ref[pl.ds(st, 8), :]
        b8 = b_ref[pl.ds(st, 8), :]
        out = jnp.zeros((8, D_RNN), F32)
        for j in (range(7, -1, -1) if reverse else range(8)):
            aj = jnp.broadcast_to(a8[j:j + 1, :], (8, D_RNN))
            bj = jnp.broadcast_to(b8[j:j + 1, :], (8, D_RNN))
            if reverse:
                cur = bj + hb
                hb = aj * cur
            else:
                cur = aj * hb + bj
                hb = cur
            out = jnp.where(rows == j, cur, out)
        out_ref[pl.ds(st, 8), :] = out
        return hb

    carry[...] = lax.fori_loop(0, ngrp, grp, carry[...])


_MIX_W = ("wp", "psc", "dww", "dwb", "lng", "lnb", "wpw", "cw", "cb", "wa", "ba", "wx", "bx", "lam")


def _mixer_fwd(h, g, w_in, mw):
    t = h.shape[0]
    tm = _row_tile(t, TM_MIX)

    def body(h_ref, g_ref, win_ref, wp, psc, dww, dwb, lng, lnb, wpw, cw, cb, wa, ba, wx, bx, lam,
             y_ref, p_ref, u_ref, hs_ref, conv_ref, xc_ref, gates_ref, ext_q, ext_u, ext_x, tmp_a, tmp_b, zbuf, a_s, b_s, hcar):
        i = pl.program_id(0)

        @pl.when(i == 0)
        def _():
            ext_q[0:HALO, :] = jnp.zeros((HALO, D_POOL), F32)
            ext_u[0:HALO, :] = jnp.zeros((HALO, D_CONV), F32)
            ext_x[0:HALO, :] = jnp.zeros((HALO, D_RNN), F32)
            hcar[...] = jnp.zeros((8, D_RNN), F32)

        u = (_rms(h_ref[...])[1] * g_ref[...]).astype(BF16)
        u_ref[...] = u
        p_ref[...] = _dot_nt(u, win_ref[...])

        ext_q[pl.ds(HALO, tm), :] = p_ref[:, 0:256]
        pooled, _, _ = _pool_fwd(ext_q, tmp_a, tmp_b, tm, i * tm)
        y_ref[:, 0:256] = (_dot(pooled.astype(BF16), wp[...]) * psc[...]).astype(BF16)

        ext_u[pl.ds(HALO, tm), :] = p_ref[:, 256:512] * _sig(p_ref[:, 512:768])
        conv = _conv_fwd(ext_u, dww, dwb[...], tm, zbuf)
        conv_ref[...] = conv
        act = _ln_silu(conv, lng[...], lnb[...])[4]
        y_ref[:, 256:512] = _dot(act.astype(BF16), wpw[...]).astype(BF16)

        ext_x[pl.ds(HALO, tm), :] = p_ref[:, 1280:1792]
        xc = _rg_conv(ext_x, cw, cb[...], tm)
        xc_ref[...] = xc
        r, ig, a, m = _rg_gates(xc, wa[...], ba[...], wx[...], bx[...], lam[...])
        for j, gate in enumerate((r, ig, a, m)):
            gates_ref[:, j * D_RNN:(j + 1) * D_RNN] = gate
        a_s[...] = a
        b_s[...] = m * (ig * xc)
        _scan_rows(a_s, b_s, hs_ref, hcar, tm, reverse=False)
        y_ref[:, 512:1024] = (_gelu_and_grad(p_ref[:, 768:1280])[0] * hs_ref[...]).astype(BF16)

        ext_q[0:HALO, :] = ext_q[pl.ds(tm, HALO), :]
        ext_u[0:HALO, :] = ext_u[pl.ds(tm, HALO), :]
        ext_x[0:HALO, :] = ext_x[pl.ds(tm, HALO), :]

    ws = [mw[k] for k in _MIX_W]
    row = lambda w: pl.BlockSpec((tm, w), lambda i: (i, 0))
    return pl.pallas_call(
        body, name="mixer_fwd", grid=(t // tm,),
        in_specs=[row(D_MODEL), _full(g), _resident(w_in)] + [_full(w) for w in ws],
        out_specs=[row(D_MODEL), row(D_IN), row(D_MODEL), row(D_RNN), row(D_CONV), row(D_RNN), row(4 * D_RNN)],
        out_shape=[jax.ShapeDtypeStruct((t, D_MODEL), BF16), jax.ShapeDtypeStruct((t, D_IN), F32),
                   jax.ShapeDtypeStruct((t, D_MODEL), BF16), jax.ShapeDtypeStruct((t, D_RNN), F32),
                   jax.ShapeDtypeStruct((t, D_CONV), F32), jax.ShapeDtypeStruct((t, D_RNN), F32),
                   jax.ShapeDtypeStruct((t, 4 * D_RNN), F32)],
        scratch_shapes=[pltpu.VMEM((HALO + tm, D_POOL), F32), pltpu.VMEM((HALO + tm, D_CONV), F32),
                        pltpu.VMEM((HALO + tm, D_RNN), F32), pltpu.VMEM((HALO + tm, D_POOL), F32),
                        pltpu.VMEM((HALO + tm, D_POOL), F32), pltpu.VMEM((tm + 8, D_CONV), F32),
                        pltpu.VMEM((tm, D_RNN), F32), pltpu.VMEM((tm, D_RNN), F32), pltpu.VMEM((8, D_RNN), F32)],
        compiler_params=_params("arbitrary"),
    )(h, g, w_in, *ws)


_MIX_G = (("wp", (D_POOL, D_POOL)), ("psc", (1, D_POOL)), ("dww", (32, 8, D_CONV)), ("dwb", (1, D_CONV)),
          ("lng", (1, D_CONV)), ("lnb", (1, D_CONV)), ("wpw", (D_CONV, D_CONV)), ("cw", (8, D_RNN)),
          ("cb", (1, D_RNN)), ("wa", (D_RNN, D_RNN)), ("ba", (1, D_RNN)), ("wx", (D_RNN, D_RNN)),
          ("bx", (1, D_RNN)), ("lam", (1, D_RNN)), ("g1", (1, D_MODEL)))


def _mixer_bwd(p, dh1, hs, conv, xc, gates, h0, g1, w_out, w_in, mw, deps=(), lead=0, t_real=None):
    t = p.shape[0]
    tm = _row_tile(t, TM_MIX)
    nt = t // tm
    hb = tm // HALO
    t_out = t_real - lead if lead else t

    def body(p_ref, ph_ref, dh1_ref, hs_ref, hsh_ref, conv_ref, xc_ref, gates_ref, h0_ref, g1_ref, wout_ref, win_ref,
             wp, psc, dww, dwb, lng, lnb, wpw, cw, cb, wa, ba, wx, bx, lam,
             dp_ref, dh0_ref, g_wp, g_psc, g_dww, g_dwb, g_lng, g_lnb, g_wpw, g_cw, g_cb, g_wa, g_ba, g_wx, g_bx, g_lam, g_g1,
             *tail):
        dlead_ref, carry = (tail[0], tail[-1]) if lead else (None, None)
        (ext_q, ext_u, ext_x, ext_h, ee, dc_s, dx_s, tmp_a, tmp_b, zbuf, d_pad, a_s, b_s, g_s, gcar, dy_ref,
         dp_s) = tail[1:-1] if lead else tail
        step = pl.program_id(0)
        i = nt - 1 - step
        grads = (g_wp, g_psc, g_dww, g_dwb, g_lng, g_lnb, g_wpw, g_cw, g_cb, g_wa, g_ba, g_wx, g_bx, g_lam, g_g1)
        if lead:
            grads += (carry,)
        dy_ref[...] = dh1_ref[...].astype(BF16)
        dy_cols = lambda lo, hi: _dot_nt(dy_ref[...], wout_ref[lo:hi, :])

        @pl.when(step == 0)
        def _():
            for gr in grads:
                gr[...] = jnp.zeros(gr.shape, F32)
            ee[pl.ds(tm, HALO), :] = jnp.zeros((HALO, D_POOL), F32)
            dc_s[pl.ds(tm, HALO), :] = jnp.zeros((HALO, D_CONV), F32)
            dx_s[pl.ds(tm, HALO), :] = jnp.zeros((HALO, D_RNN), F32)
            d_pad[0:8, :] = jnp.zeros((8, D_CONV), F32)
            d_pad[pl.ds(tm + 8, 8), :] = jnp.zeros((8, D_CONV), F32)
            gcar[...] = jnp.zeros((8, D_RNN), F32)

        hm = jnp.where(i == 0, 0.0, 1.0)

        ext_q[0:HALO, :] = ph_ref[:, 0:256] * hm
        ext_q[pl.ds(HALO, tm), :] = p_ref[:, 0:256]
        pooled, lane, cnt = _pool_fwd(ext_q, tmp_a, tmp_b, tm, i * tm)
        pooled_b = pooled.astype(BF16)
        dya = dy_cols(0, 256)
        g_psc[...] += _colsum(dya * _dot(pooled_b, wp[...]))
        dmixed_b = (dya * psc[...]).astype(BF16)
        dpooled = _dot_nt(dmixed_b, wp[...])
        g_wp[...] += _dot_tn(pooled_b, dmixed_b)
        ee[0:tm, :] = dpooled / cnt
        dp_s[:, 0:256] = _lane_sel(lane, *_window_sums_ahead(ee, tmp_a, tmp_b, tm)) - dpooled
        ee[pl.ds(tm, HALO), :] = ee[0:HALO, :]

        v = p_ref[:, 256:512]
        s = _sig(p_ref[:, 512:768])
        ext_u[0:HALO, :] = ph_ref[:, 256:512] * _sig(ph_ref[:, 512:768]) * hm
        ext_u[pl.ds(HALO, tm), :] = v * s
        z, rstd, l, sl, act = _ln_silu(conv_ref[...], lng[...], lnb[...])
        dyb_b = dy_cols(256, 512).astype(BF16)
        dact = _dot_nt(dyb_b, wpw[...])
        g_wpw[...] += _dot_tn(act.astype(BF16), dyb_b)
        dl = dact * (sl * (1.0 + l * (1.0 - sl)))
        g_lng[...] += _colsum(dl * z)
        g_lnb[...] += _colsum(dl)
        dz = dl * lng[...]
        dc = rstd * (dz - jnp.mean(dz, axis=-1, keepdims=True) - z * jnp.mean(dz * z, axis=-1, keepdims=True))
        g_dwb[...] += _colsum(dc)
        dc_s[0:tm, :] = dc
        d_pad[pl.ds(8, tm), :] = dc
        _tap_grads(d_pad, ext_u, _CONV_OFFS, tm, g_dww, zbuf)
        du0 = _taps(dc_s, lambda j: dww[CONV_K - 1 - j:CONV_K - j, :], list(range(CONV_K)), tm, zbuf)
        dp_s[:, 256:512] = du0 * s
        dp_s[:, 512:768] = du0 * v * (s * (1.0 - s))
        dc_s[pl.ds(tm, HALO), :] = dc_s[0:HALO, :]

        ext_x[0:HALO, :] = ph_ref[:, 1280:1792] * hm
        ext_x[pl.ds(HALO, tm), :] = p_ref[:, 1280:1792]
        xc = xc_ref[...]
        xcb = xc.astype(BF16)
        r, ig, a, m = (gates_ref[:, j * D_RNN:(j + 1) * D_RNN] for j in range(4))
        sp = _softplus_neg(lam[...])
        ext_h[0:HALO, :] = hsh_ref[...] * hm
        ext_h[pl.ds(HALO, tm), :] = hs_ref[...]
        dyc = dy_cols(512, 1024)
        gl, dgl = _gelu_and_grad(p_ref[:, 768:1280])
        dp_s[:, 768:1280] = dyc * hs_ref[...] * dgl
        a_s[...] = a
        b_s[...] = dyc * gl
        _scan_rows(a_s, b_s, g_s, gcar, tm, reverse=True)
        g = g_s[...]
        da = g * ext_h[pl.ds(HALO - 1, tm), :]
        dm = g * (ig * xc)
        dig = g * (m * xc)
        dlog_a = da * a - dm * (a * a) / m
        g_lam[...] += _colsum(dlog_a * (-RG_C * r)) * (-_sig(-lam[...]))
        dra = (dlog_a * (-RG_C * sp)) * (r * (1.0 - r))
        dia = dig * (ig * (1.0 - ig))
        g_ba[...] += _colsum(dra)
        g_bx[...] += _colsum(dia)
        dra_b = dra.astype(BF16)
        dia_b = dia.astype(BF16)
        dxc = g * (m * ig) + _dot_nt(dra_b, wa[...]) + _dot_nt(dia_b, wx[...])
        g_wa[...] += _dot_tn(xcb, dra_b)
        g_wx[...] += _dot_tn(xcb, dia_b)
        g_cb[...] += _colsum(dxc)
        dx_s[0:tm, :] = dxc
        for k in range(RG_CONV_K):
            g_cw[k:k + 1, :] += _colsum(dxc * ext_x[pl.ds(HALO - (RG_CONV_K - 1) + k, tm), :])
        dxin = cw[RG_CONV_K - 1:RG_CONV_K, :] * dxc
        for j in range(1, RG_CONV_K):
            dxin = dxin + cw[RG_CONV_K - 1 - j:RG_CONV_K - j, :] * dx_s[pl.ds(j, tm), :]
        dp_s[:, 1280:1792] = dxin
        dx_s[pl.ds(tm, HALO), :] = dx_s[0:HALO, :]

        dpb = dp_s[...].astype(BF16)
        dp_ref[...] = dpb
        du = _dot(dpb, win_ref[...])
        r, n = _rms(h0_ref[...])
        g_g1[...] += _colsum(du * n)
        dh0 = dh1_ref[...] + _rms_bwd(du, n, r, g1_ref[...])
        if lead:
            dh0_ref[0:tm - lead, :] = dh0[lead:tm, :]
            dh0_ref[tm - lead:tm, :] = carry[...]
            carry[...] = dh0[0:lead, :]

            @pl.when(i == 0)
            def _():
                dlead_ref[...] = dh0[0:lead, :]
        else:
            dh0_ref[...] = dh0

    ws = [mw[k] for k in _MIX_W]
    tile = lambda w: pl.BlockSpec((tm, w), lambda s: (nt - 1 - s, 0))
    halo = lambda w: pl.BlockSpec((HALO, w), lambda s: (jnp.maximum((nt - 1 - s) * hb - 1, 0), 0))
    lead_out = [pl.BlockSpec((lead, D_MODEL), lambda s: (0, 0))] if lead else []
    outs = pl.pallas_call(
        _after(body, 12 + len(ws), deps), name="mixer_bwd", grid=(nt,),
        in_specs=[tile(D_IN), halo(D_IN), tile(D_MODEL), tile(D_RNN), halo(D_RNN), tile(D_CONV), tile(D_RNN), tile(4 * D_RNN),
                  tile(D_MODEL), _full(g1),
                  _resident(w_out), _resident(w_in)] + [_full(w) for w in ws] + [ANY] * len(deps),
        out_specs=[tile(D_IN), tile(D_MODEL)] + [pl.BlockSpec(shp, lambda s, nd=len(shp): (0,) * nd) for _, shp in _MIX_G]
        + lead_out,
        out_shape=[jax.ShapeDtypeStruct((t, D_IN), BF16), jax.ShapeDtypeStruct((t_out, D_MODEL), F32)]
        + [jax.ShapeDtypeStruct(shp, F32) for _, shp in _MIX_G] + [jax.ShapeDtypeStruct((lead, D_MODEL), F32)] * bool(lead),
        scratch_shapes=[pltpu.VMEM((HALO + tm, D_POOL), F32), pltpu.VMEM((HALO + tm, D_CONV), F32),
                        pltpu.VMEM((HALO + tm, D_RNN), F32), pltpu.VMEM((HALO + tm, D_RNN), F32),
                        pltpu.VMEM((tm + HALO, D_POOL), F32), pltpu.VMEM((tm + HALO, D_CONV), F32),
                        pltpu.VMEM((tm + HALO, D_RNN), F32), pltpu.VMEM((HALO + tm, D_POOL), F32),
                        pltpu.VMEM((HALO + tm, D_POOL), F32), pltpu.VMEM((tm + 8, D_CONV), F32),
                        pltpu.VMEM((tm + 16, D_CONV), F32), pltpu.VMEM((tm, D_RNN), F32),
                        pltpu.VMEM((tm, D_RNN), F32), pltpu.VMEM((tm, D_RNN), F32), pltpu.VMEM((8, D_RNN), F32),
                        pltpu.VMEM((tm, D_MODEL), BF16), pltpu.VMEM((tm, D_IN), F32)]
        + [pltpu.VMEM((lead, D_MODEL), F32)] * bool(lead),
        compiler_params=_params("arbitrary"),
    )(p, p, dh1, hs, hs, conv, xc, gates, h0, g1, w_out, w_in, *ws, *deps)
    grads = {k: o for (k, _), o in zip(_MIX_G, outs[2:])}
    return (outs[0], (outs[1], outs[-1]), grads) if lead else (outs[0], outs[1], grads)


def _mid_fwd(y, h0, w_out, g, w_up):
    t = h0.shape[0]
    tm = _row_tile(t, TM_MAT)

    def body(y_ref, h0_ref, wo_ref, g_ref, wu_ref, h1_ref, u2_ref, f_ref):
        h1 = h0_ref[...] + _dot(y_ref[...], wo_ref[...])
        h1_ref[...] = h1
        u2 = (_rms(h1)[1] * g_ref[...]).astype(BF16)
        u2_ref[...] = u2
        for c in range(D_FF // FF_CHUNK):
            f_ref[:, c * FF_CHUNK:(c + 1) * FF_CHUNK] = _dot(u2, wu_ref[c]).astype(BF16)

    row = lambda w: pl.BlockSpec((tm, w), lambda i: (i, 0))
    return pl.pallas_call(
        body, name="mid_fwd", grid=(t // tm,),
        in_specs=[row(D_MODEL), row(D_MODEL), _resident(w_out), _full(g), _resident(w_up)],
        out_specs=[row(D_MODEL), row(D_MODEL), row(D_FF)],
        out_shape=[jax.ShapeDtypeStruct((t, D_MODEL), F32), jax.ShapeDtypeStruct((t, D_MODEL), BF16),
                   jax.ShapeDtypeStruct((t, D_FF), BF16)],
        compiler_params=_params("parallel"),
    )(y, h0, w_out, g, w_up)


def _down_proj(f_ref, h1_ref, wd_ref):
    acc = h1_ref[...]
    for c in range(D_FF // FF_CHUNK):
        cols = slice(c * FF_CHUNK, (c + 1) * FF_CHUNK)
        a = jnp.square(jnp.maximum(f_ref[:, cols].astype(F32), 0.0)).astype(BF16)
        acc = acc + _dot(a, wd_ref[cols, :])
    return acc


def _down_fwd(f, h1, w_down):
    t = h1.shape[0]
    tm = _row_tile(t, TM_MAT)

    def body(f_ref, h1_ref, wd_ref, h2_ref):
        h2_ref[...] = _down_proj(f_ref, h1_ref, wd_ref)

    row = lambda w: pl.BlockSpec((tm, w), lambda i: (i, 0))
    return pl.pallas_call(
        body, name="down_fwd", grid=(t // tm,),
        in_specs=[row(D_FF), row(D_MODEL), _resident(w_down)], out_specs=row(D_MODEL),
        out_shape=jax.ShapeDtypeStruct((t, D_MODEL), F32),
        compiler_params=_params("parallel"),
    )(f, h1, w_down)


def _down_fwd_loss(f, h1, w_down, g, tgt, t_real):
    t = h1.shape[0]
    tm = _row_tile(t, TM_MAT)

    def body(f_ref, h1_ref, wd_ref, g_ref, tgt_ref, loss_ref, dh_ref, dg_ref):
        i = pl.program_id(0)

        @pl.when(i == 0)
        def _():
            loss_ref[...] = jnp.zeros(loss_ref.shape, F32)
            dg_ref[...] = jnp.zeros(dg_ref.shape, F32)

        r, n = _rms(_down_proj(f_ref, h1_ref, wd_ref))
        row = lax.broadcasted_iota(jnp.int32, (tm, 1), 0) + i * tm
        valid = jnp.logical_and(row >= N_META, row < t_real)
        diff = jnp.where(valid, n * g_ref[...] - tgt_ref[...], 0.0)
        loss_ref[...] += 0.5 * jnp.sum(jnp.mean(diff * diff, axis=-1, keepdims=True))
        dy = diff * (1.0 / D_MODEL)
        dg_ref[...] += _colsum(dy * n)
        dh_ref[...] = _rms_bwd(dy, n, r, g_ref[...])

    row = lambda w: pl.BlockSpec((tm, w), lambda i: (i, 0))
    return pl.pallas_call(
        body, name="down_fwd_loss", grid=(t // tm,),
        in_specs=[row(D_FF), row(D_MODEL), _resident(w_down), _full(g), row(D_MODEL)],
        out_specs=[pl.BlockSpec((8, 128), lambda i: (0, 0)), row(D_MODEL), pl.BlockSpec((1, D_MODEL), lambda i: (0, 0))],
        out_shape=[jax.ShapeDtypeStruct((8, 128), F32), jax.ShapeDtypeStruct((t, D_MODEL), F32),
                   jax.ShapeDtypeStruct((1, D_MODEL), F32)],
        compiler_params=_params("arbitrary"),
    )(f, h1, w_down, g, tgt)


def _mlp_bwd(dh2, f, h1, g, w_up, w_down, deps=()):
    t = dh2.shape[0]
    tm = _row_tile(t, TM_MLP_BWD)

    def body(dh2_ref, f_ref, wd_ref, wu_ref, h1_ref, g_ref, df_ref, dh1_ref, dg_ref):
        @pl.when(pl.program_id(0) == 0)
        def _():
            dg_ref[...] = jnp.zeros(dg_ref.shape, F32)

        dh2 = dh2_ref[...]
        dhb = dh2.astype(BF16)
        du2 = None
        for c in range(D_FF // FF_CHUNK):
            cols = slice(c * FF_CHUNK, (c + 1) * FF_CHUNK)
            dact = _dot_nt(dhb, wd_ref[c])
            df = (dact * (2.0 * jnp.maximum(f_ref[:, cols].astype(F32), 0.0))).astype(BF16)
            df_ref[:, cols] = df
            part = _dot_nt(df, wu_ref[c])
            du2 = part if du2 is None else du2 + part
        r, n = _rms(h1_ref[...])
        dg_ref[...] += _colsum(du2 * n)
        dh1_ref[...] = dh2 + _rms_bwd(du2, n, r, g_ref[...])

    row = lambda w: pl.BlockSpec((tm, w), lambda i: (i, 0))
    return pl.pallas_call(
        _after(body, 6, deps), name="mlp_bwd", grid=(t // tm,),
        in_specs=[row(D_MODEL), row(D_FF), _resident(w_down), _resident(w_up), row(D_MODEL), _full(g)] + [ANY] * len(deps),
        out_specs=[row(D_FF), row(D_MODEL), pl.BlockSpec((1, D_MODEL), lambda i: (0, 0))],
        out_shape=[jax.ShapeDtypeStruct((t, D_FF), BF16), jax.ShapeDtypeStruct((t, D_MODEL), F32),
                   jax.ShapeDtypeStruct((1, D_MODEL), F32)],
        compiler_params=_params("arbitrary"),
    )(dh2, f, w_down, w_up, h1, g, *deps)


def _tn_matmul(a, b, kc, nc, relu2, name, deps=()):
    t, k = a.shape
    n = b.shape[1]
    tt = _row_tile(t, TM_MAT)
    gk, gn = k // kc, n // nc

    def body(a_ref, b_ref, o_ref):
        @pl.when(pl.program_id(2) == 0)
        def _():
            o_ref[...] = jnp.zeros(o_ref.shape, F32)

        av = a_ref[...]
        if relu2:
            av = jnp.square(jnp.maximum(av.astype(F32), 0.0))
        o_ref[...] += _dot_tn(av.astype(BF16), b_ref[...].astype(BF16))

    return pl.pallas_call(
        _after(body, 2, deps), name=name, grid=(gk, gn, t // tt),
        in_specs=[pl.BlockSpec((tt, kc), lambda ik, jn, it: (it, ik)), pl.BlockSpec((tt, nc), lambda ik, jn, it: (it, jn))]
        + [ANY] * len(deps),
        out_specs=pl.BlockSpec((None, kc, nc), lambda ik, jn, it: (ik * gn + jn, 0, 0)),
        out_shape=jax.ShapeDtypeStruct((gk * gn, kc, nc), F32),
        compiler_params=_params("parallel", "parallel", "arbitrary"),
    )(a, b, *deps)


def _block_diag(blocks):
    nb, hd, _ = blocks.shape
    eye = jnp.eye(nb, dtype=blocks.dtype)
    return (blocks[:, :, None, :] * eye[:, None, :, None]).reshape(nb * hd, nb * hd)


def _diag_blocks(m, nb):
    hd = m.shape[0] // nb
    eye = jnp.eye(nb, dtype=m.dtype)
    return jnp.sum(m.reshape(nb, hd, nb, hd) * eye[:, None, :, None], axis=2)


def _mixer_weights(w, l):
    row = lambda a: a.reshape(1, -1)
    return dict(
        wp=_block_diag(w["pool_w"][l]).astype(BF16), psc=row(w["pool_scale"][l]),
        dww=jnp.pad(w["convb_dw_w"][l], ((0, 32 - CONV_K), (0, 0))), dwb=row(w["convb_dw_b"][l]),
        lng=row(w["convb_ln_g"][l]), lnb=row(w["convb_ln_b"][l]), wpw=w["convb_pw_w"][l].astype(BF16),
        cw=jnp.pad(w["rg_conv_w"][l], ((0, 8 - RG_CONV_K), (0, 0))), cb=row(w["rg_conv_b"][l]),
        wa=_block_diag(w["rg_w_a"][l]).astype(BF16), ba=row(w["rg_b_a"][l]),
        wx=_block_diag(w["rg_w_x"][l]).astype(BF16), bx=row(w["rg_b_x"][l]), lam=row(w["rg_lambda"][l]))


def _local_step(h, tgt, t_real, w, fetch, hooks):
    depth = 2
    saved = []
    big = []
    for l in range(depth):
        mw = _mixer_weights(w, l)
        g1 = w["mix_norm_g"][l].reshape(1, -1)
        g2 = w["mlp_norm_g"][l].reshape(1, -1)
        wl = dict(w_in=fetch(l, "w_in", h))
        y, p, u, hs, conv, xc, gates = _mixer_fwd(h, g1, wl["w_in"], mw)
        wl["w_out"], wl["w_up"] = fetch(l, "w_out", y), fetch(l, "w_up", y)
        h1, u2, f = _mid_fwd(y, h, wl["w_out"], g2, wl["w_up"])
        wl["w_down"] = fetch(l, "w_down", f)
        if l == depth - 1:
            loss, dh, dgf = _down_fwd_loss(f, h1, wl["w_down"].reshape(D_FF, D_MODEL), w["final_norm_g"].reshape(1, -1), tgt,
                                           t_real)
            h2 = None
        else:
            h2 = _down_fwd(f, h1, wl["w_down"].reshape(D_FF, D_MODEL))
        saved.append(dict(mw=mw, g1=g1, g2=g2, h0=h, p=p, u=u, y=y, hs=hs, conv=conv, xc=xc, gates=gates, h1=h1, u2=u2, f=f))
        big.append(wl)
        h = h2

    gs = {k: [None] * depth for k in ("mix_norm_g", "mlp_norm_g", "pool_w", "pool_scale", "convb_dw_w", "convb_dw_b",
                                      "convb_ln_g", "convb_ln_b", "convb_pw_w", "rg_conv_w", "rg_conv_b", "rg_w_a",
                                      "rg_b_a", "rg_w_x", "rg_b_x", "rg_lambda")}
    deps = ()
    for l in reversed(range(depth)):
        s, wl = saved[l], big[l]
        df, dh1, dg2 = _mlp_bwd(dh, s["f"], s["h1"], s["g2"], wl["w_up"], wl["w_down"], deps)
        deps = hooks.point(l, "mlp_bwd", dh1)
        g_down = _tn_matmul(s["f"], dh, FF_CHUNK, D_MODEL, True, "dw_down", deps)
        hooks.grad(l, "w_down", g_down)
        deps = hooks.point(l, "dw_down", g_down)
        g_up = _tn_matmul(s["u2"], df, D_MODEL, FF_CHUNK, False, "dw_up", deps)
        hooks.grad(l, "w_up", g_up)
        deps = hooks.point(l, "dw_up", g_up)
        g_out = _tn_matmul(s["y"], dh1, D_MODEL, D_MODEL, False, "dw_out", deps)
        hooks.grad(l, "w_out", g_out.reshape(N_CHIPS, D_MODEL // N_CHIPS, D_MODEL))
        deps = hooks.point(l, "dw_out", g_out)
        dp, dh, mg = _mixer_bwd(s["p"], dh1, s["hs"], s["conv"], s["xc"], s["gates"], s["h0"], s["g1"], wl["w_out"],
                                wl["w_in"], s["mw"], deps, lead=0 if l else N_META, t_real=t_real)
        if l == 0:
            dh, dmeta = dh
        gs["mix_norm_g"][l] = mg["g1"][0]
        gs["mlp_norm_g"][l] = dg2[0]
        gs["pool_w"][l] = _diag_blocks(mg["wp"], D_POOL // POOL_GW)
        gs["pool_scale"][l] = mg["psc"][0]
        gs["convb_dw_w"][l] = jnp.sum(mg["dww"][:CONV_K], axis=1)
        gs["convb_dw_b"][l] = mg["dwb"][0]
        gs["convb_ln_g"][l] = mg["lng"][0]
        gs["convb_ln_b"][l] = mg["lnb"][0]
        gs["convb_pw_w"][l] = mg["wpw"]
        gs["rg_conv_w"][l] = mg["cw"][:RG_CONV_K]
        gs["rg_conv_b"][l] = mg["cb"][0]
        gs["rg_w_a"][l] = _diag_blocks(mg["wa"], D_RNN // RG_HD)
        gs["rg_b_a"][l] = mg["ba"][0]
        gs["rg_w_x"][l] = _diag_blocks(mg["wx"], D_RNN // RG_HD)
        gs["rg_b_x"][l] = mg["bx"][0]
        gs["rg_lambda"][l] = mg["lam"][0]
        if l == 0:
            gsmall = {k: jnp.stack(v) for k, v in gs.items()}
            gsmall["final_norm_g"] = dgf[0]
            gsmall["meta_tokens"] = dmeta
            started = hooks.small(gsmall)
        deps = hooks.point(l, "mixer_bwd", started[0] if l == 0 and started else dh)
        g_in = _tn_matmul(dp, s["u"], D_IN, D_MODEL, False, "dw_in", deps).reshape(N_CHIPS, D_IN // N_CHIPS, D_MODEL)
        hooks.grad(l, "w_in", g_in)
        deps = hooks.point(l, "dw_in", g_in)
    return loss[0, 0], dh


def _place():
    return lax.axis_index("x"), lax.axis_index("y"), lax.axis_index("c")


def _other_chips(x, y):
    return [(1 - x, y), (x, 1 - y), (1 - x, 1 - y)]


HBM_SPEC = pl.BlockSpec(memory_space=pltpu.HBM)
SEM_SPEC = pl.BlockSpec(memory_space=pltpu.SEMAPHORE)
DATAFLOW = pltpu.SideEffectType.DATAFLOW_SIDE_EFFECTING


def _gather_copies(src_refs, land_refs, send_sem, recv_sem, first):
    x, y, c = _place()
    me = 2 * x + y
    out = []
    for n in range(len(src_refs)):
        for j, (px, py) in enumerate(_other_chips(x, y)):
            out.append(pltpu.make_async_remote_copy(src_refs[n], land_refs[n].at[me], send_sem.at[first + 3 * n + j],
                                                    recv_sem.at[first + 3 * n + j], device_id=(px, py, c), device_id_type=MESH))
    return out


def _gather_start(groups, me):
    srcs = [pltpu.with_memory_space_constraint(s, pltpu.HBM) for g in groups for s in g]
    lands = [pltpu.with_memory_space_constraint(
        lax.dynamic_update_slice(jnp.zeros((N_CHIPS,) + s.shape, s.dtype), s[None], (me,) + (0,) * s.ndim), pltpu.HBM)
        for g in groups for s in g]
    n, ng = len(srcs), len(groups)
    first = [sum(len(g) for g in groups[:i]) for i in range(ng)]

    def body(*refs):
        src_refs, land_refs = refs[:n], refs[n:2 * n]
        sems = refs[2 * n:2 * n + 2 * ng]
        token = refs[-1]
        for gi, g in enumerate(groups):
            lo, hi = first[gi], first[gi] + len(g)
            for cp in _gather_copies(src_refs[lo:hi], land_refs[lo:hi], sems[2 * gi], sems[2 * gi + 1], 0):
                cp.start()
        token[...] = jnp.zeros(token.shape, token.dtype)

    sem_shapes = [pltpu.SemaphoreType.DMA((3 * len(g),)) for g in groups for _ in range(2)]
    outs = pl.pallas_call(
        body, name="gather_start",
        out_shape=sem_shapes + [pltpu.HBM(a.shape, a.dtype) for a in srcs + lands] + [jax.ShapeDtypeStruct((8, 128), F32)],
        in_specs=[HBM_SPEC] * (2 * n),
        out_specs=[SEM_SPEC] * (2 * ng) + [HBM_SPEC] * (2 * n) + [pl.BlockSpec(memory_space=pltpu.VMEM)],
        input_output_aliases={i: 2 * ng + i for i in range(2 * n)},
        compiler_params=pltpu.CompilerParams(has_side_effects=DATAFLOW),
    )(*srcs, *lands)
    sems, thru, token = outs[:2 * ng], outs[2 * ng:2 * ng + 2 * n], outs[-1]
    state = []
    for gi, g in enumerate(groups):
        lo, hi = first[gi], first[gi] + len(g)
        state.append((sems[2 * gi], sems[2 * gi + 1], thru[lo:hi], thru[n + lo:n + hi]))
    return state, token


def _gather_wait(state, after, name):
    send_sem, recv_sem, srcs, lands = state
    n = len(srcs)

    def body(*refs):
        src_refs, land_refs = refs[:n], refs[n:2 * n]
        send, recv = refs[2 * n], refs[2 * n + 1]
        for cp in _gather_copies(src_refs, land_refs, send, recv, 0):
            cp.wait_send()
            cp.wait_recv()

    outs = pl.pallas_call(
        body, name=name,
        out_shape=[pltpu.HBM(a.shape, a.dtype) for a in list(srcs) + list(lands)],
        in_specs=[HBM_SPEC] * (2 * n) + [SEM_SPEC, SEM_SPEC, ANY],
        out_specs=[HBM_SPEC] * (2 * n),
        input_output_aliases={i: i for i in range(2 * n)},
        compiler_params=pltpu.CompilerParams(has_side_effects=DATAFLOW),
    )(*srcs, *lands, send_sem, recv_sem, after)
    return outs[n:]


def _add_halves(g, recv, c1):
    nk, r, cd = g.shape
    r2 = r // 2
    rc = r2 // ROW_CHUNKS

    def body(c_ref, g_ref, r_ref, pab_ref):
        pab_ref[...] = (g_ref[...] + r_ref[...]).astype(BF16)

    blk = pl.BlockSpec((None, rc, cd), lambda k, j, c_ref: (k, j, 0))
    return pl.pallas_call(
        body, name="rs_add_halves",
        grid_spec=pltpu.PrefetchScalarGridSpec(
            num_scalar_prefetch=1, grid=(nk, ROW_CHUNKS),
            in_specs=[pl.BlockSpec((None, rc, cd), lambda k, j, c_ref: (k, c_ref[0] * ROW_CHUNKS + j, 0)), blk], out_specs=blk),
        out_shape=jax.ShapeDtypeStruct((nk, r2, cd), BF16),
        compiler_params=_params("parallel", "parallel"),
    )(c1, g, recv)


def _sum_partials(g, recv_sibling, recv_chips, c_me):
    nk, r, cd = g.shape
    r2 = r // 2
    rc = r2 // ROW_CHUNKS

    def body(cm_ref, g_ref, a_ref, r_ref, s_ref):
        own = g_ref[...] + a_ref[...]
        s_ref[...] = ((own + r_ref[0].astype(F32)) + r_ref[1].astype(F32)) + r_ref[2].astype(F32)

    return pl.pallas_call(
        body, name="rs_sum_partials",
        grid_spec=pltpu.PrefetchScalarGridSpec(
            num_scalar_prefetch=1, grid=(ROW_CHUNKS,),
            in_specs=[pl.BlockSpec((None, rc, cd), lambda j, cm: (cm[1], cm[0] * ROW_CHUNKS + j, 0)),
                      pl.BlockSpec((None, rc, cd), lambda j, cm: (cm[1], j, 0)),
                      pl.BlockSpec((3, rc, cd), lambda j, cm: (0, j, 0))],
            out_specs=pl.BlockSpec((rc, cd), lambda j, cm: (j, 0))),
        out_shape=jax.ShapeDtypeStruct((r2, cd), F32),
        compiler_params=_params("parallel"),
    )(c_me, g, recv_sibling, recv_chips)


def _split_start(name, srcs, lands, ncopies, make_copies):
    srcs = [pltpu.with_memory_space_constraint(s, pltpu.HBM) for s in srcs]
    lands = [pltpu.with_memory_space_constraint(a, pltpu.HBM) for a in lands]
    n, m = len(srcs), len(lands)

    def body(*refs):
        src_refs, land_refs = refs[:n], refs[n:n + m]
        send, recv, token = refs[n + m], refs[n + m + 1], refs[-1]
        for cp in make_copies(src_refs, land_refs, send, recv):
            cp.start()
        token[...] = jnp.zeros(token.shape, token.dtype)

    outs = pl.pallas_call(
        body, name=name,
        out_shape=[pltpu.SemaphoreType.DMA((ncopies,)), pltpu.SemaphoreType.DMA((ncopies,))]
        + [pltpu.HBM(a.shape, a.dtype) for a in srcs + lands] + [jax.ShapeDtypeStruct((8, 128), F32)],
        in_specs=[HBM_SPEC] * (n + m),
        out_specs=[SEM_SPEC, SEM_SPEC] + [HBM_SPEC] * (n + m) + [pl.BlockSpec(memory_space=pltpu.VMEM)],
        input_output_aliases={i: 2 + i for i in range(n + m)},
        compiler_params=pltpu.CompilerParams(has_side_effects=DATAFLOW),
    )(*srcs, *lands)
    return (outs[0], outs[1], outs[2:2 + n], outs[2 + n:2 + n + m], make_copies), outs[-1]


def _split_wait(name, state, after):
    send_sem, recv_sem, srcs, lands, make_copies = state
    n, m = len(srcs), len(lands)

    def body(*refs):
        src_refs, land_refs = refs[:n], refs[n:n + m]
        for cp in make_copies(src_refs, land_refs, refs[n + m], refs[n + m + 1]):
            cp.wait_send()
            cp.wait_recv()

    outs = pl.pallas_call(
        body, name=name,
        out_shape=[pltpu.HBM(a.shape, a.dtype) for a in list(srcs) + list(lands)],
        in_specs=[HBM_SPEC] * (n + m) + [SEM_SPEC, SEM_SPEC, ANY],
        out_specs=[HBM_SPEC] * (n + m),
        input_output_aliases={i: i for i in range(n + m)},
        compiler_params=pltpu.CompilerParams(has_side_effects=DATAFLOW),
    )(*srcs, *lands, send_sem, recv_sem, after)
    return outs[:n], outs[n:]


def _copies_to_sibling(src_of):
    def make(src_refs, land_refs, send, recv):
        x, y, c = _place()
        return [pltpu.make_async_remote_copy(src_of(src_refs[i], c), land_refs[i], send.at[i], recv.at[i],
                                             device_id=(x, y, 1 - c), device_id_type=MESH) for i in range(len(src_refs))]
    return make


def _copies_to_chips(src_refs, land_refs, send, recv):
    x, y, c = _place()
    return [pltpu.make_async_remote_copy(src_refs[i].at[2 * px + py], land_refs[i].at[j], send.at[3 * i + j], recv.at[3 * i + j],
                                         device_id=(px, py, c), device_id_type=MESH)
            for i in range(len(src_refs)) for j, (px, py) in enumerate(_other_chips(x, y))]


def _other_half_rows(ref, c):
    r2 = ref.shape[1] // 2
    return ref.at[:, pl.ds(pl.multiple_of((1 - c) * r2, 8), r2)]


class _ReduceScatter:
    def __init__(self, tag, grads, c1, me1):
        self.tag, self.grads, self.c1, self.me1 = tag, grads, c1, me1

    def start(self):
        lands = [lax.empty((g.shape[0], g.shape[1] // 2, g.shape[2]), F32) for g in self.grads]
        self.state, token = _split_start("rs_%s_a_start" % self.tag, self.grads, lands, len(self.grads),
                                         _copies_to_sibling(_other_half_rows))
        return token

    def to_chips(self, after):
        self.halves = _split_wait("rs_%s_a_wait" % self.tag, self.state, after)
        pabs = [_add_halves(g, r, self.c1) for g, r in zip(*self.halves)]
        lands = [lax.empty((3,) + p.shape[1:], BF16) for p in pabs]
        self.state, token = _split_start("rs_%s_b_start" % self.tag, pabs, lands, 3 * len(pabs), _copies_to_chips)
        return token

    def to_sibling(self, after):
        _, recv = _split_wait("rs_%s_b_wait" % self.tag, self.state, after)
        c_me = jnp.concatenate([self.c1, self.me1])
        sums = [_sum_partials(g, ra, rb, c_me) for g, ra, rb in zip(*self.halves, recv)]
        lands = [lax.empty(s.shape, F32) for s in sums]
        self.state, token = _split_start("rs_%s_c_start" % self.tag, sums, lands, len(sums),
                                         _copies_to_sibling(lambda ref, c: ref))
        return token

    def finish(self, after):
        return list(zip(*_split_wait("rs_%s_c_wait" % self.tag, self.state, after)))


def _add_lists(a_list, b_list):
    n = len(a_list)

    def body(*refs):
        for i in range(n):
            refs[2 * n + i][...] = refs[i][...] + refs[n + i][...]

    vm = pl.BlockSpec(memory_space=pltpu.VMEM)
    return pl.pallas_call(
        body, name="add_lists", in_specs=[vm] * (2 * n), out_specs=[vm] * n,
        out_shape=[jax.ShapeDtypeStruct(a.shape, a.dtype) for a in a_list],
        compiler_params=pltpu.CompilerParams(vmem_limit_bytes=VMEM_LIMIT),
    )(*a_list, *b_list)


def _copies_to_peer(stage):
    def make(src_refs, land_refs, send, recv):
        x, y, c = _place()
        peer = [(x, y, 1 - c), (1 - x, y, c), (x, 1 - y, c)][stage]
        return [pltpu.make_async_remote_copy(src_refs[i], land_refs[i], send.at[i], recv.at[i], device_id=peer, device_id_type=MESH)
                for i in range(len(src_refs))]
    return make


class _AllReduceSmall:
    def __init__(self, vs):
        self.vs, self.stage = list(vs), 0

    def _start(self):
        lands = [lax.empty(v.shape, v.dtype) for v in self.vs]
        self.state, token = _split_start("ar_small_start_%d" % self.stage, self.vs, lands, len(self.vs), _copies_to_peer(self.stage))
        return token

    def start(self):
        return self._start()

    def step(self, after):
        mine, theirs = _split_wait("ar_small_wait_%d" % self.stage, self.state, after)
        self.vs = _add_lists(mine, theirs)
        self.stage += 1
        return self._start() if self.stage < 3 else self.vs[0]


def _adamw_math(w, g, m, v):
    m = ADAM_B1 * m + (1.0 - ADAM_B1) * g
    v = ADAM_B2 * v + (1.0 - ADAM_B2) * jnp.square(g)
    m_hat = m / (1.0 - ADAM_B1 ** ADAM_STEP)
    v_hat = v / (1.0 - ADAM_B2 ** ADAM_STEP)
    return -ADAM_LR * (m_hat / (jnp.sqrt(v_hat) + ADAM_EPS) + ADAM_WD * w), m, v


def _adamw_big_layer(layer, w, m, v, own, sib, c1, prev):
    _, r, cd = w.shape
    rc = r // 2 // ROW_CHUNKS

    def body(c_ref, w_ref, m_ref, v_ref, own_ref, sib_ref, *rest):
        g_ref, d_ref, mo_ref, vo_ref, token = rest[-5:]
        g = jnp.where(pl.program_id(0) == c_ref[0], own_ref[...], sib_ref[...])
        g_ref[...] = g
        d_ref[...], mo_ref[...], vo_ref[...] = _adamw_math(w_ref[...], g, m_ref[...], v_ref[...])
        token[...] = jnp.zeros(token.shape, F32)

    blk = pl.BlockSpec((None, rc, cd), lambda hh, j, c_ref: (layer, hh * ROW_CHUNKS + j, 0))
    half = pl.BlockSpec((rc, cd), lambda hh, j, c_ref: (j, 0))
    prev = () if prev is None else tuple(prev)
    outs = pl.pallas_call(
        body, name="adamw_big",
        grid_spec=pltpu.PrefetchScalarGridSpec(
            num_scalar_prefetch=1, grid=(2, ROW_CHUNKS), in_specs=[blk, blk, blk, half, half] + [ANY] * len(prev),
            out_specs=[blk] * 4 + [pl.BlockSpec((8, 128), lambda hh, j, c_ref: (0, 0))]),
        out_shape=[jax.ShapeDtypeStruct(w.shape, F32)] * 4 + [jax.ShapeDtypeStruct((8, 128), F32)],
        input_output_aliases={6 + i: i for i in range(len(prev))},
        compiler_params=_params("arbitrary", "arbitrary"),
    )(c1, w, m, v, own, sib, *prev)
    return outs[:4], outs[4]


def _adamw_small(ws, gs, ms, vs):
    n = len(ws)

    def body(*refs):
        w_refs, g_refs, m_refs, v_refs = refs[:n], refs[n:2 * n], refs[2 * n:3 * n], refs[3 * n:4 * n]
        outs = refs[4 * n:]
        for i in range(n):
            outs[3 * i][...], outs[3 * i + 1][...], outs[3 * i + 2][...] = _adamw_math(
                w_refs[i][...], g_refs[i][...], m_refs[i][...], v_refs[i][...])

    vm = pl.BlockSpec(memory_space=pltpu.VMEM)
    outs = pl.pallas_call(
        body, name="adamw_small", in_specs=[vm] * (4 * n), out_specs=[vm] * (3 * n),
        out_shape=[jax.ShapeDtypeStruct(w.shape, F32) for w in ws for _ in range(3)],
        compiler_params=pltpu.CompilerParams(vmem_limit_bytes=VMEM_LIMIT),
    )(*ws, *gs, *ms, *vs)
    return [outs[3 * i:3 * i + 3] for i in range(n)]


LANES = 128
SUBLANES = 8
SHARDED_AXIS = {"meta_tokens": 1, "convb_dw_w": 2, "convb_pw_w": 1, "rg_conv_w": 2}


def _rows_of(size):
    return -(-size // (LANES * SUBLANES)) * SUBLANES


def _as_rows(a, rows=None):
    flat = a.reshape(-1)
    rows = _rows_of(flat.size) if rows is None else rows
    return jnp.pad(flat, (0, rows * LANES - flat.size)).reshape(rows, LANES)


class _GradientSchedule:
    GROUPS = {"l1": [(1, "w_down"), (1, "w_up"), (1, "w_out"), (1, "w_in")], "a0": [(0, "w_down"), (0, "w_up")],
              "b0": [(0, "w_out")], "c0": [(0, "w_in")]}
    PLAN = {
        (1, "dw_in"): [("l1", "start")],
        (0, "mlp_bwd"): [("l1", "to_chips")],
        (0, "dw_up"): [("l1", "to_sibling"), ("a0", "start")],
        (0, "dw_out"): [("l1", "finish"), ("a0", "to_chips"), ("b0", "start")],
        (0, "mixer_bwd"): [("a0", "to_sibling"), ("b0", "to_chips"), ("small", "step")],
        (0, "dw_in"): [("c0", "start"), ("small", "step"), ("c0", "to_chips"), ("a0", "finish"), ("b0", "to_sibling")],
    }

    def __init__(self, w, mom, var, c1, me1):
        self.w, self.mom, self.var, self.c1, self.me1 = w, mom, var, c1, me1
        self.grads, self.chains, self.out = {}, {}, {}

    def grad(self, layer, name, g):
        self.grads[layer, name] = g

    def small(self, gsmall):
        self.small_sum = _AllReduceSmall([g.reshape(1, -1) if g.ndim == 1 else g for g in (gsmall[k] for k in SMALL)])
        return (self.small_sum.start(),)

    def point(self, layer, kernel_name, after):
        return self.run(self.PLAN.get((layer, kernel_name), ()), after) or (after,)

    def run(self, actions, after):
        deps = []
        for tag, stage in actions:
            if tag == "small":
                deps.append(self.small_sum.step(after))
            elif stage == "start":
                self.chains[tag] = _ReduceScatter(tag, [self.grads[lk] for lk in self.GROUPS[tag]], self.c1, self.me1)
                deps.append(self.chains[tag].start())
            elif stage == "finish":
                for (layer, k), (own, sib) in zip(self.GROUPS[tag], self.chains[tag].finish(after)):
                    self.out[k], token = _adamw_big_layer(layer, self.w[k], self.mom[k], self.var[k], own, sib, self.c1,
                                                          self.out.get(k))
                    deps.append(token)
            else:
                deps.append(getattr(self.chains[tag], stage)(after))
            after = deps[-1]
        self.last = after
        return tuple(deps)


def _from_shard_major(name, sm):
    if name == "meta_tokens":
        return sm.transpose(1, 0, 2).reshape(N_META, -1)
    if name == "convb_pw_w":
        return sm.transpose(1, 0, 2, 3).reshape(2, -1, D_CONV)
    return sm.transpose(1, 2, 0, 3).reshape(sm.shape[1], sm.shape[2], -1)


def kernel(x, meta_tokens, mix_norm_g, w_in, pool_w, pool_scale, convb_dw_w, convb_dw_b, convb_ln_g, convb_ln_b, convb_pw_w, rg_conv_w, rg_conv_b, rg_w_a, rg_b_a, rg_w_x, rg_b_x, rg_lambda, w_out, mlp_norm_g, w_up, w_down, final_norm_g, loss_target, m_meta_tokens, m_mix_norm_g, m_w_in, m_pool_w, m_pool_scale, m_convb_dw_w, m_convb_dw_b, m_convb_ln_g, m_convb_ln_b, m_convb_pw_w, m_rg_conv_w, m_rg_conv_b, m_rg_w_a, m_rg_b_a, m_rg_w_x, m_rg_b_x, m_rg_lambda, m_w_out, m_mlp_norm_g, m_w_up, m_w_down, m_final_norm_g, v_meta_tokens, v_mix_norm_g, v_w_in, v_pool_w, v_pool_scale, v_convb_dw_w, v_convb_dw_b, v_convb_ln_g, v_convb_ln_b, v_convb_pw_w, v_rg_conv_w, v_rg_conv_b, v_rg_w_a, v_rg_b_a, v_rg_w_x, v_rg_b_x, v_rg_lambda, v_w_out, v_mlp_norm_g, v_w_up, v_w_down, v_final_norm_g):
    given = dict(locals())
    w = {k: given[k] for k in WEIGHTS}
    mom = {k: given["m_" + k] for k in WEIGHTS}
    var = {k: given["v_" + k] for k in WEIGHTS}
    xi, yi, ci = _place()
    me1 = (2 * xi + yi).astype(jnp.int32).reshape(1)
    c1 = ci.astype(jnp.int32).reshape(1)

    small_rows = [_rows_of(w[k].size) for k in SMALL_SHARDED]
    small_pack = jnp.concatenate([_as_rows(w[k]) for k in SMALL_SHARDED])
    transposed = lambda d: {**d, "w_in": d["w_in"].transpose(0, 2, 1)}
    wt, momt, vart = transposed(w), transposed(mom), transposed(var)
    shard = lambda l, k: wt[k][l].astype(BF16)
    order = [[(0, "w_in"), "small"], [(0, "w_out"), (0, "w_up")], [(0, "w_down")], [(1, "w_in")], [(1, "w_out"), (1, "w_up")],
             [(1, "w_down")]]
    state, token = _gather_start([[small_pack if lk == "small" else shard(*lk) for lk in g] for g in order], me1[0])
    landed = {}

    def fetch(l, k, after):
        gi = [i for i, g in enumerate(order) if (l, k) in g][0]
        if gi not in landed:
            landed[gi] = _gather_wait(state[gi], after, "gather_wait_%d" % gi)
        raw = landed[gi][order[gi].index((l, k))]
        if k == "w_in":
            return raw.reshape(D_IN, D_MODEL)
        return raw.reshape(D_MODEL, D_MODEL) if k == "w_out" else raw

    seq = x.shape[1]
    t_real = N_META + seq
    t_pad = -(-t_real // ROW_ALIGN) * ROW_ALIGN
    tail = jnp.zeros((t_pad - t_real, D_MODEL), F32)
    front = jnp.zeros((N_META, D_MODEL), F32)
    h = jnp.concatenate([front + token[0, 0], x[0], tail])
    tgt = jnp.concatenate([front, loss_target[0], tail])
    landed[0] = _gather_wait(state[0], h, "gather_wait_0")
    wfull = {k: (w[k] + token[0, 0] if k in ("pool_w", "rg_w_a", "rg_w_x") else w[k]) for k in WEIGHTS}
    off = 0
    for k, rows in zip(SMALL_SHARDED, small_rows):
        sm = landed[0][1][:, off:off + rows].reshape(N_CHIPS, -1)[:, :w[k].size].reshape((N_CHIPS,) + w[k].shape)
        wfull[k] = _from_shard_major(k, sm)
        off += rows
    h = lax.dynamic_update_slice(h, wfull["meta_tokens"], (0, 0))
    sched = _GradientSchedule(wt, momt, vart, c1, me1)
    loss, dh = _local_step(h, tgt, t_real, wfull, fetch, sched)
    grad_x = dh[None]

    names = SMALL
    two_d = lambda a: a.reshape(1, -1) if a.ndim == 1 else a
    sched.run([("small", "step")], sched.last)
    summed = dict(zip(names, sched.small_sum.vs))
    for k in SMALL_SHARDED:
        ax = SHARDED_AXIS[k]
        summed[k] = lax.dynamic_slice_in_dim(summed[k], me1[0] * w[k].shape[ax], w[k].shape[ax], axis=ax)

    out = {}
    res = _adamw_small([two_d(w[k]) for k in names], [summed[k] for k in names], [two_d(mom[k]) for k in names],
                       [two_d(var[k]) for k in names])
    for k, (d, m2, v2) in zip(names, res):
        out[k] = tuple(o.reshape(w[k].shape) for o in (summed[k], d, m2, v2))
    sched.run([("b0", "finish"), ("c0", "to_sibling"), ("c0", "finish")], res[0][0])
    out.update(sched.out)
    out["w_in"] = tuple(o.transpose(0, 2, 1) for o in out["w_in"])

    loss = lax.psum(loss, ("x", "y", "c"))
    return (loss, grad_x, *[out[k][0] for k in WEIGHTS], *[out[k][1] for k in WEIGHTS],
            *[out[k][2] for k in WEIGHTS], *[out[k][3] for k in WEIGHTS])
```

```python
import functools

import jax
import jax.numpy as jnp
from jax import lax
from jax.experimental import pallas as pl
from jax.experimental.pallas import tpu as pltpu

F32, BF16 = jnp.float32, jnp.bfloat16
MESH = pl.DeviceIdType.MESH
ANY = pl.BlockSpec(memory_space=pl.ANY)

D_MODEL = 1024
N_META = 16
D_POOL = 256
D_CONV = 256
D_RNN = 512
D_IN = D_POOL + 2 * D_CONV + 2 * D_RNN
D_FF = 4096
FF_CHUNK = 1024
POOL_GW = 64
CONV_K = 31
RG_CONV_K = 4
RG_HD = 64
RG_C = 8.0
EPS = 1e-6
ADAM_LR, ADAM_B1, ADAM_B2, ADAM_EPS, ADAM_WD, ADAM_STEP = 0.001, 0.9, 0.999, 1e-08, 0.01, 10

HALO = 32
ROW_ALIGN = 256
TM_MIX = 384
TM_MAT = 768
TM_MLP_BWD = 384
N_CHIPS = 4
ROW_CHUNKS = 1
VMEM_LIMIT = 56 * 1024 * 1024

BIG = ("w_in", "w_out", "w_up", "w_down")
SMALL_SHARDED = ("meta_tokens", "convb_dw_w", "convb_pw_w", "rg_conv_w")
SMALL_REPL = ("mix_norm_g", "pool_w", "pool_scale", "convb_dw_b", "convb_ln_g", "convb_ln_b", "rg_conv_b",
              "rg_w_a", "rg_b_a", "rg_w_x", "rg_b_x", "rg_lambda", "mlp_norm_g", "final_norm_g")
SMALL = SMALL_REPL + SMALL_SHARDED
WEIGHTS = ("meta_tokens", "mix_norm_g", "w_in", "pool_w", "pool_scale", "convb_dw_w", "convb_dw_b", "convb_ln_g",
           "convb_ln_b", "convb_pw_w", "rg_conv_w", "rg_conv_b", "rg_w_a", "rg_b_a", "rg_w_x", "rg_b_x",
           "rg_lambda", "w_out", "mlp_norm_g", "w_up", "w_down", "final_norm_g")


def _params(*sem):
    return pltpu.CompilerParams(dimension_semantics=sem, vmem_limit_bytes=VMEM_LIMIT)


def _row_tile(t, cap):
    best = None
    for tm in range(128, cap + 1, 128):
        if t % tm == 0:
            best = tm
    assert best is not None, (t, cap)
    return best


def _dot(a, b):
    return jnp.dot(a, b, preferred_element_type=F32)


def _dot_nt(a, b):
    return lax.dot_general(a, b, (((1,), (1,)), ((), ())), preferred_element_type=F32)


def _dot_tn(a, b):
    return lax.dot_general(a, b, (((0,), (0,)), ((), ())), preferred_element_type=F32)


def _rms(x):
    r = lax.rsqrt(jnp.mean(x * x, axis=-1, keepdims=True) + EPS)
    return r, x * r


def _rms_bwd(du, n, r, g):
    dn = du * g
    return r * (dn - n * jnp.mean(dn * n, axis=-1, keepdims=True))


def _sig(x):
    return jax.nn.sigmoid(x)


def _colsum(x):
    return jnp.sum(x, axis=0, keepdims=True)


def _one_minus_sq(a, log_a):
    x = 2.0 * log_a
    series = -x * (1.0 + x * (0.5 + x * (1.0 / 6)))
    return jnp.where(x > -0.01, series, 1.0 - a * a)


_GELU_K0 = 0.7978845608028654
_GELU_K1 = 0.044715


def _gelu_and_grad(x):
    th = jnp.tanh(_GELU_K0 * (x + _GELU_K1 * x * x * x))
    val = 0.5 * x * (1.0 + th)
    grad = 0.5 * (1.0 + th) + 0.5 * x * (1.0 - th * th) * _GELU_K0 * (1.0 + 3.0 * _GELU_K1 * x * x)
    return val, grad


def _full(a):
    nd = a.ndim
    return pl.BlockSpec(a.shape, lambda *_: (0,) * nd)


def _resident(a):
    nd = a.ndim
    return pl.BlockSpec(a.shape, lambda *_: (0,) * nd, pipeline_mode=pl.Buffered(1))


def _after(body, n_in, deps):
    def wrapped(*refs):
        return body(*refs[:n_in], *refs[n_in + len(deps):])
    return wrapped


def _lane_sel(lane, a2, a4, a8, a16):
    return jnp.where(lane < POOL_GW, a2, jnp.where(lane < 2 * POOL_GW, a4, jnp.where(lane < 3 * POOL_GW, a8, a16)))


def _window_sums_back(src, tmp_a, tmp_b, tm):
    n = HALO + tm
    rows = lambda ref, lo, back: ref[pl.ds(lo - back, n - lo), :]
    tmp_a[pl.ds(8, n - 8), :] = rows(src, 8, 0) + rows(src, 8, 1)
    tmp_b[pl.ds(16, n - 16), :] = rows(tmp_a, 16, 0) + rows(tmp_a, 16, 2)
    s2 = rows(tmp_a, HALO, 0)
    tmp_a[pl.ds(24, n - 24), :] = rows(tmp_b, 24, 0) + rows(tmp_b, 24, 4)
    s8 = rows(tmp_a, HALO, 0)
    return s2, rows(tmp_b, HALO, 0), s8, s8 + rows(tmp_a, HALO, 8)


def _window_sums_ahead(src, tmp_a, tmp_b, tm):
    rows = lambda ref, n, ahead: ref[pl.ds(ahead, n), :]
    tmp_a[pl.ds(0, tm + 24), :] = rows(src, tm + 24, 0) + rows(src, tm + 24, 1)
    tmp_b[pl.ds(0, tm + 16), :] = rows(tmp_a, tm + 16, 0) + rows(tmp_a, tm + 16, 2)
    s2 = rows(tmp_a, tm, 0)
    tmp_a[pl.ds(0, tm + 8), :] = rows(tmp_b, tm + 8, 0) + rows(tmp_b, tm + 8, 4)
    s8 = rows(tmp_a, tm, 0)
    return s2, rows(tmp_b, tm, 0), s8, s8 + rows(tmp_a, tm, 8)


def _pool_counts(tm, t0):
    lane = lax.broadcasted_iota(jnp.int32, (tm, D_POOL), 1)
    row = lax.broadcasted_iota(jnp.int32, (tm, D_POOL), 0) + t0
    cnt = jnp.minimum(row + 1, _lane_sel(lane, 2, 4, 8, 16)).astype(F32)
    return lane, cnt


def _pool_fwd(ext_q, tmp_a, tmp_b, tm, t0):
    lane, cnt = _pool_counts(tm, t0)
    q = ext_q[pl.ds(HALO, tm), :]
    pooled = _lane_sel(lane, *_window_sums_back(ext_q, tmp_a, tmp_b, tm)) / cnt - q
    return pooled, lane, cnt


def _taps(src, w_of, offs, tm, zbuf):
    acc = None
    for r in range(8):
        ks = [k for k in range(len(offs)) if offs[k] % 8 == r]
        if not ks:
            continue
        rows = tm + (8 if r else 0)
        z = w_of(ks[0]) * src[pl.ds(offs[ks[0]] - r, rows), :]
        for k in ks[1:]:
            z = z + w_of(k) * src[pl.ds(offs[k] - r, rows), :]
        if r:
            zbuf[...] = z
            z = zbuf[pl.ds(r, tm), :]
        acc = z if acc is None else acc + z
    return acc


def _tap_grads(d_pad, src, offs, tm, g_ref, zbuf):
    ch = src.shape[-1]
    for r in range(8):
        ks = [k for k in range(len(offs)) if offs[k] % 8 == r]
        if not ks:
            continue
        rows = tm + (8 if r else 0)
        if r:
            zbuf[...] = d_pad[pl.ds(8 - r, rows), :]
        for k in ks:
            d = zbuf[...] if r else d_pad[pl.ds(8, rows), :]
            prod = d * src[pl.ds(offs[k] - r, rows), :]
            g_ref[k] += jnp.sum(prod.reshape(rows // 8, 8, ch), axis=0)


_CONV_OFFS = [HALO - (CONV_K - 1) + k for k in range(CONV_K)]


def _conv_fwd(ext_u, dww_ref, dwb, tm, zbuf):
    return dwb + _taps(ext_u, lambda k: dww_ref[k:k + 1, :], _CONV_OFFS, tm, zbuf)


def _ln_silu(c, lng, lnb):
    mu = jnp.mean(c, axis=-1, keepdims=True)
    cc = c - mu
    rstd = lax.rsqrt(jnp.mean(cc * cc, axis=-1, keepdims=True) + EPS)
    z = cc * rstd
    l = z * lng + lnb
    sl = _sig(l)
    return z, rstd, l, sl, l * sl


def _rg_conv(ext_x, cw_ref, cb, tm):
    xc = cb + cw_ref[0:1, :] * ext_x[pl.ds(HALO - (RG_CONV_K - 1), tm), :]
    for k in range(1, RG_CONV_K):
        xc = xc + cw_ref[k:k + 1, :] * ext_x[pl.ds(HALO - (RG_CONV_K - 1) + k, tm), :]
    return xc


def _softplus_neg(lam):
    return jnp.maximum(-lam, 0.0) + jnp.log(1.0 + jnp.exp(-jnp.abs(lam)))


def _rg_gates(xc, wa, ba, wx, bx, lam):
    xcb = xc.astype(BF16)
    r = _sig(_dot(xcb, wa) + ba)
    ig = _sig(_dot(xcb, wx) + bx)
    log_a = (-RG_C * r) * _softplus_neg(lam)
    a = jnp.exp(log_a)
    return r, ig, a, jnp.sqrt(_one_minus_sq(a, log_a))


def _scan_rows(a_ref, b_ref, out_ref, carry, tm, reverse):
    rows = lax.broadcasted_iota(jnp.int32, (8, D_RNN), 0)
    ngrp = tm // 8

    def grp(gi, hb):
        st = pl.multiple_of((ngrp - 1 - gi if reverse else gi) * 8, 8)
        a8 = a_ref[pl.ds(st, 8), :]
        b8 = b_ref[pl.ds(st, 8), :]
        out = jnp.zeros((8, D_RNN), F32)
        for j in (range(7, -1, -1) if reverse else range(8)):
            aj = jnp.broadcast_to(a8[j:j + 1, :], (8, D_RNN))
            bj = jnp.broadcast_to(b8[j:j + 1, :], (8, D_RNN))
            if reverse:
                cur = bj + hb
                hb = aj * cur
            else:
                cur = aj * hb + bj
                hb = cur
            out = jnp.where(rows == j, cur, out)
        out_ref[pl.ds(st, 8), :] = out
        return hb

    carry[...] = lax.fori_loop(0, ngrp, grp, carry[...])


_MIX_W = ("wp", "psc", "dww", "dwb", "lng", "lnb", "wpw", "cw", "cb", "wa", "ba", "wx", "bx", "lam")


def _lead_tile(src_ref, dst, carry, i, nt, tm, lead, n_src):
    last = n_src - (nt - 1) * tm
    assert 0 < last and lead + last <= tm, (n_src, nt, tm)
    dst[0:lead, :] = carry[...]

    @pl.when(i < nt - 1)
    def _():
        dst[lead:tm, :] = src_ref[0:tm - lead, :]
        carry[...] = src_ref[tm - lead:tm, :]

    @pl.when(i == nt - 1)
    def _():
        dst[lead:lead + last, :] = src_ref[0:last, :]
        if lead + last < tm:
            dst[lead + last:tm, :] = jnp.zeros((tm - lead - last, dst.shape[1]), dst.dtype)


def _mixer_fwd(h, g, w_in, mw, lead=None, t_pad=None):
    t = h.shape[0] if lead is None else t_pad
    tm = _row_tile(t, TM_MIX)
    nt = t // tm
    n_lead = 0 if lead is None else lead.shape[0]

    def body(h_ref, g_ref, win_ref, wp, psc, dww, dwb, lng, lnb, wpw, cw, cb, wa, ba, wx, bx, lam, *rest):
        if n_lead:
            lead_ref, rest = rest[0], rest[1:]
            h_out, h_carry = rest[7], rest[-1]
            rest = rest[:7] + rest[8:-1]
        (y_ref, p_ref, u_ref, hs_ref, conv_ref, xc_ref, gates_ref,
         ext_q, ext_u, ext_x, tmp_a, tmp_b, zbuf, a_s, b_s, hcar) = rest
        i = pl.program_id(0)

        @pl.when(i == 0)
        def _():
            ext_q[0:HALO, :] = jnp.zeros((HALO, D_POOL), F32)
            ext_u[0:HALO, :] = jnp.zeros((HALO, D_CONV), F32)
            ext_x[0:HALO, :] = jnp.zeros((HALO, D_RNN), F32)
            hcar[...] = jnp.zeros((8, D_RNN), F32)
            if n_lead:
                h_carry[...] = lead_ref[...]

        if n_lead:
            _lead_tile(h_ref, h_out, h_carry, i, nt, tm, n_lead, h.shape[0])
            h_ref = h_out
        u = (_rms(h_ref[...])[1] * g_ref[...]).astype(BF16)
        u_ref[...] = u
        p_ref[...] = _dot_nt(u, win_ref[...])

        ext_q[pl.ds(HALO, tm), :] = p_ref[:, 0:256]
        pooled, _, _ = _pool_fwd(ext_q, tmp_a, tmp_b, tm, i * tm)
        y_ref[:, 0:256] = (_dot(pooled.astype(BF16), wp[...]) * psc[...]).astype(BF16)

        ext_u[pl.ds(HALO, tm), :] = p_ref[:, 256:512] * _sig(p_ref[:, 512:768])
        conv = _conv_fwd(ext_u, dww, dwb[...], tm, zbuf)
        conv_ref[...] = conv
        act = _ln_silu(conv, lng[...], lnb[...])[4]
        y_ref[:, 256:512] = _dot(act.astype(BF16), wpw[...]).astype(BF16)

        ext_x[pl.ds(HALO, tm), :] = p_ref[:, 1280:1792]
        xc = _rg_conv(ext_x, cw, cb[...], tm)
        xc_ref[...] = xc
        r, ig, a, m = _rg_gates(xc, wa[...], ba[...], wx[...], bx[...], lam[...])
        for j, gate in enumerate((r, ig, a, m)):
            gates_ref[:, j * D_RNN:(j + 1) * D_RNN] = gate
        a_s[...] = a
        b_s[...] = m * (ig * xc)
        _scan_rows(a_s, b_s, hs_ref, hcar, tm, reverse=False)
        y_ref[:, 512:1024] = (_gelu_and_grad(p_ref[:, 768:1280])[0] * hs_ref[...]).astype(BF16)

        ext_q[0:HALO, :] = ext_q[pl.ds(tm, HALO), :]
        ext_u[0:HALO, :] = ext_u[pl.ds(tm, HALO), :]
        ext_x[0:HALO, :] = ext_x[pl.ds(tm, HALO), :]

    ws = [mw[k] for k in _MIX_W]
    row = lambda w: pl.BlockSpec((tm, w), lambda i: (i, 0))
    extra = [] if lead is None else [lead]
    return pl.pallas_call(
        body, name="mixer_fwd", grid=(nt,),
        in_specs=[row(D_MODEL), _full(g), _resident(w_in)] + [_full(w) for w in ws] + [_full(a) for a in extra],
        out_specs=[row(D_MODEL), row(D_IN), row(D_MODEL), row(D_RNN), row(D_CONV), row(D_RNN), row(4 * D_RNN)]
        + [row(D_MODEL)] * len(extra),
        out_shape=[jax.ShapeDtypeStruct((t, D_MODEL), BF16), jax.ShapeDtypeStruct((t, D_IN), F32),
                   jax.ShapeDtypeStruct((t, D_MODEL), BF16), jax.ShapeDtypeStruct((t, D_RNN), F32),
                   jax.ShapeDtypeStruct((t, D_CONV), F32), jax.ShapeDtypeStruct((t, D_RNN), F32),
                   jax.ShapeDtypeStruct((t, 4 * D_RNN), F32)] + [jax.ShapeDtypeStruct((t, D_MODEL), F32)] * len(extra),
        scratch_shapes=[pltpu.VMEM((HALO + tm, D_POOL), F32), pltpu.VMEM((HALO + tm, D_CONV), F32),
                        pltpu.VMEM((HALO + tm, D_RNN), F32), pltpu.VMEM((HALO + tm, D_POOL), F32),
                        pltpu.VMEM((HALO + tm, D_POOL), F32), pltpu.VMEM((tm + 8, D_CONV), F32),
                        pltpu.VMEM((tm, D_RNN), F32), pltpu.VMEM((tm, D_RNN), F32), pltpu.VMEM((8, D_RNN), F32)]
        + [pltpu.VMEM(a.shape, F32) for a in extra],
        compiler_params=_params("arbitrary"),
    )(h, g, w_in, *ws, *extra)


_MIX_G = (("wp", (D_POOL, D_POOL)), ("psc", (1, D_POOL)), ("dww", (32, 8, D_CONV)), ("dwb", (1, D_CONV)),
          ("lng", (1, D_CONV)), ("lnb", (1, D_CONV)), ("wpw", (D_CONV, D_CONV)), ("cw", (8, D_RNN)),
          ("cb", (1, D_RNN)), ("wa", (D_RNN, D_RNN)), ("ba", (1, D_RNN)), ("wx", (D_RNN, D_RNN)),
          ("bx", (1, D_RNN)), ("lam", (1, D_RNN)), ("g1", (1, D_MODEL)))


def _mixer_bwd(p, dh1, hs, conv, xc, gates, h0, g1, w_out, w_in, mw, deps=(), lead=0, t_real=None):
    t = p.shape[0]
    tm = _row_tile(t, TM_MIX)
    nt = t // tm
    hb = tm // HALO
    t_out = t_real - lead if lead else t

    def body(p_ref, ph_ref, dh1_ref, hs_ref, hsh_ref, conv_ref, xc_ref, gates_ref, h0_ref, g1_ref, wout_ref, win_ref,
             wp, psc, dww, dwb, lng, lnb, wpw, cw, cb, wa, ba, wx, bx, lam,
             dp_ref, dh0_ref, g_wp, g_psc, g_dww, g_dwb, g_lng, g_lnb, g_wpw, g_cw, g_cb, g_wa, g_ba, g_wx, g_bx, g_lam, g_g1,
             *tail):
        dlead_ref, carry = (tail[0], tail[-1]) if lead else (None, None)
        (ext_q, ext_u, ext_x, ext_h, ee, dc_s, dx_s, tmp_a, tmp_b, zbuf, d_pad, a_s, b_s, g_s, gcar, dy_ref,
         dp_s) = tail[1:-1] if lead else tail
        step = pl.program_id(0)
        i = nt - 1 - step
        grads = (g_wp, g_psc, g_dww, g_dwb, g_lng, g_lnb, g_wpw, g_cw, g_cb, g_wa, g_ba, g_wx, g_bx, g_lam, g_g1)
        if lead:
            grads += (carry,)
        dy_ref[...] = dh1_ref[...].astype(BF16)
        dy_cols = lambda lo, hi: _dot_nt(dy_ref[...], wout_ref[lo:hi, :])

        @pl.when(step == 0)
        def _():
            for gr in grads:
                gr[...] = jnp.zeros(gr.shape, F32)
            ee[pl.ds(tm, HALO), :] = jnp.zeros((HALO, D_POOL), F32)
            dc_s[pl.ds(tm, HALO), :] = jnp.zeros((HALO, D_CONV), F32)
            dx_s[pl.ds(tm, HALO), :] = jnp.zeros((HALO, D_RNN), F32)
            d_pad[0:8, :] = jnp.zeros((8, D_CONV), F32)
            d_pad[pl.ds(tm + 8, 8), :] = jnp.zeros((8, D_CONV), F32)
            gcar[...] = jnp.zeros((8, D_RNN), F32)

        hm = jnp.where(i == 0, 0.0, 1.0)

        ext_q[0:HALO, :] = ph_ref[:, 0:256] * hm
        ext_q[pl.ds(HALO, tm), :] = p_ref[:, 0:256]
        pooled, lane, cnt = _pool_fwd(ext_q, tmp_a, tmp_b, tm, i * tm)
        pooled_b = pooled.astype(BF16)
        dya = dy_cols(0, 256)
        g_psc[...] += _colsum(dya * _dot(pooled_b, wp[...]))
        dmixed_b = (dya * psc[...]).astype(BF16)
        dpooled = _dot_nt(dmixed_b, wp[...])
        g_wp[...] += _dot_tn(pooled_b, dmixed_b)
        ee[0:tm, :] = dpooled / cnt
        dp_s[:, 0:256] = _lane_sel(lane, *_window_sums_ahead(ee, tmp_a, tmp_b, tm)) - dpooled
        ee[pl.ds(tm, HALO), :] = ee[0:HALO, :]

        v = p_ref[:, 256:512]
        s = _sig(p_ref[:, 512:768])
        ext_u[0:HALO, :] = ph_ref[:, 256:512] * _sig(ph_ref[:, 512:768]) * hm
        ext_u[pl.ds(HALO, tm), :] = v * s
        z, rstd, l, sl, act = _ln_silu(conv_ref[...], lng[...], lnb[...])
        dyb_b = dy_cols(256, 512).astype(BF16)
        dact = _dot_nt(dyb_b, wpw[...])
        g_wpw[...] += _dot_tn(act.astype(BF16), dyb_b)
        dl = dact * (sl * (1.0 + l * (1.0 - sl)))
        g_lng[...] += _colsum(dl * z)
        g_lnb[...] += _colsum(dl)
        dz = dl * lng[...]
        dc = rstd * (dz - jnp.mean(dz, axis=-1, keepdims=True) - z * jnp.mean(dz * z, axis=-1, keepdims=True))
        g_dwb[...] += _colsum(dc)
        dc_s[0:tm, :] = dc
        d_pad[pl.ds(8, tm), :] = dc
        _tap_grads(d_pad, ext_u, _CONV_OFFS, tm, g_dww, zbuf)
        du0 = _taps(dc_s, lambda j: dww[CONV_K - 1 - j:CONV_K - j, :], list(range(CONV_K)), tm, zbuf)
        dp_s[:, 256:512] = du0 * s
        dp_s[:, 512:768] = du0 * v * (s * (1.0 - s))
        dc_s[pl.ds(tm, HALO), :] = dc_s[0:HALO, :]

        ext_x[0:HALO, :] = ph_ref[:, 1280:1792] * hm
        ext_x[pl.ds(HALO, tm), :] = p_ref[:, 1280:1792]
        xc = xc_ref[...]
        xcb = xc.astype(BF16)
        r, ig, a, m = (gates_ref[:, j * D_RNN:(j + 1) * D_RNN] for j in range(4))
        sp = _softplus_neg(lam[...])
        ext_h[0:HALO, :] = hsh_ref[...] * hm
        ext_h[pl.ds(HALO, tm), :] = hs_ref[...]
        dyc = dy_cols(512, 1024)
        gl, dgl = _gelu_and_grad(p_ref[:, 768:1280])
        dp_s[:, 768:1280] = dyc * hs_ref[...] * dgl
        a_s[...] = a
        b_s[...] = dyc * gl
        _scan_rows(a_s, b_s, g_s, gcar, tm, reverse=True)
        g = g_s[...]
        da = g * ext_h[pl.ds(HALO - 1, tm), :]
        dm = g * (ig * xc)
        dig = g * (m * xc)
        dlog_a = da * a - dm * (a * a) / m
        g_lam[...] += _colsum(dlog_a * (-RG_C * r)) * (-_sig(-lam[...]))
        dra = (dlog_a * (-RG_C * sp)) * (r * (1.0 - r))
        dia = dig * (ig * (1.0 - ig))
        g_ba[...] += _colsum(dra)
        g_bx[...] += _colsum(dia)
        dra_b = dra.astype(BF16)
        dia_b = dia.astype(BF16)
        dxc = g * (m * ig) + _dot_nt(dra_b, wa[...]) + _dot_nt(dia_b, wx[...])
        g_wa[...] += _dot_tn(xcb, dra_b)
        g_wx[...] += _dot_tn(xcb, dia_b)
        g_cb[...] += _colsum(dxc)
        dx_s[0:tm, :] = dxc
        for k in range(RG_CONV_K):
            g_cw[k:k + 1, :] += _colsum(dxc * ext_x[pl.ds(HALO - (RG_CONV_K - 1) + k, tm), :])
        dxin = cw[RG_CONV_K - 1:RG_CONV_K, :] * dxc
        for j in range(1, RG_CONV_K):
            dxin = dxin + cw[RG_CONV_K - 1 - j:RG_CONV_K - j, :] * dx_s[pl.ds(j, tm), :]
        dp_s[:, 1280:1792] = dxin
        dx_s[pl.ds(tm, HALO), :] = dx_s[0:HALO, :]

        dpb = dp_s[...].astype(BF16)
        dp_ref[...] = dpb
        du = _dot(dpb, win_ref[...])
        r, n = _rms(h0_ref[...])
        g_g1[...] += _colsum(du * n)
        dh0 = dh1_ref[...] + _rms_bwd(du, n, r, g1_ref[...])
        if lead:
            dh0_ref[0:tm - lead, :] = dh0[lead:tm, :]
            dh0_ref[tm - lead:tm, :] = carry[...]
            carry[...] = dh0[0:lead, :]

            @pl.when(i == 0)
            def _():
                dlead_ref[...] = dh0[0:lead, :]
        else:
            dh0_ref[...] = dh0

    ws = [mw[k] for k in _MIX_W]
    tile = lambda w: pl.BlockSpec((tm, w), lambda s: (nt - 1 - s, 0))
    halo = lambda w: pl.BlockSpec((HALO, w), lambda s: (jnp.maximum((nt - 1 - s) * hb - 1, 0), 0))
    lead_out = [pl.BlockSpec((lead, D_MODEL), lambda s: (0, 0))] if lead else []
    outs = pl.pallas_call(
        _after(body, 12 + len(ws), deps), name="mixer_bwd", grid=(nt,),
        in_specs=[tile(D_IN), halo(D_IN), tile(D_MODEL), tile(D_RNN), halo(D_RNN), tile(D_CONV), tile(D_RNN), tile(4 * D_RNN),
                  tile(D_MODEL), _full(g1),
                  _resident(w_out), _resident(w_in)] + [_full(w) for w in ws] + [ANY] * len(deps),
        out_specs=[tile(D_IN), tile(D_MODEL)] + [pl.BlockSpec(shp, lambda s, nd=len(shp): (0,) * nd) for _, shp in _MIX_G]
        + lead_out,
        out_shape=[jax.ShapeDtypeStruct((t, D_IN), BF16), jax.ShapeDtypeStruct((t_out, D_MODEL), F32)]
        + [jax.ShapeDtypeStruct(shp, F32) for _, shp in _MIX_G] + [jax.ShapeDtypeStruct((lead, D_MODEL), F32)] * bool(lead),
        scratch_shapes=[pltpu.VMEM((HALO + tm, D_POOL), F32), pltpu.VMEM((HALO + tm, D_CONV), F32),
                        pltpu.VMEM((HALO + tm, D_RNN), F32), pltpu.VMEM((HALO + tm, D_RNN), F32),
                        pltpu.VMEM((tm + HALO, D_POOL), F32), pltpu.VMEM((tm + HALO, D_CONV), F32),
                        pltpu.VMEM((tm + HALO, D_RNN), F32), pltpu.VMEM((HALO + tm, D_POOL), F32),
                        pltpu.VMEM((HALO + tm, D_POOL), F32), pltpu.VMEM((tm + 8, D_CONV), F32),
                        pltpu.VMEM((tm + 16, D_CONV), F32), pltpu.VMEM((tm, D_RNN), F32),
                        pltpu.VMEM((tm, D_RNN), F32), pltpu.VMEM((tm, D_RNN), F32), pltpu.VMEM((8, D_RNN), F32),
                        pltpu.VMEM((tm, D_MODEL), BF16), pltpu.VMEM((tm, D_IN), F32)]
        + [pltpu.VMEM((lead, D_MODEL), F32)] * bool(lead),
        compiler_params=_params("arbitrary"),
    )(p, p, dh1, hs, hs, conv, xc, gates, h0, g1, w_out, w_in, *ws, *deps)
    grads = {k: o for (k, _), o in zip(_MIX_G, outs[2:])}
    return (outs[0], (outs[1], outs[-1]), grads) if lead else (outs[0], outs[1], grads)


def _mid_fwd(y, h0, w_out, g, w_up):
    t = h0.shape[0]
    tm = _row_tile(t, TM_MAT)

    def body(y_ref, h0_ref, wo_ref, g_ref, wu_ref, h1_ref, u2_ref, f_ref):
        h1 = h0_ref[...] + _dot(y_ref[...], wo_ref[...])
        h1_ref[...] = h1
        u2 = (_rms(h1)[1] * g_ref[...]).astype(BF16)
        u2_ref[...] = u2
        for c in range(D_FF // FF_CHUNK):
            f_ref[:, c * FF_CHUNK:(c + 1) * FF_CHUNK] = _dot(u2, wu_ref[c]).astype(BF16)

    row = lambda w: pl.BlockSpec((tm, w), lambda i: (i, 0))
    return pl.pallas_call(
        body, name="mid_fwd", grid=(t // tm,),
        in_specs=[row(D_MODEL), row(D_MODEL), _resident(w_out), _full(g), _resident(w_up)],
        out_specs=[row(D_MODEL), row(D_MODEL), row(D_FF)],
        out_shape=[jax.ShapeDtypeStruct((t, D_MODEL), F32), jax.ShapeDtypeStruct((t, D_MODEL), BF16),
                   jax.ShapeDtypeStruct((t, D_FF), BF16)],
        compiler_params=_params("parallel"),
    )(y, h0, w_out, g, w_up)


def _down_proj(f_ref, h1_ref, wd_ref):
    acc = h1_ref[...]
    for c in range(D_FF // FF_CHUNK):
        cols = slice(c * FF_CHUNK, (c + 1) * FF_CHUNK)
        a = jnp.square(jnp.maximum(f_ref[:, cols].astype(F32), 0.0)).astype(BF16)
        acc = acc + _dot(a, wd_ref[cols, :])
    return acc


def _down_fwd(f, h1, w_down):
    t = h1.shape[0]
    tm = _row_tile(t, TM_MAT)

    def body(f_ref, h1_ref, wd_ref, h2_ref):
        h2_ref[...] = _down_proj(f_ref, h1_ref, wd_ref)

    row = lambda w: pl.BlockSpec((tm, w), lambda i: (i, 0))
    return pl.pallas_call(
        body, name="down_fwd", grid=(t // tm,),
        in_specs=[row(D_FF), row(D_MODEL), _resident(w_down)], out_specs=row(D_MODEL),
        out_shape=jax.ShapeDtypeStruct((t, D_MODEL), F32),
        compiler_params=_params("parallel"),
    )(f, h1, w_down)


def _down_fwd_loss(f, h1, w_down, g, tgt, t_real):
    t = h1.shape[0]
    tm = _row_tile(t, TM_MAT)
    nt = t // tm

    def body(f_ref, h1_ref, wd_ref, g_ref, tgt_in, loss_ref, dh_ref, dg_ref, tgt_ref, carry):
        i = pl.program_id(0)

        @pl.when(i == 0)
        def _():
            loss_ref[...] = jnp.zeros(loss_ref.shape, F32)
            dg_ref[...] = jnp.zeros(dg_ref.shape, F32)
            carry[...] = jnp.zeros(carry.shape, F32)

        _lead_tile(tgt_in, tgt_ref, carry, i, nt, tm, N_META, tgt.shape[0])

        r, n = _rms(_down_proj(f_ref, h1_ref, wd_ref))
        row = lax.broadcasted_iota(jnp.int32, (tm, 1), 0) + i * tm
        valid = jnp.logical_and(row >= N_META, row < t_real)
        diff = jnp.where(valid, n * g_ref[...] - tgt_ref[...], 0.0)
        loss_ref[...] += 0.5 * jnp.sum(jnp.mean(diff * diff, axis=-1, keepdims=True))
        dy = diff * (1.0 / D_MODEL)
        dg_ref[...] += _colsum(dy * n)
        dh_ref[...] = _rms_bwd(dy, n, r, g_ref[...])

    row = lambda w: pl.BlockSpec((tm, w), lambda i: (i, 0))
    return pl.pallas_call(
        body, name="down_fwd_loss", grid=(t // tm,),
        in_specs=[row(D_FF), row(D_MODEL), _resident(w_down), _full(g), row(D_MODEL)],
        out_specs=[pl.BlockSpec((8, 128), lambda i: (0, 0)), row(D_MODEL), pl.BlockSpec((1, D_MODEL), lambda i: (0, 0))],
        out_shape=[jax.ShapeDtypeStruct((8, 128), F32), jax.ShapeDtypeStruct((t, D_MODEL), F32),
                   jax.ShapeDtypeStruct((1, D_MODEL), F32)],
        scratch_shapes=[pltpu.VMEM((tm, D_MODEL), F32), pltpu.VMEM((N_META, D_MODEL), F32)],
        compiler_params=_params("arbitrary"),
    )(f, h1, w_down, g, tgt)


def _mlp_bwd(dh2, f, h1, g, w_up, w_down, deps=()):
    t = dh2.shape[0]
    tm = _row_tile(t, TM_MLP_BWD)

    def body(dh2_ref, f_ref, wd_ref, wu_ref, h1_ref, g_ref, df_ref, dh1_ref, dg_ref):
        @pl.when(pl.program_id(0) == 0)
        def _():
            dg_ref[...] = jnp.zeros(dg_ref.shape, F32)

        dh2 = dh2_ref[...]
        dhb = dh2.astype(BF16)
        du2 = None
        for c in range(D_FF // FF_CHUNK):
            cols = slice(c * FF_CHUNK, (c + 1) * FF_CHUNK)
            dact = _dot_nt(dhb, wd_ref[c])
            df = (dact * (2.0 * jnp.maximum(f_ref[:, cols].astype(F32), 0.0))).astype(BF16)
            df_ref[:, cols] = df
            part = _dot_nt(df, wu_ref[c])
            du2 = part if du2 is None else du2 + part
        r, n = _rms(h1_ref[...])
        dg_ref[...] += _colsum(du2 * n)
        dh1_ref[...] = dh2 + _rms_bwd(du2, n, r, g_ref[...])

    row = lambda w: pl.BlockSpec((tm, w), lambda i: (i, 0))
    return pl.pallas_call(
        _after(body, 6, deps), name="mlp_bwd", grid=(t // tm,),
        in_specs=[row(D_MODEL), row(D_FF), _resident(w_down), _resident(w_up), row(D_MODEL), _full(g)] + [ANY] * len(deps),
        out_specs=[row(D_FF), row(D_MODEL), pl.BlockSpec((1, D_MODEL), lambda i: (0, 0))],
        out_shape=[jax.ShapeDtypeStruct((t, D_FF), BF16), jax.ShapeDtypeStruct((t, D_MODEL), F32),
                   jax.ShapeDtypeStruct((1, D_MODEL), F32)],
        compiler_params=_params("arbitrary"),
    )(dh2, f, w_down, w_up, h1, g, *deps)


def _tn_matmul(a, b, kc, nc, relu2, name, deps=()):
    t, k = a.shape
    n = b.shape[1]
    tt = _row_tile(t, TM_MAT)
    gk, gn = k // kc, n // nc

    def body(a_ref, b_ref, o_ref):
        @pl.when(pl.program_id(2) == 0)
        def _():
            o_ref[...] = jnp.zeros(o_ref.shape, F32)

        av = a_ref[...]
        if relu2:
            av = jnp.square(jnp.maximum(av.astype(F32), 0.0))
        o_ref[...] += _dot_tn(av.astype(BF16), b_ref[...].astype(BF16))

    return pl.pallas_call(
        _after(body, 2, deps), name=name, grid=(gk, gn, t // tt),
        in_specs=[pl.BlockSpec((tt, kc), lambda ik, jn, it: (it, ik)), pl.BlockSpec((tt, nc), lambda ik, jn, it: (it, jn))]
        + [ANY] * len(deps),
        out_specs=pl.BlockSpec((None, kc, nc), lambda ik, jn, it: (ik * gn + jn, 0, 0)),
        out_shape=jax.ShapeDtypeStruct((gk * gn, kc, nc), F32),
        compiler_params=_params("parallel", "parallel", "arbitrary"),
    )(a, b, *deps)


def _block_diag(blocks):
    nb, hd, _ = blocks.shape
    eye = jnp.eye(nb, dtype=blocks.dtype)
    return (blocks[:, :, None, :] * eye[:, None, :, None]).reshape(nb * hd, nb * hd)


def _diag_blocks(m, nb):
    hd = m.shape[0] // nb
    eye = jnp.eye(nb, dtype=m.dtype)
    return jnp.sum(m.reshape(nb, hd, nb, hd) * eye[:, None, :, None], axis=2)


def _mixer_weights(w, l):
    row = lambda a: a.reshape(1, -1)
    return dict(
        wp=_block_diag(w["pool_w"][l]).astype(BF16), psc=row(w["pool_scale"][l]),
        dww=jnp.pad(w["convb_dw_w"][l], ((0, 32 - CONV_K), (0, 0))), dwb=row(w["convb_dw_b"][l]),
        lng=row(w["convb_ln_g"][l]), lnb=row(w["convb_ln_b"][l]), wpw=w["convb_pw_w"][l].astype(BF16),
        cw=jnp.pad(w["rg_conv_w"][l], ((0, 8 - RG_CONV_K), (0, 0))), cb=row(w["rg_conv_b"][l]),
        wa=_block_diag(w["rg_w_a"][l]).astype(BF16), ba=row(w["rg_b_a"][l]),
        wx=_block_diag(w["rg_w_x"][l]).astype(BF16), bx=row(w["rg_b_x"][l]), lam=row(w["rg_lambda"][l]))


def _local_step(h, tgt, t_real, t_pad, w, fetch, hooks):
    depth = 2
    saved = []
    big = []
    for l in range(depth):
        mw = _mixer_weights(w, l)
        g1 = w["mix_norm_g"][l].reshape(1, -1)
        g2 = w["mlp_norm_g"][l].reshape(1, -1)
        wl = dict(w_in=fetch(l, "w_in", h))
        if l == 0:
            y, p, u, hs, conv, xc, gates, h = _mixer_fwd(h, g1, wl["w_in"], mw, lead=w["meta_tokens"], t_pad=t_pad)
        else:
            y, p, u, hs, conv, xc, gates = _mixer_fwd(h, g1, wl["w_in"], mw)
        wl["w_out"], wl["w_up"] = fetch(l, "w_out", y), fetch(l, "w_up", y)
        h1, u2, f = _mid_fwd(y, h, wl["w_out"], g2, wl["w_up"])
        wl["w_down"] = fetch(l, "w_down", f)
        if l == depth - 1:
            loss, dh, dgf = _down_fwd_loss(f, h1, wl["w_down"].reshape(D_FF, D_MODEL), w["final_norm_g"].reshape(1, -1), tgt,
                                           t_real)
            h2 = None
        else:
            h2 = _down_fwd(f, h1, wl["w_down"].reshape(D_FF, D_MODEL))
        saved.append(dict(mw=mw, g1=g1, g2=g2, h0=h, p=p, u=u, y=y, hs=hs, conv=conv, xc=xc, gates=gates, h1=h1, u2=u2, f=f))
        big.append(wl)
        h = h2

    gs = {k: [None] * depth for k in ("mix_norm_g", "mlp_norm_g", "pool_w", "pool_scale", "convb_dw_w", "convb_dw_b",
                                      "convb_ln_g", "convb_ln_b", "convb_pw_w", "rg_conv_w", "rg_conv_b", "rg_w_a",
                                      "rg_b_a", "rg_w_x", "rg_b_x", "rg_lambda")}
    deps = ()
    for l in reversed(range(depth)):
        s, wl = saved[l], big[l]
        df, dh1, dg2 = _mlp_bwd(dh, s["f"], s["h1"], s["g2"], wl["w_up"], wl["w_down"], deps)
        deps = hooks.point(l, "mlp_bwd", dh1)
        g_down = _tn_matmul(s["f"], dh, FF_CHUNK, D_MODEL, True, "dw_down", deps)
        hooks.grad(l, "w_down", g_down)
        deps = hooks.point(l, "dw_down", g_down)
        g_up = _tn_matmul(s["u2"], df, D_MODEL, FF_CHUNK, False, "dw_up", deps)
        hooks.grad(l, "w_up", g_up)
        deps = hooks.point(l, "dw_up", g_up)
        g_out = _tn_matmul(s["y"], dh1, D_MODEL, D_MODEL, False, "dw_out", deps)
        hooks.grad(l, "w_out", g_out.reshape(N_CHIPS, D_MODEL // N_CHIPS, D_MODEL))
        deps = hooks.point(l, "dw_out", g_out)
        dp, dh, mg = _mixer_bwd(s["p"], dh1, s["hs"], s["conv"], s["xc"], s["gates"], s["h0"], s["g1"], wl["w_out"],
                                wl["w_in"], s["mw"], deps, lead=0 if l else N_META, t_real=t_real)
        if l == 0:
            dh, dmeta = dh
        gs["mix_norm_g"][l] = mg["g1"][0]
        gs["mlp_norm_g"][l] = dg2[0]
        gs["pool_w"][l] = _diag_blocks(mg["wp"], D_POOL // POOL_GW)
        gs["pool_scale"][l] = mg["psc"][0]
        gs["convb_dw_w"][l] = jnp.sum(mg["dww"][:CONV_K], axis=1)
        gs["convb_dw_b"][l] = mg["dwb"][0]
        gs["convb_ln_g"][l] = mg["lng"][0]
        gs["convb_ln_b"][l] = mg["lnb"][0]
        gs["convb_pw_w"][l] = mg["wpw"]
        gs["rg_conv_w"][l] = mg["cw"][:RG_CONV_K]
        gs["rg_conv_b"][l] = mg["cb"][0]
        gs["rg_w_a"][l] = _diag_blocks(mg["wa"], D_RNN // RG_HD)
        gs["rg_b_a"][l] = mg["ba"][0]
        gs["rg_w_x"][l] = _diag_blocks(mg["wx"], D_RNN // RG_HD)
        gs["rg_b_x"][l] = mg["bx"][0]
        gs["rg_lambda"][l] = mg["lam"][0]
        if l == 0:
            gsmall = {k: jnp.stack(v) for k, v in gs.items()}
            gsmall["final_norm_g"] = dgf[0]
            gsmall["meta_tokens"] = dmeta
            started = hooks.small(gsmall)
        deps = hooks.point(l, "mixer_bwd", started[0] if l == 0 and started else dh)
        g_in = _tn_matmul(dp, s["u"], D_IN, D_MODEL, False, "dw_in", deps).reshape(N_CHIPS, D_IN // N_CHIPS, D_MODEL)
        hooks.grad(l, "w_in", g_in)
        deps = hooks.point(l, "dw_in", g_in)
    return loss[0, 0], dh


def _place():
    return lax.axis_index("x"), lax.axis_index("y"), lax.axis_index("c")


def _other_chips(x, y):
    return [(1 - x, y), (x, 1 - y), (1 - x, 1 - y)]


HBM_SPEC = pl.BlockSpec(memory_space=pltpu.HBM)
SEM_SPEC = pl.BlockSpec(memory_space=pltpu.SEMAPHORE)
DATAFLOW = pltpu.SideEffectType.DATAFLOW_SIDE_EFFECTING


def _gather_copies(src_refs, land_refs, send_sem, recv_sem, first):
    x, y, c = _place()
    me = 2 * x + y
    out = []
    for n in range(len(src_refs)):
        for j, (px, py) in enumerate(_other_chips(x, y)):
            out.append(pltpu.make_async_remote_copy(src_refs[n], land_refs[n].at[me], send_sem.at[first + 3 * n + j],
                                                    recv_sem.at[first + 3 * n + j], device_id=(px, py, c), device_id_type=MESH))
    return out


def _gather_start(groups, me, name):
    srcs = [pltpu.with_memory_space_constraint(s, pltpu.HBM) for g in groups for s in g]
    lands = [pltpu.with_memory_space_constraint(
        lax.dynamic_update_slice(jnp.zeros((N_CHIPS,) + s.shape, s.dtype), s[None], (me,) + (0,) * s.ndim), pltpu.HBM)
        for g in groups for s in g]
    n, ng = len(srcs), len(groups)
    first = [sum(len(g) for g in groups[:i]) for i in range(ng)]

    def body(*refs):
        src_refs, land_refs = refs[:n], refs[n:2 * n]
        sems = refs[2 * n:2 * n + 2 * ng]
        token = refs[-1]
        for gi, g in enumerate(groups):
            lo, hi = first[gi], first[gi] + len(g)
            for cp in _gather_copies(src_refs[lo:hi], land_refs[lo:hi], sems[2 * gi], sems[2 * gi + 1], 0):
                cp.start()
        token[...] = jnp.zeros(token.shape, token.dtype)

    sem_shapes = [pltpu.SemaphoreType.DMA((3 * len(g),)) for g in groups for _ in range(2)]
    outs = pl.pallas_call(
        body, name=name,
        out_shape=sem_shapes + [pltpu.HBM(a.shape, a.dtype) for a in srcs + lands] + [jax.ShapeDtypeStruct((8, 128), F32)],
        in_specs=[HBM_SPEC] * (2 * n),
        out_specs=[SEM_SPEC] * (2 * ng) + [HBM_SPEC] * (2 * n) + [pl.BlockSpec(memory_space=pltpu.VMEM)],
        input_output_aliases={i: 2 * ng + i for i in range(2 * n)},
        compiler_params=pltpu.CompilerParams(has_side_effects=DATAFLOW),
    )(*srcs, *lands)
    sems, thru, token = outs[:2 * ng], outs[2 * ng:2 * ng + 2 * n], outs[-1]
    state = []
    for gi, g in enumerate(groups):
        lo, hi = first[gi], first[gi] + len(g)
        state.append((sems[2 * gi], sems[2 * gi + 1], thru[lo:hi], thru[n + lo:n + hi]))
    return state, token


def _gather_wait(state, after, name):
    send_sem, recv_sem, srcs, lands = state
    n = len(srcs)

    def body(*refs):
        src_refs, land_refs = refs[:n], refs[n:2 * n]
        send, recv = refs[2 * n], refs[2 * n + 1]
        for cp in _gather_copies(src_refs, land_refs, send, recv, 0):
            cp.wait_send()
            cp.wait_recv()

    outs = pl.pallas_call(
        body, name=name,
        out_shape=[pltpu.HBM(a.shape, a.dtype) for a in list(srcs) + list(lands)],
        in_specs=[HBM_SPEC] * (2 * n) + [SEM_SPEC, SEM_SPEC, ANY],
        out_specs=[HBM_SPEC] * (2 * n),
        input_output_aliases={i: i for i in range(2 * n)},
        compiler_params=pltpu.CompilerParams(has_side_effects=DATAFLOW),
    )(*srcs, *lands, send_sem, recv_sem, after)
    return outs[n:]


def _add_halves(g, recv, c1):
    nk, r, cd = g.shape
    r2 = r // 2
    rc = r2 // ROW_CHUNKS

    def body(c_ref, g_ref, r_ref, pab_ref):
        pab_ref[...] = (g_ref[...] + r_ref[...]).astype(BF16)

    blk = pl.BlockSpec((None, rc, cd), lambda k, j, c_ref: (k, j, 0))
    return pl.pallas_call(
        body, name="rs_add_halves",
        grid_spec=pltpu.PrefetchScalarGridSpec(
            num_scalar_prefetch=1, grid=(nk, ROW_CHUNKS),
            in_specs=[pl.BlockSpec((None, rc, cd), lambda k, j, c_ref: (k, c_ref[0] * ROW_CHUNKS + j, 0)), blk], out_specs=blk),
        out_shape=jax.ShapeDtypeStruct((nk, r2, cd), BF16),
        compiler_params=_params("parallel", "parallel"),
    )(c1, g, recv)


def _sum_partials(g, recv_sibling, recv_chips, c_me):
    nk, r, cd = g.shape
    r2 = r // 2
    rc = r2 // ROW_CHUNKS

    def body(cm_ref, g_ref, a_ref, r_ref, s_ref):
        own = g_ref[...] + a_ref[...]
        s_ref[...] = ((own + r_ref[0].astype(F32)) + r_ref[1].astype(F32)) + r_ref[2].astype(F32)

    return pl.pallas_call(
        body, name="rs_sum_partials",
        grid_spec=pltpu.PrefetchScalarGridSpec(
            num_scalar_prefetch=1, grid=(ROW_CHUNKS,),
            in_specs=[pl.BlockSpec((None, rc, cd), lambda j, cm: (cm[1], cm[0] * ROW_CHUNKS + j, 0)),
                      pl.BlockSpec((None, rc, cd), lambda j, cm: (cm[1], j, 0)),
                      pl.BlockSpec((3, rc, cd), lambda j, cm: (0, j, 0))],
            out_specs=pl.BlockSpec((rc, cd), lambda j, cm: (j, 0))),
        out_shape=jax.ShapeDtypeStruct((r2, cd), F32),
        compiler_params=_params("parallel"),
    )(c_me, g, recv_sibling, recv_chips)


def _split_start(name, srcs, lands, ncopies, make_copies):
    srcs = [pltpu.with_memory_space_constraint(s, pltpu.HBM) for s in srcs]
    lands = [pltpu.with_memory_space_constraint(a, pltpu.HBM) for a in lands]
    n, m = len(srcs), len(lands)

    def body(*refs):
        src_refs, land_refs = refs[:n], refs[n:n + m]
        send, recv, token = refs[n + m], refs[n + m + 1], refs[-1]
        for cp in make_copies(src_refs, land_refs, send, recv):
            cp.start()
        token[...] = jnp.zeros(token.shape, token.dtype)

    outs = pl.pallas_call(
        body, name=name,
        out_shape=[pltpu.SemaphoreType.DMA((ncopies,)), pltpu.SemaphoreType.DMA((ncopies,))]
        + [pltpu.HBM(a.shape, a.dtype) for a in srcs + lands] + [jax.ShapeDtypeStruct((8, 128), F32)],
        in_specs=[HBM_SPEC] * (n + m),
        out_specs=[SEM_SPEC, SEM_SPEC] + [HBM_SPEC] * (n + m) + [pl.BlockSpec(memory_space=pltpu.VMEM)],
        input_output_aliases={i: 2 + i for i in range(n + m)},
        compiler_params=pltpu.CompilerParams(has_side_effects=DATAFLOW),
    )(*srcs, *lands)
    return (outs[0], outs[1], outs[2:2 + n], outs[2 + n:2 + n + m], make_copies), outs[-1]


def _split_wait(name, state, after):
    send_sem, recv_sem, srcs, lands, make_copies = state
    n, m = len(srcs), len(lands)

    def body(*refs):
        src_refs, land_refs = refs[:n], refs[n:n + m]
        for cp in make_copies(src_refs, land_refs, refs[n + m], refs[n + m + 1]):
            cp.wait_send()
            cp.wait_recv()

    outs = pl.pallas_call(
        body, name=name,
        out_shape=[pltpu.HBM(a.shape, a.dtype) for a in list(srcs) + list(lands)],
        in_specs=[HBM_SPEC] * (n + m) + [SEM_SPEC, SEM_SPEC, ANY],
        out_specs=[HBM_SPEC] * (n + m),
        input_output_aliases={i: i for i in range(n + m)},
        compiler_params=pltpu.CompilerParams(has_side_effects=DATAFLOW),
    )(*srcs, *lands, send_sem, recv_sem, after)
    return outs[:n], outs[n:]


def _copies_to_sibling(src_of):
    def make(src_refs, land_refs, send, recv):
        x, y, c = _place()
        return [pltpu.make_async_remote_copy(src_of(src_refs[i], c), land_refs[i], send.at[i], recv.at[i],
                                             device_id=(x, y, 1 - c), device_id_type=MESH) for i in range(len(src_refs))]
    return make


def _copies_to_chips(src_refs, land_refs, send, recv):
    x, y, c = _place()
    return [pltpu.make_async_remote_copy(src_refs[i].at[2 * px + py], land_refs[i].at[j], send.at[3 * i + j], recv.at[3 * i + j],
                                         device_id=(px, py, c), device_id_type=MESH)
            for i in range(len(src_refs)) for j, (px, py) in enumerate(_other_chips(x, y))]


def _other_half_rows(ref, c):
    r2 = ref.shape[1] // 2
    return ref.at[:, pl.ds(pl.multiple_of((1 - c) * r2, 8), r2)]


class _ReduceScatter:
    def __init__(self, tag, grads, c1, me1):
        self.tag, self.grads, self.c1, self.me1 = tag, grads, c1, me1

    def start(self):
        lands = [lax.empty((g.shape[0], g.shape[1] // 2, g.shape[2]), F32) for g in self.grads]
        self.state, token = _split_start("rs_%s_a_start" % self.tag, self.grads, lands, len(self.grads),
                                         _copies_to_sibling(_other_half_rows))
        return token

    def to_chips(self, after):
        self.halves = _split_wait("rs_%s_a_wait" % self.tag, self.state, after)
        pabs = [_add_halves(g, r, self.c1) for g, r in zip(*self.halves)]
        lands = [lax.empty((3,) + p.shape[1:], BF16) for p in pabs]
        self.state, token = _split_start("rs_%s_b_start" % self.tag, pabs, lands, 3 * len(pabs), _copies_to_chips)
        return token

    def to_sibling(self, after):
        _, recv = _split_wait("rs_%s_b_wait" % self.tag, self.state, after)
        c_me = jnp.concatenate([self.c1, self.me1])
        sums = [_sum_partials(g, ra, rb, c_me) for g, ra, rb in zip(*self.halves, recv)]
        lands = [lax.empty(s.shape, F32) for s in sums]
        self.state, token = _split_start("rs_%s_c_start" % self.tag, sums, lands, len(sums),
                                         _copies_to_sibling(lambda ref, c: ref))
        return token

    def finish(self, after):
        return list(zip(*_split_wait("rs_%s_c_wait" % self.tag, self.state, after)))


def _add_lists(a_list, b_list):
    n = len(a_list)

    def body(*refs):
        for i in range(n):
            refs[2 * n + i][...] = refs[i][...] + refs[n + i][...]

    vm = pl.BlockSpec(memory_space=pltpu.VMEM)
    return pl.pallas_call(
        body, name="add_lists", in_specs=[vm] * (2 * n), out_specs=[vm] * n,
        out_shape=[jax.ShapeDtypeStruct(a.shape, a.dtype) for a in a_list],
        compiler_params=pltpu.CompilerParams(vmem_limit_bytes=VMEM_LIMIT),
    )(*a_list, *b_list)


def _copies_to_peer(stage):
    def make(src_refs, land_refs, send, recv):
        x, y, c = _place()
        peer = [(x, y, 1 - c), (1 - x, y, c), (x, 1 - y, c)][stage]
        return [pltpu.make_async_remote_copy(src_refs[i], land_refs[i], send.at[i], recv.at[i], device_id=peer, device_id_type=MESH)
                for i in range(len(src_refs))]
    return make


class _AllReduceSmall:
    def __init__(self, vs):
        self.vs, self.stage = list(vs), 0

    def _start(self):
        lands = [lax.empty(v.shape, v.dtype) for v in self.vs]
        self.state, token = _split_start("ar_small_start_%d" % self.stage, self.vs, lands, len(self.vs), _copies_to_peer(self.stage))
        return token

    def start(self):
        return self._start()

    def step(self, after):
        mine, theirs = _split_wait("ar_small_wait_%d" % self.stage, self.state, after)
        self.vs = _add_lists(mine, theirs)
        self.stage += 1
        return self._start() if self.stage < 3 else self.vs[0]


def _adamw_math(w, g, m, v):
    m = ADAM_B1 * m + (1.0 - ADAM_B1) * g
    v = ADAM_B2 * v + (1.0 - ADAM_B2) * jnp.square(g)
    m_hat = m / (1.0 - ADAM_B1 ** ADAM_STEP)
    v_hat = v / (1.0 - ADAM_B2 ** ADAM_STEP)
    return -ADAM_LR * (m_hat / (jnp.sqrt(v_hat) + ADAM_EPS) + ADAM_WD * w), m, v


def _adamw_big_layer(layer, w, m, v, own, sib, c1, prev):
    _, r, cd = w.shape
    rc = r // 2 // ROW_CHUNKS

    def body(c_ref, w_ref, m_ref, v_ref, own_ref, sib_ref, *rest):
        g_ref, d_ref, mo_ref, vo_ref, token = rest[-5:]
        g = jnp.where(pl.program_id(0) == c_ref[0], own_ref[...], sib_ref[...])
        g_ref[...] = g
        d_ref[...], mo_ref[...], vo_ref[...] = _adamw_math(w_ref[...], g, m_ref[...], v_ref[...])
        token[...] = jnp.zeros(token.shape, F32)

    blk = pl.BlockSpec((None, rc, cd), lambda hh, j, c_ref: (layer, hh * ROW_CHUNKS + j, 0))
    half = pl.BlockSpec((rc, cd), lambda hh, j, c_ref: (j, 0))
    prev = () if prev is None else tuple(prev)
    outs = pl.pallas_call(
        body, name="adamw_big",
        grid_spec=pltpu.PrefetchScalarGridSpec(
            num_scalar_prefetch=1, grid=(2, ROW_CHUNKS), in_specs=[blk, blk, blk, half, half] + [ANY] * len(prev),
            out_specs=[blk] * 4 + [pl.BlockSpec((8, 128), lambda hh, j, c_ref: (0, 0))]),
        out_shape=[jax.ShapeDtypeStruct(w.shape, F32)] * 4 + [jax.ShapeDtypeStruct((8, 128), F32)],
        input_output_aliases={6 + i: i for i in range(len(prev))},
        compiler_params=_params("arbitrary", "arbitrary"),
    )(c1, w, m, v, own, sib, *prev)
    return outs[:4], outs[4]


def _adamw_small(ws, gs, ms, vs):
    n = len(ws)

    def body(*refs):
        w_refs, g_refs, m_refs, v_refs = refs[:n], refs[n:2 * n], refs[2 * n:3 * n], refs[3 * n:4 * n]
        outs = refs[4 * n:]
        for i in range(n):
            outs[3 * i][...], outs[3 * i + 1][...], outs[3 * i + 2][...] = _adamw_math(
                w_refs[i][...], g_refs[i][...], m_refs[i][...], v_refs[i][...])

    vm = pl.BlockSpec(memory_space=pltpu.VMEM)
    outs = pl.pallas_call(
        body, name="adamw_small", in_specs=[vm] * (4 * n), out_specs=[vm] * (3 * n),
        out_shape=[jax.ShapeDtypeStruct(w.shape, F32) for w in ws for _ in range(3)],
        compiler_params=pltpu.CompilerParams(vmem_limit_bytes=VMEM_LIMIT),
    )(*ws, *gs, *ms, *vs)
    return [outs[3 * i:3 * i + 3] for i in range(n)]


LANES = 128
SUBLANES = 8
SHARDED_AXIS = {"meta_tokens": 1, "convb_dw_w": 2, "convb_pw_w": 1, "rg_conv_w": 2}


def _rows_of(size):
    return -(-size // (LANES * SUBLANES)) * SUBLANES


def _as_rows(a, rows=None):
    flat = a.reshape(-1)
    rows = _rows_of(flat.size) if rows is None else rows
    return jnp.pad(flat, (0, rows * LANES - flat.size)).reshape(rows, LANES)


class _GradientSchedule:
    GROUPS = {"l1": [(1, "w_down"), (1, "w_up"), (1, "w_out"), (1, "w_in")], "a0": [(0, "w_down"), (0, "w_up")],
              "b0": [(0, "w_out")], "c0": [(0, "w_in")]}
    PLAN = {
        (1, "dw_in"): [("l1", "start")],
        (0, "mlp_bwd"): [("l1", "to_chips")],
        (0, "dw_up"): [("l1", "to_sibling"), ("a0", "start")],
        (0, "dw_out"): [("l1", "finish"), ("a0", "to_chips"), ("b0", "start")],
        (0, "mixer_bwd"): [("a0", "to_sibling"), ("b0", "to_chips"), ("small", "step")],
        (0, "dw_in"): [("c0", "start"), ("small", "step"), ("c0", "to_chips"), ("a0", "finish"), ("b0", "to_sibling")],
    }

    def __init__(self, w, mom, var, c1, me1):
        self.w, self.mom, self.var, self.c1, self.me1 = w, mom, var, c1, me1
        self.grads, self.chains, self.out = {}, {}, {}

    def grad(self, layer, name, g):
        self.grads[layer, name] = g

    def small(self, gsmall):
        self.small_sum = _AllReduceSmall([g.reshape(1, -1) if g.ndim == 1 else g for g in (gsmall[k] for k in SMALL)])
        return (self.small_sum.start(),)

    def point(self, layer, kernel_name, after):
        return self.run(self.PLAN.get((layer, kernel_name), ()), after) or (after,)

    def run(self, actions, after):
        deps = []
        for tag, stage in actions:
            if tag == "small":
                deps.append(self.small_sum.step(after))
            elif stage == "start":
                self.chains[tag] = _ReduceScatter(tag, [self.grads[lk] for lk in self.GROUPS[tag]], self.c1, self.me1)
                deps.append(self.chains[tag].start())
            elif stage == "finish":
                for (layer, k), (own, sib) in zip(self.GROUPS[tag], self.chains[tag].finish(after)):
                    self.out[k], token = _adamw_big_layer(layer, self.w[k], self.mom[k], self.var[k], own, sib, self.c1,
                                                          self.out.get(k))
                    deps.append(token)
            else:
                deps.append(getattr(self.chains[tag], stage)(after))
            after = deps[-1]
        self.last = after
        return tuple(deps)


def _from_shard_major(name, sm):
    if name == "meta_tokens":
        return sm.transpose(1, 0, 2).reshape(N_META, -1)
    if name == "convb_pw_w":
        return sm.transpose(1, 0, 2, 3).reshape(2, -1, D_CONV)
    return sm.transpose(1, 2, 0, 3).reshape(sm.shape[1], sm.shape[2], -1)


def kernel(x, meta_tokens, mix_norm_g, w_in, pool_w, pool_scale, convb_dw_w, convb_dw_b, convb_ln_g, convb_ln_b, convb_pw_w, rg_conv_w, rg_conv_b, rg_w_a, rg_b_a, rg_w_x, rg_b_x, rg_lambda, w_out, mlp_norm_g, w_up, w_down, final_norm_g, loss_target, m_meta_tokens, m_mix_norm_g, m_w_in, m_pool_w, m_pool_scale, m_convb_dw_w, m_convb_dw_b, m_convb_ln_g, m_convb_ln_b, m_convb_pw_w, m_rg_conv_w, m_rg_conv_b, m_rg_w_a, m_rg_b_a, m_rg_w_x, m_rg_b_x, m_rg_lambda, m_w_out, m_mlp_norm_g, m_w_up, m_w_down, m_final_norm_g, v_meta_tokens, v_mix_norm_g, v_w_in, v_pool_w, v_pool_scale, v_convb_dw_w, v_convb_dw_b, v_convb_ln_g, v_convb_ln_b, v_convb_pw_w, v_rg_conv_w, v_rg_conv_b, v_rg_w_a, v_rg_b_a, v_rg_w_x, v_rg_b_x, v_rg_lambda, v_w_out, v_mlp_norm_g, v_w_up, v_w_down, v_final_norm_g):
    given = dict(locals())
    w = {k: given[k] for k in WEIGHTS}
    mom = {k: given["m_" + k] for k in WEIGHTS}
    var = {k: given["v_" + k] for k in WEIGHTS}
    xi, yi, ci = _place()
    me1 = (2 * xi + yi).astype(jnp.int32).reshape(1)
    c1 = ci.astype(jnp.int32).reshape(1)

    small_rows = [_rows_of(w[k].size) for k in SMALL_SHARDED]
    small_pack = jnp.concatenate([_as_rows(w[k]) for k in SMALL_SHARDED])
    transposed = lambda d: {**d, "w_in": d["w_in"].transpose(0, 2, 1)}
    wt, momt, vart = transposed(w), transposed(mom), transposed(var)
    order = [[(0, "w_in"), "small"], [(0, "w_out"), (0, "w_up")], [(0, "w_down")], [(1, "w_in")], [(1, "w_out"), (1, "w_up")],
             [(1, "w_down")]]
    state, token = _gather_start([[wt["w_in"][0].astype(BF16), small_pack]], me1[0], "gather_start_0")
    shard = lambda l, k: (wt[k][l] + token[0, 0]).astype(BF16)
    rest, token_rest = _gather_start([[shard(*lk) for lk in g] for g in order[1:]], me1[0], "gather_start_1")
    state = state + rest
    landed = {}

    def fetch(l, k, after):
        gi = [i for i, g in enumerate(order) if (l, k) in g][0]
        if gi not in landed:
            landed[gi] = _gather_wait(state[gi], after, "gather_wait_%d" % gi)
        raw = landed[gi][order[gi].index((l, k))]
        if k == "w_in":
            return raw.reshape(D_IN, D_MODEL)
        return raw.reshape(D_MODEL, D_MODEL) if k == "w_out" else raw

    seq = x.shape[1]
    t_real = N_META + seq
    t_pad = -(-t_real // ROW_ALIGN) * ROW_ALIGN
    landed[0] = _gather_wait(state[0], token_rest, "gather_wait_0")
    wfull = {k: (w[k] + token[0, 0] if k in ("pool_w", "rg_w_a", "rg_w_x") else w[k]) for k in WEIGHTS}
    off = 0
    for k, rows in zip(SMALL_SHARDED, small_rows):
        sm = landed[0][1][:, off:off + rows].reshape(N_CHIPS, -1)[:, :w[k].size].reshape((N_CHIPS,) + w[k].shape)
        wfull[k] = _from_shard_major(k, sm)
        off += rows
    sched = _GradientSchedule(wt, momt, vart, c1, me1)
    loss, dh = _local_step(x[0], loss_target[0], t_real, t_pad, wfull, fetch, sched)
    grad_x = dh[None]

    names = SMALL
    two_d = lambda a: a.reshape(1, -1) if a.ndim == 1 else a
    sched.run([("small", "step")], sched.last)
    summed = dict(zip(names, sched.small_sum.vs))
    for k in SMALL_SHARDED:
        ax = SHARDED_AXIS[k]
        summed[k] = lax.dynamic_slice_in_dim(summed[k], me1[0] * w[k].shape[ax], w[k].shape[ax], axis=ax)

    out = {}
    res = _adamw_small([two_d(w[k]) for k in names], [summed[k] for k in names], [two_d(mom[k]) for k in names],
                       [two_d(var[k]) for k in names])
    for k, (d, m2, v2) in zip(names, res):
        out[k] = tuple(o.reshape(w[k].shape) for o in (summed[k], d, m2, v2))
    sched.run([("b0", "finish"), ("c0", "to_sibling"), ("c0", "finish")], res[0][0])
    out.update(sched.out)
    out["w_in"] = tuple(o.transpose(0, 2, 1) for o in out["w_in"])

    loss = lax.psum(loss, ("x", "y", "c"))
    return (loss, grad_x, *[out[k][0] for k in WEIGHTS], *[out[k][1] for k in WEIGHTS],
            *[out[k][2] for k in WEIGHTS], *[out[k][3] for k in WEIGHTS])
```

```python
import functools

import jax
import jax.numpy as jnp
from jax import lax
from jax.experimental import pallas as pl
from jax.experimental.pallas import tpu as pltpu

F32, BF16 = jnp.float32, jnp.bfloat16
MESH = pl.DeviceIdType.MESH
ANY = pl.BlockSpec(memory_space=pl.ANY)

D_MODEL = 1024
N_META = 16
D_POOL = 256
D_CONV = 256
D_RNN = 512
D_IN = D_POOL + 2 * D_CONV + 2 * D_RNN
D_FF = 4096
FF_CHUNK = 1024
POOL_GW = 64
CONV_K = 31
RG_CONV_K = 4
RG_HD = 64
RG_C = 8.0
EPS = 1e-6
ADAM_LR, ADAM_B1, ADAM_B2, ADAM_EPS, ADAM_WD, ADAM_STEP = 0.001, 0.9, 0.999, 1e-08, 0.01, 10

HALO = 32
ROW_ALIGN = 256
TM_MIX = 384
TM_MAT = 768
TM_MLP_BWD = 384
N_CHIPS = 4
ROW_CHUNKS = 1
VMEM_LIMIT = 56 * 1024 * 1024

BIG = ("w_in", "w_out", "w_up", "w_down")
SMALL_SHARDED = ("meta_tokens", "convb_dw_w", "convb_pw_w", "rg_conv_w")
SMALL_REPL = ("mix_norm_g", "pool_w", "pool_scale", "convb_dw_b", "convb_ln_g", "convb_ln_b", "rg_conv_b",
              "rg_w_a", "rg_b_a", "rg_w_x", "rg_b_x", "rg_lambda", "mlp_norm_g", "final_norm_g")
SMALL = SMALL_REPL + SMALL_SHARDED
WEIGHTS = ("meta_tokens", "mix_norm_g", "w_in", "pool_w", "pool_scale", "convb_dw_w", "convb_dw_b", "convb_ln_g",
           "convb_ln_b", "convb_pw_w", "rg_conv_w", "rg_conv_b", "rg_w_a", "rg_b_a", "rg_w_x", "rg_b_x",
           "rg_lambda", "w_out", "mlp_norm_g", "w_up", "w_down", "final_norm_g")


def _params(*sem):
    return pltpu.CompilerParams(dimension_semantics=sem, vmem_limit_bytes=VMEM_LIMIT)


def _row_tile(t, cap):
    best = None
    for tm in range(128, cap + 1, 128):
        if t % tm == 0:
            best = tm
    assert best is not None, (t, cap)
    return best


def _dot(a, b):
    return jnp.dot(a, b, preferred_element_type=F32)


def _dot_nt(a, b):
    return lax.dot_general(a, b, (((1,), (1,)), ((), ())), preferred_element_type=F32)


def _dot_tn(a, b):
    return lax.dot_general(a, b, (((0,), (0,)), ((), ())), preferred_element_type=F32)


def _rms(x):
    r = lax.rsqrt(jnp.mean(x * x, axis=-1, keepdims=True) + EPS)
    return r, x * r


def _rms_bwd(du, n, r, g):
    dn = du * g
    return r * (dn - n * jnp.mean(dn * n, axis=-1, keepdims=True))


def _sig(x):
    return jax.nn.sigmoid(x)


def _colsum(x):
    return jnp.sum(x, axis=0, keepdims=True)


def _one_minus_sq(a, log_a):
    x = 2.0 * log_a
    series = -x * (1.0 + x * (0.5 + x * (1.0 / 6)))
    return jnp.where(x > -0.01, series, 1.0 - a * a)


_GELU_K0 = 0.7978845608028654
_GELU_K1 = 0.044715


def _gelu_and_grad(x):
    th = jnp.tanh(_GELU_K0 * (x + _GELU_K1 * x * x * x))
    val = 0.5 * x * (1.0 + th)
    grad = 0.5 * (1.0 + th) + 0.5 * x * (1.0 - th * th) * _GELU_K0 * (1.0 + 3.0 * _GELU_K1 * x * x)
    return val, grad


def _full(a):
    nd = a.ndim
    return pl.BlockSpec(a.shape, lambda *_: (0,) * nd)


def _resident(a):
    nd = a.ndim
    return pl.BlockSpec(a.shape, lambda *_: (0,) * nd, pipeline_mode=pl.Buffered(1))


def _after(body, n_in, deps):
    def wrapped(*refs):
        return body(*refs[:n_in], *refs[n_in + len(deps):])
    return wrapped


def _lane_sel(lane, a2, a4, a8, a16):
    return jnp.where(lane < POOL_GW, a2, jnp.where(lane < 2 * POOL_GW, a4, jnp.where(lane < 3 * POOL_GW, a8, a16)))


def _window_sums_back(src, tmp_a, tmp_b, tm):
    n = HALO + tm
    rows = lambda ref, lo, back: ref[pl.ds(lo - back, n - lo), :]
    tmp_a[pl.ds(8, n - 8), :] = rows(src, 8, 0) + rows(src, 8, 1)
    tmp_b[pl.ds(16, n - 16), :] = rows(tmp_a, 16, 0) + rows(tmp_a, 16, 2)
    s2 = rows(tmp_a, HALO, 0)
    tmp_a[pl.ds(24, n - 24), :] = rows(tmp_b, 24, 0) + rows(tmp_b, 24, 4)
    s8 = rows(tmp_a, HALO, 0)
    return s2, rows(tmp_b, HALO, 0), s8, s8 + rows(tmp_a, HALO, 8)


def _window_sums_ahead(src, tmp_a, tmp_b, tm):
    rows = lambda ref, n, ahead: ref[pl.ds(ahead, n), :]
    tmp_a[pl.ds(0, tm + 24), :] = rows(src, tm + 24, 0) + rows(src, tm + 24, 1)
    tmp_b[pl.ds(0, tm + 16), :] = rows(tmp_a, tm + 16, 0) + rows(tmp_a, tm + 16, 2)
    s2 = rows(tmp_a, tm, 0)
    tmp_a[pl.ds(0, tm + 8), :] = rows(tmp_b, tm + 8, 0) + rows(tmp_b, tm + 8, 4)
    s8 = rows(tmp_a, tm, 0)
    return s2, rows(tmp_b, tm, 0), s8, s8 + rows(tmp_a, tm, 8)


def _pool_counts(tm, t0):
    lane = lax.broadcasted_iota(jnp.int32, (tm, D_POOL), 1)
    row = lax.broadcasted_iota(jnp.int32, (tm, D_POOL), 0) + t0
    cnt = jnp.minimum(row + 1, _lane_sel(lane, 2, 4, 8, 16)).astype(F32)
    return lane, cnt


def _pool_fwd(ext_q, tmp_a, tmp_b, tm, t0):
    lane, cnt = _pool_counts(tm, t0)
    q = ext_q[pl.ds(HALO, tm), :]
    pooled = _lane_sel(lane, *_window_sums_back(ext_q, tmp_a, tmp_b, tm)) / cnt - q
    return pooled, lane, cnt


def _taps(src, w_of, offs, tm, zbuf):
    acc = None
    for r in range(8):
        ks = [k for k in range(len(offs)) if offs[k] % 8 == r]
        if not ks:
            continue
        rows = tm + (8 if r else 0)
        z = w_of(ks[0]) * src[pl.ds(offs[ks[0]] - r, rows), :]
        for k in ks[1:]:
            z = z + w_of(k) * src[pl.ds(offs[k] - r, rows), :]
        if r:
            zbuf[...] = z
            z = zbuf[pl.ds(r, tm), :]
        acc = z if acc is None else acc + z
    return acc


def _tap_grads(d_pad, src, offs, tm, g_ref, zbuf):
    ch = src.shape[-1]
    for r in range(8):
        ks = [k for k in range(len(offs)) if offs[k] % 8 == r]
        if not ks:
            continue
        rows = tm + (8 if r else 0)
        if r:
            zbuf[...] = d_pad[pl.ds(8 - r, rows), :]
        for k in ks:
            d = zbuf[...] if r else d_pad[pl.ds(8, rows), :]
            prod = d * src[pl.ds(offs[k] - r, rows), :]
            g_ref[k] += jnp.sum(prod.reshape(rows // 8, 8, ch), axis=0)


_CONV_OFFS = [HALO - (CONV_K - 1) + k for k in range(CONV_K)]


def _conv_fwd(ext_u, dww_ref, dwb, tm, zbuf):
    return dwb + _taps(ext_u, lambda k: dww_ref[k:k + 1, :], _CONV_OFFS, tm, zbuf)


def _ln_silu(c, lng, lnb):
    mu = jnp.mean(c, axis=-1, keepdims=True)
    cc = c - mu
    rstd = lax.rsqrt(jnp.mean(cc * cc, axis=-1, keepdims=True) + EPS)
    z = cc * rstd
    l = z * lng + lnb
    sl = _sig(l)
    return z, rstd, l, sl, l * sl


def _rg_conv(ext_x, cw_ref, cb, tm):
    xc = cb + cw_ref[0:1, :] * ext_x[pl.ds(HALO - (RG_CONV_K - 1), tm), :]
    for k in range(1, RG_CONV_K):
        xc = xc + cw_ref[k:k + 1, :] * ext_x[pl.ds(HALO - (RG_CONV_K - 1) + k, tm), :]
    return xc


def _softplus_neg(lam):
    return jnp.maximum(-lam, 0.0) + jnp.log(1.0 + jnp.exp(-jnp.abs(lam)))


def _rg_gates(xc, wa, ba, wx, bx, lam):
    xcb = xc.astype(BF16)
    r = _sig(_dot(xcb, wa) + ba)
    ig = _sig(_dot(xcb, wx) + bx)
    log_a = (-RG_C * r) * _softplus_neg(lam)
    a = jnp.exp(log_a)
    return r, ig, a, jnp.sqrt(_one_minus_sq(a, log_a))


def _scan_rows(a_ref, b_ref, out_ref, carry, tm, reverse):
    rows = lax.broadcasted_iota(jnp.int32, (8, D_RNN), 0)
    ngrp = tm // 8

    def grp(gi, hb):
        st = pl.multiple_of((ngrp - 1 - gi if reverse else gi) * 8, 8)
        a8 = a_ref[pl.ds(st, 8), :]
        b8 = b_ref[pl.ds(st, 8), :]
        out = jnp.zeros((8, D_RNN), F32)
        for j in (range(7, -1, -1) if reverse else range(8)):
            aj = jnp.broadcast_to(a8[j:j + 1, :], (8, D_RNN))
            bj = jnp.broadcast_to(b8[j:j + 1, :], (8, D_RNN))
            if reverse:
                cur = bj + hb
                hb = aj * cur
            else:
                cur = aj * hb + bj
                hb = cur
            out = jnp.where(rows == j, cur, out)
        out_ref[pl.ds(st, 8), :] = out
        return hb

    carry[...] = lax.fori_loop(0, ngrp, grp, carry[...])


_MIX_W = ("wp", "psc", "dww", "dwb", "lng", "lnb", "wpw", "cw", "cb", "wa", "ba", "wx", "bx", "lam")


def _lead_tile(src_ref, dst, carry, i, nt, tm, lead, n_src):
    last = n_src - (nt - 1) * tm
    assert 0 < last and lead + last <= tm, (n_src, nt, tm)
    dst[0:lead, :] = carry[...]

    @pl.when(i < nt - 1)
    def _():
        dst[lead:tm, :] = src_ref[0:tm - lead, :]
        carry[...] = src_ref[tm - lead:tm, :]

    @pl.when(i == nt - 1)
    def _():
        dst[lead:lead + last, :] = src_ref[0:last, :]
        if lead + last < tm:
            dst[lead + last:tm, :] = jnp.zeros((tm - lead - last, dst.shape[1]), dst.dtype)


def _mixer_fwd(h, g, w_in, mw, lead=None, t_pad=None):
    t = h.shape[0] if lead is None else t_pad
    tm = _row_tile(t, TM_MIX)
    nt = t // tm
    n_lead = 0 if lead is None else lead.shape[0]

    def body(h_ref, g_ref, win_ref, wp, psc, dww, dwb, lng, lnb, wpw, cw, cb, wa, ba, wx, bx, lam, *rest):
        if n_lead:
            lead_ref, rest = rest[0], rest[1:]
            h_out, h_carry = rest[7], rest[-1]
            rest = rest[:7] + rest[8:-1]
        (y_ref, p_ref, u_ref, hs_ref, conv_ref, xc_ref, gates_ref,
         ext_q, ext_u, ext_x, tmp_a, tmp_b, zbuf, a_s, b_s, hcar) = rest
        i = pl.program_id(0)

        @pl.when(i == 0)
        def _():
            ext_q[0:HALO, :] = jnp.zeros((HALO, D_POOL), F32)
            ext_u[0:HALO, :] = jnp.zeros((HALO, D_CONV), F32)
            ext_x[0:HALO, :] = jnp.zeros((HALO, D_RNN), F32)
            hcar[...] = jnp.zeros((8, D_RNN), F32)
            if n_lead:
                h_carry[...] = lead_ref[...]

        if n_lead:
            _lead_tile(h_ref, h_out, h_carry, i, nt, tm, n_lead, h.shape[0])
            h_ref = h_out
        u = (_rms(h_ref[...])[1] * g_ref[...]).astype(BF16)
        u_ref[...] = u
        p_ref[...] = _dot_nt(u, win_ref[...])

        ext_q[pl.ds(HALO, tm), :] = p_ref[:, 0:256]
        pooled, _, _ = _pool_fwd(ext_q, tmp_a, tmp_b, tm, i * tm)
        y_ref[:, 0:256] = (_dot(pooled.astype(BF16), wp[...]) * psc[...]).astype(BF16)

        ext_u[pl.ds(HALO, tm), :] = p_ref[:, 256:512] * _sig(p_ref[:, 512:768])
        conv = _conv_fwd(ext_u, dww, dwb[...], tm, zbuf)
        conv_ref[...] = conv
        act = _ln_silu(conv, lng[...], lnb[...])[4]
        y_ref[:, 256:512] = _dot(act.astype(BF16), wpw[...]).astype(BF16)

        ext_x[pl.ds(HALO, tm), :] = p_ref[:, 1280:1792]
        xc = _rg_conv(ext_x, cw, cb[...], tm)
        xc_ref[...] = xc
        r, ig, a, m = _rg_gates(xc, wa[...], ba[...], wx[...], bx[...], lam[...])
        for j, gate in enumerate((r, ig, a, m)):
            gates_ref[:, j * D_RNN:(j + 1) * D_RNN] = gate
        a_s[...] = a
        b_s[...] = m * (ig * xc)
        _scan_rows(a_s, b_s, hs_ref, hcar, tm, reverse=False)
        y_ref[:, 512:1024] = (_gelu_and_grad(p_ref[:, 768:1280])[0] * hs_ref[...]).astype(BF16)

        ext_q[0:HALO, :] = ext_q[pl.ds(tm, HALO), :]
        ext_u[0:HALO, :] = ext_u[pl.ds(tm, HALO), :]
        ext_x[0:HALO, :] = ext_x[pl.ds(tm, HALO), :]

    ws = [mw[k] for k in _MIX_W]
    row = lambda w: pl.BlockSpec((tm, w), lambda i: (i, 0))
    extra = [] if lead is None else [lead]
    return pl.pallas_call(
        body, name="mixer_fwd", grid=(nt,),
        in_specs=[row(D_MODEL), _full(g), _resident(w_in)] + [_full(w) for w in ws] + [_full(a) for a in extra],
        out_specs=[row(D_MODEL), row(D_IN), row(D_MODEL), row(D_RNN), row(D_CONV), row(D_RNN), row(4 * D_RNN)]
        + [row(D_MODEL)] * len(extra),
        out_shape=[jax.ShapeDtypeStruct((t, D_MODEL), BF16), jax.ShapeDtypeStruct((t, D_IN), F32),
                   jax.ShapeDtypeStruct((t, D_MODEL), BF16), jax.ShapeDtypeStruct((t, D_RNN), F32),
                   jax.ShapeDtypeStruct((t, D_CONV), F32), jax.ShapeDtypeStruct((t, D_RNN), F32),
                   jax.ShapeDtypeStruct((t, 4 * D_RNN), F32)] + [jax.ShapeDtypeStruct((t, D_MODEL), F32)] * len(extra),
        scratch_shapes=[pltpu.VMEM((HALO + tm, D_POOL), F32), pltpu.VMEM((HALO + tm, D_CONV), F32),
                        pltpu.VMEM((HALO + tm, D_RNN), F32), pltpu.VMEM((HALO + tm, D_POOL), F32),
                        pltpu.VMEM((HALO + tm, D_POOL), F32), pltpu.VMEM((tm + 8, D_CONV), F32),
                        pltpu.VMEM((tm, D_RNN), F32), pltpu.VMEM((tm, D_RNN), F32), pltpu.VMEM((8, D_RNN), F32)]
        + [pltpu.VMEM(a.shape, F32) for a in extra],
        compiler_params=_params("arbitrary"),
    )(h, g, w_in, *ws, *extra)


_MIX_G = (("wp", (D_POOL, D_POOL)), ("psc", (1, D_POOL)), ("dww", (32, 8, D_CONV)), ("dwb", (1, D_CONV)),
          ("lng", (1, D_CONV)), ("lnb", (1, D_CONV)), ("wpw", (D_CONV, D_CONV)), ("cw", (8, D_RNN)),
          ("cb", (1, D_RNN)), ("wa", (D_RNN, D_RNN)), ("ba", (1, D_RNN)), ("wx", (D_RNN, D_RNN)),
          ("bx", (1, D_RNN)), ("lam", (1, D_RNN)), ("g1", (1, D_MODEL)))


def _mixer_bwd(p, dh1, hs, conv, xc, gates, h0, g1, w_out, w_in, mw, deps=(), lead=0, t_real=None):
    t = p.shape[0]
    tm = _row_tile(t, TM_MIX)
    nt = t // tm
    hb = tm // HALO
    t_out = t_real - lead if lead else t

    def body(p_ref, ph_ref, dh1_ref, hs_ref, hsh_ref, conv_ref, xc_ref, gates_ref, h0_ref, g1_ref, wout_ref, win_ref,
             wp, psc, dww, dwb, lng, lnb, wpw, cw, cb, wa, ba, wx, bx, lam,
             dp_ref, dh0_ref, g_wp, g_psc, g_dww, g_dwb, g_lng, g_lnb, g_wpw, g_cw, g_cb, g_wa, g_ba, g_wx, g_bx, g_lam, g_g1,
             *tail):
        dlead_ref, carry = (tail[0], tail[-1]) if lead else (None, None)
        (ext_q, ext_u, ext_x, ext_h, ee, dc_s, dx_s, tmp_a, tmp_b, zbuf, d_pad, a_s, b_s, g_s, gcar, dy_ref,
         dp_s) = tail[1:-1] if lead else tail
        step = pl.program_id(0)
        i = nt - 1 - step
        grads = (g_wp, g_psc, g_dww, g_dwb, g_lng, g_lnb, g_wpw, g_cw, g_cb, g_wa, g_ba, g_wx, g_bx, g_lam, g_g1)
        if lead:
            grads += (carry,)
        dy_ref[...] = dh1_ref[...].astype(BF16)
        dy_cols = lambda lo, hi: _dot_nt(dy_ref[...], wout_ref[lo:hi, :])

        @pl.when(step == 0)
        def _():
            for gr in grads:
                gr[...] = jnp.zeros(gr.shape, F32)
            ee[pl.ds(tm, HALO), :] = jnp.zeros((HALO, D_POOL), F32)
            dc_s[pl.ds(tm, HALO), :] = jnp.zeros((HALO, D_CONV), F32)
            dx_s[pl.ds(tm, HALO), :] = jnp.zeros((HALO, D_RNN), F32)
            d_pad[0:8, :] = jnp.zeros((8, D_CONV), F32)
            d_pad[pl.ds(tm + 8, 8), :] = jnp.zeros((8, D_CONV), F32)
            gcar[...] = jnp.zeros((8, D_RNN), F32)

        hm = jnp.where(i == 0, 0.0, 1.0)

        ext_q[0:HALO, :] = ph_ref[:, 0:256] * hm
        ext_q[pl.ds(HALO, tm), :] = p_ref[:, 0:256]
        pooled, lane, cnt = _pool_fwd(ext_q, tmp_a, tmp_b, tm, i * tm)
        pooled_b = pooled.astype(BF16)
        dya = dy_cols(0, 256)
        g_psc[...] += _colsum(dya * _dot(pooled_b, wp[...]))
        dmixed_b = (dya * psc[...]).astype(BF16)
        dpooled = _dot_nt(dmixed_b, wp[...])
        g_wp[...] += _dot_tn(pooled_b, dmixed_b)
        ee[0:tm, :] = dpooled / cnt
        dp_s[:, 0:256] = _lane_sel(lane, *_window_sums_ahead(ee, tmp_a, tmp_b, tm)) - dpooled
        ee[pl.ds(tm, HALO), :] = ee[0:HALO, :]

        v = p_ref[:, 256:512]
        s = _sig(p_ref[:, 512:768])
        ext_u[0:HALO, :] = ph_ref[:, 256:512] * _sig(ph_ref[:, 512:768]) * hm
        ext_u[pl.ds(HALO, tm), :] = v * s
        z, rstd, l, sl, act = _ln_silu(conv_ref[...], lng[...], lnb[...])
        dyb_b = dy_cols(256, 512).astype(BF16)
        dact = _dot_nt(dyb_b, wpw[...])
        g_wpw[...] += _dot_tn(act.astype(BF16), dyb_b)
        dl = dact * (sl * (1.0 + l * (1.0 - sl)))
        g_lng[...] += _colsum(dl * z)
        g_lnb[...] += _colsum(dl)
        dz = dl * lng[...]
        dc = rstd * (dz - jnp.mean(dz, axis=-1, keepdims=True) - z * jnp.mean(dz * z, axis=-1, keepdims=True))
        g_dwb[...] += _colsum(dc)
        dc_s[0:tm, :] = dc
        d_pad[pl.ds(8, tm), :] = dc
        _tap_grads(d_pad, ext_u, _CONV_OFFS, tm, g_dww, zbuf)
        du0 = _taps(dc_s, lambda j: dww[CONV_K - 1 - j:CONV_K - j, :], list(range(CONV_K)), tm, zbuf)
        dp_s[:, 256:512] = du0 * s
        dp_s[:, 512:768] = du0 * v * (s * (1.0 - s))
        dc_s[pl.ds(tm, HALO), :] = dc_s[0:HALO, :]

        ext_x[0:HALO, :] = ph_ref[:, 1280:1792] * hm
        ext_x[pl.ds(HALO, tm), :] = p_ref[:, 1280:1792]
        xc = xc_ref[...]
        xcb = xc.astype(BF16)
        r, ig, a, m = (gates_ref[:, j * D_RNN:(j + 1) * D_RNN] for j in range(4))
        sp = _softplus_neg(lam[...])
        ext_h[0:HALO, :] = hsh_ref[...] * hm
        ext_h[pl.ds(HALO, tm), :] = hs_ref[...]
        dyc = dy_cols(512, 1024)
        gl, dgl = _gelu_and_grad(p_ref[:, 768:1280])
        dp_s[:, 768:1280] = dyc * hs_ref[...] * dgl
        a_s[...] = a
        b_s[...] = dyc * gl
        _scan_rows(a_s, b_s, g_s, gcar, tm, reverse=True)
        g = g_s[...]
        da = g * ext_h[pl.ds(HALO - 1, tm), :]
        dm = g * (ig * xc)
        dig = g * (m * xc)
        dlog_a = da * a - dm * (a * a) / m
        g_lam[...] += _colsum(dlog_a * (-RG_C * r)) * (-_sig(-lam[...]))
        dra = (dlog_a * (-RG_C * sp)) * (r * (1.0 - r))
        dia = dig * (ig * (1.0 - ig))
        g_ba[...] += _colsum(dra)
        g_bx[...] += _colsum(dia)
        dra_b = dra.astype(BF16)
        dia_b = dia.astype(BF16)
        dxc = g * (m * ig) + _dot_nt(dra_b, wa[...]) + _dot_nt(dia_b, wx[...])
        g_wa[...] += _dot_tn(xcb, dra_b)
        g_wx[...] += _dot_tn(xcb, dia_b)
        g_cb[...] += _colsum(dxc)
        dx_s[0:tm, :] = dxc
        for k in range(RG_CONV_K):
            g_cw[k:k + 1, :] += _colsum(dxc * ext_x[pl.ds(HALO - (RG_CONV_K - 1) + k, tm), :])
        dxin = cw[RG_CONV_K - 1:RG_CONV_K, :] * dxc
        for j in range(1, RG_CONV_K):
            dxin = dxin + cw[RG_CONV_K - 1 - j:RG_CONV_K - j, :] * dx_s[pl.ds(j, tm), :]
        dp_s[:, 1280:1792] = dxin
        dx_s[pl.ds(tm, HALO), :] = dx_s[0:HALO, :]

        dpb = dp_s[...].astype(BF16)
        dp_ref[...] = dpb
        du = _dot(dpb, win_ref[...])
        r, n = _rms(h0_ref[...])
        g_g1[...] += _colsum(du * n)
        dh0 = dh1_ref[...] + _rms_bwd(du, n, r, g1_ref[...])
        if lead:
            dh0_ref[0:tm - lead, :] = dh0[lead:tm, :]
            dh0_ref[tm - lead:tm, :] = carry[...]
            carry[...] = dh0[0:lead, :]

            @pl.when(i == 0)
            def _():
                dlead_ref[...] = dh0[0:lead, :]
        else:
            dh0_ref[...] = dh0

    ws = [mw[k] for k in _MIX_W]
    tile = lambda w: pl.BlockSpec((tm, w), lambda s: (nt - 1 - s, 0))
    halo = lambda w: pl.BlockSpec((HALO, w), lambda s: (jnp.maximum((nt - 1 - s) * hb - 1, 0), 0))
    lead_out = [pl.BlockSpec((lead, D_MODEL), lambda s: (0, 0))] if lead else []
    outs = pl.pallas_call(
        _after(body, 12 + len(ws), deps), name="mixer_bwd", grid=(nt,),
        in_specs=[tile(D_IN), halo(D_IN), tile(D_MODEL), tile(D_RNN), halo(D_RNN), tile(D_CONV), tile(D_RNN), tile(4 * D_RNN),
                  tile(D_MODEL), _full(g1),
                  _resident(w_out), _resident(w_in)] + [_full(w) for w in ws] + [ANY] * len(deps),
        out_specs=[tile(D_IN), tile(D_MODEL)] + [pl.BlockSpec(shp, lambda s, nd=len(shp): (0,) * nd) for _, shp in _MIX_G]
        + lead_out,
        out_shape=[jax.ShapeDtypeStruct((t, D_IN), BF16), jax.ShapeDtypeStruct((t_out, D_MODEL), F32)]
        + [jax.ShapeDtypeStruct(shp, F32) for _, shp in _MIX_G] + [jax.ShapeDtypeStruct((lead, D_MODEL), F32)] * bool(lead),
        scratch_shapes=[pltpu.VMEM((HALO + tm, D_POOL), F32), pltpu.VMEM((HALO + tm, D_CONV), F32),
                        pltpu.VMEM((HALO + tm, D_RNN), F32), pltpu.VMEM((HALO + tm, D_RNN), F32),
                        pltpu.VMEM((tm + HALO, D_POOL), F32), pltpu.VMEM((tm + HALO, D_CONV), F32),
                        pltpu.VMEM((tm + HALO, D_RNN), F32), pltpu.VMEM((HALO + tm, D_POOL), F32),
                        pltpu.VMEM((HALO + tm, D_POOL), F32), pltpu.VMEM((tm + 8, D_CONV), F32),
                        pltpu.VMEM((tm + 16, D_CONV), F32), pltpu.VMEM((tm, D_RNN), F32),
                        pltpu.VMEM((tm, D_RNN), F32), pltpu.VMEM((tm, D_RNN), F32), pltpu.VMEM((8, D_RNN), F32),
                        pltpu.VMEM((tm, D_MODEL), BF16), pltpu.VMEM((tm, D_IN), F32)]
        + [pltpu.VMEM((lead, D_MODEL), F32)] * bool(lead),
        compiler_params=_params("arbitrary"),
    )(p, p, dh1, hs, hs, conv, xc, gates, h0, g1, w_out, w_in, *ws, *deps)
    grads = {k: o for (k, _), o in zip(_MIX_G, outs[2:])}
    return (outs[0], (outs[1], outs[-1]), grads) if lead else (outs[0], outs[1], grads)


def _mid_fwd(y, h0, w_out, g, w_up):
    t = h0.shape[0]
    tm = _row_tile(t, TM_MAT)

    def body(y_ref, h0_ref, wo_ref, g_ref, wu_ref, h1_ref, u2_ref, f_ref):
        h1 = h0_ref[...] + _dot(y_ref[...], wo_ref[...])
        h1_ref[...] = h1
        u2 = (_rms(h1)[1] * g_ref[...]).astype(BF16)
        u2_ref[...] = u2
        for c in range(D_FF // FF_CHUNK):
            f_ref[:, c * FF_CHUNK:(c + 1) * FF_CHUNK] = _dot(u2, wu_ref[c]).astype(BF16)

    row = lambda w: pl.BlockSpec((tm, w), lambda i: (i, 0))
    return pl.pallas_call(
        body, name="mid_fwd", grid=(t // tm,),
        in_specs=[row(D_MODEL), row(D_MODEL), _resident(w_out), _full(g), _resident(w_up)],
        out_specs=[row(D_MODEL), row(D_MODEL), row(D_FF)],
        out_shape=[jax.ShapeDtypeStruct((t, D_MODEL), F32), jax.ShapeDtypeStruct((t, D_MODEL), BF16),
                   jax.ShapeDtypeStruct((t, D_FF), BF16)],
        compiler_params=_params("parallel"),
    )(y, h0, w_out, g, w_up)


def _down_proj(f_ref, h1_ref, wd_ref):
    acc = h1_ref[...]
    for c in range(D_FF // FF_CHUNK):
        cols = slice(c * FF_CHUNK, (c + 1) * FF_CHUNK)
        a = jnp.square(jnp.maximum(f_ref[:, cols].astype(F32), 0.0)).astype(BF16)
        acc = acc + _dot(a, wd_ref[cols, :])
    return acc


def _down_fwd(f, h1, w_down):
    t = h1.shape[0]
    tm = _row_tile(t, TM_MAT)

    def body(f_ref, h1_ref, wd_ref, h2_ref):
        h2_ref[...] = _down_proj(f_ref, h1_ref, wd_ref)

    row = lambda w: pl.BlockSpec((tm, w), lambda i: (i, 0))
    return pl.pallas_call(
        body, name="down_fwd", grid=(t // tm,),
        in_specs=[row(D_FF), row(D_MODEL), _resident(w_down)], out_specs=row(D_MODEL),
        out_shape=jax.ShapeDtypeStruct((t, D_MODEL), F32),
        compiler_params=_params("parallel"),
    )(f, h1, w_down)


def _down_fwd_loss(f, h1, w_down, g, tgt, t_real):
    t = h1.shape[0]
    tm = _row_tile(t, TM_MAT)
    nt = t // tm

    def body(f_ref, h1_ref, wd_ref, g_ref, tgt_in, loss_ref, dh_ref, dg_ref, tgt_ref, carry):
        i = pl.program_id(0)

        @pl.when(i == 0)
        def _():
            loss_ref[...] = jnp.zeros(loss_ref.shape, F32)
            dg_ref[...] = jnp.zeros(dg_ref.shape, F32)
            carry[...] = jnp.zeros(carry.shape, F32)

        _lead_tile(tgt_in, tgt_ref, carry, i, nt, tm, N_META, tgt.shape[0])

        r, n = _rms(_down_proj(f_ref, h1_ref, wd_ref))
        row = lax.broadcasted_iota(jnp.int32, (tm, 1), 0) + i * tm
        valid = jnp.logical_and(row >= N_META, row < t_real)
        diff = jnp.where(valid, n * g_ref[...] - tgt_ref[...], 0.0)
        loss_ref[...] += 0.5 * jnp.sum(jnp.mean(diff * diff, axis=-1, keepdims=True))
        dy = diff * (1.0 / D_MODEL)
        dg_ref[...] += _colsum(dy * n)
        dh_ref[...] = _rms_bwd(dy, n, r, g_ref[...])

    row = lambda w: pl.BlockSpec((tm, w), lambda i: (i, 0))
    return pl.pallas_call(
        body, name="down_fwd_loss", grid=(t // tm,),
        in_specs=[row(D_FF), row(D_MODEL), _resident(w_down), _full(g), row(D_MODEL)],
        out_specs=[pl.BlockSpec((8, 128), lambda i: (0, 0)), row(D_MODEL), pl.BlockSpec((1, D_MODEL), lambda i: (0, 0))],
        out_shape=[jax.ShapeDtypeStruct((8, 128), F32), jax.ShapeDtypeStruct((t, D_MODEL), F32),
                   jax.ShapeDtypeStruct((1, D_MODEL), F32)],
        scratch_shapes=[pltpu.VMEM((tm, D_MODEL), F32), pltpu.VMEM((N_META, D_MODEL), F32)],
        compiler_params=_params("arbitrary"),
    )(f, h1, w_down, g, tgt)


def _mlp_bwd(dh2, f, h1, g, w_up, w_down, deps=()):
    t = dh2.shape[0]
    tm = _row_tile(t, TM_MLP_BWD)

    def body(dh2_ref, f_ref, wd_ref, wu_ref, h1_ref, g_ref, df_ref, dh1_ref, dg_ref):
        @pl.when(pl.program_id(0) == 0)
        def _():
            dg_ref[...] = jnp.zeros(dg_ref.shape, F32)

        dh2 = dh2_ref[...]
        dhb = dh2.astype(BF16)
        du2 = None
        for c in range(D_FF // FF_CHUNK):
            cols = slice(c * FF_CHUNK, (c + 1) * FF_CHUNK)
            dact = _dot_nt(dhb, wd_ref[c])
            df = (dact * (2.0 * jnp.maximum(f_ref[:, cols].astype(F32), 0.0))).astype(BF16)
            df_ref[:, cols] = df
            part = _dot_nt(df, wu_ref[c])
            du2 = part if du2 is None else du2 + part
        r, n = _rms(h1_ref[...])
        dg_ref[...] += _colsum(du2 * n)
        dh1_ref[...] = dh2 + _rms_bwd(du2, n, r, g_ref[...])

    row = lambda w: pl.BlockSpec((tm, w), lambda i: (i, 0))
    return pl.pallas_call(
        _after(body, 6, deps), name="mlp_bwd", grid=(t // tm,),
        in_specs=[row(D_MODEL), row(D_FF), _resident(w_down), _resident(w_up), row(D_MODEL), _full(g)] + [ANY] * len(deps),
        out_specs=[row(D_FF), row(D_MODEL), pl.BlockSpec((1, D_MODEL), lambda i: (0, 0))],
        out_shape=[jax.ShapeDtypeStruct((t, D_FF), BF16), jax.ShapeDtypeStruct((t, D_MODEL), F32),
                   jax.ShapeDtypeStruct((1, D_MODEL), F32)],
        compiler_params=_params("arbitrary"),
    )(dh2, f, w_down, w_up, h1, g, *deps)


def _tn_matmul(a, b, kc, nc, relu2, name, deps=()):
    t, k = a.shape
    n = b.shape[1]
    tt = _row_tile(t, TM_MAT)
    gk, gn = k // kc, n // nc

    def body(a_ref, b_ref, o_ref):
        @pl.when(pl.program_id(2) == 0)
        def _():
            o_ref[...] = jnp.zeros(o_ref.shape, F32)

        av = a_ref[...]
        if relu2:
            av = jnp.square(jnp.maximum(av.astype(F32), 0.0))
        o_ref[...] += _dot_tn(av.astype(BF16), b_ref[...].astype(BF16))

    return pl.pallas_call(
        _after(body, 2, deps), name=name, grid=(gk, gn, t // tt),
        in_specs=[pl.BlockSpec((tt, kc), lambda ik, jn, it: (it, ik)), pl.BlockSpec((tt, nc), lambda ik, jn, it: (it, jn))]
        + [ANY] * len(deps),
        out_specs=pl.BlockSpec((None, kc, nc), lambda ik, jn, it: (ik * gn + jn, 0, 0)),
        out_shape=jax.ShapeDtypeStruct((gk * gn, kc, nc), F32),
        compiler_params=_params("parallel", "parallel", "arbitrary"),
    )(a, b, *deps)


def _block_diag(blocks):
    nb, hd, _ = blocks.shape
    eye = jnp.eye(nb, dtype=blocks.dtype)
    return (blocks[:, :, None, :] * eye[:, None, :, None]).reshape(nb * hd, nb * hd)


def _diag_blocks(m, nb):
    hd = m.shape[0] // nb
    eye = jnp.eye(nb, dtype=m.dtype)
    return jnp.sum(m.reshape(nb, hd, nb, hd) * eye[:, None, :, None], axis=2)


def _mixer_weights(w, l):
    row = lambda a: a.reshape(1, -1)
    return dict(
        wp=_block_diag(w["pool_w"][l]).astype(BF16), psc=row(w["pool_scale"][l]),
        dww=jnp.pad(w["convb_dw_w"][l], ((0, 32 - CONV_K), (0, 0))), dwb=row(w["convb_dw_b"][l]),
        lng=row(w["convb_ln_g"][l]), lnb=row(w["convb_ln_b"][l]), wpw=w["convb_pw_w"][l].astype(BF16),
        cw=jnp.pad(w["rg_conv_w"][l], ((0, 8 - RG_CONV_K), (0, 0))), cb=row(w["rg_conv_b"][l]),
        wa=_block_diag(w["rg_w_a"][l]).astype(BF16), ba=row(w["rg_b_a"][l]),
        wx=_block_diag(w["rg_w_x"][l]).astype(BF16), bx=row(w["rg_b_x"][l]), lam=row(w["rg_lambda"][l]))


def _local_step(h, tgt, t_real, t_pad, w, fetch, hooks):
    depth = 2
    saved = []
    big = []
    for l in range(depth):
        mw = _mixer_weights(w, l)
        g1 = w["mix_norm_g"][l].reshape(1, -1)
        g2 = w["mlp_norm_g"][l].reshape(1, -1)
        wl = dict(w_in=fetch(l, "w_in", h))
        if l == 0:
            y, p, u, hs, conv, xc, gates, h = _mixer_fwd(h, g1, wl["w_in"], mw, lead=w["meta_tokens"], t_pad=t_pad)
        else:
            y, p, u, hs, conv, xc, gates = _mixer_fwd(h, g1, wl["w_in"], mw)
        wl["w_out"], wl["w_up"] = fetch(l, "w_out", y), fetch(l, "w_up", y)
        h1, u2, f = _mid_fwd(y, h, wl["w_out"], g2, wl["w_up"])
        wl["w_down"] = fetch(l, "w_down", f)
        if l == depth - 1:
            loss, dh, dgf = _down_fwd_loss(f, h1, wl["w_down"].reshape(D_FF, D_MODEL), w["final_norm_g"].reshape(1, -1), tgt,
                                           t_real)
            h2 = None
        else:
            h2 = _down_fwd(f, h1, wl["w_down"].reshape(D_FF, D_MODEL))
        saved.append(dict(mw=mw, g1=g1, g2=g2, h0=h, p=p, u=u, y=y, hs=hs, conv=conv, xc=xc, gates=gates, h1=h1, u2=u2, f=f))
        big.append(wl)
        h = h2

    gs = {k: [None] * depth for k in ("mix_norm_g", "mlp_norm_g", "pool_w", "pool_scale", "convb_dw_w", "convb_dw_b",
                                      "convb_ln_g", "convb_ln_b", "convb_pw_w", "rg_conv_w", "rg_conv_b", "rg_w_a",
                                      "rg_b_a", "rg_w_x", "rg_b_x", "rg_lambda")}
    deps = ()
    for l in reversed(range(depth)):
        s, wl = saved[l], big[l]
        df, dh1, dg2 = _mlp_bwd(dh, s["f"], s["h1"], s["g2"], wl["w_up"], wl["w_down"], deps)
        deps = hooks.point(l, "mlp_bwd", dh1)
        g_down = _tn_matmul(s["f"], dh, FF_CHUNK, D_MODEL, True, "dw_down", deps)
        hooks.grad(l, "w_down", g_down)
        deps = hooks.point(l, "dw_down", g_down)
        g_up = _tn_matmul(s["u2"], df, D_MODEL, FF_CHUNK, False, "dw_up", deps)
        hooks.grad(l, "w_up", g_up)
        deps = hooks.point(l, "dw_up", g_up)
        g_out = _tn_matmul(s["y"], dh1, D_MODEL, D_MODEL, False, "dw_out", deps)
        hooks.grad(l, "w_out", g_out.reshape(N_CHIPS, D_MODEL // N_CHIPS, D_MODEL))
        deps = hooks.point(l, "dw_out", g_out)
        dp, dh, mg = _mixer_bwd(s["p"], dh1, s["hs"], s["conv"], s["xc"], s["gates"], s["h0"], s["g1"], wl["w_out"],
                                wl["w_in"], s["mw"], deps, lead=0 if l else N_META, t_real=t_real)
        if l == 0:
            dh, dmeta = dh
        gs["mix_norm_g"][l] = mg["g1"][0]
        gs["mlp_norm_g"][l] = dg2[0]
        gs["pool_w"][l] = _diag_blocks(mg["wp"], D_POOL // POOL_GW)
        gs["pool_scale"][l] = mg["psc"][0]
        gs["convb_dw_w"][l] = jnp.sum(mg["dww"][:CONV_K], axis=1)
        gs["convb_dw_b"][l] = mg["dwb"][0]
        gs["convb_ln_g"][l] = mg["lng"][0]
        gs["convb_ln_b"][l] = mg["lnb"][0]
        gs["convb_pw_w"][l] = mg["wpw"]
        gs["rg_conv_w"][l] = mg["cw"][:RG_CONV_K]
        gs["rg_conv_b"][l] = mg["cb"][0]
        gs["rg_w_a"][l] = _diag_blocks(mg["wa"], D_RNN // RG_HD)
        gs["rg_b_a"][l] = mg["ba"][0]
        gs["rg_w_x"][l] = _diag_blocks(mg["wx"], D_RNN // RG_HD)
        gs["rg_b_x"][l] = mg["bx"][0]
        gs["rg_lambda"][l] = mg["lam"][0]
        if l == 0:
            gsmall = {k: jnp.stack(v) for k, v in gs.items()}
            gsmall["final_norm_g"] = dgf[0]
            gsmall["meta_tokens"] = dmeta
            started = hooks.small(gsmall)
        deps = hooks.point(l, "mixer_bwd", started[0] if l == 0 and started else dh)
        g_in = _tn_matmul(dp, s["u"], D_IN, D_MODEL, False, "dw_in", deps).reshape(N_CHIPS, D_IN // N_CHIPS, D_MODEL)
        hooks.grad(l, "w_in", g_in)
        deps = hooks.point(l, "dw_in", g_in)
    return loss[0, 0], dh


def _place():
    return lax.axis_index("x"), lax.axis_index("y"), lax.axis_index("c")


def _other_chips(x, y):
    return [(1 - x, y), (x, 1 - y), (1 - x, 1 - y)]


HBM_SPEC = pl.BlockSpec(memory_space=pltpu.HBM)
SEM_SPEC = pl.BlockSpec(memory_space=pltpu.SEMAPHORE)
DATAFLOW = pltpu.SideEffectType.DATAFLOW_SIDE_EFFECTING


def _gather_copies(src_refs, land_refs, send_sem, recv_sem, first):
    x, y, c = _place()
    me = 2 * x + y
    out = []
    for n in range(len(src_refs)):
        for j, (px, py) in enumerate(_other_chips(x, y) + [(x, y)]):
            k = first + N_CHIPS * n + j
            out.append(pltpu.make_async_remote_copy(src_refs[n], land_refs[n].at[me], send_sem.at[k], recv_sem.at[k],
                                                    device_id=(px, py, c), device_id_type=MESH))
    return out


def _gather_start(groups, name):
    srcs = [pltpu.with_memory_space_constraint(s, pltpu.HBM) for g in groups for s in g]
    lands = [pltpu.with_memory_space_constraint(lax.empty((N_CHIPS,) + s.shape, s.dtype), pltpu.HBM) for g in groups for s in g]
    n, ng = len(srcs), len(groups)
    first = [sum(len(g) for g in groups[:i]) for i in range(ng)]

    def body(*refs):
        src_refs, land_refs = refs[:n], refs[n:2 * n]
        sems = refs[2 * n:2 * n + 2 * ng]
        token = refs[-1]
        for gi, g in enumerate(groups):
            lo, hi = first[gi], first[gi] + len(g)
            for cp in _gather_copies(src_refs[lo:hi], land_refs[lo:hi], sems[2 * gi], sems[2 * gi + 1], 0):
                cp.start()
        token[...] = jnp.zeros(token.shape, token.dtype)

    sem_shapes = [pltpu.SemaphoreType.DMA((N_CHIPS * len(g),)) for g in groups for _ in range(2)]
    outs = pl.pallas_call(
        body, name=name,
        out_shape=sem_shapes + [pltpu.HBM(a.shape, a.dtype) for a in srcs + lands] + [jax.ShapeDtypeStruct((8, 128), F32)],
        in_specs=[HBM_SPEC] * (2 * n),
        out_specs=[SEM_SPEC] * (2 * ng) + [HBM_SPEC] * (2 * n) + [pl.BlockSpec(memory_space=pltpu.VMEM)],
        input_output_aliases={i: 2 * ng + i for i in range(2 * n)},
        compiler_params=pltpu.CompilerParams(has_side_effects=DATAFLOW),
    )(*srcs, *lands)
    sems, thru, token = outs[:2 * ng], outs[2 * ng:2 * ng + 2 * n], outs[-1]
    state = []
    for gi, g in enumerate(groups):
        lo, hi = first[gi], first[gi] + len(g)
        state.append((sems[2 * gi], sems[2 * gi + 1], thru[lo:hi], thru[n + lo:n + hi]))
    return state, token


def _gather_wait(state, after, name):
    send_sem, recv_sem, srcs, lands = state
    n = len(srcs)

    def body(*refs):
        src_refs, land_refs = refs[:n], refs[n:2 * n]
        send, recv = refs[2 * n], refs[2 * n + 1]
        for cp in _gather_copies(src_refs, land_refs, send, recv, 0):
            cp.wait_send()
            cp.wait_recv()

    outs = pl.pallas_call(
        body, name=name,
        out_shape=[pltpu.HBM(a.shape, a.dtype) for a in list(srcs) + list(lands)],
        in_specs=[HBM_SPEC] * (2 * n) + [SEM_SPEC, SEM_SPEC, ANY],
        out_specs=[HBM_SPEC] * (2 * n),
        input_output_aliases={i: i for i in range(2 * n)},
        compiler_params=pltpu.CompilerParams(has_side_effects=DATAFLOW),
    )(*srcs, *lands, send_sem, recv_sem, after)
    return outs[n:]


def _add_halves(g, recv, c1):
    nk, r, cd = g.shape
    r2 = r // 2
    rc = r2 // ROW_CHUNKS

    def body(c_ref, g_ref, r_ref, pab_ref):
        pab_ref[...] = (g_ref[...] + r_ref[...]).astype(BF16)

    blk = pl.BlockSpec((None, rc, cd), lambda k, j, c_ref: (k, j, 0))
    return pl.pallas_call(
        body, name="rs_add_halves",
        grid_spec=pltpu.PrefetchScalarGridSpec(
            num_scalar_prefetch=1, grid=(nk, ROW_CHUNKS),
            in_specs=[pl.BlockSpec((None, rc, cd), lambda k, j, c_ref: (k, c_ref[0] * ROW_CHUNKS + j, 0)), blk], out_specs=blk),
        out_shape=jax.ShapeDtypeStruct((nk, r2, cd), BF16),
        compiler_params=_params("parallel", "parallel"),
    )(c1, g, recv)


def _sum_partials(g, recv_sibling, recv_chips, c_me):
    nk, r, cd = g.shape
    r2 = r // 2
    rc = r2 // ROW_CHUNKS

    def body(cm_ref, g_ref, a_ref, r_ref, s_ref):
        own = g_ref[...] + a_ref[...]
        s_ref[...] = ((own + r_ref[0].astype(F32)) + r_ref[1].astype(F32)) + r_ref[2].astype(F32)

    return pl.pallas_call(
        body, name="rs_sum_partials",
        grid_spec=pltpu.PrefetchScalarGridSpec(
            num_scalar_prefetch=1, grid=(ROW_CHUNKS,),
            in_specs=[pl.BlockSpec((None, rc, cd), lambda j, cm: (cm[1], cm[0] * ROW_CHUNKS + j, 0)),
                      pl.BlockSpec((None, rc, cd), lambda j, cm: (cm[1], j, 0)),
                      pl.BlockSpec((3, rc, cd), lambda j, cm: (0, j, 0))],
            out_specs=pl.BlockSpec((rc, cd), lambda j, cm: (j, 0))),
        out_shape=jax.ShapeDtypeStruct((r2, cd), F32),
        compiler_params=_params("parallel"),
    )(c_me, g, recv_sibling, recv_chips)


def _split_start(name, srcs, lands, ncopies, make_copies):
    srcs = [pltpu.with_memory_space_constraint(s, pltpu.HBM) for s in srcs]
    lands = [pltpu.with_memory_space_constraint(a, pltpu.HBM) for a in lands]
    n, m = len(srcs), len(lands)

    def body(*refs):
        src_refs, land_refs = refs[:n], refs[n:n + m]
        send, recv, token = refs[n + m], refs[n + m + 1], refs[-1]
        for cp in make_copies(src_refs, land_refs, send, recv):
            cp.start()
        token[...] = jnp.zeros(token.shape, token.dtype)

    outs = pl.pallas_call(
        body, name=name,
        out_shape=[pltpu.SemaphoreType.DMA((ncopies,)), pltpu.SemaphoreType.DMA((ncopies,))]
        + [pltpu.HBM(a.shape, a.dtype) for a in srcs + lands] + [jax.ShapeDtypeStruct((8, 128), F32)],
        in_specs=[HBM_SPEC] * (n + m),
        out_specs=[SEM_SPEC, SEM_SPEC] + [HBM_SPEC] * (n + m) + [pl.BlockSpec(memory_space=pltpu.VMEM)],
        input_output_aliases={i: 2 + i for i in range(n + m)},
        compiler_params=pltpu.CompilerParams(has_side_effects=DATAFLOW),
    )(*srcs, *lands)
    return (outs[0], outs[1], outs[2:2 + n], outs[2 + n:2 + n + m], make_copies), outs[-1]


def _split_wait(name, state, after):
    send_sem, recv_sem, srcs, lands, make_copies = state
    n, m = len(srcs), len(lands)

    def body(*refs):
        src_refs, land_refs = refs[:n], refs[n:n + m]
        for cp in make_copies(src_refs, land_refs, refs[n + m], refs[n + m + 1]):
            cp.wait_send()
            cp.wait_recv()

    outs = pl.pallas_call(
        body, name=name,
        out_shape=[pltpu.HBM(a.shape, a.dtype) for a in list(srcs) + list(lands)],
        in_specs=[HBM_SPEC] * (n + m) + [SEM_SPEC, SEM_SPEC, ANY],
        out_specs=[HBM_SPEC] * (n + m),
        input_output_aliases={i: i for i in range(n + m)},
        compiler_params=pltpu.CompilerParams(has_side_effects=DATAFLOW),
    )(*srcs, *lands, send_sem, recv_sem, after)
    return outs[:n], outs[n:]


def _copies_to_sibling(src_of):
    def make(src_refs, land_refs, send, recv):
        x, y, c = _place()
        return [pltpu.make_async_remote_copy(src_of(src_refs[i], c), land_refs[i], send.at[i], recv.at[i],
                                             device_id=(x, y, 1 - c), device_id_type=MESH) for i in range(len(src_refs))]
    return make


def _copies_to_chips(src_refs, land_refs, send, recv):
    x, y, c = _place()
    return [pltpu.make_async_remote_copy(src_refs[i].at[2 * px + py], land_refs[i].at[j], send.at[3 * i + j], recv.at[3 * i + j],
                                         device_id=(px, py, c), device_id_type=MESH)
            for i in range(len(src_refs)) for j, (px, py) in enumerate(_other_chips(x, y))]


def _other_half_rows(ref, c):
    r2 = ref.shape[1] // 2
    return ref.at[:, pl.ds(pl.multiple_of((1 - c) * r2, 8), r2)]


class _ReduceScatter:
    def __init__(self, tag, grads, c1, me1):
        self.tag, self.grads, self.c1, self.me1 = tag, grads, c1, me1

    def start(self):
        lands = [lax.empty((g.shape[0], g.shape[1] // 2, g.shape[2]), F32) for g in self.grads]
        self.state, token = _split_start("rs_%s_a_start" % self.tag, self.grads, lands, len(self.grads),
                                         _copies_to_sibling(_other_half_rows))
        return token

    def to_chips(self, after):
        self.halves = _split_wait("rs_%s_a_wait" % self.tag, self.state, after)
        pabs = [_add_halves(g, r, self.c1) for g, r in zip(*self.halves)]
        lands = [lax.empty((3,) + p.shape[1:], BF16) for p in pabs]
        self.state, token = _split_start("rs_%s_b_start" % self.tag, pabs, lands, 3 * len(pabs), _copies_to_chips)
        return token

    def to_sibling(self, after):
        _, recv = _split_wait("rs_%s_b_wait" % self.tag, self.state, after)
        c_me = jnp.concatenate([self.c1, self.me1])
        sums = [_sum_partials(g, ra, rb, c_me) for g, ra, rb in zip(*self.halves, recv)]
        lands = [lax.empty(s.shape, F32) for s in sums]
        self.state, token = _split_start("rs_%s_c_start" % self.tag, sums, lands, len(sums),
                                         _copies_to_sibling(lambda ref, c: ref))
        return token

    def finish(self, after):
        return list(zip(*_split_wait("rs_%s_c_wait" % self.tag, self.state, after)))


def _add_lists(a_list, b_list):
    n = len(a_list)

    def body(*refs):
        for i in range(n):
            refs[2 * n + i][...] = refs[i][...] + refs[n + i][...]

    vm = pl.BlockSpec(memory_space=pltpu.VMEM)
    return pl.pallas_call(
        body, name="add_lists", in_specs=[vm] * (2 * n), out_specs=[vm] * n,
        out_shape=[jax.ShapeDtypeStruct(a.shape, a.dtype) for a in a_list],
        compiler_params=pltpu.CompilerParams(vmem_limit_bytes=VMEM_LIMIT),
    )(*a_list, *b_list)


def _copies_to_peer(stage):
    def make(src_refs, land_refs, send, recv):
        x, y, c = _place()
        peer = [(x, y, 1 - c), (1 - x, y, c), (x, 1 - y, c)][stage]
        return [pltpu.make_async_remote_copy(src_refs[i], land_refs[i], send.at[i], recv.at[i], device_id=peer, device_id_type=MESH)
                for i in range(len(src_refs))]
    return make


class _AllReduceSmall:
    def __init__(self, vs):
        self.vs, self.stage = list(vs), 0

    def _start(self):
        lands = [lax.empty(v.shape, v.dtype) for v in self.vs]
        self.state, token = _split_start("ar_small_start_%d" % self.stage, self.vs, lands, len(self.vs), _copies_to_peer(self.stage))
        return token

    def start(self):
        return self._start()

    def step(self, after):
        mine, theirs = _split_wait("ar_small_wait_%d" % self.stage, self.state, after)
        self.vs = _add_lists(mine, theirs)
        self.stage += 1
        return self._start() if self.stage < 3 else self.vs[0]


def _adamw_math(w, g, m, v):
    m = ADAM_B1 * m + (1.0 - ADAM_B1) * g
    v = ADAM_B2 * v + (1.0 - ADAM_B2) * jnp.square(g)
    m_hat = m / (1.0 - ADAM_B1 ** ADAM_STEP)
    v_hat = v / (1.0 - ADAM_B2 ** ADAM_STEP)
    return -ADAM_LR * (m_hat / (jnp.sqrt(v_hat) + ADAM_EPS) + ADAM_WD * w), m, v


def _adamw_big_layer(layer, w, m, v, own, sib, c1, prev):
    _, r, cd = w.shape
    rc = r // 2 // ROW_CHUNKS

    def body(c_ref, w_ref, m_ref, v_ref, own_ref, sib_ref, *rest):
        g_ref, d_ref, mo_ref, vo_ref, token = rest[-5:]
        g = jnp.where(pl.program_id(0) == c_ref[0], own_ref[...], sib_ref[...])
        g_ref[...] = g
        d_ref[...], mo_ref[...], vo_ref[...] = _adamw_math(w_ref[...], g, m_ref[...], v_ref[...])
        token[...] = jnp.zeros(token.shape, F32)

    blk = pl.BlockSpec((None, rc, cd), lambda hh, j, c_ref: (layer, hh * ROW_CHUNKS + j, 0))
    half = pl.BlockSpec((rc, cd), lambda hh, j, c_ref: (j, 0))
    prev = () if prev is None else tuple(prev)
    outs = pl.pallas_call(
        body, name="adamw_big",
        grid_spec=pltpu.PrefetchScalarGridSpec(
            num_scalar_prefetch=1, grid=(2, ROW_CHUNKS), in_specs=[blk, blk, blk, half, half] + [ANY] * len(prev),
            out_specs=[blk] * 4 + [pl.BlockSpec((8, 128), lambda hh, j, c_ref: (0, 0))]),
        out_shape=[jax.ShapeDtypeStruct(w.shape, F32)] * 4 + [jax.ShapeDtypeStruct((8, 128), F32)],
        input_output_aliases={6 + i: i for i in range(len(prev))},
        compiler_params=_params("arbitrary", "arbitrary"),
    )(c1, w, m, v, own, sib, *prev)
    return outs[:4], outs[4]


def _adamw_small(ws, gs, ms, vs):
    n = len(ws)

    def body(*refs):
        w_refs, g_refs, m_refs, v_refs = refs[:n], refs[n:2 * n], refs[2 * n:3 * n], refs[3 * n:4 * n]
        outs = refs[4 * n:]
        for i in range(n):
            outs[3 * i][...], outs[3 * i + 1][...], outs[3 * i + 2][...] = _adamw_math(
                w_refs[i][...], g_refs[i][...], m_refs[i][...], v_refs[i][...])

    vm = pl.BlockSpec(memory_space=pltpu.VMEM)
    outs = pl.pallas_call(
        body, name="adamw_small", in_specs=[vm] * (4 * n), out_specs=[vm] * (3 * n),
        out_shape=[jax.ShapeDtypeStruct(w.shape, F32) for w in ws for _ in range(3)],
        compiler_params=pltpu.CompilerParams(vmem_limit_bytes=VMEM_LIMIT),
    )(*ws, *gs, *ms, *vs)
    return [outs[3 * i:3 * i + 3] for i in range(n)]


LANES = 128
SUBLANES = 8
SHARDED_AXIS = {"meta_tokens": 1, "convb_dw_w": 2, "convb_pw_w": 1, "rg_conv_w": 2}


def _rows_of(size):
    return -(-size // (LANES * SUBLANES)) * SUBLANES


def _as_rows(a, rows=None):
    flat = a.reshape(-1)
    rows = _rows_of(flat.size) if rows is None else rows
    return jnp.pad(flat, (0, rows * LANES - flat.size)).reshape(rows, LANES)


class _GradientSchedule:
    GROUPS = {"l1": [(1, "w_down"), (1, "w_up"), (1, "w_out"), (1, "w_in")], "a0": [(0, "w_down"), (0, "w_up")],
              "b0": [(0, "w_out")], "c0": [(0, "w_in")]}
    PLAN = {
        (1, "dw_in"): [("l1", "start")],
        (0, "mlp_bwd"): [("l1", "to_chips")],
        (0, "dw_up"): [("l1", "to_sibling"), ("a0", "start")],
        (0, "dw_out"): [("l1", "finish"), ("a0", "to_chips"), ("b0", "start")],
        (0, "mixer_bwd"): [("a0", "to_sibling"), ("b0", "to_chips"), ("small", "step")],
        (0, "dw_in"): [("c0", "start"), ("small", "step"), ("c0", "to_chips"), ("a0", "finish"), ("b0", "to_sibling")],
    }

    def __init__(self, w, mom, var, c1, me1):
        self.w, self.mom, self.var, self.c1, self.me1 = w, mom, var, c1, me1
        self.grads, self.chains, self.out = {}, {}, {}

    def grad(self, layer, name, g):
        self.grads[layer, name] = g

    def small(self, gsmall):
        self.small_sum = _AllReduceSmall([g.reshape(1, -1) if g.ndim == 1 else g for g in (gsmall[k] for k in SMALL)])
        return (self.small_sum.start(),)

    def point(self, layer, kernel_name, after):
        return self.run(self.PLAN.get((layer, kernel_name), ()), after) or (after,)

    def run(self, actions, after):
        deps = []
        for tag, stage in actions:
            if tag == "small":
                deps.append(self.small_sum.step(after))
            elif stage == "start":
                self.chains[tag] = _ReduceScatter(tag, [self.grads[lk] for lk in self.GROUPS[tag]], self.c1, self.me1)
                deps.append(self.chains[tag].start())
            elif stage == "finish":
                for (layer, k), (own, sib) in zip(self.GROUPS[tag], self.chains[tag].finish(after)):
                    self.out[k], token = _adamw_big_layer(layer, self.w[k], self.mom[k], self.var[k], own, sib, self.c1,
                                                          self.out.get(k))
                    deps.append(token)
            else:
                deps.append(getattr(self.chains[tag], stage)(after))
            after = deps[-1]
        self.last = after
        return tuple(deps)


def _from_shard_major(name, sm):
    if name == "meta_tokens":
        return sm.transpose(1, 0, 2).reshape(N_META, -1)
    if name == "convb_pw_w":
        return sm.transpose(1, 0, 2, 3).reshape(2, -1, D_CONV)
    return sm.transpose(1, 2, 0, 3).reshape(sm.shape[1], sm.shape[2], -1)


def kernel(x, meta_tokens, mix_norm_g, w_in, pool_w, pool_scale, convb_dw_w, convb_dw_b, convb_ln_g, convb_ln_b, convb_pw_w, rg_conv_w, rg_conv_b, rg_w_a, rg_b_a, rg_w_x, rg_b_x, rg_lambda, w_out, mlp_norm_g, w_up, w_down, final_norm_g, loss_target, m_meta_tokens, m_mix_norm_g, m_w_in, m_pool_w, m_pool_scale, m_convb_dw_w, m_convb_dw_b, m_convb_ln_g, m_convb_ln_b, m_convb_pw_w, m_rg_conv_w, m_rg_conv_b, m_rg_w_a, m_rg_b_a, m_rg_w_x, m_rg_b_x, m_rg_lambda, m_w_out, m_mlp_norm_g, m_w_up, m_w_down, m_final_norm_g, v_meta_tokens, v_mix_norm_g, v_w_in, v_pool_w, v_pool_scale, v_convb_dw_w, v_convb_dw_b, v_convb_ln_g, v_convb_ln_b, v_convb_pw_w, v_rg_conv_w, v_rg_conv_b, v_rg_w_a, v_rg_b_a, v_rg_w_x, v_rg_b_x, v_rg_lambda, v_w_out, v_mlp_norm_g, v_w_up, v_w_down, v_final_norm_g):
    given = dict(locals())
    w = {k: given[k] for k in WEIGHTS}
    mom = {k: given["m_" + k] for k in WEIGHTS}
    var = {k: given["v_" + k] for k in WEIGHTS}
    xi, yi, ci = _place()
    me1 = (2 * xi + yi).astype(jnp.int32).reshape(1)
    c1 = ci.astype(jnp.int32).reshape(1)

    small_rows = [_rows_of(w[k].size) for k in SMALL_SHARDED]
    small_pack = jnp.concatenate([_as_rows(w[k]) for k in SMALL_SHARDED])
    transposed = lambda d: {**d, "w_in": d["w_in"].transpose(0, 2, 1)}
    wt, momt, vart = transposed(w), transposed(mom), transposed(var)
    order = [[(0, "w_in"), "small"], [(0, "w_out"), (0, "w_up")], [(0, "w_down")], [(1, "w_in")], [(1, "w_out"), (1, "w_up")],
             [(1, "w_down")]]
    state, token = _gather_start([[wt["w_in"][0].astype(BF16), small_pack]], "gather_start_0")
    shard = lambda l, k: (wt[k][l] + token[0, 0]).astype(BF16)
    rest, token_rest = _gather_start([[shard(*lk) for lk in g] for g in order[1:]], "gather_start_1")
    state = state + rest
    landed = {}

    def fetch(l, k, after):
        gi = [i for i, g in enumerate(order) if (l, k) in g][0]
        if gi not in landed:
            landed[gi] = _gather_wait(state[gi], after, "gather_wait_%d" % gi)
        raw = landed[gi][order[gi].index((l, k))]
        if k == "w_in":
            return raw.reshape(D_IN, D_MODEL)
        return raw.reshape(D_MODEL, D_MODEL) if k == "w_out" else raw

    seq = x.shape[1]
    t_real = N_META + seq
    t_pad = -(-t_real // ROW_ALIGN) * ROW_ALIGN
    landed[0] = _gather_wait(state[0], token_rest, "gather_wait_0")
    wfull = {k: (w[k] + token[0, 0] if k in ("pool_w", "rg_w_a", "rg_w_x") else w[k]) for k in WEIGHTS}
    off = 0
    for k, rows in zip(SMALL_SHARDED, small_rows):
        sm = landed[0][1][:, off:off + rows].reshape(N_CHIPS, -1)[:, :w[k].size].reshape((N_CHIPS,) + w[k].shape)
        wfull[k] = _from_shard_major(k, sm)
        off += rows
    sched = _GradientSchedule(wt, momt, vart, c1, me1)
    loss, dh = _local_step(x[0], loss_target[0], t_real, t_pad, wfull, fetch, sched)
    grad_x = dh[None]

    names = SMALL
    two_d = lambda a: a.reshape(1, -1) if a.ndim == 1 else a
    sched.run([("small", "step")], sched.last)
    summed = dict(zip(names, sched.small_sum.vs))
    for k in SMALL_SHARDED:
        ax = SHARDED_AXIS[k]
        summed[k] = lax.dynamic_slice_in_dim(summed[k], me1[0] * w[k].shape[ax], w[k].shape[ax], axis=ax)

    out = {}
    res = _adamw_small([two_d(w[k]) for k in names], [summed[k] for k in names], [two_d(mom[k]) for k in names],
                       [two_d(var[k]) for k in names])
    for k, (d, m2, v2) in zip(names, res):
        out[k] = tuple(o.reshape(w[k].shape) for o in (summed[k], d, m2, v2))
    sched.run([("b0", "finish"), ("c0", "to_sibling"), ("c0", "finish")], res[0][0])
    out.update(sched.out)
    out["w_in"] = tuple(o.transpose(0, 2, 1) for o in out["w_in"])

    loss = lax.psum(loss, ("x", "y", "c"))
    return (loss, grad_x, *[out[k][0] for k in WEIGHTS], *[out[k][1] for k in WEIGHTS],
            *[out[k][2] for k in WEIGHTS], *[out[k][3] for k in WEIGHTS])
```

```python
import functools

import jax
import jax.numpy as jnp
from jax import lax
from jax.experimental import pallas as pl
from jax.experimental.pallas import tpu as pltpu

F32, BF16 = jnp.float32, jnp.bfloat16
MESH = pl.DeviceIdType.MESH
ANY = pl.BlockSpec(memory_space=pl.ANY)

D_MODEL = 1024
N_META = 16
D_POOL = 256
D_CONV = 256
D_RNN = 512
D_IN = D_POOL + 2 * D_CONV + 2 * D_RNN
D_FF = 4096
FF_CHUNK = 1024
POOL_GW = 64
CONV_K = 31
RG_CONV_K = 4
RG_HD = 64
RG_C = 8.0
EPS = 1e-6
ADAM_LR, ADAM_B1, ADAM_B2, ADAM_EPS, ADAM_WD, ADAM_STEP = 0.001, 0.9, 0.999, 1e-08, 0.01, 10

HALO = 32
ROW_ALIGN = 256
TM_MIX = 384
TM_MAT = 768
TM_MLP_BWD = 384
N_CHIPS = 4
ROW_CHUNKS = 1
VMEM_LIMIT = 56 * 1024 * 1024

BIG = ("w_in", "w_out", "w_up", "w_down")
SMALL_SHARDED = ("meta_tokens", "convb_dw_w", "convb_pw_w", "rg_conv_w")
SMALL_REPL = ("mix_norm_g", "pool_w", "pool_scale", "convb_dw_b", "convb_ln_g", "convb_ln_b", "rg_conv_b",
              "rg_w_a", "rg_b_a", "rg_w_x", "rg_b_x", "rg_lambda", "mlp_norm_g", "final_norm_g")
SMALL = SMALL_REPL + SMALL_SHARDED
WEIGHTS = ("meta_tokens", "mix_norm_g", "w_in", "pool_w", "pool_scale", "convb_dw_w", "convb_dw_b", "convb_ln_g",
           "convb_ln_b", "convb_pw_w", "rg_conv_w", "rg_conv_b", "rg_w_a", "rg_b_a", "rg_w_x", "rg_b_x",
           "rg_lambda", "w_out", "mlp_norm_g", "w_up", "w_down", "final_norm_g")


def _params(*sem):
    return pltpu.CompilerParams(dimension_semantics=sem, vmem_limit_bytes=VMEM_LIMIT)


def _row_tile(t, cap):
    best = None
    for tm in range(128, cap + 1, 128):
        if t % tm == 0:
            best = tm
    assert best is not None, (t, cap)
    return best


def _dot(a, b):
    return jnp.dot(a, b, preferred_element_type=F32)


def _dot_nt(a, b):
    return lax.dot_general(a, b, (((1,), (1,)), ((), ())), preferred_element_type=F32)


def _dot_tn(a, b):
    return lax.dot_general(a, b, (((0,), (0,)), ((), ())), preferred_element_type=F32)


def _rms(x):
    r = lax.rsqrt(jnp.mean(x * x, axis=-1, keepdims=True) + EPS)
    return r, x * r


def _rms_bwd(du, n, r, g):
    dn = du * g
    return r * (dn - n * jnp.mean(dn * n, axis=-1, keepdims=True))


def _sig(x):
    return jax.nn.sigmoid(x)


def _colsum(x):
    return jnp.sum(x, axis=0, keepdims=True)


def _one_minus_sq(a, log_a):
    x = 2.0 * log_a
    series = -x * (1.0 + x * (0.5 + x * (1.0 / 6)))
    return jnp.where(x > -0.01, series, 1.0 - a * a)


_GELU_K0 = 0.7978845608028654
_GELU_K1 = 0.044715


def _gelu_and_grad(x):
    th = jnp.tanh(_GELU_K0 * (x + _GELU_K1 * x * x * x))
    val = 0.5 * x * (1.0 + th)
    grad = 0.5 * (1.0 + th) + 0.5 * x * (1.0 - th * th) * _GELU_K0 * (1.0 + 3.0 * _GELU_K1 * x * x)
    return val, grad


def _full(a):
    nd = a.ndim
    return pl.BlockSpec(a.shape, lambda *_: (0,) * nd)


def _resident(a):
    nd = a.ndim
    return pl.BlockSpec(a.shape, lambda *_: (0,) * nd, pipeline_mode=pl.Buffered(1))


def _after(body, n_in, deps):
    def wrapped(*refs):
        return body(*refs[:n_in], *refs[n_in + len(deps):])
    return wrapped


def _lane_sel(lane, a2, a4, a8, a16):
    return jnp.where(lane < POOL_GW, a2, jnp.where(lane < 2 * POOL_GW, a4, jnp.where(lane < 3 * POOL_GW, a8, a16)))


def _window_sums_back(src, tmp_a, tmp_b, tm):
    n = HALO + tm
    rows = lambda ref, lo, back: ref[pl.ds(lo - back, n - lo), :]
    tmp_a[pl.ds(8, n - 8), :] = rows(src, 8, 0) + rows(src, 8, 1)
    tmp_b[pl.ds(16, n - 16), :] = rows(tmp_a, 16, 0) + rows(tmp_a, 16, 2)
    s2 = rows(tmp_a, HALO, 0)
    tmp_a[pl.ds(24, n - 24), :] = rows(tmp_b, 24, 0) + rows(tmp_b, 24, 4)
    s8 = rows(tmp_a, HALO, 0)
    return s2, rows(tmp_b, HALO, 0), s8, s8 + rows(tmp_a, HALO, 8)


def _window_sums_ahead(src, tmp_a, tmp_b, tm):
    rows = lambda ref, n, ahead: ref[pl.ds(ahead, n), :]
    tmp_a[pl.ds(0, tm + 24), :] = rows(src, tm + 24, 0) + rows(src, tm + 24, 1)
    tmp_b[pl.ds(0, tm + 16), :] = rows(tmp_a, tm + 16, 0) + rows(tmp_a, tm + 16, 2)
    s2 = rows(tmp_a, tm, 0)
    tmp_a[pl.ds(0, tm + 8), :] = rows(tmp_b, tm + 8, 0) + rows(tmp_b, tm + 8, 4)
    s8 = rows(tmp_a, tm, 0)
    return s2, rows(tmp_b, tm, 0), s8, s8 + rows(tmp_a, tm, 8)


def _pool_counts(tm, t0):
    lane = lax.broadcasted_iota(jnp.int32, (tm, D_POOL), 1)
    row = lax.broadcasted_iota(jnp.int32, (tm, D_POOL), 0) + t0
    cnt = jnp.minimum(row + 1, _lane_sel(lane, 2, 4, 8, 16)).astype(F32)
    return lane, cnt


def _pool_fwd(ext_q, tmp_a, tmp_b, tm, t0):
    lane, cnt = _pool_counts(tm, t0)
    q = ext_q[pl.ds(HALO, tm), :]
    pooled = _lane_sel(lane, *_window_sums_back(ext_q, tmp_a, tmp_b, tm)) / cnt - q
    return pooled, lane, cnt


def _taps(src, w_of, offs, tm, zbuf):
    acc = None
    for r in range(8):
        ks = [k for k in range(len(offs)) if offs[k] % 8 == r]
        if not ks:
            continue
        rows = tm + (8 if r else 0)
        z = w_of(ks[0]) * src[pl.ds(offs[ks[0]] - r, rows), :]
        for k in ks[1:]:
            z = z + w_of(k) * src[pl.ds(offs[k] - r, rows), :]
        if r:
            zbuf[...] = z
            z = zbuf[pl.ds(r, tm), :]
        acc = z if acc is None else acc + z
    return acc


def _tap_grads(d_pad, src, offs, tm, g_ref, zbuf):
    ch = src.shape[-1]
    for r in range(8):
        ks = [k for k in range(len(offs)) if offs[k] % 8 == r]
        if not ks:
            continue
        rows = tm + (8 if r else 0)
        if r:
            zbuf[...] = d_pad[pl.ds(8 - r, rows), :]
        for k in ks:
            d = zbuf[...] if r else d_pad[pl.ds(8, rows), :]
            prod = d * src[pl.ds(offs[k] - r, rows), :]
            g_ref[k] += jnp.sum(prod.reshape(rows // 8, 8, ch), axis=0)


_CONV_OFFS = [HALO - (CONV_K - 1) + k for k in range(CONV_K)]


def _conv_fwd(ext_u, dww_ref, dwb, tm, zbuf):
    return dwb + _taps(ext_u, lambda k: dww_ref[k:k + 1, :], _CONV_OFFS, tm, zbuf)


def _ln_silu(c, lng, lnb):
    mu = jnp.mean(c, axis=-1, keepdims=True)
    cc = c - mu
    rstd = lax.rsqrt(jnp.mean(cc * cc, axis=-1, keepdims=True) + EPS)
    z = cc * rstd
    l = z * lng + lnb
    sl = _sig(l)
    return z, rstd, l, sl, l * sl


def _rg_conv(ext_x, cw_ref, cb, tm):
    xc = cb + cw_ref[0:1, :] * ext_x[pl.ds(HALO - (RG_CONV_K - 1), tm), :]
    for k in range(1, RG_CONV_K):
        xc = xc + cw_ref[k:k + 1, :] * ext_x[pl.ds(HALO - (RG_CONV_K - 1) + k, tm), :]
    return xc


def _softplus_neg(lam):
    return jnp.maximum(-lam, 0.0) + jnp.log(1.0 + jnp.exp(-jnp.abs(lam)))


def _rg_gates(xc, wa, ba, wx, bx, lam):
    xcb = xc.astype(BF16)
    r = _sig(_dot(xcb, wa) + ba)
    ig = _sig(_dot(xcb, wx) + bx)
    log_a = (-RG_C * r) * _softplus_neg(lam)
    a = jnp.exp(log_a)
    return r, ig, a, jnp.sqrt(_one_minus_sq(a, log_a))


def _scan_rows(a_ref, b_ref, out_ref, carry, tm, reverse):
    rows = lax.broadcasted_iota(jnp.int32, (8, D_RNN), 0)
    ngrp = tm // 8

    def grp(gi, hb):
        st = pl.multiple_of((ngrp - 1 - gi if reverse else gi) * 8, 8)
        a8 = a_ref[pl.ds(st, 8), :]
        b8 = b_ref[pl.ds(st, 8), :]
        out = jnp.zeros((8, D_RNN), F32)
        for j in (range(7, -1, -1) if reverse else range(8)):
            aj = jnp.broadcast_to(a8[j:j + 1, :], (8, D_RNN))
            bj = jnp.broadcast_to(b8[j:j + 1, :], (8, D_RNN))
            if reverse:
                cur = bj + hb
                hb = aj * cur
            else:
                cur = aj * hb + bj
                hb = cur
            out = jnp.where(rows == j, cur, out)
        out_ref[pl.ds(st, 8), :] = out
        return hb

    carry[...] = lax.fori_loop(0, ngrp, grp, carry[...])


_MIX_W = ("wp", "psc", "dww", "dwb", "lng", "lnb", "wpw", "cw", "cb", "wa", "ba", "wx", "bx", "lam")


def _lead_tile(src_ref, dst, carry, i, nt, tm, lead, n_src):
    last = n_src - (nt - 1) * tm
    assert 0 < last and lead + last <= tm, (n_src, nt, tm)
    dst[0:lead, :] = carry[...]

    @pl.when(i < nt - 1)
    def _():
        dst[lead:tm, :] = src_ref[0:tm - lead, :]
        carry[...] = src_ref[tm - lead:tm, :]

    @pl.when(i == nt - 1)
    def _():
        dst[lead:lead + last, :] = src_ref[0:last, :]
        if lead + last < tm:
            dst[lead + last:tm, :] = jnp.zeros((tm - lead - last, dst.shape[1]), dst.dtype)


def _mixer_fwd(h, g, w_in, mw, lead=None, t_pad=None):
    t = h.shape[0] if lead is None else t_pad
    tm = _row_tile(t, TM_MIX)
    nt = t // tm
    n_lead = 0 if lead is None else lead.shape[0]

    def body(h_ref, g_ref, win_ref, wp, psc, dww, dwb, lng, lnb, wpw, cw, cb, wa, ba, wx, bx, lam, *rest):
        if n_lead:
            lead_ref, rest = rest[0], rest[1:]
            h_out, h_carry = rest[7], rest[-1]
            rest = rest[:7] + rest[8:-1]
        (y_ref, p_ref, u_ref, hs_ref, conv_ref, xc_ref, gates_ref,
         ext_q, ext_u, ext_x, tmp_a, tmp_b, zbuf, a_s, b_s, hcar) = rest
        i = pl.program_id(0)

        @pl.when(i == 0)
        def _():
            ext_q[0:HALO, :] = jnp.zeros((HALO, D_POOL), F32)
            ext_u[0:HALO, :] = jnp.zeros((HALO, D_CONV), F32)
            ext_x[0:HALO, :] = jnp.zeros((HALO, D_RNN), F32)
            hcar[...] = jnp.zeros((8, D_RNN), F32)
            if n_lead:
                h_carry[...] = lead_ref[...]

        if n_lead:
            _lead_tile(h_ref, h_out, h_carry, i, nt, tm, n_lead, h.shape[0])
            h_ref = h_out
        u = (_rms(h_ref[...])[1] * g_ref[...]).astype(BF16)
        u_ref[...] = u
        p_ref[...] = _dot_nt(u, win_ref[...])

        ext_q[pl.ds(HALO, tm), :] = p_ref[:, 0:256]
        pooled, _, _ = _pool_fwd(ext_q, tmp_a, tmp_b, tm, i * tm)
        y_ref[:, 0:256] = (_dot(pooled.astype(BF16), wp[...]) * psc[...]).astype(BF16)

        ext_u[pl.ds(HALO, tm), :] = p_ref[:, 256:512] * _sig(p_ref[:, 512:768])
        conv = _conv_fwd(ext_u, dww, dwb[...], tm, zbuf)
        conv_ref[...] = conv
        act = _ln_silu(conv, lng[...], lnb[...])[4]
        y_ref[:, 256:512] = _dot(act.astype(BF16), wpw[...]).astype(BF16)

        ext_x[pl.ds(HALO, tm), :] = p_ref[:, 1280:1792]
        xc = _rg_conv(ext_x, cw, cb[...], tm)
        xc_ref[...] = xc
        r, ig, a, m = _rg_gates(xc, wa[...], ba[...], wx[...], bx[...], lam[...])
        for j, gate in enumerate((r, ig, a, m)):
            gates_ref[:, j * D_RNN:(j + 1) * D_RNN] = gate
        a_s[...] = a
        b_s[...] = m * (ig * xc)
        _scan_rows(a_s, b_s, hs_ref, hcar, tm, reverse=False)
        y_ref[:, 512:1024] = (_gelu_and_grad(p_ref[:, 768:1280])[0] * hs_ref[...]).astype(BF16)

        ext_q[0:HALO, :] = ext_q[pl.ds(tm, HALO), :]
        ext_u[0:HALO, :] = ext_u[pl.ds(tm, HALO), :]
        ext_x[0:HALO, :] = ext_x[pl.ds(tm, HALO), :]

    ws = [mw[k] for k in _MIX_W]
    row = lambda w: pl.BlockSpec((tm, w), lambda i: (i, 0))
    extra = [] if lead is None else [lead]
    return pl.pallas_call(
        body, name="mixer_fwd", grid=(nt,),
        in_specs=[row(D_MODEL), _full(g), _resident(w_in)] + [_full(w) for w in ws] + [_full(a) for a in extra],
        out_specs=[row(D_MODEL), row(D_IN), row(D_MODEL), row(D_RNN), row(D_CONV), row(D_RNN), row(4 * D_RNN)]
        + [row(D_MODEL)] * len(extra),
        out_shape=[jax.ShapeDtypeStruct((t, D_MODEL), BF16), jax.ShapeDtypeStruct((t, D_IN), F32),
                   jax.ShapeDtypeStruct((t, D_MODEL), BF16), jax.ShapeDtypeStruct((t, D_RNN), F32),
                   jax.ShapeDtypeStruct((t, D_CONV), F32), jax.ShapeDtypeStruct((t, D_RNN), F32),
                   jax.ShapeDtypeStruct((t, 4 * D_RNN), F32)] + [jax.ShapeDtypeStruct((t, D_MODEL), F32)] * len(extra),
        scratch_shapes=[pltpu.VMEM((HALO + tm, D_POOL), F32), pltpu.VMEM((HALO + tm, D_CONV), F32),
                        pltpu.VMEM((HALO + tm, D_RNN), F32), pltpu.VMEM((HALO + tm, D_POOL), F32),
                        pltpu.VMEM((HALO + tm, D_POOL), F32), pltpu.VMEM((tm + 8, D_CONV), F32),
                        pltpu.VMEM((tm, D_RNN), F32), pltpu.VMEM((tm, D_RNN), F32), pltpu.VMEM((8, D_RNN), F32)]
        + [pltpu.VMEM(a.shape, F32) for a in extra],
        compiler_params=_params("arbitrary"),
    )(h, g, w_in, *ws, *extra)


_MIX_G = (("wp", (D_POOL, D_POOL)), ("psc", (1, D_POOL)), ("dww", (32, 8, D_CONV)), ("dwb", (1, D_CONV)),
          ("lng", (1, D_CONV)), ("lnb", (1, D_CONV)), ("wpw", (D_CONV, D_CONV)), ("cw", (8, D_RNN)),
          ("cb", (1, D_RNN)), ("wa", (D_RNN, D_RNN)), ("ba", (1, D_RNN)), ("wx", (D_RNN, D_RNN)),
          ("bx", (1, D_RNN)), ("lam", (1, D_RNN)), ("g1", (1, D_MODEL)))


def _mixer_bwd(p, dh1, hs, conv, xc, gates, h0, g1, w_out, w_in, mw, deps=(), lead=0, t_real=None):
    t = p.shape[0]
    tm = _row_tile(t, TM_MIX)
    nt = t // tm
    hb = tm // HALO
    t_out = t_real - lead if lead else t

    def body(p_ref, ph_ref, dh1_ref, hs_ref, hsh_ref, conv_ref, xc_ref, gates_ref, h0_ref, g1_ref, wout_ref, win_ref,
             wp, psc, dww, dwb, lng, lnb, wpw, cw, cb, wa, ba, wx, bx, lam,
             dp_ref, dh0_ref, g_wp, g_psc, g_dww, g_dwb, g_lng, g_lnb, g_wpw, g_cw, g_cb, g_wa, g_ba, g_wx, g_bx, g_lam, g_g1,
             *tail):
        dlead_ref, carry = (tail[0], tail[-1]) if lead else (None, None)
        (ext_q, ext_u, ext_x, ext_h, ee, dc_s, dx_s, tmp_a, tmp_b, zbuf, d_pad, a_s, b_s, g_s, gcar, dy_ref,
         dp_s) = tail[1:-1] if lead else tail[1:]
        step = pl.program_id(0)
        i = nt - 1 - step
        grads = (g_wp, g_psc, g_dww, g_dwb, g_lng, g_lnb, g_wpw, g_cw, g_cb, g_wa, g_ba, g_wx, g_bx, g_lam, g_g1)
        if lead:
            grads += (carry,)
        dy_ref[...] = dh1_ref[...].astype(BF16)
        dy_cols = lambda lo, hi: _dot_nt(dy_ref[...], wout_ref[lo:hi, :])

        @pl.when(step == 0)
        def _():
            for gr in grads:
                gr[...] = jnp.zeros(gr.shape, F32)
            ee[pl.ds(tm, HALO), :] = jnp.zeros((HALO, D_POOL), F32)
            dc_s[pl.ds(tm, HALO), :] = jnp.zeros((HALO, D_CONV), F32)
            dx_s[pl.ds(tm, HALO), :] = jnp.zeros((HALO, D_RNN), F32)
            d_pad[0:8, :] = jnp.zeros((8, D_CONV), F32)
            d_pad[pl.ds(tm + 8, 8), :] = jnp.zeros((8, D_CONV), F32)
            gcar[...] = jnp.zeros((8, D_RNN), F32)

        hm = jnp.where(i == 0, 0.0, 1.0)

        ext_q[0:HALO, :] = ph_ref[:, 0:256] * hm
        ext_q[pl.ds(HALO, tm), :] = p_ref[:, 0:256]
        pooled, lane, cnt = _pool_fwd(ext_q, tmp_a, tmp_b, tm, i * tm)
        pooled_b = pooled.astype(BF16)
        dya = dy_cols(0, 256)
        g_psc[...] += _colsum(dya * _dot(pooled_b, wp[...]))
        dmixed_b = (dya * psc[...]).astype(BF16)
        dpooled = _dot_nt(dmixed_b, wp[...])
        g_wp[...] += _dot_tn(pooled_b, dmixed_b)
        ee[0:tm, :] = dpooled / cnt
        dp_s[:, 0:256] = _lane_sel(lane, *_window_sums_ahead(ee, tmp_a, tmp_b, tm)) - dpooled
        ee[pl.ds(tm, HALO), :] = ee[0:HALO, :]

        v = p_ref[:, 256:512]
        s = _sig(p_ref[:, 512:768])
        ext_u[0:HALO, :] = ph_ref[:, 256:512] * _sig(ph_ref[:, 512:768]) * hm
        ext_u[pl.ds(HALO, tm), :] = v * s
        z, rstd, l, sl, act = _ln_silu(conv_ref[...], lng[...], lnb[...])
        dyb_b = dy_cols(256, 512).astype(BF16)
        dact = _dot_nt(dyb_b, wpw[...])
        g_wpw[...] += _dot_tn(act.astype(BF16), dyb_b)
        dl = dact * (sl * (1.0 + l * (1.0 - sl)))
        g_lng[...] += _colsum(dl * z)
        g_lnb[...] += _colsum(dl)
        dz = dl * lng[...]
        dc = rstd * (dz - jnp.mean(dz, axis=-1, keepdims=True) - z * jnp.mean(dz * z, axis=-1, keepdims=True))
        g_dwb[...] += _colsum(dc)
        dc_s[0:tm, :] = dc
        d_pad[pl.ds(8, tm), :] = dc
        _tap_grads(d_pad, ext_u, _CONV_OFFS, tm, g_dww, zbuf)
        du0 = _taps(dc_s, lambda j: dww[CONV_K - 1 - j:CONV_K - j, :], list(range(CONV_K)), tm, zbuf)
        dp_s[:, 256:512] = du0 * s
        dp_s[:, 512:768] = du0 * v * (s * (1.0 - s))
        dc_s[pl.ds(tm, HALO), :] = dc_s[0:HALO, :]

        ext_x[0:HALO, :] = ph_ref[:, 1280:1792] * hm
        ext_x[pl.ds(HALO, tm), :] = p_ref[:, 1280:1792]
        xc = xc_ref[...]
        xcb = xc.astype(BF16)
        r, ig, a, m = (gates_ref[:, j * D_RNN:(j + 1) * D_RNN] for j in range(4))
        sp = _softplus_neg(lam[...])
        ext_h[0:HALO, :] = hsh_ref[...] * hm
        ext_h[pl.ds(HALO, tm), :] = hs_ref[...]
        dyc = dy_cols(512, 1024)
        gl, dgl = _gelu_and_grad(p_ref[:, 768:1280])
        dp_s[:, 768:1280] = dyc * hs_ref[...] * dgl
        a_s[...] = a
        b_s[...] = dyc * gl
        _scan_rows(a_s, b_s, g_s, gcar, tm, reverse=True)
        g = g_s[...]
        da = g * ext_h[pl.ds(HALO - 1, tm), :]
        dm = g * (ig * xc)
        dig = g * (m * xc)
        dlog_a = da * a - dm * (a * a) / m
        g_lam[...] += _colsum(dlog_a * (-RG_C * r)) * (-_sig(-lam[...]))
        dra = (dlog_a * (-RG_C * sp)) * (r * (1.0 - r))
        dia = dig * (ig * (1.0 - ig))
        g_ba[...] += _colsum(dra)
        g_bx[...] += _colsum(dia)
        dra_b = dra.astype(BF16)
        dia_b = dia.astype(BF16)
        dxc = g * (m * ig) + _dot_nt(dra_b, wa[...]) + _dot_nt(dia_b, wx[...])
        g_wa[...] += _dot_tn(xcb, dra_b)
        g_wx[...] += _dot_tn(xcb, dia_b)
        g_cb[...] += _colsum(dxc)
        dx_s[0:tm, :] = dxc
        for k in range(RG_CONV_K):
            g_cw[k:k + 1, :] += _colsum(dxc * ext_x[pl.ds(HALO - (RG_CONV_K - 1) + k, tm), :])
        dxin = cw[RG_CONV_K - 1:RG_CONV_K, :] * dxc
        for j in range(1, RG_CONV_K):
            dxin = dxin + cw[RG_CONV_K - 1 - j:RG_CONV_K - j, :] * dx_s[pl.ds(j, tm), :]
        dp_s[:, 1280:1792] = dxin
        dx_s[pl.ds(tm, HALO), :] = dx_s[0:HALO, :]

        dpb = dp_s[...].astype(BF16)
        dp_ref[...] = dpb
        du = _dot(dpb, win_ref[...])
        r, n = _rms(h0_ref[...])
        g_g1[...] += _colsum(du * n)
        dh0 = dh1_ref[...] + _rms_bwd(du, n, r, g1_ref[...])
        if lead:
            dh0_ref[0:tm - lead, :] = dh0[lead:tm, :]
            dh0_ref[tm - lead:tm, :] = carry[...]
            carry[...] = dh0[0:lead, :]

            @pl.when(i == 0)
            def _():
                dlead_ref[...] = dh0[0:lead, :]
        else:
            dh0_ref[...] = dh0
            tail[0][...] = dh0.astype(BF16)

    ws = [mw[k] for k in _MIX_W]
    tile = lambda w: pl.BlockSpec((tm, w), lambda s: (nt - 1 - s, 0))
    halo = lambda w: pl.BlockSpec((HALO, w), lambda s: (jnp.maximum((nt - 1 - s) * hb - 1, 0), 0))
    lead_out = [pl.BlockSpec((lead, D_MODEL), lambda s: (0, 0))] if lead else [tile(D_MODEL)]
    outs = pl.pallas_call(
        _after(body, 12 + len(ws), deps), name="mixer_bwd", grid=(nt,),
        in_specs=[tile(D_IN), halo(D_IN), tile(D_MODEL), tile(D_RNN), halo(D_RNN), tile(D_CONV), tile(D_RNN), tile(4 * D_RNN),
                  tile(D_MODEL), _full(g1),
                  _resident(w_out), _resident(w_in)] + [_full(w) for w in ws] + [ANY] * len(deps),
        out_specs=[tile(D_IN), tile(D_MODEL)] + [pl.BlockSpec(shp, lambda s, nd=len(shp): (0,) * nd) for _, shp in _MIX_G]
        + lead_out,
        out_shape=[jax.ShapeDtypeStruct((t, D_IN), BF16), jax.ShapeDtypeStruct((t_out, D_MODEL), F32)]
        + [jax.ShapeDtypeStruct(shp, F32) for _, shp in _MIX_G]
        + [jax.ShapeDtypeStruct((lead, D_MODEL), F32) if lead else jax.ShapeDtypeStruct((t, D_MODEL), BF16)],
        scratch_shapes=[pltpu.VMEM((HALO + tm, D_POOL), F32), pltpu.VMEM((HALO + tm, D_CONV), F32),
                        pltpu.VMEM((HALO + tm, D_RNN), F32), pltpu.VMEM((HALO + tm, D_RNN), F32),
                        pltpu.VMEM((tm + HALO, D_POOL), F32), pltpu.VMEM((tm + HALO, D_CONV), F32),
                        pltpu.VMEM((tm + HALO, D_RNN), F32), pltpu.VMEM((HALO + tm, D_POOL), F32),
                        pltpu.VMEM((HALO + tm, D_POOL), F32), pltpu.VMEM((tm + 8, D_CONV), F32),
                        pltpu.VMEM((tm + 16, D_CONV), F32), pltpu.VMEM((tm, D_RNN), F32),
                        pltpu.VMEM((tm, D_RNN), F32), pltpu.VMEM((tm, D_RNN), F32), pltpu.VMEM((8, D_RNN), F32),
                        pltpu.VMEM((tm, D_MODEL), BF16), pltpu.VMEM((tm, D_IN), F32)]
        + [pltpu.VMEM((lead, D_MODEL), F32)] * bool(lead),
        compiler_params=_params("arbitrary"),
    )(p, p, dh1, hs, hs, conv, xc, gates, h0, g1, w_out, w_in, *ws, *deps)
    grads = {k: o for (k, _), o in zip(_MIX_G, outs[2:])}
    return outs[0], (outs[1], outs[-1]), grads


def _mid_fwd(y, h0, w_out, g, w_up):
    t = h0.shape[0]
    tm = _row_tile(t, TM_MAT)

    def body(y_ref, h0_ref, wo_ref, g_ref, wu_ref, h1_ref, u2_ref, f_ref):
        h1 = h0_ref[...] + _dot(y_ref[...], wo_ref[...])
        h1_ref[...] = h1
        u2 = (_rms(h1)[1] * g_ref[...]).astype(BF16)
        u2_ref[...] = u2
        for c in range(D_FF // FF_CHUNK):
            f_ref[:, c * FF_CHUNK:(c + 1) * FF_CHUNK] = _dot(u2, wu_ref[c]).astype(BF16)

    row = lambda w: pl.BlockSpec((tm, w), lambda i: (i, 0))
    return pl.pallas_call(
        body, name="mid_fwd", grid=(t // tm,),
        in_specs=[row(D_MODEL), row(D_MODEL), _resident(w_out), _full(g), _resident(w_up)],
        out_specs=[row(D_MODEL), row(D_MODEL), row(D_FF)],
        out_shape=[jax.ShapeDtypeStruct((t, D_MODEL), F32), jax.ShapeDtypeStruct((t, D_MODEL), BF16),
                   jax.ShapeDtypeStruct((t, D_FF), BF16)],
        compiler_params=_params("parallel"),
    )(y, h0, w_out, g, w_up)


def _down_proj(f_ref, h1_ref, wd_ref):
    acc = h1_ref[...]
    for c in range(D_FF // FF_CHUNK):
        cols = slice(c * FF_CHUNK, (c + 1) * FF_CHUNK)
        a = jnp.square(jnp.maximum(f_ref[:, cols].astype(F32), 0.0)).astype(BF16)
        acc = acc + _dot(a, wd_ref[cols, :])
    return acc


def _down_fwd(f, h1, w_down):
    t = h1.shape[0]
    tm = _row_tile(t, TM_MAT)

    def body(f_ref, h1_ref, wd_ref, h2_ref):
        h2_ref[...] = _down_proj(f_ref, h1_ref, wd_ref)

    row = lambda w: pl.BlockSpec((tm, w), lambda i: (i, 0))
    return pl.pallas_call(
        body, name="down_fwd", grid=(t // tm,),
        in_specs=[row(D_FF), row(D_MODEL), _resident(w_down)], out_specs=row(D_MODEL),
        out_shape=jax.ShapeDtypeStruct((t, D_MODEL), F32),
        compiler_params=_params("parallel"),
    )(f, h1, w_down)


def _down_fwd_loss(f, h1, w_down, g, tgt, t_real):
    t = h1.shape[0]
    tm = _row_tile(t, TM_MAT)
    nt = t // tm

    def body(f_ref, h1_ref, wd_ref, g_ref, tgt_in, loss_ref, dh_ref, dg_ref, dhb_ref, tgt_ref, carry):
        i = pl.program_id(0)

        @pl.when(i == 0)
        def _():
            loss_ref[...] = jnp.zeros(loss_ref.shape, F32)
            dg_ref[...] = jnp.zeros(dg_ref.shape, F32)
            carry[...] = jnp.zeros(carry.shape, F32)

        _lead_tile(tgt_in, tgt_ref, carry, i, nt, tm, N_META, tgt.shape[0])

        r, n = _rms(_down_proj(f_ref, h1_ref, wd_ref))
        row = lax.broadcasted_iota(jnp.int32, (tm, 1), 0) + i * tm
        valid = jnp.logical_and(row >= N_META, row < t_real)
        diff = jnp.where(valid, n * g_ref[...] - tgt_ref[...], 0.0)
        loss_ref[...] += 0.5 * jnp.sum(jnp.mean(diff * diff, axis=-1, keepdims=True))
        dy = diff * (1.0 / D_MODEL)
        dg_ref[...] += _colsum(dy * n)
        dh = _rms_bwd(dy, n, r, g_ref[...])
        dh_ref[...] = dh
        dhb_ref[...] = dh.astype(BF16)

    row = lambda w: pl.BlockSpec((tm, w), lambda i: (i, 0))
    return pl.pallas_call(
        body, name="down_fwd_loss", grid=(t // tm,),
        in_specs=[row(D_FF), row(D_MODEL), _resident(w_down), _full(g), row(D_MODEL)],
        out_specs=[pl.BlockSpec((8, 128), lambda i: (0, 0)), row(D_MODEL), pl.BlockSpec((1, D_MODEL), lambda i: (0, 0)),
                   row(D_MODEL)],
        out_shape=[jax.ShapeDtypeStruct((8, 128), F32), jax.ShapeDtypeStruct((t, D_MODEL), F32),
                   jax.ShapeDtypeStruct((1, D_MODEL), F32), jax.ShapeDtypeStruct((t, D_MODEL), BF16)],
        scratch_shapes=[pltpu.VMEM((tm, D_MODEL), F32), pltpu.VMEM((N_META, D_MODEL), F32)],
        compiler_params=_params("arbitrary"),
    )(f, h1, w_down, g, tgt)


def _mlp_bwd(dh2, f, h1, g, w_up, w_down, deps=()):
    t = dh2.shape[0]
    tm = _row_tile(t, TM_MLP_BWD)

    def body(dh2_ref, f_ref, wd_ref, wu_ref, h1_ref, g_ref, df_ref, dh1_ref, dg_ref, dh1b_ref):
        @pl.when(pl.program_id(0) == 0)
        def _():
            dg_ref[...] = jnp.zeros(dg_ref.shape, F32)

        dh2 = dh2_ref[...]
        dhb = dh2.astype(BF16)
        du2 = None
        for c in range(D_FF // FF_CHUNK):
            cols = slice(c * FF_CHUNK, (c + 1) * FF_CHUNK)
            dact = _dot_nt(dhb, wd_ref[c])
            df = (dact * (2.0 * jnp.maximum(f_ref[:, cols].astype(F32), 0.0))).astype(BF16)
            df_ref[:, cols] = df
            part = _dot_nt(df, wu_ref[c])
            du2 = part if du2 is None else du2 + part
        r, n = _rms(h1_ref[...])
        dg_ref[...] += _colsum(du2 * n)
        dh1 = dh2 + _rms_bwd(du2, n, r, g_ref[...])
        dh1_ref[...] = dh1
        dh1b_ref[...] = dh1.astype(BF16)

    row = lambda w: pl.BlockSpec((tm, w), lambda i: (i, 0))
    return pl.pallas_call(
        _after(body, 6, deps), name="mlp_bwd", grid=(t // tm,),
        in_specs=[row(D_MODEL), row(D_FF), _resident(w_down), _resident(w_up), row(D_MODEL), _full(g)] + [ANY] * len(deps),
        out_specs=[row(D_FF), row(D_MODEL), pl.BlockSpec((1, D_MODEL), lambda i: (0, 0)), row(D_MODEL)],
        out_shape=[jax.ShapeDtypeStruct((t, D_FF), BF16), jax.ShapeDtypeStruct((t, D_MODEL), F32),
                   jax.ShapeDtypeStruct((1, D_MODEL), F32), jax.ShapeDtypeStruct((t, D_MODEL), BF16)],
        compiler_params=_params("arbitrary"),
    )(dh2, f, w_down, w_up, h1, g, *deps)


def _tn_matmul(a, b, kc, nc, relu2, name, deps=()):
    t, k = a.shape
    n = b.shape[1]
    tt = _row_tile(t, TM_MAT)
    gk, gn = k // kc, n // nc

    def body(a_ref, b_ref, o_ref):
        @pl.when(pl.program_id(2) == 0)
        def _():
            o_ref[...] = jnp.zeros(o_ref.shape, F32)

        av = a_ref[...]
        if relu2:
            av = jnp.square(jnp.maximum(av.astype(F32), 0.0))
        o_ref[...] += _dot_tn(av.astype(BF16), b_ref[...].astype(BF16))

    return pl.pallas_call(
        _after(body, 2, deps), name=name, grid=(gk, gn, t // tt),
        in_specs=[pl.BlockSpec((tt, kc), lambda ik, jn, it: (it, ik)), pl.BlockSpec((tt, nc), lambda ik, jn, it: (it, jn))]
        + [ANY] * len(deps),
        out_specs=pl.BlockSpec((None, kc, nc), lambda ik, jn, it: (ik * gn + jn, 0, 0)),
        out_shape=jax.ShapeDtypeStruct((gk * gn, kc, nc), F32),
        compiler_params=_params("parallel", "parallel", "arbitrary"),
    )(a, b, *deps)


def _block_diag(blocks):
    nb, hd, _ = blocks.shape
    eye = jnp.eye(nb, dtype=blocks.dtype)
    return (blocks[:, :, None, :] * eye[:, None, :, None]).reshape(nb * hd, nb * hd)


def _diag_blocks(m, nb):
    hd = m.shape[0] // nb
    eye = jnp.eye(nb, dtype=m.dtype)
    return jnp.sum(m.reshape(nb, hd, nb, hd) * eye[:, None, :, None], axis=2)


def _mixer_weights(w, l):
    row = lambda a: a.reshape(1, -1)
    return dict(
        wp=_block_diag(w["pool_w"][l]).astype(BF16), psc=row(w["pool_scale"][l]),
        dww=jnp.pad(w["convb_dw_w"][l], ((0, 32 - CONV_K), (0, 0))), dwb=row(w["convb_dw_b"][l]),
        lng=row(w["convb_ln_g"][l]), lnb=row(w["convb_ln_b"][l]), wpw=w["convb_pw_w"][l].astype(BF16),
        cw=jnp.pad(w["rg_conv_w"][l], ((0, 8 - RG_CONV_K), (0, 0))), cb=row(w["rg_conv_b"][l]),
        wa=_block_diag(w["rg_w_a"][l]).astype(BF16), ba=row(w["rg_b_a"][l]),
        wx=_block_diag(w["rg_w_x"][l]).astype(BF16), bx=row(w["rg_b_x"][l]), lam=row(w["rg_lambda"][l]))


def _local_step(h, tgt, t_real, t_pad, w, fetch, hooks):
    depth = 2
    saved = []
    big = []
    for l in range(depth):
        mw = _mixer_weights(w, l)
        g1 = w["mix_norm_g"][l].reshape(1, -1)
        g2 = w["mlp_norm_g"][l].reshape(1, -1)
        wl = dict(w_in=fetch(l, "w_in", h))
        if l == 0:
            y, p, u, hs, conv, xc, gates, h = _mixer_fwd(h, g1, wl["w_in"], mw, lead=w["meta_tokens"], t_pad=t_pad)
        else:
            y, p, u, hs, conv, xc, gates = _mixer_fwd(h, g1, wl["w_in"], mw)
        wl["w_out"], wl["w_up"] = fetch(l, "w_out", y), fetch(l, "w_up", y)
        h1, u2, f = _mid_fwd(y, h, wl["w_out"], g2, wl["w_up"])
        wl["w_down"] = fetch(l, "w_down", f)
        if l == depth - 1:
            loss, dh, dgf, dh_b = _down_fwd_loss(f, h1, wl["w_down"].reshape(D_FF, D_MODEL),
                                                 w["final_norm_g"].reshape(1, -1), tgt, t_real)
            h2 = None
        else:
            h2 = _down_fwd(f, h1, wl["w_down"].reshape(D_FF, D_MODEL))
        saved.append(dict(mw=mw, g1=g1, g2=g2, h0=h, p=p, u=u, y=y, hs=hs, conv=conv, xc=xc, gates=gates, h1=h1, u2=u2, f=f))
        big.append(wl)
        h = h2

    gs = {k: [None] * depth for k in ("mix_norm_g", "mlp_norm_g", "pool_w", "pool_scale", "convb_dw_w", "convb_dw_b",
                                      "convb_ln_g", "convb_ln_b", "convb_pw_w", "rg_conv_w", "rg_conv_b", "rg_w_a",
                                      "rg_b_a", "rg_w_x", "rg_b_x", "rg_lambda")}
    deps = ()
    for l in reversed(range(depth)):
        s, wl = saved[l], big[l]
        df, dh1, dg2, dh1_b = _mlp_bwd(dh, s["f"], s["h1"], s["g2"], wl["w_up"], wl["w_down"], deps)
        deps = hooks.point(l, "mlp_bwd", dh1)
        g_down = _tn_matmul(s["f"], dh_b, FF_CHUNK, D_MODEL, True, "dw_down", deps)
        hooks.grad(l, "w_down", g_down)
        deps = hooks.point(l, "dw_down", g_down)
        g_up = _tn_matmul(s["u2"], df, D_MODEL, FF_CHUNK, False, "dw_up", deps)
        hooks.grad(l, "w_up", g_up)
        deps = hooks.point(l, "dw_up", g_up)
        g_out = _tn_matmul(s["y"], dh1_b, D_MODEL, D_MODEL, False, "dw_out", deps)
        hooks.grad(l, "w_out", g_out.reshape(N_CHIPS, D_MODEL // N_CHIPS, D_MODEL))
        deps = hooks.point(l, "dw_out", g_out)
        dp, dh, mg = _mixer_bwd(s["p"], dh1, s["hs"], s["conv"], s["xc"], s["gates"], s["h0"], s["g1"], wl["w_out"],
                                wl["w_in"], s["mw"], deps, lead=0 if l else N_META, t_real=t_real)
        dh, dh_b = dh
        if l == 0:
            dmeta = dh_b
        gs["mix_norm_g"][l] = mg["g1"][0]
        gs["mlp_norm_g"][l] = dg2[0]
        gs["pool_w"][l] = _diag_blocks(mg["wp"], D_POOL // POOL_GW)
        gs["pool_scale"][l] = mg["psc"][0]
        gs["convb_dw_w"][l] = jnp.sum(mg["dww"][:CONV_K], axis=1)
        gs["convb_dw_b"][l] = mg["dwb"][0]
        gs["convb_ln_g"][l] = mg["lng"][0]
        gs["convb_ln_b"][l] = mg["lnb"][0]
        gs["convb_pw_w"][l] = mg["wpw"]
        gs["rg_conv_w"][l] = mg["cw"][:RG_CONV_K]
        gs["rg_conv_b"][l] = mg["cb"][0]
        gs["rg_w_a"][l] = _diag_blocks(mg["wa"], D_RNN // RG_HD)
        gs["rg_b_a"][l] = mg["ba"][0]
        gs["rg_w_x"][l] = _diag_blocks(mg["wx"], D_RNN // RG_HD)
        gs["rg_b_x"][l] = mg["bx"][0]
        gs["rg_lambda"][l] = mg["lam"][0]
        if l == 0:
            gsmall = {k: jnp.stack(v) for k, v in gs.items()}
            gsmall["final_norm_g"] = dgf[0]
            gsmall["meta_tokens"] = dmeta
            started = hooks.small(gsmall)
        deps = hooks.point(l, "mixer_bwd", started[0] if l == 0 and started else dh)
        g_in = _tn_matmul(dp, s["u"], D_IN, D_MODEL, False, "dw_in", deps).reshape(N_CHIPS, D_IN // N_CHIPS, D_MODEL)
        hooks.grad(l, "w_in", g_in)
        deps = hooks.point(l, "dw_in", g_in)
    return loss[0, 0], dh


def _place():
    return lax.axis_index("x"), lax.axis_index("y"), lax.axis_index("c")


def _other_chips(x, y):
    return [(1 - x, y), (x, 1 - y), (1 - x, 1 - y)]


HBM_SPEC = pl.BlockSpec(memory_space=pltpu.HBM)
SEM_SPEC = pl.BlockSpec(memory_space=pltpu.SEMAPHORE)
DATAFLOW = pltpu.SideEffectType.DATAFLOW_SIDE_EFFECTING


def _gather_copies(src_refs, land_refs, send_sem, recv_sem, first):
    x, y, c = _place()
    me = 2 * x + y
    out = []
    for n in range(len(src_refs)):
        for j, (px, py) in enumerate(_other_chips(x, y) + [(x, y)]):
            k = first + N_CHIPS * n + j
            out.append(pltpu.make_async_remote_copy(src_refs[n], land_refs[n].at[me], send_sem.at[k], recv_sem.at[k],
                                                    device_id=(px, py, c), device_id_type=MESH))
    return out


def _gather_start(groups, name):
    srcs = [pltpu.with_memory_space_constraint(s, pltpu.HBM) for g in groups for s in g]
    lands = [pltpu.with_memory_space_constraint(lax.empty((N_CHIPS,) + s.shape, s.dtype), pltpu.HBM) for g in groups for s in g]
    n, ng = len(srcs), len(groups)
    first = [sum(len(g) for g in groups[:i]) for i in range(ng)]

    def body(*refs):
        src_refs, land_refs = refs[:n], refs[n:2 * n]
        sems = refs[2 * n:2 * n + 2 * ng]
        token = refs[-1]
        for gi, g in enumerate(groups):
            lo, hi = first[gi], first[gi] + len(g)
            for cp in _gather_copies(src_refs[lo:hi], land_refs[lo:hi], sems[2 * gi], sems[2 * gi + 1], 0):
                cp.start()
        token[...] = jnp.zeros(token.shape, token.dtype)

    sem_shapes = [pltpu.SemaphoreType.DMA((N_CHIPS * len(g),)) for g in groups for _ in range(2)]
    outs = pl.pallas_call(
        body, name=name,
        out_shape=sem_shapes + [pltpu.HBM(a.shape, a.dtype) for a in srcs + lands] + [jax.ShapeDtypeStruct((8, 128), F32)],
        in_specs=[HBM_SPEC] * (2 * n),
        out_specs=[SEM_SPEC] * (2 * ng) + [HBM_SPEC] * (2 * n) + [pl.BlockSpec(memory_space=pltpu.VMEM)],
        input_output_aliases={i: 2 * ng + i for i in range(2 * n)},
        compiler_params=pltpu.CompilerParams(has_side_effects=DATAFLOW),
    )(*srcs, *lands)
    sems, thru, token = outs[:2 * ng], outs[2 * ng:2 * ng + 2 * n], outs[-1]
    state = []
    for gi, g in enumerate(groups):
        lo, hi = first[gi], first[gi] + len(g)
        state.append((sems[2 * gi], sems[2 * gi + 1], thru[lo:hi], thru[n + lo:n + hi]))
    return state, token


def _gather_wait(state, after, name):
    send_sem, recv_sem, srcs, lands = state
    n = len(srcs)

    def body(*refs):
        src_refs, land_refs = refs[:n], refs[n:2 * n]
        send, recv = refs[2 * n], refs[2 * n + 1]
        for cp in _gather_copies(src_refs, land_refs, send, recv, 0):
            cp.wait_send()
            cp.wait_recv()

    outs = pl.pallas_call(
        body, name=name,
        out_shape=[pltpu.HBM(a.shape, a.dtype) for a in list(srcs) + list(lands)],
        in_specs=[HBM_SPEC] * (2 * n) + [SEM_SPEC, SEM_SPEC, ANY],
        out_specs=[HBM_SPEC] * (2 * n),
        input_output_aliases={i: i for i in range(2 * n)},
        compiler_params=pltpu.CompilerParams(has_side_effects=DATAFLOW),
    )(*srcs, *lands, send_sem, recv_sem, after)
    return outs[n:]


def _add_halves(g, recv, c1):
    nk, r, cd = g.shape
    r2 = r // 2
    rc = r2 // ROW_CHUNKS

    def body(c_ref, g_ref, r_ref, pab_ref):
        pab_ref[...] = (g_ref[...] + r_ref[...]).astype(BF16)

    blk = pl.BlockSpec((None, rc, cd), lambda k, j, c_ref: (k, j, 0))
    return pl.pallas_call(
        body, name="rs_add_halves",
        grid_spec=pltpu.PrefetchScalarGridSpec(
            num_scalar_prefetch=1, grid=(nk, ROW_CHUNKS),
            in_specs=[pl.BlockSpec((None, rc, cd), lambda k, j, c_ref: (k, c_ref[0] * ROW_CHUNKS + j, 0)), blk], out_specs=blk),
        out_shape=jax.ShapeDtypeStruct((nk, r2, cd), BF16),
        compiler_params=_params("parallel", "parallel"),
    )(c1, g, recv)


def _sum_partials(g, recv_sibling, recv_chips, c_me):
    nk, r, cd = g.shape
    r2 = r // 2
    rc = r2 // ROW_CHUNKS

    def body(cm_ref, g_ref, a_ref, r_ref, s_ref):
        own = g_ref[...] + a_ref[...]
        s_ref[...] = ((own + r_ref[0].astype(F32)) + r_ref[1].astype(F32)) + r_ref[2].astype(F32)

    return pl.pallas_call(
        body, name="rs_sum_partials",
        grid_spec=pltpu.PrefetchScalarGridSpec(
            num_scalar_prefetch=1, grid=(ROW_CHUNKS,),
            in_specs=[pl.BlockSpec((None, rc, cd), lambda j, cm: (cm[1], cm[0] * ROW_CHUNKS + j, 0)),
                      pl.BlockSpec((None, rc, cd), lambda j, cm: (cm[1], j, 0)),
                      pl.BlockSpec((3, rc, cd), lambda j, cm: (0, j, 0))],
            out_specs=pl.BlockSpec((rc, cd), lambda j, cm: (j, 0))),
        out_shape=jax.ShapeDtypeStruct((r2, cd), F32),
        compiler_params=_params("parallel"),
    )(c_me, g, recv_sibling, recv_chips)


def _split_start(name, srcs, lands, ncopies, make_copies):
    srcs = [pltpu.with_memory_space_constraint(s, pltpu.HBM) for s in srcs]
    lands = [pltpu.with_memory_space_constraint(a, pltpu.HBM) for a in lands]
    n, m = len(srcs), len(lands)

    def body(*refs):
        src_refs, land_refs = refs[:n], refs[n:n + m]
        send, recv, token = refs[n + m], refs[n + m + 1], refs[-1]
        for cp in make_copies(src_refs, land_refs, send, recv):
            cp.start()
        token[...] = jnp.zeros(token.shape, token.dtype)

    outs = pl.pallas_call(
        body, name=name,
        out_shape=[pltpu.SemaphoreType.DMA((ncopies,)), pltpu.SemaphoreType.DMA((ncopies,))]
        + [pltpu.HBM(a.shape, a.dtype) for a in srcs + lands] + [jax.ShapeDtypeStruct((8, 128), F32)],
        in_specs=[HBM_SPEC] * (n + m),
        out_specs=[SEM_SPEC, SEM_SPEC] + [HBM_SPEC] * (n + m) + [pl.BlockSpec(memory_space=pltpu.VMEM)],
        input_output_aliases={i: 2 + i for i in range(n + m)},
        compiler_params=pltpu.CompilerParams(has_side_effects=DATAFLOW),
    )(*srcs, *lands)
    return (outs[0], outs[1], outs[2:2 + n], outs[2 + n:2 + n + m], make_copies), outs[-1]


def _split_wait(name, state, after):
    send_sem, recv_sem, srcs, lands, make_copies = state
    n, m = len(srcs), len(lands)

    def body(*refs):
        src_refs, land_refs = refs[:n], refs[n:n + m]
        for cp in make_copies(src_refs, land_refs, refs[n + m], refs[n + m + 1]):
            cp.wait_send()
            cp.wait_recv()

    outs = pl.pallas_call(
        body, name=name,
        out_shape=[pltpu.HBM(a.shape, a.dtype) for a in list(srcs) + list(lands)],
        in_specs=[HBM_SPEC] * (n + m) + [SEM_SPEC, SEM_SPEC, ANY],
        out_specs=[HBM_SPEC] * (n + m),
        input_output_aliases={i: i for i in range(n + m)},
        compiler_params=pltpu.CompilerParams(has_side_effects=DATAFLOW),
    )(*srcs, *lands, send_sem, recv_sem, after)
    return outs[:n], outs[n:]


def _copies_to_sibling(src_of):
    def make(src_refs, land_refs, send, recv):
        x, y, c = _place()
        return [pltpu.make_async_remote_copy(src_of(src_refs[i], c), land_refs[i], send.at[i], recv.at[i],
                                             device_id=(x, y, 1 - c), device_id_type=MESH) for i in range(len(src_refs))]
    return make


def _copies_to_chips(src_refs, land_refs, send, recv):
    x, y, c = _place()
    return [pltpu.make_async_remote_copy(src_refs[i].at[2 * px + py], land_refs[i].at[j], send.at[3 * i + j], recv.at[3 * i + j],
                                         device_id=(px, py, c), device_id_type=MESH)
            for i in range(len(src_refs)) for j, (px, py) in enumerate(_other_chips(x, y))]


def _other_half_rows(ref, c):
    r2 = ref.shape[1] // 2
    return ref.at[:, pl.ds(pl.multiple_of((1 - c) * r2, 8), r2)]


class _ReduceScatter:
    def __init__(self, tag, grads, c1, me1):
        self.tag, self.grads, self.c1, self.me1 = tag, grads, c1, me1

    def start(self):
        lands = [lax.empty((g.shape[0], g.shape[1] // 2, g.shape[2]), F32) for g in self.grads]
        self.state, token = _split_start("rs_%s_a_start" % self.tag, self.grads, lands, len(self.grads),
                                         _copies_to_sibling(_other_half_rows))
        return token

    def to_chips(self, after):
        self.halves = _split_wait("rs_%s_a_wait" % self.tag, self.state, after)
        pabs = [_add_halves(g, r, self.c1) for g, r in zip(*self.halves)]
        lands = [lax.empty((3,) + p.shape[1:], BF16) for p in pabs]
        self.state, token = _split_start("rs_%s_b_start" % self.tag, pabs, lands, 3 * len(pabs), _copies_to_chips)
        return token

    def to_sibling(self, after):
        _, recv = _split_wait("rs_%s_b_wait" % self.tag, self.state, after)
        c_me = jnp.concatenate([self.c1, self.me1])
        sums = [_sum_partials(g, ra, rb, c_me) for g, ra, rb in zip(*self.halves, recv)]
        lands = [lax.empty(s.shape, F32) for s in sums]
        self.state, token = _split_start("rs_%s_c_start" % self.tag, sums, lands, len(sums),
                                         _copies_to_sibling(lambda ref, c: ref))
        return token

    def finish(self, after):
        return list(zip(*_split_wait("rs_%s_c_wait" % self.tag, self.state, after)))


def _add_lists(a_list, b_list):
    n = len(a_list)

    def body(*refs):
        for i in range(n):
            refs[2 * n + i][...] = refs[i][...] + refs[n + i][...]

    vm = pl.BlockSpec(memory_space=pltpu.VMEM)
    return pl.pallas_call(
        body, name="add_lists", in_specs=[vm] * (2 * n), out_specs=[vm] * n,
        out_shape=[jax.ShapeDtypeStruct(a.shape, a.dtype) for a in a_list],
        compiler_params=pltpu.CompilerParams(vmem_limit_bytes=VMEM_LIMIT),
    )(*a_list, *b_list)


def _copies_to_peer(stage):
    def make(src_refs, land_refs, send, recv):
        x, y, c = _place()
        peer = [(x, y, 1 - c), (1 - x, y, c), (x, 1 - y, c)][stage]
        return [pltpu.make_async_remote_copy(src_refs[i], land_refs[i], send.at[i], recv.at[i], device_id=peer, device_id_type=MESH)
                for i in range(len(src_refs))]
    return make


class _AllReduceSmall:
    def __init__(self, vs):
        self.vs, self.stage = list(vs), 0

    def _start(self):
        lands = [lax.empty(v.shape, v.dtype) for v in self.vs]
        self.state, token = _split_start("ar_small_start_%d" % self.stage, self.vs, lands, len(self.vs), _copies_to_peer(self.stage))
        return token

    def start(self):
        return self._start()

    def step(self, after):
        mine, theirs = _split_wait("ar_small_wait_%d" % self.stage, self.state, after)
        self.vs = _add_lists(mine, theirs)
        self.stage += 1
        return self._start() if self.stage < 3 else self.vs[0]


def _adamw_math(w, g, m, v):
    m = ADAM_B1 * m + (1.0 - ADAM_B1) * g
    v = ADAM_B2 * v + (1.0 - ADAM_B2) * jnp.square(g)
    m_hat = m / (1.0 - ADAM_B1 ** ADAM_STEP)
    v_hat = v / (1.0 - ADAM_B2 ** ADAM_STEP)
    return -ADAM_LR * (m_hat / (jnp.sqrt(v_hat) + ADAM_EPS) + ADAM_WD * w), m, v


def _adamw_big_layer(layer, w, m, v, own, sib, c1, prev):
    _, r, cd = w.shape
    rc = r // 2 // ROW_CHUNKS

    def body(c_ref, w_ref, m_ref, v_ref, own_ref, sib_ref, *rest):
        g_ref, d_ref, mo_ref, vo_ref, token = rest[-5:]
        g = jnp.where(pl.program_id(0) == c_ref[0], own_ref[...], sib_ref[...])
        g_ref[...] = g
        d_ref[...], mo_ref[...], vo_ref[...] = _adamw_math(w_ref[...], g, m_ref[...], v_ref[...])
        token[...] = jnp.zeros(token.shape, F32)

    blk = pl.BlockSpec((None, rc, cd), lambda hh, j, c_ref: (layer, hh * ROW_CHUNKS + j, 0))
    half = pl.BlockSpec((rc, cd), lambda hh, j, c_ref: (j, 0))
    prev = () if prev is None else tuple(prev)
    outs = pl.pallas_call(
        body, name="adamw_big",
        grid_spec=pltpu.PrefetchScalarGridSpec(
            num_scalar_prefetch=1, grid=(2, ROW_CHUNKS), in_specs=[blk, blk, blk, half, half] + [ANY] * len(prev),
            out_specs=[blk] * 4 + [pl.BlockSpec((8, 128), lambda hh, j, c_ref: (0, 0))]),
        out_shape=[jax.ShapeDtypeStruct(w.shape, F32)] * 4 + [jax.ShapeDtypeStruct((8, 128), F32)],
        input_output_aliases={6 + i: i for i in range(len(prev))},
        compiler_params=_params("arbitrary", "arbitrary"),
    )(c1, w, m, v, own, sib, *prev)
    return outs[:4], outs[4]


def _adamw_small(ws, gs, ms, vs):
    n = len(ws)

    def body(*refs):
        w_refs, g_refs, m_refs, v_refs = refs[:n], refs[n:2 * n], refs[2 * n:3 * n], refs[3 * n:4 * n]
        outs = refs[4 * n:]
        for i in range(n):
            outs[3 * i][...], outs[3 * i + 1][...], outs[3 * i + 2][...] = _adamw_math(
                w_refs[i][...], g_refs[i][...], m_refs[i][...], v_refs[i][...])

    vm = pl.BlockSpec(memory_space=pltpu.VMEM)
    outs = pl.pallas_call(
        body, name="adamw_small", in_specs=[vm] * (4 * n), out_specs=[vm] * (3 * n),
        out_shape=[jax.ShapeDtypeStruct(w.shape, F32) for w in ws for _ in range(3)],
        compiler_params=pltpu.CompilerParams(vmem_limit_bytes=VMEM_LIMIT),
    )(*ws, *gs, *ms, *vs)
    return [outs[3 * i:3 * i + 3] for i in range(n)]


LANES = 128
SUBLANES = 8
SHARDED_AXIS = {"meta_tokens": 1, "convb_dw_w": 2, "convb_pw_w": 1, "rg_conv_w": 2}


def _rows_of(size):
    return -(-size // (LANES * SUBLANES)) * SUBLANES


def _as_rows(a, rows=None):
    flat = a.reshape(-1)
    rows = _rows_of(flat.size) if rows is None else rows
    return jnp.pad(flat, (0, rows * LANES - flat.size)).reshape(rows, LANES)


class _GradientSchedule:
    GROUPS = {"l1": [(1, "w_down"), (1, "w_up"), (1, "w_out"), (1, "w_in")], "a0": [(0, "w_down"), (0, "w_up")],
              "b0": [(0, "w_out")], "c0": [(0, "w_in")]}
    PLAN = {
        (1, "dw_in"): [("l1", "start")],
        (0, "mlp_bwd"): [("l1", "to_chips")],
        (0, "dw_up"): [("l1", "to_sibling"), ("a0", "start")],
        (0, "dw_out"): [("l1", "finish"), ("a0", "to_chips"), ("b0", "start")],
        (0, "mixer_bwd"): [("a0", "to_sibling"), ("b0", "to_chips"), ("small", "step")],
        (0, "dw_in"): [("c0", "start"), ("small", "step"), ("c0", "to_chips"), ("a0", "finish"), ("b0", "to_sibling")],
    }

    def __init__(self, w, mom, var, c1, me1):
        self.w, self.mom, self.var, self.c1, self.me1 = w, mom, var, c1, me1
        self.grads, self.chains, self.out = {}, {}, {}

    def grad(self, layer, name, g):
        self.grads[layer, name] = g

    def small(self, gsmall):
        self.small_sum = _AllReduceSmall([g.reshape(1, -1) if g.ndim == 1 else g for g in (gsmall[k] for k in SMALL)])
        return (self.small_sum.start(),)

    def point(self, layer, kernel_name, after):
        return self.run(self.PLAN.get((layer, kernel_name), ()), after) or (after,)

    def run(self, actions, after):
        deps = []
        for tag, stage in actions:
            if tag == "small":
                deps.append(self.small_sum.step(after))
            elif stage == "start":
                self.chains[tag] = _ReduceScatter(tag, [self.grads[lk] for lk in self.GROUPS[tag]], self.c1, self.me1)
                deps.append(self.chains[tag].start())
            elif stage == "finish":
                for (layer, k), (own, sib) in zip(self.GROUPS[tag], self.chains[tag].finish(after)):
                    self.out[k], token = _adamw_big_layer(layer, self.w[k], self.mom[k], self.var[k], own, sib, self.c1,
                                                          self.out.get(k))
                    deps.append(token)
            else:
                deps.append(getattr(self.chains[tag], stage)(after))
            after = deps[-1]
        self.last = after
        return tuple(deps)


def _from_shard_major(name, sm):
    if name == "meta_tokens":
        return sm.transpose(1, 0, 2).reshape(N_META, -1)
    if name == "convb_pw_w":
        return sm.transpose(1, 0, 2, 3).reshape(2, -1, D_CONV)
    return sm.transpose(1, 2, 0, 3).reshape(sm.shape[1], sm.shape[2], -1)


def kernel(x, meta_tokens, mix_norm_g, w_in, pool_w, pool_scale, convb_dw_w, convb_dw_b, convb_ln_g, convb_ln_b, convb_pw_w, rg_conv_w, rg_conv_b, rg_w_a, rg_b_a, rg_w_x, rg_b_x, rg_lambda, w_out, mlp_norm_g, w_up, w_down, final_norm_g, loss_target, m_meta_tokens, m_mix_norm_g, m_w_in, m_pool_w, m_pool_scale, m_convb_dw_w, m_convb_dw_b, m_convb_ln_g, m_convb_ln_b, m_convb_pw_w, m_rg_conv_w, m_rg_conv_b, m_rg_w_a, m_rg_b_a, m_rg_w_x, m_rg_b_x, m_rg_lambda, m_w_out, m_mlp_norm_g, m_w_up, m_w_down, m_final_norm_g, v_meta_tokens, v_mix_norm_g, v_w_in, v_pool_w, v_pool_scale, v_convb_dw_w, v_convb_dw_b, v_convb_ln_g, v_convb_ln_b, v_convb_pw_w, v_rg_conv_w, v_rg_conv_b, v_rg_w_a, v_rg_b_a, v_rg_w_x, v_rg_b_x, v_rg_lambda, v_w_out, v_mlp_norm_g, v_w_up, v_w_down, v_final_norm_g):
    given = dict(locals())
    w = {k: given[k] for k in WEIGHTS}
    mom = {k: given["m_" + k] for k in WEIGHTS}
    var = {k: given["v_" + k] for k in WEIGHTS}
    xi, yi, ci = _place()
    me1 = (2 * xi + yi).astype(jnp.int32).reshape(1)
    c1 = ci.astype(jnp.int32).reshape(1)

    small_rows = [_rows_of(w[k].size) for k in SMALL_SHARDED]
    small_pack = jnp.concatenate([_as_rows(w[k]) for k in SMALL_SHARDED])
    transposed = lambda d: {**d, "w_in": d["w_in"].transpose(0, 2, 1)}
    wt, momt, vart = transposed(w), transposed(mom), transposed(var)
    order = [[(0, "w_in"), "small"], [(0, "w_out"), (0, "w_up")], [(0, "w_down")], [(1, "w_in")], [(1, "w_out"), (1, "w_up")],
             [(1, "w_down")]]
    state, token = _gather_start([[wt["w_in"][0].astype(BF16), small_pack]], "gather_start_0")
    shard = lambda l, k: (wt[k][l] + token[0, 0]).astype(BF16)
    rest, token_rest = _gather_start([[shard(*lk) for lk in g] for g in order[1:]], "gather_start_1")
    state = state + rest
    landed = {}

    def fetch(l, k, after):
        gi = [i for i, g in enumerate(order) if (l, k) in g][0]
        if gi not in landed:
            landed[gi] = _gather_wait(state[gi], after, "gather_wait_%d" % gi)
        raw = landed[gi][order[gi].index((l, k))]
        if k == "w_in":
            return raw.reshape(D_IN, D_MODEL)
        return raw.reshape(D_MODEL, D_MODEL) if k == "w_out" else raw

    seq = x.shape[1]
    t_real = N_META + seq
    t_pad = -(-t_real // ROW_ALIGN) * ROW_ALIGN
    landed[0] = _gather_wait(state[0], token_rest, "gather_wait_0")
    wfull = {k: (w[k] + token[0, 0] if k in ("pool_w", "rg_w_a", "rg_w_x") else w[k]) for k in WEIGHTS}
    off = 0
    for k, rows in zip(SMALL_SHARDED, small_rows):
        sm = landed[0][1][:, off:off + rows].reshape(N_CHIPS, -1)[:, :w[k].size].reshape((N_CHIPS,) + w[k].shape)
        wfull[k] = _from_shard_major(k, sm)
        off += rows
    sched = _GradientSchedule(wt, momt, vart, c1, me1)
    loss, dh = _local_step(x[0], loss_target[0], t_real, t_pad, wfull, fetch, sched)
    grad_x = dh[None]

    names = SMALL
    two_d = lambda a: a.reshape(1, -1) if a.ndim == 1 else a
    sched.run([("small", "step")], sched.last)
    summed = dict(zip(names, sched.small_sum.vs))
    for k in SMALL_SHARDED:
        ax = SHARDED_AXIS[k]
        summed[k] = lax.dynamic_slice_in_dim(summed[k], me1[0] * w[k].shape[ax], w[k].shape[ax], axis=ax)

    out = {}
    res = _adamw_small([two_d(w[k]) for k in names], [summed[k] for k in names], [two_d(mom[k]) for k in names],
                       [two_d(var[k]) for k in names])
    for k, (d, m2, v2) in zip(names, res):
        out[k] = tuple(o.reshape(w[k].shape) for o in (summed[k], d, m2, v2))
    sched.run([("b0", "finish"), ("c0", "to_sibling"), ("c0", "finish")], res[0][0])
    out.update(sched.out)
    out["w_in"] = tuple(o.transpose(0, 2, 1) for o in out["w_in"])

    loss = lax.psum(loss, ("x", "y", "c"))
    return (loss, grad_x, *[out[k][0] for k in WEIGHTS], *[out[k][1] for k in WEIGHTS],
            *[out[k][2] for k in WEIGHTS], *[out[k][3] for k in WEIGHTS])
```

```python
import functools

import jax
import jax.numpy as jnp
from jax import lax
from jax.experimental import pallas as pl
from jax.experimental.pallas import tpu as pltpu

F32, BF16 = jnp.float32, jnp.bfloat16
MESH = pl.DeviceIdType.MESH
ANY = pl.BlockSpec(memory_space=pl.ANY)

D_MODEL = 1024
N_META = 16
D_POOL = 256
D_CONV = 256
D_RNN = 512
D_IN = D_POOL + 2 * D_CONV + 2 * D_RNN
D_FF = 4096
FF_CHUNK = 1024
POOL_GW = 64
CONV_K = 31
RG_CONV_K = 4
RG_HD = 64
RG_C = 8.0
EPS = 1e-6
ADAM_LR, ADAM_B1, ADAM_B2, ADAM_EPS, ADAM_WD, ADAM_STEP = 0.001, 0.9, 0.999, 1e-08, 0.01, 10

HALO = 32
ROW_ALIGN = 256
TM_MIX = 384
TM_MAT = 768
TM_MLP_BWD = 384
TM_TN = 2816
N_CHIPS = 4
ROW_CHUNKS = 1
VMEM_LIMIT = 56 * 1024 * 1024

BIG = ("w_in", "w_out", "w_up", "w_down")
SMALL_SHARDED = ("meta_tokens", "convb_dw_w", "convb_pw_w", "rg_conv_w")
SMALL_REPL = ("mix_norm_g", "pool_w", "pool_scale", "convb_dw_b", "convb_ln_g", "convb_ln_b", "rg_conv_b",
              "rg_w_a", "rg_b_a", "rg_w_x", "rg_b_x", "rg_lambda", "mlp_norm_g", "final_norm_g")
SMALL = SMALL_REPL + SMALL_SHARDED
WEIGHTS = ("meta_tokens", "mix_norm_g", "w_in", "pool_w", "pool_scale", "convb_dw_w", "convb_dw_b", "convb_ln_g",
           "convb_ln_b", "convb_pw_w", "rg_conv_w", "rg_conv_b", "rg_w_a", "rg_b_a", "rg_w_x", "rg_b_x",
           "rg_lambda", "w_out", "mlp_norm_g", "w_up", "w_down", "final_norm_g")


def _params(*sem):
    return pltpu.CompilerParams(dimension_semantics=sem, vmem_limit_bytes=VMEM_LIMIT)


def _row_tile(t, cap):
    best = None
    for tm in range(128, cap + 1, 128):
        if t % tm == 0:
            best = tm
    assert best is not None, (t, cap)
    return best


def _dot(a, b):
    return jnp.dot(a, b, preferred_element_type=F32)


def _dot_nt(a, b):
    return lax.dot_general(a, b, (((1,), (1,)), ((), ())), preferred_element_type=F32)


def _dot_tn(a, b):
    return lax.dot_general(a, b, (((0,), (0,)), ((), ())), preferred_element_type=F32)


def _rms(x):
    r = lax.rsqrt(jnp.mean(x * x, axis=-1, keepdims=True) + EPS)
    return r, x * r


def _rms_bwd(du, n, r, g):
    dn = du * g
    return r * (dn - n * jnp.mean(dn * n, axis=-1, keepdims=True))


def _sig(x):
    return jax.nn.sigmoid(x)


def _colsum(x):
    return jnp.sum(x, axis=0, keepdims=True)


def _one_minus_sq(a, log_a):
    x = 2.0 * log_a
    series = -x * (1.0 + x * (0.5 + x * (1.0 / 6)))
    return jnp.where(x > -0.01, series, 1.0 - a * a)


_GELU_K0 = 0.7978845608028654
_GELU_K1 = 0.044715


def _gelu_and_grad(x):
    th = jnp.tanh(_GELU_K0 * (x + _GELU_K1 * x * x * x))
    val = 0.5 * x * (1.0 + th)
    grad = 0.5 * (1.0 + th) + 0.5 * x * (1.0 - th * th) * _GELU_K0 * (1.0 + 3.0 * _GELU_K1 * x * x)
    return val, grad


def _full(a):
    nd = a.ndim
    return pl.BlockSpec(a.shape, lambda *_: (0,) * nd)


def _resident(a):
    nd = a.ndim
    return pl.BlockSpec(a.shape, lambda *_: (0,) * nd, pipeline_mode=pl.Buffered(1))


def _after(body, n_in, deps):
    def wrapped(*refs):
        return body(*refs[:n_in], *refs[n_in + len(deps):])
    return wrapped


def _lane_sel(lane, a2, a4, a8, a16):
    return jnp.where(lane < POOL_GW, a2, jnp.where(lane < 2 * POOL_GW, a4, jnp.where(lane < 3 * POOL_GW, a8, a16)))


def _window_sums_back(src, tmp_a, tmp_b, tm):
    n = HALO + tm
    rows = lambda ref, lo, back: ref[pl.ds(lo - back, n - lo), :]
    tmp_a[pl.ds(8, n - 8), :] = rows(src, 8, 0) + rows(src, 8, 1)
    tmp_b[pl.ds(16, n - 16), :] = rows(tmp_a, 16, 0) + rows(tmp_a, 16, 2)
    s2 = rows(tmp_a, HALO, 0)
    tmp_a[pl.ds(24, n - 24), :] = rows(tmp_b, 24, 0) + rows(tmp_b, 24, 4)
    s8 = rows(tmp_a, HALO, 0)
    return s2, rows(tmp_b, HALO, 0), s8, s8 + rows(tmp_a, HALO, 8)


def _window_sums_ahead(src, tmp_a, tmp_b, tm):
    rows = lambda ref, n, ahead: ref[pl.ds(ahead, n), :]
    tmp_a[pl.ds(0, tm + 24), :] = rows(src, tm + 24, 0) + rows(src, tm + 24, 1)
    tmp_b[pl.ds(0, tm + 16), :] = rows(tmp_a, tm + 16, 0) + rows(tmp_a, tm + 16, 2)
    s2 = rows(tmp_a, tm, 0)
    tmp_a[pl.ds(0, tm + 8), :] = rows(tmp_b, tm + 8, 0) + rows(tmp_b, tm + 8, 4)
    s8 = rows(tmp_a, tm, 0)
    return s2, rows(tmp_b, tm, 0), s8, s8 + rows(tmp_a, tm, 8)


def _pool_counts(tm, t0):
    lane = lax.broadcasted_iota(jnp.int32, (tm, D_POOL), 1)
    row = lax.broadcasted_iota(jnp.int32, (tm, D_POOL), 0) + t0
    cnt = jnp.minimum(row + 1, _lane_sel(lane, 2, 4, 8, 16)).astype(F32)
    return lane, cnt


def _pool_fwd(ext_q, tmp_a, tmp_b, tm, t0):
    lane, cnt = _pool_counts(tm, t0)
    q = ext_q[pl.ds(HALO, tm), :]
    pooled = _lane_sel(lane, *_window_sums_back(ext_q, tmp_a, tmp_b, tm)) / cnt - q
    return pooled, lane, cnt


def _taps(src, w_of, offs, tm, zbuf):
    acc = None
    for r in range(8):
        ks = [k for k in range(len(offs)) if offs[k] % 8 == r]
        if not ks:
            continue
        rows = tm + (8 if r else 0)
        z = w_of(ks[0]) * src[pl.ds(offs[ks[0]] - r, rows), :]
        for k in ks[1:]:
            z = z + w_of(k) * src[pl.ds(offs[k] - r, rows), :]
        if r:
            zbuf[...] = z
            z = zbuf[pl.ds(r, tm), :]
        acc = z if acc is None else acc + z
    return acc


def _tap_grads(d_pad, src, offs, tm, g_ref, zbuf):
    ch = src.shape[-1]
    for r in range(8):
        ks = [k for k in range(len(offs)) if offs[k] % 8 == r]
        if not ks:
            continue
        rows = tm + (8 if r else 0)
        if r:
            zbuf[...] = d_pad[pl.ds(8 - r, rows), :]
        for k in ks:
            d = zbuf[...] if r else d_pad[pl.ds(8, rows), :]
            prod = d * src[pl.ds(offs[k] - r, rows), :]
            g_ref[k] += jnp.sum(prod.reshape(rows // 8, 8, ch), axis=0)


_CONV_OFFS = [HALO - (CONV_K - 1) + k for k in range(CONV_K)]


def _conv_fwd(ext_u, dww_ref, dwb, tm, zbuf):
    return dwb + _taps(ext_u, lambda k: dww_ref[k:k + 1, :], _CONV_OFFS, tm, zbuf)


def _ln_silu(c, lng, lnb):
    mu = jnp.mean(c, axis=-1, keepdims=True)
    cc = c - mu
    rstd = lax.rsqrt(jnp.mean(cc * cc, axis=-1, keepdims=True) + EPS)
    z = cc * rstd
    l = z * lng + lnb
    sl = _sig(l)
    return z, rstd, l, sl, l * sl


def _rg_conv(ext_x, cw_ref, cb, tm):
    xc = cb + cw_ref[0:1, :] * ext_x[pl.ds(HALO - (RG_CONV_K - 1), tm), :]
    for k in range(1, RG_CONV_K):
        xc = xc + cw_ref[k:k + 1, :] * ext_x[pl.ds(HALO - (RG_CONV_K - 1) + k, tm), :]
    return xc


def _softplus_neg(lam):
    return jnp.maximum(-lam, 0.0) + jnp.log(1.0 + jnp.exp(-jnp.abs(lam)))


def _rg_gates(xc, wa, ba, wx, bx, lam):
    xcb = xc.astype(BF16)
    r = _sig(_dot(xcb, wa) + ba)
    ig = _sig(_dot(xcb, wx) + bx)
    log_a = (-RG_C * r) * _softplus_neg(lam)
    a = jnp.exp(log_a)
    return r, ig, a, jnp.sqrt(_one_minus_sq(a, log_a))


def _scan_rows(a_ref, b_ref, out_ref, carry, tm, reverse):
    rows = lax.broadcasted_iota(jnp.int32, (8, D_RNN), 0)
    ngrp = tm // 8

    def grp(gi, hb):
        st = pl.multiple_of((ngrp - 1 - gi if reverse else gi) * 8, 8)
        a8 = a_ref[pl.ds(st, 8), :]
        b8 = b_ref[pl.ds(st, 8), :]
        out = jnp.zeros((8, D_RNN), F32)
        for j in (range(7, -1, -1) if reverse else range(8)):
            aj = jnp.broadcast_to(a8[j:j + 1, :], (8, D_RNN))
            bj = jnp.broadcast_to(b8[j:j + 1, :], (8, D_RNN))
            if reverse:
                cur = bj + hb
                hb = aj * cur
            else:
                cur = aj * hb + bj
                hb = cur
            out = jnp.where(rows == j, cur, out)
        out_ref[pl.ds(st, 8), :] = out
        return hb

    carry[...] = lax.fori_loop(0, ngrp, grp, carry[...])


_MIX_W = ("wp", "psc", "dww", "dwb", "lng", "lnb", "wpw", "cw", "cb", "wa", "ba", "wx", "bx", "lam")


def _lead_tile(src_ref, dst, carry, i, nt, tm, lead, n_src):
    last = n_src - (nt - 1) * tm
    assert 0 < last and lead + last <= tm, (n_src, nt, tm)
    dst[0:lead, :] = carry[...]

    @pl.when(i < nt - 1)
    def _():
        dst[lead:tm, :] = src_ref[0:tm - lead, :]
        carry[...] = src_ref[tm - lead:tm, :]

    @pl.when(i == nt - 1)
    def _():
        dst[lead:lead + last, :] = src_ref[0:last, :]
        if lead + last < tm:
            dst[lead + last:tm, :] = jnp.zeros((tm - lead - last, dst.shape[1]), dst.dtype)


def _mixer_fwd(h, g, w_in, mw, lead=None, t_pad=None):
    t = h.shape[0] if lead is None else t_pad
    tm = _row_tile(t, TM_MIX)
    nt = t // tm
    n_lead = 0 if lead is None else lead.shape[0]

    def body(h_ref, g_ref, win_ref, wp, psc, dww, dwb, lng, lnb, wpw, cw, cb, wa, ba, wx, bx, lam, *rest):
        if n_lead:
            lead_ref, rest = rest[0], rest[1:]
            h_out, h_carry = rest[7], rest[-1]
            rest = rest[:7] + rest[8:-1]
        (y_ref, p_ref, u_ref, hs_ref, conv_ref, xc_ref, gates_ref,
         ext_q, ext_u, ext_x, tmp_a, tmp_b, zbuf, a_s, b_s, hcar) = rest
        i = pl.program_id(0)

        @pl.when(i == 0)
        def _():
            ext_q[0:HALO, :] = jnp.zeros((HALO, D_POOL), F32)
            ext_u[0:HALO, :] = jnp.zeros((HALO, D_CONV), F32)
            ext_x[0:HALO, :] = jnp.zeros((HALO, D_RNN), F32)
            hcar[...] = jnp.zeros((8, D_RNN), F32)
            if n_lead:
                h_carry[...] = lead_ref[...]

        if n_lead:
            _lead_tile(h_ref, h_out, h_carry, i, nt, tm, n_lead, h.shape[0])
            h_ref = h_out
        u = (_rms(h_ref[...])[1] * g_ref[...]).astype(BF16)
        u_ref[...] = u
        p_ref[...] = _dot_nt(u, win_ref[...])

        ext_q[pl.ds(HALO, tm), :] = p_ref[:, 0:256]
        pooled, _, _ = _pool_fwd(ext_q, tmp_a, tmp_b, tm, i * tm)
        y_ref[:, 0:256] = (_dot(pooled.astype(BF16), wp[...]) * psc[...]).astype(BF16)

        ext_u[pl.ds(HALO, tm), :] = p_ref[:, 256:512] * _sig(p_ref[:, 512:768])
        conv = _conv_fwd(ext_u, dww, dwb[...], tm, zbuf)
        conv_ref[...] = conv
        act = _ln_silu(conv, lng[...], lnb[...])[4]
        y_ref[:, 256:512] = _dot(act.astype(BF16), wpw[...]).astype(BF16)

        ext_x[pl.ds(HALO, tm), :] = p_ref[:, 1280:1792]
        xc = _rg_conv(ext_x, cw, cb[...], tm)
        xc_ref[...] = xc
        r, ig, a, m = _rg_gates(xc, wa[...], ba[...], wx[...], bx[...], lam[...])
        for j, gate in enumerate((r, ig, a, m)):
            gates_ref[:, j * D_RNN:(j + 1) * D_RNN] = gate
        a_s[...] = a
        b_s[...] = m * (ig * xc)
        _scan_rows(a_s, b_s, hs_ref, hcar, tm, reverse=False)
        y_ref[:, 512:1024] = (_gelu_and_grad(p_ref[:, 768:1280])[0] * hs_ref[...]).astype(BF16)

        ext_q[0:HALO, :] = ext_q[pl.ds(tm, HALO), :]
        ext_u[0:HALO, :] = ext_u[pl.ds(tm, HALO), :]
        ext_x[0:HALO, :] = ext_x[pl.ds(tm, HALO), :]

    ws = [mw[k] for k in _MIX_W]
    row = lambda w: pl.BlockSpec((tm, w), lambda i: (i, 0))
    extra = [] if lead is None else [lead]
    return pl.pallas_call(
        body, name="mixer_fwd", grid=(nt,),
        in_specs=[row(D_MODEL), _full(g), _resident(w_in)] + [_full(w) for w in ws] + [_full(a) for a in extra],
        out_specs=[row(D_MODEL), row(D_IN), row(D_MODEL), row(D_RNN), row(D_CONV), row(D_RNN), row(4 * D_RNN)]
        + [row(D_MODEL)] * len(extra),
        out_shape=[jax.ShapeDtypeStruct((t, D_MODEL), BF16), jax.ShapeDtypeStruct((t, D_IN), F32),
                   jax.ShapeDtypeStruct((t, D_MODEL), BF16), jax.ShapeDtypeStruct((t, D_RNN), F32),
                   jax.ShapeDtypeStruct((t, D_CONV), F32), jax.ShapeDtypeStruct((t, D_RNN), F32),
                   jax.ShapeDtypeStruct((t, 4 * D_RNN), F32)] + [jax.ShapeDtypeStruct((t, D_MODEL), F32)] * len(extra),
        scratch_shapes=[pltpu.VMEM((HALO + tm, D_POOL), F32), pltpu.VMEM((HALO + tm, D_CONV), F32),
                        pltpu.VMEM((HALO + tm, D_RNN), F32), pltpu.VMEM((HALO + tm, D_POOL), F32),
                        pltpu.VMEM((HALO + tm, D_POOL), F32), pltpu.VMEM((tm + 8, D_CONV), F32),
                        pltpu.VMEM((tm, D_RNN), F32), pltpu.VMEM((tm, D_RNN), F32), pltpu.VMEM((8, D_RNN), F32)]
        + [pltpu.VMEM(a.shape, F32) for a in extra],
        compiler_params=_params("arbitrary"),
    )(h, g, w_in, *ws, *extra)


_MIX_G = (("wp", (D_POOL, D_POOL)), ("psc", (1, D_POOL)), ("dww", (32, 8, D_CONV)), ("dwb", (1, D_CONV)),
          ("lng", (1, D_CONV)), ("lnb", (1, D_CONV)), ("wpw", (D_CONV, D_CONV)), ("cw", (8, D_RNN)),
          ("cb", (1, D_RNN)), ("wa", (D_RNN, D_RNN)), ("ba", (1, D_RNN)), ("wx", (D_RNN, D_RNN)),
          ("bx", (1, D_RNN)), ("lam", (1, D_RNN)), ("g1", (1, D_MODEL)))


def _mixer_bwd(p, dh1, hs, conv, xc, gates, h0, g1, w_out, w_in, mw, deps=(), lead=0, t_real=None):
    t = p.shape[0]
    tm = _row_tile(t, TM_MIX)
    nt = t // tm
    hb = tm // HALO
    t_out = t_real - lead if lead else t

    def body(p_ref, ph_ref, dh1_ref, hs_ref, hsh_ref, conv_ref, xc_ref, gates_ref, h0_ref, g1_ref, wout_ref, win_ref,
             wp, psc, dww, dwb, lng, lnb, wpw, cw, cb, wa, ba, wx, bx, lam,
             dp_ref, dh0_ref, g_wp, g_psc, g_dww, g_dwb, g_lng, g_lnb, g_wpw, g_cw, g_cb, g_wa, g_ba, g_wx, g_bx, g_lam, g_g1,
             *tail):
        dlead_ref, carry = (tail[0], tail[-1]) if lead else (None, None)
        (ext_q, ext_u, ext_x, ext_h, ee, dc_s, dx_s, tmp_a, tmp_b, zbuf, d_pad, a_s, b_s, g_s, gcar, dy_ref,
         dp_s) = tail[1:-1] if lead else tail[1:]
        step = pl.program_id(0)
        i = nt - 1 - step
        grads = (g_wp, g_psc, g_dww, g_dwb, g_lng, g_lnb, g_wpw, g_cw, g_cb, g_wa, g_ba, g_wx, g_bx, g_lam, g_g1)
        if lead:
            grads += (carry,)
        dy_ref[...] = dh1_ref[...].astype(BF16)
        dy_cols = lambda lo, hi: _dot_nt(dy_ref[...], wout_ref[lo:hi, :])

        @pl.when(step == 0)
        def _():
            for gr in grads:
                gr[...] = jnp.zeros(gr.shape, F32)
            ee[pl.ds(tm, HALO), :] = jnp.zeros((HALO, D_POOL), F32)
            dc_s[pl.ds(tm, HALO), :] = jnp.zeros((HALO, D_CONV), F32)
            dx_s[pl.ds(tm, HALO), :] = jnp.zeros((HALO, D_RNN), F32)
            d_pad[0:8, :] = jnp.zeros((8, D_CONV), F32)
            d_pad[pl.ds(tm + 8, 8), :] = jnp.zeros((8, D_CONV), F32)
            gcar[...] = jnp.zeros((8, D_RNN), F32)

        hm = jnp.where(i == 0, 0.0, 1.0)

        ext_q[0:HALO, :] = ph_ref[:, 0:256] * hm
        ext_q[pl.ds(HALO, tm), :] = p_ref[:, 0:256]
        pooled, lane, cnt = _pool_fwd(ext_q, tmp_a, tmp_b, tm, i * tm)
        pooled_b = pooled.astype(BF16)
        dya = dy_cols(0, 256)
        g_psc[...] += _colsum(dya * _dot(pooled_b, wp[...]))
        dmixed_b = (dya * psc[...]).astype(BF16)
        dpooled = _dot_nt(dmixed_b, wp[...])
        g_wp[...] += _dot_tn(pooled_b, dmixed_b)
        ee[0:tm, :] = dpooled / cnt
        dp_s[:, 0:256] = _lane_sel(lane, *_window_sums_ahead(ee, tmp_a, tmp_b, tm)) - dpooled
        ee[pl.ds(tm, HALO), :] = ee[0:HALO, :]

        v = p_ref[:, 256:512]
        s = _sig(p_ref[:, 512:768])
        ext_u[0:HALO, :] = ph_ref[:, 256:512] * _sig(ph_ref[:, 512:768]) * hm
        ext_u[pl.ds(HALO, tm), :] = v * s
        z, rstd, l, sl, act = _ln_silu(conv_ref[...], lng[...], lnb[...])
        dyb_b = dy_cols(256, 512).astype(BF16)
        dact = _dot_nt(dyb_b, wpw[...])
        g_wpw[...] += _dot_tn(act.astype(BF16), dyb_b)
        dl = dact * (sl * (1.0 + l * (1.0 - sl)))
        g_lng[...] += _colsum(dl * z)
        g_lnb[...] += _colsum(dl)
        dz = dl * lng[...]
        dc = rstd * (dz - jnp.mean(dz, axis=-1, keepdims=True) - z * jnp.mean(dz * z, axis=-1, keepdims=True))
        g_dwb[...] += _colsum(dc)
        dc_s[0:tm, :] = dc
        d_pad[pl.ds(8, tm), :] = dc
        _tap_grads(d_pad, ext_u, _CONV_OFFS, tm, g_dww, zbuf)
        du0 = _taps(dc_s, lambda j: dww[CONV_K - 1 - j:CONV_K - j, :], list(range(CONV_K)), tm, zbuf)
        dp_s[:, 256:512] = du0 * s
        dp_s[:, 512:768] = du0 * v * (s * (1.0 - s))
        dc_s[pl.ds(tm, HALO), :] = dc_s[0:HALO, :]

        ext_x[0:HALO, :] = ph_ref[:, 1280:1792] * hm
        ext_x[pl.ds(HALO, tm), :] = p_ref[:, 1280:1792]
        xc = xc_ref[...]
        xcb = xc.astype(BF16)
        r, ig, a, m = (gates_ref[:, j * D_RNN:(j + 1) * D_RNN] for j in range(4))
        sp = _softplus_neg(lam[...])
        ext_h[0:HALO, :] = hsh_ref[...] * hm
        ext_h[pl.ds(HALO, tm), :] = hs_ref[...]
        dyc = dy_cols(512, 1024)
        gl, dgl = _gelu_and_grad(p_ref[:, 768:1280])
        dp_s[:, 768:1280] = dyc * hs_ref[...] * dgl
        a_s[...] = a
        b_s[...] = dyc * gl
        _scan_rows(a_s, b_s, g_s, gcar, tm, reverse=True)
        g = g_s[...]
        da = g * ext_h[pl.ds(HALO - 1, tm), :]
        dm = g * (ig * xc)
        dig = g * (m * xc)
        dlog_a = da * a - dm * (a * a) / m
        g_lam[...] += _colsum(dlog_a * (-RG_C * r)) * (-_sig(-lam[...]))
        dra = (dlog_a * (-RG_C * sp)) * (r * (1.0 - r))
        dia = dig * (ig * (1.0 - ig))
        g_ba[...] += _colsum(dra)
        g_bx[...] += _colsum(dia)
        dra_b = dra.astype(BF16)
        dia_b = dia.astype(BF16)
        dxc = g * (m * ig) + _dot_nt(dra_b, wa[...]) + _dot_nt(dia_b, wx[...])
        g_wa[...] += _dot_tn(xcb, dra_b)
        g_wx[...] += _dot_tn(xcb, dia_b)
        g_cb[...] += _colsum(dxc)
        dx_s[0:tm, :] = dxc
        for k in range(RG_CONV_K):
            g_cw[k:k + 1, :] += _colsum(dxc * ext_x[pl.ds(HALO - (RG_CONV_K - 1) + k, tm), :])
        dxin = cw[RG_CONV_K - 1:RG_CONV_K, :] * dxc
        for j in range(1, RG_CONV_K):
            dxin = dxin + cw[RG_CONV_K - 1 - j:RG_CONV_K - j, :] * dx_s[pl.ds(j, tm), :]
        dp_s[:, 1280:1792] = dxin
        dx_s[pl.ds(tm, HALO), :] = dx_s[0:HALO, :]

        dpb = dp_s[...].astype(BF16)
        dp_ref[...] = dpb
        du = _dot(dpb, win_ref[...])
        r, n = _rms(h0_ref[...])
        g_g1[...] += _colsum(du * n)
        dh0 = dh1_ref[...] + _rms_bwd(du, n, r, g1_ref[...])
        if lead:
            dh0_ref[0:tm - lead, :] = dh0[lead:tm, :]
            dh0_ref[tm - lead:tm, :] = carry[...]
            carry[...] = dh0[0:lead, :]

            @pl.when(i == 0)
            def _():
                dlead_ref[...] = dh0[0:lead, :]
        else:
            dh0_ref[...] = dh0
            tail[0][...] = dh0.astype(BF16)

    ws = [mw[k] for k in _MIX_W]
    tile = lambda w: pl.BlockSpec((tm, w), lambda s: (nt - 1 - s, 0))
    halo = lambda w: pl.BlockSpec((HALO, w), lambda s: (jnp.maximum((nt - 1 - s) * hb - 1, 0), 0))
    lead_out = [pl.BlockSpec((lead, D_MODEL), lambda s: (0, 0))] if lead else [tile(D_MODEL)]
    outs = pl.pallas_call(
        _after(body, 12 + len(ws), deps), name="mixer_bwd", grid=(nt,),
        in_specs=[tile(D_IN), halo(D_IN), tile(D_MODEL), tile(D_RNN), halo(D_RNN), tile(D_CONV), tile(D_RNN), tile(4 * D_RNN),
                  tile(D_MODEL), _full(g1),
                  _resident(w_out), _resident(w_in)] + [_full(w) for w in ws] + [ANY] * len(deps),
        out_specs=[tile(D_IN), tile(D_MODEL)] + [pl.BlockSpec(shp, lambda s, nd=len(shp): (0,) * nd) for _, shp in _MIX_G]
        + lead_out,
        out_shape=[jax.ShapeDtypeStruct((t, D_IN), BF16), jax.ShapeDtypeStruct((t_out, D_MODEL), F32)]
        + [jax.ShapeDtypeStruct(shp, F32) for _, shp in _MIX_G]
        + [jax.ShapeDtypeStruct((lead, D_MODEL), F32) if lead else jax.ShapeDtypeStruct((t, D_MODEL), BF16)],
        scratch_shapes=[pltpu.VMEM((HALO + tm, D_POOL), F32), pltpu.VMEM((HALO + tm, D_CONV), F32),
                        pltpu.VMEM((HALO + tm, D_RNN), F32), pltpu.VMEM((HALO + tm, D_RNN), F32),
                        pltpu.VMEM((tm + HALO, D_POOL), F32), pltpu.VMEM((tm + HALO, D_CONV), F32),
                        pltpu.VMEM((tm + HALO, D_RNN), F32), pltpu.VMEM((HALO + tm, D_POOL), F32),
                        pltpu.VMEM((HALO + tm, D_POOL), F32), pltpu.VMEM((tm + 8, D_CONV), F32),
                        pltpu.VMEM((tm + 16, D_CONV), F32), pltpu.VMEM((tm, D_RNN), F32),
                        pltpu.VMEM((tm, D_RNN), F32), pltpu.VMEM((tm, D_RNN), F32), pltpu.VMEM((8, D_RNN), F32),
                        pltpu.VMEM((tm, D_MODEL), BF16), pltpu.VMEM((tm, D_IN), F32)]
        + [pltpu.VMEM((lead, D_MODEL), F32)] * bool(lead),
        compiler_params=_params("arbitrary"),
    )(p, p, dh1, hs, hs, conv, xc, gates, h0, g1, w_out, w_in, *ws, *deps)
    grads = {k: o for (k, _), o in zip(_MIX_G, outs[2:])}
    return outs[0], (outs[1], outs[-1]), grads


def _mid_fwd(y, h0, w_out, g, w_up):
    t = h0.shape[0]
    tm = _row_tile(t, TM_MAT)

    def body(y_ref, h0_ref, wo_ref, g_ref, wu_ref, h1_ref, u2_ref, f_ref):
        h1 = h0_ref[...] + _dot(y_ref[...], wo_ref[...])
        h1_ref[...] = h1
        u2 = (_rms(h1)[1] * g_ref[...]).astype(BF16)
        u2_ref[...] = u2
        for c in range(D_FF // FF_CHUNK):
            f_ref[:, c * FF_CHUNK:(c + 1) * FF_CHUNK] = _dot(u2, wu_ref[c]).astype(BF16)

    row = lambda w: pl.BlockSpec((tm, w), lambda i: (i, 0))
    return pl.pallas_call(
        body, name="mid_fwd", grid=(t // tm,),
        in_specs=[row(D_MODEL), row(D_MODEL), _resident(w_out), _full(g), _resident(w_up)],
        out_specs=[row(D_MODEL), row(D_MODEL), row(D_FF)],
        out_shape=[jax.ShapeDtypeStruct((t, D_MODEL), F32), jax.ShapeDtypeStruct((t, D_MODEL), BF16),
                   jax.ShapeDtypeStruct((t, D_FF), BF16)],
        compiler_params=_params("parallel"),
    )(y, h0, w_out, g, w_up)


def _down_proj(f_ref, h1_ref, wd_ref):
    acc = h1_ref[...]
    for c in range(D_FF // FF_CHUNK):
        cols = slice(c * FF_CHUNK, (c + 1) * FF_CHUNK)
        a = jnp.square(jnp.maximum(f_ref[:, cols].astype(F32), 0.0)).astype(BF16)
        acc = acc + _dot(a, wd_ref[cols, :])
    return acc


def _down_fwd(f, h1, w_down):
    t = h1.shape[0]
    tm = _row_tile(t, TM_MAT)

    def body(f_ref, h1_ref, wd_ref, h2_ref):
        h2_ref[...] = _down_proj(f_ref, h1_ref, wd_ref)

    row = lambda w: pl.BlockSpec((tm, w), lambda i: (i, 0))
    return pl.pallas_call(
        body, name="down_fwd", grid=(t // tm,),
        in_specs=[row(D_FF), row(D_MODEL), _resident(w_down)], out_specs=row(D_MODEL),
        out_shape=jax.ShapeDtypeStruct((t, D_MODEL), F32),
        compiler_params=_params("parallel"),
    )(f, h1, w_down)


def _down_fwd_loss(f, h1, w_down, g, tgt, t_real):
    t = h1.shape[0]
    tm = _row_tile(t, TM_MAT)
    nt = t // tm

    def body(f_ref, h1_ref, wd_ref, g_ref, tgt_in, loss_ref, dh_ref, dg_ref, dhb_ref, tgt_ref, carry):
        i = pl.program_id(0)

        @pl.when(i == 0)
        def _():
            loss_ref[...] = jnp.zeros(loss_ref.shape, F32)
            dg_ref[...] = jnp.zeros(dg_ref.shape, F32)
            carry[...] = jnp.zeros(carry.shape, F32)

        _lead_tile(tgt_in, tgt_ref, carry, i, nt, tm, N_META, tgt.shape[0])

        r, n = _rms(_down_proj(f_ref, h1_ref, wd_ref))
        row = lax.broadcasted_iota(jnp.int32, (tm, 1), 0) + i * tm
        valid = jnp.logical_and(row >= N_META, row < t_real)
        diff = jnp.where(valid, n * g_ref[...] - tgt_ref[...], 0.0)
        loss_ref[...] += 0.5 * jnp.sum(jnp.mean(diff * diff, axis=-1, keepdims=True))
        dy = diff * (1.0 / D_MODEL)
        dg_ref[...] += _colsum(dy * n)
        dh = _rms_bwd(dy, n, r, g_ref[...])
        dh_ref[...] = dh
        dhb_ref[...] = dh.astype(BF16)

    row = lambda w: pl.BlockSpec((tm, w), lambda i: (i, 0))
    return pl.pallas_call(
        body, name="down_fwd_loss", grid=(t // tm,),
        in_specs=[row(D_FF), row(D_MODEL), _resident(w_down), _full(g), row(D_MODEL)],
        out_specs=[pl.BlockSpec((8, 128), lambda i: (0, 0)), row(D_MODEL), pl.BlockSpec((1, D_MODEL), lambda i: (0, 0)),
                   row(D_MODEL)],
        out_shape=[jax.ShapeDtypeStruct((8, 128), F32), jax.ShapeDtypeStruct((t, D_MODEL), F32),
                   jax.ShapeDtypeStruct((1, D_MODEL), F32), jax.ShapeDtypeStruct((t, D_MODEL), BF16)],
        scratch_shapes=[pltpu.VMEM((tm, D_MODEL), F32), pltpu.VMEM((N_META, D_MODEL), F32)],
        compiler_params=_params("arbitrary"),
    )(f, h1, w_down, g, tgt)


def _mlp_bwd(dh2, f, h1, g, w_up, w_down, deps=()):
    t = dh2.shape[0]
    tm = _row_tile(t, TM_MLP_BWD)

    def body(dh2_ref, f_ref, wd_ref, wu_ref, h1_ref, g_ref, df_ref, dh1_ref, dg_ref, dh1b_ref):
        @pl.when(pl.program_id(0) == 0)
        def _():
            dg_ref[...] = jnp.zeros(dg_ref.shape, F32)

        dh2 = dh2_ref[...]
        dhb = dh2.astype(BF16)
        du2 = None
        for c in range(D_FF // FF_CHUNK):
            cols = slice(c * FF_CHUNK, (c + 1) * FF_CHUNK)
            dact = _dot_nt(dhb, wd_ref[c])
            df = (dact * (2.0 * jnp.maximum(f_ref[:, cols].astype(F32), 0.0))).astype(BF16)
            df_ref[:, cols] = df
            part = _dot_nt(df, wu_ref[c])
            du2 = part if du2 is None else du2 + part
        r, n = _rms(h1_ref[...])
        dg_ref[...] += _colsum(du2 * n)
        dh1 = dh2 + _rms_bwd(du2, n, r, g_ref[...])
        dh1_ref[...] = dh1
        dh1b_ref[...] = dh1.astype(BF16)

    row = lambda w: pl.BlockSpec((tm, w), lambda i: (i, 0))
    return pl.pallas_call(
        _after(body, 6, deps), name="mlp_bwd", grid=(t // tm,),
        in_specs=[row(D_MODEL), row(D_FF), _resident(w_down), _resident(w_up), row(D_MODEL), _full(g)] + [ANY] * len(deps),
        out_specs=[row(D_FF), row(D_MODEL), pl.BlockSpec((1, D_MODEL), lambda i: (0, 0)), row(D_MODEL)],
        out_shape=[jax.ShapeDtypeStruct((t, D_FF), BF16), jax.ShapeDtypeStruct((t, D_MODEL), F32),
                   jax.ShapeDtypeStruct((1, D_MODEL), F32), jax.ShapeDtypeStruct((t, D_MODEL), BF16)],
        compiler_params=_params("arbitrary"),
    )(dh2, f, w_down, w_up, h1, g, *deps)


def _tn_matmul(a, b, kc, nc, relu2, name, deps=()):
    t, k = a.shape
    n = b.shape[1]
    tt = _row_tile(t, TM_TN)
    gk, gn = k // kc, n // nc

    def body(a_ref, b_ref, o_ref):
        @pl.when(pl.program_id(2) == 0)
        def _():
            o_ref[...] = jnp.zeros(o_ref.shape, F32)

        av = a_ref[...]
        if relu2:
            av = jnp.square(jnp.maximum(av.astype(F32), 0.0))
        o_ref[...] += _dot_tn(av.astype(BF16), b_ref[...].astype(BF16))

    return pl.pallas_call(
        _after(body, 2, deps), name=name, grid=(gk, gn, t // tt),
        in_specs=[pl.BlockSpec((tt, kc), lambda ik, jn, it: (it, ik)), pl.BlockSpec((tt, nc), lambda ik, jn, it: (it, jn))]
        + [ANY] * len(deps),
        out_specs=pl.BlockSpec((None, kc, nc), lambda ik, jn, it: (ik * gn + jn, 0, 0)),
        out_shape=jax.ShapeDtypeStruct((gk * gn, kc, nc), F32),
        compiler_params=_params("parallel", "parallel", "arbitrary"),
    )(a, b, *deps)


def _block_diag(blocks):
    nb, hd, _ = blocks.shape
    eye = jnp.eye(nb, dtype=blocks.dtype)
    return (blocks[:, :, None, :] * eye[:, None, :, None]).reshape(nb * hd, nb * hd)


def _diag_blocks(m, nb):
    hd = m.shape[0] // nb
    eye = jnp.eye(nb, dtype=m.dtype)
    return jnp.sum(m.reshape(nb, hd, nb, hd) * eye[:, None, :, None], axis=2)


def _mixer_weights(w, l):
    row = lambda a: a.reshape(1, -1)
    return dict(
        wp=_block_diag(w["pool_w"][l]).astype(BF16), psc=row(w["pool_scale"][l]),
        dww=jnp.pad(w["convb_dw_w"][l], ((0, 32 - CONV_K), (0, 0))), dwb=row(w["convb_dw_b"][l]),
        lng=row(w["convb_ln_g"][l]), lnb=row(w["convb_ln_b"][l]), wpw=w["convb_pw_w"][l].astype(BF16),
        cw=jnp.pad(w["rg_conv_w"][l], ((0, 8 - RG_CONV_K), (0, 0))), cb=row(w["rg_conv_b"][l]),
        wa=_block_diag(w["rg_w_a"][l]).astype(BF16), ba=row(w["rg_b_a"][l]),
        wx=_block_diag(w["rg_w_x"][l]).astype(BF16), bx=row(w["rg_b_x"][l]), lam=row(w["rg_lambda"][l]))


def _local_step(h, tgt, t_real, t_pad, w, fetch, hooks):
    depth = 2
    saved = []
    big = []
    for l in range(depth):
        mw = _mixer_weights(w, l)
        g1 = w["mix_norm_g"][l].reshape(1, -1)
        g2 = w["mlp_norm_g"][l].reshape(1, -1)
        wl = dict(w_in=fetch(l, "w_in", h))
        if l == 0:
            y, p, u, hs, conv, xc, gates, h = _mixer_fwd(h, g1, wl["w_in"], mw, lead=w["meta_tokens"], t_pad=t_pad)
        else:
            y, p, u, hs, conv, xc, gates = _mixer_fwd(h, g1, wl["w_in"], mw)
        wl["w_out"], wl["w_up"] = fetch(l, "w_out", y), fetch(l, "w_up", y)
        h1, u2, f = _mid_fwd(y, h, wl["w_out"], g2, wl["w_up"])
        wl["w_down"] = fetch(l, "w_down", f)
        if l == depth - 1:
            loss, dh, dgf, dh_b = _down_fwd_loss(f, h1, wl["w_down"].reshape(D_FF, D_MODEL),
                                                 w["final_norm_g"].reshape(1, -1), tgt, t_real)
            h2 = None
        else:
            h2 = _down_fwd(f, h1, wl["w_down"].reshape(D_FF, D_MODEL))
        saved.append(dict(mw=mw, g1=g1, g2=g2, h0=h, p=p, u=u, y=y, hs=hs, conv=conv, xc=xc, gates=gates, h1=h1, u2=u2, f=f))
        big.append(wl)
        h = h2

    gs = {k: [None] * depth for k in ("mix_norm_g", "mlp_norm_g", "pool_w", "pool_scale", "convb_dw_w", "convb_dw_b",
                                      "convb_ln_g", "convb_ln_b", "convb_pw_w", "rg_conv_w", "rg_conv_b", "rg_w_a",
                                      "rg_b_a", "rg_w_x", "rg_b_x", "rg_lambda")}
    deps = ()
    for l in reversed(range(depth)):
        s, wl = saved[l], big[l]
        df, dh1, dg2, dh1_b = _mlp_bwd(dh, s["f"], s["h1"], s["g2"], wl["w_up"], wl["w_down"], deps)
        deps = hooks.point(l, "mlp_bwd", dh1)
        g_down = _tn_matmul(s["f"], dh_b, FF_CHUNK, D_MODEL, True, "dw_down", deps)
        hooks.grad(l, "w_down", g_down)
        deps = hooks.point(l, "dw_down", g_down)
        g_up = _tn_matmul(s["u2"], df, D_MODEL, FF_CHUNK, False, "dw_up", deps)
        hooks.grad(l, "w_up", g_up)
        deps = hooks.point(l, "dw_up", g_up)
        g_out = _tn_matmul(s["y"], dh1_b, D_MODEL, D_MODEL, False, "dw_out", deps)
        hooks.grad(l, "w_out", g_out.reshape(N_CHIPS, D_MODEL // N_CHIPS, D_MODEL))
        deps = hooks.point(l, "dw_out", g_out)
        dp, dh, mg = _mixer_bwd(s["p"], dh1, s["hs"], s["conv"], s["xc"], s["gates"], s["h0"], s["g1"], wl["w_out"],
                                wl["w_in"], s["mw"], deps, lead=0 if l else N_META, t_real=t_real)
        dh, dh_b = dh
        if l == 0:
            dmeta = dh_b
        gs["mix_norm_g"][l] = mg["g1"][0]
        gs["mlp_norm_g"][l] = dg2[0]
        gs["pool_w"][l] = _diag_blocks(mg["wp"], D_POOL // POOL_GW)
        gs["pool_scale"][l] = mg["psc"][0]
        gs["convb_dw_w"][l] = jnp.sum(mg["dww"][:CONV_K], axis=1)
        gs["convb_dw_b"][l] = mg["dwb"][0]
        gs["convb_ln_g"][l] = mg["lng"][0]
        gs["convb_ln_b"][l] = mg["lnb"][0]
        gs["convb_pw_w"][l] = mg["wpw"]
        gs["rg_conv_w"][l] = mg["cw"][:RG_CONV_K]
        gs["rg_conv_b"][l] = mg["cb"][0]
        gs["rg_w_a"][l] = _diag_blocks(mg["wa"], D_RNN // RG_HD)
        gs["rg_b_a"][l] = mg["ba"][0]
        gs["rg_w_x"][l] = _diag_blocks(mg["wx"], D_RNN // RG_HD)
        gs["rg_b_x"][l] = mg["bx"][0]
        gs["rg_lambda"][l] = mg["lam"][0]
        if l == 0:
            gsmall = {k: jnp.stack(v) for k, v in gs.items()}
            gsmall["final_norm_g"] = dgf[0]
            gsmall["meta_tokens"] = dmeta
            started = hooks.small(gsmall)
        deps = hooks.point(l, "mixer_bwd", started[0] if l == 0 and started else dh)
        g_in = _tn_matmul(dp, s["u"], D_IN, D_MODEL, False, "dw_in", deps).reshape(N_CHIPS, D_IN // N_CHIPS, D_MODEL)
        hooks.grad(l, "w_in", g_in)
        deps = hooks.point(l, "dw_in", g_in)
    return loss[0, 0], dh


def _place():
    return lax.axis_index("x"), lax.axis_index("y"), lax.axis_index("c")


def _other_chips(x, y):
    return [(1 - x, y), (x, 1 - y), (1 - x, 1 - y)]


HBM_SPEC = pl.BlockSpec(memory_space=pltpu.HBM)
SEM_SPEC = pl.BlockSpec(memory_space=pltpu.SEMAPHORE)
DATAFLOW = pltpu.SideEffectType.DATAFLOW_SIDE_EFFECTING


def _gather_copies(src_refs, land_refs, send_sem, recv_sem, first):
    x, y, c = _place()
    me = 2 * x + y
    out = []
    for n in range(len(src_refs)):
        for j, (px, py) in enumerate(_other_chips(x, y) + [(x, y)]):
            k = first + N_CHIPS * n + j
            out.append(pltpu.make_async_remote_copy(src_refs[n], land_refs[n].at[me], send_sem.at[k], recv_sem.at[k],
                                                    device_id=(px, py, c), device_id_type=MESH))
    return out


def _gather_start(groups, name):
    srcs = [pltpu.with_memory_space_constraint(s, pltpu.HBM) for g in groups for s in g]
    lands = [pltpu.with_memory_space_constraint(lax.empty((N_CHIPS,) + s.shape, s.dtype), pltpu.HBM) for g in groups for s in g]
    n, ng = len(srcs), len(groups)
    first = [sum(len(g) for g in groups[:i]) for i in range(ng)]

    def body(*refs):
        src_refs, land_refs = refs[:n], refs[n:2 * n]
        sems = refs[2 * n:2 * n + 2 * ng]
        token = refs[-1]
        for gi, g in enumerate(groups):
            lo, hi = first[gi], first[gi] + len(g)
            for cp in _gather_copies(src_refs[lo:hi], land_refs[lo:hi], sems[2 * gi], sems[2 * gi + 1], 0):
                cp.start()
        token[...] = jnp.zeros(token.shape, token.dtype)

    sem_shapes = [pltpu.SemaphoreType.DMA((N_CHIPS * len(g),)) for g in groups for _ in range(2)]
    outs = pl.pallas_call(
        body, name=name,
        out_shape=sem_shapes + [pltpu.HBM(a.shape, a.dtype) for a in srcs + lands] + [jax.ShapeDtypeStruct((8, 128), F32)],
        in_specs=[HBM_SPEC] * (2 * n),
        out_specs=[SEM_SPEC] * (2 * ng) + [HBM_SPEC] * (2 * n) + [pl.BlockSpec(memory_space=pltpu.VMEM)],
        input_output_aliases={i: 2 * ng + i for i in range(2 * n)},
        compiler_params=pltpu.CompilerParams(has_side_effects=DATAFLOW),
    )(*srcs, *lands)
    sems, thru, token = outs[:2 * ng], outs[2 * ng:2 * ng + 2 * n], outs[-1]
    state = []
    for gi, g in enumerate(groups):
        lo, hi = first[gi], first[gi] + len(g)
        state.append((sems[2 * gi], sems[2 * gi + 1], thru[lo:hi], thru[n + lo:n + hi]))
    return state, token


def _gather_wait(state, after, name):
    send_sem, recv_sem, srcs, lands = state
    n = len(srcs)

    def body(*refs):
        src_refs, land_refs = refs[:n], refs[n:2 * n]
        send, recv = refs[2 * n], refs[2 * n + 1]
        for cp in _gather_copies(src_refs, land_refs, send, recv, 0):
            cp.wait_send()
            cp.wait_recv()

    outs = pl.pallas_call(
        body, name=name,
        out_shape=[pltpu.HBM(a.shape, a.dtype) for a in list(srcs) + list(lands)],
        in_specs=[HBM_SPEC] * (2 * n) + [SEM_SPEC, SEM_SPEC, ANY],
        out_specs=[HBM_SPEC] * (2 * n),
        input_output_aliases={i: i for i in range(2 * n)},
        compiler_params=pltpu.CompilerParams(has_side_effects=DATAFLOW),
    )(*srcs, *lands, send_sem, recv_sem, after)
    return outs[n:]


def _add_halves(g, recv, c1):
    nk, r, cd = g.shape
    r2 = r // 2
    rc = r2 // ROW_CHUNKS

    def body(c_ref, g_ref, r_ref, pab_ref):
        pab_ref[...] = (g_ref[...] + r_ref[...]).astype(BF16)

    blk = pl.BlockSpec((None, rc, cd), lambda k, j, c_ref: (k, j, 0))
    return pl.pallas_call(
        body, name="rs_add_halves",
        grid_spec=pltpu.PrefetchScalarGridSpec(
            num_scalar_prefetch=1, grid=(nk, ROW_CHUNKS),
            in_specs=[pl.BlockSpec((None, rc, cd), lambda k, j, c_ref: (k, c_ref[0] * ROW_CHUNKS + j, 0)), blk], out_specs=blk),
        out_shape=jax.ShapeDtypeStruct((nk, r2, cd), BF16),
        compiler_params=_params("parallel", "parallel"),
    )(c1, g, recv)


def _sum_partials(g, recv_sibling, recv_chips, c_me):
    nk, r, cd = g.shape
    r2 = r // 2
    rc = r2 // ROW_CHUNKS

    def body(cm_ref, g_ref, a_ref, r_ref, s_ref):
        own = g_ref[...] + a_ref[...]
        s_ref[...] = ((own + r_ref[0].astype(F32)) + r_ref[1].astype(F32)) + r_ref[2].astype(F32)

    return pl.pallas_call(
        body, name="rs_sum_partials",
        grid_spec=pltpu.PrefetchScalarGridSpec(
            num_scalar_prefetch=1, grid=(ROW_CHUNKS,),
            in_specs=[pl.BlockSpec((None, rc, cd), lambda j, cm: (cm[1], cm[0] * ROW_CHUNKS + j, 0)),
                      pl.BlockSpec((None, rc, cd), lambda j, cm: (cm[1], j, 0)),
                      pl.BlockSpec((3, rc, cd), lambda j, cm: (0, j, 0))],
            out_specs=pl.BlockSpec((rc, cd), lambda j, cm: (j, 0))),
        out_shape=jax.ShapeDtypeStruct((r2, cd), F32),
        compiler_params=_params("parallel"),
    )(c_me, g, recv_sibling, recv_chips)


def _split_start(name, srcs, lands, ncopies, make_copies):
    srcs = [pltpu.with_memory_space_constraint(s, pltpu.HBM) for s in srcs]
    lands = [pltpu.with_memory_space_constraint(a, pltpu.HBM) for a in lands]
    n, m = len(srcs), len(lands)

    def body(*refs):
        src_refs, land_refs = refs[:n], refs[n:n + m]
        send, recv, token = refs[n + m], refs[n + m + 1], refs[-1]
        for cp in make_copies(src_refs, land_refs, send, recv):
            cp.start()
        token[...] = jnp.zeros(token.shape, token.dtype)

    outs = pl.pallas_call(
        body, name=name,
        out_shape=[pltpu.SemaphoreType.DMA((ncopies,)), pltpu.SemaphoreType.DMA((ncopies,))]
        + [pltpu.HBM(a.shape, a.dtype) for a in srcs + lands] + [jax.ShapeDtypeStruct((8, 128), F32)],
        in_specs=[HBM_SPEC] * (n + m),
        out_specs=[SEM_SPEC, SEM_SPEC] + [HBM_SPEC] * (n + m) + [pl.BlockSpec(memory_space=pltpu.VMEM)],
        input_output_aliases={i: 2 + i for i in range(n + m)},
        compiler_params=pltpu.CompilerParams(has_side_effects=DATAFLOW),
    )(*srcs, *lands)
    return (outs[0], outs[1], outs[2:2 + n], outs[2 + n:2 + n + m], make_copies), outs[-1]


def _split_wait(name, state, after):
    send_sem, recv_sem, srcs, lands, make_copies = state
    n, m = len(srcs), len(lands)

    def body(*refs):
        src_refs, land_refs = refs[:n], refs[n:n + m]
        for cp in make_copies(src_refs, land_refs, refs[n + m], refs[n + m + 1]):
            cp.wait_send()
            cp.wait_recv()

    outs = pl.pallas_call(
        body, name=name,
        out_shape=[pltpu.HBM(a.shape, a.dtype) for a in list(srcs) + list(lands)],
        in_specs=[HBM_SPEC] * (n + m) + [SEM_SPEC, SEM_SPEC, ANY],
        out_specs=[HBM_SPEC] * (n + m),
        input_output_aliases={i: i for i in range(n + m)},
        compiler_params=pltpu.CompilerParams(has_side_effects=DATAFLOW),
    )(*srcs, *lands, send_sem, recv_sem, after)
    return outs[:n], outs[n:]


def _copies_to_sibling(src_of):
    def make(src_refs, land_refs, send, recv):
        x, y, c = _place()
        return [pltpu.make_async_remote_copy(src_of(src_refs[i], c), land_refs[i], send.at[i], recv.at[i],
                                             device_id=(x, y, 1 - c), device_id_type=MESH) for i in range(len(src_refs))]
    return make


def _copies_to_chips(src_refs, land_refs, send, recv):
    x, y, c = _place()
    return [pltpu.make_async_remote_copy(src_refs[i].at[2 * px + py], land_refs[i].at[j], send.at[3 * i + j], recv.at[3 * i + j],
                                         device_id=(px, py, c), device_id_type=MESH)
            for i in range(len(src_refs)) for j, (px, py) in enumerate(_other_chips(x, y))]


def _other_half_rows(ref, c):
    r2 = ref.shape[1] // 2
    return ref.at[:, pl.ds(pl.multiple_of((1 - c) * r2, 8), r2)]


class _ReduceScatter:
    def __init__(self, tag, grads, c1, me1):
        self.tag, self.grads, self.c1, self.me1 = tag, grads, c1, me1

    def start(self):
        lands = [lax.empty((g.shape[0], g.shape[1] // 2, g.shape[2]), F32) for g in self.grads]
        self.state, token = _split_start("rs_%s_a_start" % self.tag, self.grads, lands, len(self.grads),
                                         _copies_to_sibling(_other_half_rows))
        return token

    def to_chips(self, after):
        self.halves = _split_wait("rs_%s_a_wait" % self.tag, self.state, after)
        pabs = [_add_halves(g, r, self.c1) for g, r in zip(*self.halves)]
        lands = [lax.empty((3,) + p.shape[1:], BF16) for p in pabs]
        self.state, token = _split_start("rs_%s_b_start" % self.tag, pabs, lands, 3 * len(pabs), _copies_to_chips)
        return token

    def to_sibling(self, after):
        _, recv = _split_wait("rs_%s_b_wait" % self.tag, self.state, after)
        c_me = jnp.concatenate([self.c1, self.me1])
        sums = [_sum_partials(g, ra, rb, c_me) for g, ra, rb in zip(*self.halves, recv)]
        lands = [lax.empty(s.shape, F32) for s in sums]
        self.state, token = _split_start("rs_%s_c_start" % self.tag, sums, lands, len(sums),
                                         _copies_to_sibling(lambda ref, c: ref))
        return token

    def finish(self, after):
        return list(zip(*_split_wait("rs_%s_c_wait" % self.tag, self.state, after)))


def _add_lists(a_list, b_list):
    n = len(a_list)

    def body(*refs):
        for i in range(n):
            refs[2 * n + i][...] = refs[i][...] + refs[n + i][...]

    vm = pl.BlockSpec(memory_space=pltpu.VMEM)
    return pl.pallas_call(
        body, name="add_lists", in_specs=[vm] * (2 * n), out_specs=[vm] * n,
        out_shape=[jax.ShapeDtypeStruct(a.shape, a.dtype) for a in a_list],
        compiler_params=pltpu.CompilerParams(vmem_limit_bytes=VMEM_LIMIT),
    )(*a_list, *b_list)


def _copies_to_peer(stage):
    def make(src_refs, land_refs, send, recv):
        x, y, c = _place()
        peer = [(x, y, 1 - c), (1 - x, y, c), (x, 1 - y, c)][stage]
        return [pltpu.make_async_remote_copy(src_refs[i], land_refs[i], send.at[i], recv.at[i], device_id=peer, device_id_type=MESH)
                for i in range(len(src_refs))]
    return make


class _AllReduceSmall:
    def __init__(self, vs):
        self.vs, self.stage = list(vs), 0

    def _start(self):
        lands = [lax.empty(v.shape, v.dtype) for v in self.vs]
        self.state, token = _split_start("ar_small_start_%d" % self.stage, self.vs, lands, len(self.vs), _copies_to_peer(self.stage))
        return token

    def start(self):
        return self._start()

    def step(self, after):
        mine, theirs = _split_wait("ar_small_wait_%d" % self.stage, self.state, after)
        self.vs = _add_lists(mine, theirs)
        self.stage += 1
        return self._start() if self.stage < 3 else self.vs[0]


def _adamw_math(w, g, m, v):
    m = ADAM_B1 * m + (1.0 - ADAM_B1) * g
    v = ADAM_B2 * v + (1.0 - ADAM_B2) * jnp.square(g)
    m_hat = m / (1.0 - ADAM_B1 ** ADAM_STEP)
    v_hat = v / (1.0 - ADAM_B2 ** ADAM_STEP)
    return -ADAM_LR * (m_hat / (jnp.sqrt(v_hat) + ADAM_EPS) + ADAM_WD * w), m, v


def _adamw_big_layer(layer, w, m, v, own, sib, c1, prev):
    _, r, cd = w.shape
    rc = r // 2 // ROW_CHUNKS

    def body(c_ref, w_ref, m_ref, v_ref, own_ref, sib_ref, *rest):
        g_ref, d_ref, mo_ref, vo_ref, token = rest[-5:]
        g = jnp.where(pl.program_id(0) == c_ref[0], own_ref[...], sib_ref[...])
        g_ref[...] = g
        d_ref[...], mo_ref[...], vo_ref[...] = _adamw_math(w_ref[...], g, m_ref[...], v_ref[...])
        token[...] = jnp.zeros(token.shape, F32)

    blk = pl.BlockSpec((None, rc, cd), lambda hh, j, c_ref: (layer, hh * ROW_CHUNKS + j, 0))
    half = pl.BlockSpec((rc, cd), lambda hh, j, c_ref: (j, 0))
    prev = () if prev is None else tuple(prev)
    outs = pl.pallas_call(
        body, name="adamw_big",
        grid_spec=pltpu.PrefetchScalarGridSpec(
            num_scalar_prefetch=1, grid=(2, ROW_CHUNKS), in_specs=[blk, blk, blk, half, half] + [ANY] * len(prev),
            out_specs=[blk] * 4 + [pl.BlockSpec((8, 128), lambda hh, j, c_ref: (0, 0))]),
        out_shape=[jax.ShapeDtypeStruct(w.shape, F32)] * 4 + [jax.ShapeDtypeStruct((8, 128), F32)],
        input_output_aliases={6 + i: i for i in range(len(prev))},
        compiler_params=_params("arbitrary", "arbitrary"),
    )(c1, w, m, v, own, sib, *prev)
    return outs[:4], outs[4]


def _adamw_small(ws, gs, ms, vs):
    n = len(ws)

    def body(*refs):
        w_refs, g_refs, m_refs, v_refs = refs[:n], refs[n:2 * n], refs[2 * n:3 * n], refs[3 * n:4 * n]
        outs = refs[4 * n:]
        for i in range(n):
            outs[3 * i][...], outs[3 * i + 1][...], outs[3 * i + 2][...] = _adamw_math(
                w_refs[i][...], g_refs[i][...], m_refs[i][...], v_refs[i][...])

    vm = pl.BlockSpec(memory_space=pltpu.VMEM)
    outs = pl.pallas_call(
        body, name="adamw_small", in_specs=[vm] * (4 * n), out_specs=[vm] * (3 * n),
        out_shape=[jax.ShapeDtypeStruct(w.shape, F32) for w in ws for _ in range(3)],
        compiler_params=pltpu.CompilerParams(vmem_limit_bytes=VMEM_LIMIT),
    )(*ws, *gs, *ms, *vs)
    return [outs[3 * i:3 * i + 3] for i in range(n)]


LANES = 128
SUBLANES = 8
SHARDED_AXIS = {"meta_tokens": 1, "convb_dw_w": 2, "convb_pw_w": 1, "rg_conv_w": 2}


def _rows_of(size):
    return -(-size // (LANES * SUBLANES)) * SUBLANES


def _as_rows(a, rows=None):
    flat = a.reshape(-1)
    rows = _rows_of(flat.size) if rows is None else rows
    return jnp.pad(flat, (0, rows * LANES - flat.size)).reshape(rows, LANES)


class _GradientSchedule:
    GROUPS = {"l1": [(1, "w_down"), (1, "w_up"), (1, "w_out"), (1, "w_in")], "a0": [(0, "w_down"), (0, "w_up")],
              "b0": [(0, "w_out")], "c0": [(0, "w_in")]}
    PLAN = {
        (1, "dw_in"): [("l1", "start")],
        (0, "mlp_bwd"): [("l1", "to_chips")],
        (0, "dw_up"): [("l1", "to_sibling"), ("a0", "start")],
        (0, "dw_out"): [("l1", "finish"), ("a0", "to_chips"), ("b0", "start")],
        (0, "mixer_bwd"): [("a0", "to_sibling"), ("b0", "to_chips"), ("small", "step")],
        (0, "dw_in"): [("c0", "start"), ("small", "step"), ("c0", "to_chips"), ("a0", "finish"), ("b0", "to_sibling")],
    }

    def __init__(self, w, mom, var, c1, me1):
        self.w, self.mom, self.var, self.c1, self.me1 = w, mom, var, c1, me1
        self.grads, self.chains, self.out = {}, {}, {}

    def grad(self, layer, name, g):
        self.grads[layer, name] = g

    def small(self, gsmall):
        self.small_sum = _AllReduceSmall([g.reshape(1, -1) if g.ndim == 1 else g for g in (gsmall[k] for k in SMALL)])
        return (self.small_sum.start(),)

    def point(self, layer, kernel_name, after):
        return self.run(self.PLAN.get((layer, kernel_name), ()), after) or (after,)

    def run(self, actions, after):
        deps = []
        for tag, stage in actions:
            if tag == "small":
                deps.append(self.small_sum.step(after))
            elif stage == "start":
                self.chains[tag] = _ReduceScatter(tag, [self.grads[lk] for lk in self.GROUPS[tag]], self.c1, self.me1)
                deps.append(self.chains[tag].start())
            elif stage == "finish":
                for (layer, k), (own, sib) in zip(self.GROUPS[tag], self.chains[tag].finish(after)):
                    self.out[k], token = _adamw_big_layer(layer, self.w[k], self.mom[k], self.var[k], own, sib, self.c1,
                                                          self.out.get(k))
                    deps.append(token)
            else:
                deps.append(getattr(self.chains[tag], stage)(after))
            after = deps[-1]
        self.last = after
        return tuple(deps)


def _from_shard_major(name, sm):
    if name == "meta_tokens":
        return sm.transpose(1, 0, 2).reshape(N_META, -1)
    if name == "convb_pw_w":
        return sm.transpose(1, 0, 2, 3).reshape(2, -1, D_CONV)
    return sm.transpose(1, 2, 0, 3).reshape(sm.shape[1], sm.shape[2], -1)


def kernel(x, meta_tokens, mix_norm_g, w_in, pool_w, pool_scale, convb_dw_w, convb_dw_b, convb_ln_g, convb_ln_b, convb_pw_w, rg_conv_w, rg_conv_b, rg_w_a, rg_b_a, rg_w_x, rg_b_x, rg_lambda, w_out, mlp_norm_g, w_up, w_down, final_norm_g, loss_target, m_meta_tokens, m_mix_norm_g, m_w_in, m_pool_w, m_pool_scale, m_convb_dw_w, m_convb_dw_b, m_convb_ln_g, m_convb_ln_b, m_convb_pw_w, m_rg_conv_w, m_rg_conv_b, m_rg_w_a, m_rg_b_a, m_rg_w_x, m_rg_b_x, m_rg_lambda, m_w_out, m_mlp_norm_g, m_w_up, m_w_down, m_final_norm_g, v_meta_tokens, v_mix_norm_g, v_w_in, v_pool_w, v_pool_scale, v_convb_dw_w, v_convb_dw_b, v_convb_ln_g, v_convb_ln_b, v_convb_pw_w, v_rg_conv_w, v_rg_conv_b, v_rg_w_a, v_rg_b_a, v_rg_w_x, v_rg_b_x, v_rg_lambda, v_w_out, v_mlp_norm_g, v_w_up, v_w_down, v_final_norm_g):
    given = dict(locals())
    w = {k: given[k] for k in WEIGHTS}
    mom = {k: given["m_" + k] for k in WEIGHTS}
    var = {k: given["v_" + k] for k in WEIGHTS}
    xi, yi, ci = _place()
    me1 = (2 * xi + yi).astype(jnp.int32).reshape(1)
    c1 = ci.astype(jnp.int32).reshape(1)

    small_rows = [_rows_of(w[k].size) for k in SMALL_SHARDED]
    small_pack = jnp.concatenate([_as_rows(w[k]) for k in SMALL_SHARDED])
    transposed = lambda d: {**d, "w_in": d["w_in"].transpose(0, 2, 1)}
    wt, momt, vart = transposed(w), transposed(mom), transposed(var)
    order = [[(0, "w_in"), "small"], [(0, "w_out"), (0, "w_up")], [(0, "w_down")], [(1, "w_in")], [(1, "w_out"), (1, "w_up")],
             [(1, "w_down")]]
    state, token = _gather_start([[wt["w_in"][0].astype(BF16), small_pack]], "gather_start_0")
    shard = lambda l, k: (wt[k][l] + token[0, 0]).astype(BF16)
    rest, token_rest = _gather_start([[shard(*lk) for lk in g] for g in order[1:]], "gather_start_1")
    state = state + rest
    landed = {}

    def fetch(l, k, after):
        gi = [i for i, g in enumerate(order) if (l, k) in g][0]
        if gi not in landed:
            landed[gi] = _gather_wait(state[gi], after, "gather_wait_%d" % gi)
        raw = landed[gi][order[gi].index((l, k))]
        if k == "w_in":
            return raw.reshape(D_IN, D_MODEL)
        return raw.reshape(D_MODEL, D_MODEL) if k == "w_out" else raw

    seq = x.shape[1]
    t_real = N_META + seq
    t_pad = -(-t_real // ROW_ALIGN) * ROW_ALIGN
    landed[0] = _gather_wait(state[0], token_rest, "gather_wait_0")
    wfull = {k: (w[k] + token[0, 0] if k in ("pool_w", "rg_w_a", "rg_w_x") else w[k]) for k in WEIGHTS}
    off = 0
    for k, rows in zip(SMALL_SHARDED, small_rows):
        sm = landed[0][1][:, off:off + rows].reshape(N_CHIPS, -1)[:, :w[k].size].reshape((N_CHIPS,) + w[k].shape)
        wfull[k] = _from_shard_major(k, sm)
        off += rows
    sched = _GradientSchedule(wt, momt, vart, c1, me1)
    loss, dh = _local_step(x[0], loss_target[0], t_real, t_pad, wfull, fetch, sched)
    grad_x = dh[None]

    names = SMALL
    two_d = lambda a: a.reshape(1, -1) if a.ndim == 1 else a
    sched.run([("small", "step")], sched.last)
    summed = dict(zip(names, sched.small_sum.vs))
    for k in SMALL_SHARDED:
        ax = SHARDED_AXIS[k]
        summed[k] = lax.dynamic_slice_in_dim(summed[k], me1[0] * w[k].shape[ax], w[k].shape[ax], axis=ax)

    out = {}
    res = _adamw_small([two_d(w[k]) for k in names], [summed[k] for k in names], [two_d(mom[k]) for k in names],
                       [two_d(var[k]) for k in names])
    for k, (d, m2, v2) in zip(names, res):
        out[k] = tuple(o.reshape(w[k].shape) for o in (summed[k], d, m2, v2))
    sched.run([("b0", "finish"), ("c0", "to_sibling"), ("c0", "finish")], res[0][0])
    out.update(sched.out)
    out["w_in"] = tuple(o.transpose(0, 2, 1) for o in out["w_in"])

    loss = lax.psum(loss, ("x", "y", "c"))
    return (loss, grad_x, *[out[k][0] for k in WEIGHTS], *[out[k][1] for k in WEIGHTS],
            *[out[k][2] for k in WEIGHTS], *[out[k][3] for k in WEIGHTS])
```

```python
import functools

import jax
import jax.numpy as jnp
from jax import lax
from jax.experimental import pallas as pl
from jax.experimental.pallas import tpu as pltpu

F32, BF16 = jnp.float32, jnp.bfloat16
MESH = pl.DeviceIdType.MESH
ANY = pl.BlockSpec(memory_space=pl.ANY)

D_MODEL = 1024
N_META = 16
D_POOL = 256
D_CONV = 256
D_RNN = 512
D_IN = D_POOL + 2 * D_CONV + 2 * D_RNN
D_FF = 4096
FF_CHUNK = 1024
POOL_GW = 64
CONV_K = 31
RG_CONV_K = 4
RG_HD = 64
RG_C = 8.0
EPS = 1e-6
ADAM_LR, ADAM_B1, ADAM_B2, ADAM_EPS, ADAM_WD, ADAM_STEP = 0.001, 0.9, 0.999, 1e-08, 0.01, 10

HALO = 32
ROW_ALIGN = 256
TM_MIX = 384
TM_MAT = 768
TM_MLP_BWD = 384
TM_TN = 2816
N_CHIPS = 4
ROW_CHUNKS = 1
VMEM_LIMIT = 56 * 1024 * 1024

BIG = ("w_in", "w_out", "w_up", "w_down")
SMALL_SHARDED = ("meta_tokens", "convb_dw_w", "convb_pw_w", "rg_conv_w")
SMALL_REPL = ("mix_norm_g", "pool_w", "pool_scale", "convb_dw_b", "convb_ln_g", "convb_ln_b", "rg_conv_b",
              "rg_w_a", "rg_b_a", "rg_w_x", "rg_b_x", "rg_lambda", "mlp_norm_g", "final_norm_g")
SMALL = SMALL_REPL + SMALL_SHARDED
WEIGHTS = ("meta_tokens", "mix_norm_g", "w_in", "pool_w", "pool_scale", "convb_dw_w", "convb_dw_b", "convb_ln_g",
           "convb_ln_b", "convb_pw_w", "rg_conv_w", "rg_conv_b", "rg_w_a", "rg_b_a", "rg_w_x", "rg_b_x",
           "rg_lambda", "w_out", "mlp_norm_g", "w_up", "w_down", "final_norm_g")


def _params(*sem):
    return pltpu.CompilerParams(dimension_semantics=sem, vmem_limit_bytes=VMEM_LIMIT)


def _row_tile(t, cap):
    best = None
    for tm in range(128, cap + 1, 128):
        if t % tm == 0:
            best = tm
    assert best is not None, (t, cap)
    return best


def _dot(a, b):
    return jnp.dot(a, b, preferred_element_type=F32)


def _dot_nt(a, b):
    return lax.dot_general(a, b, (((1,), (1,)), ((), ())), preferred_element_type=F32)


def _dot_tn(a, b):
    return lax.dot_general(a, b, (((0,), (0,)), ((), ())), preferred_element_type=F32)


def _rms(x):
    r = lax.rsqrt(jnp.mean(x * x, axis=-1, keepdims=True) + EPS)
    return r, x * r


def _rms_bwd(du, n, r, g):
    dn = du * g
    return r * (dn - n * jnp.mean(dn * n, axis=-1, keepdims=True))


def _sig(x):
    return jax.nn.sigmoid(x)


def _colsum(x):
    return jnp.sum(x, axis=0, keepdims=True)


def _one_minus_sq(a, log_a):
    x = 2.0 * log_a
    series = -x * (1.0 + x * (0.5 + x * (1.0 / 6)))
    return jnp.where(x > -0.01, series, 1.0 - a * a)


_GELU_K0 = 0.7978845608028654
_GELU_K1 = 0.044715


def _gelu_and_grad(x):
    th = jnp.tanh(_GELU_K0 * (x + _GELU_K1 * x * x * x))
    val = 0.5 * x * (1.0 + th)
    grad = 0.5 * (1.0 + th) + 0.5 * x * (1.0 - th * th) * _GELU_K0 * (1.0 + 3.0 * _GELU_K1 * x * x)
    return val, grad


def _full(a):
    nd = a.ndim
    return pl.BlockSpec(a.shape, lambda *_: (0,) * nd)


def _resident(a):
    nd = a.ndim
    return pl.BlockSpec(a.shape, lambda *_: (0,) * nd, pipeline_mode=pl.Buffered(1))


def _after(body, n_in, deps):
    def wrapped(*refs):
        return body(*refs[:n_in], *refs[n_in + len(deps):])
    return wrapped


def _lane_sel(lane, a2, a4, a8, a16):
    return jnp.where(lane < POOL_GW, a2, jnp.where(lane < 2 * POOL_GW, a4, jnp.where(lane < 3 * POOL_GW, a8, a16)))


def _window_sums_back(src, tmp_a, tmp_b, tm):
    n = HALO + tm
    rows = lambda ref, lo, back: ref[pl.ds(lo - back, n - lo), :]
    tmp_a[pl.ds(8, n - 8), :] = rows(src, 8, 0) + rows(src, 8, 1)
    tmp_b[pl.ds(16, n - 16), :] = rows(tmp_a, 16, 0) + rows(tmp_a, 16, 2)
    s2 = rows(tmp_a, HALO, 0)
    tmp_a[pl.ds(24, n - 24), :] = rows(tmp_b, 24, 0) + rows(tmp_b, 24, 4)
    s8 = rows(tmp_a, HALO, 0)
    return s2, rows(tmp_b, HALO, 0), s8, s8 + rows(tmp_a, HALO, 8)


def _window_sums_ahead(src, tmp_a, tmp_b, tm):
    rows = lambda ref, n, ahead: ref[pl.ds(ahead, n), :]
    tmp_a[pl.ds(0, tm + 24), :] = rows(src, tm + 24, 0) + rows(src, tm + 24, 1)
    tmp_b[pl.ds(0, tm + 16), :] = rows(tmp_a, tm + 16, 0) + rows(tmp_a, tm + 16, 2)
    s2 = rows(tmp_a, tm, 0)
    tmp_a[pl.ds(0, tm + 8), :] = rows(tmp_b, tm + 8, 0) + rows(tmp_b, tm + 8, 4)
    s8 = rows(tmp_a, tm, 0)
    return s2, rows(tmp_b, tm, 0), s8, s8 + rows(tmp_a, tm, 8)


def _pool_counts(tm, t0):
    lane = lax.broadcasted_iota(jnp.int32, (tm, D_POOL), 1)
    row = lax.broadcasted_iota(jnp.int32, (tm, D_POOL), 0) + t0
    cnt = jnp.minimum(row + 1, _lane_sel(lane, 2, 4, 8, 16)).astype(F32)
    return lane, cnt


def _pool_fwd(ext_q, tmp_a, tmp_b, tm, t0):
    lane, cnt = _pool_counts(tm, t0)
    q = ext_q[pl.ds(HALO, tm), :]
    pooled = _lane_sel(lane, *_window_sums_back(ext_q, tmp_a, tmp_b, tm)) / cnt - q
    return pooled, lane, cnt


def _taps(src, w_of, offs, tm, zbuf):
    acc = None
    for r in range(8):
        ks = [k for k in range(len(offs)) if offs[k] % 8 == r]
        if not ks:
            continue
        rows = tm + (8 if r else 0)
        z = w_of(ks[0]) * src[pl.ds(offs[ks[0]] - r, rows), :]
        for k in ks[1:]:
            z = z + w_of(k) * src[pl.ds(offs[k] - r, rows), :]
        if r:
            zbuf[...] = z
            z = zbuf[pl.ds(r, tm), :]
        acc = z if acc is None else acc + z
    return acc


def _tap_grads(d_pad, src, offs, tm, g_ref, zbuf):
    ch = src.shape[-1]
    for r in range(8):
        ks = [k for k in range(len(offs)) if offs[k] % 8 == r]
        if not ks:
            continue
        rows = tm + (8 if r else 0)
        if r:
            zbuf[...] = d_pad[pl.ds(8 - r, rows), :]
        for k in ks:
            d = zbuf[...] if r else d_pad[pl.ds(8, rows), :]
            prod = d * src[pl.ds(offs[k] - r, rows), :]
            g_ref[k] += jnp.sum(prod.reshape(rows // 8, 8, ch), axis=0)


_CONV_OFFS = [HALO - (CONV_K - 1) + k for k in range(CONV_K)]


def _conv_fwd(ext_u, dww_ref, dwb, tm, zbuf):
    return dwb + _taps(ext_u, lambda k: dww_ref[k:k + 1, :], _CONV_OFFS, tm, zbuf)


def _ln_silu(c, lng, lnb):
    mu = jnp.mean(c, axis=-1, keepdims=True)
    cc = c - mu
    rstd = lax.rsqrt(jnp.mean(cc * cc, axis=-1, keepdims=True) + EPS)
    z = cc * rstd
    l = z * lng + lnb
    sl = _sig(l)
    return z, rstd, l, sl, l * sl


def _rg_conv(ext_x, cw_ref, cb, tm):
    xc = cb + cw_ref[0:1, :] * ext_x[pl.ds(HALO - (RG_CONV_K - 1), tm), :]
    for k in range(1, RG_CONV_K):
        xc = xc + cw_ref[k:k + 1, :] * ext_x[pl.ds(HALO - (RG_CONV_K - 1) + k, tm), :]
    return xc


def _softplus_neg(lam):
    return jnp.maximum(-lam, 0.0) + jnp.log(1.0 + jnp.exp(-jnp.abs(lam)))


def _rg_gates(xc, wa, ba, wx, bx, lam):
    xcb = xc.astype(BF16)
    r = _sig(_dot(xcb, wa) + ba)
    ig = _sig(_dot(xcb, wx) + bx)
    log_a = (-RG_C * r) * _softplus_neg(lam)
    a = jnp.exp(log_a)
    return r, ig, a, jnp.sqrt(_one_minus_sq(a, log_a))


def _scan_rows(a_ref, b_ref, out_ref, carry, tm, reverse):
    rows = lax.broadcasted_iota(jnp.int32, (8, D_RNN), 0)
    ngrp = tm // 8

    def grp(gi, hb):
        st = pl.multiple_of((ngrp - 1 - gi if reverse else gi) * 8, 8)
        a8 = a_ref[pl.ds(st, 8), :]
        b8 = b_ref[pl.ds(st, 8), :]
        out = jnp.zeros((8, D_RNN), F32)
        for j in (range(7, -1, -1) if reverse else range(8)):
            aj = jnp.broadcast_to(a8[j:j + 1, :], (8, D_RNN))
            bj = jnp.broadcast_to(b8[j:j + 1, :], (8, D_RNN))
            if reverse:
                cur = bj + hb
                hb = aj * cur
            else:
                cur = aj * hb + bj
                hb = cur
            out = jnp.where(rows == j, cur, out)
        out_ref[pl.ds(st, 8), :] = out
        return hb

    carry[...] = lax.fori_loop(0, ngrp, grp, carry[...])


_MIX_W = ("wp", "psc", "dww", "dwb", "lng", "lnb", "wpw", "cw", "cb", "wa", "ba", "wx", "bx", "lam")


def _lead_tile(src_ref, dst, carry, i, nt, tm, lead, n_src):
    last = n_src - (nt - 1) * tm
    assert 0 < last and lead + last <= tm, (n_src, nt, tm)
    dst[0:lead, :] = carry[...]

    @pl.when(i < nt - 1)
    def _():
        dst[lead:tm, :] = src_ref[0:tm - lead, :]
        carry[...] = src_ref[tm - lead:tm, :]

    @pl.when(i == nt - 1)
    def _():
        dst[lead:lead + last, :] = src_ref[0:last, :]
        if lead + last < tm:
            dst[lead + last:tm, :] = jnp.zeros((tm - lead - last, dst.shape[1]), dst.dtype)


def _mixer_fwd(h, g, w_in, mw, lead=None, t_pad=None):
    t = h.shape[0] if lead is None else t_pad
    tm = _row_tile(t, TM_MIX)
    nt = t // tm
    n_lead = 0 if lead is None else lead.shape[0]

    def body(h_ref, g_ref, win_ref, wp, psc, dww, dwb, lng, lnb, wpw, cw, cb, wa, ba, wx, bx, lam, *rest):
        if n_lead:
            lead_ref, rest = rest[0], rest[1:]
            h_out, h_carry = rest[7], rest[-1]
            rest = rest[:7] + rest[8:-1]
        (y_ref, p_ref, u_ref, hs_ref, conv_ref, xc_ref, gates_ref,
         ext_q, ext_u, ext_x, tmp_a, tmp_b, zbuf, a_s, b_s, hcar) = rest
        i = pl.program_id(0)

        @pl.when(i == 0)
        def _():
            ext_q[0:HALO, :] = jnp.zeros((HALO, D_POOL), F32)
            ext_u[0:HALO, :] = jnp.zeros((HALO, D_CONV), F32)
            ext_x[0:HALO, :] = jnp.zeros((HALO, D_RNN), F32)
            hcar[...] = jnp.zeros((8, D_RNN), F32)
            if n_lead:
                h_carry[...] = lead_ref[...]

        if n_lead:
            _lead_tile(h_ref, h_out, h_carry, i, nt, tm, n_lead, h.shape[0])
            h_ref = h_out
        u = (_rms(h_ref[...])[1] * g_ref[...]).astype(BF16)
        u_ref[...] = u
        p_ref[...] = _dot_nt(u, win_ref[...])

        ext_q[pl.ds(HALO, tm), :] = p_ref[:, 0:256]
        pooled, _, _ = _pool_fwd(ext_q, tmp_a, tmp_b, tm, i * tm)
        y_ref[:, 0:256] = (_dot(pooled.astype(BF16), wp[...]) * psc[...]).astype(BF16)

        ext_u[pl.ds(HALO, tm), :] = p_ref[:, 256:512] * _sig(p_ref[:, 512:768])
        conv = _conv_fwd(ext_u, dww, dwb[...], tm, zbuf)
        conv_ref[...] = conv
        act = _ln_silu(conv, lng[...], lnb[...])[4]
        y_ref[:, 256:512] = _dot(act.astype(BF16), wpw[...]).astype(BF16)

        ext_x[pl.ds(HALO, tm), :] = p_ref[:, 1280:1792]
        xc = _rg_conv(ext_x, cw, cb[...], tm)
        xc_ref[...] = xc
        r, ig, a, m = _rg_gates(xc, wa[...], ba[...], wx[...], bx[...], lam[...])
        for j, gate in enumerate((r, ig, a, m)):
            gates_ref[:, j * D_RNN:(j + 1) * D_RNN] = gate
        a_s[...] = a
        b_s[...] = m * (ig * xc)
        _scan_rows(a_s, b_s, hs_ref, hcar, tm, reverse=False)
        y_ref[:, 512:1024] = (_gelu_and_grad(p_ref[:, 768:1280])[0] * hs_ref[...]).astype(BF16)

        ext_q[0:HALO, :] = ext_q[pl.ds(tm, HALO), :]
        ext_u[0:HALO, :] = ext_u[pl.ds(tm, HALO), :]
        ext_x[0:HALO, :] = ext_x[pl.ds(tm, HALO), :]

    ws = [mw[k] for k in _MIX_W]
    row = lambda w: pl.BlockSpec((tm, w), lambda i: (i, 0))
    extra = [] if lead is None else [lead]
    return pl.pallas_call(
        body, name="mixer_fwd", grid=(nt,),
        in_specs=[row(D_MODEL), _full(g), _resident(w_in)] + [_full(w) for w in ws] + [_full(a) for a in extra],
        out_specs=[row(D_MODEL), row(D_IN), row(D_MODEL), row(D_RNN), row(D_CONV), row(D_RNN), row(4 * D_RNN)]
        + [row(D_MODEL)] * len(extra),
        out_shape=[jax.ShapeDtypeStruct((t, D_MODEL), BF16), jax.ShapeDtypeStruct((t, D_IN), F32),
                   jax.ShapeDtypeStruct((t, D_MODEL), BF16), jax.ShapeDtypeStruct((t, D_RNN), F32),
                   jax.ShapeDtypeStruct((t, D_CONV), F32), jax.ShapeDtypeStruct((t, D_RNN), F32),
                   jax.ShapeDtypeStruct((t, 4 * D_RNN), F32)] + [jax.ShapeDtypeStruct((t, D_MODEL), F32)] * len(extra),
        scratch_shapes=[pltpu.VMEM((HALO + tm, D_POOL), F32), pltpu.VMEM((HALO + tm, D_CONV), F32),
                        pltpu.VMEM((HALO + tm, D_RNN), F32), pltpu.VMEM((HALO + tm, D_POOL), F32),
                        pltpu.VMEM((HALO + tm, D_POOL), F32), pltpu.VMEM((tm + 8, D_CONV), F32),
                        pltpu.VMEM((tm, D_RNN), F32), pltpu.VMEM((tm, D_RNN), F32), pltpu.VMEM((8, D_RNN), F32)]
        + [pltpu.VMEM(a.shape, F32) for a in extra],
        compiler_params=_params("arbitrary"),
    )(h, g, w_in, *ws, *extra)


_MIX_G = (("wp", (D_POOL, D_POOL)), ("psc", (1, D_POOL)), ("dww", (32, 8, D_CONV)), ("dwb", (1, D_CONV)),
          ("lng", (1, D_CONV)), ("lnb", (1, D_CONV)), ("wpw", (D_CONV, D_CONV)), ("cw", (8, D_RNN)),
          ("cb", (1, D_RNN)), ("wa", (D_RNN, D_RNN)), ("ba", (1, D_RNN)), ("wx", (D_RNN, D_RNN)),
          ("bx", (1, D_RNN)), ("lam", (1, D_RNN)), ("g1", (1, D_MODEL)))


def _mixer_bwd(p, dh1, hs, conv, xc, gates, h0, g1, w_out, w_in, mw, deps=(), lead=0, t_real=None):
    t = p.shape[0]
    tm = _row_tile(t, TM_MIX)
    nt = t // tm
    hb = tm // HALO
    t_out = t_real - lead if lead else t

    def body(p_ref, ph_ref, dh1_ref, hs_ref, hsh_ref, conv_ref, xc_ref, gates_ref, h0_ref, g1_ref, wout_ref, win_ref,
             wp, psc, dww, dwb, lng, lnb, wpw, cw, cb, wa, ba, wx, bx, lam,
             dp_ref, dh0_ref, g_wp, g_psc, g_dww, g_dwb, g_lng, g_lnb, g_wpw, g_cw, g_cb, g_wa, g_ba, g_wx, g_bx, g_lam, g_g1,
             *tail):
        dlead_ref, carry = (tail[0], tail[-1]) if lead else (None, None)
        (ext_q, ext_u, ext_x, ext_h, ee, dc_s, dx_s, tmp_a, tmp_b, zbuf, d_pad, a_s, b_s, g_s, gcar, dy_ref,
         dp_s) = tail[1:-1] if lead else tail[1:]
        step = pl.program_id(0)
        i = nt - 1 - step
        grads = (g_wp, g_psc, g_dww, g_dwb, g_lng, g_lnb, g_wpw, g_cw, g_cb, g_wa, g_ba, g_wx, g_bx, g_lam, g_g1)
        if lead:
            grads += (carry,)
        dy_ref[...] = dh1_ref[...].astype(BF16)
        dy_cols = lambda lo, hi: _dot_nt(dy_ref[...], wout_ref[lo:hi, :])

        @pl.when(step == 0)
        def _():
            for gr in grads:
                gr[...] = jnp.zeros(gr.shape, F32)
            ee[pl.ds(tm, HALO), :] = jnp.zeros((HALO, D_POOL), F32)
            dc_s[pl.ds(tm, HALO), :] = jnp.zeros((HALO, D_CONV), F32)
            dx_s[pl.ds(tm, HALO), :] = jnp.zeros((HALO, D_RNN), F32)
            d_pad[0:8, :] = jnp.zeros((8, D_CONV), F32)
            d_pad[pl.ds(tm + 8, 8), :] = jnp.zeros((8, D_CONV), F32)
            gcar[...] = jnp.zeros((8, D_RNN), F32)

        hm = jnp.where(i == 0, 0.0, 1.0)

        ext_q[0:HALO, :] = ph_ref[:, 0:256] * hm
        ext_q[pl.ds(HALO, tm), :] = p_ref[:, 0:256]
        pooled, lane, cnt = _pool_fwd(ext_q, tmp_a, tmp_b, tm, i * tm)
        pooled_b = pooled.astype(BF16)
        dya = dy_cols(0, 256)
        g_psc[...] += _colsum(dya * _dot(pooled_b, wp[...]))
        dmixed_b = (dya * psc[...]).astype(BF16)
        dpooled = _dot_nt(dmixed_b, wp[...])
        g_wp[...] += _dot_tn(pooled_b, dmixed_b)
        ee[0:tm, :] = dpooled / cnt
        dp_s[:, 0:256] = _lane_sel(lane, *_window_sums_ahead(ee, tmp_a, tmp_b, tm)) - dpooled
        ee[pl.ds(tm, HALO), :] = ee[0:HALO, :]

        v = p_ref[:, 256:512]
        s = _sig(p_ref[:, 512:768])
        ext_u[0:HALO, :] = ph_ref[:, 256:512] * _sig(ph_ref[:, 512:768]) * hm
        ext_u[pl.ds(HALO, tm), :] = v * s
        z, rstd, l, sl, act = _ln_silu(conv_ref[...], lng[...], lnb[...])
        dyb_b = dy_cols(256, 512).astype(BF16)
        dact = _dot_nt(dyb_b, wpw[...])
        g_wpw[...] += _dot_tn(act.astype(BF16), dyb_b)
        dl = dact * (sl * (1.0 + l * (1.0 - sl)))
        g_lng[...] += _colsum(dl * z)
        g_lnb[...] += _colsum(dl)
        dz = dl * lng[...]
        dc = rstd * (dz - jnp.mean(dz, axis=-1, keepdims=True) - z * jnp.mean(dz * z, axis=-1, keepdims=True))
        g_dwb[...] += _colsum(dc)
        dc_s[0:tm, :] = dc
        d_pad[pl.ds(8, tm), :] = dc
        _tap_grads(d_pad, ext_u, _CONV_OFFS, tm, g_dww, zbuf)
        du0 = _taps(dc_s, lambda j: dww[CONV_K - 1 - j:CONV_K - j, :], list(range(CONV_K)), tm, zbuf)
        dp_s[:, 256:512] = du0 * s
        dp_s[:, 512:768] = du0 * v * (s * (1.0 - s))
        dc_s[pl.ds(tm, HALO), :] = dc_s[0:HALO, :]

        ext_x[0:HALO, :] = ph_ref[:, 1280:1792] * hm
        ext_x[pl.ds(HALO, tm), :] = p_ref[:, 1280:1792]
        xc = xc_ref[...]
        xcb = xc.astype(BF16)
        r, ig, a, m = (gates_ref[:, j * D_RNN:(j + 1) * D_RNN] for j in range(4))
        sp = _softplus_neg(lam[...])
        ext_h[0:HALO, :] = hsh_ref[...] * hm
        ext_h[pl.ds(HALO, tm), :] = hs_ref[...]
        dyc = dy_cols(512, 1024)
        gl, dgl = _gelu_and_grad(p_ref[:, 768:1280])
        dp_s[:, 768:1280] = dyc * hs_ref[...] * dgl
        a_s[...] = a
        b_s[...] = dyc * gl
        _scan_rows(a_s, b_s, g_s, gcar, tm, reverse=True)
        g = g_s[...]
        da = g * ext_h[pl.ds(HALO - 1, tm), :]
        dm = g * (ig * xc)
        dig = g * (m * xc)
        dlog_a = da * a - dm * (a * a) / m
        g_lam[...] += _colsum(dlog_a * (-RG_C * r)) * (-_sig(-lam[...]))
        dra = (dlog_a * (-RG_C * sp)) * (r * (1.0 - r))
        dia = dig * (ig * (1.0 - ig))
        g_ba[...] += _colsum(dra)
        g_bx[...] += _colsum(dia)
        dra_b = dra.astype(BF16)
        dia_b = dia.astype(BF16)
        dxc = g * (m * ig) + _dot_nt(dra_b, wa[...]) + _dot_nt(dia_b, wx[...])
        g_wa[...] += _dot_tn(xcb, dra_b)
        g_wx[...] += _dot_tn(xcb, dia_b)
        g_cb[...] += _colsum(dxc)
        dx_s[0:tm, :] = dxc
        for k in range(RG_CONV_K):
            g_cw[k:k + 1, :] += _colsum(dxc * ext_x[pl.ds(HALO - (RG_CONV_K - 1) + k, tm), :])
        dxin = cw[RG_CONV_K - 1:RG_CONV_K, :] * dxc
        for j in range(1, RG_CONV_K):
            dxin = dxin + cw[RG_CONV_K - 1 - j:RG_CONV_K - j, :] * dx_s[pl.ds(j, tm), :]
        dp_s[:, 1280:1792] = dxin
        dx_s[pl.ds(tm, HALO), :] = dx_s[0:HALO, :]

        dpb = dp_s[...].astype(BF16)
        dp_ref[...] = dpb
        du = _dot(dpb, win_ref[...])
        r, n = _rms(h0_ref[...])
        g_g1[...] += _colsum(du * n)
        dh0 = dh1_ref[...] + _rms_bwd(du, n, r, g1_ref[...])
        if lead:
            dh0_ref[0:tm - lead, :] = dh0[lead:tm, :]
            dh0_ref[tm - lead:tm, :] = carry[...]
            carry[...] = dh0[0:lead, :]

            @pl.when(i == 0)
            def _():
                dlead_ref[...] = dh0[0:lead, :]
        else:
            dh0_ref[...] = dh0
            tail[0][...] = dh0.astype(BF16)

    ws = [mw[k] for k in _MIX_W]
    tile = lambda w: pl.BlockSpec((tm, w), lambda s: (nt - 1 - s, 0))
    halo = lambda w: pl.BlockSpec((HALO, w), lambda s: (jnp.maximum((nt - 1 - s) * hb - 1, 0), 0))
    lead_out = [pl.BlockSpec((lead, D_MODEL), lambda s: (0, 0))] if lead else [tile(D_MODEL)]
    outs = pl.pallas_call(
        _after(body, 12 + len(ws), deps), name="mixer_bwd", grid=(nt,),
        in_specs=[tile(D_IN), halo(D_IN), tile(D_MODEL), tile(D_RNN), halo(D_RNN), tile(D_CONV), tile(D_RNN), tile(4 * D_RNN),
                  tile(D_MODEL), _full(g1),
                  _resident(w_out), _resident(w_in)] + [_full(w) for w in ws] + [ANY] * len(deps),
        out_specs=[tile(D_IN), tile(D_MODEL)] + [pl.BlockSpec(shp, lambda s, nd=len(shp): (0,) * nd) for _, shp in _MIX_G]
        + lead_out,
        out_shape=[jax.ShapeDtypeStruct((t, D_IN), BF16), jax.ShapeDtypeStruct((t_out, D_MODEL), F32)]
        + [jax.ShapeDtypeStruct(shp, F32) for _, shp in _MIX_G]
        + [jax.ShapeDtypeStruct((lead, D_MODEL), F32) if lead else jax.ShapeDtypeStruct((t, D_MODEL), BF16)],
        scratch_shapes=[pltpu.VMEM((HALO + tm, D_POOL), F32), pltpu.VMEM((HALO + tm, D_CONV), F32),
                        pltpu.VMEM((HALO + tm, D_RNN), F32), pltpu.VMEM((HALO + tm, D_RNN), F32),
                        pltpu.VMEM((tm + HALO, D_POOL), F32), pltpu.VMEM((tm + HALO, D_CONV), F32),
                        pltpu.VMEM((tm + HALO, D_RNN), F32), pltpu.VMEM((HALO + tm, D_POOL), F32),
                        pltpu.VMEM((HALO + tm, D_POOL), F32), pltpu.VMEM((tm + 8, D_CONV), F32),
                        pltpu.VMEM((tm + 16, D_CONV), F32), pltpu.VMEM((tm, D_RNN), F32),
                        pltpu.VMEM((tm, D_RNN), F32), pltpu.VMEM((tm, D_RNN), F32), pltpu.VMEM((8, D_RNN), F32),
                        pltpu.VMEM((tm, D_MODEL), BF16), pltpu.VMEM((tm, D_IN), F32)]
        + [pltpu.VMEM((lead, D_MODEL), F32)] * bool(lead),
        compiler_params=_params("arbitrary"),
    )(p, p, dh1, hs, hs, conv, xc, gates, h0, g1, w_out, w_in, *ws, *deps)
    grads = {k: o for (k, _), o in zip(_MIX_G, outs[2:])}
    return outs[0], (outs[1], outs[-1]), grads


def _mid_fwd(y, h0, w_out, g, w_up):
    t = h0.shape[0]
    tm = _row_tile(t, TM_MAT)

    def body(y_ref, h0_ref, wo_ref, g_ref, wu_ref, h1_ref, u2_ref, f_ref):
        h1 = h0_ref[...] + _dot(y_ref[...], wo_ref[...])
        h1_ref[...] = h1
        u2 = (_rms(h1)[1] * g_ref[...]).astype(BF16)
        u2_ref[...] = u2
        for c in range(D_FF // FF_CHUNK):
            f_ref[:, c * FF_CHUNK:(c + 1) * FF_CHUNK] = _dot(u2, wu_ref[c]).astype(BF16)

    row = lambda w: pl.BlockSpec((tm, w), lambda i: (i, 0))
    return pl.pallas_call(
        body, name="mid_fwd", grid=(t // tm,),
        in_specs=[row(D_MODEL), row(D_MODEL), _resident(w_out), _full(g), _resident(w_up)],
        out_specs=[row(D_MODEL), row(D_MODEL), row(D_FF)],
        out_shape=[jax.ShapeDtypeStruct((t, D_MODEL), F32), jax.ShapeDtypeStruct((t, D_MODEL), BF16),
                   jax.ShapeDtypeStruct((t, D_FF), BF16)],
        compiler_params=_params("parallel"),
    )(y, h0, w_out, g, w_up)


def _down_proj(f_ref, h1_ref, wd_ref):
    acc = h1_ref[...]
    for c in range(D_FF // FF_CHUNK):
        cols = slice(c * FF_CHUNK, (c + 1) * FF_CHUNK)
        a = jnp.square(jnp.maximum(f_ref[:, cols].astype(F32), 0.0)).astype(BF16)
        acc = acc + _dot(a, wd_ref[cols, :])
    return acc


def _down_fwd(f, h1, w_down):
    t = h1.shape[0]
    tm = _row_tile(t, TM_MAT)

    def body(f_ref, h1_ref, wd_ref, h2_ref):
        h2_ref[...] = _down_proj(f_ref, h1_ref, wd_ref)

    row = lambda w: pl.BlockSpec((tm, w), lambda i: (i, 0))
    return pl.pallas_call(
        body, name="down_fwd", grid=(t // tm,),
        in_specs=[row(D_FF), row(D_MODEL), _resident(w_down)], out_specs=row(D_MODEL),
        out_shape=jax.ShapeDtypeStruct((t, D_MODEL), F32),
        compiler_params=_params("parallel"),
    )(f, h1, w_down)


def _down_fwd_loss(f, h1, w_down, g, tgt, t_real):
    t = h1.shape[0]
    tm = _row_tile(t, TM_MAT)
    nt = t // tm

    def body(f_ref, h1_ref, wd_ref, g_ref, tgt_in, loss_ref, dh_ref, dg_ref, dhb_ref, tgt_ref, carry):
        i = pl.program_id(0)

        @pl.when(i == 0)
        def _():
            loss_ref[...] = jnp.zeros(loss_ref.shape, F32)
            dg_ref[...] = jnp.zeros(dg_ref.shape, F32)
            carry[...] = jnp.zeros(carry.shape, F32)

        _lead_tile(tgt_in, tgt_ref, carry, i, nt, tm, N_META, tgt.shape[0])

        r, n = _rms(_down_proj(f_ref, h1_ref, wd_ref))
        row = lax.broadcasted_iota(jnp.int32, (tm, 1), 0) + i * tm
        valid = jnp.logical_and(row >= N_META, row < t_real)
        diff = jnp.where(valid, n * g_ref[...] - tgt_ref[...], 0.0)
        loss_ref[...] += 0.5 * jnp.sum(jnp.mean(diff * diff, axis=-1, keepdims=True))
        dy = diff * (1.0 / D_MODEL)
        dg_ref[...] += _colsum(dy * n)
        dh = _rms_bwd(dy, n, r, g_ref[...])
        dh_ref[...] = dh
        dhb_ref[...] = dh.astype(BF16)

    row = lambda w: pl.BlockSpec((tm, w), lambda i: (i, 0))
    return pl.pallas_call(
        body, name="down_fwd_loss", grid=(t // tm,),
        in_specs=[row(D_FF), row(D_MODEL), _resident(w_down), _full(g), row(D_MODEL)],
        out_specs=[pl.BlockSpec((8, 128), lambda i: (0, 0)), row(D_MODEL), pl.BlockSpec((1, D_MODEL), lambda i: (0, 0)),
                   row(D_MODEL)],
        out_shape=[jax.ShapeDtypeStruct((8, 128), F32), jax.ShapeDtypeStruct((t, D_MODEL), F32),
                   jax.ShapeDtypeStruct((1, D_MODEL), F32), jax.ShapeDtypeStruct((t, D_MODEL), BF16)],
        scratch_shapes=[pltpu.VMEM((tm, D_MODEL), F32), pltpu.VMEM((N_META, D_MODEL), F32)],
        compiler_params=_params("arbitrary"),
    )(f, h1, w_down, g, tgt)


def _mlp_bwd(dh2, f, h1, g, w_up, w_down, deps=()):
    t = dh2.shape[0]
    tm = _row_tile(t, TM_MLP_BWD)

    def body(dh2_ref, f_ref, wd_ref, wu_ref, h1_ref, g_ref, df_ref, dh1_ref, dg_ref, dh1b_ref):
        @pl.when(pl.program_id(0) == 0)
        def _():
            dg_ref[...] = jnp.zeros(dg_ref.shape, F32)

        dh2 = dh2_ref[...]
        dhb = dh2.astype(BF16)
        du2 = None
        for c in range(D_FF // FF_CHUNK):
            cols = slice(c * FF_CHUNK, (c + 1) * FF_CHUNK)
            dact = _dot_nt(dhb, wd_ref[c])
            df = (dact * (2.0 * jnp.maximum(f_ref[:, cols].astype(F32), 0.0))).astype(BF16)
            df_ref[:, cols] = df
            part = _dot_nt(df, wu_ref[c])
            du2 = part if du2 is None else du2 + part
        r, n = _rms(h1_ref[...])
        dg_ref[...] += _colsum(du2 * n)
        dh1 = dh2 + _rms_bwd(du2, n, r, g_ref[...])
        dh1_ref[...] = dh1
        dh1b_ref[...] = dh1.astype(BF16)

    row = lambda w: pl.BlockSpec((tm, w), lambda i: (i, 0))
    return pl.pallas_call(
        _after(body, 6, deps), name="mlp_bwd", grid=(t // tm,),
        in_specs=[row(D_MODEL), row(D_FF), _resident(w_down), _resident(w_up), row(D_MODEL), _full(g)] + [ANY] * len(deps),
        out_specs=[row(D_FF), row(D_MODEL), pl.BlockSpec((1, D_MODEL), lambda i: (0, 0)), row(D_MODEL)],
        out_shape=[jax.ShapeDtypeStruct((t, D_FF), BF16), jax.ShapeDtypeStruct((t, D_MODEL), F32),
                   jax.ShapeDtypeStruct((1, D_MODEL), F32), jax.ShapeDtypeStruct((t, D_MODEL), BF16)],
        compiler_params=_params("arbitrary"),
    )(dh2, f, w_down, w_up, h1, g, *deps)


def _tn_matmul(a, b, kc, nc, relu2, name, deps=()):
    t, k = a.shape
    n = b.shape[1]
    tt = _row_tile(t, TM_TN)
    gk, gn = k // kc, n // nc

    def body(a_ref, b_ref, o_ref):
        @pl.when(pl.program_id(2) == 0)
        def _():
            o_ref[...] = jnp.zeros(o_ref.shape, F32)

        av = a_ref[...]
        if relu2:
            av = jnp.square(jnp.maximum(av.astype(F32), 0.0))
        o_ref[...] += _dot_tn(av.astype(BF16), b_ref[...].astype(BF16))

    return pl.pallas_call(
        _after(body, 2, deps), name=name, grid=(gk, gn, t // tt),
        in_specs=[pl.BlockSpec((tt, kc), lambda ik, jn, it: (it, ik)), pl.BlockSpec((tt, nc), lambda ik, jn, it: (it, jn))]
        + [ANY] * len(deps),
        out_specs=pl.BlockSpec((None, kc, nc), lambda ik, jn, it: (ik * gn + jn, 0, 0)),
        out_shape=jax.ShapeDtypeStruct((gk * gn, kc, nc), F32),
        compiler_params=_params("parallel", "parallel", "arbitrary"),
    )(a, b, *deps)


def _block_diag(blocks):
    nb, hd, _ = blocks.shape
    eye = jnp.eye(nb, dtype=blocks.dtype)
    return (blocks[:, :, None, :] * eye[:, None, :, None]).reshape(nb * hd, nb * hd)


def _diag_blocks(m, nb):
    hd = m.shape[0] // nb
    eye = jnp.eye(nb, dtype=m.dtype)
    return jnp.sum(m.reshape(nb, hd, nb, hd) * eye[:, None, :, None], axis=2)


def _mixer_weights(w, l):
    row = lambda a: a.reshape(1, -1)
    return dict(
        wp=_block_diag(w["pool_w"][l]).astype(BF16), psc=row(w["pool_scale"][l]),
        dww=jnp.pad(w["convb_dw_w"][l], ((0, 32 - CONV_K), (0, 0))), dwb=row(w["convb_dw_b"][l]),
        lng=row(w["convb_ln_g"][l]), lnb=row(w["convb_ln_b"][l]), wpw=w["convb_pw_w"][l].astype(BF16),
        cw=jnp.pad(w["rg_conv_w"][l], ((0, 8 - RG_CONV_K), (0, 0))), cb=row(w["rg_conv_b"][l]),
        wa=_block_diag(w["rg_w_a"][l]).astype(BF16), ba=row(w["rg_b_a"][l]),
        wx=_block_diag(w["rg_w_x"][l]).astype(BF16), bx=row(w["rg_b_x"][l]), lam=row(w["rg_lambda"][l]))


def _local_step(h, tgt, t_real, t_pad, w, fetch, hooks):
    depth = 2
    saved = []
    big = []
    for l in range(depth):
        mw = _mixer_weights(w, l)
        g1 = w["mix_norm_g"][l].reshape(1, -1)
        g2 = w["mlp_norm_g"][l].reshape(1, -1)
        wl = dict(w_in=fetch(l, "w_in", h))
        if l == 0:
            y, p, u, hs, conv, xc, gates, h = _mixer_fwd(h, g1, wl["w_in"], mw, lead=w["meta_tokens"], t_pad=t_pad)
        else:
            y, p, u, hs, conv, xc, gates = _mixer_fwd(h, g1, wl["w_in"], mw)
        wl["w_out"], wl["w_up"] = fetch(l, "w_out", y), fetch(l, "w_up", y)
        h1, u2, f = _mid_fwd(y, h, wl["w_out"], g2, wl["w_up"])
        wl["w_down"] = fetch(l, "w_down", f)
        if l == depth - 1:
            loss, dh, dgf, dh_b = _down_fwd_loss(f, h1, wl["w_down"].reshape(D_FF, D_MODEL),
                                                 w["final_norm_g"].reshape(1, -1), tgt, t_real)
            h2 = None
        else:
            h2 = _down_fwd(f, h1, wl["w_down"].reshape(D_FF, D_MODEL))
        saved.append(dict(mw=mw, g1=g1, g2=g2, h0=h, p=p, u=u, y=y, hs=hs, conv=conv, xc=xc, gates=gates, h1=h1, u2=u2, f=f))
        big.append(wl)
        h = h2

    gs = {k: [None] * depth for k in ("mix_norm_g", "mlp_norm_g", "pool_w", "pool_scale", "convb_dw_w", "convb_dw_b",
                                      "convb_ln_g", "convb_ln_b", "convb_pw_w", "rg_conv_w", "rg_conv_b", "rg_w_a",
                                      "rg_b_a", "rg_w_x", "rg_b_x", "rg_lambda")}
    deps = ()
    for l in reversed(range(depth)):
        s, wl = saved[l], big[l]
        df, dh1, dg2, dh1_b = _mlp_bwd(dh, s["f"], s["h1"], s["g2"], wl["w_up"], wl["w_down"], deps)
        deps = hooks.point(l, "mlp_bwd", dh1)
        g_down = _tn_matmul(s["f"], dh_b, FF_CHUNK, D_MODEL, True, "dw_down", deps)
        hooks.grad(l, "w_down", g_down)
        deps = hooks.point(l, "dw_down", g_down)
        g_up = _tn_matmul(s["u2"], df, D_MODEL, FF_CHUNK, False, "dw_up", deps)
        hooks.grad(l, "w_up", g_up)
        deps = hooks.point(l, "dw_up", g_up)
        g_out = _tn_matmul(s["y"], dh1_b, D_MODEL, D_MODEL, False, "dw_out", deps)
        hooks.grad(l, "w_out", g_out.reshape(N_CHIPS, D_MODEL // N_CHIPS, D_MODEL))
        deps = hooks.point(l, "dw_out", g_out)
        dp, dh, mg = _mixer_bwd(s["p"], dh1, s["hs"], s["conv"], s["xc"], s["gates"], s["h0"], s["g1"], wl["w_out"],
                                wl["w_in"], s["mw"], deps, lead=0 if l else N_META, t_real=t_real)
        dh, dh_b = dh
        if l == 0:
            dmeta = dh_b
        gs["mix_norm_g"][l] = mg["g1"][0]
        gs["mlp_norm_g"][l] = dg2[0]
        gs["pool_w"][l] = _diag_blocks(mg["wp"], D_POOL // POOL_GW)
        gs["pool_scale"][l] = mg["psc"][0]
        gs["convb_dw_w"][l] = jnp.sum(mg["dww"][:CONV_K], axis=1)
        gs["convb_dw_b"][l] = mg["dwb"][0]
        gs["convb_ln_g"][l] = mg["lng"][0]
        gs["convb_ln_b"][l] = mg["lnb"][0]
        gs["convb_pw_w"][l] = mg["wpw"]
        gs["rg_conv_w"][l] = mg["cw"][:RG_CONV_K]
        gs["rg_conv_b"][l] = mg["cb"][0]
        gs["rg_w_a"][l] = _diag_blocks(mg["wa"], D_RNN // RG_HD)
        gs["rg_b_a"][l] = mg["ba"][0]
        gs["rg_w_x"][l] = _diag_blocks(mg["wx"], D_RNN // RG_HD)
        gs["rg_b_x"][l] = mg["bx"][0]
        gs["rg_lambda"][l] = mg["lam"][0]
        if l == 0:
            gsmall = {k: jnp.stack(v) for k, v in gs.items()}
            gsmall["final_norm_g"] = dgf[0]
            gsmall["meta_tokens"] = dmeta
            started = hooks.small(gsmall)
        deps = hooks.point(l, "mixer_bwd", started[0] if l == 0 and started else dh)
        g_in = _tn_matmul(dp, s["u"], D_IN, D_MODEL, False, "dw_in", deps).reshape(N_CHIPS, D_IN // N_CHIPS, D_MODEL)
        hooks.grad(l, "w_in", g_in)
        deps = hooks.point(l, "dw_in", g_in)
    return loss[0, 0], dh


def _place():
    return lax.axis_index("x"), lax.axis_index("y"), lax.axis_index("c")


def _other_chips(x, y):
    return [(1 - x, y), (x, 1 - y), (1 - x, 1 - y)]


HBM_SPEC = pl.BlockSpec(memory_space=pltpu.HBM)
SEM_SPEC = pl.BlockSpec(memory_space=pltpu.SEMAPHORE)
DATAFLOW = pltpu.SideEffectType.DATAFLOW_SIDE_EFFECTING


def _gather_copies(src_refs, land_refs, send_sem, recv_sem, first):
    x, y, c = _place()
    me = 2 * x + y
    out = []
    for n in range(len(src_refs)):
        for j, (px, py) in enumerate(_other_chips(x, y) + [(x, y)]):
            k = first + N_CHIPS * n + j
            out.append(pltpu.make_async_remote_copy(src_refs[n], land_refs[n].at[me], send_sem.at[k], recv_sem.at[k],
                                                    device_id=(px, py, c), device_id_type=MESH))
    return out


def _gather_start(groups, name):
    srcs = [pltpu.with_memory_space_constraint(s, pltpu.HBM) for g in groups for s in g]
    lands = [pltpu.with_memory_space_constraint(lax.empty((N_CHIPS,) + s.shape, s.dtype), pltpu.HBM) for g in groups for s in g]
    n, ng = len(srcs), len(groups)
    first = [sum(len(g) for g in groups[:i]) for i in range(ng)]

    def body(*refs):
        src_refs, land_refs = refs[:n], refs[n:2 * n]
        sems = refs[2 * n:2 * n + 2 * ng]
        token = refs[-1]
        for gi, g in enumerate(groups):
            lo, hi = first[gi], first[gi] + len(g)
            for cp in _gather_copies(src_refs[lo:hi], land_refs[lo:hi], sems[2 * gi], sems[2 * gi + 1], 0):
                cp.start()
        token[...] = jnp.zeros(token.shape, token.dtype)

    sem_shapes = [pltpu.SemaphoreType.DMA((N_CHIPS * len(g),)) for g in groups for _ in range(2)]
    outs = pl.pallas_call(
        body, name=name,
        out_shape=sem_shapes + [pltpu.HBM(a.shape, a.dtype) for a in srcs + lands] + [jax.ShapeDtypeStruct((8, 128), F32)],
        in_specs=[HBM_SPEC] * (2 * n),
        out_specs=[SEM_SPEC] * (2 * ng) + [HBM_SPEC] * (2 * n) + [pl.BlockSpec(memory_space=pltpu.VMEM)],
        input_output_aliases={i: 2 * ng + i for i in range(2 * n)},
        compiler_params=pltpu.CompilerParams(has_side_effects=DATAFLOW),
    )(*srcs, *lands)
    sems, thru, token = outs[:2 * ng], outs[2 * ng:2 * ng + 2 * n], outs[-1]
    state = []
    for gi, g in enumerate(groups):
        lo, hi = first[gi], first[gi] + len(g)
        state.append((sems[2 * gi], sems[2 * gi + 1], thru[lo:hi], thru[n + lo:n + hi]))
    return state, token


def _gather_wait(state, after, name):
    send_sem, recv_sem, srcs, lands = state
    n = len(srcs)

    def body(*refs):
        src_refs, land_refs = refs[:n], refs[n:2 * n]
        send, recv = refs[2 * n], refs[2 * n + 1]
        for cp in _gather_copies(src_refs, land_refs, send, recv, 0):
            cp.wait_send()
            cp.wait_recv()

    outs = pl.pallas_call(
        body, name=name,
        out_shape=[pltpu.HBM(a.shape, a.dtype) for a in list(srcs) + list(lands)],
        in_specs=[HBM_SPEC] * (2 * n) + [SEM_SPEC, SEM_SPEC, ANY],
        out_specs=[HBM_SPEC] * (2 * n),
        input_output_aliases={i: i for i in range(2 * n)},
        compiler_params=pltpu.CompilerParams(has_side_effects=DATAFLOW),
    )(*srcs, *lands, send_sem, recv_sem, after)
    return outs[n:]


def _add_halves(g, recv, c1):
    nk, r, cd = g.shape
    r2 = r // 2
    rc = r2 // ROW_CHUNKS

    def body(c_ref, g_ref, r_ref, pab_ref):
        pab_ref[...] = (g_ref[...] + r_ref[...]).astype(BF16)

    blk = pl.BlockSpec((None, rc, cd), lambda k, j, c_ref: (k, j, 0))
    return pl.pallas_call(
        body, name="rs_add_halves",
        grid_spec=pltpu.PrefetchScalarGridSpec(
            num_scalar_prefetch=1, grid=(nk, ROW_CHUNKS),
            in_specs=[pl.BlockSpec((None, rc, cd), lambda k, j, c_ref: (k, c_ref[0] * ROW_CHUNKS + j, 0)), blk], out_specs=blk),
        out_shape=jax.ShapeDtypeStruct((nk, r2, cd), BF16),
        compiler_params=_params("parallel", "parallel"),
    )(c1, g, recv)


def _sum_partials(g, recv_sibling, recv_chips, c_me):
    nk, r, cd = g.shape
    r2 = r // 2
    rc = r2 // ROW_CHUNKS

    def body(cm_ref, g_ref, a_ref, r_ref, s_ref):
        own = g_ref[...] + a_ref[...]
        s_ref[...] = ((own + r_ref[0].astype(F32)) + r_ref[1].astype(F32)) + r_ref[2].astype(F32)

    return pl.pallas_call(
        body, name="rs_sum_partials",
        grid_spec=pltpu.PrefetchScalarGridSpec(
            num_scalar_prefetch=1, grid=(ROW_CHUNKS,),
            in_specs=[pl.BlockSpec((None, rc, cd), lambda j, cm: (cm[1], cm[0] * ROW_CHUNKS + j, 0)),
                      pl.BlockSpec((None, rc, cd), lambda j, cm: (cm[1], j, 0)),
                      pl.BlockSpec((3, rc, cd), lambda j, cm: (0, j, 0))],
            out_specs=pl.BlockSpec((rc, cd), lambda j, cm: (j, 0))),
        out_shape=jax.ShapeDtypeStruct((r2, cd), F32),
        compiler_params=_params("parallel"),
    )(c_me, g, recv_sibling, recv_chips)


def _split_start(name, srcs, lands, ncopies, make_copies):
    srcs = [pltpu.with_memory_space_constraint(s, pltpu.HBM) for s in srcs]
    lands = [pltpu.with_memory_space_constraint(a, pltpu.HBM) for a in lands]
    n, m = len(srcs), len(lands)

    def body(*refs):
        src_refs, land_refs = refs[:n], refs[n:n + m]
        send, recv, token = refs[n + m], refs[n + m + 1], refs[-1]
        for cp in make_copies(src_refs, land_refs, send, recv):
            cp.start()
        token[...] = jnp.zeros(token.shape, token.dtype)

    outs = pl.pallas_call(
        body, name=name,
        out_shape=[pltpu.SemaphoreType.DMA((ncopies,)), pltpu.SemaphoreType.DMA((ncopies,))]
        + [pltpu.HBM(a.shape, a.dtype) for a in srcs + lands] + [jax.ShapeDtypeStruct((8, 128), F32)],
        in_specs=[HBM_SPEC] * (n + m),
        out_specs=[SEM_SPEC, SEM_SPEC] + [HBM_SPEC] * (n + m) + [pl.BlockSpec(memory_space=pltpu.VMEM)],
        input_output_aliases={i: 2 + i for i in range(n + m)},
        compiler_params=pltpu.CompilerParams(has_side_effects=DATAFLOW),
    )(*srcs, *lands)
    return (outs[0], outs[1], outs[2:2 + n], outs[2 + n:2 + n + m], make_copies), outs[-1]


def _split_wait(name, state, after):
    send_sem, recv_sem, srcs, lands, make_copies = state
    n, m = len(srcs), len(lands)

    def body(*refs):
        src_refs, land_refs = refs[:n], refs[n:n + m]
        for cp in make_copies(src_refs, land_refs, refs[n + m], refs[n + m + 1]):
            cp.wait_send()
            cp.wait_recv()

    outs = pl.pallas_call(
        body, name=name,
        out_shape=[pltpu.HBM(a.shape, a.dtype) for a in list(srcs) + list(lands)],
        in_specs=[HBM_SPEC] * (n + m) + [SEM_SPEC, SEM_SPEC, ANY],
        out_specs=[HBM_SPEC] * (n + m),
        input_output_aliases={i: i for i in range(n + m)},
        compiler_params=pltpu.CompilerParams(has_side_effects=DATAFLOW),
    )(*srcs, *lands, send_sem, recv_sem, after)
    return outs[:n], outs[n:]


def _copies_to_sibling(src_of):
    def make(src_refs, land_refs, send, recv):
        x, y, c = _place()
        return [pltpu.make_async_remote_copy(src_of(src_refs[i], c), land_refs[i], send.at[i], recv.at[i],
                                             device_id=(x, y, 1 - c), device_id_type=MESH) for i in range(len(src_refs))]
    return make


def _copies_to_chips(src_refs, land_refs, send, recv):
    x, y, c = _place()
    return [pltpu.make_async_remote_copy(src_refs[i].at[2 * px + py], land_refs[i].at[j], send.at[3 * i + j], recv.at[3 * i + j],
                                         device_id=(px, py, c), device_id_type=MESH)
            for i in range(len(src_refs)) for j, (px, py) in enumerate(_other_chips(x, y))]


def _other_half_rows(ref, c):
    r2 = ref.shape[1] // 2
    return ref.at[:, pl.ds(pl.multiple_of((1 - c) * r2, 8), r2)]


class _ReduceScatter:
    def __init__(self, tag, grads, c1, me1):
        self.tag, self.grads, self.c1, self.me1 = tag, grads, c1, me1

    def start(self):
        lands = [lax.empty((g.shape[0], g.shape[1] // 2, g.shape[2]), F32) for g in self.grads]
        self.state, token = _split_start("rs_%s_a_start" % self.tag, self.grads, lands, len(self.grads),
                                         _copies_to_sibling(_other_half_rows))
        return token

    def to_chips(self, after):
        self.halves = _split_wait("rs_%s_a_wait" % self.tag, self.state, after)
        pabs = [_add_halves(g, r, self.c1) for g, r in zip(*self.halves)]
        lands = [lax.empty((3,) + p.shape[1:], BF16) for p in pabs]
        self.state, token = _split_start("rs_%s_b_start" % self.tag, pabs, lands, 3 * len(pabs), _copies_to_chips)
        return token

    def to_sibling(self, after):
        _, recv = _split_wait("rs_%s_b_wait" % self.tag, self.state, after)
        c_me = jnp.concatenate([self.c1, self.me1])
        sums = [_sum_partials(g, ra, rb, c_me) for g, ra, rb in zip(*self.halves, recv)]
        lands = [lax.empty(s.shape, F32) for s in sums]
        self.state, token = _split_start("rs_%s_c_start" % self.tag, sums, lands, len(sums),
                                         _copies_to_sibling(lambda ref, c: ref))
        return token

    def finish(self, after):
        return list(zip(*_split_wait("rs_%s_c_wait" % self.tag, self.state, after)))


def _add_lists(a_list, b_list):
    n = len(a_list)

    def body(*refs):
        for i in range(n):
            refs[2 * n + i][...] = refs[i][...] + refs[n + i][...]

    vm = pl.BlockSpec(memory_space=pltpu.VMEM)
    return pl.pallas_call(
        body, name="add_lists", in_specs=[vm] * (2 * n), out_specs=[vm] * n,
        out_shape=[jax.ShapeDtypeStruct(a.shape, a.dtype) for a in a_list],
        compiler_params=pltpu.CompilerParams(vmem_limit_bytes=VMEM_LIMIT),
    )(*a_list, *b_list)


def _copies_to_peer(stage):
    def make(src_refs, land_refs, send, recv):
        x, y, c = _place()
        peer = [(x, y, 1 - c), (1 - x, y, c), (x, 1 - y, c)][stage]
        return [pltpu.make_async_remote_copy(src_refs[i], land_refs[i], send.at[i], recv.at[i], device_id=peer, device_id_type=MESH)
                for i in range(len(src_refs))]
    return make


class _AllReduceSmall:
    def __init__(self, vs):
        self.vs, self.stage = list(vs), 0

    def _start(self):
        lands = [lax.empty(v.shape, v.dtype) for v in self.vs]
        self.state, token = _split_start("ar_small_start_%d" % self.stage, self.vs, lands, len(self.vs), _copies_to_peer(self.stage))
        return token

    def start(self):
        return self._start()

    def step(self, after):
        mine, theirs = _split_wait("ar_small_wait_%d" % self.stage, self.state, after)
        self.vs = _add_lists(mine, theirs)
        self.stage += 1
        return self._start() if self.stage < 3 else self.vs[0]


def _adamw_math(w, g, m, v):
    m = ADAM_B1 * m + (1.0 - ADAM_B1) * g
    v = ADAM_B2 * v + (1.0 - ADAM_B2) * jnp.square(g)
    m_hat = m / (1.0 - ADAM_B1 ** ADAM_STEP)
    v_hat = v / (1.0 - ADAM_B2 ** ADAM_STEP)
    return -ADAM_LR * (m_hat / (jnp.sqrt(v_hat) + ADAM_EPS) + ADAM_WD * w), m, v


def _adamw_big_layer(layer, w, m, v, own, sib, c1, prev):
    _, r, cd = w.shape
    rc = r // 2 // ROW_CHUNKS

    def body(c_ref, w_ref, m_ref, v_ref, own_ref, sib_ref, *rest):
        g_ref, d_ref, mo_ref, vo_ref, token = rest[-5:]
        g = jnp.where(pl.program_id(0) == c_ref[0], own_ref[...], sib_ref[...])
        g_ref[...] = g
        d_ref[...], mo_ref[...], vo_ref[...] = _adamw_math(w_ref[...], g, m_ref[...], v_ref[...])
        token[...] = jnp.zeros(token.shape, F32)

    blk = pl.BlockSpec((None, rc, cd), lambda hh, j, c_ref: (layer, hh * ROW_CHUNKS + j, 0))
    half = pl.BlockSpec((rc, cd), lambda hh, j, c_ref: (j, 0))
    prev = () if prev is None else tuple(prev)
    outs = pl.pallas_call(
        body, name="adamw_big",
        grid_spec=pltpu.PrefetchScalarGridSpec(
            num_scalar_prefetch=1, grid=(2, ROW_CHUNKS), in_specs=[blk, blk, blk, half, half] + [ANY] * len(prev),
            out_specs=[blk] * 4 + [pl.BlockSpec((8, 128), lambda hh, j, c_ref: (0, 0))]),
        out_shape=[jax.ShapeDtypeStruct(w.shape, F32)] * 4 + [jax.ShapeDtypeStruct((8, 128), F32)],
        input_output_aliases={6 + i: i for i in range(len(prev))},
        compiler_params=_params("arbitrary", "arbitrary"),
    )(c1, w, m, v, own, sib, *prev)
    return outs[:4], outs[4]


def _adamw_small(ws, gs, ms, vs):
    n = len(ws)

    def body(*refs):
        w_refs, g_refs, m_refs, v_refs = refs[:n], refs[n:2 * n], refs[2 * n:3 * n], refs[3 * n:4 * n]
        outs = refs[4 * n:]
        for i in range(n):
            outs[3 * i][...], outs[3 * i + 1][...], outs[3 * i + 2][...] = _adamw_math(
                w_refs[i][...], g_refs[i][...], m_refs[i][...], v_refs[i][...])

    vm = pl.BlockSpec(memory_space=pltpu.VMEM)
    outs = pl.pallas_call(
        body, name="adamw_small", in_specs=[vm] * (4 * n), out_specs=[vm] * (3 * n),
        out_shape=[jax.ShapeDtypeStruct(w.shape, F32) for w in ws for _ in range(3)],
        compiler_params=pltpu.CompilerParams(vmem_limit_bytes=VMEM_LIMIT),
    )(*ws, *gs, *ms, *vs)
    return [outs[3 * i:3 * i + 3] for i in range(n)]


LANES = 128
SUBLANES = 8
SHARDED_AXIS = {"meta_tokens": 1, "convb_dw_w": 2, "convb_pw_w": 1, "rg_conv_w": 2}


def _rows_of(size):
    return -(-size // (LANES * SUBLANES)) * SUBLANES


def _as_rows(a, rows=None):
    flat = a.reshape(-1)
    rows = _rows_of(flat.size) if rows is None else rows
    return jnp.pad(flat, (0, rows * LANES - flat.size)).reshape(rows, LANES)


class _GradientSchedule:
    GROUPS = {"l1": [(1, "w_down"), (1, "w_up"), (1, "w_out"), (1, "w_in")], "a0": [(0, "w_down"), (0, "w_up")],
              "b0": [(0, "w_out")], "c0": [(0, "w_in")]}
    PLAN = {
        (1, "dw_in"): [("l1", "start")],
        (0, "mlp_bwd"): [("l1", "to_chips")],
        (0, "dw_up"): [("l1", "to_sibling"), ("a0", "start")],
        (0, "dw_out"): [("l1", "finish"), ("a0", "to_chips"), ("b0", "start")],
        (0, "mixer_bwd"): [("a0", "to_sibling"), ("b0", "to_chips"), ("small", "step")],
        (0, "dw_in"): [("c0", "start"), ("small", "step"), ("c0", "to_chips"), ("a0", "finish"), ("b0", "to_sibling")],
    }

    def __init__(self, w, mom, var, c1, me1):
        self.w, self.mom, self.var, self.c1, self.me1 = w, mom, var, c1, me1
        self.grads, self.chains, self.out = {}, {}, {}

    def grad(self, layer, name, g):
        self.grads[layer, name] = g

    def small(self, gsmall):
        self.small_sum = _AllReduceSmall([g.reshape(1, -1) if g.ndim == 1 else g for g in (gsmall[k] for k in SMALL)])
        return (self.small_sum.start(),)

    def point(self, layer, kernel_name, after):
        return self.run(self.PLAN.get((layer, kernel_name), ()), after) or (after,)

    def run(self, actions, after):
        deps = []
        for tag, stage in actions:
            if tag == "small":
                deps.append(self.small_sum.step(after))
            elif stage == "start":
                self.chains[tag] = _ReduceScatter(tag, [self.grads[lk] for lk in self.GROUPS[tag]], self.c1, self.me1)
                deps.append(self.chains[tag].start())
            elif stage == "finish":
                for (layer, k), (own, sib) in zip(self.GROUPS[tag], self.chains[tag].finish(after)):
                    self.out[k], token = _adamw_big_layer(layer, self.w[k], self.mom[k], self.var[k], own, sib, self.c1,
                                                          self.out.get(k))
                    deps.append(token)
            else:
                deps.append(getattr(self.chains[tag], stage)(after))
            after = deps[-1]
        self.last = after
        return tuple(deps)


def _from_shard_major(name, sm):
    if name == "meta_tokens":
        return sm.transpose(1, 0, 2).reshape(N_META, -1)
    if name == "convb_pw_w":
        return sm.transpose(1, 0, 2, 3).reshape(2, -1, D_CONV)
    return sm.transpose(1, 2, 0, 3).reshape(sm.shape[1], sm.shape[2], -1)


def kernel(x, meta_tokens, mix_norm_g, w_in, pool_w, pool_scale, convb_dw_w, convb_dw_b, convb_ln_g, convb_ln_b, convb_pw_w, rg_conv_w, rg_conv_b, rg_w_a, rg_b_a, rg_w_x, rg_b_x, rg_lambda, w_out, mlp_norm_g, w_up, w_down, final_norm_g, loss_target, m_meta_tokens, m_mix_norm_g, m_w_in, m_pool_w, m_pool_scale, m_convb_dw_w, m_convb_dw_b, m_convb_ln_g, m_convb_ln_b, m_convb_pw_w, m_rg_conv_w, m_rg_conv_b, m_rg_w_a, m_rg_b_a, m_rg_w_x, m_rg_b_x, m_rg_lambda, m_w_out, m_mlp_norm_g, m_w_up, m_w_down, m_final_norm_g, v_meta_tokens, v_mix_norm_g, v_w_in, v_pool_w, v_pool_scale, v_convb_dw_w, v_convb_dw_b, v_convb_ln_g, v_convb_ln_b, v_convb_pw_w, v_rg_conv_w, v_rg_conv_b, v_rg_w_a, v_rg_b_a, v_rg_w_x, v_rg_b_x, v_rg_lambda, v_w_out, v_mlp_norm_g, v_w_up, v_w_down, v_final_norm_g):
    given = dict(locals())
    w = {k: given[k] for k in WEIGHTS}
    mom = {k: given["m_" + k] for k in WEIGHTS}
    var = {k: given["v_" + k] for k in WEIGHTS}
    xi, yi, ci = _place()
    me1 = (2 * xi + yi).astype(jnp.int32).reshape(1)
    c1 = ci.astype(jnp.int32).reshape(1)

    small_rows = [_rows_of(w[k].size) for k in SMALL_SHARDED]
    small_pack = jnp.concatenate([_as_rows(w[k]) for k in SMALL_SHARDED])
    transposed = lambda d: {**d, "w_in": d["w_in"].transpose(0, 2, 1)}
    wt, momt, vart = transposed(w), transposed(mom), transposed(var)
    order = [[(0, "w_in"), "small"], [(0, "w_out"), (0, "w_up")], [(0, "w_down")], [(1, "w_in")], [(1, "w_out"), (1, "w_up")],
             [(1, "w_down")]]
    state, token = _gather_start([[wt["w_in"][0].astype(BF16), small_pack]], "gather_start_0")
    shard = lambda l, k: (wt[k][l] + token[0, 0]).astype(BF16)
    rest, token_rest = _gather_start([[shard(*lk) for lk in g] for g in order[1:]], "gather_start_1")
    state = state + rest
    landed = {}

    def fetch(l, k, after):
        gi = [i for i, g in enumerate(order) if (l, k) in g][0]
        if gi not in landed:
            landed[gi] = _gather_wait(state[gi], after, "gather_wait_%d" % gi)
        raw = landed[gi][order[gi].index((l, k))]
        if k == "w_in":
            return raw.reshape(D_IN, D_MODEL)
        return raw.reshape(D_MODEL, D_MODEL) if k == "w_out" else raw

    seq = x.shape[1]
    t_real = N_META + seq
    t_pad = -(-t_real // ROW_ALIGN) * ROW_ALIGN
    landed[0] = _gather_wait(state[0], token_rest, "gather_wait_0")
    wfull = {k: (w[k] + token[0, 0] if k in ("pool_w", "rg_w_a", "rg_w_x") else w[k]) for k in WEIGHTS}
    off = 0
    for k, rows in zip(SMALL_SHARDED, small_rows):
        sm = landed[0][1][:, off:off + rows].reshape(N_CHIPS, -1)[:, :w[k].size].reshape((N_CHIPS,) + w[k].shape)
        wfull[k] = _from_shard_major(k, sm)
        off += rows
    sched = _GradientSchedule(wt, momt, vart, c1, me1)
    loss, dh = _local_step(x[0], loss_target[0], t_real, t_pad, wfull, fetch, sched)
    grad_x = dh[None]

    names = SMALL
    two_d = lambda a: a.reshape(1, -1) if a.ndim == 1 else a
    sched.run([("b0", "finish"), ("c0", "to_sibling"), ("small", "step")], sched.last)
    summed = dict(zip(names, sched.small_sum.vs))
    for k in SMALL_SHARDED:
        ax = SHARDED_AXIS[k]
        summed[k] = lax.dynamic_slice_in_dim(summed[k], me1[0] * w[k].shape[ax], w[k].shape[ax], axis=ax)

    out = {}
    res = _adamw_small([two_d(w[k]) for k in names], [summed[k] for k in names], [two_d(mom[k]) for k in names],
                       [two_d(var[k]) for k in names])
    for k, (d, m2, v2) in zip(names, res):
        out[k] = tuple(o.reshape(w[k].shape) for o in (summed[k], d, m2, v2))
    sched.run([("c0", "finish")], res[0][0])
    out.update(sched.out)
    out["w_in"] = tuple(o.transpose(0, 2, 1) for o in out["w_in"])

    loss = lax.psum(loss, ("x", "y", "c"))
    return (loss, grad_x, *[out[k][0] for k in WEIGHTS], *[out[k][1] for k in WEIGHTS],
            *[out[k][2] for k in WEIGHTS], *[out[k][3] for k in WEIGHTS])
```

```python
import functools

import jax
import jax.numpy as jnp
from jax import lax
from jax.experimental import pallas as pl
from jax.experimental.pallas import tpu as pltpu

F32, BF16 = jnp.float32, jnp.bfloat16
MESH = pl.DeviceIdType.MESH
ANY = pl.BlockSpec(memory_space=pl.ANY)

D_MODEL = 1024
N_META = 16
D_POOL = 256
D_CONV = 256
D_RNN = 512
D_IN = D_POOL + 2 * D_CONV + 2 * D_RNN
D_FF = 4096
FF_CHUNK = 1024
POOL_GW = 64
CONV_K = 31
RG_CONV_K = 4
RG_HD = 64
RG_C = 8.0
EPS = 1e-6
ADAM_LR, ADAM_B1, ADAM_B2, ADAM_EPS, ADAM_WD, ADAM_STEP = 0.001, 0.9, 0.999, 1e-08, 0.01, 10

HALO = 32
ROW_ALIGN = 256
TM_MIX = 384
TM_MAT = 768
TM_MLP_BWD = 384
TM_TN = 2816
N_CHIPS = 4
ROW_CHUNKS = 1
VMEM_LIMIT = 56 * 1024 * 1024

BIG = ("w_in", "w_out", "w_up", "w_down")
SMALL_SHARDED = ("meta_tokens", "convb_dw_w", "convb_pw_w", "rg_conv_w")
SMALL_REPL = ("mix_norm_g", "pool_w", "pool_scale", "convb_dw_b", "convb_ln_g", "convb_ln_b", "rg_conv_b",
              "rg_w_a", "rg_b_a", "rg_w_x", "rg_b_x", "rg_lambda", "mlp_norm_g", "final_norm_g")
SMALL = SMALL_REPL + SMALL_SHARDED
WEIGHTS = ("meta_tokens", "mix_norm_g", "w_in", "pool_w", "pool_scale", "convb_dw_w", "convb_dw_b", "convb_ln_g",
           "convb_ln_b", "convb_pw_w", "rg_conv_w", "rg_conv_b", "rg_w_a", "rg_b_a", "rg_w_x", "rg_b_x",
           "rg_lambda", "w_out", "mlp_norm_g", "w_up", "w_down", "final_norm_g")


def _params(*sem):
    return pltpu.CompilerParams(dimension_semantics=sem, vmem_limit_bytes=VMEM_LIMIT)


def _row_tile(t, cap):
    best = None
    for tm in range(128, cap + 1, 128):
        if t % tm == 0:
            best = tm
    assert best is not None, (t, cap)
    return best


def _dot(a, b):
    return jnp.dot(a, b, preferred_element_type=F32)


def _dot_nt(a, b):
    return lax.dot_general(a, b, (((1,), (1,)), ((), ())), preferred_element_type=F32)


def _dot_tn(a, b):
    return lax.dot_general(a, b, (((0,), (0,)), ((), ())), preferred_element_type=F32)


def _rms(x):
    r = lax.rsqrt(jnp.mean(x * x, axis=-1, keepdims=True) + EPS)
    return r, x * r


def _rms_bwd(du, n, r, g):
    dn = du * g
    return r * (dn - n * jnp.mean(dn * n, axis=-1, keepdims=True))


def _sig(x):
    return jax.nn.sigmoid(x)


def _colsum(x):
    return jnp.sum(x, axis=0, keepdims=True)


def _one_minus_sq(a, log_a):
    x = 2.0 * log_a
    series = -x * (1.0 + x * (0.5 + x * (1.0 / 6)))
    return jnp.where(x > -0.01, series, 1.0 - a * a)


_GELU_K0 = 0.7978845608028654
_GELU_K1 = 0.044715


def _gelu_and_grad(x):
    th = jnp.tanh(_GELU_K0 * (x + _GELU_K1 * x * x * x))
    val = 0.5 * x * (1.0 + th)
    grad = 0.5 * (1.0 + th) + 0.5 * x * (1.0 - th * th) * _GELU_K0 * (1.0 + 3.0 * _GELU_K1 * x * x)
    return val, grad


def _full(a):
    nd = a.ndim
    return pl.BlockSpec(a.shape, lambda *_: (0,) * nd)


def _resident(a):
    nd = a.ndim
    return pl.BlockSpec(a.shape, lambda *_: (0,) * nd, pipeline_mode=pl.Buffered(1))


def _after(body, n_in, deps):
    def wrapped(*refs):
        return body(*refs[:n_in], *refs[n_in + len(deps):])
    return wrapped


def _lane_sel(lane, a2, a4, a8, a16):
    return jnp.where(lane < POOL_GW, a2, jnp.where(lane < 2 * POOL_GW, a4, jnp.where(lane < 3 * POOL_GW, a8, a16)))


def _window_sums_back(src, tmp_a, tmp_b, tm):
    n = HALO + tm
    rows = lambda ref, lo, back: ref[pl.ds(lo - back, n - lo), :]
    tmp_a[pl.ds(8, n - 8), :] = rows(src, 8, 0) + rows(src, 8, 1)
    tmp_b[pl.ds(16, n - 16), :] = rows(tmp_a, 16, 0) + rows(tmp_a, 16, 2)
    s2 = rows(tmp_a, HALO, 0)
    tmp_a[pl.ds(24, n - 24), :] = rows(tmp_b, 24, 0) + rows(tmp_b, 24, 4)
    s8 = rows(tmp_a, HALO, 0)
    return s2, rows(tmp_b, HALO, 0), s8, s8 + rows(tmp_a, HALO, 8)


def _window_sums_ahead(src, tmp_a, tmp_b, tm):
    rows = lambda ref, n, ahead: ref[pl.ds(ahead, n), :]
    tmp_a[pl.ds(0, tm + 24), :] = rows(src, tm + 24, 0) + rows(src, tm + 24, 1)
    tmp_b[pl.ds(0, tm + 16), :] = rows(tmp_a, tm + 16, 0) + rows(tmp_a, tm + 16, 2)
    s2 = rows(tmp_a, tm, 0)
    tmp_a[pl.ds(0, tm + 8), :] = rows(tmp_b, tm + 8, 0) + rows(tmp_b, tm + 8, 4)
    s8 = rows(tmp_a, tm, 0)
    return s2, rows(tmp_b, tm, 0), s8, s8 + rows(tmp_a, tm, 8)


def _pool_counts(tm, t0):
    lane = lax.broadcasted_iota(jnp.int32, (tm, D_POOL), 1)
    row = lax.broadcasted_iota(jnp.int32, (tm, D_POOL), 0) + t0
    cnt = jnp.minimum(row + 1, _lane_sel(lane, 2, 4, 8, 16)).astype(F32)
    return lane, cnt


def _pool_fwd(ext_q, tmp_a, tmp_b, tm, t0):
    lane, cnt = _pool_counts(tm, t0)
    q = ext_q[pl.ds(HALO, tm), :]
    pooled = _lane_sel(lane, *_window_sums_back(ext_q, tmp_a, tmp_b, tm)) / cnt - q
    return pooled, lane, cnt


def _taps(src, w_of, offs, tm, zbuf):
    acc = None
    for r in range(8):
        ks = [k for k in range(len(offs)) if offs[k] % 8 == r]
        if not ks:
            continue
        rows = tm + (8 if r else 0)
        z = w_of(ks[0]) * src[pl.ds(offs[ks[0]] - r, rows), :]
        for k in ks[1:]:
            z = z + w_of(k) * src[pl.ds(offs[k] - r, rows), :]
        if r:
            zbuf[...] = z
            z = zbuf[pl.ds(r, tm), :]
        acc = z if acc is None else acc + z
    return acc


def _tap_grads(d_pad, src, offs, tm, g_ref, zbuf):
    ch = src.shape[-1]
    for r in range(8):
        ks = [k for k in range(len(offs)) if offs[k] % 8 == r]
        if not ks:
            continue
        rows = tm + (8 if r else 0)
        if r:
            zbuf[...] = d_pad[pl.ds(8 - r, rows), :]
        for k in ks:
            d = zbuf[...] if r else d_pad[pl.ds(8, rows), :]
            prod = d * src[pl.ds(offs[k] - r, rows), :]
            g_ref[k] += jnp.sum(prod.reshape(rows // 8, 8, ch), axis=0)


_CONV_OFFS = [HALO - (CONV_K - 1) + k for k in range(CONV_K)]


def _conv_fwd(ext_u, dww_ref, dwb, tm, zbuf):
    return dwb + _taps(ext_u, lambda k: dww_ref[k:k + 1, :], _CONV_OFFS, tm, zbuf)


def _ln_silu(c, lng, lnb):
    mu = jnp.mean(c, axis=-1, keepdims=True)
    cc = c - mu
    rstd = lax.rsqrt(jnp.mean(cc * cc, axis=-1, keepdims=True) + EPS)
    z = cc * rstd
    l = z * lng + lnb
    sl = _sig(l)
    return z, rstd, l, sl, l * sl


def _rg_conv(ext_x, cw_ref, cb, tm):
    xc = cb + cw_ref[0:1, :] * ext_x[pl.ds(HALO - (RG_CONV_K - 1), tm), :]
    for k in range(1, RG_CONV_K):
        xc = xc + cw_ref[k:k + 1, :] * ext_x[pl.ds(HALO - (RG_CONV_K - 1) + k, tm), :]
    return xc


def _softplus_neg(lam):
    return jnp.maximum(-lam, 0.0) + jnp.log(1.0 + jnp.exp(-jnp.abs(lam)))


def _rg_gates(xc, wa, ba, wx, bx, lam):
    xcb = xc.astype(BF16)
    r = _sig(_dot(xcb, wa) + ba)
    ig = _sig(_dot(xcb, wx) + bx)
    log_a = (-RG_C * r) * _softplus_neg(lam)
    a = jnp.exp(log_a)
    return r, ig, a, jnp.sqrt(_one_minus_sq(a, log_a))


def _scan_rows(a_ref, b_ref, out_ref, carry, tm, reverse):
    rows = lax.broadcasted_iota(jnp.int32, (8, D_RNN), 0)
    ngrp = tm // 8

    def grp(gi, hb):
        st = pl.multiple_of((ngrp - 1 - gi if reverse else gi) * 8, 8)
        a8 = a_ref[pl.ds(st, 8), :]
        b8 = b_ref[pl.ds(st, 8), :]
        out = jnp.zeros((8, D_RNN), F32)
        for j in (range(7, -1, -1) if reverse else range(8)):
            aj = jnp.broadcast_to(a8[j:j + 1, :], (8, D_RNN))
            bj = jnp.broadcast_to(b8[j:j + 1, :], (8, D_RNN))
            if reverse:
                cur = bj + hb
                hb = aj * cur
            else:
                cur = aj * hb + bj
                hb = cur
            out = jnp.where(rows == j, cur, out)
        out_ref[pl.ds(st, 8), :] = out
        return hb

    carry[...] = lax.fori_loop(0, ngrp, grp, carry[...])


_MIX_W = ("wp", "psc", "dww", "dwb", "lng", "lnb", "wpw", "cw", "cb", "wa", "ba", "wx", "bx", "lam")


def _lead_tile(src_ref, dst, carry, i, nt, tm, lead, n_src):
    last = n_src - (nt - 1) * tm
    assert 0 < last and lead + last <= tm, (n_src, nt, tm)
    dst[0:lead, :] = carry[...]

    @pl.when(i < nt - 1)
    def _():
        dst[lead:tm, :] = src_ref[0:tm - lead, :]
        carry[...] = src_ref[tm - lead:tm, :]

    @pl.when(i == nt - 1)
    def _():
        dst[lead:lead + last, :] = src_ref[0:last, :]
        if lead + last < tm:
            dst[lead + last:tm, :] = jnp.zeros((tm - lead - last, dst.shape[1]), dst.dtype)


def _mixer_fwd(h, g, w_in, mw, lead=None, t_pad=None):
    t = h.shape[0] if lead is None else t_pad
    tm = _row_tile(t, TM_MIX)
    nt = t // tm
    n_lead = 0 if lead is None else lead.shape[0]

    def body(h_ref, g_ref, win_ref, wp, psc, dww, dwb, lng, lnb, wpw, cw, cb, wa, ba, wx, bx, lam, *rest):
        if n_lead:
            lead_ref, rest = rest[0], rest[1:]
            h_out, h_carry = rest[7], rest[-1]
            rest = rest[:7] + rest[8:-1]
        (y_ref, p_ref, u_ref, hs_ref, conv_ref, xc_ref, gates_ref,
         ext_q, ext_u, ext_x, tmp_a, tmp_b, zbuf, a_s, b_s, hcar) = rest
        i = pl.program_id(0)

        @pl.when(i == 0)
        def _():
            ext_q[0:HALO, :] = jnp.zeros((HALO, D_POOL), F32)
            ext_u[0:HALO, :] = jnp.zeros((HALO, D_CONV), F32)
            ext_x[0:HALO, :] = jnp.zeros((HALO, D_RNN), F32)
            hcar[...] = jnp.zeros((8, D_RNN), F32)
            if n_lead:
                h_carry[...] = lead_ref[...]

        if n_lead:
            _lead_tile(h_ref, h_out, h_carry, i, nt, tm, n_lead, h.shape[0])
            h_ref = h_out
        u = (_rms(h_ref[...])[1] * g_ref[...]).astype(BF16)
        u_ref[...] = u
        p_ref[...] = _dot_nt(u, win_ref[...])

        ext_q[pl.ds(HALO, tm), :] = p_ref[:, 0:256]
        pooled, _, _ = _pool_fwd(ext_q, tmp_a, tmp_b, tm, i * tm)
        y_ref[:, 0:256] = (_dot(pooled.astype(BF16), wp[...]) * psc[...]).astype(BF16)

        ext_u[pl.ds(HALO, tm), :] = p_ref[:, 256:512] * _sig(p_ref[:, 512:768])
        conv = _conv_fwd(ext_u, dww, dwb[...], tm, zbuf)
        conv_ref[...] = conv
        act = _ln_silu(conv, lng[...], lnb[...])[4]
        y_ref[:, 256:512] = _dot(act.astype(BF16), wpw[...]).astype(BF16)

        ext_x[pl.ds(HALO, tm), :] = p_ref[:, 1280:1792]
        xc = _rg_conv(ext_x, cw, cb[...], tm)
        xc_ref[...] = xc
        r, ig, a, m = _rg_gates(xc, wa[...], ba[...], wx[...], bx[...], lam[...])
        for j, gate in enumerate((r, ig, a, m)):
            gates_ref[:, j * D_RNN:(j + 1) * D_RNN] = gate
        a_s[...] = a
        b_s[...] = m * (ig * xc)
        _scan_rows(a_s, b_s, hs_ref, hcar, tm, reverse=False)
        y_ref[:, 512:1024] = (_gelu_and_grad(p_ref[:, 768:1280])[0] * hs_ref[...]).astype(BF16)

        ext_q[0:HALO, :] = ext_q[pl.ds(tm, HALO), :]
        ext_u[0:HALO, :] = ext_u[pl.ds(tm, HALO), :]
        ext_x[0:HALO, :] = ext_x[pl.ds(tm, HALO), :]

    ws = [mw[k] for k in _MIX_W]
    row = lambda w: pl.BlockSpec((tm, w), lambda i: (i, 0))
    extra = [] if lead is None else [lead]
    return pl.pallas_call(
        body, name="mixer_fwd", grid=(nt,),
        in_specs=[row(D_MODEL), _full(g), _resident(w_in)] + [_full(w) for w in ws] + [_full(a) for a in extra],
        out_specs=[row(D_MODEL), row(D_IN), row(D_MODEL), row(D_RNN), row(D_CONV), row(D_RNN), row(4 * D_RNN)]
        + [row(D_MODEL)] * len(extra),
        out_shape=[jax.ShapeDtypeStruct((t, D_MODEL), BF16), jax.ShapeDtypeStruct((t, D_IN), F32),
                   jax.ShapeDtypeStruct((t, D_MODEL), BF16), jax.ShapeDtypeStruct((t, D_RNN), F32),
                   jax.ShapeDtypeStruct((t, D_CONV), F32), jax.ShapeDtypeStruct((t, D_RNN), F32),
                   jax.ShapeDtypeStruct((t, 4 * D_RNN), F32)] + [jax.ShapeDtypeStruct((t, D_MODEL), F32)] * len(extra),
        scratch_shapes=[pltpu.VMEM((HALO + tm, D_POOL), F32), pltpu.VMEM((HALO + tm, D_CONV), F32),
                        pltpu.VMEM((HALO + tm, D_RNN), F32), pltpu.VMEM((HALO + tm, D_POOL), F32),
                        pltpu.VMEM((HALO + tm, D_POOL), F32), pltpu.VMEM((tm + 8, D_CONV), F32),
                        pltpu.VMEM((tm, D_RNN), F32), pltpu.VMEM((tm, D_RNN), F32), pltpu.VMEM((8, D_RNN), F32)]
        + [pltpu.VMEM(a.shape, F32) for a in extra],
        compiler_params=_params("arbitrary"),
    )(h, g, w_in, *ws, *extra)


_MIX_G = (("wp", (D_POOL, D_POOL)), ("psc", (1, D_POOL)), ("dww", (32, 8, D_CONV)), ("dwb", (1, D_CONV)),
          ("lng", (1, D_CONV)), ("lnb", (1, D_CONV)), ("wpw", (D_CONV, D_CONV)), ("cw", (8, D_RNN)),
          ("cb", (1, D_RNN)), ("wa", (D_RNN, D_RNN)), ("ba", (1, D_RNN)), ("wx", (D_RNN, D_RNN)),
          ("bx", (1, D_RNN)), ("lam", (1, D_RNN)), ("g1", (1, D_MODEL)))


def _mixer_bwd(p, dh1, hs, conv, xc, gates, h0, g1, w_out, w_in, mw, deps=(), lead=0, t_real=None):
    t = p.shape[0]
    tm = _row_tile(t, TM_MIX)
    nt = t // tm
    hb = tm // HALO
    t_out = t_real - lead if lead else t

    def body(p_ref, ph_ref, dh1_ref, hs_ref, hsh_ref, conv_ref, xc_ref, gates_ref, h0_ref, g1_ref, wout_ref, win_ref,
             wp, psc, dww, dwb, lng, lnb, wpw, cw, cb, wa, ba, wx, bx, lam,
             dp_ref, dh0_ref, g_wp, g_psc, g_dww, g_dwb, g_lng, g_lnb, g_wpw, g_cw, g_cb, g_wa, g_ba, g_wx, g_bx, g_lam, g_g1,
             *tail):
        dlead_ref, carry = (tail[0], tail[-1]) if lead else (None, None)
        (ext_q, ext_u, ext_x, ext_h, ee, dc_s, dx_s, tmp_a, tmp_b, zbuf, d_pad, a_s, b_s, g_s, gcar, dy_ref,
         dp_s) = tail[1:-1] if lead else tail[1:]
        step = pl.program_id(0)
        i = nt - 1 - step
        grads = (g_wp, g_psc, g_dww, g_dwb, g_lng, g_lnb, g_wpw, g_cw, g_cb, g_wa, g_ba, g_wx, g_bx, g_lam, g_g1)
        if lead:
            grads += (carry,)
        dy_ref[...] = dh1_ref[...].astype(BF16)
        dy_cols = lambda lo, hi: _dot_nt(dy_ref[...], wout_ref[lo:hi, :])

        @pl.when(step == 0)
        def _():
            for gr in grads:
                gr[...] = jnp.zeros(gr.shape, F32)
            ee[pl.ds(tm, HALO), :] = jnp.zeros((HALO, D_POOL), F32)
            dc_s[pl.ds(tm, HALO), :] = jnp.zeros((HALO, D_CONV), F32)
            dx_s[pl.ds(tm, HALO), :] = jnp.zeros((HALO, D_RNN), F32)
            d_pad[0:8, :] = jnp.zeros((8, D_CONV), F32)
            d_pad[pl.ds(tm + 8, 8), :] = jnp.zeros((8, D_CONV), F32)
            gcar[...] = jnp.zeros((8, D_RNN), F32)

        hm = jnp.where(i == 0, 0.0, 1.0)

        ext_q[0:HALO, :] = ph_ref[:, 0:256] * hm
        ext_q[pl.ds(HALO, tm), :] = p_ref[:, 0:256]
        pooled, lane, cnt = _pool_fwd(ext_q, tmp_a, tmp_b, tm, i * tm)
        pooled_b = pooled.astype(BF16)
        dya = dy_cols(0, 256)
        g_psc[...] += _colsum(dya * _dot(pooled_b, wp[...]))
        dmixed_b = (dya * psc[...]).astype(BF16)
        dpooled = _dot_nt(dmixed_b, wp[...])
        g_wp[...] += _dot_tn(pooled_b, dmixed_b)
        ee[0:tm, :] = dpooled / cnt
        dp_s[:, 0:256] = _lane_sel(lane, *_window_sums_ahead(ee, tmp_a, tmp_b, tm)) - dpooled
        ee[pl.ds(tm, HALO), :] = ee[0:HALO, :]

        v = p_ref[:, 256:512]
        s = _sig(p_ref[:, 512:768])
        ext_u[0:HALO, :] = ph_ref[:, 256:512] * _sig(ph_ref[:, 512:768]) * hm
        ext_u[pl.ds(HALO, tm), :] = v * s
        z, rstd, l, sl, act = _ln_silu(conv_ref[...], lng[...], lnb[...])
        dyb_b = dy_cols(256, 512).astype(BF16)
        dact = _dot_nt(dyb_b, wpw[...])
        g_wpw[...] += _dot_tn(act.astype(BF16), dyb_b)
        dl = dact * (sl * (1.0 + l * (1.0 - sl)))
        g_lng[...] += _colsum(dl * z)
        g_lnb[...] += _colsum(dl)
        dz = dl * lng[...]
        dc = rstd * (dz - jnp.mean(dz, axis=-1, keepdims=True) - z * jnp.mean(dz * z, axis=-1, keepdims=True))
        g_dwb[...] += _colsum(dc)
        dc_s[0:tm, :] = dc
        d_pad[pl.ds(8, tm), :] = dc
        _tap_grads(d_pad, ext_u, _CONV_OFFS, tm, g_dww, zbuf)
        du0 = _taps(dc_s, lambda j: dww[CONV_K - 1 - j:CONV_K - j, :], list(range(CONV_K)), tm, zbuf)
        dp_s[:, 256:512] = du0 * s
        dp_s[:, 512:768] = du0 * v * (s * (1.0 - s))
        dc_s[pl.ds(tm, HALO), :] = dc_s[0:HALO, :]

        ext_x[0:HALO, :] = ph_ref[:, 1280:1792] * hm
        ext_x[pl.ds(HALO, tm), :] = p_ref[:, 1280:1792]
        xc = xc_ref[...]
        xcb = xc.astype(BF16)
        r, ig, a, m = (gates_ref[:, j * D_RNN:(j + 1) * D_RNN] for j in range(4))
        sp = _softplus_neg(lam[...])
        ext_h[0:HALO, :] = hsh_ref[...] * hm
        ext_h[pl.ds(HALO, tm), :] = hs_ref[...]
        dyc = dy_cols(512, 1024)
        gl, dgl = _gelu_and_grad(p_ref[:, 768:1280])
        dp_s[:, 768:1280] = dyc * hs_ref[...] * dgl
        a_s[...] = a
        b_s[...] = dyc * gl
        _scan_rows(a_s, b_s, g_s, gcar, tm, reverse=True)
        g = g_s[...]
        da = g * ext_h[pl.ds(HALO - 1, tm), :]
        dm = g * (ig * xc)
        dig = g * (m * xc)
        dlog_a = da * a - dm * (a * a) / m
        g_lam[...] += _colsum(dlog_a * (-RG_C * r)) * (-_sig(-lam[...]))
        dra = (dlog_a * (-RG_C * sp)) * (r * (1.0 - r))
        dia = dig * (ig * (1.0 - ig))
        g_ba[...] += _colsum(dra)
        g_bx[...] += _colsum(dia)
        dra_b = dra.astype(BF16)
        dia_b = dia.astype(BF16)
        dxc = g * (m * ig) + _dot_nt(dra_b, wa[...]) + _dot_nt(dia_b, wx[...])
        g_wa[...] += _dot_tn(xcb, dra_b)
        g_wx[...] += _dot_tn(xcb, dia_b)
        g_cb[...] += _colsum(dxc)
        dx_s[0:tm, :] = dxc
        for k in range(RG_CONV_K):
            g_cw[k:k + 1, :] += _colsum(dxc * ext_x[pl.ds(HALO - (RG_CONV_K - 1) + k, tm), :])
        dxin = cw[RG_CONV_K - 1:RG_CONV_K, :] * dxc
        for j in range(1, RG_CONV_K):
            dxin = dxin + cw[RG_CONV_K - 1 - j:RG_CONV_K - j, :] * dx_s[pl.ds(j, tm), :]
        dp_s[:, 1280:1792] = dxin
        dx_s[pl.ds(tm, HALO), :] = dx_s[0:HALO, :]

        dpb = dp_s[...].astype(BF16)
        dp_ref[...] = dpb
        du = _dot(dpb, win_ref[...])
        r, n = _rms(h0_ref[...])
        g_g1[...] += _colsum(du * n)
        dh0 = dh1_ref[...] + _rms_bwd(du, n, r, g1_ref[...])
        if lead:
            dh0_ref[0:tm - lead, :] = dh0[lead:tm, :]
            dh0_ref[tm - lead:tm, :] = carry[...]
            carry[...] = dh0[0:lead, :]

            @pl.when(i == 0)
            def _():
                dlead_ref[...] = dh0[0:lead, :]
        else:
            dh0_ref[...] = dh0
            tail[0][...] = dh0.astype(BF16)

    ws = [mw[k] for k in _MIX_W]
    tile = lambda w: pl.BlockSpec((tm, w), lambda s: (nt - 1 - s, 0))
    halo = lambda w: pl.BlockSpec((HALO, w), lambda s: (jnp.maximum((nt - 1 - s) * hb - 1, 0), 0))
    lead_out = [pl.BlockSpec((lead, D_MODEL), lambda s: (0, 0))] if lead else [tile(D_MODEL)]
    outs = pl.pallas_call(
        _after(body, 12 + len(ws), deps), name="mixer_bwd", grid=(nt,),
        in_specs=[tile(D_IN), halo(D_IN), tile(D_MODEL), tile(D_RNN), halo(D_RNN), tile(D_CONV), tile(D_RNN), tile(4 * D_RNN),
                  tile(D_MODEL), _full(g1),
                  _resident(w_out), _resident(w_in)] + [_full(w) for w in ws] + [ANY] * len(deps),
        out_specs=[tile(D_IN), tile(D_MODEL)] + [pl.BlockSpec(shp, lambda s, nd=len(shp): (0,) * nd) for _, shp in _MIX_G]
        + lead_out,
        out_shape=[jax.ShapeDtypeStruct((t, D_IN), BF16), jax.ShapeDtypeStruct((t_out, D_MODEL), F32)]
        + [jax.ShapeDtypeStruct(shp, F32) for _, shp in _MIX_G]
        + [jax.ShapeDtypeStruct((lead, D_MODEL), F32) if lead else jax.ShapeDtypeStruct((t, D_MODEL), BF16)],
        scratch_shapes=[pltpu.VMEM((HALO + tm, D_POOL), F32), pltpu.VMEM((HALO + tm, D_CONV), F32),
                        pltpu.VMEM((HALO + tm, D_RNN), F32), pltpu.VMEM((HALO + tm, D_RNN), F32),
                        pltpu.VMEM((tm + HALO, D_POOL), F32), pltpu.VMEM((tm + HALO, D_CONV), F32),
                        pltpu.VMEM((tm + HALO, D_RNN), F32), pltpu.VMEM((HALO + tm, D_POOL), F32),
                        pltpu.VMEM((HALO + tm, D_POOL), F32), pltpu.VMEM((tm + 8, D_CONV), F32),
                        pltpu.VMEM((tm + 16, D_CONV), F32), pltpu.VMEM((tm, D_RNN), F32),
                        pltpu.VMEM((tm, D_RNN), F32), pltpu.VMEM((tm, D_RNN), F32), pltpu.VMEM((8, D_RNN), F32),
                        pltpu.VMEM((tm, D_MODEL), BF16), pltpu.VMEM((tm, D_IN), F32)]
        + [pltpu.VMEM((lead, D_MODEL), F32)] * bool(lead),
        compiler_params=_params("arbitrary"),
    )(p, p, dh1, hs, hs, conv, xc, gates, h0, g1, w_out, w_in, *ws, *deps)
    grads = {k: o for (k, _), o in zip(_MIX_G, outs[2:])}
    return outs[0], (outs[1], outs[-1]), grads


def _mid_fwd(y, h0, w_out, g, w_up):
    t = h0.shape[0]
    tm = _row_tile(t, TM_MAT)

    def body(y_ref, h0_ref, wo_ref, g_ref, wu_ref, h1_ref, u2_ref, f_ref):
        h1 = h0_ref[...] + _dot(y_ref[...], wo_ref[...])
        h1_ref[...] = h1
        u2 = (_rms(h1)[1] * g_ref[...]).astype(BF16)
        u2_ref[...] = u2
        for c in range(D_FF // FF_CHUNK):
            f_ref[:, c * FF_CHUNK:(c + 1) * FF_CHUNK] = _dot(u2, wu_ref[c]).astype(BF16)

    row = lambda w: pl.BlockSpec((tm, w), lambda i: (i, 0))
    return pl.pallas_call(
        body, name="mid_fwd", grid=(t // tm,),
        in_specs=[row(D_MODEL), row(D_MODEL), _resident(w_out), _full(g), _resident(w_up)],
        out_specs=[row(D_MODEL), row(D_MODEL), row(D_FF)],
        out_shape=[jax.ShapeDtypeStruct((t, D_MODEL), F32), jax.ShapeDtypeStruct((t, D_MODEL), BF16),
                   jax.ShapeDtypeStruct((t, D_FF), BF16)],
        compiler_params=_params("parallel"),
    )(y, h0, w_out, g, w_up)


def _down_proj(f_ref, h1_ref, wd_ref):
    acc = h1_ref[...]
    for c in range(D_FF // FF_CHUNK):
        cols = slice(c * FF_CHUNK, (c + 1) * FF_CHUNK)
        a = jnp.square(jnp.maximum(f_ref[:, cols].astype(F32), 0.0)).astype(BF16)
        acc = acc + _dot(a, wd_ref[cols, :])
    return acc


def _down_fwd(f, h1, w_down):
    t = h1.shape[0]
    tm = _row_tile(t, TM_MAT)

    def body(f_ref, h1_ref, wd_ref, h2_ref):
        h2_ref[...] = _down_proj(f_ref, h1_ref, wd_ref)

    row = lambda w: pl.BlockSpec((tm, w), lambda i: (i, 0))
    return pl.pallas_call(
        body, name="down_fwd", grid=(t // tm,),
        in_specs=[row(D_FF), row(D_MODEL), _resident(w_down)], out_specs=row(D_MODEL),
        out_shape=jax.ShapeDtypeStruct((t, D_MODEL), F32),
        compiler_params=_params("parallel"),
    )(f, h1, w_down)


def _down_fwd_loss(f, h1, w_down, g, tgt, t_real):
    t = h1.shape[0]
    tm = _row_tile(t, TM_MAT)
    nt = t // tm

    def body(f_ref, h1_ref, wd_ref, g_ref, tgt_in, loss_ref, dh_ref, dg_ref, dhb_ref, tgt_ref, carry):
        i = pl.program_id(0)

        @pl.when(i == 0)
        def _():
            loss_ref[...] = jnp.zeros(loss_ref.shape, F32)
            dg_ref[...] = jnp.zeros(dg_ref.shape, F32)
            carry[...] = jnp.zeros(carry.shape, F32)

        _lead_tile(tgt_in, tgt_ref, carry, i, nt, tm, N_META, tgt.shape[0])

        r, n = _rms(_down_proj(f_ref, h1_ref, wd_ref))
        row = lax.broadcasted_iota(jnp.int32, (tm, 1), 0) + i * tm
        valid = jnp.logical_and(row >= N_META, row < t_real)
        diff = jnp.where(valid, n * g_ref[...] - tgt_ref[...], 0.0)
        loss_ref[...] += 0.5 * jnp.sum(jnp.mean(diff * diff, axis=-1, keepdims=True))
        dy = diff * (1.0 / D_MODEL)
        dg_ref[...] += _colsum(dy * n)
        dh = _rms_bwd(dy, n, r, g_ref[...])
        dh_ref[...] = dh
        dhb_ref[...] = dh.astype(BF16)

    row = lambda w: pl.BlockSpec((tm, w), lambda i: (i, 0))
    return pl.pallas_call(
        body, name="down_fwd_loss", grid=(t // tm,),
        in_specs=[row(D_FF), row(D_MODEL), _resident(w_down), _full(g), row(D_MODEL)],
        out_specs=[pl.BlockSpec((8, 128), lambda i: (0, 0)), row(D_MODEL), pl.BlockSpec((1, D_MODEL), lambda i: (0, 0)),
                   row(D_MODEL)],
        out_shape=[jax.ShapeDtypeStruct((8, 128), F32), jax.ShapeDtypeStruct((t, D_MODEL), F32),
                   jax.ShapeDtypeStruct((1, D_MODEL), F32), jax.ShapeDtypeStruct((t, D_MODEL), BF16)],
        scratch_shapes=[pltpu.VMEM((tm, D_MODEL), F32), pltpu.VMEM((N_META, D_MODEL), F32)],
        compiler_params=_params("arbitrary"),
    )(f, h1, w_down, g, tgt)


def _mlp_bwd(dh2, f, h1, g, w_up, w_down, deps=()):
    t = dh2.shape[0]
    tm = _row_tile(t, TM_MLP_BWD)

    def body(dh2_ref, f_ref, wd_ref, wu_ref, h1_ref, g_ref, df_ref, dh1_ref, dg_ref, dh1b_ref):
        @pl.when(pl.program_id(0) == 0)
        def _():
            dg_ref[...] = jnp.zeros(dg_ref.shape, F32)

        dh2 = dh2_ref[...]
        dhb = dh2.astype(BF16)
        du2 = None
        for c in range(D_FF // FF_CHUNK):
            cols = slice(c * FF_CHUNK, (c + 1) * FF_CHUNK)
            dact = _dot_nt(dhb, wd_ref[c])
            df = (dact * (2.0 * jnp.maximum(f_ref[:, cols].astype(F32), 0.0))).astype(BF16)
            df_ref[:, cols] = df
            part = _dot_nt(df, wu_ref[c])
            du2 = part if du2 is None else du2 + part
        r, n = _rms(h1_ref[...])
        dg_ref[...] += _colsum(du2 * n)
        dh1 = dh2 + _rms_bwd(du2, n, r, g_ref[...])
        dh1_ref[...] = dh1
        dh1b_ref[...] = dh1.astype(BF16)

    row = lambda w: pl.BlockSpec((tm, w), lambda i: (i, 0))
    return pl.pallas_call(
        _after(body, 6, deps), name="mlp_bwd", grid=(t // tm,),
        in_specs=[row(D_MODEL), row(D_FF), _resident(w_down), _resident(w_up), row(D_MODEL), _full(g)] + [ANY] * len(deps),
        out_specs=[row(D_FF), row(D_MODEL), pl.BlockSpec((1, D_MODEL), lambda i: (0, 0)), row(D_MODEL)],
        out_shape=[jax.ShapeDtypeStruct((t, D_FF), BF16), jax.ShapeDtypeStruct((t, D_MODEL), F32),
                   jax.ShapeDtypeStruct((1, D_MODEL), F32), jax.ShapeDtypeStruct((t, D_MODEL), BF16)],
        compiler_params=_params("arbitrary"),
    )(dh2, f, w_down, w_up, h1, g, *deps)


def _tn_matmul(a, b, kc, nc, relu2, name, deps=()):
    t, k = a.shape
    n = b.shape[1]
    tt = _row_tile(t, TM_TN)
    gk, gn = k // kc, n // nc

    def body(a_ref, b_ref, o_ref):
        @pl.when(pl.program_id(2) == 0)
        def _():
            o_ref[...] = jnp.zeros(o_ref.shape, F32)

        av = a_ref[...]
        if relu2:
            av = jnp.square(jnp.maximum(av.astype(F32), 0.0))
        o_ref[...] += _dot_tn(av.astype(BF16), b_ref[...].astype(BF16))

    return pl.pallas_call(
        _after(body, 2, deps), name=name, grid=(gk, gn, t // tt),
        in_specs=[pl.BlockSpec((tt, kc), lambda ik, jn, it: (it, ik)), pl.BlockSpec((tt, nc), lambda ik, jn, it: (it, jn))]
        + [ANY] * len(deps),
        out_specs=pl.BlockSpec((None, kc, nc), lambda ik, jn, it: (ik * gn + jn, 0, 0)),
        out_shape=jax.ShapeDtypeStruct((gk * gn, kc, nc), F32),
        compiler_params=_params("parallel", "parallel", "arbitrary"),
    )(a, b, *deps)


def _block_diag(blocks):
    nb, hd, _ = blocks.shape
    eye = jnp.eye(nb, dtype=blocks.dtype)
    return (blocks[:, :, None, :] * eye[:, None, :, None]).reshape(nb * hd, nb * hd)


def _diag_blocks(m, nb):
    hd = m.shape[0] // nb
    eye = jnp.eye(nb, dtype=m.dtype)
    return jnp.sum(m.reshape(nb, hd, nb, hd) * eye[:, None, :, None], axis=2)


def _mixer_weights(w, l):
    row = lambda a: a.reshape(1, -1)
    return dict(
        wp=_block_diag(w["pool_w"][l]).astype(BF16), psc=row(w["pool_scale"][l]),
        dww=jnp.pad(w["convb_dw_w"][l], ((0, 32 - CONV_K), (0, 0))), dwb=row(w["convb_dw_b"][l]),
        lng=row(w["convb_ln_g"][l]), lnb=row(w["convb_ln_b"][l]), wpw=w["convb_pw_w"][l].astype(BF16),
        cw=jnp.pad(w["rg_conv_w"][l], ((0, 8 - RG_CONV_K), (0, 0))), cb=row(w["rg_conv_b"][l]),
        wa=_block_diag(w["rg_w_a"][l]).astype(BF16), ba=row(w["rg_b_a"][l]),
        wx=_block_diag(w["rg_w_x"][l]).astype(BF16), bx=row(w["rg_b_x"][l]), lam=row(w["rg_lambda"][l]))


def _local_step(h, tgt, t_real, t_pad, w, fetch, hooks):
    depth = 2
    saved = []
    big = []
    for l in range(depth):
        mw = _mixer_weights(w, l)
        g1 = w["mix_norm_g"][l].reshape(1, -1)
        g2 = w["mlp_norm_g"][l].reshape(1, -1)
        wl = dict(w_in=fetch(l, "w_in", h))
        if l == 0:
            y, p, u, hs, conv, xc, gates, h = _mixer_fwd(h, g1, wl["w_in"], mw, lead=w["meta_tokens"], t_pad=t_pad)
        else:
            y, p, u, hs, conv, xc, gates = _mixer_fwd(h, g1, wl["w_in"], mw)
        wl["w_out"], wl["w_up"] = fetch(l, "w_out", y), fetch(l, "w_up", y)
        h1, u2, f = _mid_fwd(y, h, wl["w_out"], g2, wl["w_up"])
        wl["w_down"] = fetch(l, "w_down", f)
        if l == depth - 1:
            loss, dh, dgf, dh_b = _down_fwd_loss(f, h1, wl["w_down"].reshape(D_FF, D_MODEL),
                                                 w["final_norm_g"].reshape(1, -1), tgt, t_real)
            h2 = None
        else:
            h2 = _down_fwd(f, h1, wl["w_down"].reshape(D_FF, D_MODEL))
        saved.append(dict(mw=mw, g1=g1, g2=g2, h0=h, p=p, u=u, y=y, hs=hs, conv=conv, xc=xc, gates=gates, h1=h1, u2=u2, f=f))
        big.append(wl)
        h = h2

    gs = {k: [None] * depth for k in ("mix_norm_g", "mlp_norm_g", "pool_w", "pool_scale", "convb_dw_w", "convb_dw_b",
                                      "convb_ln_g", "convb_ln_b", "convb_pw_w", "rg_conv_w", "rg_conv_b", "rg_w_a",
                                      "rg_b_a", "rg_w_x", "rg_b_x", "rg_lambda")}
    deps = ()
    for l in reversed(range(depth)):
        s, wl = saved[l], big[l]
        df, dh1, dg2, dh1_b = _mlp_bwd(dh, s["f"], s["h1"], s["g2"], wl["w_up"], wl["w_down"], deps)
        deps = hooks.point(l, "mlp_bwd", dh1)
        g_down = _tn_matmul(s["f"], dh_b, FF_CHUNK, D_MODEL, True, "dw_down", deps)
        hooks.grad(l, "w_down", g_down)
        deps = hooks.point(l, "dw_down", g_down)
        g_up = _tn_matmul(s["u2"], df, D_MODEL, FF_CHUNK, False, "dw_up", deps)
        hooks.grad(l, "w_up", g_up)
        deps = hooks.point(l, "dw_up", g_up)
        g_out = _tn_matmul(s["y"], dh1_b, D_MODEL, D_MODEL, False, "dw_out", deps)
        hooks.grad(l, "w_out", g_out.reshape(N_CHIPS, D_MODEL // N_CHIPS, D_MODEL))
        deps = hooks.point(l, "dw_out", g_out)
        dp, dh, mg = _mixer_bwd(s["p"], dh1, s["hs"], s["conv"], s["xc"], s["gates"], s["h0"], s["g1"], wl["w_out"],
                                wl["w_in"], s["mw"], deps, lead=0 if l else N_META, t_real=t_real)
        dh, dh_b = dh
        if l == 0:
            dmeta = dh_b
        gs["mix_norm_g"][l] = mg["g1"][0]
        gs["mlp_norm_g"][l] = dg2[0]
        gs["pool_w"][l] = _diag_blocks(mg["wp"], D_POOL // POOL_GW)
        gs["pool_scale"][l] = mg["psc"][0]
        gs["convb_dw_w"][l] = jnp.sum(mg["dww"][:CONV_K], axis=1)
        gs["convb_dw_b"][l] = mg["dwb"][0]
        gs["convb_ln_g"][l] = mg["lng"][0]
        gs["convb_ln_b"][l] = mg["lnb"][0]
        gs["convb_pw_w"][l] = mg["wpw"]
        gs["rg_conv_w"][l] = mg["cw"][:RG_CONV_K]
        gs["rg_conv_b"][l] = mg["cb"][0]
        gs["rg_w_a"][l] = _diag_blocks(mg["wa"], D_RNN // RG_HD)
        gs["rg_b_a"][l] = mg["ba"][0]
        gs["rg_w_x"][l] = _diag_blocks(mg["wx"], D_RNN // RG_HD)
        gs["rg_b_x"][l] = mg["bx"][0]
        gs["rg_lambda"][l] = mg["lam"][0]
        if l == 0:
            gsmall = {k: jnp.stack(v) for k, v in gs.items()}
            gsmall["final_norm_g"] = dgf[0]
            gsmall["meta_tokens"] = dmeta
            started = hooks.small(gsmall)
        deps = hooks.point(l, "mixer_bwd", started[0] if l == 0 and started else dh)
        g_in = _tn_matmul(dp, s["u"], D_IN, D_MODEL, False, "dw_in", deps).reshape(N_CHIPS, D_IN // N_CHIPS, D_MODEL)
        hooks.grad(l, "w_in", g_in)
        deps = hooks.point(l, "dw_in", g_in)
    return loss[0, 0], dh


def _place():
    return lax.axis_index("x"), lax.axis_index("y"), lax.axis_index("c")


def _other_chips(x, y):
    return [(1 - x, y), (x, 1 - y), (1 - x, 1 - y)]


HBM_SPEC = pl.BlockSpec(memory_space=pltpu.HBM)
SEM_SPEC = pl.BlockSpec(memory_space=pltpu.SEMAPHORE)
DATAFLOW = pltpu.SideEffectType.DATAFLOW_SIDE_EFFECTING


def _gather_copies(src_refs, land_refs, send_sem, recv_sem, first):
    x, y, c = _place()
    me = 2 * x + y
    out = []
    for n in range(len(src_refs)):
        for j, (px, py) in enumerate(_other_chips(x, y) + [(x, y)]):
            k = first + N_CHIPS * n + j
            out.append(pltpu.make_async_remote_copy(src_refs[n], land_refs[n].at[me], send_sem.at[k], recv_sem.at[k],
                                                    device_id=(px, py, c), device_id_type=MESH))
    return out


def _gather_start(groups, name):
    srcs = [pltpu.with_memory_space_constraint(s, pltpu.HBM) for g in groups for s in g]
    lands = [pltpu.with_memory_space_constraint(lax.empty((N_CHIPS,) + s.shape, s.dtype), pltpu.HBM) for g in groups for s in g]
    n, ng = len(srcs), len(groups)
    first = [sum(len(g) for g in groups[:i]) for i in range(ng)]

    def body(*refs):
        src_refs, land_refs = refs[:n], refs[n:2 * n]
        sems = refs[2 * n:2 * n + 2 * ng]
        token = refs[-1]
        for gi, g in enumerate(groups):
            lo, hi = first[gi], first[gi] + len(g)
            for cp in _gather_copies(src_refs[lo:hi], land_refs[lo:hi], sems[2 * gi], sems[2 * gi + 1], 0):
                cp.start()
        token[...] = jnp.zeros(token.shape, token.dtype)

    sem_shapes = [pltpu.SemaphoreType.DMA((N_CHIPS * len(g),)) for g in groups for _ in range(2)]
    outs = pl.pallas_call(
        body, name=name,
        out_shape=sem_shapes + [pltpu.HBM(a.shape, a.dtype) for a in srcs + lands] + [jax.ShapeDtypeStruct((8, 128), F32)],
        in_specs=[HBM_SPEC] * (2 * n),
        out_specs=[SEM_SPEC] * (2 * ng) + [HBM_SPEC] * (2 * n) + [pl.BlockSpec(memory_space=pltpu.VMEM)],
        input_output_aliases={i: 2 * ng + i for i in range(2 * n)},
        compiler_params=pltpu.CompilerParams(has_side_effects=DATAFLOW),
    )(*srcs, *lands)
    sems, thru, token = outs[:2 * ng], outs[2 * ng:2 * ng + 2 * n], outs[-1]
    state = []
    for gi, g in enumerate(groups):
        lo, hi = first[gi], first[gi] + len(g)
        state.append((sems[2 * gi], sems[2 * gi + 1], thru[lo:hi], thru[n + lo:n + hi]))
    return state, token


def _gather_wait(state, after, name):
    send_sem, recv_sem, srcs, lands = state
    n = len(srcs)

    def body(*refs):
        src_refs, land_refs = refs[:n], refs[n:2 * n]
        send, recv = refs[2 * n], refs[2 * n + 1]
        for cp in _gather_copies(src_refs, land_refs, send, recv, 0):
            cp.wait_send()
            cp.wait_recv()

    outs = pl.pallas_call(
        body, name=name,
        out_shape=[pltpu.HBM(a.shape, a.dtype) for a in list(srcs) + list(lands)],
        in_specs=[HBM_SPEC] * (2 * n) + [SEM_SPEC, SEM_SPEC, ANY],
        out_specs=[HBM_SPEC] * (2 * n),
        input_output_aliases={i: i for i in range(2 * n)},
        compiler_params=pltpu.CompilerParams(has_side_effects=DATAFLOW),
    )(*srcs, *lands, send_sem, recv_sem, after)
    return outs[n:]


def _add_halves(g, recv, c1):
    nk, r, cd = g.shape
    r2 = r // 2
    rc = r2 // ROW_CHUNKS

    def body(c_ref, g_ref, r_ref, pab_ref):
        pab_ref[...] = (g_ref[...] + r_ref[...]).astype(BF16)

    blk = pl.BlockSpec((None, rc, cd), lambda k, j, c_ref: (k, j, 0))
    return pl.pallas_call(
        body, name="rs_add_halves",
        grid_spec=pltpu.PrefetchScalarGridSpec(
            num_scalar_prefetch=1, grid=(nk, ROW_CHUNKS),
            in_specs=[pl.BlockSpec((None, rc, cd), lambda k, j, c_ref: (k, c_ref[0] * ROW_CHUNKS + j, 0)), blk], out_specs=blk),
        out_shape=jax.ShapeDtypeStruct((nk, r2, cd), BF16),
        compiler_params=_params("parallel", "parallel"),
    )(c1, g, recv)


def _sum_partials(g, recv_sibling, recv_chips, c_me):
    nk, r, cd = g.shape
    r2 = r // 2
    rc = r2 // ROW_CHUNKS

    def body(cm_ref, g_ref, a_ref, r_ref, s_ref):
        own = g_ref[...] + a_ref[...]
        s_ref[...] = ((own + r_ref[0].astype(F32)) + r_ref[1].astype(F32)) + r_ref[2].astype(F32)

    return pl.pallas_call(
        body, name="rs_sum_partials",
        grid_spec=pltpu.PrefetchScalarGridSpec(
            num_scalar_prefetch=1, grid=(ROW_CHUNKS,),
            in_specs=[pl.BlockSpec((None, rc, cd), lambda j, cm: (cm[1], cm[0] * ROW_CHUNKS + j, 0)),
                      pl.BlockSpec((None, rc, cd), lambda j, cm: (cm[1], j, 0)),
                      pl.BlockSpec((3, rc, cd), lambda j, cm: (0, j, 0))],
            out_specs=pl.BlockSpec((rc, cd), lambda j, cm: (j, 0))),
        out_shape=jax.ShapeDtypeStruct((r2, cd), F32),
        compiler_params=_params("parallel"),
    )(c_me, g, recv_sibling, recv_chips)


def _split_start(name, srcs, lands, ncopies, make_copies):
    srcs = [pltpu.with_memory_space_constraint(s, pltpu.HBM) for s in srcs]
    lands = [pltpu.with_memory_space_constraint(a, pltpu.HBM) for a in lands]
    n, m = len(srcs), len(lands)

    def body(*refs):
        src_refs, land_refs = refs[:n], refs[n:n + m]
        send, recv, token = refs[n + m], refs[n + m + 1], refs[-1]
        for cp in make_copies(src_refs, land_refs, send, recv):
            cp.start()
        token[...] = jnp.zeros(token.shape, token.dtype)

    outs = pl.pallas_call(
        body, name=name,
        out_shape=[pltpu.SemaphoreType.DMA((ncopies,)), pltpu.SemaphoreType.DMA((ncopies,))]
        + [pltpu.HBM(a.shape, a.dtype) for a in srcs + lands] + [jax.ShapeDtypeStruct((8, 128), F32)],
        in_specs=[HBM_SPEC] * (n + m),
        out_specs=[SEM_SPEC, SEM_SPEC] + [HBM_SPEC] * (n + m) + [pl.BlockSpec(memory_space=pltpu.VMEM)],
        input_output_aliases={i: 2 + i for i in range(n + m)},
        compiler_params=pltpu.CompilerParams(has_side_effects=DATAFLOW),
    )(*srcs, *lands)
    return (outs[0], outs[1], outs[2:2 + n], outs[2 + n:2 + n + m], make_copies), outs[-1]


def _split_wait(name, state, after):
    send_sem, recv_sem, srcs, lands, make_copies = state
    n, m = len(srcs), len(lands)

    def body(*refs):
        src_refs, land_refs = refs[:n], refs[n:n + m]
        for cp in make_copies(src_refs, land_refs, refs[n + m], refs[n + m + 1]):
            cp.wait_send()
            cp.wait_recv()

    outs = pl.pallas_call(
        body, name=name,
        out_shape=[pltpu.HBM(a.shape, a.dtype) for a in list(srcs) + list(lands)],
        in_specs=[HBM_SPEC] * (n + m) + [SEM_SPEC, SEM_SPEC, ANY],
        out_specs=[HBM_SPEC] * (n + m),
        input_output_aliases={i: i for i in range(n + m)},
        compiler_params=pltpu.CompilerParams(has_side_effects=DATAFLOW),
    )(*srcs, *lands, send_sem, recv_sem, after)
    return outs[:n], outs[n:]


def _copies_to_sibling(src_of):
    def make(src_refs, land_refs, send, recv):
        x, y, c = _place()
        return [pltpu.make_async_remote_copy(src_of(src_refs[i], c), land_refs[i], send.at[i], recv.at[i],
                                             device_id=(x, y, 1 - c), device_id_type=MESH) for i in range(len(src_refs))]
    return make


def _copies_to_chips(src_refs, land_refs, send, recv):
    x, y, c = _place()
    return [pltpu.make_async_remote_copy(src_refs[i].at[2 * px + py], land_refs[i].at[j], send.at[3 * i + j], recv.at[3 * i + j],
                                         device_id=(px, py, c), device_id_type=MESH)
            for i in range(len(src_refs)) for j, (px, py) in enumerate(_other_chips(x, y))]


def _other_half_rows(ref, c):
    r2 = ref.shape[1] // 2
    return ref.at[:, pl.ds(pl.multiple_of((1 - c) * r2, 8), r2)]


class _ReduceScatter:
    def __init__(self, tag, grads, c1, me1):
        self.tag, self.grads, self.c1, self.me1 = tag, grads, c1, me1

    def start(self):
        lands = [lax.empty((g.shape[0], g.shape[1] // 2, g.shape[2]), F32) for g in self.grads]
        self.state, token = _split_start("rs_%s_a_start" % self.tag, self.grads, lands, len(self.grads),
                                         _copies_to_sibling(_other_half_rows))
        return token

    def to_chips(self, after):
        self.halves = _split_wait("rs_%s_a_wait" % self.tag, self.state, after)
        pabs = [_add_halves(g, r, self.c1) for g, r in zip(*self.halves)]
        lands = [lax.empty((3,) + p.shape[1:], BF16) for p in pabs]
        self.state, token = _split_start("rs_%s_b_start" % self.tag, pabs, lands, 3 * len(pabs), _copies_to_chips)
        return token

    def to_sibling(self, after):
        _, recv = _split_wait("rs_%s_b_wait" % self.tag, self.state, after)
        c_me = jnp.concatenate([self.c1, self.me1])
        sums = [_sum_partials(g, ra, rb, c_me) for g, ra, rb in zip(*self.halves, recv)]
        lands = [lax.empty(s.shape, F32) for s in sums]
        self.state, token = _split_start("rs_%s_c_start" % self.tag, sums, lands, len(sums),
                                         _copies_to_sibling(lambda ref, c: ref))
        return token

    def finish(self, after):
        return list(zip(*_split_wait("rs_%s_c_wait" % self.tag, self.state, after)))


def _add_lists(a_list, b_list):
    n = len(a_list)

    def body(*refs):
        for i in range(n):
            refs[2 * n + i][...] = refs[i][...] + refs[n + i][...]

    vm = pl.BlockSpec(memory_space=pltpu.VMEM)
    return pl.pallas_call(
        body, name="add_lists", in_specs=[vm] * (2 * n), out_specs=[vm] * n,
        out_shape=[jax.ShapeDtypeStruct(a.shape, a.dtype) for a in a_list],
        compiler_params=pltpu.CompilerParams(vmem_limit_bytes=VMEM_LIMIT),
    )(*a_list, *b_list)


def _copies_to_peer(stage):
    def make(src_refs, land_refs, send, recv):
        x, y, c = _place()
        peer = [(x, y, 1 - c), (1 - x, y, c), (x, 1 - y, c)][stage]
        return [pltpu.make_async_remote_copy(src_refs[i], land_refs[i], send.at[i], recv.at[i], device_id=peer, device_id_type=MESH)
                for i in range(len(src_refs))]
    return make


class _AllReduceSmall:
    def __init__(self, vs):
        self.vs, self.stage = list(vs), 0

    def _start(self):
        lands = [lax.empty(v.shape, v.dtype) for v in self.vs]
        self.state, token = _split_start("ar_small_start_%d" % self.stage, self.vs, lands, len(self.vs), _copies_to_peer(self.stage))
        return token

    def start(self):
        return self._start()

    def step(self, after):
        mine, theirs = _split_wait("ar_small_wait_%d" % self.stage, self.state, after)
        self.vs = _add_lists(mine, theirs)
        self.stage += 1
        return self._start() if self.stage < 3 else self.vs[0]


def _adamw_math(w, g, m, v):
    m = ADAM_B1 * m + (1.0 - ADAM_B1) * g
    v = ADAM_B2 * v + (1.0 - ADAM_B2) * jnp.square(g)
    m_hat = m / (1.0 - ADAM_B1 ** ADAM_STEP)
    v_hat = v / (1.0 - ADAM_B2 ** ADAM_STEP)
    return -ADAM_LR * (m_hat / (jnp.sqrt(v_hat) + ADAM_EPS) + ADAM_WD * w), m, v


def _adamw_big_layer(layer, w, m, v, own, sib, c1, prev):
    _, r, cd = w.shape
    rc = r // 2 // ROW_CHUNKS

    def body(c_ref, w_ref, m_ref, v_ref, own_ref, sib_ref, *rest):
        g_ref, d_ref, mo_ref, vo_ref, token = rest[-5:]
        g = jnp.where(pl.program_id(0) == c_ref[0], own_ref[...], sib_ref[...])
        g_ref[...] = g
        d_ref[...], mo_ref[...], vo_ref[...] = _adamw_math(w_ref[...], g, m_ref[...], v_ref[...])
        token[...] = jnp.zeros(token.shape, F32)

    blk = pl.BlockSpec((None, rc, cd), lambda hh, j, c_ref: (layer, hh * ROW_CHUNKS + j, 0))
    half = pl.BlockSpec((rc, cd), lambda hh, j, c_ref: (j, 0))
    prev = () if prev is None else tuple(prev)
    outs = pl.pallas_call(
        body, name="adamw_big",
        grid_spec=pltpu.PrefetchScalarGridSpec(
            num_scalar_prefetch=1, grid=(2, ROW_CHUNKS), in_specs=[blk, blk, blk, half, half] + [ANY] * len(prev),
            out_specs=[blk] * 4 + [pl.BlockSpec((8, 128), lambda hh, j, c_ref: (0, 0))]),
        out_shape=[jax.ShapeDtypeStruct(w.shape, F32)] * 4 + [jax.ShapeDtypeStruct((8, 128), F32)],
        input_output_aliases={6 + i: i for i in range(len(prev))},
        compiler_params=_params("arbitrary", "arbitrary"),
    )(c1, w, m, v, own, sib, *prev)
    return outs[:4], outs[4]


def _adamw_small(ws, gs, ms, vs):
    n = len(ws)

    def body(*refs):
        w_refs, g_refs, m_refs, v_refs = refs[:n], refs[n:2 * n], refs[2 * n:3 * n], refs[3 * n:4 * n]
        outs = refs[4 * n:]
        for i in range(n):
            outs[3 * i][...], outs[3 * i + 1][...], outs[3 * i + 2][...] = _adamw_math(
                w_refs[i][...], g_refs[i][...], m_refs[i][...], v_refs[i][...])

    vm = pl.BlockSpec(memory_space=pltpu.VMEM)
    outs = pl.pallas_call(
        body, name="adamw_small", in_specs=[vm] * (4 * n), out_specs=[vm] * (3 * n),
        out_shape=[jax.ShapeDtypeStruct(w.shape, F32) for w in ws for _ in range(3)],
        compiler_params=pltpu.CompilerParams(vmem_limit_bytes=VMEM_LIMIT),
    )(*ws, *gs, *ms, *vs)
    return [outs[3 * i:3 * i + 3] for i in range(n)]


LANES = 128
SUBLANES = 8
SHARDED_AXIS = {"meta_tokens": 1, "convb_dw_w": 2, "convb_pw_w": 1, "rg_conv_w": 2}


def _rows_of(size):
    return -(-size // (LANES * SUBLANES)) * SUBLANES


def _as_rows(a, rows=None):
    flat = a.reshape(-1)
    rows = _rows_of(flat.size) if rows is None else rows
    return jnp.pad(flat, (0, rows * LANES - flat.size)).reshape(rows, LANES)


class _GradientSchedule:
    GROUPS = {"l1": [(1, "w_down"), (1, "w_up"), (1, "w_out"), (1, "w_in")], "a0": [(0, "w_down"), (0, "w_up")],
              "b0": [(0, "w_out")], "c0": [(0, "w_in")]}
    PLAN = {
        (1, "dw_in"): [("l1", "start")],
        (0, "mlp_bwd"): [("l1", "to_chips")],
        (0, "dw_up"): [("l1", "to_sibling"), ("a0", "start")],
        (0, "dw_out"): [("l1", "finish"), ("a0", "to_chips"), ("b0", "start")],
        (0, "mixer_bwd"): [("a0", "to_sibling"), ("b0", "to_chips"), ("small", "step")],
        (0, "dw_in"): [("c0", "start"), ("c0", "to_chips"), ("small", "step"), ("a0", "finish"), ("b0", "to_sibling")],
    }

    def __init__(self, w, mom, var, c1, me1):
        self.w, self.mom, self.var, self.c1, self.me1 = w, mom, var, c1, me1
        self.grads, self.chains, self.out = {}, {}, {}

    def grad(self, layer, name, g):
        self.grads[layer, name] = g

    def small(self, gsmall):
        self.small_sum = _AllReduceSmall([g.reshape(1, -1) if g.ndim == 1 else g for g in (gsmall[k] for k in SMALL)])
        return (self.small_sum.start(),)

    def point(self, layer, kernel_name, after):
        return self.run(self.PLAN.get((layer, kernel_name), ()), after) or (after,)

    def run(self, actions, after):
        deps = []
        for tag, stage in actions:
            if tag == "small":
                deps.append(self.small_sum.step(after))
            elif stage == "start":
                self.chains[tag] = _ReduceScatter(tag, [self.grads[lk] for lk in self.GROUPS[tag]], self.c1, self.me1)
                deps.append(self.chains[tag].start())
            elif stage == "finish":
                for (layer, k), (own, sib) in zip(self.GROUPS[tag], self.chains[tag].finish(after)):
                    self.out[k], token = _adamw_big_layer(layer, self.w[k], self.mom[k], self.var[k], own, sib, self.c1,
                                                          self.out.get(k))
                    deps.append(token)
            else:
                deps.append(getattr(self.chains[tag], stage)(after))
            after = deps[-1]
        self.last = after
        return tuple(deps)


def _from_shard_major(name, sm):
    if name == "meta_tokens":
        return sm.transpose(1, 0, 2).reshape(N_META, -1)
    if name == "convb_pw_w":
        return sm.transpose(1, 0, 2, 3).reshape(2, -1, D_CONV)
    return sm.transpose(1, 2, 0, 3).reshape(sm.shape[1], sm.shape[2], -1)


def kernel(x, meta_tokens, mix_norm_g, w_in, pool_w, pool_scale, convb_dw_w, convb_dw_b, convb_ln_g, convb_ln_b, convb_pw_w, rg_conv_w, rg_conv_b, rg_w_a, rg_b_a, rg_w_x, rg_b_x, rg_lambda, w_out, mlp_norm_g, w_up, w_down, final_norm_g, loss_target, m_meta_tokens, m_mix_norm_g, m_w_in, m_pool_w, m_pool_scale, m_convb_dw_w, m_convb_dw_b, m_convb_ln_g, m_convb_ln_b, m_convb_pw_w, m_rg_conv_w, m_rg_conv_b, m_rg_w_a, m_rg_b_a, m_rg_w_x, m_rg_b_x, m_rg_lambda, m_w_out, m_mlp_norm_g, m_w_up, m_w_down, m_final_norm_g, v_meta_tokens, v_mix_norm_g, v_w_in, v_pool_w, v_pool_scale, v_convb_dw_w, v_convb_dw_b, v_convb_ln_g, v_convb_ln_b, v_convb_pw_w, v_rg_conv_w, v_rg_conv_b, v_rg_w_a, v_rg_b_a, v_rg_w_x, v_rg_b_x, v_rg_lambda, v_w_out, v_mlp_norm_g, v_w_up, v_w_down, v_final_norm_g):
    given = dict(locals())
    w = {k: given[k] for k in WEIGHTS}
    mom = {k: given["m_" + k] for k in WEIGHTS}
    var = {k: given["v_" + k] for k in WEIGHTS}
    xi, yi, ci = _place()
    me1 = (2 * xi + yi).astype(jnp.int32).reshape(1)
    c1 = ci.astype(jnp.int32).reshape(1)

    small_rows = [_rows_of(w[k].size) for k in SMALL_SHARDED]
    small_pack = jnp.concatenate([_as_rows(w[k]) for k in SMALL_SHARDED])
    transposed = lambda d: {**d, "w_in": d["w_in"].transpose(0, 2, 1)}
    wt, momt, vart = transposed(w), transposed(mom), transposed(var)
    order = [[(0, "w_in"), "small"], [(0, "w_out"), (0, "w_up")], [(0, "w_down")], [(1, "w_in")], [(1, "w_out"), (1, "w_up")],
             [(1, "w_down")]]
    state, token = _gather_start([[wt["w_in"][0].astype(BF16), small_pack]], "gather_start_0")
    shard = lambda l, k: (wt[k][l] + token[0, 0]).astype(BF16)
    rest, token_rest = _gather_start([[shard(*lk) for lk in g] for g in order[1:]], "gather_start_1")
    state = state + rest
    landed = {}

    def fetch(l, k, after):
        gi = [i for i, g in enumerate(order) if (l, k) in g][0]
        if gi not in landed:
            landed[gi] = _gather_wait(state[gi], after, "gather_wait_%d" % gi)
        raw = landed[gi][order[gi].index((l, k))]
        if k == "w_in":
            return raw.reshape(D_IN, D_MODEL)
        return raw.reshape(D_MODEL, D_MODEL) if k == "w_out" else raw

    seq = x.shape[1]
    t_real = N_META + seq
    t_pad = -(-t_real // ROW_ALIGN) * ROW_ALIGN
    landed[0] = _gather_wait(state[0], token_rest, "gather_wait_0")
    wfull = {k: (w[k] + token[0, 0] if k in ("pool_w", "rg_w_a", "rg_w_x") else w[k]) for k in WEIGHTS}
    off = 0
    for k, rows in zip(SMALL_SHARDED, small_rows):
        sm = landed[0][1][:, off:off + rows].reshape(N_CHIPS, -1)[:, :w[k].size].reshape((N_CHIPS,) + w[k].shape)
        wfull[k] = _from_shard_major(k, sm)
        off += rows
    sched = _GradientSchedule(wt, momt, vart, c1, me1)
    loss, dh = _local_step(x[0], loss_target[0], t_real, t_pad, wfull, fetch, sched)
    grad_x = dh[None]

    names = SMALL
    two_d = lambda a: a.reshape(1, -1) if a.ndim == 1 else a
    sched.run([("b0", "finish"), ("c0", "to_sibling"), ("small", "step")], sched.last)
    summed = dict(zip(names, sched.small_sum.vs))
    for k in SMALL_SHARDED:
        ax = SHARDED_AXIS[k]
        summed[k] = lax.dynamic_slice_in_dim(summed[k], me1[0] * w[k].shape[ax], w[k].shape[ax], axis=ax)

    out = {}
    res = _adamw_small([two_d(w[k]) for k in names], [summed[k] for k in names], [two_d(mom[k]) for k in names],
                       [two_d(var[k]) for k in names])
    for k, (d, m2, v2) in zip(names, res):
        out[k] = tuple(o.reshape(w[k].shape) for o in (summed[k], d, m2, v2))
    sched.run([("c0", "finish")], res[0][0])
    out.update(sched.out)
    out["w_in"] = tuple(o.transpose(0, 2, 1) for o in out["w_in"])

    loss = lax.psum(loss, ("x", "y", "c"))
    return (loss, grad_x, *[out[k][0] for k in WEIGHTS], *[out[k][1] for k in WEIGHTS],
            *[out[k][2] for k in WEIGHTS], *[out[k][3] for k in WEIGHTS])
```
